```python
import math
import jax, jax.numpy as jnp
from jax import lax
import numpy as np


D_MODEL = 1024
BATCH = 8
SEQ = 4096
DEPTH = 2

EPS = 1e-6
CONV_WIDTH = 4
N_BRANCHES = 3
GMLP_WIDTH = D_MODEL
GMLP_GROUPS = 8
GMLP_GROUP_DIM = GMLP_WIDTH // GMLP_GROUPS
GMLP_CHUNK = 128
LRU_WIDTH = D_MODEL
LRU_HEADS = 8
LRU_HEAD_DIM = LRU_WIDTH // LRU_HEADS
LRU_C = 8.0
SSD_WIDTH = D_MODEL
SSD_HEAD_DIM = 64
SSD_HEADS = SSD_WIDTH // SSD_HEAD_DIM
SSD_GROUPS = 4
SSD_HEADS_PER_GROUP = SSD_HEADS // SSD_GROUPS
SSD_STATE = 128
SSD_CHUNK = 128
SSD_CONV_DIM = SSD_WIDTH + 2 * SSD_GROUPS * SSD_STATE
MLP_HIDDEN = 4 * D_MODEL
GMLP_IN = 2 * GMLP_WIDTH
LRU_IN = 2 * LRU_WIDTH
SSD_IN = SSD_WIDTH + SSD_CONV_DIM + SSD_HEADS
GATE_IN = N_BRANCHES * D_MODEL
D_IN = GMLP_IN + LRU_IN + SSD_IN + GATE_IN
SPLIT_POINTS = (GMLP_IN,
                GMLP_IN + LRU_IN,
                GMLP_IN + LRU_IN + SSD_WIDTH,
                GMLP_IN + LRU_IN + SSD_WIDTH + SSD_CONV_DIM,
                GMLP_IN + LRU_IN + SSD_IN)

kernel_name = 'hybrid_gmlp_rglru_ssd_gated_merge'


def rms_norm(x, g):
    x32 = x.astype(jnp.float32)
    y = x32 * lax.rsqrt(jnp.mean(x32 * x32, axis=-1, keepdims=True) + EPS)
    return (y * g.astype(jnp.float32)).astype(x.dtype)


def layer_norm(x, g, b):
    x32 = x.astype(jnp.float32)
    mu = jnp.mean(x32, axis=-1, keepdims=True)
    xc = x32 - mu
    y = xc * lax.rsqrt(jnp.mean(xc * xc, axis=-1, keepdims=True) + EPS)
    return (y * g.astype(jnp.float32) + b.astype(jnp.float32)).astype(x.dtype)


def causal_dwconv(x, w, b):
    k, c = w.shape
    y = lax.conv_general_dilated(x, w[:, None, :], window_strides=(1,), padding=[(k - 1, 0)],
                                 dimension_numbers=('NWC', 'WIO', 'NWC'), feature_group_count=c)
    return y + b


def gmlp_mixer(za, ln_g, ln_b, w_s, b_s):
    bsz, seq, _ = za.shape
    u, v = jnp.split(jax.nn.gelu(za), 2, axis=-1)
    v = layer_norm(v, ln_g, ln_b)
    nc = seq // GMLP_CHUNK
    vc = v.reshape(bsz, nc, GMLP_CHUNK, GMLP_GROUPS, GMLP_GROUP_DIM)
    causal = jnp.tril(jnp.ones((GMLP_CHUNK, GMLP_CHUNK), dtype=bool))
    w = jnp.where(causal, w_s, 0)
    mixed = jnp.einsum('gts,bcsgd->bctgd', w, vc) + b_s.T[:, :, None]
    return u * mixed.reshape(bsz, seq, GMLP_WIDTH)


def rg_lru_mixer(zb, conv_w, conv_b, w_r, b_r, w_i, b_i, lam):
    bsz, seq, _ = zb.shape
    xb, gate = jnp.split(zb, 2, axis=-1)
    xb = causal_dwconv(xb, conv_w, conv_b)
    xh = xb.reshape(bsz, seq, LRU_HEADS, LRU_HEAD_DIM)
    r = jax.nn.sigmoid(jnp.einsum('bshi,hij->bshj', xh, w_r).reshape(bsz, seq, LRU_WIDTH) + b_r)
    i = jax.nn.sigmoid(jnp.einsum('bshi,hij->bshj', xh, w_i).reshape(bsz, seq, LRU_WIDTH) + b_i)
    log_a = -LRU_C * r.astype(jnp.float32) * jax.nn.softplus(-lam.astype(jnp.float32))
    a = jnp.exp(log_a)
    inp = jnp.sqrt(-jnp.expm1(2.0 * log_a)) * (i * xb).astype(jnp.float32)

    def combine(c1, c2):
        a1, b1 = c1
        a2, b2 = c2
        return a1 * a2, a2 * b1 + b2

    _, h = lax.associative_scan(combine, (a, inp), axis=1)
    return jax.nn.gelu(gate) * h.astype(gate.dtype)


def segsum(x):
    t = x.shape[-1]
    cs = jnp.cumsum(x, axis=-1)
    seg = cs[..., :, None] - cs[..., None, :]
    return jnp.where(jnp.tril(jnp.ones((t, t), dtype=bool)), seg, -jnp.inf)


def ssd_mixer(z, xbc, dt_raw, conv_w, conv_b, dt_bias, a_log, d_skip, norm_g):
    bsz, seq, _ = z.shape
    nc = seq // SSD_CHUNK
    g, r, q = SSD_GROUPS, SSD_HEADS_PER_GROUP, SSD_CHUNK
    xbc = jax.nn.silu(causal_dwconv(xbc, conv_w, conv_b))
    xs, bm, cm = jnp.split(xbc, [SSD_WIDTH, SSD_WIDTH + g * SSD_STATE], axis=-1)
    dt = jax.nn.softplus((dt_raw + dt_bias).astype(jnp.float32))
    a = -jnp.exp(a_log.astype(jnp.float32))
    x32 = xs.astype(jnp.float32).reshape(bsz, nc, q, g, r, SSD_HEAD_DIM)
    xdt = x32 * dt.reshape(bsz, nc, q, g, r)[..., None]
    bc = bm.astype(jnp.float32).reshape(bsz, nc, q, g, SSD_STATE)
    cc = cm.astype(jnp.float32).reshape(bsz, nc, q, g, SSD_STATE)
    adt = (dt * a).reshape(bsz, nc, q, g, r).transpose(0, 3, 4, 1, 2)
    a_cs = jnp.cumsum(adt, axis=-1)
    decay = jnp.exp(segsum(adt))
    cb = jnp.einsum('bclgn,bcsgn->bgcls', cc, bc)
    y_diag = jnp.einsum('bgcls,bgrcls,bcsgrp->bclgrp', cb, decay, xdt)
    decay_states = jnp.exp(a_cs[..., -1:] - a_cs)
    states = jnp.einsum('bcsgn,bgrcs,bcsgrp->cbgrpn', bc, decay_states, xdt)
    chunk_decay = jnp.exp(a_cs[..., -1]).transpose(3, 0, 1, 2)

    def step(h, inp):
        dec, st = inp
        return h * dec[..., None, None] + st, h

    _, prev = lax.scan(step, jnp.zeros_like(states[0]), (chunk_decay, states))
    y_off = jnp.einsum('bclgn,cbgrpn,bgrcl->bclgrp', cc, prev, jnp.exp(a_cs))
    y = y_diag + y_off + x32 * d_skip.astype(jnp.float32).reshape(g, r)[:, :, None]
    y = y.reshape(bsz, seq, SSD_WIDTH) * jax.nn.silu(z.astype(jnp.float32))
    yg = y.reshape(bsz, seq, g, SSD_WIDTH // g)
    yg = yg * lax.rsqrt(jnp.mean(yg * yg, axis=-1, keepdims=True) + EPS)
    y = yg.reshape(bsz, seq, SSD_WIDTH) * norm_g.astype(jnp.float32)
    return y.astype(z.dtype)


def _normal(k, shape, scale):
    return scale * jax.random.normal(k, shape, jnp.float32)


def _fwd_setup_inputs(seed: int = 0) -> dict:
    key = jax.random.key(seed)
    ks = jax.random.split(key, 32)
    L = DEPTH
    a0 = jax.random.uniform(ks[15], (L, LRU_WIDTH), jnp.float32, minval=0.9, maxval=0.999)
    dt0 = jnp.exp(jax.random.uniform(ks[18], (L, SSD_HEADS), jnp.float32,
                                     minval=math.log(1e-3), maxval=math.log(1e-1)))
    return {
        'x': _normal(ks[0], (BATCH, SEQ, D_MODEL), 1.0),
        'norm_mix_g': 1.0 + _normal(ks[1], (L, D_MODEL), 0.1),
        'w_in': _normal(ks[2], (L, D_MODEL, D_IN), D_MODEL ** -0.5),
        'b_gate': _normal(ks[3], (L, N_BRANCHES, D_MODEL), 0.1),
        'gmlp_ln_g': 1.0 + _normal(ks[4], (L, GMLP_WIDTH), 0.1),
        'gmlp_ln_b': _normal(ks[5], (L, GMLP_WIDTH), 0.1),
        'gmlp_w_s': _normal(ks[6], (L, GMLP_GROUPS, GMLP_CHUNK, GMLP_CHUNK), GMLP_CHUNK ** -0.5),
        'gmlp_b_s': 1.0 + _normal(ks[7], (L, GMLP_GROUPS, GMLP_CHUNK), 0.1),
        'lru_conv_w': _normal(ks[8], (L, CONV_WIDTH, LRU_WIDTH), CONV_WIDTH ** -0.5),
        'lru_conv_b': _normal(ks[9], (L, LRU_WIDTH), 0.1),
        'lru_w_r': _normal(ks[10], (L, LRU_HEADS, LRU_HEAD_DIM, LRU_HEAD_DIM), LRU_HEAD_DIM ** -0.5),
        'lru_b_r': _normal(ks[11], (L, LRU_WIDTH), 0.1),
        'lru_w_i': _normal(ks[12], (L, LRU_HEADS, LRU_HEAD_DIM, LRU_HEAD_DIM), LRU_HEAD_DIM ** -0.5),
        'lru_b_i': _normal(ks[13], (L, LRU_WIDTH), 0.1),
        'lru_lambda': jnp.log(a0) - jnp.log1p(-a0),
        'ssd_conv_w': _normal(ks[16], (L, CONV_WIDTH, SSD_CONV_DIM), CONV_WIDTH ** -0.5),
        'ssd_conv_b': _normal(ks[17], (L, SSD_CONV_DIM), 0.1),
        'ssd_dt_bias': dt0 + jnp.log(-jnp.expm1(-dt0)),
        'ssd_a_log': jnp.log(jax.random.uniform(ks[19], (L, SSD_HEADS), jnp.float32, minval=1.0, maxval=16.0)),
        'ssd_d': 1.0 + _normal(ks[20], (L, SSD_HEADS), 0.1),
        'ssd_norm_g': 1.0 + _normal(ks[21], (L, SSD_WIDTH), 0.1),
        'w_branch_a': _normal(ks[22], (L, GMLP_WIDTH, D_MODEL), GMLP_WIDTH ** -0.5),
        'w_branch_b': _normal(ks[23], (L, LRU_WIDTH, D_MODEL), LRU_WIDTH ** -0.5),
        'w_branch_c': _normal(ks[24], (L, SSD_WIDTH, D_MODEL), SSD_WIDTH ** -0.5),
        'w_out': _normal(ks[25], (L, D_MODEL, D_MODEL), D_MODEL ** -0.5),
        'norm_mlp_g': 1.0 + _normal(ks[26], (L, D_MODEL), 0.1),
        'w_mlp_up': _normal(ks[27], (L, D_MODEL, MLP_HIDDEN), D_MODEL ** -0.5),
        'w_mlp_down': _normal(ks[28], (L, MLP_HIDDEN, D_MODEL), MLP_HIDDEN ** -0.5),
        'final_norm_g': 1.0 + _normal(ks[29], (D_MODEL,), 0.1),
    }


def _fwd_reference(x, norm_mix_g, w_in, b_gate, gmlp_ln_g, gmlp_ln_b, gmlp_w_s, gmlp_b_s,
              lru_conv_w, lru_conv_b, lru_w_r, lru_b_r, lru_w_i, lru_b_i, lru_lambda,
              ssd_conv_w, ssd_conv_b, ssd_dt_bias, ssd_a_log, ssd_d, ssd_norm_g,
              w_branch_a, w_branch_b, w_branch_c, w_out, norm_mlp_g, w_mlp_up, w_mlp_down,
              final_norm_g):
    bsz, seq, _ = x.shape
    h = x
    for l in range(DEPTH):
        hn = rms_norm(h, norm_mix_g[l])
        proj = jnp.einsum('bsd,de->bse', hn, w_in[l])
        za, zb, zc, xbc, dt_raw, g_raw = jnp.split(proj, SPLIT_POINTS, axis=-1)
        ya = gmlp_mixer(za, gmlp_ln_g[l], gmlp_ln_b[l], gmlp_w_s[l], gmlp_b_s[l])
        yb = rg_lru_mixer(zb, lru_conv_w[l], lru_conv_b[l], lru_w_r[l], lru_b_r[l],
                          lru_w_i[l], lru_b_i[l], lru_lambda[l])
        yc = ssd_mixer(zc, xbc, dt_raw, ssd_conv_w[l], ssd_conv_b[l], ssd_dt_bias[l],
                       ssd_a_log[l], ssd_d[l], ssd_norm_g[l])
        gates = jax.nn.sigmoid(g_raw.reshape(bsz, seq, N_BRANCHES, D_MODEL) + b_gate[l])
        merged = (gates[:, :, 0] * jnp.einsum('bse,ed->bsd', ya, w_branch_a[l])
                  + gates[:, :, 1] * jnp.einsum('bse,ed->bsd', yb, w_branch_b[l])
                  + gates[:, :, 2] * jnp.einsum('bse,ed->bsd', yc, w_branch_c[l]))
        h = h + jnp.einsum('bsd,de->bse', merged, w_out[l])
        hn = rms_norm(h, norm_mlp_g[l])
        up = jax.nn.relu(jnp.einsum('bsd,df->bsf', hn, w_mlp_up[l]))
        h = h + jnp.einsum('bsf,fd->bsd', up * up, w_mlp_down[l])
    return rms_norm(h, final_norm_g)


import jax as _jax
import jax.numpy as _jnp

TWIN_FORMAT = 'train_step'
FWD_PARAMS = ['x', 'norm_mix_g', 'w_in', 'b_gate', 'gmlp_ln_g', 'gmlp_ln_b', 'gmlp_w_s', 'gmlp_b_s', 'lru_conv_w', 'lru_conv_b', 'lru_w_r', 'lru_b_r', 'lru_w_i', 'lru_b_i', 'lru_lambda', 'ssd_conv_w', 'ssd_conv_b', 'ssd_dt_bias', 'ssd_a_log', 'ssd_d', 'ssd_norm_g', 'w_branch_a', 'w_branch_b', 'w_branch_c', 'w_out', 'norm_mlp_g', 'w_mlp_up', 'w_mlp_down', 'final_norm_g']
TWIN_WEIGHTS = ['norm_mix_g', 'w_in', 'b_gate', 'gmlp_ln_g', 'gmlp_ln_b', 'gmlp_w_s', 'gmlp_b_s', 'lru_conv_w', 'lru_conv_b', 'lru_w_r', 'lru_b_r', 'lru_w_i', 'lru_b_i', 'lru_lambda', 'ssd_conv_w', 'ssd_conv_b', 'ssd_dt_bias', 'ssd_a_log', 'ssd_d', 'ssd_norm_g', 'w_branch_a', 'w_branch_b', 'w_branch_c', 'w_out', 'norm_mlp_g', 'w_mlp_up', 'w_mlp_down', 'final_norm_g']
TWIN_DIFF_INPUT = 'x'
TWIN_INPUTS = ['x', 'norm_mix_g', 'w_in', 'b_gate', 'gmlp_ln_g', 'gmlp_ln_b', 'gmlp_w_s', 'gmlp_b_s', 'lru_conv_w', 'lru_conv_b', 'lru_w_r', 'lru_b_r', 'lru_w_i', 'lru_b_i', 'lru_lambda', 'ssd_conv_w', 'ssd_conv_b', 'ssd_dt_bias', 'ssd_a_log', 'ssd_d', 'ssd_norm_g', 'w_branch_a', 'w_branch_b', 'w_branch_c', 'w_out', 'norm_mlp_g', 'w_mlp_up', 'w_mlp_down', 'final_norm_g', 'loss_target', 'm_norm_mix_g', 'm_w_in', 'm_b_gate', 'm_gmlp_ln_g', 'm_gmlp_ln_b', 'm_gmlp_w_s', 'm_gmlp_b_s', 'm_lru_conv_w', 'm_lru_conv_b', 'm_lru_w_r', 'm_lru_b_r', 'm_lru_w_i', 'm_lru_b_i', 'm_lru_lambda', 'm_ssd_conv_w', 'm_ssd_conv_b', 'm_ssd_dt_bias', 'm_ssd_a_log', 'm_ssd_d', 'm_ssd_norm_g', 'm_w_branch_a', 'm_w_branch_b', 'm_w_branch_c', 'm_w_out', 'm_norm_mlp_g', 'm_w_mlp_up', 'm_w_mlp_down', 'm_final_norm_g', 'v_norm_mix_g', 'v_w_in', 'v_b_gate', 'v_gmlp_ln_g', 'v_gmlp_ln_b', 'v_gmlp_w_s', 'v_gmlp_b_s', 'v_lru_conv_w', 'v_lru_conv_b', 'v_lru_w_r', 'v_lru_b_r', 'v_lru_w_i', 'v_lru_b_i', 'v_lru_lambda', 'v_ssd_conv_w', 'v_ssd_conv_b', 'v_ssd_dt_bias', 'v_ssd_a_log', 'v_ssd_d', 'v_ssd_norm_g', 'v_w_branch_a', 'v_w_branch_b', 'v_w_branch_c', 'v_w_out', 'v_norm_mlp_g', 'v_w_mlp_up', 'v_w_mlp_down', 'v_final_norm_g']
TWIN_OUTPUTS = ['loss', 'grad_x', 'grad_norm_mix_g', 'grad_w_in', 'grad_b_gate', 'grad_gmlp_ln_g', 'grad_gmlp_ln_b', 'grad_gmlp_w_s', 'grad_gmlp_b_s', 'grad_lru_conv_w', 'grad_lru_conv_b', 'grad_lru_w_r', 'grad_lru_b_r', 'grad_lru_w_i', 'grad_lru_b_i', 'grad_lru_lambda', 'grad_ssd_conv_w', 'grad_ssd_conv_b', 'grad_ssd_dt_bias', 'grad_ssd_a_log', 'grad_ssd_d', 'grad_ssd_norm_g', 'grad_w_branch_a', 'grad_w_branch_b', 'grad_w_branch_c', 'grad_w_out', 'grad_norm_mlp_g', 'grad_w_mlp_up', 'grad_w_mlp_down', 'grad_final_norm_g', 'delta_norm_mix_g', 'delta_w_in', 'delta_b_gate', 'delta_gmlp_ln_g', 'delta_gmlp_ln_b', 'delta_gmlp_w_s', 'delta_gmlp_b_s', 'delta_lru_conv_w', 'delta_lru_conv_b', 'delta_lru_w_r', 'delta_lru_b_r', 'delta_lru_w_i', 'delta_lru_b_i', 'delta_lru_lambda', 'delta_ssd_conv_w', 'delta_ssd_conv_b', 'delta_ssd_dt_bias', 'delta_ssd_a_log', 'delta_ssd_d', 'delta_ssd_norm_g', 'delta_w_branch_a', 'delta_w_branch_b', 'delta_w_branch_c', 'delta_w_out', 'delta_norm_mlp_g', 'delta_w_mlp_up', 'delta_w_mlp_down', 'delta_final_norm_g', 'new_m_norm_mix_g', 'new_m_w_in', 'new_m_b_gate', 'new_m_gmlp_ln_g', 'new_m_gmlp_ln_b', 'new_m_gmlp_w_s', 'new_m_gmlp_b_s', 'new_m_lru_conv_w', 'new_m_lru_conv_b', 'new_m_lru_w_r', 'new_m_lru_b_r', 'new_m_lru_w_i', 'new_m_lru_b_i', 'new_m_lru_lambda', 'new_m_ssd_conv_w', 'new_m_ssd_conv_b', 'new_m_ssd_dt_bias', 'new_m_ssd_a_log', 'new_m_ssd_d', 'new_m_ssd_norm_g', 'new_m_w_branch_a', 'new_m_w_branch_b', 'new_m_w_branch_c', 'new_m_w_out', 'new_m_norm_mlp_g', 'new_m_w_mlp_up', 'new_m_w_mlp_down', 'new_m_final_norm_g', 'new_v_norm_mix_g', 'new_v_w_in', 'new_v_b_gate', 'new_v_gmlp_ln_g', 'new_v_gmlp_ln_b', 'new_v_gmlp_w_s', 'new_v_gmlp_b_s', 'new_v_lru_conv_w', 'new_v_lru_conv_b', 'new_v_lru_w_r', 'new_v_lru_b_r', 'new_v_lru_w_i', 'new_v_lru_b_i', 'new_v_lru_lambda', 'new_v_ssd_conv_w', 'new_v_ssd_conv_b', 'new_v_ssd_dt_bias', 'new_v_ssd_a_log', 'new_v_ssd_d', 'new_v_ssd_norm_g', 'new_v_w_branch_a', 'new_v_w_branch_b', 'new_v_w_branch_c', 'new_v_w_out', 'new_v_norm_mlp_g', 'new_v_w_mlp_up', 'new_v_w_mlp_down', 'new_v_final_norm_g']
TWIN_LEAF_KINDS = {'loss': 'loss', 'grad_x': 'grad_x', 'grad_norm_mix_g': 'grad_w', 'grad_w_in': 'grad_w', 'grad_b_gate': 'grad_w', 'grad_gmlp_ln_g': 'grad_w', 'grad_gmlp_ln_b': 'grad_w', 'grad_gmlp_w_s': 'grad_w', 'grad_gmlp_b_s': 'grad_w', 'grad_lru_conv_w': 'grad_w', 'grad_lru_conv_b': 'grad_w', 'grad_lru_w_r': 'grad_w', 'grad_lru_b_r': 'grad_w', 'grad_lru_w_i': 'grad_w', 'grad_lru_b_i': 'grad_w', 'grad_lru_lambda': 'grad_w', 'grad_ssd_conv_w': 'grad_w', 'grad_ssd_conv_b': 'grad_w', 'grad_ssd_dt_bias': 'grad_w', 'grad_ssd_a_log': 'grad_w', 'grad_ssd_d': 'grad_w', 'grad_ssd_norm_g': 'grad_w', 'grad_w_branch_a': 'grad_w', 'grad_w_branch_b': 'grad_w', 'grad_w_branch_c': 'grad_w', 'grad_w_out': 'grad_w', 'grad_norm_mlp_g': 'grad_w', 'grad_w_mlp_up': 'grad_w', 'grad_w_mlp_down': 'grad_w', 'grad_final_norm_g': 'grad_w', 'delta_norm_mix_g': 'delta_w', 'delta_w_in': 'delta_w', 'delta_b_gate': 'delta_w', 'delta_gmlp_ln_g': 'delta_w', 'delta_gmlp_ln_b': 'delta_w', 'delta_gmlp_w_s': 'delta_w', 'delta_gmlp_b_s': 'delta_w', 'delta_lru_conv_w': 'delta_w', 'delta_lru_conv_b': 'delta_w', 'delta_lru_w_r': 'delta_w', 'delta_lru_b_r': 'delta_w', 'delta_lru_w_i': 'delta_w', 'delta_lru_b_i': 'delta_w', 'delta_lru_lambda': 'delta_w', 'delta_ssd_conv_w': 'delta_w', 'delta_ssd_conv_b': 'delta_w', 'delta_ssd_dt_bias': 'delta_w', 'delta_ssd_a_log': 'delta_w', 'delta_ssd_d': 'delta_w', 'delta_ssd_norm_g': 'delta_w', 'delta_w_branch_a': 'delta_w', 'delta_w_branch_b': 'delta_w', 'delta_w_branch_c': 'delta_w', 'delta_w_out': 'delta_w', 'delta_norm_mlp_g': 'delta_w', 'delta_w_mlp_up': 'delta_w', 'delta_w_mlp_down': 'delta_w', 'delta_final_norm_g': 'delta_w', 'new_m_norm_mix_g': 'new_m', 'new_m_w_in': 'new_m', 'new_m_b_gate': 'new_m', 'new_m_gmlp_ln_g': 'new_m', 'new_m_gmlp_ln_b': 'new_m', 'new_m_gmlp_w_s': 'new_m', 'new_m_gmlp_b_s': 'new_m', 'new_m_lru_conv_w': 'new_m', 'new_m_lru_conv_b': 'new_m', 'new_m_lru_w_r': 'new_m', 'new_m_lru_b_r': 'new_m', 'new_m_lru_w_i': 'new_m', 'new_m_lru_b_i': 'new_m', 'new_m_lru_lambda': 'new_m', 'new_m_ssd_conv_w': 'new_m', 'new_m_ssd_conv_b': 'new_m', 'new_m_ssd_dt_bias': 'new_m', 'new_m_ssd_a_log': 'new_m', 'new_m_ssd_d': 'new_m', 'new_m_ssd_norm_g': 'new_m', 'new_m_w_branch_a': 'new_m', 'new_m_w_branch_b': 'new_m', 'new_m_w_branch_c': 'new_m', 'new_m_w_out': 'new_m', 'new_m_norm_mlp_g': 'new_m', 'new_m_w_mlp_up': 'new_m', 'new_m_w_mlp_down': 'new_m', 'new_m_final_norm_g': 'new_m', 'new_v_norm_mix_g': 'new_v', 'new_v_w_in': 'new_v', 'new_v_b_gate': 'new_v', 'new_v_gmlp_ln_g': 'new_v', 'new_v_gmlp_ln_b': 'new_v', 'new_v_gmlp_w_s': 'new_v', 'new_v_gmlp_b_s': 'new_v', 'new_v_lru_conv_w': 'new_v', 'new_v_lru_conv_b': 'new_v', 'new_v_lru_w_r': 'new_v', 'new_v_lru_b_r': 'new_v', 'new_v_lru_w_i': 'new_v', 'new_v_lru_b_i': 'new_v', 'new_v_lru_lambda': 'new_v', 'new_v_ssd_conv_w': 'new_v', 'new_v_ssd_conv_b': 'new_v', 'new_v_ssd_dt_bias': 'new_v', 'new_v_ssd_a_log': 'new_v', 'new_v_ssd_d': 'new_v', 'new_v_ssd_norm_g': 'new_v', 'new_v_w_branch_a': 'new_v', 'new_v_w_branch_b': 'new_v', 'new_v_w_branch_c': 'new_v', 'new_v_w_out': 'new_v', 'new_v_norm_mlp_g': 'new_v', 'new_v_w_mlp_up': 'new_v', 'new_v_w_mlp_down': 'new_v', 'new_v_final_norm_g': 'new_v'}


def _forward(args):
    return _fwd_reference(*[args[k] for k in FWD_PARAMS])


def _output_shape():
    def fwd():
        inp = _fwd_setup_inputs(0)
        return _fwd_reference(*[inp[k] for k in FWD_PARAMS])
    out = _jax.eval_shape(fwd)
    return out.shape, out.dtype

N_MICROBATCH = 1
ADAM_LR = 0.001
ADAM_B1 = 0.9
ADAM_B2 = 0.999
ADAM_EPS = 1e-08
ADAM_WD = 0.01
ADAM_STEP = 10
PER_EXAMPLE_BATCH_AXIS = {'x': 0, 'loss_target': 0}
SHARED_INPUTS = []
_WEIGHT_DTYPES = {'norm_mix_g': _jnp.float32, 'w_in': _jnp.float32, 'b_gate': _jnp.float32, 'gmlp_ln_g': _jnp.float32, 'gmlp_ln_b': _jnp.float32, 'gmlp_w_s': _jnp.float32, 'gmlp_b_s': _jnp.float32, 'lru_conv_w': _jnp.float32, 'lru_conv_b': _jnp.float32, 'lru_w_r': _jnp.float32, 'lru_b_r': _jnp.float32, 'lru_w_i': _jnp.float32, 'lru_b_i': _jnp.float32, 'lru_lambda': _jnp.float32, 'ssd_conv_w': _jnp.float32, 'ssd_conv_b': _jnp.float32, 'ssd_dt_bias': _jnp.float32, 'ssd_a_log': _jnp.float32, 'ssd_d': _jnp.float32, 'ssd_norm_g': _jnp.float32, 'w_branch_a': _jnp.float32, 'w_branch_b': _jnp.float32, 'w_branch_c': _jnp.float32, 'w_out': _jnp.float32, 'norm_mlp_g': _jnp.float32, 'w_mlp_up': _jnp.float32, 'w_mlp_down': _jnp.float32, 'final_norm_g': _jnp.float32}
MOMENT_SCALE = {'norm_mix_g': 1.892168e-01, 'w_in': 5.948306e-02, 'b_gate': 5.264100e-02, 'gmlp_ln_g': 3.802951e-02, 'gmlp_ln_b': 3.600904e-02, 'gmlp_w_s': 3.674970e-02, 'gmlp_b_s': 5.304133e-02, 'lru_conv_w': 1.251887e-01, 'lru_conv_b': 4.545115e-01, 'lru_w_r': 1.135331e-02, 'lru_b_r': 1.723257e-02, 'lru_w_i': 2.275748e-02, 'lru_b_i': 3.720856e-02, 'lru_lambda': 4.033716e-02, 'ssd_conv_w': 7.688788e-02, 'ssd_conv_b': 1.750742e-01, 'ssd_dt_bias': 3.768606e-01, 'ssd_a_log': 2.491857e-01, 'ssd_d': 4.677955e-01, 'ssd_norm_g': 1.202414e-01, 'w_branch_a': 1.683604e-01, 'w_branch_b': 9.128787e-02, 'w_branch_c': 1.285125e-01, 'w_out': 2.258159e-01, 'norm_mlp_g': 2.141479e-01, 'w_mlp_up': 1.102678e-01, 'w_mlp_down': 5.444567e-01, 'final_norm_g': 3.264641e+01}


def _to_microbatches(a, axis):
    t = _jnp.moveaxis(a, axis, 0)
    t = t.reshape((N_MICROBATCH, t.shape[0] // N_MICROBATCH) + t.shape[1:])
    return _jnp.moveaxis(t, 1, axis + 1)


def setup_inputs(seed: int = 0) -> dict:
    inp = _fwd_setup_inputs(seed)
    key = _jax.random.fold_in(_jax.random.key(seed), 7919)
    shape, _ = _output_shape()
    out = dict(inp)
    out["loss_target"] = _jax.random.normal(_jax.random.fold_in(key, 0), shape, _jnp.float32)
    for i, name in enumerate(TWIN_WEIGHTS):
        w = inp[name].astype(_jnp.float32)
        if MOMENT_SCALE is None:
            s = _jnp.sqrt(_jnp.mean(_jnp.square(w)) + 1e-30)
        else:
            s = MOMENT_SCALE[name]
        km, kv = _jax.random.split(_jax.random.fold_in(key, i + 1))
        out[name] = w
        out["m_" + name] = s * _jax.random.normal(km, w.shape, _jnp.float32)
        out["v_" + name] = (s * s) * _jax.random.uniform(kv, w.shape, _jnp.float32, 0.5, 1.5)
    if N_MICROBATCH > 1:
        for name, axis in PER_EXAMPLE_BATCH_AXIS.items():
            out[name] = _to_microbatches(out[name], axis)
    return {'x': out['x'], 'norm_mix_g': out['norm_mix_g'], 'w_in': out['w_in'], 'b_gate': out['b_gate'], 'gmlp_ln_g': out['gmlp_ln_g'], 'gmlp_ln_b': out['gmlp_ln_b'], 'gmlp_w_s': out['gmlp_w_s'], 'gmlp_b_s': out['gmlp_b_s'], 'lru_conv_w': out['lru_conv_w'], 'lru_conv_b': out['lru_conv_b'], 'lru_w_r': out['lru_w_r'], 'lru_b_r': out['lru_b_r'], 'lru_w_i': out['lru_w_i'], 'lru_b_i': out['lru_b_i'], 'lru_lambda': out['lru_lambda'], 'ssd_conv_w': out['ssd_conv_w'], 'ssd_conv_b': out['ssd_conv_b'], 'ssd_dt_bias': out['ssd_dt_bias'], 'ssd_a_log': out['ssd_a_log'], 'ssd_d': out['ssd_d'], 'ssd_norm_g': out['ssd_norm_g'], 'w_branch_a': out['w_branch_a'], 'w_branch_b': out['w_branch_b'], 'w_branch_c': out['w_branch_c'], 'w_out': out['w_out'], 'norm_mlp_g': out['norm_mlp_g'], 'w_mlp_up': out['w_mlp_up'], 'w_mlp_down': out['w_mlp_down'], 'final_norm_g': out['final_norm_g'], 'loss_target': out['loss_target'], 'm_norm_mix_g': out['m_norm_mix_g'], 'm_w_in': out['m_w_in'], 'm_b_gate': out['m_b_gate'], 'm_gmlp_ln_g': out['m_gmlp_ln_g'], 'm_gmlp_ln_b': out['m_gmlp_ln_b'], 'm_gmlp_w_s': out['m_gmlp_w_s'], 'm_gmlp_b_s': out['m_gmlp_b_s'], 'm_lru_conv_w': out['m_lru_conv_w'], 'm_lru_conv_b': out['m_lru_conv_b'], 'm_lru_w_r': out['m_lru_w_r'], 'm_lru_b_r': out['m_lru_b_r'], 'm_lru_w_i': out['m_lru_w_i'], 'm_lru_b_i': out['m_lru_b_i'], 'm_lru_lambda': out['m_lru_lambda'], 'm_ssd_conv_w': out['m_ssd_conv_w'], 'm_ssd_conv_b': out['m_ssd_conv_b'], 'm_ssd_dt_bias': out['m_ssd_dt_bias'], 'm_ssd_a_log': out['m_ssd_a_log'], 'm_ssd_d': out['m_ssd_d'], 'm_ssd_norm_g': out['m_ssd_norm_g'], 'm_w_branch_a': out['m_w_branch_a'], 'm_w_branch_b': out['m_w_branch_b'], 'm_w_branch_c': out['m_w_branch_c'], 'm_w_out': out['m_w_out'], 'm_norm_mlp_g': out['m_norm_mlp_g'], 'm_w_mlp_up': out['m_w_mlp_up'], 'm_w_mlp_down': out['m_w_mlp_down'], 'm_final_norm_g': out['m_final_norm_g'], 'v_norm_mix_g': out['v_norm_mix_g'], 'v_w_in': out['v_w_in'], 'v_b_gate': out['v_b_gate'], 'v_gmlp_ln_g': out['v_gmlp_ln_g'], 'v_gmlp_ln_b': out['v_gmlp_ln_b'], 'v_gmlp_w_s': out['v_gmlp_w_s'], 'v_gmlp_b_s': out['v_gmlp_b_s'], 'v_lru_conv_w': out['v_lru_conv_w'], 'v_lru_conv_b': out['v_lru_conv_b'], 'v_lru_w_r': out['v_lru_w_r'], 'v_lru_b_r': out['v_lru_b_r'], 'v_lru_w_i': out['v_lru_w_i'], 'v_lru_b_i': out['v_lru_b_i'], 'v_lru_lambda': out['v_lru_lambda'], 'v_ssd_conv_w': out['v_ssd_conv_w'], 'v_ssd_conv_b': out['v_ssd_conv_b'], 'v_ssd_dt_bias': out['v_ssd_dt_bias'], 'v_ssd_a_log': out['v_ssd_a_log'], 'v_ssd_d': out['v_ssd_d'], 'v_ssd_norm_g': out['v_ssd_norm_g'], 'v_w_branch_a': out['v_w_branch_a'], 'v_w_branch_b': out['v_w_branch_b'], 'v_w_branch_c': out['v_w_branch_c'], 'v_w_out': out['v_w_out'], 'v_norm_mlp_g': out['v_norm_mlp_g'], 'v_w_mlp_up': out['v_w_mlp_up'], 'v_w_mlp_down': out['v_w_mlp_down'], 'v_final_norm_g': out['v_final_norm_g']}


def _loss(weights, diff, rest, loss_target):
    with _jax.named_scope("forward"):
        args = {**rest, TWIN_DIFF_INPUT: diff, **{k: w.astype(_WEIGHT_DTYPES[k]) for k, w in weights.items()}}
        y = _forward(args)
    with _jax.named_scope("loss_head"):
        err = _jnp.square(y.astype(_jnp.float32) - loss_target)
        return 0.5 * _jnp.sum(_jnp.mean(err, axis=-1)) if err.ndim else 0.5 * err


def _adamw(w, g, m, v):
    m = ADAM_B1 * m + (1.0 - ADAM_B1) * g
    v = ADAM_B2 * v + (1.0 - ADAM_B2) * _jnp.square(g)
    m_hat = m / (1.0 - ADAM_B1 ** ADAM_STEP)
    v_hat = v / (1.0 - ADAM_B2 ** ADAM_STEP)
    delta = -ADAM_LR * (m_hat / (_jnp.sqrt(v_hat) + ADAM_EPS) + ADAM_WD * w)
    return delta, m, v


def reference(x, norm_mix_g, w_in, b_gate, gmlp_ln_g, gmlp_ln_b, gmlp_w_s, gmlp_b_s, lru_conv_w, lru_conv_b, lru_w_r, lru_b_r, lru_w_i, lru_b_i, lru_lambda, ssd_conv_w, ssd_conv_b, ssd_dt_bias, ssd_a_log, ssd_d, ssd_norm_g, w_branch_a, w_branch_b, w_branch_c, w_out, norm_mlp_g, w_mlp_up, w_mlp_down, final_norm_g, loss_target, m_norm_mix_g, m_w_in, m_b_gate, m_gmlp_ln_g, m_gmlp_ln_b, m_gmlp_w_s, m_gmlp_b_s, m_lru_conv_w, m_lru_conv_b, m_lru_w_r, m_lru_b_r, m_lru_w_i, m_lru_b_i, m_lru_lambda, m_ssd_conv_w, m_ssd_conv_b, m_ssd_dt_bias, m_ssd_a_log, m_ssd_d, m_ssd_norm_g, m_w_branch_a, m_w_branch_b, m_w_branch_c, m_w_out, m_norm_mlp_g, m_w_mlp_up, m_w_mlp_down, m_final_norm_g, v_norm_mix_g, v_w_in, v_b_gate, v_gmlp_ln_g, v_gmlp_ln_b, v_gmlp_w_s, v_gmlp_b_s, v_lru_conv_w, v_lru_conv_b, v_lru_w_r, v_lru_b_r, v_lru_w_i, v_lru_b_i, v_lru_lambda, v_ssd_conv_w, v_ssd_conv_b, v_ssd_dt_bias, v_ssd_a_log, v_ssd_d, v_ssd_norm_g, v_w_branch_a, v_w_branch_b, v_w_branch_c, v_w_out, v_norm_mlp_g, v_w_mlp_up, v_w_mlp_down, v_final_norm_g):
    given = dict(x=x, norm_mix_g=norm_mix_g, w_in=w_in, b_gate=b_gate, gmlp_ln_g=gmlp_ln_g, gmlp_ln_b=gmlp_ln_b, gmlp_w_s=gmlp_w_s, gmlp_b_s=gmlp_b_s, lru_conv_w=lru_conv_w, lru_conv_b=lru_conv_b, lru_w_r=lru_w_r, lru_b_r=lru_b_r, lru_w_i=lru_w_i, lru_b_i=lru_b_i, lru_lambda=lru_lambda, ssd_conv_w=ssd_conv_w, ssd_conv_b=ssd_conv_b, ssd_dt_bias=ssd_dt_bias, ssd_a_log=ssd_a_log, ssd_d=ssd_d, ssd_norm_g=ssd_norm_g, w_branch_a=w_branch_a, w_branch_b=w_branch_b, w_branch_c=w_branch_c, w_out=w_out, norm_mlp_g=norm_mlp_g, w_mlp_up=w_mlp_up, w_mlp_down=w_mlp_down, final_norm_g=final_norm_g, loss_target=loss_target, m_norm_mix_g=m_norm_mix_g, m_w_in=m_w_in, m_b_gate=m_b_gate, m_gmlp_ln_g=m_gmlp_ln_g, m_gmlp_ln_b=m_gmlp_ln_b, m_gmlp_w_s=m_gmlp_w_s, m_gmlp_b_s=m_gmlp_b_s, m_lru_conv_w=m_lru_conv_w, m_lru_conv_b=m_lru_conv_b, m_lru_w_r=m_lru_w_r, m_lru_b_r=m_lru_b_r, m_lru_w_i=m_lru_w_i, m_lru_b_i=m_lru_b_i, m_lru_lambda=m_lru_lambda, m_ssd_conv_w=m_ssd_conv_w, m_ssd_conv_b=m_ssd_conv_b, m_ssd_dt_bias=m_ssd_dt_bias, m_ssd_a_log=m_ssd_a_log, m_ssd_d=m_ssd_d, m_ssd_norm_g=m_ssd_norm_g, m_w_branch_a=m_w_branch_a, m_w_branch_b=m_w_branch_b, m_w_branch_c=m_w_branch_c, m_w_out=m_w_out, m_norm_mlp_g=m_norm_mlp_g, m_w_mlp_up=m_w_mlp_up, m_w_mlp_down=m_w_mlp_down, m_final_norm_g=m_final_norm_g, v_norm_mix_g=v_norm_mix_g, v_w_in=v_w_in, v_b_gate=v_b_gate, v_gmlp_ln_g=v_gmlp_ln_g, v_gmlp_ln_b=v_gmlp_ln_b, v_gmlp_w_s=v_gmlp_w_s, v_gmlp_b_s=v_gmlp_b_s, v_lru_conv_w=v_lru_conv_w, v_lru_conv_b=v_lru_conv_b, v_lru_w_r=v_lru_w_r, v_lru_b_r=v_lru_b_r, v_lru_w_i=v_lru_w_i, v_lru_b_i=v_lru_b_i, v_lru_lambda=v_lru_lambda, v_ssd_conv_w=v_ssd_conv_w, v_ssd_conv_b=v_ssd_conv_b, v_ssd_dt_bias=v_ssd_dt_bias, v_ssd_a_log=v_ssd_a_log, v_ssd_d=v_ssd_d, v_ssd_norm_g=v_ssd_norm_g, v_w_branch_a=v_w_branch_a, v_w_branch_b=v_w_branch_b, v_w_branch_c=v_w_branch_c, v_w_out=v_w_out, v_norm_mlp_g=v_norm_mlp_g, v_w_mlp_up=v_w_mlp_up, v_w_mlp_down=v_w_mlp_down, v_final_norm_g=v_final_norm_g)
    weights = {n: given[n] for n in TWIN_WEIGHTS}
    shared = {n: given[n] for n in SHARED_INPUTS}
    per_example = {n: given[n] for n in ['x']}
    grad_fn = _jax.value_and_grad(_loss, argnums=(0, 1))

    def one_microbatch(ex, loss_target):
        ex = dict(ex)
        diff = ex.pop(TWIN_DIFF_INPUT)
        return grad_fn(weights, diff, {**shared, **ex}, loss_target)

    if N_MICROBATCH == 1:
        loss, (grad_w, grad_x) = one_microbatch(per_example, given["loss_target"])
    else:
        def body(carry, xs):
            loss_sum, grad_sum = carry
            l_k, (gw_k, gx_k) = one_microbatch(xs[0], xs[1])
            with _jax.named_scope("update"):
                return (loss_sum + l_k, _jax.tree.map(_jnp.add, grad_sum, gw_k)), gx_k

        init = (_jnp.zeros((), _jnp.float32), _jax.tree.map(_jnp.zeros_like, weights))
        (loss, grad_w), grad_x = _jax.lax.scan(body, init, (per_example, given["loss_target"]))
    with _jax.named_scope("update"):
        delta_w, new_m, new_v = {}, {}, {}
        for n in TWIN_WEIGHTS:
            delta_w[n], new_m[n], new_v[n] = _adamw(weights[n], grad_w[n], given["m_" + n], given["v_" + n])
    return (loss, grad_x, *[grad_w[n] for n in TWIN_WEIGHTS], *[delta_w[n] for n in TWIN_WEIGHTS],
            *[new_m[n] for n in TWIN_WEIGHTS], *[new_v[n] for n in TWIN_WEIGHTS])
```

```python
import functools

import jax
import jax.numpy as jnp
from jax import lax
from jax.experimental import pallas as pl
from jax.experimental.pallas import tpu as pltpu

F32 = jnp.float32
BF16 = jnp.bfloat16
MESH = pl.DeviceIdType.MESH

D = 1024
DEPTH = 2
EPS = 1e-6
CHUNK = 128
GROUPS_A = 8
HEADS_B = 8
LRU_C = 8.0
HEADS_C = 16
HEAD_DIM_C = 64
GROUPS_C = 4
STATE_C = 128
HIDDEN = 4 * D
DT_PAD = 128
OFF_ZA, W_ZA = 0, 2048
OFF_ZB, W_ZB = 2048, 2048
OFF_XBC, W_XBC = 4096, 2048
OFF_GATE, W_GATE = 6144, 3072
OFF_ZC, W_ZC = 9216, 1024
OFF_DT, W_DT = 10240, DT_PAD
D_IN_PAD = 10368
D_IN = 10256
N_DEV = 8

ADAM_LR = 0.001
ADAM_B1 = 0.9
ADAM_B2 = 0.999
ADAM_EPS = 1e-08
ADAM_WD = 0.01
ADAM_STEP = 10

VMEM_LIMIT = 56 * 1024 * 1024
HALO = 8


def _cparams(*sem):
    return pltpu.CompilerParams(dimension_semantics=sem, vmem_limit_bytes=VMEM_LIMIT)


def _bf(x):
    return x.astype(BF16)


def _dg(a, b, ca, cb):
    return lax.dot_general(a, b, (((ca,), (cb,)), ((), ())), preferred_element_type=F32)


@functools.partial(jax.custom_vjp, nondiff_argnums=(2, 3))
def _mm(a, b, ta, tb):
    return _dg(_bf(a), _bf(b), 0 if ta else 1, 1 if tb else 0)


def _mm_fwd(a, b, ta, tb):
    return _mm(a, b, ta, tb), (a, b)


def _mm_bwd(ta, tb, res, g):
    a, b = res
    ma = 1 if ta else 0
    nb = 0 if tb else 1
    gb, ab, bb = _bf(g), _bf(a), _bf(b)
    da = _dg(bb, gb, nb, 1) if ta else _dg(gb, bb, 1, nb)
    db = _dg(gb, ab, 0, ma) if tb else _dg(ab, gb, ma, 0)
    return da.astype(a.dtype), db.astype(b.dtype)


_mm.defvjp(_mm_fwd, _mm_bwd)


@functools.partial(jax.custom_vjp, nondiff_argnums=(1, 2))
def _cols(x, lo, hi):
    return x[:, lo:hi]


def _cols_fwd(x, lo, hi):
    return x[:, lo:hi], x.shape[1]


def _cols_bwd(lo, hi, width, g):
    parts = []
    if lo:
        parts.append(jnp.zeros((g.shape[0], lo), g.dtype))
    parts.append(g)
    if width - hi:
        parts.append(jnp.zeros((g.shape[0], width - hi), g.dtype))
    return (jnp.concatenate(parts, axis=1) if len(parts) > 1 else g,)


_cols.defvjp(_cols_fwd, _cols_bwd)


@functools.partial(jax.custom_vjp, nondiff_argnums=(1, 2))
def _rows(x, lo, hi):
    return x[lo:hi, :]


def _rows_fwd(x, lo, hi):
    return x[lo:hi, :], x.shape[0]


def _rows_bwd(lo, hi, height, g):
    parts = []
    if lo:
        parts.append(jnp.zeros((lo, g.shape[1]), g.dtype))
    parts.append(g)
    if height - hi:
        parts.append(jnp.zeros((height - hi, g.shape[1]), g.dtype))
    return (jnp.concatenate(parts, axis=0) if len(parts) > 1 else g,)


_rows.defvjp(_rows_fwd, _rows_bwd)


def _col(x, j):
    lane = lax.broadcasted_iota(jnp.int32, x.shape, 1)
    return jnp.sum(jnp.where(lane == j, x, 0.0), axis=1, keepdims=True)


def _row(x, i):
    r = lax.broadcasted_iota(jnp.int32, x.shape, 0)
    return jnp.sum(jnp.where(r == i, x, 0.0), axis=0, keepdims=True)


def _roll_down(x, s):
    return pltpu.roll(x, s, 0)


def _roll_up(x, s):
    return pltpu.roll(x, x.shape[0] - s, 0)


def _row_iota(x):
    return lax.broadcasted_iota(jnp.int32, x.shape, 0)


@functools.partial(jax.custom_vjp, nondiff_argnums=(2,))
def _shift_rows(halo, x, s):
    if s == 0:
        return x
    return _roll_down(jnp.concatenate([halo, x], axis=0), s)[HALO:]


def _shift_rows_fwd(halo, x, s):
    return _shift_rows(halo, x, s), None


def _shift_rows_bwd(s, _, g):
    if s == 0:
        return jnp.zeros((HALO, g.shape[1]), g.dtype), g
    ge = jnp.concatenate([jnp.zeros((HALO, g.shape[1]), g.dtype), g], axis=0)
    de = _roll_up(ge, s)
    return de[:HALO], de[HALO:]


_shift_rows.defvjp(_shift_rows_fwd, _shift_rows_bwd)


def _scan_down(a, b):
    n = a.shape[0]
    row = _row_iota(a)
    s = 1
    while s < n:
        keep = row >= s
        a_sh = jnp.where(keep, _roll_down(a, s), 1.0)
        b_sh = jnp.where(keep, _roll_down(b, s), 0.0)
        b = a * b_sh + b
        a = a * a_sh
        s *= 2
    return a, b


def _scan_up(a, b):
    n = a.shape[0]
    row = _row_iota(a)
    s = 1
    while s < n:
        keep = row < n - s
        a_sh = jnp.where(keep, _roll_up(a, s), 1.0)
        b_sh = jnp.where(keep, _roll_up(b, s), 0.0)
        b = a * b_sh + b
        a = a * a_sh
        s *= 2
    return b


@jax.custom_vjp
def _lin_scan(a, b, h0):
    p, h = _scan_down(a, b)
    return h + p * h0


def _lin_scan_fwd(a, b, h0):
    h = _lin_scan(a, b, h0)
    return h, (a, h0, h)


def _lin_scan_bwd(res, g):
    a, h0, h = res
    n = a.shape[0]
    row = _row_iota(a)
    a_next = jnp.where(row < n - 1, _roll_up(a, 1), 0.0)
    gg = _scan_up(a_next, g)
    h_prev = jnp.where(row >= 1, _roll_down(h, 1), h0)
    return gg * h_prev, gg, _row(a * gg, 0)


_lin_scan.defvjp(_lin_scan_fwd, _lin_scan_bwd)


@jax.custom_vjp
def _cumsum_rows(x):
    n = x.shape[0]
    row = _row_iota(x)
    s = 1
    while s < n:
        x = x + jnp.where(row >= s, _roll_down(x, s), 0.0)
        s *= 2
    return x


def _cumsum_rows_fwd(x):
    return _cumsum_rows(x), None


def _cumsum_rows_bwd(_, g):
    n = g.shape[0]
    row = _row_iota(g)
    s = 1
    while s < n:
        g = g + jnp.where(row < n - s, _roll_up(g, s), 0.0)
        s *= 2
    return (g,)


_cumsum_rows.defvjp(_cumsum_rows_fwd, _cumsum_rows_bwd)


def _sigmoid(x):
    return jax.nn.sigmoid(x)


def _softplus(x):
    return jnp.maximum(x, 0.0) + jnp.log1p(jnp.exp(-jnp.abs(x)))


def _gelu(x):
    return jax.nn.gelu(x, approximate=True)


def _neg_expm1(x):
    series = -x * (1.0 + x * (0.5 + x * (1.0 / 6.0 + x * (1.0 / 24.0))))
    return jnp.where(x > -0.01, series, 1.0 - jnp.exp(x))


def _rms(x, g):
    return x * lax.rsqrt(jnp.mean(x * x, axis=-1, keepdims=True) + EPS) * g


def _f_rmsnorm(carries, halos, xs, params):
    (h,) = xs
    (g,) = params
    return (), (_rms(h, g),)


def _f_rmsnorm_res(carries, halos, xs, params):
    (h,) = xs
    (g,) = params
    return (), (_rms(h, g), h)


def _f_gmlp(carries, halos, xs, params):
    (za,) = xs
    ln_g, ln_b, w_s, b_st = params
    ga = _gelu(za)
    u = _cols(ga, 0, D)
    v = _cols(ga, D, 2 * D)
    vc = v - jnp.mean(v, axis=-1, keepdims=True)
    vn = vc * lax.rsqrt(jnp.mean(vc * vc, axis=-1, keepdims=True) + EPS) * ln_g + ln_b
    q = CHUNK
    causal = lax.broadcasted_iota(jnp.int32, (q, q), 0) >= lax.broadcasted_iota(jnp.int32, (q, q), 1)
    mixed = []
    for g in range(GROUPS_A):
        w = jnp.where(causal, _rows(w_s, g * q, (g + 1) * q), 0.0)
        mixed.append(_mm(w, _cols(vn, g * q, (g + 1) * q), False, False) + _col(b_st, g))
    return (), (u * jnp.concatenate(mixed, axis=1),)


def _conv4(halo, x, w, b):
    y = b + _row(w, 3) * x
    for k in range(3):
        y = y + _row(w, k) * _shift_rows(halo, x, 3 - k)
    return y


def _f_lru(carries, halos, xs, params):
    (h0,) = carries
    (halo,) = halos
    xb_pre, gate = xs
    conv_w, conv_b, w_r, b_r, w_i, b_i, lam = params
    xb = _conv4(halo, xb_pre, conv_w, conv_b)
    hd = D // HEADS_B
    r_parts, i_parts = [], []
    for h in range(HEADS_B):
        xh = _cols(xb, h * hd, (h + 1) * hd)
        r_parts.append(_mm(xh, _rows(w_r, h * hd, (h + 1) * hd), False, False))
        i_parts.append(_mm(xh, _rows(w_i, h * hd, (h + 1) * hd), False, False))
    r = _sigmoid(jnp.concatenate(r_parts, axis=1) + b_r)
    i = _sigmoid(jnp.concatenate(i_parts, axis=1) + b_i)
    log_a = -LRU_C * r * _softplus(-lam)
    a = jnp.exp(log_a)
    inp = jnp.sqrt(_neg_expm1(2.0 * log_a)) * (i * xb)
    h = _lin_scan(a, inp, h0)
    return (_row(h, h.shape[0] - 1),), (_gelu(gate) * h,)


def _f_ssd(carries, halos, xs, params):
    (st,) = carries
    (halo,) = halos
    z, xbc_pre, dt_raw = xs
    conv_w, conv_b, dt_bias, a_log, d_skip, norm_g = params
    t = z.shape[0]
    xc = _conv4(halo, xbc_pre, conv_w, conv_b)
    xbc = xc * _sigmoid(xc)
    x_all = _cols(xbc, 0, D)
    b_all = _cols(xbc, D, D + GROUPS_C * STATE_C)
    c_all = _cols(xbc, D + GROUPS_C * STATE_C, D + 2 * GROUPS_C * STATE_C)
    dt = _softplus(dt_raw + dt_bias)
    adt = dt * (-jnp.exp(a_log))
    acs = _cumsum_rows(adt)
    acs_t = acs.T
    a_last = _row(acs, t - 1)
    lo = lax.broadcasted_iota(jnp.int32, (t, 128), 1) < HEAD_DIM_C
    lo_rows = lax.broadcasted_iota(jnp.int32, (128, STATE_C), 0) < HEAD_DIM_C
    causal = lax.broadcasted_iota(jnp.int32, (t, t), 0) >= lax.broadcasted_iota(jnp.int32, (t, t), 1)
    y_parts, st_parts = [], []
    for g in range(GROUPS_C):
        bg = _cols(b_all, g * STATE_C, (g + 1) * STATE_C)
        cg = _cols(c_all, g * STATE_C, (g + 1) * STATE_C)
        cb = _mm(cg, bg, False, True)
        for pr in range(2):
            pair = 2 * g + pr
            h0, h1 = 2 * pair, 2 * pair + 1
            x2 = _cols(x_all, pair * 128, (pair + 1) * 128)
            ac0, ac1 = _col(acs, h0), _col(acs, h1)
            l0 = jnp.exp(jnp.where(causal, ac0 - _row(acs_t, h0), -1e30))
            l1 = jnp.exp(jnp.where(causal, ac1 - _row(acs_t, h1), -1e30))
            xdt = x2 * jnp.where(lo, _col(dt, h0), _col(dt, h1))
            y_diag = (_mm(cb * l0, jnp.where(lo, xdt, 0.0), False, False)
                      + _mm(cb * l1, jnp.where(lo, 0.0, xdt), False, False))
            al0, al1 = _col(a_last, h0), _col(a_last, h1)
            decay_s = jnp.where(lo, jnp.exp(al0 - ac0), jnp.exp(al1 - ac1))
            s_new = _mm(xdt * decay_s, bg, True, False)
            prev = _rows(st, pair * 128, (pair + 1) * 128)
            y_off = _mm(cg, prev, False, True) * jnp.where(lo, jnp.exp(ac0), jnp.exp(ac1))
            skip = jnp.where(lo, _col(d_skip, h0), _col(d_skip, h1))
            y_parts.append(y_diag + y_off + x2 * skip)
            st_parts.append(prev * jnp.where(lo_rows, jnp.exp(al0), jnp.exp(al1)) + s_new)
    y = jnp.concatenate(y_parts, axis=1) * (z * _sigmoid(z))
    gw = D // GROUPS_C
    yn = []
    for g in range(GROUPS_C):
        yg = _cols(y, g * gw, (g + 1) * gw)
        yn.append(yg * lax.rsqrt(jnp.mean(yg * yg, axis=-1, keepdims=True) + EPS))
    return (jnp.concatenate(st_parts, axis=0),), (jnp.concatenate(yn, axis=1) * norm_g,)


def _f_merge(carries, halos, xs, params):
    pa, pb, pc, g_raw = xs
    (b_gate,) = params
    m = (_sigmoid(_cols(g_raw, 0, D) + _row(b_gate, 0)) * pa
         + _sigmoid(_cols(g_raw, D, 2 * D) + _row(b_gate, 1)) * pb
         + _sigmoid(_cols(g_raw, 2 * D, 3 * D) + _row(b_gate, 2)) * pc)
    return (), (m,)


def _f_loss(carries, halos, xs, params):
    (acc,) = carries
    h, target = xs
    (g,) = params
    err = jnp.square(_rms(h, g) - target)
    part = 0.5 * jnp.sum(jnp.mean(err, axis=-1, keepdims=True), axis=0, keepdims=True)
    return (acc + part,), ()


def _x_specs(xs, t, index_of):
    specs = []
    for arr, off, width in xs:
        assert off % width == 0 and off + width <= arr.shape[1]
        specs.append(pl.BlockSpec((t, width), functools.partial(lambda j, cb: (index_of(j), cb), cb=off // width)))
    return specs


def _halo_specs(xs, halo_idx, t, index_of):
    specs = []
    for xi in halo_idx:
        _, off, width = xs[xi]
        specs.append(pl.BlockSpec(
            (HALO, width),
            functools.partial(lambda j, cb: (jnp.maximum(index_of(j) * (t // HALO) - 1, 0), cb), cb=off // width)))
    return specs


def _full_spec(a):
    return pl.BlockSpec(a.shape, functools.partial(lambda j, nd: (0,) * nd, nd=a.ndim))


def _chunk_fwd(f, name, t, xs, params, outs, halo_idx=(), carry_shapes=(), save_carries=False, final_carries=False):
    s = xs[0][0].shape[0]
    n = s // t
    nx, nh, npar, no, nc = len(xs), len(halo_idx), len(params), len(outs), len(carry_shapes)
    ns = nc if save_carries else 0
    nf = nc if final_carries else 0

    def body(*refs):
        x_refs, refs = refs[:nx], refs[nx:]
        h_refs, refs = refs[:nh], refs[nh:]
        p_refs, refs = refs[:npar], refs[npar:]
        y_refs, refs = refs[:no], refs[no:]
        s_refs, refs = refs[:ns], refs[ns:]
        f_refs, c_refs = refs[:nf], refs[nf:]
        i = pl.program_id(0)

        @pl.when(i == 0)
        def _():
            for c in c_refs:
                c[...] = jnp.zeros_like(c)

        carries = tuple(c[...] for c in c_refs)
        for s_ref, c in zip(s_refs, carries):
            s_ref[0] = c
        halos = tuple(jnp.where(i > 0, h[...].astype(F32), 0.0) for h in h_refs)
        new_c, ys = f(carries, halos, tuple(x[...].astype(F32) for x in x_refs), tuple(p[...] for p in p_refs))
        for y_ref, y in zip(y_refs, ys):
            y_ref[...] = y.astype(y_ref.dtype)
        for c, v in zip(c_refs, new_c):
            c[...] = v
        for f_ref, v in zip(f_refs, new_c):
            f_ref[...] = v

    ident = lambda j: j
    out_shape = [jax.ShapeDtypeStruct((s, w), dt) for w, dt in outs]
    out_specs = [pl.BlockSpec((t, w), lambda j: (j, 0)) for w, _ in outs]
    if save_carries:
        out_shape += [jax.ShapeDtypeStruct((n,) + tuple(cs), F32) for cs in carry_shapes]
        out_specs += [pl.BlockSpec((1,) + tuple(cs), lambda j: (j, 0, 0)) for cs in carry_shapes]
    if final_carries:
        out_shape += [jax.ShapeDtypeStruct(tuple(cs), F32) for cs in carry_shapes]
        out_specs += [pl.BlockSpec(tuple(cs), lambda j: (0, 0)) for cs in carry_shapes]
    res = pl.pallas_call(
        body, name=name, grid=(n,),
        in_specs=_x_specs(xs, t, ident) + _halo_specs(xs, halo_idx, t, ident) + [_full_spec(p) for p in params],
        out_specs=out_specs, out_shape=out_shape,
        scratch_shapes=[pltpu.VMEM(tuple(cs), F32) for cs in carry_shapes],
        compiler_params=_cparams("arbitrary"),
    )(*[x[0] for x in xs], *[xs[xi][0] for xi in halo_idx], *params)
    return res[:no], res[no:no + ns], res[no + ns:]


def _chunk_bwd(f, name, t, xs, params, dys, dx_dtypes, halo_idx=(), saved=(), carry_seed=None):
    s = xs[0][0].shape[0]
    n = s // t
    nx, nh, npar, nc, ndy = len(xs), len(halo_idx), len(params), len(saved), len(dys)

    def body(*refs):
        x_refs, refs = refs[:nx], refs[nx:]
        h_refs, refs = refs[:nh], refs[nh:]
        p_refs, refs = refs[:npar], refs[npar:]
        s_refs, refs = refs[:nc], refs[nc:]
        dy_refs, refs = refs[:ndy], refs[ndy:]
        dx_refs, refs = refs[:nx], refs[nx:]
        dp_refs, refs = refs[:npar], refs[npar:]
        dc_refs, dh_refs = refs[:nc], refs[nc:]
        j = pl.program_id(0)
        i = n - 1 - j

        @pl.when(j == 0)
        def _():
            for dc in dc_refs:
                dc[...] = jnp.zeros_like(dc) if carry_seed is None else carry_seed(dc.shape)
            for r in dh_refs + dp_refs:
                r[...] = jnp.zeros_like(r)

        carries = tuple(s_ref[0] for s_ref in s_refs)
        halos = tuple(jnp.where(i > 0, h[...].astype(F32), 0.0) for h in h_refs)
        x_vals = tuple(x[...].astype(F32) for x in x_refs)
        p_vals = tuple(p[...] for p in p_refs)
        _, vjp = jax.vjp(f, carries, halos, x_vals, p_vals)
        d_car, d_hal, d_xs, d_par = vjp((tuple(dc[...] for dc in dc_refs), tuple(d[...].astype(F32) for d in dy_refs)))
        d_xs = list(d_xs)
        for k, xi in enumerate(halo_idx):
            w = xs[xi][2]
            d_xs[xi] = d_xs[xi] + jnp.concatenate([jnp.zeros((t - HALO, w), F32), dh_refs[k][...]], axis=0)
            dh_refs[k][...] = jnp.where(i > 0, d_hal[k], 0.0)
        for dx_ref, dx in zip(dx_refs, d_xs):
            dx_ref[...] = dx.astype(dx_ref.dtype)
        for dp_ref, dp in zip(dp_refs, d_par):
            dp_ref[...] += dp
        for dc, v in zip(dc_refs, d_car):
            dc[...] = v

    rev = lambda j: n - 1 - j
    in_specs = (_x_specs(xs, t, rev) + _halo_specs(xs, halo_idx, t, rev) + [_full_spec(p) for p in params]
                + [pl.BlockSpec((1,) + a.shape[1:], lambda j: (n - 1 - j, 0, 0)) for a in saved]
                + [pl.BlockSpec((t, d.shape[1]), lambda j: (n - 1 - j, 0)) for d in dys])
    out_shape = ([jax.ShapeDtypeStruct((s, w), dt) for (_, _, w), dt in zip(xs, dx_dtypes)]
                 + [jax.ShapeDtypeStruct(p.shape, F32) for p in params])
    out_specs = ([pl.BlockSpec((t, w), lambda j: (n - 1 - j, 0)) for _, _, w in xs] + [_full_spec(p) for p in params])
    res = pl.pallas_call(
        body, name=name, grid=(n,), in_specs=in_specs, out_specs=out_specs, out_shape=out_shape,
        scratch_shapes=([pltpu.VMEM(a.shape[1:], F32) for a in saved]
                        + [pltpu.VMEM((HALO, xs[xi][2]), F32) for xi in halo_idx]),
        compiler_params=_cparams("arbitrary"),
    )(*[x[0] for x in xs], *[xs[xi][0] for xi in halo_idx], *params, *saved, *dys)
    return res[:nx], res[nx:]


def _tile(dim, pref):
    for cand in pref:
        if dim % cand == 0:
            return cand
    return dim


def _matmul(a, b, name, ta=False, tb=False, outs=(F32,), epilogue=None, extras=()):
    m, k = (a.shape[1], a.shape[0]) if ta else a.shape
    n = b.shape[0] if tb else b.shape[1]
    tm = _tile(m, (1024, 1152, 512, 256, 128))
    tn = _tile(n, (1152, 1024, 512, 256, 128))
    tk = _tile(k, (1024, 1152, 512, 256, 128))
    nk = k // tk
    ne, no = len(extras), len(outs)
    ca, cb = (0 if ta else 1), (1 if tb else 0)

    def body(*refs):
        a_ref, b_ref = refs[:2]
        e_refs = refs[2:2 + ne]
        o_refs = refs[2 + ne:2 + ne + no]
        acc = refs[-1]
        kk = pl.program_id(2)

        @pl.when(kk == 0)
        def _():
            acc[...] = jnp.zeros_like(acc)

        acc[...] += _dg(_bf(a_ref[...]), _bf(b_ref[...]), ca, cb)

        @pl.when(kk == nk - 1)
        def _():
            res = acc[...]
            vals = (res,) if epilogue is None else epilogue(res, *[e[...] for e in e_refs])
            for o_ref, v in zip(o_refs, vals):
                o_ref[...] = v.astype(o_ref.dtype)

    a_spec = pl.BlockSpec((tk, tm), lambda i, j, kk: (kk, i)) if ta else pl.BlockSpec((tm, tk), lambda i, j, kk: (i, kk))
    b_spec = pl.BlockSpec((tn, tk), lambda i, j, kk: (j, kk)) if tb else pl.BlockSpec((tk, tn), lambda i, j, kk: (kk, j))
    mn_spec = pl.BlockSpec((tm, tn), lambda i, j, kk: (i, j))
    res = pl.pallas_call(
        body, name=name, grid=(m // tm, n // tn, nk),
        in_specs=[a_spec, b_spec] + [mn_spec] * ne,
        out_specs=[mn_spec] * no,
        out_shape=[jax.ShapeDtypeStruct((m, n), dt) for dt in outs],
        scratch_shapes=[pltpu.VMEM((tm, tn), F32)],
        compiler_params=_cparams("parallel", "parallel", "arbitrary"),
    )(a, b, *extras)
    return res if no > 1 else res[0]


def _elementwise_block(r, c):
    if r % 8 == 0 and r >= 8:
        return _tile(r, (256, 128, 64, 32, 16, 8)), c
    return r, _tile(c, (256, 128))


def _adamw_math(g, w, m, v):
    m_new = ADAM_B1 * m + (1.0 - ADAM_B1) * g
    v_new = ADAM_B2 * v + (1.0 - ADAM_B2) * jnp.square(g)
    m_hat = m_new / (1.0 - ADAM_B1 ** ADAM_STEP)
    v_hat = v_new / (1.0 - ADAM_B2 ** ADAM_STEP)
    return -ADAM_LR * (m_hat / (jnp.sqrt(v_hat) + ADAM_EPS) + ADAM_WD * w), m_new, v_new


def _adamw(parts, w, m, v, name):
    nl, r, c = w.shape
    k = parts[0].shape[0]
    tr = _tile(r, (128, 64, 32, 16, 8))
    nb = r // tr

    def body(*refs):
        p_refs, (w_ref, m_ref, v_ref), outs = refs[:nl], refs[nl:nl + 3], refs[nl + 3:]
        layer = pl.program_id(0)
        for q in range(nl):
            @pl.when(layer == q)
            def _(q=q):
                g = p_refs[q][0]
                for j in range(1, k):
                    g = g + p_refs[q][j]
                vals = (g,) + _adamw_math(g, w_ref[0], m_ref[0], v_ref[0])
                for o_ref, val in zip(outs, vals):
                    o_ref[0] = val

    spec = pl.BlockSpec((1, tr, c), lambda l, i: (l, i, 0))
    part_specs = [pl.BlockSpec((k, tr, c), functools.partial(
        lambda l, i, q: (0, jnp.where(l == q, i, jnp.where(l < q, 0, nb - 1)), 0), q=q)) for q in range(nl)]
    return pl.pallas_call(
        body, name=name, grid=(nl, nb), in_specs=part_specs + [spec] * 3,
        out_specs=[spec] * 4, out_shape=[jax.ShapeDtypeStruct((nl, r, c), F32)] * 4,
        compiler_params=_cparams("arbitrary", "arbitrary"),
    )(*parts, w, m, v)


def _adamw_small(gathered, ws, ms, vs, name):
    n = len(ws)

    def body(*refs):
        g_refs, w_refs, m_refs, v_refs = refs[:n], refs[n:2 * n], refs[2 * n:3 * n], refs[3 * n:4 * n]
        outs = refs[4 * n:]
        for i in range(n):
            g = g_refs[i][0]
            for j in range(1, N_DEV):
                g = g + g_refs[i][j]
            vals = (g,) + _adamw_math(g, w_refs[i][...], m_refs[i][...], v_refs[i][...])
            for kind, val in enumerate(vals):
                outs[kind * n + i][...] = val

    res = pl.pallas_call(
        body, name=name, out_shape=[jax.ShapeDtypeStruct(a.shape, F32) for _ in range(4) for a in ws],
        compiler_params=pltpu.CompilerParams(vmem_limit_bytes=VMEM_LIMIT),
    )(*gathered, *ws, *ms, *vs)
    return [res[kind * n:(kind + 1) * n] for kind in range(4)]


def _sum_parts(parts, name):
    k, r, c = parts.shape
    tr, tc = _elementwise_block(r, c)

    def body(p_ref, o_ref):
        g = p_ref[0]
        for j in range(1, k):
            g = g + p_ref[j]
        o_ref[...] = g

    return pl.pallas_call(
        body, name=name, grid=(r // tr, c // tc), in_specs=[pl.BlockSpec((k, tr, tc), lambda i, j: (0, i, j))],
        out_specs=pl.BlockSpec((tr, tc), lambda i, j: (i, j)), out_shape=jax.ShapeDtypeStruct((r, c), F32),
        compiler_params=_cparams("parallel", "parallel"),
    )(parts)


ANY = pl.BlockSpec(memory_space=pl.ANY)


def _place():
    return lax.axis_index("x"), lax.axis_index("y"), lax.axis_index("c")


def _all_gather(blocks, name):
    n = len(blocks)

    def body(*refs):
        x_refs, out_refs = refs[:n], refs[n:2 * n]
        send_sems, recv_sems, local_sems = refs[2 * n:]
        x, y, c = _place()
        me, sibling = (x, y, c), (x, y, 1 - c)
        chips = [(1 - x, y), (x, 1 - y), (1 - x, 1 - y)]

        def slot(a, px, py, pc):
            return out_refs[a].at[4 * px + 2 * py + pc]

        def copy(a, k, blk, to, src=None):
            return pltpu.make_async_remote_copy(
                src_ref=slot(a, *blk) if src is None else src, dst_ref=slot(a, *blk),
                send_sem=send_sems.at[7 * a + k], recv_sem=recv_sems.at[7 * a + k], device_id=to, device_id_type=MESH)

        mine = [pltpu.make_async_copy(x_refs[a], slot(a, *me), local_sems.at[a]) for a in range(n)]
        first = []
        for a in range(n):
            mine[a].start()
            first.append(copy(a, 0, me, sibling, src=x_refs[a]))
            first += [copy(a, 1 + j, me, (*chip, c), src=x_refs[a]) for j, chip in enumerate(chips)]
        for cp in first:
            cp.start()
        passed = []
        for j, chip in enumerate(chips):
            for a in range(n):
                copy(a, 1 + j, (*chip, c), me).wait_recv()
                passed.append(copy(a, 4 + j, (*chip, c), sibling))
                passed[-1].start()
        for a in range(n):
            copy(a, 0, sibling, me).wait_recv()
            for j, chip in enumerate(chips):
                copy(a, 4 + j, (*chip, 1 - c), me).wait_recv()
        for cp in first + passed:
            cp.wait_send()
        for cp in mine:
            cp.wait()

    return pl.pallas_call(
        body, name=name, out_shape=[jax.ShapeDtypeStruct((N_DEV,) + b.shape, b.dtype) for b in blocks],
        in_specs=[ANY] * n, out_specs=[ANY] * n,
        scratch_shapes=[pltpu.SemaphoreType.DMA((7 * n,)), pltpu.SemaphoreType.DMA((7 * n,)),
                        pltpu.SemaphoreType.DMA((n,))],
    )(*blocks)


def _exchange_sibling(gs, name):
    n = len(gs)

    def body(*refs):
        g_refs, out_refs = refs[:n], refs[n:2 * n]
        send_sems, recv_sems = refs[2 * n:]
        x, y, c = _place()
        copies = [pltpu.make_async_remote_copy(
            src_ref=g_refs[a].at[2 * k + 1 - c], dst_ref=out_refs[a].at[k], send_sem=send_sems.at[4 * a + k],
            recv_sem=recv_sems.at[4 * a + k], device_id=(x, y, 1 - c), device_id_type=MESH)
            for a in range(n) for k in range(4)]
        for cp in copies:
            cp.start()
        for cp in copies:
            cp.wait()

    return pl.pallas_call(
        body, name=name, out_shape=[jax.ShapeDtypeStruct((4,) + g.shape[1:], g.dtype) for g in gs],
        in_specs=[ANY] * n, out_specs=[ANY] * n,
        scratch_shapes=[pltpu.SemaphoreType.DMA((4 * n,)), pltpu.SemaphoreType.DMA((4 * n,))],
    )(*gs)


def _add_sibling(g, r1, name):
    _, r, w = g.shape
    tr, tc = _elementwise_block(r, w)
    core = jnp.reshape(lax.axis_index("c"), (1,)).astype(jnp.int32)

    def body(c_ref, g_ref, r_ref, o_ref):
        o_ref[...] = g_ref[...] + r_ref[...]

    return pl.pallas_call(
        body, name=name, out_shape=jax.ShapeDtypeStruct((4, r, w), F32),
        grid_spec=pltpu.PrefetchScalarGridSpec(
            num_scalar_prefetch=1, grid=(4, r // tr, w // tc),
            in_specs=[pl.BlockSpec((1, tr, tc), lambda k, i, j, c_ref: (2 * k + c_ref[0], i, j)),
                      pl.BlockSpec((1, tr, tc), lambda k, i, j, c_ref: (k, i, j))],
            out_specs=pl.BlockSpec((1, tr, tc), lambda k, i, j, c_ref: (k, i, j))),
        compiler_params=_cparams("parallel", "parallel", "parallel"),
    )(core, g, r1)


def _exchange_chips(ps, name):
    n = len(ps)

    def body(*refs):
        p_refs, out_refs = refs[:n], refs[n:2 * n]
        send_sems, recv_sems, local_sems = refs[2 * n:]
        x, y, c = _place()
        mine = 2 * x + y
        chips = [(1 - x, y), (x, 1 - y), (1 - x, 1 - y)]

        def copy(a, j, slab, slot):
            px, py = chips[j]
            return pltpu.make_async_remote_copy(
                src_ref=p_refs[a].at[slab], dst_ref=out_refs[a].at[slot], send_sem=send_sems.at[3 * a + j],
                recv_sem=recv_sems.at[3 * a + j], device_id=(px, py, c), device_id_type=MESH)

        own = [pltpu.make_async_copy(p_refs[a].at[mine], out_refs[a].at[mine], local_sems.at[a]) for a in range(n)]
        sends = [copy(a, j, 2 * chips[j][0] + chips[j][1], mine) for a in range(n) for j in range(3)]
        for cp in own + sends:
            cp.start()
        for a in range(n):
            for j in range(3):
                copy(a, j, mine, 2 * chips[j][0] + chips[j][1]).wait_recv()
        for cp in sends:
            cp.wait_send()
        for cp in own:
            cp.wait()

    return pl.pallas_call(
        body, name=name, out_shape=[jax.ShapeDtypeStruct(p.shape, p.dtype) for p in ps],
        in_specs=[ANY] * n, out_specs=[ANY] * n,
        scratch_shapes=[pltpu.SemaphoreType.DMA((3 * n,)), pltpu.SemaphoreType.DMA((3 * n,)),
                        pltpu.SemaphoreType.DMA((n,))],
    )(*ps)


def _reorder_in_proj(wt):
    za_zb, zc, xbc, dt, gates = (wt[:4096], wt[4096:5120], wt[5120:7168], wt[7168:7184], wt[7184:])
    return jnp.concatenate([za_zb, xbc, gates, zc, dt, jnp.zeros((DT_PAD - 16, wt.shape[1]), wt.dtype)], axis=0)


def _restore_in_proj(wt):
    return jnp.concatenate([wt[:4096], wt[OFF_ZC:OFF_ZC + W_ZC], wt[OFF_XBC:OFF_XBC + W_XBC],
                            wt[OFF_DT:OFF_DT + 16], wt[OFF_GATE:OFF_GATE + W_GATE]], axis=0)


def _lanes_from_devices(g):
    return jnp.moveaxis(g, 0, 1).reshape(g.shape[1], N_DEV * g.shape[2])


def _lanes_to_devices(a):
    return jnp.moveaxis(a.reshape(a.shape[0], N_DEV, a.shape[1] // N_DEV), 1, 0)


def _pad_lanes(a, width):
    return jnp.pad(a, ((0, 0), (0, width - a.shape[1])))


BIG = ("w_in", "w_branch_a", "w_branch_b", "w_branch_c", "w_out", "w_mlp_up", "w_mlp_down")
SMALL_SHARDED = ("b_gate", "lru_conv_w", "ssd_conv_w")
REPLICATED = ("norm_mix_g", "gmlp_ln_g", "gmlp_ln_b", "gmlp_w_s", "gmlp_b_s", "lru_conv_b", "lru_w_r", "lru_b_r",
              "lru_w_i", "lru_b_i", "lru_lambda", "ssd_conv_b", "ssd_dt_bias", "ssd_a_log", "ssd_d", "ssd_norm_g",
              "norm_mlp_g", "final_norm_g")
WEIGHTS = ("norm_mix_g", "w_in", "b_gate", "gmlp_ln_g", "gmlp_ln_b", "gmlp_w_s", "gmlp_b_s", "lru_conv_w", "lru_conv_b",
           "lru_w_r", "lru_b_r", "lru_w_i", "lru_b_i", "lru_lambda", "ssd_conv_w", "ssd_conv_b", "ssd_dt_bias",
           "ssd_a_log", "ssd_d", "ssd_norm_g", "w_branch_a", "w_branch_b", "w_branch_c", "w_out", "norm_mlp_g",
           "w_mlp_up", "w_mlp_down", "final_norm_g")
TRANSPOSED = ("w_in", "w_mlp_up")
SMALL_MATRICES = ("gmlp_w_s", "lru_w_r", "lru_w_i")
SMALL_VECTORS = tuple(n for n in REPLICATED if n not in SMALL_MATRICES)


def _layer_params(full, l):
    row = lambda a: a.reshape(1, -1)
    return dict(
        norm_mix_g=row(full["norm_mix_g"][l]), norm_mlp_g=row(full["norm_mlp_g"][l]),
        gmlp=(row(full["gmlp_ln_g"][l]), row(full["gmlp_ln_b"][l]), full["gmlp_w_s"][l].reshape(GROUPS_A * CHUNK, CHUNK),
              full["gmlp_b_s"][l].T),
        lru=(full["lru_conv_w"][l], row(full["lru_conv_b"][l]), full["lru_w_r"][l].reshape(D, D // HEADS_B),
             row(full["lru_b_r"][l]), full["lru_w_i"][l].reshape(D, D // HEADS_B), row(full["lru_b_i"][l]),
             row(full["lru_lambda"][l])),
        ssd=(full["ssd_conv_w"][l], row(full["ssd_conv_b"][l]), _pad_lanes(row(full["ssd_dt_bias"][l]), DT_PAD),
             _pad_lanes(row(full["ssd_a_log"][l]), DT_PAD), _pad_lanes(row(full["ssd_d"][l]), DT_PAD),
             row(full["ssd_norm_g"][l])),
        b_gate=full["b_gate"][l],
    )


def _forward_layer(h, p, wb, l):
    tag = f"l{l}"
    t_row = 512
    (hn,), _, _ = _chunk_fwd(_f_rmsnorm, f"norm_mix_{tag}", t_row, [(h, 0, D)], [p["norm_mix_g"]], [(D, BF16)])
    proj = _matmul(hn, wb["w_in"], f"in_proj_{tag}", tb=True)
    (ya,), _, _ = _chunk_fwd(_f_gmlp, f"gmlp_{tag}", CHUNK, [(proj, OFF_ZA, W_ZA)], p["gmlp"], [(D, BF16)])
    lru_xs = [(proj, OFF_ZB, D), (proj, OFF_ZB + D, D)]
    (yb,), lru_saved, _ = _chunk_fwd(_f_lru, f"lru_{tag}", CHUNK, lru_xs, p["lru"], [(D, BF16)], halo_idx=(0,),
                                     carry_shapes=[(1, D)], save_carries=True)
    ssd_xs = [(proj, OFF_ZC, W_ZC), (proj, OFF_XBC, W_XBC), (proj, OFF_DT, W_DT)]
    (yc,), ssd_saved, _ = _chunk_fwd(_f_ssd, f"ssd_{tag}", CHUNK, ssd_xs, p["ssd"], [(D, BF16)], halo_idx=(1,),
                                     carry_shapes=[(HEADS_C * HEAD_DIM_C, STATE_C)], save_carries=True)
    pa = _matmul(ya, wb["w_branch_a"], f"branch_a_{tag}")
    pb = _matmul(yb, wb["w_branch_b"], f"branch_b_{tag}")
    pc = _matmul(yc, wb["w_branch_c"], f"branch_c_{tag}")
    merge_xs = [(pa, 0, D), (pb, 0, D), (pc, 0, D), (proj, OFF_GATE, W_GATE)]
    (merged,), _, _ = _chunk_fwd(_f_merge, f"merge_{tag}", t_row, merge_xs, [p["b_gate"]], [(D, BF16)])
    h_mid = _matmul(merged, wb["w_out"], f"out_proj_{tag}", epilogue=lambda acc, res: (acc + res,), extras=(h,))
    (hn2,), _, _ = _chunk_fwd(_f_rmsnorm, f"norm_mlp_{tag}", t_row, [(h_mid, 0, D)], [p["norm_mlp_g"]], [(D, BF16)])

    def relu_sq(acc):
        r = jnp.maximum(acc, 0.0)
        return r, r * r

    relu_up, act = _matmul(hn2, wb["w_mlp_up"], f"mlp_up_{tag}", tb=True, outs=(F32, BF16), epilogue=relu_sq)
    h_out = _matmul(act, wb["w_mlp_down"], f"mlp_down_{tag}", epilogue=lambda acc, res: (acc + res,), extras=(h_mid,))
    saved = dict(h=h, hn=hn, proj=proj, ya=ya, yb=yb, yc=yc, lru_saved=lru_saved, ssd_saved=ssd_saved, pa=pa, pb=pb,
                 pc=pc, merged=merged, h_mid=h_mid, hn2=hn2, relu_up=relu_up, act=act, lru_xs=lru_xs, ssd_xs=ssd_xs,
                 merge_xs=merge_xs)
    return h_out, saved


def _backward_layer(dh, sv, p, wb, l):
    tag = f"l{l}"
    t_row = 512
    g = {}
    d_up = _matmul(dh, wb["w_mlp_down"], f"d_act_{tag}", tb=True, outs=(BF16,),
                   epilogue=lambda acc, r: (acc * (2.0 * r),), extras=(sv["relu_up"],))
    g["w_mlp_down"] = _matmul(sv["act"], dh, f"dw_mlp_down_{tag}", ta=True)
    g["w_mlp_up"] = _matmul(d_up, sv["hn2"], f"dw_mlp_up_{tag}", ta=True)
    d_hn2 = _matmul(d_up, wb["w_mlp_up"], f"d_hn2_{tag}")
    (d_mid,), (g["norm_mlp_g"],) = _chunk_bwd(_f_rmsnorm_res, f"norm_mlp_bwd_{tag}", t_row, [(sv["h_mid"], 0, D)],
                                              [p["norm_mlp_g"]], [d_hn2, dh], [F32])
    d_merged = _matmul(d_mid, wb["w_out"], f"d_merged_{tag}", tb=True)
    g["w_out"] = _matmul(sv["merged"], d_mid, f"dw_out_{tag}", ta=True)
    (d_pa, d_pb, d_pc, d_gate), (g["b_gate"],) = _chunk_bwd(
        _f_merge, f"merge_bwd_{tag}", t_row, sv["merge_xs"], [p["b_gate"]], [d_merged], [BF16] * 4)
    d_y = {}
    for br, d_p, y in (("a", d_pa, sv["ya"]), ("b", d_pb, sv["yb"]), ("c", d_pc, sv["yc"])):
        g[f"w_branch_{br}"] = _matmul(y, d_p, f"dw_branch_{br}_{tag}", ta=True)
        d_y[br] = _matmul(d_p, wb[f"w_branch_{br}"], f"d_y{br}_{tag}", tb=True)
    (d_za,), g_gmlp = _chunk_bwd(_f_gmlp, f"gmlp_bwd_{tag}", CHUNK, [(sv["proj"], OFF_ZA, W_ZA)], p["gmlp"],
                                 [d_y["a"]], [BF16])
    (d_xb, d_gt), g_lru = _chunk_bwd(_f_lru, f"lru_bwd_{tag}", CHUNK, sv["lru_xs"], p["lru"], [d_y["b"]], [BF16] * 2,
                                     halo_idx=(0,), saved=sv["lru_saved"])
    (d_zc, d_xbc, d_dt), g_ssd = _chunk_bwd(_f_ssd, f"ssd_bwd_{tag}", CHUNK, sv["ssd_xs"], p["ssd"], [d_y["c"]],
                                            [BF16] * 3, halo_idx=(1,), saved=sv["ssd_saved"])
    d_proj = jnp.concatenate([d_za, d_xb, d_gt, d_xbc, d_gate, d_zc, d_dt], axis=1)
    g["w_in"] = _matmul(d_proj, sv["hn"], f"dw_in_{tag}", ta=True)
    d_hn = _matmul(d_proj, wb["w_in"], f"d_hn_{tag}")
    (d_h,), (g["norm_mix_g"],) = _chunk_bwd(_f_rmsnorm_res, f"norm_mix_bwd_{tag}", t_row, [(sv["h"], 0, D)],
                                            [p["norm_mix_g"]], [d_hn, d_mid], [F32])
    g["w_in"] = _restore_in_proj(g["w_in"])
    for n in BIG:
        g[n] = g[n].reshape(N_DEV, g[n].shape[0] // N_DEV, g[n].shape[1])
    g["gmlp_ln_g"], g["gmlp_ln_b"], g["gmlp_w_s"] = g_gmlp[:3]
    g["gmlp_b_s"] = g_gmlp[3].T
    (g["lru_conv_w"], g["lru_conv_b"], g["lru_w_r"], g["lru_b_r"], g["lru_w_i"], g["lru_b_i"], g["lru_lambda"]) = g_lru
    g["ssd_conv_w"], g["ssd_conv_b"] = g_ssd[:2]
    g["ssd_dt_bias"], g["ssd_a_log"], g["ssd_d"] = (a[:, :HEADS_C] for a in g_ssd[2:5])
    g["ssd_norm_g"] = g_ssd[5]
    return d_h, g


LOSS_ROWS = 512


def _loss_and_grads(h, target, full, layer_w):
    seq = h.shape[0]
    layer_p = [_layer_params(full, l) for l in range(DEPTH)]
    saved = []
    for l in range(DEPTH):
        h, sv = _forward_layer(h, layer_p[l], layer_w[l], l)
        saved.append(sv)
    final_g = full["final_norm_g"].reshape(1, D)
    loss_xs = [(h, 0, D), (target, 0, D)]
    t_loss = min(LOSS_ROWS, seq)
    _, _, (loss_acc,) = _chunk_fwd(_f_loss, "loss", t_loss, loss_xs, [final_g], [], carry_shapes=[(1, 128)],
                                   final_carries=True)
    zero_acc = jnp.zeros((seq // t_loss, 1, 128), F32)
    seed = lambda shape: (lax.broadcasted_iota(jnp.int32, shape, 1) == 0).astype(F32)
    (dh, _), (g_final,) = _chunk_bwd(_f_loss, "loss_bwd", t_loss, loss_xs, [final_g], [], [F32, F32], saved=[zero_acc],
                                     carry_seed=seed)
    layer_g = [None] * DEPTH
    for l in reversed(range(DEPTH)):
        dh, layer_g[l] = _backward_layer(dh, saved[l], layer_p[l], layer_w[l], l)
    return loss_acc[0, 0], dh, layer_g, g_final


def _small_views(d):
    views = {n: d[n] for n in REPLICATED}
    views["gmlp_b_s"] = d["gmlp_b_s"].reshape(DEPTH * GROUPS_A, CHUNK)
    views["final_norm_g"] = d["final_norm_g"].reshape(1, D)
    for n in SMALL_MATRICES:
        views[n] = d[n].reshape(DEPTH * D, D // HEADS_B)
    return views


def kernel(x, norm_mix_g, w_in, b_gate, gmlp_ln_g, gmlp_ln_b, gmlp_w_s, gmlp_b_s, lru_conv_w, lru_conv_b, lru_w_r, lru_b_r, lru_w_i, lru_b_i, lru_lambda, ssd_conv_w, ssd_conv_b, ssd_dt_bias, ssd_a_log, ssd_d, ssd_norm_g, w_branch_a, w_branch_b, w_branch_c, w_out, norm_mlp_g, w_mlp_up, w_mlp_down, final_norm_g, loss_target, m_norm_mix_g, m_w_in, m_b_gate, m_gmlp_ln_g, m_gmlp_ln_b, m_gmlp_w_s, m_gmlp_b_s, m_lru_conv_w, m_lru_conv_b, m_lru_w_r, m_lru_b_r, m_lru_w_i, m_lru_b_i, m_lru_lambda, m_ssd_conv_w, m_ssd_conv_b, m_ssd_dt_bias, m_ssd_a_log, m_ssd_d, m_ssd_norm_g, m_w_branch_a, m_w_branch_b, m_w_branch_c, m_w_out, m_norm_mlp_g, m_w_mlp_up, m_w_mlp_down, m_final_norm_g, v_norm_mix_g, v_w_in, v_b_gate, v_gmlp_ln_g, v_gmlp_ln_b, v_gmlp_w_s, v_gmlp_b_s, v_lru_conv_w, v_lru_conv_b, v_lru_w_r, v_lru_b_r, v_lru_w_i, v_lru_b_i, v_lru_lambda, v_ssd_conv_w, v_ssd_conv_b, v_ssd_dt_bias, v_ssd_a_log, v_ssd_d, v_ssd_norm_g, v_w_branch_a, v_w_branch_b, v_w_branch_c, v_w_out, v_norm_mlp_g, v_w_mlp_up, v_w_mlp_down, v_final_norm_g):
    args = locals()
    w = {n: args[n] for n in WEIGHTS}
    m = {n: args["m_" + n] for n in WEIGHTS}
    v = {n: args["v_" + n] for n in WEIGHTS}
    seq = x.shape[1]
    h = x.reshape(seq, D)
    target = loss_target.reshape(seq, D)

    def shard_on_wire(n):
        if n in SMALL_SHARDED:
            return w[n]
        return (jnp.swapaxes(w[n], 1, 2) if n in TRANSPOSED else w[n]).astype(BF16)

    sharded = BIG + SMALL_SHARDED
    gathered = dict(zip(sharded, _all_gather([shard_on_wire(n) for n in sharded], "gather_weights")))
    full = {n: w[n] for n in REPLICATED}
    for n in SMALL_SHARDED:
        full[n] = jnp.stack([_lanes_from_devices(gathered[n][:, l]) for l in range(DEPTH)])
    layer_w = []
    for l in range(DEPTH):
        wl = {n: gathered[n][:, l].reshape(-1, D) for n in BIG}
        wl["w_in"] = _reorder_in_proj(wl["w_in"])
        layer_w.append(wl)

    loss_local, dh, layer_g, g_final = _loss_and_grads(h, target, full, layer_w)
    loss = lax.psum(loss_local, ("x", "y", "c"))
    grad_x = dh.reshape(x.shape)
    out = {}
    kinds = ("grad", "delta", "new_m", "new_v")

    slabs = {(n, l): layer_g[l][n] for l in range(DEPTH) for n in BIG}
    for n in SMALL_SHARDED:
        slabs[n, None] = jnp.concatenate([_lanes_to_devices(layer_g[l][n]) for l in range(DEPTH)], axis=1)
    keys = list(slabs)
    from_sibling = _exchange_sibling([slabs[k] for k in keys], "grads_to_sibling")
    chip_sums = [_add_sibling(slabs[k], r, f"add_sibling_{k[0]}_{k[1]}") for k, r in zip(keys, from_sibling)]
    reduced = dict(zip(keys, _exchange_chips(chip_sums, "grads_to_chips")))
    for n in BIG:
        parts = [reduced[n, l] for l in range(DEPTH)]
        if n in TRANSPOSED:
            parts = [_sum_parts(p, f"sum_{n}_l{l}").T[None] for l, p in enumerate(parts)]
        for kind, a in zip(kinds, _adamw(parts, w[n], m[n], v[n], f"adamw_{n}")):
            out[kind, n] = a
    for n in SMALL_SHARDED:
        one = lambda a: a.reshape((1, -1, a.shape[-1]))
        for kind, a in zip(kinds, _adamw([reduced[n, None]], one(w[n]), one(m[n]), one(v[n]), f"adamw_{n}")):
            out[kind, n] = a.reshape(w[n].shape)

    g_small = {n: jnp.concatenate([layer_g[l][n] for l in range(DEPTH)], axis=0) for n in REPLICATED[:-1]}
    g_small["final_norm_g"] = g_final
    g_all = dict(zip(REPLICATED, _all_gather([g_small[n] for n in REPLICATED], "gather_small_grads")))
    wv, mv, vv = _small_views(w), _small_views(m), _small_views(v)
    res = _adamw_small([g_all[n] for n in SMALL_VECTORS], *[[d[n] for n in SMALL_VECTORS] for d in (wv, mv, vv)],
                       "adamw_vectors")
    for kind, arrays in zip(kinds, res):
        for n, a in zip(SMALL_VECTORS, arrays):
            out[kind, n] = a.reshape(w[n].shape)
    for n in SMALL_MATRICES:
        for kind, a in zip(kinds, _adamw([g_all[n]], wv[n][None], mv[n][None], vv[n][None], f"adamw_{n}")):
            out[kind, n] = a.reshape(w[n].shape)

    return (loss, grad_x, *[out[kind, n] for kind in kinds for n in WEIGHTS])
```

```python
import functools

import jax
import jax.numpy as jnp
from jax import lax
from jax.experimental import pallas as pl
from jax.experimental.pallas import tpu as pltpu

F32 = jnp.float32
BF16 = jnp.bfloat16
MESH = pl.DeviceIdType.MESH

D = 1024
DEPTH = 2
EPS = 1e-6
CHUNK = 128
GROUPS_A = 8
HEADS_B = 8
LRU_C = 8.0
HEADS_C = 16
HEAD_DIM_C = 64
GROUPS_C = 4
STATE_C = 128
HIDDEN = 4 * D
DT_PAD = 128
OFF_ZA, W_ZA = 0, 2048
OFF_ZB, W_ZB = 2048, 2048
OFF_XBC, W_XBC = 4096, 2048
OFF_GATE, W_GATE = 6144, 3072
OFF_ZC, W_ZC = 9216, 1024
OFF_DT, W_DT = 10240, DT_PAD
D_IN_PAD = 10368
D_IN = 10256
N_DEV = 8

ADAM_LR = 0.001
ADAM_B1 = 0.9
ADAM_B2 = 0.999
ADAM_EPS = 1e-08
ADAM_WD = 0.01
ADAM_STEP = 10

VMEM_LIMIT = 56 * 1024 * 1024
HALO = 8


def _cparams(*sem):
    return pltpu.CompilerParams(dimension_semantics=sem, vmem_limit_bytes=VMEM_LIMIT)


def _bf(x):
    return x.astype(BF16)


def _dg(a, b, ca, cb):
    return lax.dot_general(a, b, (((ca,), (cb,)), ((), ())), preferred_element_type=F32)


@functools.partial(jax.custom_vjp, nondiff_argnums=(2, 3))
def _mm(a, b, ta, tb):
    return _dg(_bf(a), _bf(b), 0 if ta else 1, 1 if tb else 0)


def _mm_fwd(a, b, ta, tb):
    return _mm(a, b, ta, tb), (a, b)


def _mm_bwd(ta, tb, res, g):
    a, b = res
    ma = 1 if ta else 0
    nb = 0 if tb else 1
    gb, ab, bb = _bf(g), _bf(a), _bf(b)
    da = _dg(bb, gb, nb, 1) if ta else _dg(gb, bb, 1, nb)
    db = _dg(gb, ab, 0, ma) if tb else _dg(ab, gb, ma, 0)
    return da.astype(a.dtype), db.astype(b.dtype)


_mm.defvjp(_mm_fwd, _mm_bwd)


@functools.partial(jax.custom_vjp, nondiff_argnums=(1, 2))
def _cols(x, lo, hi):
    return x[:, lo:hi]


def _cols_fwd(x, lo, hi):
    return x[:, lo:hi], x.shape[1]


def _cols_bwd(lo, hi, width, g):
    parts = []
    if lo:
        parts.append(jnp.zeros((g.shape[0], lo), g.dtype))
    parts.append(g)
    if width - hi:
        parts.append(jnp.zeros((g.shape[0], width - hi), g.dtype))
    return (jnp.concatenate(parts, axis=1) if len(parts) > 1 else g,)


_cols.defvjp(_cols_fwd, _cols_bwd)


@functools.partial(jax.custom_vjp, nondiff_argnums=(1, 2))
def _rows(x, lo, hi):
    return x[lo:hi, :]


def _rows_fwd(x, lo, hi):
    return x[lo:hi, :], x.shape[0]


def _rows_bwd(lo, hi, height, g):
    parts = []
    if lo:
        parts.append(jnp.zeros((lo, g.shape[1]), g.dtype))
    parts.append(g)
    if height - hi:
        parts.append(jnp.zeros((height - hi, g.shape[1]), g.dtype))
    return (jnp.concatenate(parts, axis=0) if len(parts) > 1 else g,)


_rows.defvjp(_rows_fwd, _rows_bwd)


def _col(x, j):
    lane = lax.broadcasted_iota(jnp.int32, x.shape, 1)
    return jnp.sum(jnp.where(lane == j, x, 0.0), axis=1, keepdims=True)


def _row(x, i):
    r = lax.broadcasted_iota(jnp.int32, x.shape, 0)
    return jnp.sum(jnp.where(r == i, x, 0.0), axis=0, keepdims=True)


def _roll_down(x, s):
    return pltpu.roll(x, s, 0)


def _roll_up(x, s):
    return pltpu.roll(x, x.shape[0] - s, 0)


def _row_iota(x):
    return lax.broadcasted_iota(jnp.int32, x.shape, 0)


@functools.partial(jax.custom_vjp, nondiff_argnums=(2,))
def _shift_rows(halo, x, s):
    if s == 0:
        return x
    return _roll_down(jnp.concatenate([halo, x], axis=0), s)[HALO:]


def _shift_rows_fwd(halo, x, s):
    return _shift_rows(halo, x, s), None


def _shift_rows_bwd(s, _, g):
    if s == 0:
        return jnp.zeros((HALO, g.shape[1]), g.dtype), g
    ge = jnp.concatenate([jnp.zeros((HALO, g.shape[1]), g.dtype), g], axis=0)
    de = _roll_up(ge, s)
    return de[:HALO], de[HALO:]


_shift_rows.defvjp(_shift_rows_fwd, _shift_rows_bwd)


def _scan_down(a, b):
    n = a.shape[0]
    row = _row_iota(a)
    s = 1
    while s < n:
        keep = row >= s
        a_sh = jnp.where(keep, _roll_down(a, s), 1.0)
        b_sh = jnp.where(keep, _roll_down(b, s), 0.0)
        b = a * b_sh + b
        a = a * a_sh
        s *= 2
    return a, b


def _scan_up(a, b):
    n = a.shape[0]
    row = _row_iota(a)
    s = 1
    while s < n:
        keep = row < n - s
        a_sh = jnp.where(keep, _roll_up(a, s), 1.0)
        b_sh = jnp.where(keep, _roll_up(b, s), 0.0)
        b = a * b_sh + b
        a = a * a_sh
        s *= 2
    return b


@jax.custom_vjp
def _lin_scan(a, b, h0):
    p, h = _scan_down(a, b)
    return h + p * h0


def _lin_scan_fwd(a, b, h0):
    h = _lin_scan(a, b, h0)
    return h, (a, h0, h)


def _lin_scan_bwd(res, g):
    a, h0, h = res
    n = a.shape[0]
    row = _row_iota(a)
    a_next = jnp.where(row < n - 1, _roll_up(a, 1), 0.0)
    gg = _scan_up(a_next, g)
    h_prev = jnp.where(row >= 1, _roll_down(h, 1), h0)
    return gg * h_prev, gg, _row(a * gg, 0)


_lin_scan.defvjp(_lin_scan_fwd, _lin_scan_bwd)


@jax.custom_vjp
def _cumsum_rows(x):
    n = x.shape[0]
    row = _row_iota(x)
    s = 1
    while s < n:
        x = x + jnp.where(row >= s, _roll_down(x, s), 0.0)
        s *= 2
    return x


def _cumsum_rows_fwd(x):
    return _cumsum_rows(x), None


def _cumsum_rows_bwd(_, g):
    n = g.shape[0]
    row = _row_iota(g)
    s = 1
    while s < n:
        g = g + jnp.where(row < n - s, _roll_up(g, s), 0.0)
        s *= 2
    return (g,)


_cumsum_rows.defvjp(_cumsum_rows_fwd, _cumsum_rows_bwd)


def _sigmoid(x):
    return jax.nn.sigmoid(x)


def _softplus(x):
    return jnp.maximum(x, 0.0) + jnp.log1p(jnp.exp(-jnp.abs(x)))


def _gelu(x):
    return jax.nn.gelu(x, approximate=True)


def _neg_expm1(x):
    series = -x * (1.0 + x * (0.5 + x * (1.0 / 6.0 + x * (1.0 / 24.0))))
    return jnp.where(x > -0.01, series, 1.0 - jnp.exp(x))


def _rms(x, g):
    return x * lax.rsqrt(jnp.mean(x * x, axis=-1, keepdims=True) + EPS) * g


def _f_rmsnorm(carries, halos, xs, params):
    (h,) = xs
    (g,) = params
    return (), (_rms(h, g),)


def _f_rmsnorm_res(carries, halos, xs, params):
    (h,) = xs
    (g,) = params
    return (), (_rms(h, g), h)


def _f_gmlp(carries, halos, xs, params):
    (za,) = xs
    ln_g, ln_b, w_s, b_st = params
    ga = _gelu(za)
    u = _cols(ga, 0, D)
    v = _cols(ga, D, 2 * D)
    vc = v - jnp.mean(v, axis=-1, keepdims=True)
    vn = vc * lax.rsqrt(jnp.mean(vc * vc, axis=-1, keepdims=True) + EPS) * ln_g + ln_b
    q = CHUNK
    causal = lax.broadcasted_iota(jnp.int32, (q, q), 0) >= lax.broadcasted_iota(jnp.int32, (q, q), 1)
    mixed = []
    for g in range(GROUPS_A):
        w = jnp.where(causal, _rows(w_s, g * q, (g + 1) * q), 0.0)
        mixed.append(_mm(w, _cols(vn, g * q, (g + 1) * q), False, False) + _col(b_st, g))
    return (), (u * jnp.concatenate(mixed, axis=1),)


def _conv4(halo, x, w, b):
    y = b + _row(w, 3) * x
    for k in range(3):
        y = y + _row(w, k) * _shift_rows(halo, x, 3 - k)
    return y


def _f_lru(carries, halos, xs, params):
    (h0,) = carries
    (halo,) = halos
    xb_pre, gate = xs
    conv_w, conv_b, w_r, b_r, w_i, b_i, lam = params
    xb = _conv4(halo, xb_pre, conv_w, conv_b)
    hd = D // HEADS_B
    r_parts, i_parts = [], []
    for h in range(HEADS_B):
        xh = _cols(xb, h * hd, (h + 1) * hd)
        r_parts.append(_mm(xh, _rows(w_r, h * hd, (h + 1) * hd), False, False))
        i_parts.append(_mm(xh, _rows(w_i, h * hd, (h + 1) * hd), False, False))
    r = _sigmoid(jnp.concatenate(r_parts, axis=1) + b_r)
    i = _sigmoid(jnp.concatenate(i_parts, axis=1) + b_i)
    log_a = -LRU_C * r * _softplus(-lam)
    a = jnp.exp(log_a)
    inp = jnp.sqrt(_neg_expm1(2.0 * log_a)) * (i * xb)
    h = _lin_scan(a, inp, h0)
    return (_row(h, h.shape[0] - 1),), (_gelu(gate) * h,)


def _f_ssd(carries, halos, xs, params):
    (st,) = carries
    (halo,) = halos
    z, xbc_pre, dt_raw = xs
    conv_w, conv_b, dt_bias, a_log, d_skip, norm_g = params
    t = z.shape[0]
    xc = _conv4(halo, xbc_pre, conv_w, conv_b)
    xbc = xc * _sigmoid(xc)
    x_all = _cols(xbc, 0, D)
    b_all = _cols(xbc, D, D + GROUPS_C * STATE_C)
    c_all = _cols(xbc, D + GROUPS_C * STATE_C, D + 2 * GROUPS_C * STATE_C)
    dt = _softplus(dt_raw + dt_bias)
    adt = dt * (-jnp.exp(a_log))
    acs = _cumsum_rows(adt)
    acs_t = acs.T
    a_last = _row(acs, t - 1)
    lo = lax.broadcasted_iota(jnp.int32, (t, 128), 1) < HEAD_DIM_C
    lo_rows = lax.broadcasted_iota(jnp.int32, (128, STATE_C), 0) < HEAD_DIM_C
    causal = lax.broadcasted_iota(jnp.int32, (t, t), 0) >= lax.broadcasted_iota(jnp.int32, (t, t), 1)
    y_parts, st_parts = [], []
    for g in range(GROUPS_C):
        bg = _cols(b_all, g * STATE_C, (g + 1) * STATE_C)
        cg = _cols(c_all, g * STATE_C, (g + 1) * STATE_C)
        cb = _mm(cg, bg, False, True)
        for pr in range(2):
            pair = 2 * g + pr
            h0, h1 = 2 * pair, 2 * pair + 1
            x2 = _cols(x_all, pair * 128, (pair + 1) * 128)
            ac0, ac1 = _col(acs, h0), _col(acs, h1)
            l0 = jnp.exp(jnp.where(causal, ac0 - _row(acs_t, h0), -1e30))
            l1 = jnp.exp(jnp.where(causal, ac1 - _row(acs_t, h1), -1e30))
            xdt = x2 * jnp.where(lo, _col(dt, h0), _col(dt, h1))
            y_diag = (_mm(cb * l0, jnp.where(lo, xdt, 0.0), False, False)
                      + _mm(cb * l1, jnp.where(lo, 0.0, xdt), False, False))
            al0, al1 = _col(a_last, h0), _col(a_last, h1)
            decay_s = jnp.where(lo, jnp.exp(al0 - ac0), jnp.exp(al1 - ac1))
            s_new = _mm(xdt * decay_s, bg, True, False)
            prev = _rows(st, pair * 128, (pair + 1) * 128)
            y_off = _mm(cg, prev, False, True) * jnp.where(lo, jnp.exp(ac0), jnp.exp(ac1))
            skip = jnp.where(lo, _col(d_skip, h0), _col(d_skip, h1))
            y_parts.append(y_diag + y_off + x2 * skip)
            st_parts.append(prev * jnp.where(lo_rows, jnp.exp(al0), jnp.exp(al1)) + s_new)
    y = jnp.concatenate(y_parts, axis=1) * (z * _sigmoid(z))
    gw = D // GROUPS_C
    yn = []
    for g in range(GROUPS_C):
        yg = _cols(y, g * gw, (g + 1) * gw)
        yn.append(yg * lax.rsqrt(jnp.mean(yg * yg, axis=-1, keepdims=True) + EPS))
    return (jnp.concatenate(st_parts, axis=0),), (jnp.concatenate(yn, axis=1) * norm_g,)


def _f_merge(carries, halos, xs, params):
    pa, pb, pc, g_raw = xs
    (b_gate,) = params
    m = (_sigmoid(_cols(g_raw, 0, D) + _row(b_gate, 0)) * pa
         + _sigmoid(_cols(g_raw, D, 2 * D) + _row(b_gate, 1)) * pb
         + _sigmoid(_cols(g_raw, 2 * D, 3 * D) + _row(b_gate, 2)) * pc)
    return (), (m,)


def _f_loss(carries, halos, xs, params):
    (acc,) = carries
    h, target = xs
    (g,) = params
    err = jnp.square(_rms(h, g) - target)
    part = 0.5 * jnp.sum(jnp.mean(err, axis=-1, keepdims=True), axis=0, keepdims=True)
    return (acc + part,), ()


def _x_specs(xs, t, index_of):
    specs = []
    for arr, off, width in xs:
        assert off % width == 0 and off + width <= arr.shape[1]
        specs.append(pl.BlockSpec((t, width), functools.partial(lambda j, cb: (index_of(j), cb), cb=off // width)))
    return specs


def _halo_specs(xs, halo_idx, t, index_of):
    specs = []
    for xi in halo_idx:
        _, off, width = xs[xi]
        specs.append(pl.BlockSpec(
            (HALO, width),
            functools.partial(lambda j, cb: (jnp.maximum(index_of(j) * (t // HALO) - 1, 0), cb), cb=off // width)))
    return specs


def _full_spec(a):
    return pl.BlockSpec(a.shape, functools.partial(lambda j, nd: (0,) * nd, nd=a.ndim))


def _chunk_fwd(f, name, t, xs, params, outs, halo_idx=(), carry_shapes=(), save_carries=False, final_carries=False):
    s = xs[0][0].shape[0]
    n = s // t
    nx, nh, npar, no, nc = len(xs), len(halo_idx), len(params), len(outs), len(carry_shapes)
    ns = nc if save_carries else 0
    nf = nc if final_carries else 0

    def body(*refs):
        x_refs, refs = refs[:nx], refs[nx:]
        h_refs, refs = refs[:nh], refs[nh:]
        p_refs, refs = refs[:npar], refs[npar:]
        y_refs, refs = refs[:no], refs[no:]
        s_refs, refs = refs[:ns], refs[ns:]
        f_refs, c_refs = refs[:nf], refs[nf:]
        i = pl.program_id(0)

        @pl.when(i == 0)
        def _():
            for c in c_refs:
                c[...] = jnp.zeros_like(c)

        carries = tuple(c[...] for c in c_refs)
        for s_ref, c in zip(s_refs, carries):
            s_ref[0] = c
        halos = tuple(jnp.where(i > 0, h[...].astype(F32), 0.0) for h in h_refs)
        new_c, ys = f(carries, halos, tuple(x[...].astype(F32) for x in x_refs), tuple(p[...] for p in p_refs))
        for y_ref, y in zip(y_refs, ys):
            y_ref[...] = y.astype(y_ref.dtype)
        for c, v in zip(c_refs, new_c):
            c[...] = v
        for f_ref, v in zip(f_refs, new_c):
            f_ref[...] = v

    ident = lambda j: j
    out_shape = [jax.ShapeDtypeStruct((s, w), dt) for w, dt in outs]
    out_specs = [pl.BlockSpec((t, w), lambda j: (j, 0)) for w, _ in outs]
    if save_carries:
        out_shape += [jax.ShapeDtypeStruct((n,) + tuple(cs), F32) for cs in carry_shapes]
        out_specs += [pl.BlockSpec((1,) + tuple(cs), lambda j: (j, 0, 0)) for cs in carry_shapes]
    if final_carries:
        out_shape += [jax.ShapeDtypeStruct(tuple(cs), F32) for cs in carry_shapes]
        out_specs += [pl.BlockSpec(tuple(cs), lambda j: (0, 0)) for cs in carry_shapes]
    res = pl.pallas_call(
        body, name=name, grid=(n,),
        in_specs=_x_specs(xs, t, ident) + _halo_specs(xs, halo_idx, t, ident) + [_full_spec(p) for p in params],
        out_specs=out_specs, out_shape=out_shape,
        scratch_shapes=[pltpu.VMEM(tuple(cs), F32) for cs in carry_shapes],
        compiler_params=_cparams("arbitrary"),
    )(*[x[0] for x in xs], *[xs[xi][0] for xi in halo_idx], *params)
    return res[:no], res[no:no + ns], res[no + ns:]


def _chunk_bwd(f, name, t, xs, params, dys, dx_dtypes, halo_idx=(), saved=(), carry_seed=None):
    s = xs[0][0].shape[0]
    n = s // t
    nx, nh, npar, nc, ndy = len(xs), len(halo_idx), len(params), len(saved), len(dys)

    def body(*refs):
        x_refs, refs = refs[:nx], refs[nx:]
        h_refs, refs = refs[:nh], refs[nh:]
        p_refs, refs = refs[:npar], refs[npar:]
        s_refs, refs = refs[:nc], refs[nc:]
        dy_refs, refs = refs[:ndy], refs[ndy:]
        dx_refs, refs = refs[:nx], refs[nx:]
        dp_refs, refs = refs[:npar], refs[npar:]
        dc_refs, dh_refs = refs[:nc], refs[nc:]
        j = pl.program_id(0)
        i = n - 1 - j

        @pl.when(j == 0)
        def _():
            for dc in dc_refs:
                dc[...] = jnp.zeros_like(dc) if carry_seed is None else carry_seed(dc.shape)
            for r in dh_refs + dp_refs:
                r[...] = jnp.zeros_like(r)

        carries = tuple(s_ref[0] for s_ref in s_refs)
        halos = tuple(jnp.where(i > 0, h[...].astype(F32), 0.0) for h in h_refs)
        x_vals = tuple(x[...].astype(F32) for x in x_refs)
        p_vals = tuple(p[...] for p in p_refs)
        _, vjp = jax.vjp(f, carries, halos, x_vals, p_vals)
        d_car, d_hal, d_xs, d_par = vjp((tuple(dc[...] for dc in dc_refs), tuple(d[...].astype(F32) for d in dy_refs)))
        d_xs = list(d_xs)
        for k, xi in enumerate(halo_idx):
            w = xs[xi][2]
            d_xs[xi] = d_xs[xi] + jnp.concatenate([jnp.zeros((t - HALO, w), F32), dh_refs[k][...]], axis=0)
            dh_refs[k][...] = jnp.where(i > 0, d_hal[k], 0.0)
        for dx_ref, dx in zip(dx_refs, d_xs):
            dx_ref[...] = dx.astype(dx_ref.dtype)
        for dp_ref, dp in zip(dp_refs, d_par):
            dp_ref[...] += dp
        for dc, v in zip(dc_refs, d_car):
            dc[...] = v

    rev = lambda j: n - 1 - j
    in_specs = (_x_specs(xs, t, rev) + _halo_specs(xs, halo_idx, t, rev) + [_full_spec(p) for p in params]
                + [pl.BlockSpec((1,) + a.shape[1:], lambda j: (n - 1 - j, 0, 0)) for a in saved]
                + [pl.BlockSpec((t, d.shape[1]), lambda j: (n - 1 - j, 0)) for d in dys])
    out_shape = ([jax.ShapeDtypeStruct((s, w), dt) for (_, _, w), dt in zip(xs, dx_dtypes)]
                 + [jax.ShapeDtypeStruct(p.shape, F32) for p in params])
    out_specs = ([pl.BlockSpec((t, w), lambda j: (n - 1 - j, 0)) for _, _, w in xs] + [_full_spec(p) for p in params])
    res = pl.pallas_call(
        body, name=name, grid=(n,), in_specs=in_specs, out_specs=out_specs, out_shape=out_shape,
        scratch_shapes=([pltpu.VMEM(a.shape[1:], F32) for a in saved]
                        + [pltpu.VMEM((HALO, xs[xi][2]), F32) for xi in halo_idx]),
        compiler_params=_cparams("arbitrary"),
    )(*[x[0] for x in xs], *[xs[xi][0] for xi in halo_idx], *params, *saved, *dys)
    return res[:nx], res[nx:]


def _tile(dim, pref):
    for cand in pref:
        if dim % cand == 0:
            return cand
    return dim


def _matmul(a, b, name, ta=False, tb=False, outs=(F32,), epilogue=None, extras=()):
    m, k = (a.shape[1], a.shape[0]) if ta else a.shape
    n = b.shape[0] if tb else b.shape[1]
    tm = _tile(m, (1024, 1152, 512, 256, 128))
    tn = _tile(n, (1152, 1024, 512, 256, 128))
    tk = _tile(k, (1024, 1152, 512, 256, 128))
    nk = k // tk
    ne, no = len(extras), len(outs)
    ca, cb = (0 if ta else 1), (1 if tb else 0)

    def body(*refs):
        a_ref, b_ref = refs[:2]
        e_refs = refs[2:2 + ne]
        o_refs = refs[2 + ne:2 + ne + no]
        acc = refs[-1]
        kk = pl.program_id(2)

        @pl.when(kk == 0)
        def _():
            acc[...] = jnp.zeros_like(acc)

        acc[...] += _dg(_bf(a_ref[...]), _bf(b_ref[...]), ca, cb)

        @pl.when(kk == nk - 1)
        def _():
            res = acc[...]
            vals = (res,) if epilogue is None else epilogue(res, *[e[...] for e in e_refs])
            for o_ref, v in zip(o_refs, vals):
                o_ref[...] = v.astype(o_ref.dtype)

    a_spec = pl.BlockSpec((tk, tm), lambda i, j, kk: (kk, i)) if ta else pl.BlockSpec((tm, tk), lambda i, j, kk: (i, kk))
    b_spec = pl.BlockSpec((tn, tk), lambda i, j, kk: (j, kk)) if tb else pl.BlockSpec((tk, tn), lambda i, j, kk: (kk, j))
    mn_spec = pl.BlockSpec((tm, tn), lambda i, j, kk: (i, j))
    res = pl.pallas_call(
        body, name=name, grid=(m // tm, n // tn, nk),
        in_specs=[a_spec, b_spec] + [mn_spec] * ne,
        out_specs=[mn_spec] * no,
        out_shape=[jax.ShapeDtypeStruct((m, n), dt) for dt in outs],
        scratch_shapes=[pltpu.VMEM((tm, tn), F32)],
        compiler_params=_cparams("parallel", "parallel", "arbitrary"),
    )(a, b, *extras)
    return res if no > 1 else res[0]


def _elementwise_block(r, c):
    if r % 8 == 0 and r >= 8:
        return _tile(r, (256, 128, 64, 32, 16, 8)), c
    return r, _tile(c, (256, 128))


def _adamw_math(g, w, m, v):
    m_new = ADAM_B1 * m + (1.0 - ADAM_B1) * g
    v_new = ADAM_B2 * v + (1.0 - ADAM_B2) * jnp.square(g)
    m_hat = m_new / (1.0 - ADAM_B1 ** ADAM_STEP)
    v_hat = v_new / (1.0 - ADAM_B2 ** ADAM_STEP)
    return -ADAM_LR * (m_hat / (jnp.sqrt(v_hat) + ADAM_EPS) + ADAM_WD * w), m_new, v_new


def _adamw(parts, w, m, v, name):
    nl, r, c = w.shape
    k = parts[0].shape[0]
    tr = _tile(r, (128, 64, 32, 16, 8))
    nb = r // tr

    def body(*refs):
        p_refs, (w_ref, m_ref, v_ref), outs = refs[:nl], refs[nl:nl + 3], refs[nl + 3:]
        layer = pl.program_id(0)
        for q in range(nl):
            @pl.when(layer == q)
            def _(q=q):
                g = p_refs[q][0]
                for j in range(1, k):
                    g = g + p_refs[q][j]
                vals = (g,) + _adamw_math(g, w_ref[0], m_ref[0], v_ref[0])
                for o_ref, val in zip(outs, vals):
                    o_ref[0] = val

    spec = pl.BlockSpec((1, tr, c), lambda l, i: (l, i, 0))
    part_specs = [pl.BlockSpec((k, tr, c), functools.partial(
        lambda l, i, q: (0, jnp.where(l == q, i, jnp.where(l < q, 0, nb - 1)), 0), q=q)) for q in range(nl)]
    return pl.pallas_call(
        body, name=name, grid=(nl, nb), in_specs=part_specs + [spec] * 3,
        out_specs=[spec] * 4, out_shape=[jax.ShapeDtypeStruct((nl, r, c), F32)] * 4,
        compiler_params=_cparams("arbitrary", "arbitrary"),
    )(*parts, w, m, v)


def _adamw_small(gathered, ws, ms, vs, name):
    n = len(ws)

    def body(*refs):
        g_refs, w_refs, m_refs, v_refs = refs[:n], refs[n:2 * n], refs[2 * n:3 * n], refs[3 * n:4 * n]
        outs = refs[4 * n:]
        for i in range(n):
            g = g_refs[i][0]
            for j in range(1, N_DEV):
                g = g + g_refs[i][j]
            vals = (g,) + _adamw_math(g, w_refs[i][...], m_refs[i][...], v_refs[i][...])
            for kind, val in enumerate(vals):
                outs[kind * n + i][...] = val

    res = pl.pallas_call(
        body, name=name, out_shape=[jax.ShapeDtypeStruct(a.shape, F32) for _ in range(4) for a in ws],
        compiler_params=pltpu.CompilerParams(vmem_limit_bytes=VMEM_LIMIT),
    )(*gathered, *ws, *ms, *vs)
    return [res[kind * n:(kind + 1) * n] for kind in range(4)]


ANY = pl.BlockSpec(memory_space=pl.ANY)


def _place():
    return lax.axis_index("x"), lax.axis_index("y"), lax.axis_index("c")


def _all_gather(blocks, name):
    n = len(blocks)

    def body(*refs):
        x_refs, out_refs = refs[:n], refs[n:2 * n]
        send_sems, recv_sems, local_sems = refs[2 * n:]
        x, y, c = _place()
        me, sibling = (x, y, c), (x, y, 1 - c)
        chips = [(1 - x, y), (x, 1 - y), (1 - x, 1 - y)]

        def slot(a, px, py, pc):
            return out_refs[a].at[4 * px + 2 * py + pc]

        def copy(a, k, blk, to, src=None):
            return pltpu.make_async_remote_copy(
                src_ref=slot(a, *blk) if src is None else src, dst_ref=slot(a, *blk),
                send_sem=send_sems.at[7 * a + k], recv_sem=recv_sems.at[7 * a + k], device_id=to, device_id_type=MESH)

        mine = [pltpu.make_async_copy(x_refs[a], slot(a, *me), local_sems.at[a]) for a in range(n)]
        first = []
        for a in range(n):
            mine[a].start()
            first.append(copy(a, 0, me, sibling, src=x_refs[a]))
            first += [copy(a, 1 + j, me, (*chip, c), src=x_refs[a]) for j, chip in enumerate(chips)]
        for cp in first:
            cp.start()
        passed = []
        for j, chip in enumerate(chips):
            for a in range(n):
                copy(a, 1 + j, (*chip, c), me).wait_recv()
                passed.append(copy(a, 4 + j, (*chip, c), sibling))
                passed[-1].start()
        for a in range(n):
            copy(a, 0, sibling, me).wait_recv()
            for j, chip in enumerate(chips):
                copy(a, 4 + j, (*chip, 1 - c), me).wait_recv()
        for cp in first + passed:
            cp.wait_send()
        for cp in mine:
            cp.wait()

    return pl.pallas_call(
        body, name=name, out_shape=[jax.ShapeDtypeStruct((N_DEV,) + b.shape, b.dtype) for b in blocks],
        in_specs=[ANY] * n, out_specs=[ANY] * n,
        scratch_shapes=[pltpu.SemaphoreType.DMA((7 * n,)), pltpu.SemaphoreType.DMA((7 * n,)),
                        pltpu.SemaphoreType.DMA((n,))],
    )(*blocks)


def _exchange_sibling(gs, name):
    n = len(gs)

    def body(*refs):
        g_refs, out_refs = refs[:n], refs[n:2 * n]
        send_sems, recv_sems = refs[2 * n:]
        x, y, c = _place()
        copies = [pltpu.make_async_remote_copy(
            src_ref=g_refs[a].at[2 * k + 1 - c], dst_ref=out_refs[a].at[k], send_sem=send_sems.at[4 * a + k],
            recv_sem=recv_sems.at[4 * a + k], device_id=(x, y, 1 - c), device_id_type=MESH)
            for a in range(n) for k in range(4)]
        for cp in copies:
            cp.start()
        for cp in copies:
            cp.wait()

    return pl.pallas_call(
        body, name=name, out_shape=[jax.ShapeDtypeStruct((4,) + g.shape[1:], g.dtype) for g in gs],
        in_specs=[ANY] * n, out_specs=[ANY] * n,
        scratch_shapes=[pltpu.SemaphoreType.DMA((4 * n,)), pltpu.SemaphoreType.DMA((4 * n,))],
    )(*gs)


def _add_sibling(g, r1, name):
    _, r, w = g.shape
    tr, tc = _elementwise_block(r, w)
    core = jnp.reshape(lax.axis_index("c"), (1,)).astype(jnp.int32)

    def body(c_ref, g_ref, r_ref, o_ref, wire_ref):
        s = g_ref[...] + r_ref[...]
        o_ref[...] = s
        wire_ref[...] = s.astype(BF16)

    spec = pl.BlockSpec((1, tr, tc), lambda k, i, j, c_ref: (k, i, j))
    return pl.pallas_call(
        body, name=name, out_shape=[jax.ShapeDtypeStruct((4, r, w), F32), jax.ShapeDtypeStruct((4, r, w), BF16)],
        grid_spec=pltpu.PrefetchScalarGridSpec(
            num_scalar_prefetch=1, grid=(4, r // tr, w // tc),
            in_specs=[pl.BlockSpec((1, tr, tc), lambda k, i, j, c_ref: (2 * k + c_ref[0], i, j)), spec],
            out_specs=[spec, spec]),
        compiler_params=_cparams("parallel", "parallel", "parallel"),
    )(core, g, r1)


def _exchange_chips(ps, name):
    n = len(ps)

    def body(*refs):
        p_refs, out_refs = refs[:n], refs[n:2 * n]
        send_sems, recv_sems = refs[2 * n:]
        x, y, c = _place()
        chips = [(1 - x, y), (x, 1 - y), (1 - x, 1 - y)]
        copies = [pltpu.make_async_remote_copy(
            src_ref=p_refs[a].at[2 * px + py], dst_ref=out_refs[a].at[j], send_sem=send_sems.at[3 * a + j],
            recv_sem=recv_sems.at[3 * a + j], device_id=(px, py, c), device_id_type=MESH)
            for a in range(n) for j, (px, py) in enumerate(chips)]
        for cp in copies:
            cp.start()
        for cp in copies:
            cp.wait()

    return pl.pallas_call(
        body, name=name, out_shape=[jax.ShapeDtypeStruct((3,) + p.shape[1:], p.dtype) for p in ps],
        in_specs=[ANY] * n, out_specs=[ANY] * n,
        scratch_shapes=[pltpu.SemaphoreType.DMA((3 * n,)), pltpu.SemaphoreType.DMA((3 * n,))],
    )(*ps)


def _sum_chips(own, others, name):
    _, r, c = own.shape
    tr, tc = _elementwise_block(r, c)
    chip = jnp.reshape(2 * lax.axis_index("x") + lax.axis_index("y"), (1,)).astype(jnp.int32)

    def body(chip_ref, own_ref, others_ref, o_ref):
        g = own_ref[0]
        for j in range(3):
            g = g + others_ref[j].astype(F32)
        o_ref[...] = g

    return pl.pallas_call(
        body, name=name, out_shape=jax.ShapeDtypeStruct((r, c), F32),
        grid_spec=pltpu.PrefetchScalarGridSpec(
            num_scalar_prefetch=1, grid=(r // tr, c // tc),
            in_specs=[pl.BlockSpec((1, tr, tc), lambda i, j, chip_ref: (chip_ref[0], i, j)),
                      pl.BlockSpec((3, tr, tc), lambda i, j, chip_ref: (0, i, j))],
            out_specs=pl.BlockSpec((tr, tc), lambda i, j, chip_ref: (i, j))),
        compiler_params=_cparams("parallel", "parallel"),
    )(chip, own, others)


def _reorder_in_proj(wt):
    za_zb, zc, xbc, dt, gates = (wt[:4096], wt[4096:5120], wt[5120:7168], wt[7168:7184], wt[7184:])
    return jnp.concatenate([za_zb, xbc, gates, zc, dt, jnp.zeros((DT_PAD - 16, wt.shape[1]), wt.dtype)], axis=0)


def _restore_in_proj(wt):
    return jnp.concatenate([wt[:4096], wt[OFF_ZC:OFF_ZC + W_ZC], wt[OFF_XBC:OFF_XBC + W_XBC],
                            wt[OFF_DT:OFF_DT + 16], wt[OFF_GATE:OFF_GATE + W_GATE]], axis=0)


def _lanes_from_devices(g):
    return jnp.moveaxis(g, 0, 1).reshape(g.shape[1], N_DEV * g.shape[2])


def _lanes_to_devices(a):
    return jnp.moveaxis(a.reshape(a.shape[0], N_DEV, a.shape[1] // N_DEV), 1, 0)


def _pad_lanes(a, width):
    return jnp.pad(a, ((0, 0), (0, width - a.shape[1])))


BIG = ("w_in", "w_branch_a", "w_branch_b", "w_branch_c", "w_out", "w_mlp_up", "w_mlp_down")
SMALL_SHARDED = ("b_gate", "lru_conv_w", "ssd_conv_w")
REPLICATED = ("norm_mix_g", "gmlp_ln_g", "gmlp_ln_b", "gmlp_w_s", "gmlp_b_s", "lru_conv_b", "lru_w_r", "lru_b_r",
              "lru_w_i", "lru_b_i", "lru_lambda", "ssd_conv_b", "ssd_dt_bias", "ssd_a_log", "ssd_d", "ssd_norm_g",
              "norm_mlp_g", "final_norm_g")
WEIGHTS = ("norm_mix_g", "w_in", "b_gate", "gmlp_ln_g", "gmlp_ln_b", "gmlp_w_s", "gmlp_b_s", "lru_conv_w", "lru_conv_b",
           "lru_w_r", "lru_b_r", "lru_w_i", "lru_b_i", "lru_lambda", "ssd_conv_w", "ssd_conv_b", "ssd_dt_bias",
           "ssd_a_log", "ssd_d", "ssd_norm_g", "w_branch_a", "w_branch_b", "w_branch_c", "w_out", "norm_mlp_g",
           "w_mlp_up", "w_mlp_down", "final_norm_g")
TRANSPOSED = ("w_in", "w_mlp_up")
SMALL_MATRICES = ("gmlp_w_s", "lru_w_r", "lru_w_i")
SMALL_VECTORS = tuple(n for n in REPLICATED if n not in SMALL_MATRICES)


def _layer_params(full, l):
    row = lambda a: a.reshape(1, -1)
    return dict(
        norm_mix_g=row(full["norm_mix_g"][l]), norm_mlp_g=row(full["norm_mlp_g"][l]),
        gmlp=(row(full["gmlp_ln_g"][l]), row(full["gmlp_ln_b"][l]), full["gmlp_w_s"][l].reshape(GROUPS_A * CHUNK, CHUNK),
              full["gmlp_b_s"][l].T),
        lru=(full["lru_conv_w"][l], row(full["lru_conv_b"][l]), full["lru_w_r"][l].reshape(D, D // HEADS_B),
             row(full["lru_b_r"][l]), full["lru_w_i"][l].reshape(D, D // HEADS_B), row(full["lru_b_i"][l]),
             row(full["lru_lambda"][l])),
        ssd=(full["ssd_conv_w"][l], row(full["ssd_conv_b"][l]), _pad_lanes(row(full["ssd_dt_bias"][l]), DT_PAD),
             _pad_lanes(row(full["ssd_a_log"][l]), DT_PAD), _pad_lanes(row(full["ssd_d"][l]), DT_PAD),
             row(full["ssd_norm_g"][l])),
        b_gate=full["b_gate"][l],
    )


def _forward_layer(h, p, wb, l):
    tag = f"l{l}"
    t_row = 512
    (hn,), _, _ = _chunk_fwd(_f_rmsnorm, f"norm_mix_{tag}", t_row, [(h, 0, D)], [p["norm_mix_g"]], [(D, BF16)])
    proj = _matmul(hn, wb["w_in"], f"in_proj_{tag}", tb=True)
    (ya,), _, _ = _chunk_fwd(_f_gmlp, f"gmlp_{tag}", CHUNK, [(proj, OFF_ZA, W_ZA)], p["gmlp"], [(D, BF16)])
    lru_xs = [(proj, OFF_ZB, D), (proj, OFF_ZB + D, D)]
    (yb,), lru_saved, _ = _chunk_fwd(_f_lru, f"lru_{tag}", CHUNK, lru_xs, p["lru"], [(D, BF16)], halo_idx=(0,),
                                     carry_shapes=[(1, D)], save_carries=True)
    ssd_xs = [(proj, OFF_ZC, W_ZC), (proj, OFF_XBC, W_XBC), (proj, OFF_DT, W_DT)]
    (yc,), ssd_saved, _ = _chunk_fwd(_f_ssd, f"ssd_{tag}", CHUNK, ssd_xs, p["ssd"], [(D, BF16)], halo_idx=(1,),
                                     carry_shapes=[(HEADS_C * HEAD_DIM_C, STATE_C)], save_carries=True)
    pa = _matmul(ya, wb["w_branch_a"], f"branch_a_{tag}")
    pb = _matmul(yb, wb["w_branch_b"], f"branch_b_{tag}")
    pc = _matmul(yc, wb["w_branch_c"], f"branch_c_{tag}")
    merge_xs = [(pa, 0, D), (pb, 0, D), (pc, 0, D), (proj, OFF_GATE, W_GATE)]
    (merged,), _, _ = _chunk_fwd(_f_merge, f"merge_{tag}", t_row, merge_xs, [p["b_gate"]], [(D, BF16)])
    h_mid = _matmul(merged, wb["w_out"], f"out_proj_{tag}", epilogue=lambda acc, res: (acc + res,), extras=(h,))
    (hn2,), _, _ = _chunk_fwd(_f_rmsnorm, f"norm_mlp_{tag}", t_row, [(h_mid, 0, D)], [p["norm_mlp_g"]], [(D, BF16)])

    def relu_sq(acc):
        r = jnp.maximum(acc, 0.0)
        return r, r * r

    relu_up, act = _matmul(hn2, wb["w_mlp_up"], f"mlp_up_{tag}", tb=True, outs=(F32, BF16), epilogue=relu_sq)
    h_out = _matmul(act, wb["w_mlp_down"], f"mlp_down_{tag}", epilogue=lambda acc, res: (acc + res,), extras=(h_mid,))
    saved = dict(h=h, hn=hn, proj=proj, ya=ya, yb=yb, yc=yc, lru_saved=lru_saved, ssd_saved=ssd_saved, pa=pa, pb=pb,
                 pc=pc, merged=merged, h_mid=h_mid, hn2=hn2, relu_up=relu_up, act=act, lru_xs=lru_xs, ssd_xs=ssd_xs,
                 merge_xs=merge_xs)
    return h_out, saved


def _backward_layer(dh, sv, p, wb, l):
    tag = f"l{l}"
    t_row = 512
    g = {}
    d_up = _matmul(dh, wb["w_mlp_down"], f"d_act_{tag}", tb=True, outs=(BF16,),
                   epilogue=lambda acc, r: (acc * (2.0 * r),), extras=(sv["relu_up"],))
    g["w_mlp_down"] = _matmul(sv["act"], dh, f"dw_mlp_down_{tag}", ta=True)
    g["w_mlp_up"] = _matmul(d_up, sv["hn2"], f"dw_mlp_up_{tag}", ta=True)
    d_hn2 = _matmul(d_up, wb["w_mlp_up"], f"d_hn2_{tag}")
    (d_mid,), (g["norm_mlp_g"],) = _chunk_bwd(_f_rmsnorm_res, f"norm_mlp_bwd_{tag}", t_row, [(sv["h_mid"], 0, D)],
                                              [p["norm_mlp_g"]], [d_hn2, dh], [F32])
    d_merged = _matmul(d_mid, wb["w_out"], f"d_merged_{tag}", tb=True)
    g["w_out"] = _matmul(sv["merged"], d_mid, f"dw_out_{tag}", ta=True)
    (d_pa, d_pb, d_pc, d_gate), (g["b_gate"],) = _chunk_bwd(
        _f_merge, f"merge_bwd_{tag}", t_row, sv["merge_xs"], [p["b_gate"]], [d_merged], [BF16] * 4)
    d_y = {}
    for br, d_p, y in (("a", d_pa, sv["ya"]), ("b", d_pb, sv["yb"]), ("c", d_pc, sv["yc"])):
        g[f"w_branch_{br}"] = _matmul(y, d_p, f"dw_branch_{br}_{tag}", ta=True)
        d_y[br] = _matmul(d_p, wb[f"w_branch_{br}"], f"d_y{br}_{tag}", tb=True)
    (d_za,), g_gmlp = _chunk_bwd(_f_gmlp, f"gmlp_bwd_{tag}", CHUNK, [(sv["proj"], OFF_ZA, W_ZA)], p["gmlp"],
                                 [d_y["a"]], [BF16])
    (d_xb, d_gt), g_lru = _chunk_bwd(_f_lru, f"lru_bwd_{tag}", CHUNK, sv["lru_xs"], p["lru"], [d_y["b"]], [BF16] * 2,
                                     halo_idx=(0,), saved=sv["lru_saved"])
    (d_zc, d_xbc, d_dt), g_ssd = _chunk_bwd(_f_ssd, f"ssd_bwd_{tag}", CHUNK, sv["ssd_xs"], p["ssd"], [d_y["c"]],
                                            [BF16] * 3, halo_idx=(1,), saved=sv["ssd_saved"])
    d_proj = jnp.concatenate([d_za, d_xb, d_gt, d_xbc, d_gate, d_zc, d_dt], axis=1)
    g["w_in"] = _matmul(d_proj, sv["hn"], f"dw_in_{tag}", ta=True)
    d_hn = _matmul(d_proj, wb["w_in"], f"d_hn_{tag}")
    (d_h,), (g["norm_mix_g"],) = _chunk_bwd(_f_rmsnorm_res, f"norm_mix_bwd_{tag}", t_row, [(sv["h"], 0, D)],
                                            [p["norm_mix_g"]], [d_hn, d_mid], [F32])
    g["w_in"] = _restore_in_proj(g["w_in"])
    for n in BIG:
        g[n] = g[n].reshape(N_DEV, g[n].shape[0] // N_DEV, g[n].shape[1])
    g["gmlp_ln_g"], g["gmlp_ln_b"], g["gmlp_w_s"] = g_gmlp[:3]
    g["gmlp_b_s"] = g_gmlp[3].T
    (g["lru_conv_w"], g["lru_conv_b"], g["lru_w_r"], g["lru_b_r"], g["lru_w_i"], g["lru_b_i"], g["lru_lambda"]) = g_lru
    g["ssd_conv_w"], g["ssd_conv_b"] = g_ssd[:2]
    g["ssd_dt_bias"], g["ssd_a_log"], g["ssd_d"] = (a[:, :HEADS_C] for a in g_ssd[2:5])
    g["ssd_norm_g"] = g_ssd[5]
    return d_h, g


LOSS_ROWS = 512


def _loss_and_grads(h, target, full, layer_w):
    seq = h.shape[0]
    layer_p = [_layer_params(full, l) for l in range(DEPTH)]
    saved = []
    for l in range(DEPTH):
        h, sv = _forward_layer(h, layer_p[l], layer_w[l], l)
        saved.append(sv)
    final_g = full["final_norm_g"].reshape(1, D)
    loss_xs = [(h, 0, D), (target, 0, D)]
    t_loss = min(LOSS_ROWS, seq)
    _, _, (loss_acc,) = _chunk_fwd(_f_loss, "loss", t_loss, loss_xs, [final_g], [], carry_shapes=[(1, 128)],
                                   final_carries=True)
    zero_acc = jnp.zeros((seq // t_loss, 1, 128), F32)
    seed = lambda shape: (lax.broadcasted_iota(jnp.int32, shape, 1) == 0).astype(F32)
    (dh, _), (g_final,) = _chunk_bwd(_f_loss, "loss_bwd", t_loss, loss_xs, [final_g], [], [F32, F32], saved=[zero_acc],
                                     carry_seed=seed)
    layer_g = [None] * DEPTH
    for l in reversed(range(DEPTH)):
        dh, layer_g[l] = _backward_layer(dh, saved[l], layer_p[l], layer_w[l], l)
    return loss_acc[0, 0], dh, layer_g, g_final


def _small_views(d):
    views = {n: d[n] for n in REPLICATED}
    views["gmlp_b_s"] = d["gmlp_b_s"].reshape(DEPTH * GROUPS_A, CHUNK)
    views["final_norm_g"] = d["final_norm_g"].reshape(1, D)
    for n in SMALL_MATRICES:
        views[n] = d[n].reshape(DEPTH * D, D // HEADS_B)
    return views


def kernel(x, norm_mix_g, w_in, b_gate, gmlp_ln_g, gmlp_ln_b, gmlp_w_s, gmlp_b_s, lru_conv_w, lru_conv_b, lru_w_r, lru_b_r, lru_w_i, lru_b_i, lru_lambda, ssd_conv_w, ssd_conv_b, ssd_dt_bias, ssd_a_log, ssd_d, ssd_norm_g, w_branch_a, w_branch_b, w_branch_c, w_out, norm_mlp_g, w_mlp_up, w_mlp_down, final_norm_g, loss_target, m_norm_mix_g, m_w_in, m_b_gate, m_gmlp_ln_g, m_gmlp_ln_b, m_gmlp_w_s, m_gmlp_b_s, m_lru_conv_w, m_lru_conv_b, m_lru_w_r, m_lru_b_r, m_lru_w_i, m_lru_b_i, m_lru_lambda, m_ssd_conv_w, m_ssd_conv_b, m_ssd_dt_bias, m_ssd_a_log, m_ssd_d, m_ssd_norm_g, m_w_branch_a, m_w_branch_b, m_w_branch_c, m_w_out, m_norm_mlp_g, m_w_mlp_up, m_w_mlp_down, m_final_norm_g, v_norm_mix_g, v_w_in, v_b_gate, v_gmlp_ln_g, v_gmlp_ln_b, v_gmlp_w_s, v_gmlp_b_s, v_lru_conv_w, v_lru_conv_b, v_lru_w_r, v_lru_b_r, v_lru_w_i, v_lru_b_i, v_lru_lambda, v_ssd_conv_w, v_ssd_conv_b, v_ssd_dt_bias, v_ssd_a_log, v_ssd_d, v_ssd_norm_g, v_w_branch_a, v_w_branch_b, v_w_branch_c, v_w_out, v_norm_mlp_g, v_w_mlp_up, v_w_mlp_down, v_final_norm_g):
    args = locals()
    w = {n: args[n] for n in WEIGHTS}
    m = {n: args["m_" + n] for n in WEIGHTS}
    v = {n: args["v_" + n] for n in WEIGHTS}
    seq = x.shape[1]
    h = x.reshape(seq, D)
    target = loss_target.reshape(seq, D)

    def shard_on_wire(n):
        if n in SMALL_SHARDED:
            return w[n]
        return (jnp.swapaxes(w[n], 1, 2) if n in TRANSPOSED else w[n]).astype(BF16)

    sharded = BIG + SMALL_SHARDED
    gathered = dict(zip(sharded, _all_gather([shard_on_wire(n) for n in sharded], "gather_weights")))
    full = {n: w[n] for n in REPLICATED}
    for n in SMALL_SHARDED:
        full[n] = jnp.stack([_lanes_from_devices(gathered[n][:, l]) for l in range(DEPTH)])
    layer_w = []
    for l in range(DEPTH):
        wl = {n: gathered[n][:, l].reshape(-1, D) for n in BIG}
        wl["w_in"] = _reorder_in_proj(wl["w_in"])
        layer_w.append(wl)

    loss_local, dh, layer_g, g_final = _loss_and_grads(h, target, full, layer_w)
    loss = lax.psum(loss_local, ("x", "y", "c"))
    grad_x = dh.reshape(x.shape)
    out = {}
    kinds = ("grad", "delta", "new_m", "new_v")

    slabs = {(n, l): layer_g[l][n] for l in range(DEPTH) for n in BIG}
    for n in SMALL_SHARDED:
        slabs[n, None] = jnp.concatenate([_lanes_to_devices(layer_g[l][n]) for l in range(DEPTH)], axis=1)
    g_small = {n: jnp.concatenate([layer_g[l][n] for l in range(DEPTH)], axis=0) for n in REPLICATED[:-1]}
    g_small["final_norm_g"] = g_final
    for n in SMALL_MATRICES:
        slabs[n, None] = g_small[n].reshape(N_DEV, -1, g_small[n].shape[-1])
    keys = list(slabs)
    from_sibling = _exchange_sibling([slabs[k] for k in keys], "grads_to_sibling")
    chip_sums = [_add_sibling(slabs[k], r, f"add_sibling_{k[0]}_{k[1]}") for k, r in zip(keys, from_sibling)]
    from_chips = _exchange_chips([wire for _, wire in chip_sums], "grads_to_chips")
    reduced = {k: _sum_chips(own, others, f"sum_chips_{k[0]}_{k[1]}")
               for k, (own, _), others in zip(keys, chip_sums, from_chips)}
    for n in BIG:
        parts = [(reduced[n, l].T if n in TRANSPOSED else reduced[n, l])[None] for l in range(DEPTH)]
        for kind, a in zip(kinds, _adamw(parts, w[n], m[n], v[n], f"adamw_{n}")):
            out[kind, n] = a
    for n in SMALL_SHARDED:
        one = lambda a: a.reshape((1, -1, a.shape[-1]))
        for kind, a in zip(kinds, _adamw([reduced[n, None][None]], one(w[n]), one(m[n]), one(v[n]), f"adamw_{n}")):
            out[kind, n] = a.reshape(w[n].shape)

    to_gather = [reduced[n, None] if n in SMALL_MATRICES else g_small[n] for n in REPLICATED]
    g_all = dict(zip(REPLICATED, _all_gather(to_gather, "gather_small_grads")))
    wv, mv, vv = _small_views(w), _small_views(m), _small_views(v)
    res = _adamw_small([g_all[n] for n in SMALL_VECTORS], *[[d[n] for n in SMALL_VECTORS] for d in (wv, mv, vv)],
                       "adamw_vectors")
    for kind, arrays in zip(kinds, res):
        for n, a in zip(SMALL_VECTORS, arrays):
            out[kind, n] = a.reshape(w[n].shape)
    for n in SMALL_MATRICES:
        g_full = g_all[n].reshape((1, 1) + wv[n].shape)
        for kind, a in zip(kinds, _adamw([g_full[0]], wv[n][None], mv[n][None], vv[n][None], f"adamw_{n}")):
            out[kind, n] = a.reshape(w[n].shape)

    return (loss, grad_x, *[out[kind, n] for kind in kinds for n in WEIGHTS])
```

```python
import functools

import jax
import jax.numpy as jnp
from jax import lax
from jax.experimental import pallas as pl
from jax.experimental.pallas import tpu as pltpu
from jax.experimental.pallas import tpu_sc as plsc

F32 = jnp.float32
BF16 = jnp.bfloat16
MESH = pl.DeviceIdType.MESH

D = 1024
DEPTH = 2
EPS = 1e-6
CHUNK = 128
GROUPS_A = 8
HEADS_B = 8
LRU_C = 8.0
HEADS_C = 16
HEAD_DIM_C = 64
GROUPS_C = 4
STATE_C = 128
HIDDEN = 4 * D
DT_PAD = 128
OFF_ZA, W_ZA = 0, 2048
OFF_ZB, W_ZB = 2048, 2048
OFF_XBC, W_XBC = 4096, 2048
OFF_GATE, W_GATE = 6144, 3072
OFF_ZC, W_ZC = 9216, 1024
OFF_DT, W_DT = 10240, DT_PAD
D_IN_PAD = 10368
D_IN = 10256
N_DEV = 8
SEQ_GATHER = 1
SEQ_TO_SIBLING = SEQ_GATHER + DEPTH - 1
SEQ_TO_CHIPS = SEQ_TO_SIBLING + DEPTH - 1

ADAM_LR = 0.001
ADAM_B1 = 0.9
ADAM_B2 = 0.999
ADAM_EPS = 1e-08
ADAM_WD = 0.01
ADAM_STEP = 10

VMEM_LIMIT = 56 * 1024 * 1024
HALO = 8


def _cparams(*sem):
    return pltpu.CompilerParams(dimension_semantics=sem, vmem_limit_bytes=VMEM_LIMIT)


def _bf(x):
    return x.astype(BF16)


def _dg(a, b, ca, cb):
    return lax.dot_general(a, b, (((ca,), (cb,)), ((), ())), preferred_element_type=F32)


@functools.partial(jax.custom_vjp, nondiff_argnums=(2, 3))
def _mm(a, b, ta, tb):
    return _dg(_bf(a), _bf(b), 0 if ta else 1, 1 if tb else 0)


def _mm_fwd(a, b, ta, tb):
    return _mm(a, b, ta, tb), (a, b)


def _mm_bwd(ta, tb, res, g):
    a, b = res
    ma = 1 if ta else 0
    nb = 0 if tb else 1
    gb, ab, bb = _bf(g), _bf(a), _bf(b)
    da = _dg(bb, gb, nb, 1) if ta else _dg(gb, bb, 1, nb)
    db = _dg(gb, ab, 0, ma) if tb else _dg(ab, gb, ma, 0)
    return da.astype(a.dtype), db.astype(b.dtype)


_mm.defvjp(_mm_fwd, _mm_bwd)


@functools.partial(jax.custom_vjp, nondiff_argnums=(1, 2))
def _cols(x, lo, hi):
    return x[:, lo:hi]


def _cols_fwd(x, lo, hi):
    return x[:, lo:hi], x.shape[1]


def _cols_bwd(lo, hi, width, g):
    parts = []
    if lo:
        parts.append(jnp.zeros((g.shape[0], lo), g.dtype))
    parts.append(g)
    if width - hi:
        parts.append(jnp.zeros((g.shape[0], width - hi), g.dtype))
    return (jnp.concatenate(parts, axis=1) if len(parts) > 1 else g,)


_cols.defvjp(_cols_fwd, _cols_bwd)


@functools.partial(jax.custom_vjp, nondiff_argnums=(1, 2))
def _rows(x, lo, hi):
    return x[lo:hi, :]


def _rows_fwd(x, lo, hi):
    return x[lo:hi, :], x.shape[0]


def _rows_bwd(lo, hi, height, g):
    parts = []
    if lo:
        parts.append(jnp.zeros((lo, g.shape[1]), g.dtype))
    parts.append(g)
    if height - hi:
        parts.append(jnp.zeros((height - hi, g.shape[1]), g.dtype))
    return (jnp.concatenate(parts, axis=0) if len(parts) > 1 else g,)


_rows.defvjp(_rows_fwd, _rows_bwd)


def _col(x, j):
    lane = lax.broadcasted_iota(jnp.int32, x.shape, 1)
    return jnp.sum(jnp.where(lane == j, x, 0.0), axis=1, keepdims=True)


def _row(x, i):
    r = lax.broadcasted_iota(jnp.int32, x.shape, 0)
    return jnp.sum(jnp.where(r == i, x, 0.0), axis=0, keepdims=True)


def _roll_down(x, s):
    return pltpu.roll(x, s, 0)


def _roll_up(x, s):
    return pltpu.roll(x, x.shape[0] - s, 0)


def _row_iota(x):
    return lax.broadcasted_iota(jnp.int32, x.shape, 0)


@functools.partial(jax.custom_vjp, nondiff_argnums=(2,))
def _shift_rows(halo, x, s):
    if s == 0:
        return x
    return _roll_down(jnp.concatenate([halo, x], axis=0), s)[HALO:]


def _shift_rows_fwd(halo, x, s):
    return _shift_rows(halo, x, s), None


def _shift_rows_bwd(s, _, g):
    if s == 0:
        return jnp.zeros((HALO, g.shape[1]), g.dtype), g
    ge = jnp.concatenate([jnp.zeros((HALO, g.shape[1]), g.dtype), g], axis=0)
    de = _roll_up(ge, s)
    return de[:HALO], de[HALO:]


_shift_rows.defvjp(_shift_rows_fwd, _shift_rows_bwd)


def _scan_down(a, b):
    n = a.shape[0]
    row = _row_iota(a)
    s = 1
    while s < n:
        keep = row >= s
        a_sh = jnp.where(keep, _roll_down(a, s), 1.0)
        b_sh = jnp.where(keep, _roll_down(b, s), 0.0)
        b = a * b_sh + b
        a = a * a_sh
        s *= 2
    return a, b


def _scan_up(a, b):
    n = a.shape[0]
    row = _row_iota(a)
    s = 1
    while s < n:
        keep = row < n - s
        a_sh = jnp.where(keep, _roll_up(a, s), 1.0)
        b_sh = jnp.where(keep, _roll_up(b, s), 0.0)
        b = a * b_sh + b
        a = a * a_sh
        s *= 2
    return b


@jax.custom_vjp
def _lin_scan(a, b, h0):
    p, h = _scan_down(a, b)
    return h + p * h0


def _lin_scan_fwd(a, b, h0):
    h = _lin_scan(a, b, h0)
    return h, (a, h0, h)


def _lin_scan_bwd(res, g):
    a, h0, h = res
    n = a.shape[0]
    row = _row_iota(a)
    a_next = jnp.where(row < n - 1, _roll_up(a, 1), 0.0)
    gg = _scan_up(a_next, g)
    h_prev = jnp.where(row >= 1, _roll_down(h, 1), h0)
    return gg * h_prev, gg, _row(a * gg, 0)


_lin_scan.defvjp(_lin_scan_fwd, _lin_scan_bwd)


@jax.custom_vjp
def _cumsum_rows(x):
    n = x.shape[0]
    row = _row_iota(x)
    s = 1
    while s < n:
        x = x + jnp.where(row >= s, _roll_down(x, s), 0.0)
        s *= 2
    return x


def _cumsum_rows_fwd(x):
    return _cumsum_rows(x), None


def _cumsum_rows_bwd(_, g):
    n = g.shape[0]
    row = _row_iota(g)
    s = 1
    while s < n:
        g = g + jnp.where(row < n - s, _roll_up(g, s), 0.0)
        s *= 2
    return (g,)


_cumsum_rows.defvjp(_cumsum_rows_fwd, _cumsum_rows_bwd)


def _sigmoid(x):
    return jax.nn.sigmoid(x)


def _softplus(x):
    return jnp.maximum(x, 0.0) + jnp.log1p(jnp.exp(-jnp.abs(x)))


def _gelu(x):
    return jax.nn.gelu(x, approximate=True)


def _neg_expm1(x):
    series = -x * (1.0 + x * (0.5 + x * (1.0 / 6.0 + x * (1.0 / 24.0))))
    return jnp.where(x > -0.01, series, 1.0 - jnp.exp(x))


def _rms(x, g):
    return x * lax.rsqrt(jnp.mean(x * x, axis=-1, keepdims=True) + EPS) * g


def _f_rmsnorm(carries, halos, xs, params):
    (h,) = xs
    (g,) = params
    return (), (_rms(h, g),)


def _f_rmsnorm_res(carries, halos, xs, params):
    (h,) = xs
    (g,) = params
    return (), (_rms(h, g), h)


def _f_gmlp(carries, halos, xs, params):
    (za,) = xs
    ln_g, ln_b, w_s, b_st = params
    ga = _gelu(za)
    u = _cols(ga, 0, D)
    v = _cols(ga, D, 2 * D)
    vc = v - jnp.mean(v, axis=-1, keepdims=True)
    vn = vc * lax.rsqrt(jnp.mean(vc * vc, axis=-1, keepdims=True) + EPS) * ln_g + ln_b
    q = CHUNK
    causal = lax.broadcasted_iota(jnp.int32, (q, q), 0) >= lax.broadcasted_iota(jnp.int32, (q, q), 1)
    mixed = []
    for g in range(GROUPS_A):
        w = jnp.where(causal, _rows(w_s, g * q, (g + 1) * q), 0.0)
        mixed.append(_mm(w, _cols(vn, g * q, (g + 1) * q), False, False) + _col(b_st, g))
    return (), (u * jnp.concatenate(mixed, axis=1),)


def _conv4(halo, x, w, b):
    y = b + _row(w, 3) * x
    for k in range(3):
        y = y + _row(w, k) * _shift_rows(halo, x, 3 - k)
    return y


def _f_lru(carries, halos, xs, params):
    (h0,) = carries
    (halo,) = halos
    xb_pre, gate = xs
    conv_w, conv_b, w_r, b_r, w_i, b_i, lam = params
    xb = _conv4(halo, xb_pre, conv_w, conv_b)
    hd = D // HEADS_B
    r_parts, i_parts = [], []
    for h in range(HEADS_B):
        xh = _cols(xb, h * hd, (h + 1) * hd)
        r_parts.append(_mm(xh, _rows(w_r, h * hd, (h + 1) * hd), False, False))
        i_parts.append(_mm(xh, _rows(w_i, h * hd, (h + 1) * hd), False, False))
    r = _sigmoid(jnp.concatenate(r_parts, axis=1) + b_r)
    i = _sigmoid(jnp.concatenate(i_parts, axis=1) + b_i)
    log_a = -LRU_C * r * _softplus(-lam)
    a = jnp.exp(log_a)
    inp = jnp.sqrt(_neg_expm1(2.0 * log_a)) * (i * xb)
    h = _lin_scan(a, inp, h0)
    return (_row(h, h.shape[0] - 1),), (_gelu(gate) * h,)


def _f_ssd(carries, halos, xs, params):
    (st,) = carries
    (halo,) = halos
    z, xbc_pre, dt_raw = xs
    conv_w, conv_b, dt_bias, a_log, d_skip, norm_g = params
    t = z.shape[0]
    xc = _conv4(halo, xbc_pre, conv_w, conv_b)
    xbc = xc * _sigmoid(xc)
    x_all = _cols(xbc, 0, D)
    b_all = _cols(xbc, D, D + GROUPS_C * STATE_C)
    c_all = _cols(xbc, D + GROUPS_C * STATE_C, D + 2 * GROUPS_C * STATE_C)
    dt = _softplus(dt_raw + dt_bias)
    adt = dt * (-jnp.exp(a_log))
    acs = _cumsum_rows(adt)
    acs_t = acs.T
    a_last = _row(acs, t - 1)
    lo = lax.broadcasted_iota(jnp.int32, (t, 128), 1) < HEAD_DIM_C
    lo_rows = lax.broadcasted_iota(jnp.int32, (128, STATE_C), 0) < HEAD_DIM_C
    causal = lax.broadcasted_iota(jnp.int32, (t, t), 0) >= lax.broadcasted_iota(jnp.int32, (t, t), 1)
    y_parts, st_parts = [], []
    for g in range(GROUPS_C):
        bg = _cols(b_all, g * STATE_C, (g + 1) * STATE_C)
        cg = _cols(c_all, g * STATE_C, (g + 1) * STATE_C)
        cb = _mm(cg, bg, False, True)
        for pr in range(2):
            pair = 2 * g + pr
            h0, h1 = 2 * pair, 2 * pair + 1
            x2 = _cols(x_all, pair * 128, (pair + 1) * 128)
            ac0, ac1 = _col(acs, h0), _col(acs, h1)
            l0 = jnp.exp(jnp.where(causal, ac0 - _row(acs_t, h0), -1e30))
            l1 = jnp.exp(jnp.where(causal, ac1 - _row(acs_t, h1), -1e30))
            xdt = x2 * jnp.where(lo, _col(dt, h0), _col(dt, h1))
            y_diag = (_mm(cb * l0, jnp.where(lo, xdt, 0.0), False, False)
                      + _mm(cb * l1, jnp.where(lo, 0.0, xdt), False, False))
            al0, al1 = _col(a_last, h0), _col(a_last, h1)
            decay_s = jnp.where(lo, jnp.exp(al0 - ac0), jnp.exp(al1 - ac1))
            s_new = _mm(xdt * decay_s, bg, True, False)
            prev = _rows(st, pair * 128, (pair + 1) * 128)
            y_off = _mm(cg, prev, False, True) * jnp.where(lo, jnp.exp(ac0), jnp.exp(ac1))
            skip = jnp.where(lo, _col(d_skip, h0), _col(d_skip, h1))
            y_parts.append(y_diag + y_off + x2 * skip)
            st_parts.append(prev * jnp.where(lo_rows, jnp.exp(al0), jnp.exp(al1)) + s_new)
    y = jnp.concatenate(y_parts, axis=1) * (z * _sigmoid(z))
    gw = D // GROUPS_C
    yn = []
    for g in range(GROUPS_C):
        yg = _cols(y, g * gw, (g + 1) * gw)
        yn.append(yg * lax.rsqrt(jnp.mean(yg * yg, axis=-1, keepdims=True) + EPS))
    return (jnp.concatenate(st_parts, axis=0),), (jnp.concatenate(yn, axis=1) * norm_g,)


def _f_merge(carries, halos, xs, params):
    pa, pb, pc, g_raw = xs
    (b_gate,) = params
    m = (_sigmoid(_cols(g_raw, 0, D) + _row(b_gate, 0)) * pa
         + _sigmoid(_cols(g_raw, D, 2 * D) + _row(b_gate, 1)) * pb
         + _sigmoid(_cols(g_raw, 2 * D, 3 * D) + _row(b_gate, 2)) * pc)
    return (), (m,)


def _f_loss(carries, halos, xs, params):
    (acc,) = carries
    h, target = xs
    (g,) = params
    err = jnp.square(_rms(h, g) - target)
    part = 0.5 * jnp.sum(jnp.mean(err, axis=-1, keepdims=True), axis=0, keepdims=True)
    return (acc + part,), ()


def _x_specs(xs, t, index_of):
    specs = []
    for arr, off, width in xs:
        assert off % width == 0 and off + width <= arr.shape[1]
        specs.append(pl.BlockSpec((t, width), functools.partial(lambda j, cb: (index_of(j), cb), cb=off // width)))
    return specs


def _halo_specs(xs, halo_idx, t, index_of):
    specs = []
    for xi in halo_idx:
        _, off, width = xs[xi]
        specs.append(pl.BlockSpec(
            (HALO, width),
            functools.partial(lambda j, cb: (jnp.maximum(index_of(j) * (t // HALO) - 1, 0), cb), cb=off // width)))
    return specs


def _full_spec(a):
    return pl.BlockSpec(a.shape, functools.partial(lambda j, nd: (0,) * nd, nd=a.ndim))


def _chunk_fwd(f, name, t, xs, params, outs, halo_idx=(), carry_shapes=(), save_carries=False, final_carries=False):
    s = xs[0][0].shape[0]
    n = s // t
    nx, nh, npar, no, nc = len(xs), len(halo_idx), len(params), len(outs), len(carry_shapes)
    ns = nc if save_carries else 0
    nf = nc if final_carries else 0

    def body(*refs):
        x_refs, refs = refs[:nx], refs[nx:]
        h_refs, refs = refs[:nh], refs[nh:]
        p_refs, refs = refs[:npar], refs[npar:]
        y_refs, refs = refs[:no], refs[no:]
        s_refs, refs = refs[:ns], refs[ns:]
        f_refs, c_refs = refs[:nf], refs[nf:]
        i = pl.program_id(0)

        @pl.when(i == 0)
        def _():
            for c in c_refs:
                c[...] = jnp.zeros_like(c)

        carries = tuple(c[...] for c in c_refs)
        for s_ref, c in zip(s_refs, carries):
            s_ref[0] = c
        halos = tuple(jnp.where(i > 0, h[...].astype(F32), 0.0) for h in h_refs)
        new_c, ys = f(carries, halos, tuple(x[...].astype(F32) for x in x_refs), tuple(p[...] for p in p_refs))
        for y_ref, y in zip(y_refs, ys):
            y_ref[...] = y.astype(y_ref.dtype)
        for c, v in zip(c_refs, new_c):
            c[...] = v
        for f_ref, v in zip(f_refs, new_c):
            f_ref[...] = v

    ident = lambda j: j
    out_shape = [jax.ShapeDtypeStruct((s, w), dt) for w, dt in outs]
    out_specs = [pl.BlockSpec((t, w), lambda j: (j, 0)) for w, _ in outs]
    if save_carries:
        out_shape += [jax.ShapeDtypeStruct((n,) + tuple(cs), F32) for cs in carry_shapes]
        out_specs += [pl.BlockSpec((1,) + tuple(cs), lambda j: (j, 0, 0)) for cs in carry_shapes]
    if final_carries:
        out_shape += [jax.ShapeDtypeStruct(tuple(cs), F32) for cs in carry_shapes]
        out_specs += [pl.BlockSpec(tuple(cs), lambda j: (0, 0)) for cs in carry_shapes]
    res = pl.pallas_call(
        body, name=name, grid=(n,),
        in_specs=_x_specs(xs, t, ident) + _halo_specs(xs, halo_idx, t, ident) + [_full_spec(p) for p in params],
        out_specs=out_specs, out_shape=out_shape,
        scratch_shapes=[pltpu.VMEM(tuple(cs), F32) for cs in carry_shapes],
        compiler_params=_cparams("arbitrary"),
    )(*[x[0] for x in xs], *[xs[xi][0] for xi in halo_idx], *params)
    return res[:no], res[no:no + ns], res[no + ns:]


def _chunk_bwd(f, name, t, xs, params, dys, dx_dtypes, halo_idx=(), saved=(), carry_seed=None):
    s = xs[0][0].shape[0]
    n = s // t
    nx, nh, npar, nc, ndy = len(xs), len(halo_idx), len(params), len(saved), len(dys)

    def body(*refs):
        x_refs, refs = refs[:nx], refs[nx:]
        h_refs, refs = refs[:nh], refs[nh:]
        p_refs, refs = refs[:npar], refs[npar:]
        s_refs, refs = refs[:nc], refs[nc:]
        dy_refs, refs = refs[:ndy], refs[ndy:]
        dx_refs, refs = refs[:nx], refs[nx:]
        dp_refs, refs = refs[:npar], refs[npar:]
        dc_refs, dh_refs = refs[:nc], refs[nc:]
        j = pl.program_id(0)
        i = n - 1 - j

        @pl.when(j == 0)
        def _():
            for dc in dc_refs:
                dc[...] = jnp.zeros_like(dc) if carry_seed is None else carry_seed(dc.shape)
            for r in dh_refs + dp_refs:
                r[...] = jnp.zeros_like(r)

        carries = tuple(s_ref[0] for s_ref in s_refs)
        halos = tuple(jnp.where(i > 0, h[...].astype(F32), 0.0) for h in h_refs)
        x_vals = tuple(x[...].astype(F32) for x in x_refs)
        p_vals = tuple(p[...] for p in p_refs)
        _, vjp = jax.vjp(f, carries, halos, x_vals, p_vals)
        d_car, d_hal, d_xs, d_par = vjp((tuple(dc[...] for dc in dc_refs), tuple(d[...].astype(F32) for d in dy_refs)))
        d_xs = list(d_xs)
        for k, xi in enumerate(halo_idx):
            w = xs[xi][2]
            d_xs[xi] = d_xs[xi] + jnp.concatenate([jnp.zeros((t - HALO, w), F32), dh_refs[k][...]], axis=0)
            dh_refs[k][...] = jnp.where(i > 0, d_hal[k], 0.0)
        for dx_ref, dx in zip(dx_refs, d_xs):
            dx_ref[...] = dx.astype(dx_ref.dtype)
        for dp_ref, dp in zip(dp_refs, d_par):
            dp_ref[...] += dp
        for dc, v in zip(dc_refs, d_car):
            dc[...] = v

    rev = lambda j: n - 1 - j
    in_specs = (_x_specs(xs, t, rev) + _halo_specs(xs, halo_idx, t, rev) + [_full_spec(p) for p in params]
                + [pl.BlockSpec((1,) + a.shape[1:], lambda j: (n - 1 - j, 0, 0)) for a in saved]
                + [pl.BlockSpec((t, d.shape[1]), lambda j: (n - 1 - j, 0)) for d in dys])
    out_shape = ([jax.ShapeDtypeStruct((s, w), dt) for (_, _, w), dt in zip(xs, dx_dtypes)]
                 + [jax.ShapeDtypeStruct(p.shape, F32) for p in params])
    out_specs = ([pl.BlockSpec((t, w), lambda j: (n - 1 - j, 0)) for _, _, w in xs] + [_full_spec(p) for p in params])
    res = pl.pallas_call(
        body, name=name, grid=(n,), in_specs=in_specs, out_specs=out_specs, out_shape=out_shape,
        scratch_shapes=([pltpu.VMEM(a.shape[1:], F32) for a in saved]
                        + [pltpu.VMEM((HALO, xs[xi][2]), F32) for xi in halo_idx]),
        compiler_params=_cparams("arbitrary"),
    )(*[x[0] for x in xs], *[xs[xi][0] for xi in halo_idx], *params, *saved, *dys)
    return res[:nx], res[nx:]


def _tile(dim, pref):
    for cand in pref:
        if dim % cand == 0:
            return cand
    return dim


def _matmul(a, b, name, ta=False, tb=False, outs=(F32,), epilogue=None, extras=()):
    m, k = (a.shape[1], a.shape[0]) if ta else a.shape
    n = b.shape[0] if tb else b.shape[1]
    tm = _tile(m, (1024, 1152, 512, 256, 128))
    tn = _tile(n, (1152, 1024, 512, 256, 128))
    tk = _tile(k, (1024, 1152, 512, 256, 128))
    nk = k // tk
    ne, no = len(extras), len(outs)
    ca, cb = (0 if ta else 1), (1 if tb else 0)

    def body(*refs):
        a_ref, b_ref = refs[:2]
        e_refs = refs[2:2 + ne]
        o_refs = refs[2 + ne:2 + ne + no]
        acc = refs[-1]
        kk = pl.program_id(2)

        @pl.when(kk == 0)
        def _():
            acc[...] = jnp.zeros_like(acc)

        acc[...] += _dg(_bf(a_ref[...]), _bf(b_ref[...]), ca, cb)

        @pl.when(kk == nk - 1)
        def _():
            res = acc[...]
            vals = (res,) if epilogue is None else epilogue(res, *[e[...] for e in e_refs])
            for o_ref, v in zip(o_refs, vals):
                o_ref[...] = v.astype(o_ref.dtype)

    a_spec = pl.BlockSpec((tk, tm), lambda i, j, kk: (kk, i)) if ta else pl.BlockSpec((tm, tk), lambda i, j, kk: (i, kk))
    b_spec = pl.BlockSpec((tn, tk), lambda i, j, kk: (j, kk)) if tb else pl.BlockSpec((tk, tn), lambda i, j, kk: (kk, j))
    mn_spec = pl.BlockSpec((tm, tn), lambda i, j, kk: (i, j))
    res = pl.pallas_call(
        body, name=name, grid=(m // tm, n // tn, nk),
        in_specs=[a_spec, b_spec] + [mn_spec] * ne,
        out_specs=[mn_spec] * no,
        out_shape=[jax.ShapeDtypeStruct((m, n), dt) for dt in outs],
        scratch_shapes=[pltpu.VMEM((tm, tn), F32)],
        compiler_params=_cparams("parallel", "parallel", "arbitrary"),
    )(a, b, *extras)
    return res if no > 1 else res[0]


def _elementwise_block(r, c):
    if r % 8 == 0 and r >= 8:
        return _tile(r, (256, 128, 64, 32, 16, 8)), c
    return r, _tile(c, (256, 128))


def _adamw_math(g, w, m, v):
    m_new = ADAM_B1 * m + (1.0 - ADAM_B1) * g
    v_new = ADAM_B2 * v + (1.0 - ADAM_B2) * jnp.square(g)
    m_hat = m_new / (1.0 - ADAM_B1 ** ADAM_STEP)
    v_hat = v_new / (1.0 - ADAM_B2 ** ADAM_STEP)
    return -ADAM_LR * (m_hat / (jnp.sqrt(v_hat) + ADAM_EPS) + ADAM_WD * w), m_new, v_new


def _adamw(parts, w, m, v, name):
    nl, r, c = w.shape
    k = parts[0].shape[0]
    tr = _tile(r, (128, 64, 32, 16, 8))
    nb = r // tr

    def body(*refs):
        p_refs, (w_ref, m_ref, v_ref), outs = refs[:nl], refs[nl:nl + 3], refs[nl + 3:]
        layer = pl.program_id(0)
        for q in range(nl):
            @pl.when(layer == q)
            def _(q=q):
                g = p_refs[q][0]
                for j in range(1, k):
                    g = g + p_refs[q][j]
                vals = (g,) + _adamw_math(g, w_ref[0], m_ref[0], v_ref[0])
                for o_ref, val in zip(outs, vals):
                    o_ref[0] = val

    spec = pl.BlockSpec((1, tr, c), lambda l, i: (l, i, 0))
    part_specs = [pl.BlockSpec((k, tr, c), functools.partial(
        lambda l, i, q: (0, jnp.where(l == q, i, jnp.where(l < q, 0, nb - 1)), 0), q=q)) for q in range(nl)]
    return pl.pallas_call(
        body, name=name, grid=(nl, nb), in_specs=part_specs + [spec] * 3,
        out_specs=[spec] * 4, out_shape=[jax.ShapeDtypeStruct((nl, r, c), F32)] * 4,
        compiler_params=_cparams("arbitrary", "arbitrary"),
    )(*parts, w, m, v)


def _adamw_small(gathered, ws, ms, vs, name):
    n = len(ws)

    def body(*refs):
        g_refs, w_refs, m_refs, v_refs = refs[:n], refs[n:2 * n], refs[2 * n:3 * n], refs[3 * n:4 * n]
        outs = refs[4 * n:]
        for i in range(n):
            g = g_refs[i][0]
            for j in range(1, N_DEV):
                g = g + g_refs[i][j]
            vals = (g,) + _adamw_math(g, w_refs[i][...], m_refs[i][...], v_refs[i][...])
            for kind, val in enumerate(vals):
                outs[kind * n + i][...] = val

    res = pl.pallas_call(
        body, name=name, out_shape=[jax.ShapeDtypeStruct(a.shape, F32) for _ in range(4) for a in ws],
        compiler_params=pltpu.CompilerParams(vmem_limit_bytes=VMEM_LIMIT),
    )(*gathered, *ws, *ms, *vs)
    return [res[kind * n:(kind + 1) * n] for kind in range(4)]


ANY = pl.BlockSpec(memory_space=pl.ANY)


def _place():
    return lax.axis_index("x"), lax.axis_index("y"), lax.axis_index("c")


def _handshake(peers):
    barrier = pltpu.get_barrier_semaphore()
    for peer in peers:
        pl.semaphore_signal(barrier, inc=1, device_id=peer, device_id_type=MESH)
    pl.semaphore_wait(barrier, len(peers))


def _comm_call(body, name, inputs, out_shape, scratch, sequencer_id=None, after=()):
    if sequencer_id is None:
        return pl.pallas_call(body, name=name, out_shape=out_shape, in_specs=[ANY] * len(inputs),
                              out_specs=[ANY] * len(out_shape), scratch_shapes=scratch)(*inputs)
    n_in, n_after = len(inputs), len(after)

    def sequencer_body(*refs):
        body(*refs[:n_in], *refs[n_in + n_after:])

    return pl.kernel(
        sequencer_body, out_type=out_shape, mesh=plsc.ScalarSubcoreMesh(axis_name="sequencer", num_cores=1),
        scratch_types=scratch, compiler_params=pltpu.CompilerParams(collective_id=sequencer_id), name=name,
    )(*inputs, *after)


def _all_gather(blocks, name, sequencer_id=None, after=()):
    n = len(blocks)

    def body(*refs):
        x_refs, out_refs = refs[:n], refs[n:2 * n]
        send_sems, recv_sems, local_sems = refs[2 * n:]
        x, y, c = _place()
        me, sibling = (x, y, c), (x, y, 1 - c)
        chips = [(1 - x, y), (x, 1 - y), (1 - x, 1 - y)]
        if sequencer_id is not None:
            _handshake([sibling] + [(*chip, c) for chip in chips])

        def slot(a, px, py, pc):
            return out_refs[a].at[4 * px + 2 * py + pc]

        def copy(a, k, blk, to, src=None):
            return pltpu.make_async_remote_copy(
                src_ref=slot(a, *blk) if src is None else src, dst_ref=slot(a, *blk),
                send_sem=send_sems.at[7 * a + k], recv_sem=recv_sems.at[7 * a + k], device_id=to, device_id_type=MESH)

        mine = [pltpu.make_async_copy(x_refs[a], slot(a, *me), local_sems.at[a]) for a in range(n)]
        first = []
        for a in range(n):
            mine[a].start()
            first.append(copy(a, 0, me, sibling, src=x_refs[a]))
            first += [copy(a, 1 + j, me, (*chip, c), src=x_refs[a]) for j, chip in enumerate(chips)]
        for cp in first:
            cp.start()
        passed = []
        for j, chip in enumerate(chips):
            for a in range(n):
                copy(a, 1 + j, (*chip, c), me).wait_recv()
                passed.append(copy(a, 4 + j, (*chip, c), sibling))
                passed[-1].start()
        for a in range(n):
            copy(a, 0, sibling, me).wait_recv()
            for j, chip in enumerate(chips):
                copy(a, 4 + j, (*chip, 1 - c), me).wait_recv()
        for cp in first + passed:
            cp.wait_send()
        for cp in mine:
            cp.wait()

    return _comm_call(
        body, name, blocks, [jax.ShapeDtypeStruct((N_DEV,) + b.shape, b.dtype) for b in blocks],
        [pltpu.SemaphoreType.DMA((7 * n,)), pltpu.SemaphoreType.DMA((7 * n,)), pltpu.SemaphoreType.DMA((n,))],
        sequencer_id, after)


def _exchange_sibling(gs, name, sequencer_id=None, after=()):
    n = len(gs)

    def body(*refs):
        g_refs, out_refs = refs[:n], refs[n:2 * n]
        send_sems, recv_sems = refs[2 * n:]
        x, y, c = _place()
        if sequencer_id is not None:
            _handshake([(x, y, 1 - c)])
        copies = [pltpu.make_async_remote_copy(
            src_ref=g_refs[a].at[2 * k + 1 - c], dst_ref=out_refs[a].at[k], send_sem=send_sems.at[4 * a + k],
            recv_sem=recv_sems.at[4 * a + k], device_id=(x, y, 1 - c), device_id_type=MESH)
            for a in range(n) for k in range(4)]
        for cp in copies:
            cp.start()
        for cp in copies:
            cp.wait()

    return _comm_call(body, name, gs, [jax.ShapeDtypeStruct((4,) + g.shape[1:], g.dtype) for g in gs],
                      [pltpu.SemaphoreType.DMA((4 * n,)), pltpu.SemaphoreType.DMA((4 * n,))], sequencer_id, after)


def _add_sibling(g, r1, name):
    _, r, w = g.shape
    tr, tc = _elementwise_block(r, w)
    core = jnp.reshape(lax.axis_index("c"), (1,)).astype(jnp.int32)

    def body(c_ref, g_ref, r_ref, o_ref, wire_ref):
        s = g_ref[...] + r_ref[...]
        o_ref[...] = s
        wire_ref[...] = s.astype(BF16)

    spec = pl.BlockSpec((1, tr, tc), lambda k, i, j, c_ref: (k, i, j))
    return pl.pallas_call(
        body, name=name, out_shape=[jax.ShapeDtypeStruct((4, r, w), F32), jax.ShapeDtypeStruct((4, r, w), BF16)],
        grid_spec=pltpu.PrefetchScalarGridSpec(
            num_scalar_prefetch=1, grid=(4, r // tr, w // tc),
            in_specs=[pl.BlockSpec((1, tr, tc), lambda k, i, j, c_ref: (2 * k + c_ref[0], i, j)), spec],
            out_specs=[spec, spec]),
        compiler_params=_cparams("parallel", "parallel", "parallel"),
    )(core, g, r1)


def _exchange_chips(ps, name, sequencer_id=None, after=()):
    n = len(ps)

    def body(*refs):
        p_refs, out_refs = refs[:n], refs[n:2 * n]
        send_sems, recv_sems = refs[2 * n:]
        x, y, c = _place()
        chips = [(1 - x, y), (x, 1 - y), (1 - x, 1 - y)]
        if sequencer_id is not None:
            _handshake([(*chip, c) for chip in chips])
        copies = [pltpu.make_async_remote_copy(
            src_ref=p_refs[a].at[2 * px + py], dst_ref=out_refs[a].at[j], send_sem=send_sems.at[3 * a + j],
            recv_sem=recv_sems.at[3 * a + j], device_id=(px, py, c), device_id_type=MESH)
            for a in range(n) for j, (px, py) in enumerate(chips)]
        for cp in copies:
            cp.start()
        for cp in copies:
            cp.wait()

    return _comm_call(body, name, ps, [jax.ShapeDtypeStruct((3,) + p.shape[1:], p.dtype) for p in ps],
                      [pltpu.SemaphoreType.DMA((3 * n,)), pltpu.SemaphoreType.DMA((3 * n,))], sequencer_id, after)


def _sum_chips(own, others, name):
    _, r, c = own.shape
    tr, tc = _elementwise_block(r, c)
    chip = jnp.reshape(2 * lax.axis_index("x") + lax.axis_index("y"), (1,)).astype(jnp.int32)

    def body(chip_ref, own_ref, others_ref, o_ref):
        g = own_ref[0]
        for j in range(3):
            g = g + others_ref[j].astype(F32)
        o_ref[...] = g

    return pl.pallas_call(
        body, name=name, out_shape=jax.ShapeDtypeStruct((r, c), F32),
        grid_spec=pltpu.PrefetchScalarGridSpec(
            num_scalar_prefetch=1, grid=(r // tr, c // tc),
            in_specs=[pl.BlockSpec((1, tr, tc), lambda i, j, chip_ref: (chip_ref[0], i, j)),
                      pl.BlockSpec((3, tr, tc), lambda i, j, chip_ref: (0, i, j))],
            out_specs=pl.BlockSpec((tr, tc), lambda i, j, chip_ref: (i, j))),
        compiler_params=_cparams("parallel", "parallel"),
    )(chip, own, others)


def _reorder_in_proj(wt):
    za_zb, zc, xbc, dt, gates = (wt[:4096], wt[4096:5120], wt[5120:7168], wt[7168:7184], wt[7184:])
    return jnp.concatenate([za_zb, xbc, gates, zc, dt, jnp.zeros((DT_PAD - 16, wt.shape[1]), wt.dtype)], axis=0)


def _restore_in_proj(wt):
    return jnp.concatenate([wt[:4096], wt[OFF_ZC:OFF_ZC + W_ZC], wt[OFF_XBC:OFF_XBC + W_XBC],
                            wt[OFF_DT:OFF_DT + 16], wt[OFF_GATE:OFF_GATE + W_GATE]], axis=0)


def _lanes_from_devices(g):
    return jnp.moveaxis(g, 0, 1).reshape(g.shape[1], N_DEV * g.shape[2])


def _lanes_to_devices(a):
    return jnp.moveaxis(a.reshape(a.shape[0], N_DEV, a.shape[1] // N_DEV), 1, 0)


def _pad_lanes(a, width):
    return jnp.pad(a, ((0, 0), (0, width - a.shape[1])))


BIG = ("w_in", "w_branch_a", "w_branch_b", "w_branch_c", "w_out", "w_mlp_up", "w_mlp_down")
SMALL_SHARDED = ("b_gate", "lru_conv_w", "ssd_conv_w")
REPLICATED = ("norm_mix_g", "gmlp_ln_g", "gmlp_ln_b", "gmlp_w_s", "gmlp_b_s", "lru_conv_b", "lru_w_r", "lru_b_r",
              "lru_w_i", "lru_b_i", "lru_lambda", "ssd_conv_b", "ssd_dt_bias", "ssd_a_log", "ssd_d", "ssd_norm_g",
              "norm_mlp_g", "final_norm_g")
WEIGHTS = ("norm_mix_g", "w_in", "b_gate", "gmlp_ln_g", "gmlp_ln_b", "gmlp_w_s", "gmlp_b_s", "lru_conv_w", "lru_conv_b",
           "lru_w_r", "lru_b_r", "lru_w_i", "lru_b_i", "lru_lambda", "ssd_conv_w", "ssd_conv_b", "ssd_dt_bias",
           "ssd_a_log", "ssd_d", "ssd_norm_g", "w_branch_a", "w_branch_b", "w_branch_c", "w_out", "norm_mlp_g",
           "w_mlp_up", "w_mlp_down", "final_norm_g")
TRANSPOSED = ("w_in", "w_mlp_up")
SMALL_MATRICES = ("gmlp_w_s", "lru_w_r", "lru_w_i")
SMALL_VECTORS = tuple(n for n in REPLICATED if n not in SMALL_MATRICES)


def _layer_params(full, l):
    row = lambda a: a.reshape(1, -1)
    return dict(
        norm_mix_g=row(full["norm_mix_g"][l]), norm_mlp_g=row(full["norm_mlp_g"][l]),
        gmlp=(row(full["gmlp_ln_g"][l]), row(full["gmlp_ln_b"][l]), full["gmlp_w_s"][l].reshape(GROUPS_A * CHUNK, CHUNK),
              full["gmlp_b_s"][l].T),
        lru=(full["lru_conv_w"][l], row(full["lru_conv_b"][l]), full["lru_w_r"][l].reshape(D, D // HEADS_B),
             row(full["lru_b_r"][l]), full["lru_w_i"][l].reshape(D, D // HEADS_B), row(full["lru_b_i"][l]),
             row(full["lru_lambda"][l])),
        ssd=(full["ssd_conv_w"][l], row(full["ssd_conv_b"][l]), _pad_lanes(row(full["ssd_dt_bias"][l]), DT_PAD),
             _pad_lanes(row(full["ssd_a_log"][l]), DT_PAD), _pad_lanes(row(full["ssd_d"][l]), DT_PAD),
             row(full["ssd_norm_g"][l])),
        b_gate=full["b_gate"][l],
    )


def _forward_layer(h, p, wb, l):
    tag = f"l{l}"
    t_row = 512
    (hn,), _, _ = _chunk_fwd(_f_rmsnorm, f"norm_mix_{tag}", t_row, [(h, 0, D)], [p["norm_mix_g"]], [(D, BF16)])
    proj = _matmul(hn, wb["w_in"], f"in_proj_{tag}", tb=True)
    (ya,), _, _ = _chunk_fwd(_f_gmlp, f"gmlp_{tag}", CHUNK, [(proj, OFF_ZA, W_ZA)], p["gmlp"], [(D, BF16)])
    lru_xs = [(proj, OFF_ZB, D), (proj, OFF_ZB + D, D)]
    (yb,), lru_saved, _ = _chunk_fwd(_f_lru, f"lru_{tag}", CHUNK, lru_xs, p["lru"], [(D, BF16)], halo_idx=(0,),
                                     carry_shapes=[(1, D)], save_carries=True)
    ssd_xs = [(proj, OFF_ZC, W_ZC), (proj, OFF_XBC, W_XBC), (proj, OFF_DT, W_DT)]
    (yc,), ssd_saved, _ = _chunk_fwd(_f_ssd, f"ssd_{tag}", CHUNK, ssd_xs, p["ssd"], [(D, BF16)], halo_idx=(1,),
                                     carry_shapes=[(HEADS_C * HEAD_DIM_C, STATE_C)], save_carries=True)
    pa = _matmul(ya, wb["w_branch_a"], f"branch_a_{tag}")
    pb = _matmul(yb, wb["w_branch_b"], f"branch_b_{tag}")
    pc = _matmul(yc, wb["w_branch_c"], f"branch_c_{tag}")
    merge_xs = [(pa, 0, D), (pb, 0, D), (pc, 0, D), (proj, OFF_GATE, W_GATE)]
    (merged,), _, _ = _chunk_fwd(_f_merge, f"merge_{tag}", t_row, merge_xs, [p["b_gate"]], [(D, BF16)])
    h_mid = _matmul(merged, wb["w_out"], f"out_proj_{tag}", epilogue=lambda acc, res: (acc + res,), extras=(h,))
    (hn2,), _, _ = _chunk_fwd(_f_rmsnorm, f"norm_mlp_{tag}", t_row, [(h_mid, 0, D)], [p["norm_mlp_g"]], [(D, BF16)])

    def relu_sq(acc):
        r = jnp.maximum(acc, 0.0)
        return r, r * r

    relu_up, act = _matmul(hn2, wb["w_mlp_up"], f"mlp_up_{tag}", tb=True, outs=(F32, BF16), epilogue=relu_sq)
    h_out = _matmul(act, wb["w_mlp_down"], f"mlp_down_{tag}", epilogue=lambda acc, res: (acc + res,), extras=(h_mid,))
    saved = dict(h=h, hn=hn, proj=proj, ya=ya, yb=yb, yc=yc, lru_saved=lru_saved, ssd_saved=ssd_saved, pa=pa, pb=pb,
                 pc=pc, merged=merged, h_mid=h_mid, hn2=hn2, relu_up=relu_up, act=act, lru_xs=lru_xs, ssd_xs=ssd_xs,
                 merge_xs=merge_xs)
    return h_out, saved


def _backward_layer(dh, sv, p, wb, l):
    tag = f"l{l}"
    t_row = 512
    g = {}
    d_up = _matmul(dh, wb["w_mlp_down"], f"d_act_{tag}", tb=True, outs=(BF16,),
                   epilogue=lambda acc, r: (acc * (2.0 * r),), extras=(sv["relu_up"],))
    g["w_mlp_down"] = _matmul(sv["act"], dh, f"dw_mlp_down_{tag}", ta=True)
    g["w_mlp_up"] = _matmul(d_up, sv["hn2"], f"dw_mlp_up_{tag}", ta=True)
    d_hn2 = _matmul(d_up, wb["w_mlp_up"], f"d_hn2_{tag}")
    (d_mid,), (g["norm_mlp_g"],) = _chunk_bwd(_f_rmsnorm_res, f"norm_mlp_bwd_{tag}", t_row, [(sv["h_mid"], 0, D)],
                                              [p["norm_mlp_g"]], [d_hn2, dh], [F32])
    d_merged = _matmul(d_mid, wb["w_out"], f"d_merged_{tag}", tb=True)
    g["w_out"] = _matmul(sv["merged"], d_mid, f"dw_out_{tag}", ta=True)
    (d_pa, d_pb, d_pc, d_gate), (g["b_gate"],) = _chunk_bwd(
        _f_merge, f"merge_bwd_{tag}", t_row, sv["merge_xs"], [p["b_gate"]], [d_merged], [BF16] * 4)
    d_y = {}
    for br, d_p, y in (("a", d_pa, sv["ya"]), ("b", d_pb, sv["yb"]), ("c", d_pc, sv["yc"])):
        g[f"w_branch_{br}"] = _matmul(y, d_p, f"dw_branch_{br}_{tag}", ta=True)
        d_y[br] = _matmul(d_p, wb[f"w_branch_{br}"], f"d_y{br}_{tag}", tb=True)
    (d_za,), g_gmlp = _chunk_bwd(_f_gmlp, f"gmlp_bwd_{tag}", CHUNK, [(sv["proj"], OFF_ZA, W_ZA)], p["gmlp"],
                                 [d_y["a"]], [BF16])
    (d_xb, d_gt), g_lru = _chunk_bwd(_f_lru, f"lru_bwd_{tag}", CHUNK, sv["lru_xs"], p["lru"], [d_y["b"]], [BF16] * 2,
                                     halo_idx=(0,), saved=sv["lru_saved"])
    (d_zc, d_xbc, d_dt), g_ssd = _chunk_bwd(_f_ssd, f"ssd_bwd_{tag}", CHUNK, sv["ssd_xs"], p["ssd"], [d_y["c"]],
                                            [BF16] * 3, halo_idx=(1,), saved=sv["ssd_saved"])
    d_proj = jnp.concatenate([d_za, d_xb, d_gt, d_xbc, d_gate, d_zc, d_dt], axis=1)
    g["w_in"] = _matmul(d_proj, sv["hn"], f"dw_in_{tag}", ta=True)
    d_hn = _matmul(d_proj, wb["w_in"], f"d_hn_{tag}")
    (d_h,), (g["norm_mix_g"],) = _chunk_bwd(_f_rmsnorm_res, f"norm_mix_bwd_{tag}", t_row, [(sv["h"], 0, D)],
                                            [p["norm_mix_g"]], [d_hn, d_mid], [F32])
    g["w_in"] = _restore_in_proj(g["w_in"])
    for n in BIG:
        g[n] = g[n].reshape(N_DEV, g[n].shape[0] // N_DEV, g[n].shape[1])
    g["gmlp_ln_g"], g["gmlp_ln_b"], g["gmlp_w_s"] = g_gmlp[:3]
    g["gmlp_b_s"] = g_gmlp[3].T
    (g["lru_conv_w"], g["lru_conv_b"], g["lru_w_r"], g["lru_b_r"], g["lru_w_i"], g["lru_b_i"], g["lru_lambda"]) = g_lru
    g["ssd_conv_w"], g["ssd_conv_b"] = g_ssd[:2]
    g["ssd_dt_bias"], g["ssd_a_log"], g["ssd_d"] = (a[:, :HEADS_C] for a in g_ssd[2:5])
    g["ssd_norm_g"] = g_ssd[5]
    return d_h, g


LOSS_ROWS = 512


def _loss_and_grads(h, target, full, layer_w):
    seq = h.shape[0]
    layer_p = [_layer_params(full, l) for l in range(DEPTH)]
    saved = []
    for l in range(DEPTH):
        h, sv = _forward_layer(h, layer_p[l], layer_w[l], l)
        saved.append(sv)
    final_g = full["final_norm_g"].reshape(1, D)
    loss_xs = [(h, 0, D), (target, 0, D)]
    t_loss = min(LOSS_ROWS, seq)
    _, _, (loss_acc,) = _chunk_fwd(_f_loss, "loss", t_loss, loss_xs, [final_g], [], carry_shapes=[(1, 128)],
                                   final_carries=True)
    zero_acc = jnp.zeros((seq // t_loss, 1, 128), F32)
    seed = lambda shape: (lax.broadcasted_iota(jnp.int32, shape, 1) == 0).astype(F32)
    (dh, _), (g_final,) = _chunk_bwd(_f_loss, "loss_bwd", t_loss, loss_xs, [final_g], [], [F32, F32], saved=[zero_acc],
                                     carry_seed=seed)
    layer_g = [None] * DEPTH
    for l in reversed(range(DEPTH)):
        dh, layer_g[l] = _backward_layer(dh, saved[l], layer_p[l], layer_w[l], l)
    return loss_acc[0, 0], dh, layer_g, g_final


def _small_views(d):
    views = {n: d[n] for n in REPLICATED}
    views["gmlp_b_s"] = d["gmlp_b_s"].reshape(DEPTH * GROUPS_A, CHUNK)
    views["final_norm_g"] = d["final_norm_g"].reshape(1, D)
    for n in SMALL_MATRICES:
        views[n] = d[n].reshape(DEPTH * D, D // HEADS_B)
    return views


def kernel(x, norm_mix_g, w_in, b_gate, gmlp_ln_g, gmlp_ln_b, gmlp_w_s, gmlp_b_s, lru_conv_w, lru_conv_b, lru_w_r, lru_b_r, lru_w_i, lru_b_i, lru_lambda, ssd_conv_w, ssd_conv_b, ssd_dt_bias, ssd_a_log, ssd_d, ssd_norm_g, w_branch_a, w_branch_b, w_branch_c, w_out, norm_mlp_g, w_mlp_up, w_mlp_down, final_norm_g, loss_target, m_norm_mix_g, m_w_in, m_b_gate, m_gmlp_ln_g, m_gmlp_ln_b, m_gmlp_w_s, m_gmlp_b_s, m_lru_conv_w, m_lru_conv_b, m_lru_w_r, m_lru_b_r, m_lru_w_i, m_lru_b_i, m_lru_lambda, m_ssd_conv_w, m_ssd_conv_b, m_ssd_dt_bias, m_ssd_a_log, m_ssd_d, m_ssd_norm_g, m_w_branch_a, m_w_branch_b, m_w_branch_c, m_w_out, m_norm_mlp_g, m_w_mlp_up, m_w_mlp_down, m_final_norm_g, v_norm_mix_g, v_w_in, v_b_gate, v_gmlp_ln_g, v_gmlp_ln_b, v_gmlp_w_s, v_gmlp_b_s, v_lru_conv_w, v_lru_conv_b, v_lru_w_r, v_lru_b_r, v_lru_w_i, v_lru_b_i, v_lru_lambda, v_ssd_conv_w, v_ssd_conv_b, v_ssd_dt_bias, v_ssd_a_log, v_ssd_d, v_ssd_norm_g, v_w_branch_a, v_w_branch_b, v_w_branch_c, v_w_out, v_norm_mlp_g, v_w_mlp_up, v_w_mlp_down, v_final_norm_g):
    args = locals()
    w = {n: args[n] for n in WEIGHTS}
    m = {n: args["m_" + n] for n in WEIGHTS}
    v = {n: args["v_" + n] for n in WEIGHTS}
    seq = x.shape[1]
    h = x.reshape(seq, D)
    target = loss_target.reshape(seq, D)

    def shard_on_wire(n, l):
        return (w[n][l].T if n in TRANSPOSED else w[n][l]).astype(BF16)

    first = _all_gather([shard_on_wire(n, 0) for n in BIG] + [w[n] for n in SMALL_SHARDED], "gather_weights_l0")
    gathered = [dict(zip(BIG, first))]
    for l in range(1, DEPTH):
        later = _all_gather([shard_on_wire(n, l) for n in BIG], f"gather_weights_l{l}", sequencer_id=SEQ_GATHER + l - 1,
                            after=(gathered[-1]["w_in"],))
        gathered.append(dict(zip(BIG, later)))
    full = {n: w[n] for n in REPLICATED}
    for n, g in zip(SMALL_SHARDED, first[len(BIG):]):
        full[n] = jnp.stack([_lanes_from_devices(g[:, l]) for l in range(DEPTH)])
    layer_w = []
    for l in range(DEPTH):
        wl = {n: gathered[l][n].reshape(-1, D) for n in BIG}
        wl["w_in"] = _reorder_in_proj(wl["w_in"])
        layer_w.append(wl)

    loss_local, dh, layer_g, g_final = _loss_and_grads(h, target, full, layer_w)
    loss = lax.psum(loss_local, ("x", "y", "c"))
    grad_x = dh.reshape(x.shape)
    out = {}
    kinds = ("grad", "delta", "new_m", "new_v")

    def reduce_scatter(slabs, tag, sequencer_ids=(None, None)):
        keys = list(slabs)
        from_sibling = _exchange_sibling([slabs[k] for k in keys], f"grads_to_sibling_{tag}", sequencer_ids[0])
        chip_sums = [_add_sibling(slabs[k], r, f"add_sibling_{k[0]}_{k[1]}") for k, r in zip(keys, from_sibling)]
        from_chips = _exchange_chips([wire for _, wire in chip_sums], f"grads_to_chips_{tag}", sequencer_ids[1])
        return {k: _sum_chips(own, others, f"sum_chips_{k[0]}_{k[1]}")
                for k, (own, _), others in zip(keys, chip_sums, from_chips)}

    reduced = {}
    for l in range(DEPTH - 1, 0, -1):
        ids = (SEQ_TO_SIBLING + l - 1, SEQ_TO_CHIPS + l - 1)
        reduced.update(reduce_scatter({(n, l): layer_g[l][n] for n in BIG}, f"l{l}", ids))
    slabs = {(n, 0): layer_g[0][n] for n in BIG}
    for n in SMALL_SHARDED:
        slabs[n, None] = jnp.concatenate([_lanes_to_devices(layer_g[l][n]) for l in range(DEPTH)], axis=1)
    g_small = {n: jnp.concatenate([layer_g[l][n] for l in range(DEPTH)], axis=0) for n in REPLICATED[:-1]}
    g_small["final_norm_g"] = g_final
    for n in SMALL_MATRICES:
        slabs[n, None] = g_small[n].reshape(N_DEV, -1, g_small[n].shape[-1])
    reduced.update(reduce_scatter(slabs, "l0"))
    for n in BIG:
        parts = [(reduced[n, l].T if n in TRANSPOSED else reduced[n, l])[None] for l in range(DEPTH)]
        for kind, a in zip(kinds, _adamw(parts, w[n], m[n], v[n], f"adamw_{n}")):
            out[kind, n] = a
    for n in SMALL_SHARDED:
        one = lambda a: a.reshape((1, -1, a.shape[-1]))
        for kind, a in zip(kinds, _adamw([reduced[n, None][None]], one(w[n]), one(m[n]), one(v[n]), f"adamw_{n}")):
            out[kind, n] = a.reshape(w[n].shape)

    to_gather = [reduced[n, None] if n in SMALL_MATRICES else g_small[n] for n in REPLICATED]
    g_all = dict(zip(REPLICATED, _all_gather(to_gather, "gather_small_grads")))
    wv, mv, vv = _small_views(w), _small_views(m), _small_views(v)
    res = _adamw_small([g_all[n] for n in SMALL_VECTORS], *[[d[n] for n in SMALL_VECTORS] for d in (wv, mv, vv)],
                       "adamw_vectors")
    for kind, arrays in zip(kinds, res):
        for n, a in zip(SMALL_VECTORS, arrays):
            out[kind, n] = a.reshape(w[n].shape)
    for n in SMALL_MATRICES:
        g_full = g_all[n].reshape((1, 1) + wv[n].shape)
        for kind, a in zip(kinds, _adamw([g_full[0]], wv[n][None], mv[n][None], vv[n][None], f"adamw_{n}")):
            out[kind, n] = a.reshape(w[n].shape)

    return (loss, grad_x, *[out[kind, n] for kind in kinds for n in WEIGHTS])
```

```python
import functools

import jax
import jax.numpy as jnp
from jax import lax
from jax.experimental import pallas as pl
from jax.experimental.pallas import tpu as pltpu
from jax.experimental.pallas import tpu_sc as plsc

F32 = jnp.float32
BF16 = jnp.bfloat16
MESH = pl.DeviceIdType.MESH

D = 1024
DEPTH = 2
EPS = 1e-6
CHUNK = 128
GROUPS_A = 8
HEADS_B = 8
LRU_C = 8.0
HEADS_C = 16
HEAD_DIM_C = 64
GROUPS_C = 4
STATE_C = 128
HIDDEN = 4 * D
DT_PAD = 128
OFF_ZA, W_ZA = 0, 2048
OFF_ZB, W_ZB = 2048, 2048
OFF_XBC, W_XBC = 4096, 2048
OFF_GATE, W_GATE = 6144, 3072
OFF_ZC, W_ZC = 9216, 1024
OFF_DT, W_DT = 10240, DT_PAD
D_IN_PAD = 10368
D_IN = 10256
N_DEV = 8
SEQ_GATHER = 1
SEQ_TO_SIBLING = SEQ_GATHER + DEPTH
SEQ_TO_CHIPS = SEQ_TO_SIBLING + 1

ADAM_LR = 0.001
ADAM_B1 = 0.9
ADAM_B2 = 0.999
ADAM_EPS = 1e-08
ADAM_WD = 0.01
ADAM_STEP = 10

VMEM_LIMIT = 56 * 1024 * 1024
HALO = 8


def _cparams(*sem):
    return pltpu.CompilerParams(dimension_semantics=sem, vmem_limit_bytes=VMEM_LIMIT)


def _bf(x):
    return x.astype(BF16)


def _dg(a, b, ca, cb):
    return lax.dot_general(a, b, (((ca,), (cb,)), ((), ())), preferred_element_type=F32)


@functools.partial(jax.custom_vjp, nondiff_argnums=(2, 3))
def _mm(a, b, ta, tb):
    return _dg(_bf(a), _bf(b), 0 if ta else 1, 1 if tb else 0)


def _mm_fwd(a, b, ta, tb):
    return _mm(a, b, ta, tb), (a, b)


def _mm_bwd(ta, tb, res, g):
    a, b = res
    ma = 1 if ta else 0
    nb = 0 if tb else 1
    gb, ab, bb = _bf(g), _bf(a), _bf(b)
    da = _dg(bb, gb, nb, 1) if ta else _dg(gb, bb, 1, nb)
    db = _dg(gb, ab, 0, ma) if tb else _dg(ab, gb, ma, 0)
    return da.astype(a.dtype), db.astype(b.dtype)


_mm.defvjp(_mm_fwd, _mm_bwd)


@functools.partial(jax.custom_vjp, nondiff_argnums=(1, 2))
def _cols(x, lo, hi):
    return x[:, lo:hi]


def _cols_fwd(x, lo, hi):
    return x[:, lo:hi], x.shape[1]


def _cols_bwd(lo, hi, width, g):
    parts = []
    if lo:
        parts.append(jnp.zeros((g.shape[0], lo), g.dtype))
    parts.append(g)
    if width - hi:
        parts.append(jnp.zeros((g.shape[0], width - hi), g.dtype))
    return (jnp.concatenate(parts, axis=1) if len(parts) > 1 else g,)


_cols.defvjp(_cols_fwd, _cols_bwd)


@functools.partial(jax.custom_vjp, nondiff_argnums=(1, 2))
def _rows(x, lo, hi):
    return x[lo:hi, :]


def _rows_fwd(x, lo, hi):
    return x[lo:hi, :], x.shape[0]


def _rows_bwd(lo, hi, height, g):
    parts = []
    if lo:
        parts.append(jnp.zeros((lo, g.shape[1]), g.dtype))
    parts.append(g)
    if height - hi:
        parts.append(jnp.zeros((height - hi, g.shape[1]), g.dtype))
    return (jnp.concatenate(parts, axis=0) if len(parts) > 1 else g,)


_rows.defvjp(_rows_fwd, _rows_bwd)


def _col(x, j):
    lane = lax.broadcasted_iota(jnp.int32, x.shape, 1)
    return jnp.sum(jnp.where(lane == j, x, 0.0), axis=1, keepdims=True)


def _row(x, i):
    r = lax.broadcasted_iota(jnp.int32, x.shape, 0)
    return jnp.sum(jnp.where(r == i, x, 0.0), axis=0, keepdims=True)


def _roll_down(x, s):
    return pltpu.roll(x, s, 0)


def _roll_up(x, s):
    return pltpu.roll(x, x.shape[0] - s, 0)


def _row_iota(x):
    return lax.broadcasted_iota(jnp.int32, x.shape, 0)


@functools.partial(jax.custom_vjp, nondiff_argnums=(2,))
def _shift_rows(halo, x, s):
    if s == 0:
        return x
    return _roll_down(jnp.concatenate([halo, x], axis=0), s)[HALO:]


def _shift_rows_fwd(halo, x, s):
    return _shift_rows(halo, x, s), None


def _shift_rows_bwd(s, _, g):
    if s == 0:
        return jnp.zeros((HALO, g.shape[1]), g.dtype), g
    ge = jnp.concatenate([jnp.zeros((HALO, g.shape[1]), g.dtype), g], axis=0)
    de = _roll_up(ge, s)
    return de[:HALO], de[HALO:]


_shift_rows.defvjp(_shift_rows_fwd, _shift_rows_bwd)


def _scan_down(a, b):
    n = a.shape[0]
    row = _row_iota(a)
    s = 1
    while s < n:
        keep = row >= s
        a_sh = jnp.where(keep, _roll_down(a, s), 1.0)
        b_sh = jnp.where(keep, _roll_down(b, s), 0.0)
        b = a * b_sh + b
        a = a * a_sh
        s *= 2
    return a, b


def _scan_up(a, b):
    n = a.shape[0]
    row = _row_iota(a)
    s = 1
    while s < n:
        keep = row < n - s
        a_sh = jnp.where(keep, _roll_up(a, s), 1.0)
        b_sh = jnp.where(keep, _roll_up(b, s), 0.0)
        b = a * b_sh + b
        a = a * a_sh
        s *= 2
    return b


@jax.custom_vjp
def _lin_scan(a, b, h0):
    p, h = _scan_down(a, b)
    return h + p * h0


def _lin_scan_fwd(a, b, h0):
    h = _lin_scan(a, b, h0)
    return h, (a, h0, h)


def _lin_scan_bwd(res, g):
    a, h0, h = res
    n = a.shape[0]
    row = _row_iota(a)
    a_next = jnp.where(row < n - 1, _roll_up(a, 1), 0.0)
    gg = _scan_up(a_next, g)
    h_prev = jnp.where(row >= 1, _roll_down(h, 1), h0)
    return gg * h_prev, gg, _row(a * gg, 0)


_lin_scan.defvjp(_lin_scan_fwd, _lin_scan_bwd)


@jax.custom_vjp
def _cumsum_rows(x):
    n = x.shape[0]
    row = _row_iota(x)
    s = 1
    while s < n:
        x = x + jnp.where(row >= s, _roll_down(x, s), 0.0)
        s *= 2
    return x


def _cumsum_rows_fwd(x):
    return _cumsum_rows(x), None


def _cumsum_rows_bwd(_, g):
    n = g.shape[0]
    row = _row_iota(g)
    s = 1
    while s < n:
        g = g + jnp.where(row < n - s, _roll_up(g, s), 0.0)
        s *= 2
    return (g,)


_cumsum_rows.defvjp(_cumsum_rows_fwd, _cumsum_rows_bwd)


def _sigmoid(x):
    return jax.nn.sigmoid(x)


def _softplus(x):
    return jnp.maximum(x, 0.0) + jnp.log1p(jnp.exp(-jnp.abs(x)))


def _gelu(x):
    return jax.nn.gelu(x, approximate=True)


def _neg_expm1(x):
    series = -x * (1.0 + x * (0.5 + x * (1.0 / 6.0 + x * (1.0 / 24.0))))
    return jnp.where(x > -0.01, series, 1.0 - jnp.exp(x))


def _rms(x, g):
    return x * lax.rsqrt(jnp.mean(x * x, axis=-1, keepdims=True) + EPS) * g


def _f_rmsnorm(carries, halos, xs, params):
    (h,) = xs
    (g,) = params
    return (), (_rms(h, g),)


def _f_rmsnorm_res(carries, halos, xs, params):
    (h,) = xs
    (g,) = params
    return (), (_rms(h, g), h)


def _f_gmlp(carries, halos, xs, params):
    (za,) = xs
    ln_g, ln_b, w_s, b_st = params
    ga = _gelu(za)
    u = _cols(ga, 0, D)
    v = _cols(ga, D, 2 * D)
    vc = v - jnp.mean(v, axis=-1, keepdims=True)
    vn = vc * lax.rsqrt(jnp.mean(vc * vc, axis=-1, keepdims=True) + EPS) * ln_g + ln_b
    q = CHUNK
    causal = lax.broadcasted_iota(jnp.int32, (q, q), 0) >= lax.broadcasted_iota(jnp.int32, (q, q), 1)
    mixed = []
    for g in range(GROUPS_A):
        w = jnp.where(causal, _rows(w_s, g * q, (g + 1) * q), 0.0)
        mixed.append(_mm(w, _cols(vn, g * q, (g + 1) * q), False, False) + _col(b_st, g))
    return (), (u * jnp.concatenate(mixed, axis=1),)


def _conv4(halo, x, w, b):
    y = b + _row(w, 3) * x
    for k in range(3):
        y = y + _row(w, k) * _shift_rows(halo, x, 3 - k)
    return y


def _f_lru(carries, halos, xs, params):
    (h0,) = carries
    (halo,) = halos
    xb_pre, gate = xs
    conv_w, conv_b, w_r, b_r, w_i, b_i, lam = params
    xb = _conv4(halo, xb_pre, conv_w, conv_b)
    hd = D // HEADS_B
    r_parts, i_parts = [], []
    for h in range(HEADS_B):
        xh = _cols(xb, h * hd, (h + 1) * hd)
        r_parts.append(_mm(xh, _rows(w_r, h * hd, (h + 1) * hd), False, False))
        i_parts.append(_mm(xh, _rows(w_i, h * hd, (h + 1) * hd), False, False))
    r = _sigmoid(jnp.concatenate(r_parts, axis=1) + b_r)
    i = _sigmoid(jnp.concatenate(i_parts, axis=1) + b_i)
    log_a = -LRU_C * r * _softplus(-lam)
    a = jnp.exp(log_a)
    inp = jnp.sqrt(_neg_expm1(2.0 * log_a)) * (i * xb)
    h = _lin_scan(a, inp, h0)
    return (_row(h, h.shape[0] - 1),), (_gelu(gate) * h,)


def _f_ssd(carries, halos, xs, params):
    (st,) = carries
    (halo,) = halos
    z, xbc_pre, dt_raw = xs
    conv_w, conv_b, dt_bias, a_log, d_skip, norm_g = params
    t = z.shape[0]
    xc = _conv4(halo, xbc_pre, conv_w, conv_b)
    xbc = xc * _sigmoid(xc)
    x_all = _cols(xbc, 0, D)
    b_all = _cols(xbc, D, D + GROUPS_C * STATE_C)
    c_all = _cols(xbc, D + GROUPS_C * STATE_C, D + 2 * GROUPS_C * STATE_C)
    dt = _softplus(dt_raw + dt_bias)
    adt = dt * (-jnp.exp(a_log))
    acs = _cumsum_rows(adt)
    acs_t = acs.T
    a_last = _row(acs, t - 1)
    lo = lax.broadcasted_iota(jnp.int32, (t, 128), 1) < HEAD_DIM_C
    lo_rows = lax.broadcasted_iota(jnp.int32, (128, STATE_C), 0) < HEAD_DIM_C
    causal = lax.broadcasted_iota(jnp.int32, (t, t), 0) >= lax.broadcasted_iota(jnp.int32, (t, t), 1)
    y_parts, st_parts = [], []
    for g in range(GROUPS_C):
        bg = _cols(b_all, g * STATE_C, (g + 1) * STATE_C)
        cg = _cols(c_all, g * STATE_C, (g + 1) * STATE_C)
        cb = _mm(cg, bg, False, True)
        for pr in range(2):
            pair = 2 * g + pr
            h0, h1 = 2 * pair, 2 * pair + 1
            x2 = _cols(x_all, pair * 128, (pair + 1) * 128)
            ac0, ac1 = _col(acs, h0), _col(acs, h1)
            l0 = jnp.exp(jnp.where(causal, ac0 - _row(acs_t, h0), -1e30))
            l1 = jnp.exp(jnp.where(causal, ac1 - _row(acs_t, h1), -1e30))
            xdt = x2 * jnp.where(lo, _col(dt, h0), _col(dt, h1))
            y_diag = (_mm(cb * l0, jnp.where(lo, xdt, 0.0), False, False)
                      + _mm(cb * l1, jnp.where(lo, 0.0, xdt), False, False))
            al0, al1 = _col(a_last, h0), _col(a_last, h1)
            decay_s = jnp.where(lo, jnp.exp(al0 - ac0), jnp.exp(al1 - ac1))
            s_new = _mm(xdt * decay_s, bg, True, False)
            prev = _rows(st, pair * 128, (pair + 1) * 128)
            y_off = _mm(cg, prev, False, True) * jnp.where(lo, jnp.exp(ac0), jnp.exp(ac1))
            skip = jnp.where(lo, _col(d_skip, h0), _col(d_skip, h1))
            y_parts.append(y_diag + y_off + x2 * skip)
            st_parts.append(prev * jnp.where(lo_rows, jnp.exp(al0), jnp.exp(al1)) + s_new)
    y = jnp.concatenate(y_parts, axis=1) * (z * _sigmoid(z))
    gw = D // GROUPS_C
    yn = []
    for g in range(GROUPS_C):
        yg = _cols(y, g * gw, (g + 1) * gw)
        yn.append(yg * lax.rsqrt(jnp.mean(yg * yg, axis=-1, keepdims=True) + EPS))
    return (jnp.concatenate(st_parts, axis=0),), (jnp.concatenate(yn, axis=1) * norm_g,)


def _f_merge(carries, halos, xs, params):
    pa, pb, pc, g_raw = xs
    (b_gate,) = params
    m = (_sigmoid(_cols(g_raw, 0, D) + _row(b_gate, 0)) * pa
         + _sigmoid(_cols(g_raw, D, 2 * D) + _row(b_gate, 1)) * pb
         + _sigmoid(_cols(g_raw, 2 * D, 3 * D) + _row(b_gate, 2)) * pc)
    return (), (m,)


def _f_loss(carries, halos, xs, params):
    (acc,) = carries
    h, target = xs
    (g,) = params
    err = jnp.square(_rms(h, g) - target)
    part = 0.5 * jnp.sum(jnp.mean(err, axis=-1, keepdims=True), axis=0, keepdims=True)
    return (acc + part,), ()


def _x_specs(xs, t, index_of):
    specs = []
    for arr, off, width in xs:
        assert off % width == 0 and off + width <= arr.shape[1]
        specs.append(pl.BlockSpec((t, width), functools.partial(lambda j, cb: (index_of(j), cb), cb=off // width)))
    return specs


def _halo_specs(xs, halo_idx, t, index_of):
    specs = []
    for xi in halo_idx:
        _, off, width = xs[xi]
        specs.append(pl.BlockSpec(
            (HALO, width),
            functools.partial(lambda j, cb: (jnp.maximum(index_of(j) * (t // HALO) - 1, 0), cb), cb=off // width)))
    return specs


def _full_spec(a):
    return pl.BlockSpec(a.shape, functools.partial(lambda j, nd: (0,) * nd, nd=a.ndim))


def _chunk_fwd(f, name, t, xs, params, outs, halo_idx=(), carry_shapes=(), save_carries=False, final_carries=False):
    s = xs[0][0].shape[0]
    n = s // t
    nx, nh, npar, no, nc = len(xs), len(halo_idx), len(params), len(outs), len(carry_shapes)
    ns = nc if save_carries else 0
    nf = nc if final_carries else 0

    def body(*refs):
        x_refs, refs = refs[:nx], refs[nx:]
        h_refs, refs = refs[:nh], refs[nh:]
        p_refs, refs = refs[:npar], refs[npar:]
        y_refs, refs = refs[:no], refs[no:]
        s_refs, refs = refs[:ns], refs[ns:]
        f_refs, c_refs = refs[:nf], refs[nf:]
        i = pl.program_id(0)

        @pl.when(i == 0)
        def _():
            for c in c_refs:
                c[...] = jnp.zeros_like(c)

        carries = tuple(c[...] for c in c_refs)
        for s_ref, c in zip(s_refs, carries):
            s_ref[0] = c
        halos = tuple(jnp.where(i > 0, h[...].astype(F32), 0.0) for h in h_refs)
        new_c, ys = f(carries, halos, tuple(x[...].astype(F32) for x in x_refs), tuple(p[...] for p in p_refs))
        for y_ref, y in zip(y_refs, ys):
            y_ref[...] = y.astype(y_ref.dtype)
        for c, v in zip(c_refs, new_c):
            c[...] = v
        for f_ref, v in zip(f_refs, new_c):
            f_ref[...] = v

    ident = lambda j: j
    out_shape = [jax.ShapeDtypeStruct((s, w), dt) for w, dt in outs]
    out_specs = [pl.BlockSpec((t, w), lambda j: (j, 0)) for w, _ in outs]
    if save_carries:
        out_shape += [jax.ShapeDtypeStruct((n,) + tuple(cs), F32) for cs in carry_shapes]
        out_specs += [pl.BlockSpec((1,) + tuple(cs), lambda j: (j, 0, 0)) for cs in carry_shapes]
    if final_carries:
        out_shape += [jax.ShapeDtypeStruct(tuple(cs), F32) for cs in carry_shapes]
        out_specs += [pl.BlockSpec(tuple(cs), lambda j: (0, 0)) for cs in carry_shapes]
    res = pl.pallas_call(
        body, name=name, grid=(n,),
        in_specs=_x_specs(xs, t, ident) + _halo_specs(xs, halo_idx, t, ident) + [_full_spec(p) for p in params],
        out_specs=out_specs, out_shape=out_shape,
        scratch_shapes=[pltpu.VMEM(tuple(cs), F32) for cs in carry_shapes],
        compiler_params=_cparams("arbitrary"),
    )(*[x[0] for x in xs], *[xs[xi][0] for xi in halo_idx], *params)
    return res[:no], res[no:no + ns], res[no + ns:]


def _chunk_bwd(f, name, t, xs, params, dys, dx_dtypes, halo_idx=(), saved=(), carry_seed=None):
    s = xs[0][0].shape[0]
    n = s // t
    nx, nh, npar, nc, ndy = len(xs), len(halo_idx), len(params), len(saved), len(dys)

    def body(*refs):
        x_refs, refs = refs[:nx], refs[nx:]
        h_refs, refs = refs[:nh], refs[nh:]
        p_refs, refs = refs[:npar], refs[npar:]
        s_refs, refs = refs[:nc], refs[nc:]
        dy_refs, refs = refs[:ndy], refs[ndy:]
        dx_refs, refs = refs[:nx], refs[nx:]
        dp_refs, refs = refs[:npar], refs[npar:]
        dc_refs, dh_refs = refs[:nc], refs[nc:]
        j = pl.program_id(0)
        i = n - 1 - j

        @pl.when(j == 0)
        def _():
            for dc in dc_refs:
                dc[...] = jnp.zeros_like(dc) if carry_seed is None else carry_seed(dc.shape)
            for r in dh_refs + dp_refs:
                r[...] = jnp.zeros_like(r)

        carries = tuple(s_ref[0] for s_ref in s_refs)
        halos = tuple(jnp.where(i > 0, h[...].astype(F32), 0.0) for h in h_refs)
        x_vals = tuple(x[...].astype(F32) for x in x_refs)
        p_vals = tuple(p[...] for p in p_refs)
        _, vjp = jax.vjp(f, carries, halos, x_vals, p_vals)
        d_car, d_hal, d_xs, d_par = vjp((tuple(dc[...] for dc in dc_refs), tuple(d[...].astype(F32) for d in dy_refs)))
        d_xs = list(d_xs)
        for k, xi in enumerate(halo_idx):
            w = xs[xi][2]
            d_xs[xi] = d_xs[xi] + jnp.concatenate([jnp.zeros((t - HALO, w), F32), dh_refs[k][...]], axis=0)
            dh_refs[k][...] = jnp.where(i > 0, d_hal[k], 0.0)
        for dx_ref, dx in zip(dx_refs, d_xs):
            dx_ref[...] = dx.astype(dx_ref.dtype)
        for dp_ref, dp in zip(dp_refs, d_par):
            dp_ref[...] += dp
        for dc, v in zip(dc_refs, d_car):
            dc[...] = v

    rev = lambda j: n - 1 - j
    in_specs = (_x_specs(xs, t, rev) + _halo_specs(xs, halo_idx, t, rev) + [_full_spec(p) for p in params]
                + [pl.BlockSpec((1,) + a.shape[1:], lambda j: (n - 1 - j, 0, 0)) for a in saved]
                + [pl.BlockSpec((t, d.shape[1]), lambda j: (n - 1 - j, 0)) for d in dys])
    out_shape = ([jax.ShapeDtypeStruct((s, w), dt) for (_, _, w), dt in zip(xs, dx_dtypes)]
                 + [jax.ShapeDtypeStruct(p.shape, F32) for p in params])
    out_specs = ([pl.BlockSpec((t, w), lambda j: (n - 1 - j, 0)) for _, _, w in xs] + [_full_spec(p) for p in params])
    res = pl.pallas_call(
        body, name=name, grid=(n,), in_specs=in_specs, out_specs=out_specs, out_shape=out_shape,
        scratch_shapes=([pltpu.VMEM(a.shape[1:], F32) for a in saved]
                        + [pltpu.VMEM((HALO, xs[xi][2]), F32) for xi in halo_idx]),
        compiler_params=_cparams("arbitrary"),
    )(*[x[0] for x in xs], *[xs[xi][0] for xi in halo_idx], *params, *saved, *dys)
    return res[:nx], res[nx:]


def _tile(dim, pref):
    for cand in pref:
        if dim % cand == 0:
            return cand
    return dim


def _matmul(a, b, name, ta=False, tb=False, outs=(F32,), epilogue=None, extras=()):
    m, k = (a.shape[1], a.shape[0]) if ta else a.shape
    n = b.shape[0] if tb else b.shape[1]
    tm = _tile(m, (1024, 1152, 512, 256, 128))
    tn = _tile(n, (1152, 1024, 512, 256, 128))
    tk = _tile(k, (1024, 1152, 512, 256, 128))
    nk = k // tk
    ne, no = len(extras), len(outs)
    ca, cb = (0 if ta else 1), (1 if tb else 0)

    def body(*refs):
        a_ref, b_ref = refs[:2]
        e_refs = refs[2:2 + ne]
        o_refs = refs[2 + ne:2 + ne + no]
        acc = refs[-1]
        kk = pl.program_id(2)

        @pl.when(kk == 0)
        def _():
            acc[...] = jnp.zeros_like(acc)

        acc[...] += _dg(_bf(a_ref[...]), _bf(b_ref[...]), ca, cb)

        @pl.when(kk == nk - 1)
        def _():
            res = acc[...]
            vals = (res,) if epilogue is None else epilogue(res, *[e[...] for e in e_refs])
            for o_ref, v in zip(o_refs, vals):
                o_ref[...] = v.astype(o_ref.dtype)

    a_spec = pl.BlockSpec((tk, tm), lambda i, j, kk: (kk, i)) if ta else pl.BlockSpec((tm, tk), lambda i, j, kk: (i, kk))
    b_spec = pl.BlockSpec((tn, tk), lambda i, j, kk: (j, kk)) if tb else pl.BlockSpec((tk, tn), lambda i, j, kk: (kk, j))
    mn_spec = pl.BlockSpec((tm, tn), lambda i, j, kk: (i, j))
    res = pl.pallas_call(
        body, name=name, grid=(m // tm, n // tn, nk),
        in_specs=[a_spec, b_spec] + [mn_spec] * ne,
        out_specs=[mn_spec] * no,
        out_shape=[jax.ShapeDtypeStruct((m, n), dt) for dt in outs],
        scratch_shapes=[pltpu.VMEM((tm, tn), F32)],
        compiler_params=_cparams("parallel", "parallel", "arbitrary"),
    )(a, b, *extras)
    return res if no > 1 else res[0]


def _elementwise_block(r, c):
    if r % 8 == 0 and r >= 8:
        return _tile(r, (256, 128, 64, 32, 16, 8)), c
    return r, _tile(c, (256, 128))


def _adamw_math(g, w, m, v):
    m_new = ADAM_B1 * m + (1.0 - ADAM_B1) * g
    v_new = ADAM_B2 * v + (1.0 - ADAM_B2) * jnp.square(g)
    m_hat = m_new / (1.0 - ADAM_B1 ** ADAM_STEP)
    v_hat = v_new / (1.0 - ADAM_B2 ** ADAM_STEP)
    return -ADAM_LR * (m_hat / (jnp.sqrt(v_hat) + ADAM_EPS) + ADAM_WD * w), m_new, v_new


def _adamw(parts, w, m, v, name):
    nl, r, c = w.shape
    k = parts[0].shape[0]
    tr = _tile(r, (128, 64, 32, 16, 8))
    nb = r // tr

    def body(*refs):
        p_refs, (w_ref, m_ref, v_ref), outs = refs[:nl], refs[nl:nl + 3], refs[nl + 3:]
        layer = pl.program_id(0)
        for q in range(nl):
            @pl.when(layer == q)
            def _(q=q):
                g = p_refs[q][0]
                for j in range(1, k):
                    g = g + p_refs[q][j]
                vals = (g,) + _adamw_math(g, w_ref[0], m_ref[0], v_ref[0])
                for o_ref, val in zip(outs, vals):
                    o_ref[0] = val

    spec = pl.BlockSpec((1, tr, c), lambda l, i: (l, i, 0))
    part_specs = [pl.BlockSpec((k, tr, c), functools.partial(
        lambda l, i, q: (0, jnp.where(l == q, i, jnp.where(l < q, 0, nb - 1)), 0), q=q)) for q in range(nl)]
    return pl.pallas_call(
        body, name=name, grid=(nl, nb), in_specs=part_specs + [spec] * 3,
        out_specs=[spec] * 4, out_shape=[jax.ShapeDtypeStruct((nl, r, c), F32)] * 4,
        compiler_params=_cparams("arbitrary", "arbitrary"),
    )(*parts, w, m, v)


def _adamw_small(gathered, ws, ms, vs, loss_terms, name):
    n = len(ws)

    def device_sum(ref):
        s = ref[0]
        for j in range(1, N_DEV):
            s = s + ref[j]
        return s

    def body(*refs):
        g_refs, w_refs, m_refs, v_refs = refs[:n], refs[n:2 * n], refs[2 * n:3 * n], refs[3 * n:4 * n]
        loss_ref, outs, loss_out = refs[4 * n], refs[4 * n + 1:-1], refs[-1]
        for i in range(n):
            g = device_sum(g_refs[i])
            vals = (g,) + _adamw_math(g, w_refs[i][...], m_refs[i][...], v_refs[i][...])
            for kind, val in enumerate(vals):
                outs[kind * n + i][...] = val
        loss_out[...] = device_sum(loss_ref)

    res = pl.pallas_call(
        body, name=name,
        out_shape=[jax.ShapeDtypeStruct(a.shape, F32) for _ in range(4) for a in ws] + [
            jax.ShapeDtypeStruct(loss_terms.shape[1:], F32)],
        compiler_params=pltpu.CompilerParams(vmem_limit_bytes=VMEM_LIMIT),
    )(*gathered, *ws, *ms, *vs, loss_terms)
    return [res[kind * n:(kind + 1) * n] for kind in range(4)], res[-1]


ANY = pl.BlockSpec(memory_space=pl.ANY)


def _place():
    return lax.axis_index("x"), lax.axis_index("y"), lax.axis_index("c")


def _handshake(peers):
    barrier = pltpu.get_barrier_semaphore()
    for peer in peers:
        pl.semaphore_signal(barrier, inc=1, device_id=peer, device_id_type=MESH)
    pl.semaphore_wait(barrier, len(peers))


def _comm_call(body, name, inputs, out_shape, scratch, sequencer_id=None, after=()):
    if sequencer_id is None:
        return pl.pallas_call(body, name=name, out_shape=out_shape, in_specs=[ANY] * len(inputs),
                              out_specs=[ANY] * len(out_shape), scratch_shapes=scratch)(*inputs)
    n_in, n_after = len(inputs), len(after)

    def sequencer_body(*refs):
        body(*refs[:n_in], *refs[n_in + n_after:])

    return pl.kernel(
        sequencer_body, out_type=out_shape, mesh=plsc.ScalarSubcoreMesh(axis_name="sequencer", num_cores=1),
        scratch_types=scratch, compiler_params=pltpu.CompilerParams(collective_id=sequencer_id), name=name,
    )(*inputs, *after)


def _all_gather(blocks, name, sequencer_id=None, after=()):
    n = len(blocks)

    def body(*refs):
        x_refs, out_refs = refs[:n], refs[n:2 * n]
        send_sems, recv_sems, local_sems = refs[2 * n:]
        x, y, c = _place()
        me, sibling = (x, y, c), (x, y, 1 - c)
        chips = [(1 - x, y), (x, 1 - y), (1 - x, 1 - y)]
        if sequencer_id is not None:
            _handshake([sibling] + [(*chip, c) for chip in chips])

        def slot(a, px, py, pc):
            return out_refs[a].at[4 * px + 2 * py + pc]

        def copy(a, k, blk, to, src=None):
            return pltpu.make_async_remote_copy(
                src_ref=slot(a, *blk) if src is None else src, dst_ref=slot(a, *blk),
                send_sem=send_sems.at[7 * a + k], recv_sem=recv_sems.at[7 * a + k], device_id=to, device_id_type=MESH)

        mine = [pltpu.make_async_copy(x_refs[a], slot(a, *me), local_sems.at[a]) for a in range(n)]
        first = []
        for a in range(n):
            mine[a].start()
            first.append(copy(a, 0, me, sibling, src=x_refs[a]))
            first += [copy(a, 1 + j, me, (*chip, c), src=x_refs[a]) for j, chip in enumerate(chips)]
        for cp in first:
            cp.start()
        passed = []
        for j, chip in enumerate(chips):
            for a in range(n):
                copy(a, 1 + j, (*chip, c), me).wait_recv()
                passed.append(copy(a, 4 + j, (*chip, c), sibling))
                passed[-1].start()
        for a in range(n):
            copy(a, 0, sibling, me).wait_recv()
            for j, chip in enumerate(chips):
                copy(a, 4 + j, (*chip, 1 - c), me).wait_recv()
        for cp in first + passed:
            cp.wait_send()
        for cp in mine:
            cp.wait()

    return _comm_call(
        body, name, blocks, [jax.ShapeDtypeStruct((N_DEV,) + b.shape, b.dtype) for b in blocks],
        [pltpu.SemaphoreType.DMA((7 * n,)), pltpu.SemaphoreType.DMA((7 * n,)), pltpu.SemaphoreType.DMA((n,))],
        sequencer_id, after)


def _exchange_sibling(gs, name, sequencer_id=None, after=()):
    n = len(gs)

    def body(*refs):
        g_refs, out_refs = refs[:n], refs[n:2 * n]
        send_sems, recv_sems = refs[2 * n:]
        x, y, c = _place()
        if sequencer_id is not None:
            _handshake([(x, y, 1 - c)])
        copies = [pltpu.make_async_remote_copy(
            src_ref=g_refs[a].at[2 * k + 1 - c], dst_ref=out_refs[a].at[k], send_sem=send_sems.at[4 * a + k],
            recv_sem=recv_sems.at[4 * a + k], device_id=(x, y, 1 - c), device_id_type=MESH)
            for a in range(n) for k in range(4)]
        for cp in copies:
            cp.start()
        for cp in copies:
            cp.wait()

    return _comm_call(body, name, gs, [jax.ShapeDtypeStruct((4,) + g.shape[1:], g.dtype) for g in gs],
                      [pltpu.SemaphoreType.DMA((4 * n,)), pltpu.SemaphoreType.DMA((4 * n,))], sequencer_id, after)


def _add_sibling(g, r1, name, after=()):
    _, r, w = g.shape
    tr, tc = _elementwise_block(r, w)
    core = jnp.reshape(lax.axis_index("c"), (1,)).astype(jnp.int32)

    def body(c_ref, g_ref, r_ref, *rest):
        o_ref, wire_ref = rest[len(after):]
        s = g_ref[...] + r_ref[...]
        o_ref[...] = s
        wire_ref[...] = s.astype(BF16)

    spec = pl.BlockSpec((1, tr, tc), lambda k, i, j, c_ref: (k, i, j))
    return pl.pallas_call(
        body, name=name, out_shape=[jax.ShapeDtypeStruct((4, r, w), F32), jax.ShapeDtypeStruct((4, r, w), BF16)],
        grid_spec=pltpu.PrefetchScalarGridSpec(
            num_scalar_prefetch=1, grid=(4, r // tr, w // tc),
            in_specs=[pl.BlockSpec((1, tr, tc), lambda k, i, j, c_ref: (2 * k + c_ref[0], i, j)), spec]
            + [ANY] * len(after),
            out_specs=[spec, spec]),
        compiler_params=_cparams("parallel", "parallel", "parallel"),
    )(core, g, r1, *after)


def _exchange_chips(ps, name, sequencer_id=None, after=()):
    n = len(ps)

    def body(*refs):
        p_refs, out_refs = refs[:n], refs[n:2 * n]
        send_sems, recv_sems = refs[2 * n:]
        x, y, c = _place()
        chips = [(1 - x, y), (x, 1 - y), (1 - x, 1 - y)]
        if sequencer_id is not None:
            _handshake([(*chip, c) for chip in chips])
        copies = [pltpu.make_async_remote_copy(
            src_ref=p_refs[a].at[2 * px + py], dst_ref=out_refs[a].at[j], send_sem=send_sems.at[3 * a + j],
            recv_sem=recv_sems.at[3 * a + j], device_id=(px, py, c), device_id_type=MESH)
            for a in range(n) for j, (px, py) in enumerate(chips)]
        for cp in copies:
            cp.start()
        for cp in copies:
            cp.wait()

    return _comm_call(body, name, ps, [jax.ShapeDtypeStruct((3,) + p.shape[1:], p.dtype) for p in ps],
                      [pltpu.SemaphoreType.DMA((3 * n,)), pltpu.SemaphoreType.DMA((3 * n,))], sequencer_id, after)


def _sum_chips(own, others, name):
    _, r, c = own.shape
    tr, tc = _elementwise_block(r, c)
    chip = jnp.reshape(2 * lax.axis_index("x") + lax.axis_index("y"), (1,)).astype(jnp.int32)

    def body(chip_ref, own_ref, others_ref, o_ref):
        g = own_ref[0]
        for j in range(3):
            g = g + others_ref[j].astype(F32)
        o_ref[...] = g

    return pl.pallas_call(
        body, name=name, out_shape=jax.ShapeDtypeStruct((r, c), F32),
        grid_spec=pltpu.PrefetchScalarGridSpec(
            num_scalar_prefetch=1, grid=(r // tr, c // tc),
            in_specs=[pl.BlockSpec((1, tr, tc), lambda i, j, chip_ref: (chip_ref[0], i, j)),
                      pl.BlockSpec((3, tr, tc), lambda i, j, chip_ref: (0, i, j))],
            out_specs=pl.BlockSpec((tr, tc), lambda i, j, chip_ref: (i, j))),
        compiler_params=_cparams("parallel", "parallel"),
    )(chip, own, others)


def _reorder_in_proj(wt):
    za_zb, zc, xbc, dt, gates = (wt[:4096], wt[4096:5120], wt[5120:7168], wt[7168:7184], wt[7184:])
    return jnp.concatenate([za_zb, xbc, gates, zc, dt, jnp.zeros((DT_PAD - 16, wt.shape[1]), wt.dtype)], axis=0)


def _restore_in_proj(wt):
    return jnp.concatenate([wt[:4096], wt[OFF_ZC:OFF_ZC + W_ZC], wt[OFF_XBC:OFF_XBC + W_XBC],
                            wt[OFF_DT:OFF_DT + 16], wt[OFF_GATE:OFF_GATE + W_GATE]], axis=0)


def _lanes_from_devices(g):
    return jnp.moveaxis(g, 0, 1).reshape(g.shape[1], N_DEV * g.shape[2])


def _lanes_to_devices(a):
    return jnp.moveaxis(a.reshape(a.shape[0], N_DEV, a.shape[1] // N_DEV), 1, 0)


def _pad_lanes(a, width):
    return jnp.pad(a, ((0, 0), (0, width - a.shape[1])))


BIG = ("w_in", "w_branch_a", "w_branch_b", "w_branch_c", "w_out", "w_mlp_up", "w_mlp_down")
SMALL_SHARDED = ("b_gate", "lru_conv_w", "ssd_conv_w")
REPLICATED = ("norm_mix_g", "gmlp_ln_g", "gmlp_ln_b", "gmlp_w_s", "gmlp_b_s", "lru_conv_b", "lru_w_r", "lru_b_r",
              "lru_w_i", "lru_b_i", "lru_lambda", "ssd_conv_b", "ssd_dt_bias", "ssd_a_log", "ssd_d", "ssd_norm_g",
              "norm_mlp_g", "final_norm_g")
WEIGHTS = ("norm_mix_g", "w_in", "b_gate", "gmlp_ln_g", "gmlp_ln_b", "gmlp_w_s", "gmlp_b_s", "lru_conv_w", "lru_conv_b",
           "lru_w_r", "lru_b_r", "lru_w_i", "lru_b_i", "lru_lambda", "ssd_conv_w", "ssd_conv_b", "ssd_dt_bias",
           "ssd_a_log", "ssd_d", "ssd_norm_g", "w_branch_a", "w_branch_b", "w_branch_c", "w_out", "norm_mlp_g",
           "w_mlp_up", "w_mlp_down", "final_norm_g")
TRANSPOSED = ("w_in", "w_mlp_up")
SMALL_MATRICES = ("gmlp_w_s", "lru_w_r", "lru_w_i")
SMALL_VECTORS = tuple(n for n in REPLICATED if n not in SMALL_MATRICES)
GRADIENT_GROUPS = {"mlp": ("w_mlp_up", "w_mlp_down"), "mix": ("w_branch_a", "w_branch_b", "w_branch_c", "w_out"),
                   "in": ("w_in",)}


def _layer_params(full, l):
    row = lambda a: a.reshape(1, -1)
    return dict(
        norm_mix_g=row(full["norm_mix_g"][l]), norm_mlp_g=row(full["norm_mlp_g"][l]),
        gmlp=(row(full["gmlp_ln_g"][l]), row(full["gmlp_ln_b"][l]), full["gmlp_w_s"][l].reshape(GROUPS_A * CHUNK, CHUNK),
              full["gmlp_b_s"][l].T),
        lru=(full["lru_conv_w"][l], row(full["lru_conv_b"][l]), full["lru_w_r"][l].reshape(D, D // HEADS_B),
             row(full["lru_b_r"][l]), full["lru_w_i"][l].reshape(D, D // HEADS_B), row(full["lru_b_i"][l]),
             row(full["lru_lambda"][l])),
        ssd=(full["ssd_conv_w"][l], row(full["ssd_conv_b"][l]), _pad_lanes(row(full["ssd_dt_bias"][l]), DT_PAD),
             _pad_lanes(row(full["ssd_a_log"][l]), DT_PAD), _pad_lanes(row(full["ssd_d"][l]), DT_PAD),
             row(full["ssd_norm_g"][l])),
        b_gate=full["b_gate"][l],
    )


def _forward_layer(h, p, wb, l, after_mixers=None):
    tag = f"l{l}"
    t_row = 512
    (hn,), _, _ = _chunk_fwd(_f_rmsnorm, f"norm_mix_{tag}", t_row, [(h, 0, D)], [p["norm_mix_g"]], [(D, BF16)])
    proj = _matmul(hn, wb["w_in"], f"in_proj_{tag}", tb=True)
    (ya,), _, _ = _chunk_fwd(_f_gmlp, f"gmlp_{tag}", CHUNK, [(proj, OFF_ZA, W_ZA)], p["gmlp"], [(D, BF16)])
    lru_xs = [(proj, OFF_ZB, D), (proj, OFF_ZB + D, D)]
    (yb,), lru_saved, _ = _chunk_fwd(_f_lru, f"lru_{tag}", CHUNK, lru_xs, p["lru"], [(D, BF16)], halo_idx=(0,),
                                     carry_shapes=[(1, D)], save_carries=True)
    ssd_xs = [(proj, OFF_ZC, W_ZC), (proj, OFF_XBC, W_XBC), (proj, OFF_DT, W_DT)]
    (yc,), ssd_saved, _ = _chunk_fwd(_f_ssd, f"ssd_{tag}", CHUNK, ssd_xs, p["ssd"], [(D, BF16)], halo_idx=(1,),
                                     carry_shapes=[(HEADS_C * HEAD_DIM_C, STATE_C)], save_carries=True)
    if after_mixers is not None:
        after_mixers(yc)
    pa = _matmul(ya, wb["w_branch_a"], f"branch_a_{tag}")
    pb = _matmul(yb, wb["w_branch_b"], f"branch_b_{tag}")
    pc = _matmul(yc, wb["w_branch_c"], f"branch_c_{tag}")
    merge_xs = [(pa, 0, D), (pb, 0, D), (pc, 0, D), (proj, OFF_GATE, W_GATE)]
    (merged,), _, _ = _chunk_fwd(_f_merge, f"merge_{tag}", t_row, merge_xs, [p["b_gate"]], [(D, BF16)])
    h_mid = _matmul(merged, wb["w_out"], f"out_proj_{tag}", epilogue=lambda acc, res: (acc + res,), extras=(h,))
    (hn2,), _, _ = _chunk_fwd(_f_rmsnorm, f"norm_mlp_{tag}", t_row, [(h_mid, 0, D)], [p["norm_mlp_g"]], [(D, BF16)])

    def relu_sq(acc):
        r = jnp.maximum(acc, 0.0)
        return r, r * r

    relu_up, act = _matmul(hn2, wb["w_mlp_up"], f"mlp_up_{tag}", tb=True, outs=(F32, BF16), epilogue=relu_sq)
    h_out = _matmul(act, wb["w_mlp_down"], f"mlp_down_{tag}", epilogue=lambda acc, res: (acc + res,), extras=(h_mid,))
    saved = dict(h=h, hn=hn, proj=proj, ya=ya, yb=yb, yc=yc, lru_saved=lru_saved, ssd_saved=ssd_saved, pa=pa, pb=pb,
                 pc=pc, merged=merged, h_mid=h_mid, hn2=hn2, relu_up=relu_up, act=act, lru_xs=lru_xs, ssd_xs=ssd_xs,
                 merge_xs=merge_xs)
    return h_out, saved


def _backward_layer(dh, sv, p, wb, l):
    tag = f"l{l}"
    t_row = 512
    g = {}
    d_up = _matmul(dh, wb["w_mlp_down"], f"d_act_{tag}", tb=True, outs=(BF16,),
                   epilogue=lambda acc, r: (acc * (2.0 * r),), extras=(sv["relu_up"],))
    g["w_mlp_down"] = _matmul(sv["act"], dh, f"dw_mlp_down_{tag}", ta=True)
    g["w_mlp_up"] = _matmul(d_up, sv["hn2"], f"dw_mlp_up_{tag}", ta=True)
    d_hn2 = _matmul(d_up, wb["w_mlp_up"], f"d_hn2_{tag}")
    (d_mid,), (g["norm_mlp_g"],) = _chunk_bwd(_f_rmsnorm_res, f"norm_mlp_bwd_{tag}", t_row, [(sv["h_mid"], 0, D)],
                                              [p["norm_mlp_g"]], [d_hn2, dh], [F32])
    d_merged = _matmul(d_mid, wb["w_out"], f"d_merged_{tag}", tb=True)
    g["w_out"] = _matmul(sv["merged"], d_mid, f"dw_out_{tag}", ta=True)
    (d_pa, d_pb, d_pc, d_gate), (g["b_gate"],) = _chunk_bwd(
        _f_merge, f"merge_bwd_{tag}", t_row, sv["merge_xs"], [p["b_gate"]], [d_merged], [BF16] * 4)
    d_y = {}
    for br, d_p, y in (("a", d_pa, sv["ya"]), ("b", d_pb, sv["yb"]), ("c", d_pc, sv["yc"])):
        g[f"w_branch_{br}"] = _matmul(y, d_p, f"dw_branch_{br}_{tag}", ta=True)
        d_y[br] = _matmul(d_p, wb[f"w_branch_{br}"], f"d_y{br}_{tag}", tb=True)
    (d_za,), g_gmlp = _chunk_bwd(_f_gmlp, f"gmlp_bwd_{tag}", CHUNK, [(sv["proj"], OFF_ZA, W_ZA)], p["gmlp"],
                                 [d_y["a"]], [BF16])
    (d_xb, d_gt), g_lru = _chunk_bwd(_f_lru, f"lru_bwd_{tag}", CHUNK, sv["lru_xs"], p["lru"], [d_y["b"]], [BF16] * 2,
                                     halo_idx=(0,), saved=sv["lru_saved"])
    (d_zc, d_xbc, d_dt), g_ssd = _chunk_bwd(_f_ssd, f"ssd_bwd_{tag}", CHUNK, sv["ssd_xs"], p["ssd"], [d_y["c"]],
                                            [BF16] * 3, halo_idx=(1,), saved=sv["ssd_saved"])
    d_proj = jnp.concatenate([d_za, d_xb, d_gt, d_xbc, d_gate, d_zc, d_dt], axis=1)
    g["w_in"] = _matmul(d_proj, sv["hn"], f"dw_in_{tag}", ta=True)
    d_hn = _matmul(d_proj, wb["w_in"], f"d_hn_{tag}")
    (d_h,), (g["norm_mix_g"],) = _chunk_bwd(_f_rmsnorm_res, f"norm_mix_bwd_{tag}", t_row, [(sv["h"], 0, D)],
                                            [p["norm_mix_g"]], [d_hn, d_mid], [F32])
    g["w_in"] = _restore_in_proj(g["w_in"])
    for n in BIG:
        g[n] = g[n].reshape(N_DEV, g[n].shape[0] // N_DEV, g[n].shape[1])
    g["gmlp_ln_g"], g["gmlp_ln_b"], g["gmlp_w_s"] = g_gmlp[:3]
    g["gmlp_b_s"] = g_gmlp[3].T
    (g["lru_conv_w"], g["lru_conv_b"], g["lru_w_r"], g["lru_b_r"], g["lru_w_i"], g["lru_b_i"], g["lru_lambda"]) = g_lru
    g["ssd_conv_w"], g["ssd_conv_b"] = g_ssd[:2]
    g["ssd_dt_bias"], g["ssd_a_log"], g["ssd_d"] = (a[:, :HEADS_C] for a in g_ssd[2:5])
    g["ssd_norm_g"] = g_ssd[5]
    g["later"] = {"mlp": d_merged, "mix": d_za, "in": d_h}
    return d_h, g


LOSS_ROWS = 512


def _loss_and_grads(h, target, full, layer_weights, first_gathered):
    seq = h.shape[0]
    layer_p = [_layer_params(full, l) for l in range(DEPTH)]
    layer_w = [layer_weights(0, (first_gathered,))]
    saved = []
    for l in range(DEPTH):
        fetch_next = None
        if l + 1 < DEPTH:
            fetch_next = lambda y, l=l: layer_w.append(layer_weights(l + 1, (layer_w[l]["w_mlp_down"], y)))
        h, sv = _forward_layer(h, layer_p[l], layer_w[l], l, fetch_next)
        saved.append(sv)
    final_g = full["final_norm_g"].reshape(1, D)
    loss_xs = [(h, 0, D), (target, 0, D)]
    t_loss = min(LOSS_ROWS, seq)
    _, _, (loss_acc,) = _chunk_fwd(_f_loss, "loss", t_loss, loss_xs, [final_g], [], carry_shapes=[(1, 128)],
                                   final_carries=True)
    zero_acc = jnp.zeros((seq // t_loss, 1, 128), F32)
    seed = lambda shape: (lax.broadcasted_iota(jnp.int32, shape, 1) == 0).astype(F32)
    (dh, _), (g_final,) = _chunk_bwd(_f_loss, "loss_bwd", t_loss, loss_xs, [final_g], [], [F32, F32], saved=[zero_acc],
                                     carry_seed=seed)
    layer_g = [None] * DEPTH
    for l in reversed(range(DEPTH)):
        dh, layer_g[l] = _backward_layer(dh, saved[l], layer_p[l], layer_w[l], l)
    return loss_acc, dh, layer_g, g_final


def _small_views(d):
    views = {n: d[n] for n in REPLICATED}
    views["gmlp_b_s"] = d["gmlp_b_s"].reshape(DEPTH * GROUPS_A, CHUNK)
    views["final_norm_g"] = d["final_norm_g"].reshape(1, D)
    for n in SMALL_MATRICES:
        views[n] = d[n].reshape(DEPTH * D, D // HEADS_B)
    return views


def kernel(x, norm_mix_g, w_in, b_gate, gmlp_ln_g, gmlp_ln_b, gmlp_w_s, gmlp_b_s, lru_conv_w, lru_conv_b, lru_w_r, lru_b_r, lru_w_i, lru_b_i, lru_lambda, ssd_conv_w, ssd_conv_b, ssd_dt_bias, ssd_a_log, ssd_d, ssd_norm_g, w_branch_a, w_branch_b, w_branch_c, w_out, norm_mlp_g, w_mlp_up, w_mlp_down, final_norm_g, loss_target, m_norm_mix_g, m_w_in, m_b_gate, m_gmlp_ln_g, m_gmlp_ln_b, m_gmlp_w_s, m_gmlp_b_s, m_lru_conv_w, m_lru_conv_b, m_lru_w_r, m_lru_b_r, m_lru_w_i, m_lru_b_i, m_lru_lambda, m_ssd_conv_w, m_ssd_conv_b, m_ssd_dt_bias, m_ssd_a_log, m_ssd_d, m_ssd_norm_g, m_w_branch_a, m_w_branch_b, m_w_branch_c, m_w_out, m_norm_mlp_g, m_w_mlp_up, m_w_mlp_down, m_final_norm_g, v_norm_mix_g, v_w_in, v_b_gate, v_gmlp_ln_g, v_gmlp_ln_b, v_gmlp_w_s, v_gmlp_b_s, v_lru_conv_w, v_lru_conv_b, v_lru_w_r, v_lru_b_r, v_lru_w_i, v_lru_b_i, v_lru_lambda, v_ssd_conv_w, v_ssd_conv_b, v_ssd_dt_bias, v_ssd_a_log, v_ssd_d, v_ssd_norm_g, v_w_branch_a, v_w_branch_b, v_w_branch_c, v_w_out, v_norm_mlp_g, v_w_mlp_up, v_w_mlp_down, v_final_norm_g):
    args = locals()
    w = {n: args[n] for n in WEIGHTS}
    m = {n: args["m_" + n] for n in WEIGHTS}
    v = {n: args["v_" + n] for n in WEIGHTS}
    seq = x.shape[1]
    h = x.reshape(seq, D)
    target = loss_target.reshape(seq, D)

    def shard_on_wire(n, l):
        return (w[n][l].T if n in TRANSPOSED else w[n][l]).astype(BF16)

    first = _all_gather([shard_on_wire("w_in", 0)] + [w[n] for n in SMALL_SHARDED], "gather_weights_first")
    full = {n: w[n] for n in REPLICATED}
    for n, g in zip(SMALL_SHARDED, first[1:]):
        full[n] = jnp.stack([_lanes_from_devices(g[:, l]) for l in range(DEPTH)])

    def layer_weights(l, after):
        have = {"w_in": first[0]} if l == 0 else {}
        names = [n for n in BIG if n not in have]
        later = _all_gather([shard_on_wire(n, l) for n in names], f"gather_weights_l{l}", SEQ_GATHER + l, after=after)
        have.update(zip(names, later))
        wl = {n: have[n].reshape(-1, D) for n in BIG}
        wl["w_in"] = _reorder_in_proj(wl["w_in"])
        return wl

    loss_local, dh, layer_g, g_final = _loss_and_grads(h, target, full, layer_weights, first[0])
    grad_x = dh.reshape(x.shape)
    out = {}
    kinds = ("grad", "delta", "new_m", "new_v")

    sequencer_before = {}

    def reduce_scatter(slabs, tag, on_sequencer, later=()):
        keys = list(slabs)
        ids = (SEQ_TO_SIBLING, SEQ_TO_CHIPS) if on_sequencer else (None, None)
        from_sibling = _exchange_sibling([slabs[k] for k in keys], f"grads_to_sibling_{tag}", ids[0],
                                         sequencer_before.get("sibling", ()))
        chip_sums = [_add_sibling(slabs[k], r, f"add_sibling_{k[0]}_{k[1]}", later) for k, r in zip(keys, from_sibling)]
        from_chips = _exchange_chips([wire for _, wire in chip_sums], f"grads_to_chips_{tag}", ids[1],
                                     sequencer_before.get("chips", ()))
        if on_sequencer:
            sequencer_before["sibling"], sequencer_before["chips"] = (from_sibling[0],), (from_chips[0],)
        return {k: _sum_chips(own, others, f"sum_chips_{k[0]}_{k[1]}")
                for k, (own, _), others in zip(keys, chip_sums, from_chips)}

    reduced = {}
    groups = [(l, grp) for l in range(DEPTH - 1, -1, -1) for grp in GRADIENT_GROUPS]
    for l, grp in groups[:-1]:
        slabs = {(n, l): layer_g[l][n] for n in GRADIENT_GROUPS[grp]}
        reduced.update(reduce_scatter(slabs, f"{grp}_l{l}", True, (layer_g[l]["later"][grp],)))
    l, grp = groups[-1]
    slabs = {(n, l): layer_g[l][n] for n in GRADIENT_GROUPS[grp]}
    for n in SMALL_SHARDED:
        slabs[n, None] = jnp.concatenate([_lanes_to_devices(layer_g[l][n]) for l in range(DEPTH)], axis=1)
    g_small = {n: jnp.concatenate([layer_g[l][n] for l in range(DEPTH)], axis=0) for n in REPLICATED[:-1]}
    g_small["final_norm_g"] = g_final
    for n in SMALL_MATRICES:
        slabs[n, None] = g_small[n].reshape(N_DEV, -1, g_small[n].shape[-1])
    reduced.update(reduce_scatter(slabs, "last", False))
    for n in BIG:
        parts = [(reduced[n, l].T if n in TRANSPOSED else reduced[n, l])[None] for l in range(DEPTH)]
        for kind, a in zip(kinds, _adamw(parts, w[n], m[n], v[n], f"adamw_{n}")):
            out[kind, n] = a
    for n in SMALL_SHARDED:
        one = lambda a: a.reshape((1, -1, a.shape[-1]))
        for kind, a in zip(kinds, _adamw([reduced[n, None][None]], one(w[n]), one(m[n]), one(v[n]), f"adamw_{n}")):
            out[kind, n] = a.reshape(w[n].shape)

    to_gather = [reduced[n, None] if n in SMALL_MATRICES else g_small[n] for n in REPLICATED]
    *g_gathered, loss_terms = _all_gather(to_gather + [loss_local], "gather_small_grads")
    g_all = dict(zip(REPLICATED, g_gathered))
    wv, mv, vv = _small_views(w), _small_views(m), _small_views(v)
    res, loss_sum = _adamw_small([g_all[n] for n in SMALL_VECTORS],
                                 *[[d[n] for n in SMALL_VECTORS] for d in (wv, mv, vv)], loss_terms, "adamw_vectors")
    loss = loss_sum[0, 0]
    for kind, arrays in zip(kinds, res):
        for n, a in zip(SMALL_VECTORS, arrays):
            out[kind, n] = a.reshape(w[n].shape)
    for n in SMALL_MATRICES:
        g_full = g_all[n].reshape((1, 1) + wv[n].shape)
        for kind, a in zip(kinds, _adamw([g_full[0]], wv[n][None], mv[n][None], vv[n][None], f"adamw_{n}")):
            out[kind, n] = a.reshape(w[n].shape)

    return (loss, grad_x, *[out[kind, n] for kind in kinds for n in WEIGHTS])
```

```python
import functools

import jax
import jax.numpy as jnp
from jax import lax
from jax.experimental import pallas as pl
from jax.experimental.pallas import tpu as pltpu
from jax.experimental.pallas import tpu_sc as plsc

F32 = jnp.float32
BF16 = jnp.bfloat16
MESH = pl.DeviceIdType.MESH

D = 1024
DEPTH = 2
EPS = 1e-6
CHUNK = 128
GROUPS_A = 8
HEADS_B = 8
LRU_C = 8.0
HEADS_C = 16
HEAD_DIM_C = 64
GROUPS_C = 4
STATE_C = 128
HIDDEN = 4 * D
DT_PAD = 128
OFF_ZA, W_ZA = 0, 2048
OFF_ZB, W_ZB = 2048, 2048
OFF_XBC, W_XBC = 4096, 2048
OFF_GATE, W_GATE = 6144, 3072
OFF_ZC, W_ZC = 9216, 1024
OFF_DT, W_DT = 10240, DT_PAD
D_IN_PAD = 10368
D_IN = 10256
N_DEV = 8
SEQ_GATHER = 1
SEQ_TO_SIBLING = SEQ_GATHER + DEPTH
SEQ_TO_CHIPS = SEQ_TO_SIBLING + 1

ADAM_LR = 0.001
ADAM_B1 = 0.9
ADAM_B2 = 0.999
ADAM_EPS = 1e-08
ADAM_WD = 0.01
ADAM_STEP = 10

VMEM_LIMIT = 56 * 1024 * 1024
HALO = 8


def _cparams(*sem):
    return pltpu.CompilerParams(dimension_semantics=sem, vmem_limit_bytes=VMEM_LIMIT)


def _bf(x):
    return x.astype(BF16)


def _dg(a, b, ca, cb):
    return lax.dot_general(a, b, (((ca,), (cb,)), ((), ())), preferred_element_type=F32)


@functools.partial(jax.custom_vjp, nondiff_argnums=(2, 3))
def _mm(a, b, ta, tb):
    return _dg(_bf(a), _bf(b), 0 if ta else 1, 1 if tb else 0)


def _mm_fwd(a, b, ta, tb):
    return _mm(a, b, ta, tb), (a, b)


def _mm_bwd(ta, tb, res, g):
    a, b = res
    ma = 1 if ta else 0
    nb = 0 if tb else 1
    gb, ab, bb = _bf(g), _bf(a), _bf(b)
    da = _dg(bb, gb, nb, 1) if ta else _dg(gb, bb, 1, nb)
    db = _dg(gb, ab, 0, ma) if tb else _dg(ab, gb, ma, 0)
    return da.astype(a.dtype), db.astype(b.dtype)


_mm.defvjp(_mm_fwd, _mm_bwd)


@functools.partial(jax.custom_vjp, nondiff_argnums=(1, 2))
def _cols(x, lo, hi):
    return x[:, lo:hi]


def _cols_fwd(x, lo, hi):
    return x[:, lo:hi], x.shape[1]


def _cols_bwd(lo, hi, width, g):
    parts = []
    if lo:
        parts.append(jnp.zeros((g.shape[0], lo), g.dtype))
    parts.append(g)
    if width - hi:
        parts.append(jnp.zeros((g.shape[0], width - hi), g.dtype))
    return (jnp.concatenate(parts, axis=1) if len(parts) > 1 else g,)


_cols.defvjp(_cols_fwd, _cols_bwd)


@functools.partial(jax.custom_vjp, nondiff_argnums=(1, 2))
def _rows(x, lo, hi):
    return x[lo:hi, :]


def _rows_fwd(x, lo, hi):
    return x[lo:hi, :], x.shape[0]


def _rows_bwd(lo, hi, height, g):
    parts = []
    if lo:
        parts.append(jnp.zeros((lo, g.shape[1]), g.dtype))
    parts.append(g)
    if height - hi:
        parts.append(jnp.zeros((height - hi, g.shape[1]), g.dtype))
    return (jnp.concatenate(parts, axis=0) if len(parts) > 1 else g,)


_rows.defvjp(_rows_fwd, _rows_bwd)


def _col(x, j):
    lane = lax.broadcasted_iota(jnp.int32, x.shape, 1)
    return jnp.sum(jnp.where(lane == j, x, 0.0), axis=1, keepdims=True)


def _row(x, i):
    r = lax.broadcasted_iota(jnp.int32, x.shape, 0)
    return jnp.sum(jnp.where(r == i, x, 0.0), axis=0, keepdims=True)


def _roll_down(x, s):
    return pltpu.roll(x, s, 0)


def _roll_up(x, s):
    return pltpu.roll(x, x.shape[0] - s, 0)


def _row_iota(x):
    return lax.broadcasted_iota(jnp.int32, x.shape, 0)


@functools.partial(jax.custom_vjp, nondiff_argnums=(2,))
def _shift_rows(halo, x, s):
    if s == 0:
        return x
    return _roll_down(jnp.concatenate([halo, x], axis=0), s)[HALO:]


def _shift_rows_fwd(halo, x, s):
    return _shift_rows(halo, x, s), None


def _shift_rows_bwd(s, _, g):
    if s == 0:
        return jnp.zeros((HALO, g.shape[1]), g.dtype), g
    ge = jnp.concatenate([jnp.zeros((HALO, g.shape[1]), g.dtype), g], axis=0)
    de = _roll_up(ge, s)
    return de[:HALO], de[HALO:]


_shift_rows.defvjp(_shift_rows_fwd, _shift_rows_bwd)


SUBLANES = 8
LANES = 128


def _scan_tiles(a, b, carry, up):
    n, c = a.shape
    nt = n // SUBLANES
    a = a.reshape(nt, SUBLANES, c)
    b = b.reshape(nt, SUBLANES, c)
    sub = lax.broadcasted_iota(jnp.int32, a.shape, 1)
    s = 1
    while s < SUBLANES:
        keep = (sub < SUBLANES - s) if up else (sub >= s)
        shift = SUBLANES - s if up else s
        a_sh = jnp.where(keep, pltpu.roll(a, shift, 1), 1.0)
        b_sh = jnp.where(keep, pltpu.roll(b, shift, 1), 0.0)
        b = a * b_sh + b
        a = a * a_sh
        s *= 2
    tiles = [None] * nt
    edge = 0 if up else SUBLANES - 1
    for j in (range(nt - 1, -1, -1) if up else range(nt)):
        tiles[j] = b[j] if carry is None else b[j] + a[j] * carry
        carry = tiles[j][edge:edge + 1, :]
    return jnp.concatenate(tiles, axis=0)


@jax.custom_vjp
def _lin_scan(a, b, h0):
    return _scan_tiles(a, b, h0, up=False)


def _lin_scan_fwd(a, b, h0):
    h = _lin_scan(a, b, h0)
    return h, (a, h0, h)


def _lin_scan_bwd(res, g):
    a, h0, h = res
    n = a.shape[0]
    row = _row_iota(a)
    a_next = jnp.where(row < n - 1, _roll_up(a, 1), 0.0)
    gg = _scan_tiles(a_next, g, None, up=True)
    h_prev = jnp.where(row >= 1, _roll_down(h, 1), h0)
    return gg * h_prev, gg, _row(a * gg, 0)


_lin_scan.defvjp(_lin_scan_fwd, _lin_scan_bwd)


@jax.custom_vjp
def _cumsum_rows(x):
    n = x.shape[0]
    row = _row_iota(x)
    s = 1
    while s < n:
        x = x + jnp.where(row >= s, _roll_down(x, s), 0.0)
        s *= 2
    return x


def _cumsum_rows_fwd(x):
    return _cumsum_rows(x), None


def _cumsum_rows_bwd(_, g):
    n = g.shape[0]
    row = _row_iota(g)
    s = 1
    while s < n:
        g = g + jnp.where(row < n - s, _roll_up(g, s), 0.0)
        s *= 2
    return (g,)


_cumsum_rows.defvjp(_cumsum_rows_fwd, _cumsum_rows_bwd)


def _sigmoid(x):
    return jax.nn.sigmoid(x)


def _softplus(x):
    return jnp.maximum(x, 0.0) + jnp.log1p(jnp.exp(-jnp.abs(x)))


def _gelu(x):
    return jax.nn.gelu(x, approximate=True)


def _neg_expm1(x):
    series = -x * (1.0 + x * (0.5 + x * (1.0 / 6.0 + x * (1.0 / 24.0))))
    return jnp.where(x > -0.01, series, 1.0 - jnp.exp(x))


def _rms(x, g):
    return x * lax.rsqrt(jnp.mean(x * x, axis=-1, keepdims=True) + EPS) * g


def _f_rmsnorm(carries, halos, xs, params):
    (h,) = xs
    (g,) = params
    return (), (_rms(h, g),)


def _f_rmsnorm_res(carries, halos, xs, params):
    (h,) = xs
    (g,) = params
    return (), (_rms(h, g), h)


def _f_gmlp(carries, halos, xs, params):
    (za,) = xs
    ln_g, ln_b, w_s, b_st = params
    ga = _gelu(za)
    u = _cols(ga, 0, D)
    v = _cols(ga, D, 2 * D)
    vc = v - jnp.mean(v, axis=-1, keepdims=True)
    vn = vc * lax.rsqrt(jnp.mean(vc * vc, axis=-1, keepdims=True) + EPS) * ln_g + ln_b
    q = CHUNK
    causal = lax.broadcasted_iota(jnp.int32, (q, q), 0) >= lax.broadcasted_iota(jnp.int32, (q, q), 1)
    mixed = []
    for g in range(GROUPS_A):
        w = jnp.where(causal, _rows(w_s, g * q, (g + 1) * q), 0.0)
        mixed.append(_mm(w, _cols(vn, g * q, (g + 1) * q), False, False) + _col(b_st, g))
    return (), (u * jnp.concatenate(mixed, axis=1),)


def _conv4(halo, x, w, b):
    y = b + _row(w, 3) * x
    for k in range(3):
        y = y + _row(w, k) * _shift_rows(halo, x, 3 - k)
    return y


def _f_lru(carries, halos, xs, params):
    (h0,) = carries
    (halo,) = halos
    xb_pre, gate = xs
    conv_w, conv_b, w_r, b_r, w_i, b_i, lam = params
    xb = _conv4(halo, xb_pre, conv_w, conv_b)
    hd = D // HEADS_B
    r_parts, i_parts = [], []
    for h in range(HEADS_B):
        xh = _cols(xb, h * hd, (h + 1) * hd)
        r_parts.append(_mm(xh, _rows(w_r, h * hd, (h + 1) * hd), False, False))
        i_parts.append(_mm(xh, _rows(w_i, h * hd, (h + 1) * hd), False, False))
    r = _sigmoid(jnp.concatenate(r_parts, axis=1) + b_r)
    i = _sigmoid(jnp.concatenate(i_parts, axis=1) + b_i)
    log_a = -LRU_C * r * _softplus(-lam)
    a = jnp.exp(log_a)
    inp = jnp.sqrt(_neg_expm1(2.0 * log_a)) * (i * xb)
    h = _lin_scan(a, inp, h0)
    return (_row(h, h.shape[0] - 1),), (_gelu(gate) * h,)


def _f_ssd(carries, halos, xs, params):
    (st,) = carries
    (halo,) = halos
    z, xbc_pre, dt_raw = xs
    conv_w, conv_b, dt_bias, a_log, d_skip, norm_g = params
    t = z.shape[0]
    xc = _conv4(halo, xbc_pre, conv_w, conv_b)
    xbc = xc * _sigmoid(xc)
    x_all = _cols(xbc, 0, D)
    b_all = _cols(xbc, D, D + GROUPS_C * STATE_C)
    c_all = _cols(xbc, D + GROUPS_C * STATE_C, D + 2 * GROUPS_C * STATE_C)
    dt = _softplus(dt_raw + dt_bias)
    adt = dt * (-jnp.exp(a_log))
    acs = _cumsum_rows(adt)
    acs_t = acs.T
    a_last = _row(acs, t - 1)
    lo = lax.broadcasted_iota(jnp.int32, (t, 128), 1) < HEAD_DIM_C
    lo_rows = lax.broadcasted_iota(jnp.int32, (128, STATE_C), 0) < HEAD_DIM_C
    causal = lax.broadcasted_iota(jnp.int32, (t, t), 0) >= lax.broadcasted_iota(jnp.int32, (t, t), 1)
    y_parts, st_parts = [], []
    for g in range(GROUPS_C):
        bg = _cols(b_all, g * STATE_C, (g + 1) * STATE_C)
        cg = _cols(c_all, g * STATE_C, (g + 1) * STATE_C)
        cb = _mm(cg, bg, False, True)
        for pr in range(2):
            pair = 2 * g + pr
            h0, h1 = 2 * pair, 2 * pair + 1
            x2 = _cols(x_all, pair * 128, (pair + 1) * 128)
            ac0, ac1 = _col(acs, h0), _col(acs, h1)
            l0 = jnp.exp(jnp.where(causal, ac0 - _row(acs_t, h0), -1e30))
            l1 = jnp.exp(jnp.where(causal, ac1 - _row(acs_t, h1), -1e30))
            xdt = x2 * jnp.where(lo, _col(dt, h0), _col(dt, h1))
            y_diag = (_mm(cb * l0, jnp.where(lo, xdt, 0.0), False, False)
                      + _mm(cb * l1, jnp.where(lo, 0.0, xdt), False, False))
            al0, al1 = _col(a_last, h0), _col(a_last, h1)
            decay_s = jnp.where(lo, jnp.exp(al0 - ac0), jnp.exp(al1 - ac1))
            s_new = _mm(xdt * decay_s, bg, True, False)
            prev = _rows(st, pair * 128, (pair + 1) * 128)
            y_off = _mm(cg, prev, False, True) * jnp.where(lo, jnp.exp(ac0), jnp.exp(ac1))
            skip = jnp.where(lo, _col(d_skip, h0), _col(d_skip, h1))
            y_parts.append(y_diag + y_off + x2 * skip)
            st_parts.append(prev * jnp.where(lo_rows, jnp.exp(al0), jnp.exp(al1)) + s_new)
    y = jnp.concatenate(y_parts, axis=1) * (z * _sigmoid(z))
    gw = D // GROUPS_C
    yn = []
    for g in range(GROUPS_C):
        yg = _cols(y, g * gw, (g + 1) * gw)
        yn.append(yg * lax.rsqrt(jnp.mean(yg * yg, axis=-1, keepdims=True) + EPS))
    return (jnp.concatenate(st_parts, axis=0),), (jnp.concatenate(yn, axis=1) * norm_g,)


def _f_merge(carries, halos, xs, params):
    pa, pb, pc, g_raw = xs
    (b_gate,) = params
    m = (_sigmoid(_cols(g_raw, 0, D) + _row(b_gate, 0)) * pa
         + _sigmoid(_cols(g_raw, D, 2 * D) + _row(b_gate, 1)) * pb
         + _sigmoid(_cols(g_raw, 2 * D, 3 * D) + _row(b_gate, 2)) * pc)
    return (), (m,)


def _f_loss(carries, halos, xs, params):
    (acc,) = carries
    h, target = xs
    (g,) = params
    err = jnp.square(_rms(h, g) - target)
    part = 0.5 * jnp.sum(jnp.mean(err, axis=-1, keepdims=True), axis=0, keepdims=True)
    return (acc + part,), ()


def _x_specs(xs, t, index_of):
    specs = []
    for arr, off, width in xs:
        assert off % width == 0 and off + width <= arr.shape[1]
        specs.append(pl.BlockSpec((t, width), functools.partial(lambda j, cb: (index_of(j), cb), cb=off // width)))
    return specs


def _halo_specs(xs, halo_idx, t, index_of):
    specs = []
    for xi in halo_idx:
        _, off, width = xs[xi]
        specs.append(pl.BlockSpec(
            (HALO, width),
            functools.partial(lambda j, cb: (jnp.maximum(index_of(j) * (t // HALO) - 1, 0), cb), cb=off // width)))
    return specs


def _full_spec(a):
    return pl.BlockSpec(a.shape, functools.partial(lambda j, nd: (0,) * nd, nd=a.ndim))


def _chunk_fwd(f, name, t, xs, params, outs, halo_idx=(), carry_shapes=(), save_carries=False, final_carries=False):
    s = xs[0][0].shape[0]
    n = s // t
    nx, nh, npar, no, nc = len(xs), len(halo_idx), len(params), len(outs), len(carry_shapes)
    ns = nc if save_carries else 0
    nf = nc if final_carries else 0

    def body(*refs):
        x_refs, refs = refs[:nx], refs[nx:]
        h_refs, refs = refs[:nh], refs[nh:]
        p_refs, refs = refs[:npar], refs[npar:]
        y_refs, refs = refs[:no], refs[no:]
        s_refs, refs = refs[:ns], refs[ns:]
        f_refs, c_refs = refs[:nf], refs[nf:]
        i = pl.program_id(0)

        @pl.when(i == 0)
        def _():
            for c in c_refs:
                c[...] = jnp.zeros_like(c)

        carries = tuple(c[...] for c in c_refs)
        for s_ref, c in zip(s_refs, carries):
            s_ref[0] = c
        halos = tuple(jnp.where(i > 0, h[...].astype(F32), 0.0) for h in h_refs)
        new_c, ys = f(carries, halos, tuple(x[...].astype(F32) for x in x_refs), tuple(p[...] for p in p_refs))
        for y_ref, y in zip(y_refs, ys):
            y_ref[...] = y.astype(y_ref.dtype)
        for c, v in zip(c_refs, new_c):
            c[...] = v
        for f_ref, v in zip(f_refs, new_c):
            f_ref[...] = v

    ident = lambda j: j
    out_shape = [jax.ShapeDtypeStruct((s, w), dt) for w, dt in outs]
    out_specs = [pl.BlockSpec((t, w), lambda j: (j, 0)) for w, _ in outs]
    if save_carries:
        out_shape += [jax.ShapeDtypeStruct((n,) + tuple(cs), F32) for cs in carry_shapes]
        out_specs += [pl.BlockSpec((1,) + tuple(cs), lambda j: (j, 0, 0)) for cs in carry_shapes]
    if final_carries:
        out_shape += [jax.ShapeDtypeStruct(tuple(cs), F32) for cs in carry_shapes]
        out_specs += [pl.BlockSpec(tuple(cs), lambda j: (0, 0)) for cs in carry_shapes]
    res = pl.pallas_call(
        body, name=name, grid=(n,),
        in_specs=_x_specs(xs, t, ident) + _halo_specs(xs, halo_idx, t, ident) + [_full_spec(p) for p in params],
        out_specs=out_specs, out_shape=out_shape,
        scratch_shapes=[pltpu.VMEM(tuple(cs), F32) for cs in carry_shapes],
        compiler_params=_cparams("arbitrary"),
    )(*[x[0] for x in xs], *[xs[xi][0] for xi in halo_idx], *params)
    return res[:no], res[no:no + ns], res[no + ns:]


def _chunk_bwd(f, name, t, xs, params, dys, dx_dtypes, halo_idx=(), saved=(), carry_seed=None):
    s = xs[0][0].shape[0]
    n = s // t
    nx, nh, npar, nc, ndy = len(xs), len(halo_idx), len(params), len(saved), len(dys)

    def body(*refs):
        x_refs, refs = refs[:nx], refs[nx:]
        h_refs, refs = refs[:nh], refs[nh:]
        p_refs, refs = refs[:npar], refs[npar:]
        s_refs, refs = refs[:nc], refs[nc:]
        dy_refs, refs = refs[:ndy], refs[ndy:]
        dx_refs, refs = refs[:nx], refs[nx:]
        dp_refs, refs = refs[:npar], refs[npar:]
        dc_refs, dh_refs = refs[:nc], refs[nc:]
        j = pl.program_id(0)
        i = n - 1 - j

        @pl.when(j == 0)
        def _():
            for dc in dc_refs:
                dc[...] = jnp.zeros_like(dc) if carry_seed is None else carry_seed(dc.shape)
            for r in dh_refs + dp_refs:
                r[...] = jnp.zeros_like(r)

        carries = tuple(s_ref[0] for s_ref in s_refs)
        halos = tuple(jnp.where(i > 0, h[...].astype(F32), 0.0) for h in h_refs)
        x_vals = tuple(x[...].astype(F32) for x in x_refs)
        p_vals = tuple(p[...] for p in p_refs)
        _, vjp = jax.vjp(f, carries, halos, x_vals, p_vals)
        d_car, d_hal, d_xs, d_par = vjp((tuple(dc[...] for dc in dc_refs), tuple(d[...].astype(F32) for d in dy_refs)))
        d_xs = list(d_xs)
        for k, xi in enumerate(halo_idx):
            w = xs[xi][2]
            d_xs[xi] = d_xs[xi] + jnp.concatenate([jnp.zeros((t - HALO, w), F32), dh_refs[k][...]], axis=0)
            dh_refs[k][...] = jnp.where(i > 0, d_hal[k], 0.0)
        for dx_ref, dx in zip(dx_refs, d_xs):
            dx_ref[...] = dx.astype(dx_ref.dtype)
        for dp_ref, dp in zip(dp_refs, d_par):
            dp_ref[...] += dp
        for dc, v in zip(dc_refs, d_car):
            dc[...] = v

    rev = lambda j: n - 1 - j
    in_specs = (_x_specs(xs, t, rev) + _halo_specs(xs, halo_idx, t, rev) + [_full_spec(p) for p in params]
                + [pl.BlockSpec((1,) + a.shape[1:], lambda j: (n - 1 - j, 0, 0)) for a in saved]
                + [pl.BlockSpec((t, d.shape[1]), lambda j: (n - 1 - j, 0)) for d in dys])
    out_shape = ([jax.ShapeDtypeStruct((s, w), dt) for (_, _, w), dt in zip(xs, dx_dtypes)]
                 + [jax.ShapeDtypeStruct(p.shape, F32) for p in params])
    out_specs = ([pl.BlockSpec((t, w), lambda j: (n - 1 - j, 0)) for _, _, w in xs] + [_full_spec(p) for p in params])
    res = pl.pallas_call(
        body, name=name, grid=(n,), in_specs=in_specs, out_specs=out_specs, out_shape=out_shape,
        scratch_shapes=([pltpu.VMEM(a.shape[1:], F32) for a in saved]
                        + [pltpu.VMEM((HALO, xs[xi][2]), F32) for xi in halo_idx]),
        compiler_params=_cparams("arbitrary"),
    )(*[x[0] for x in xs], *[xs[xi][0] for xi in halo_idx], *params, *saved, *dys)
    return res[:nx], res[nx:]


def _tile(dim, pref):
    for cand in pref:
        if dim % cand == 0:
            return cand
    return dim


def _matmul(a, b, name, ta=False, tb=False, outs=(F32,), epilogue=None, extras=()):
    m, k = (a.shape[1], a.shape[0]) if ta else a.shape
    n = b.shape[0] if tb else b.shape[1]
    tm = _tile(m, (1024, 1152, 512, 256, 128))
    tn = _tile(n, (1152, 1024, 512, 256, 128))
    tk = _tile(k, (1024, 1152, 512, 256, 128))
    nk = k // tk
    ne, no = len(extras), len(outs)
    ca, cb = (0 if ta else 1), (1 if tb else 0)

    def body(*refs):
        a_ref, b_ref = refs[:2]
        e_refs = refs[2:2 + ne]
        o_refs = refs[2 + ne:2 + ne + no]
        acc = refs[-1]
        kk = pl.program_id(2)

        @pl.when(kk == 0)
        def _():
            acc[...] = jnp.zeros_like(acc)

        acc[...] += _dg(_bf(a_ref[...]), _bf(b_ref[...]), ca, cb)

        @pl.when(kk == nk - 1)
        def _():
            res = acc[...]
            vals = (res,) if epilogue is None else epilogue(res, *[e[...] for e in e_refs])
            for o_ref, v in zip(o_refs, vals):
                o_ref[...] = v.astype(o_ref.dtype)

    a_spec = pl.BlockSpec((tk, tm), lambda i, j, kk: (kk, i)) if ta else pl.BlockSpec((tm, tk), lambda i, j, kk: (i, kk))
    b_spec = pl.BlockSpec((tn, tk), lambda i, j, kk: (j, kk)) if tb else pl.BlockSpec((tk, tn), lambda i, j, kk: (kk, j))
    mn_spec = pl.BlockSpec((tm, tn), lambda i, j, kk: (i, j))
    res = pl.pallas_call(
        body, name=name, grid=(m // tm, n // tn, nk),
        in_specs=[a_spec, b_spec] + [mn_spec] * ne,
        out_specs=[mn_spec] * no,
        out_shape=[jax.ShapeDtypeStruct((m, n), dt) for dt in outs],
        scratch_shapes=[pltpu.VMEM((tm, tn), F32)],
        compiler_params=_cparams("parallel", "parallel", "arbitrary"),
    )(a, b, *extras)
    return res if no > 1 else res[0]


def _elementwise_block(r, c):
    if r % 8 == 0 and r >= 8:
        return _tile(r, (256, 128, 64, 32, 16, 8)), c
    return r, _tile(c, (256, 128))


def _adamw_math(g, w, m, v):
    m_new = ADAM_B1 * m + (1.0 - ADAM_B1) * g
    v_new = ADAM_B2 * v + (1.0 - ADAM_B2) * jnp.square(g)
    m_hat = m_new / (1.0 - ADAM_B1 ** ADAM_STEP)
    v_hat = v_new / (1.0 - ADAM_B2 ** ADAM_STEP)
    return -ADAM_LR * (m_hat / (jnp.sqrt(v_hat) + ADAM_EPS) + ADAM_WD * w), m_new, v_new


def _adamw(parts, w, m, v, name):
    nl, r, c = w.shape
    k = parts[0].shape[0]
    tr = _tile(r, (128, 64, 32, 16, 8))
    nb = r // tr

    def body(*refs):
        p_refs, (w_ref, m_ref, v_ref), outs = refs[:nl], refs[nl:nl + 3], refs[nl + 3:]
        layer = pl.program_id(0)
        for q in range(nl):
            @pl.when(layer == q)
            def _(q=q):
                g = p_refs[q][0]
                for j in range(1, k):
                    g = g + p_refs[q][j]
                vals = (g,) + _adamw_math(g, w_ref[0], m_ref[0], v_ref[0])
                for o_ref, val in zip(outs, vals):
                    o_ref[0] = val

    spec = pl.BlockSpec((1, tr, c), lambda l, i: (l, i, 0))
    part_specs = [pl.BlockSpec((k, tr, c), functools.partial(
        lambda l, i, q: (0, jnp.where(l == q, i, jnp.where(l < q, 0, nb - 1)), 0), q=q)) for q in range(nl)]
    return pl.pallas_call(
        body, name=name, grid=(nl, nb), in_specs=part_specs + [spec] * 3,
        out_specs=[spec] * 4, out_shape=[jax.ShapeDtypeStruct((nl, r, c), F32)] * 4,
        compiler_params=_cparams("arbitrary", "arbitrary"),
    )(*parts, w, m, v)


def _adamw_transposed(grads, w, m, v, name):
    nl, r, c = w.shape
    tc = 64
    views = [jnp.transpose(a, (2, 0, 1)) for a in (w, m, v)]

    def body(*refs):
        g_refs, (w_ref, m_ref, v_ref), outs = refs[:nl], refs[nl:nl + 3], refs[nl + 3:]
        for l in range(nl):
            g = g_refs[l][...]
            vals = (g,) + _adamw_math(g, w_ref[:, l, :], m_ref[:, l, :], v_ref[:, l, :])
            for o_ref, val in zip(outs, vals):
                o_ref[:, l, :] = val

    spec = pl.BlockSpec((tc, nl, r), lambda i: (i, 0, 0))
    res = pl.pallas_call(
        body, name=name, grid=(pl.cdiv(c, tc),), in_specs=[pl.BlockSpec((tc, r), lambda i: (i, 0))] * nl + [spec] * 3,
        out_specs=[spec] * 4, out_shape=[jax.ShapeDtypeStruct((c, nl, r), F32)] * 4,
        compiler_params=_cparams("parallel"),
    )(*grads, *views)
    return [jnp.transpose(a, (1, 2, 0)) for a in res]


def _adamw_small(gathered, ws, ms, vs, loss_terms, name):
    n = len(ws)

    def device_sum(ref):
        s = ref[0]
        for j in range(1, N_DEV):
            s = s + ref[j]
        return s

    def body(*refs):
        g_refs, w_refs, m_refs, v_refs = refs[:n], refs[n:2 * n], refs[2 * n:3 * n], refs[3 * n:4 * n]
        loss_ref, outs, loss_out = refs[4 * n], refs[4 * n + 1:-1], refs[-1]
        for i in range(n):
            g = device_sum(g_refs[i])
            vals = (g,) + _adamw_math(g, w_refs[i][...], m_refs[i][...], v_refs[i][...])
            for kind, val in enumerate(vals):
                outs[kind * n + i][...] = val
        loss_out[...] = device_sum(loss_ref)

    res = pl.pallas_call(
        body, name=name,
        out_shape=[jax.ShapeDtypeStruct(a.shape, F32) for _ in range(4) for a in ws] + [
            jax.ShapeDtypeStruct(loss_terms.shape[1:], F32)],
        compiler_params=pltpu.CompilerParams(vmem_limit_bytes=VMEM_LIMIT),
    )(*gathered, *ws, *ms, *vs, loss_terms)
    return [res[kind * n:(kind + 1) * n] for kind in range(4)], res[-1]


ANY = pl.BlockSpec(memory_space=pl.ANY)


def _place():
    return lax.axis_index("x"), lax.axis_index("y"), lax.axis_index("c")


def _handshake(peers):
    barrier = pltpu.get_barrier_semaphore()
    for peer in peers:
        pl.semaphore_signal(barrier, inc=1, device_id=peer, device_id_type=MESH)
    pl.semaphore_wait(barrier, len(peers))


def _comm_call(body, name, inputs, out_shape, scratch, sequencer_id=None, after=()):
    if sequencer_id is None:
        return pl.pallas_call(body, name=name, out_shape=out_shape, in_specs=[ANY] * len(inputs),
                              out_specs=[ANY] * len(out_shape), scratch_shapes=scratch)(*inputs)
    n_in, n_after = len(inputs), len(after)

    def sequencer_body(*refs):
        body(*refs[:n_in], *refs[n_in + n_after:])

    return pl.kernel(
        sequencer_body, out_type=out_shape, mesh=plsc.ScalarSubcoreMesh(axis_name="sequencer", num_cores=1),
        scratch_types=scratch, compiler_params=pltpu.CompilerParams(collective_id=sequencer_id), name=name,
    )(*inputs, *after)


def _all_gather(blocks, name, sequencer_id=None, after=()):
    n = len(blocks)

    def body(*refs):
        x_refs, out_refs = refs[:n], refs[n:2 * n]
        send_sems, recv_sems, local_sems = refs[2 * n:]
        x, y, c = _place()
        me, sibling = (x, y, c), (x, y, 1 - c)
        chips = [(1 - x, y), (x, 1 - y), (1 - x, 1 - y)]
        if sequencer_id is not None:
            _handshake([sibling] + [(*chip, c) for chip in chips])

        def slot(a, px, py, pc):
            return out_refs[a].at[4 * px + 2 * py + pc]

        def copy(a, k, blk, to, src=None):
            return pltpu.make_async_remote_copy(
                src_ref=slot(a, *blk) if src is None else src, dst_ref=slot(a, *blk),
                send_sem=send_sems.at[7 * a + k], recv_sem=recv_sems.at[7 * a + k], device_id=to, device_id_type=MESH)

        mine = [pltpu.make_async_copy(x_refs[a], slot(a, *me), local_sems.at[a]) for a in range(n)]
        first = []
        for a in range(n):
            mine[a].start()
            first.append(copy(a, 0, me, sibling, src=x_refs[a]))
            first += [copy(a, 1 + j, me, (*chip, c), src=x_refs[a]) for j, chip in enumerate(chips)]
        for cp in first:
            cp.start()
        passed = []
        for j, chip in enumerate(chips):
            for a in range(n):
                copy(a, 1 + j, (*chip, c), me).wait_recv()
                passed.append(copy(a, 4 + j, (*chip, c), sibling))
                passed[-1].start()
        for a in range(n):
            copy(a, 0, sibling, me).wait_recv()
            for j, chip in enumerate(chips):
                copy(a, 4 + j, (*chip, 1 - c), me).wait_recv()
        for cp in first + passed:
            cp.wait_send()
        for cp in mine:
            cp.wait()

    return _comm_call(
        body, name, blocks, [jax.ShapeDtypeStruct((N_DEV,) + b.shape, b.dtype) for b in blocks],
        [pltpu.SemaphoreType.DMA((7 * n,)), pltpu.SemaphoreType.DMA((7 * n,)), pltpu.SemaphoreType.DMA((n,))],
        sequencer_id, after)


def _exchange_sibling(gs, name, sequencer_id=None, after=()):
    n = len(gs)

    def body(*refs):
        g_refs, out_refs = refs[:n], refs[n:2 * n]
        send_sems, recv_sems = refs[2 * n:]
        x, y, c = _place()
        if sequencer_id is not None:
            _handshake([(x, y, 1 - c)])
        copies = [pltpu.make_async_remote_copy(
            src_ref=g_refs[a].at[2 * k + 1 - c], dst_ref=out_refs[a].at[k], send_sem=send_sems.at[4 * a + k],
            recv_sem=recv_sems.at[4 * a + k], device_id=(x, y, 1 - c), device_id_type=MESH)
            for a in range(n) for k in range(4)]
        for cp in copies:
            cp.start()
        for cp in copies:
            cp.wait()

    return _comm_call(body, name, gs, [jax.ShapeDtypeStruct((4,) + g.shape[1:], g.dtype) for g in gs],
                      [pltpu.SemaphoreType.DMA((4 * n,)), pltpu.SemaphoreType.DMA((4 * n,))], sequencer_id, after)


def _add_sibling(g, r1, name, after=()):
    _, r, w = g.shape
    tr, tc = _elementwise_block(r, w)
    core = jnp.reshape(lax.axis_index("c"), (1,)).astype(jnp.int32)

    def body(c_ref, g_ref, r_ref, *rest):
        o_ref, wire_ref = rest[len(after):]
        s = g_ref[...] + r_ref[...]
        o_ref[...] = s
        wire_ref[...] = s.astype(BF16)

    spec = pl.BlockSpec((1, tr, tc), lambda k, i, j, c_ref: (k, i, j))
    return pl.pallas_call(
        body, name=name, out_shape=[jax.ShapeDtypeStruct((4, r, w), F32), jax.ShapeDtypeStruct((4, r, w), BF16)],
        grid_spec=pltpu.PrefetchScalarGridSpec(
            num_scalar_prefetch=1, grid=(4, r // tr, w // tc),
            in_specs=[pl.BlockSpec((1, tr, tc), lambda k, i, j, c_ref: (2 * k + c_ref[0], i, j)), spec]
            + [ANY] * len(after),
            out_specs=[spec, spec]),
        compiler_params=_cparams("parallel", "parallel", "parallel"),
    )(core, g, r1, *after)


def _exchange_chips(ps, name, sequencer_id=None, after=()):
    n = len(ps)

    def body(*refs):
        p_refs, out_refs = refs[:n], refs[n:2 * n]
        send_sems, recv_sems = refs[2 * n:]
        x, y, c = _place()
        chips = [(1 - x, y), (x, 1 - y), (1 - x, 1 - y)]
        if sequencer_id is not None:
            _handshake([(*chip, c) for chip in chips])
        copies = [pltpu.make_async_remote_copy(
            src_ref=p_refs[a].at[2 * px + py], dst_ref=out_refs[a].at[j], send_sem=send_sems.at[3 * a + j],
            recv_sem=recv_sems.at[3 * a + j], device_id=(px, py, c), device_id_type=MESH)
            for a in range(n) for j, (px, py) in enumerate(chips)]
        for cp in copies:
            cp.start()
        for cp in copies:
            cp.wait()

    return _comm_call(body, name, ps, [jax.ShapeDtypeStruct((3,) + p.shape[1:], p.dtype) for p in ps],
                      [pltpu.SemaphoreType.DMA((3 * n,)), pltpu.SemaphoreType.DMA((3 * n,))], sequencer_id, after)


def _sum_chips(own, others, name):
    _, r, c = own.shape
    tr, tc = _elementwise_block(r, c)
    chip = jnp.reshape(2 * lax.axis_index("x") + lax.axis_index("y"), (1,)).astype(jnp.int32)

    def body(chip_ref, own_ref, others_ref, o_ref):
        g = own_ref[0]
        for j in range(3):
            g = g + others_ref[j].astype(F32)
        o_ref[...] = g

    return pl.pallas_call(
        body, name=name, out_shape=jax.ShapeDtypeStruct((r, c), F32),
        grid_spec=pltpu.PrefetchScalarGridSpec(
            num_scalar_prefetch=1, grid=(r // tr, c // tc),
            in_specs=[pl.BlockSpec((1, tr, tc), lambda i, j, chip_ref: (chip_ref[0], i, j)),
                      pl.BlockSpec((3, tr, tc), lambda i, j, chip_ref: (0, i, j))],
            out_specs=pl.BlockSpec((tr, tc), lambda i, j, chip_ref: (i, j))),
        compiler_params=_cparams("parallel", "parallel"),
    )(chip, own, others)


def _reorder_in_proj(wt):
    za_zb, zc, xbc, dt, gates = (wt[:4096], wt[4096:5120], wt[5120:7168], wt[7168:7184], wt[7184:])
    return jnp.concatenate([za_zb, xbc, gates, zc, dt, jnp.zeros((DT_PAD - 16, wt.shape[1]), wt.dtype)], axis=0)


def _restore_in_proj(wt):
    return jnp.concatenate([wt[:4096], wt[OFF_ZC:OFF_ZC + W_ZC], wt[OFF_XBC:OFF_XBC + W_XBC],
                            wt[OFF_DT:OFF_DT + 16], wt[OFF_GATE:OFF_GATE + W_GATE]], axis=0)


def _lanes_from_devices(g):
    return jnp.moveaxis(g, 0, 1).reshape(g.shape[1], N_DEV * g.shape[2])


def _lanes_to_devices(a):
    return jnp.moveaxis(a.reshape(a.shape[0], N_DEV, a.shape[1] // N_DEV), 1, 0)


def _pad_lanes(a, width):
    return jnp.pad(a, ((0, 0), (0, width - a.shape[1])))


BIG = ("w_in", "w_branch_a", "w_branch_b", "w_branch_c", "w_out", "w_mlp_up", "w_mlp_down")
SMALL_SHARDED = ("b_gate", "lru_conv_w", "ssd_conv_w")
REPLICATED = ("norm_mix_g", "gmlp_ln_g", "gmlp_ln_b", "gmlp_w_s", "gmlp_b_s", "lru_conv_b", "lru_w_r", "lru_b_r",
              "lru_w_i", "lru_b_i", "lru_lambda", "ssd_conv_b", "ssd_dt_bias", "ssd_a_log", "ssd_d", "ssd_norm_g",
              "norm_mlp_g", "final_norm_g")
WEIGHTS = ("norm_mix_g", "w_in", "b_gate", "gmlp_ln_g", "gmlp_ln_b", "gmlp_w_s", "gmlp_b_s", "lru_conv_w", "lru_conv_b",
           "lru_w_r", "lru_b_r", "lru_w_i", "lru_b_i", "lru_lambda", "ssd_conv_w", "ssd_conv_b", "ssd_dt_bias",
           "ssd_a_log", "ssd_d", "ssd_norm_g", "w_branch_a", "w_branch_b", "w_branch_c", "w_out", "norm_mlp_g",
           "w_mlp_up", "w_mlp_down", "final_norm_g")
TRANSPOSED = ("w_in", "w_mlp_up")
SMALL_MATRICES = ("gmlp_w_s", "lru_w_r", "lru_w_i")
SMALL_VECTORS = tuple(n for n in REPLICATED if n not in SMALL_MATRICES)
GRADIENT_GROUPS = {"mlp": ("w_mlp_up", "w_mlp_down"), "mix": ("w_branch_a", "w_branch_b", "w_branch_c", "w_out"),
                   "in": ("w_in",)}


def _layer_params(full, l):
    row = lambda a: a.reshape(1, -1)
    return dict(
        norm_mix_g=row(full["norm_mix_g"][l]), norm_mlp_g=row(full["norm_mlp_g"][l]),
        gmlp=(row(full["gmlp_ln_g"][l]), row(full["gmlp_ln_b"][l]), full["gmlp_w_s"][l].reshape(GROUPS_A * CHUNK, CHUNK),
              full["gmlp_b_s"][l].T),
        lru=(full["lru_conv_w"][l], row(full["lru_conv_b"][l]), full["lru_w_r"][l].reshape(D, D // HEADS_B),
             row(full["lru_b_r"][l]), full["lru_w_i"][l].reshape(D, D // HEADS_B), row(full["lru_b_i"][l]),
             row(full["lru_lambda"][l])),
        ssd=(full["ssd_conv_w"][l], row(full["ssd_conv_b"][l]), _pad_lanes(row(full["ssd_dt_bias"][l]), DT_PAD),
             _pad_lanes(row(full["ssd_a_log"][l]), DT_PAD), _pad_lanes(row(full["ssd_d"][l]), DT_PAD),
             row(full["ssd_norm_g"][l])),
        b_gate=full["b_gate"][l],
    )


def _forward_layer(h, p, wb, l, after_mixers=None):
    tag = f"l{l}"
    t_row = 512
    (hn,), _, _ = _chunk_fwd(_f_rmsnorm, f"norm_mix_{tag}", t_row, [(h, 0, D)], [p["norm_mix_g"]], [(D, BF16)])
    proj = _matmul(hn, wb["w_in"], f"in_proj_{tag}", tb=True)
    (ya,), _, _ = _chunk_fwd(_f_gmlp, f"gmlp_{tag}", CHUNK, [(proj, OFF_ZA, W_ZA)], p["gmlp"], [(D, BF16)])
    lru_xs = [(proj, OFF_ZB, D), (proj, OFF_ZB + D, D)]
    (yb,), lru_saved, _ = _chunk_fwd(_f_lru, f"lru_{tag}", CHUNK, lru_xs, p["lru"], [(D, BF16)], halo_idx=(0,),
                                     carry_shapes=[(1, D)], save_carries=True)
    ssd_xs = [(proj, OFF_ZC, W_ZC), (proj, OFF_XBC, W_XBC), (proj, OFF_DT, W_DT)]
    (yc,), ssd_saved, _ = _chunk_fwd(_f_ssd, f"ssd_{tag}", CHUNK, ssd_xs, p["ssd"], [(D, BF16)], halo_idx=(1,),
                                     carry_shapes=[(HEADS_C * HEAD_DIM_C, STATE_C)], save_carries=True)
    if after_mixers is not None:
        after_mixers(yc)
    pa = _matmul(ya, wb["w_branch_a"], f"branch_a_{tag}")
    pb = _matmul(yb, wb["w_branch_b"], f"branch_b_{tag}")
    pc = _matmul(yc, wb["w_branch_c"], f"branch_c_{tag}")
    merge_xs = [(pa, 0, D), (pb, 0, D), (pc, 0, D), (proj, OFF_GATE, W_GATE)]
    (merged,), _, _ = _chunk_fwd(_f_merge, f"merge_{tag}", t_row, merge_xs, [p["b_gate"]], [(D, BF16)])
    h_mid = _matmul(merged, wb["w_out"], f"out_proj_{tag}", epilogue=lambda acc, res: (acc + res,), extras=(h,))
    (hn2,), _, _ = _chunk_fwd(_f_rmsnorm, f"norm_mlp_{tag}", t_row, [(h_mid, 0, D)], [p["norm_mlp_g"]], [(D, BF16)])

    def relu_sq(acc):
        r = jnp.maximum(acc, 0.0)
        return r, r * r

    relu_up, act = _matmul(hn2, wb["w_mlp_up"], f"mlp_up_{tag}", tb=True, outs=(F32, BF16), epilogue=relu_sq)
    h_out = _matmul(act, wb["w_mlp_down"], f"mlp_down_{tag}", epilogue=lambda acc, res: (acc + res,), extras=(h_mid,))
    saved = dict(h=h, hn=hn, proj=proj, ya=ya, yb=yb, yc=yc, lru_saved=lru_saved, ssd_saved=ssd_saved, pa=pa, pb=pb,
                 pc=pc, merged=merged, h_mid=h_mid, hn2=hn2, relu_up=relu_up, act=act, lru_xs=lru_xs, ssd_xs=ssd_xs,
                 merge_xs=merge_xs)
    return h_out, saved


def _backward_layer(dh, sv, p, wb, l):
    tag = f"l{l}"
    t_row = 512
    g = {}
    d_up = _matmul(dh, wb["w_mlp_down"], f"d_act_{tag}", tb=True, outs=(BF16,),
                   epilogue=lambda acc, r: (acc * (2.0 * r),), extras=(sv["relu_up"],))
    g["w_mlp_down"] = _matmul(sv["act"], dh, f"dw_mlp_down_{tag}", ta=True)
    g["w_mlp_up"] = _matmul(d_up, sv["hn2"], f"dw_mlp_up_{tag}", ta=True)
    d_hn2 = _matmul(d_up, wb["w_mlp_up"], f"d_hn2_{tag}")
    (d_mid,), (g["norm_mlp_g"],) = _chunk_bwd(_f_rmsnorm_res, f"norm_mlp_bwd_{tag}", t_row, [(sv["h_mid"], 0, D)],
                                              [p["norm_mlp_g"]], [d_hn2, dh], [F32])
    d_merged = _matmul(d_mid, wb["w_out"], f"d_merged_{tag}", tb=True)
    g["w_out"] = _matmul(sv["merged"], d_mid, f"dw_out_{tag}", ta=True)
    (d_pa, d_pb, d_pc, d_gate), (g["b_gate"],) = _chunk_bwd(
        _f_merge, f"merge_bwd_{tag}", t_row, sv["merge_xs"], [p["b_gate"]], [d_merged], [BF16] * 4)
    d_y = {}
    for br, d_p, y in (("a", d_pa, sv["ya"]), ("b", d_pb, sv["yb"]), ("c", d_pc, sv["yc"])):
        g[f"w_branch_{br}"] = _matmul(y, d_p, f"dw_branch_{br}_{tag}", ta=True)
        d_y[br] = _matmul(d_p, wb[f"w_branch_{br}"], f"d_y{br}_{tag}", tb=True)
    (d_za,), g_gmlp = _chunk_bwd(_f_gmlp, f"gmlp_bwd_{tag}", CHUNK, [(sv["proj"], OFF_ZA, W_ZA)], p["gmlp"],
                                 [d_y["a"]], [BF16])
    (d_xb, d_gt), g_lru = _chunk_bwd(_f_lru, f"lru_bwd_{tag}", CHUNK, sv["lru_xs"], p["lru"], [d_y["b"]], [BF16] * 2,
                                     halo_idx=(0,), saved=sv["lru_saved"])
    (d_zc, d_xbc, d_dt), g_ssd = _chunk_bwd(_f_ssd, f"ssd_bwd_{tag}", CHUNK, sv["ssd_xs"], p["ssd"], [d_y["c"]],
                                            [BF16] * 3, halo_idx=(1,), saved=sv["ssd_saved"])
    d_proj = jnp.concatenate([d_za, d_xb, d_gt, d_xbc, d_gate, d_zc, d_dt], axis=1)
    g["w_in"] = _matmul(d_proj, sv["hn"], f"dw_in_{tag}", ta=True)
    d_hn = _matmul(d_proj, wb["w_in"], f"d_hn_{tag}")
    (d_h,), (g["norm_mix_g"],) = _chunk_bwd(_f_rmsnorm_res, f"norm_mix_bwd_{tag}", t_row, [(sv["h"], 0, D)],
                                            [p["norm_mix_g"]], [d_hn, d_mid], [F32])
    g["w_in"] = _restore_in_proj(g["w_in"])
    for n in BIG:
        g[n] = g[n].reshape(N_DEV, g[n].shape[0] // N_DEV, g[n].shape[1])
    g["gmlp_ln_g"], g["gmlp_ln_b"], g["gmlp_w_s"] = g_gmlp[:3]
    g["gmlp_b_s"] = g_gmlp[3].T
    (g["lru_conv_w"], g["lru_conv_b"], g["lru_w_r"], g["lru_b_r"], g["lru_w_i"], g["lru_b_i"], g["lru_lambda"]) = g_lru
    g["ssd_conv_w"], g["ssd_conv_b"] = g_ssd[:2]
    g["ssd_dt_bias"], g["ssd_a_log"], g["ssd_d"] = (a[:, :HEADS_C] for a in g_ssd[2:5])
    g["ssd_norm_g"] = g_ssd[5]
    g["later"] = {"mlp": d_merged, "mix": d_za, "in": d_h}
    return d_h, g


LOSS_ROWS = 512


def _loss_and_grads(h, target, full, layer_weights, first_gathered):
    seq = h.shape[0]
    layer_p = [_layer_params(full, l) for l in range(DEPTH)]
    layer_w = [layer_weights(0, (first_gathered,))]
    saved = []
    for l in range(DEPTH):
        fetch_next = None
        if l + 1 < DEPTH:
            fetch_next = lambda y, l=l: layer_w.append(layer_weights(l + 1, (layer_w[l]["w_mlp_down"], y)))
        h, sv = _forward_layer(h, layer_p[l], layer_w[l], l, fetch_next)
        saved.append(sv)
    final_g = full["final_norm_g"].reshape(1, D)
    loss_xs = [(h, 0, D), (target, 0, D)]
    t_loss = min(LOSS_ROWS, seq)
    _, _, (loss_acc,) = _chunk_fwd(_f_loss, "loss", t_loss, loss_xs, [final_g], [], carry_shapes=[(1, 128)],
                                   final_carries=True)
    zero_acc = jnp.zeros((seq // t_loss, 1, 128), F32)
    seed = lambda shape: (lax.broadcasted_iota(jnp.int32, shape, 1) == 0).astype(F32)
    (dh, _), (g_final,) = _chunk_bwd(_f_loss, "loss_bwd", t_loss, loss_xs, [final_g], [], [F32, F32], saved=[zero_acc],
                                     carry_seed=seed)
    layer_g = [None] * DEPTH
    for l in reversed(range(DEPTH)):
        dh, layer_g[l] = _backward_layer(dh, saved[l], layer_p[l], layer_w[l], l)
    return loss_acc, dh, layer_g, g_final


def _small_views(d):
    views = {n: d[n] for n in REPLICATED}
    views["gmlp_b_s"] = d["gmlp_b_s"].reshape(DEPTH * GROUPS_A, CHUNK)
    views["final_norm_g"] = d["final_norm_g"].reshape(1, D)
    for n in SMALL_MATRICES:
        views[n] = d[n].reshape(DEPTH * D, D // HEADS_B)
    return views


def kernel(x, norm_mix_g, w_in, b_gate, gmlp_ln_g, gmlp_ln_b, gmlp_w_s, gmlp_b_s, lru_conv_w, lru_conv_b, lru_w_r, lru_b_r, lru_w_i, lru_b_i, lru_lambda, ssd_conv_w, ssd_conv_b, ssd_dt_bias, ssd_a_log, ssd_d, ssd_norm_g, w_branch_a, w_branch_b, w_branch_c, w_out, norm_mlp_g, w_mlp_up, w_mlp_down, final_norm_g, loss_target, m_norm_mix_g, m_w_in, m_b_gate, m_gmlp_ln_g, m_gmlp_ln_b, m_gmlp_w_s, m_gmlp_b_s, m_lru_conv_w, m_lru_conv_b, m_lru_w_r, m_lru_b_r, m_lru_w_i, m_lru_b_i, m_lru_lambda, m_ssd_conv_w, m_ssd_conv_b, m_ssd_dt_bias, m_ssd_a_log, m_ssd_d, m_ssd_norm_g, m_w_branch_a, m_w_branch_b, m_w_branch_c, m_w_out, m_norm_mlp_g, m_w_mlp_up, m_w_mlp_down, m_final_norm_g, v_norm_mix_g, v_w_in, v_b_gate, v_gmlp_ln_g, v_gmlp_ln_b, v_gmlp_w_s, v_gmlp_b_s, v_lru_conv_w, v_lru_conv_b, v_lru_w_r, v_lru_b_r, v_lru_w_i, v_lru_b_i, v_lru_lambda, v_ssd_conv_w, v_ssd_conv_b, v_ssd_dt_bias, v_ssd_a_log, v_ssd_d, v_ssd_norm_g, v_w_branch_a, v_w_branch_b, v_w_branch_c, v_w_out, v_norm_mlp_g, v_w_mlp_up, v_w_mlp_down, v_final_norm_g):
    args = locals()
    w = {n: args[n] for n in WEIGHTS}
    m = {n: args["m_" + n] for n in WEIGHTS}
    v = {n: args["v_" + n] for n in WEIGHTS}
    seq = x.shape[1]
    h = x.reshape(seq, D)
    target = loss_target.reshape(seq, D)

    def shard_on_wire(n, l):
        return (w[n][l].T if n in TRANSPOSED else w[n][l]).astype(BF16)

    first = _all_gather([shard_on_wire("w_in", 0)] + [w[n] for n in SMALL_SHARDED], "gather_weights_first")
    full = {n: w[n] for n in REPLICATED}
    for n, g in zip(SMALL_SHARDED, first[1:]):
        full[n] = jnp.stack([_lanes_from_devices(g[:, l]) for l in range(DEPTH)])

    def layer_weights(l, after):
        have = {"w_in": first[0]} if l == 0 else {}
        names = [n for n in BIG if n not in have]
        later = _all_gather([shard_on_wire(n, l) for n in names], f"gather_weights_l{l}", SEQ_GATHER + l, after=after)
        have.update(zip(names, later))
        wl = {n: have[n].reshape(-1, D) for n in BIG}
        wl["w_in"] = _reorder_in_proj(wl["w_in"])
        return wl

    loss_local, dh, layer_g, g_final = _loss_and_grads(h, target, full, layer_weights, first[0])
    grad_x = dh.reshape(x.shape)
    out = {}
    kinds = ("grad", "delta", "new_m", "new_v")

    sequencer_before = {}

    def reduce_scatter(slabs, tag, on_sequencer, later=()):
        keys = list(slabs)
        ids = (SEQ_TO_SIBLING, SEQ_TO_CHIPS) if on_sequencer else (None, None)
        from_sibling = _exchange_sibling([slabs[k] for k in keys], f"grads_to_sibling_{tag}", ids[0],
                                         sequencer_before.get("sibling", ()))
        chip_sums = [_add_sibling(slabs[k], r, f"add_sibling_{k[0]}_{k[1]}", later) for k, r in zip(keys, from_sibling)]
        from_chips = _exchange_chips([wire for _, wire in chip_sums], f"grads_to_chips_{tag}", ids[1],
                                     sequencer_before.get("chips", ()))
        if on_sequencer:
            sequencer_before["sibling"], sequencer_before["chips"] = (from_sibling[0],), (from_chips[0],)
        return {k: _sum_chips(own, others, f"sum_chips_{k[0]}_{k[1]}")
                for k, (own, _), others in zip(keys, chip_sums, from_chips)}

    reduced = {}
    groups = [(l, grp) for l in range(DEPTH - 1, -1, -1) for grp in GRADIENT_GROUPS]
    for l, grp in groups[:-1]:
        slabs = {(n, l): layer_g[l][n] for n in GRADIENT_GROUPS[grp]}
        reduced.update(reduce_scatter(slabs, f"{grp}_l{l}", True, (layer_g[l]["later"][grp],)))
    l, grp = groups[-1]
    slabs = {(n, l): layer_g[l][n] for n in GRADIENT_GROUPS[grp]}
    for n in SMALL_SHARDED:
        slabs[n, None] = jnp.concatenate([_lanes_to_devices(layer_g[l][n]) for l in range(DEPTH)], axis=1)
    g_small = {n: jnp.concatenate([layer_g[l][n] for l in range(DEPTH)], axis=0) for n in REPLICATED[:-1]}
    g_small["final_norm_g"] = g_final
    for n in SMALL_MATRICES:
        slabs[n, None] = g_small[n].reshape(N_DEV, -1, g_small[n].shape[-1])
    reduced.update(reduce_scatter(slabs, "last", True, (layer_g[l]["later"][grp],)))
    for n in BIG:
        grads = [reduced[n, l] for l in range(DEPTH)]
        if n in TRANSPOSED and w[n].shape[-1] % LANES:
            res = _adamw_transposed(grads, w[n], m[n], v[n], f"adamw_{n}")
        else:
            res = _adamw([(g.T if n in TRANSPOSED else g)[None] for g in grads], w[n], m[n], v[n], f"adamw_{n}")
        for kind, a in zip(kinds, res):
            out[kind, n] = a
    for n in SMALL_SHARDED:
        one = lambda a: a.reshape((1, -1, a.shape[-1]))
        for kind, a in zip(kinds, _adamw([reduced[n, None][None]], one(w[n]), one(m[n]), one(v[n]), f"adamw_{n}")):
            out[kind, n] = a.reshape(w[n].shape)

    to_gather = [reduced[n, None] if n in SMALL_MATRICES else g_small[n] for n in REPLICATED]
    *g_gathered, loss_terms = _all_gather(to_gather + [loss_local], "gather_small_grads")
    g_all = dict(zip(REPLICATED, g_gathered))
    wv, mv, vv = _small_views(w), _small_views(m), _small_views(v)
    res, loss_sum = _adamw_small([g_all[n] for n in SMALL_VECTORS],
                                 *[[d[n] for n in SMALL_VECTORS] for d in (wv, mv, vv)], loss_terms, "adamw_vectors")
    loss = loss_sum[0, 0]
    for kind, arrays in zip(kinds, res):
        for n, a in zip(SMALL_VECTORS, arrays):
            out[kind, n] = a.reshape(w[n].shape)
    for n in SMALL_MATRICES:
        g_full = g_all[n].reshape((1, 1) + wv[n].shape)
        for kind, a in zip(kinds, _adamw([g_full[0]], wv[n][None], mv[n][None], vv[n][None], f"adamw_{n}")):
            out[kind, n] = a.reshape(w[n].shape)

    return (loss, grad_x, *[out[kind, n] for kind in kinds for n in WEIGHTS])
```

```python
import functools

import jax
import jax.numpy as jnp
from jax import lax
from jax.experimental import pallas as pl
from jax.experimental.pallas import tpu as pltpu
from jax.experimental.pallas import tpu_sc as plsc

F32 = jnp.float32
BF16 = jnp.bfloat16
MESH = pl.DeviceIdType.MESH

D = 1024
DEPTH = 2
EPS = 1e-6
CHUNK = 128
GROUPS_A = 8
HEADS_B = 8
LRU_C = 8.0
HEADS_C = 16
HEAD_DIM_C = 64
GROUPS_C = 4
STATE_C = 128
HIDDEN = 4 * D
DT_PAD = 128
OFF_ZA, W_ZA = 0, 2048
OFF_ZB, W_ZB = 2048, 2048
OFF_XBC, W_XBC = 4096, 2048
OFF_GATE, W_GATE = 6144, 3072
OFF_ZC, W_ZC = 9216, 1024
OFF_DT, W_DT = 10240, DT_PAD
D_IN_PAD = 10368
D_IN = 10256
N_DEV = 8
SEQ_GATHER = 1
SEQ_TO_SIBLING = SEQ_GATHER + DEPTH
SEQ_TO_CHIPS = SEQ_TO_SIBLING + 1

ADAM_LR = 0.001
ADAM_B1 = 0.9
ADAM_B2 = 0.999
ADAM_EPS = 1e-08
ADAM_WD = 0.01
ADAM_STEP = 10

VMEM_LIMIT = 56 * 1024 * 1024
HALO = 8


def _cparams(*sem):
    return pltpu.CompilerParams(dimension_semantics=sem, vmem_limit_bytes=VMEM_LIMIT)


def _bf(x):
    return x.astype(BF16)


def _dg(a, b, ca, cb):
    return lax.dot_general(a, b, (((ca,), (cb,)), ((), ())), preferred_element_type=F32)


@functools.partial(jax.custom_vjp, nondiff_argnums=(2, 3))
def _mm(a, b, ta, tb):
    return _dg(_bf(a), _bf(b), 0 if ta else 1, 1 if tb else 0)


def _mm_fwd(a, b, ta, tb):
    return _mm(a, b, ta, tb), (a, b)


def _mm_bwd(ta, tb, res, g):
    a, b = res
    ma = 1 if ta else 0
    nb = 0 if tb else 1
    gb, ab, bb = _bf(g), _bf(a), _bf(b)
    da = _dg(bb, gb, nb, 1) if ta else _dg(gb, bb, 1, nb)
    db = _dg(gb, ab, 0, ma) if tb else _dg(ab, gb, ma, 0)
    return da.astype(a.dtype), db.astype(b.dtype)


_mm.defvjp(_mm_fwd, _mm_bwd)


@functools.partial(jax.custom_vjp, nondiff_argnums=(1, 2))
def _cols(x, lo, hi):
    return x[:, lo:hi]


def _cols_fwd(x, lo, hi):
    return x[:, lo:hi], x.shape[1]


def _cols_bwd(lo, hi, width, g):
    parts = []
    if lo:
        parts.append(jnp.zeros((g.shape[0], lo), g.dtype))
    parts.append(g)
    if width - hi:
        parts.append(jnp.zeros((g.shape[0], width - hi), g.dtype))
    return (jnp.concatenate(parts, axis=1) if len(parts) > 1 else g,)


_cols.defvjp(_cols_fwd, _cols_bwd)


@functools.partial(jax.custom_vjp, nondiff_argnums=(1, 2))
def _rows(x, lo, hi):
    return x[lo:hi, :]


def _rows_fwd(x, lo, hi):
    return x[lo:hi, :], x.shape[0]


def _rows_bwd(lo, hi, height, g):
    parts = []
    if lo:
        parts.append(jnp.zeros((lo, g.shape[1]), g.dtype))
    parts.append(g)
    if height - hi:
        parts.append(jnp.zeros((height - hi, g.shape[1]), g.dtype))
    return (jnp.concatenate(parts, axis=0) if len(parts) > 1 else g,)


_rows.defvjp(_rows_fwd, _rows_bwd)


def _col(x, j):
    lane = lax.broadcasted_iota(jnp.int32, x.shape, 1)
    return jnp.sum(jnp.where(lane == j, x, 0.0), axis=1, keepdims=True)


def _row(x, i):
    r = lax.broadcasted_iota(jnp.int32, x.shape, 0)
    return jnp.sum(jnp.where(r == i, x, 0.0), axis=0, keepdims=True)


def _roll_down(x, s):
    return pltpu.roll(x, s, 0)


def _roll_up(x, s):
    return pltpu.roll(x, x.shape[0] - s, 0)


def _row_iota(x):
    return lax.broadcasted_iota(jnp.int32, x.shape, 0)


@functools.partial(jax.custom_vjp, nondiff_argnums=(2,))
def _shift_rows(halo, x, s):
    if s == 0:
        return x
    return _roll_down(jnp.concatenate([halo, x], axis=0), s)[HALO:]


def _shift_rows_fwd(halo, x, s):
    return _shift_rows(halo, x, s), None


def _shift_rows_bwd(s, _, g):
    if s == 0:
        return jnp.zeros((HALO, g.shape[1]), g.dtype), g
    ge = jnp.concatenate([jnp.zeros((HALO, g.shape[1]), g.dtype), g], axis=0)
    de = _roll_up(ge, s)
    return de[:HALO], de[HALO:]


_shift_rows.defvjp(_shift_rows_fwd, _shift_rows_bwd)


SUBLANES = 8
LANES = 128


def _scan_tiles(a, b, carry, up):
    n, c = a.shape
    nt = n // SUBLANES
    a = a.reshape(nt, SUBLANES, c)
    b = b.reshape(nt, SUBLANES, c)
    sub = lax.broadcasted_iota(jnp.int32, a.shape, 1)
    s = 1
    while s < SUBLANES:
        keep = (sub < SUBLANES - s) if up else (sub >= s)
        shift = SUBLANES - s if up else s
        a_sh = jnp.where(keep, pltpu.roll(a, shift, 1), 1.0)
        b_sh = jnp.where(keep, pltpu.roll(b, shift, 1), 0.0)
        b = a * b_sh + b
        a = a * a_sh
        s *= 2
    tiles = [None] * nt
    edge = 0 if up else SUBLANES - 1
    for j in (range(nt - 1, -1, -1) if up else range(nt)):
        tiles[j] = b[j] if carry is None else b[j] + a[j] * carry
        carry = tiles[j][edge:edge + 1, :]
    return jnp.concatenate(tiles, axis=0)


@jax.custom_vjp
def _lin_scan(a, b, h0):
    return _scan_tiles(a, b, h0, up=False)


def _lin_scan_fwd(a, b, h0):
    h = _lin_scan(a, b, h0)
    return h, (a, h0, h)


def _lin_scan_bwd(res, g):
    a, h0, h = res
    n = a.shape[0]
    row = _row_iota(a)
    a_next = jnp.where(row < n - 1, _roll_up(a, 1), 0.0)
    gg = _scan_tiles(a_next, g, None, up=True)
    h_prev = jnp.where(row >= 1, _roll_down(h, 1), h0)
    return gg * h_prev, gg, _row(a * gg, 0)


_lin_scan.defvjp(_lin_scan_fwd, _lin_scan_bwd)


@jax.custom_vjp
def _cumsum_rows(x):
    n = x.shape[0]
    row = _row_iota(x)
    s = 1
    while s < n:
        x = x + jnp.where(row >= s, _roll_down(x, s), 0.0)
        s *= 2
    return x


def _cumsum_rows_fwd(x):
    return _cumsum_rows(x), None


def _cumsum_rows_bwd(_, g):
    n = g.shape[0]
    row = _row_iota(g)
    s = 1
    while s < n:
        g = g + jnp.where(row < n - s, _roll_up(g, s), 0.0)
        s *= 2
    return (g,)


_cumsum_rows.defvjp(_cumsum_rows_fwd, _cumsum_rows_bwd)


def _sigmoid(x):
    return jax.nn.sigmoid(x)


def _softplus(x):
    return jnp.maximum(x, 0.0) + jnp.log1p(jnp.exp(-jnp.abs(x)))


def _gelu(x):
    return jax.nn.gelu(x, approximate=True)


def _neg_expm1(x):
    series = -x * (1.0 + x * (0.5 + x * (1.0 / 6.0 + x * (1.0 / 24.0))))
    return jnp.where(x > -0.01, series, 1.0 - jnp.exp(x))


def _rms(x, g):
    return x * lax.rsqrt(jnp.mean(x * x, axis=-1, keepdims=True) + EPS) * g


def _f_rmsnorm(carries, halos, xs, params):
    (h,) = xs
    (g,) = params
    return (), (_rms(h, g),)


def _f_rmsnorm_res(carries, halos, xs, params):
    (h,) = xs
    (g,) = params
    return (), (_rms(h, g), h)


def _f_gmlp(carries, halos, xs, params):
    (za,) = xs
    ln_g, ln_b, w_s, b_st = params
    ga = _gelu(za)
    u = _cols(ga, 0, D)
    v = _cols(ga, D, 2 * D)
    vc = v - jnp.mean(v, axis=-1, keepdims=True)
    vn = vc * lax.rsqrt(jnp.mean(vc * vc, axis=-1, keepdims=True) + EPS) * ln_g + ln_b
    q = CHUNK
    causal = lax.broadcasted_iota(jnp.int32, (q, q), 0) >= lax.broadcasted_iota(jnp.int32, (q, q), 1)
    mixed = []
    for g in range(GROUPS_A):
        w = jnp.where(causal, _rows(w_s, g * q, (g + 1) * q), 0.0)
        mixed.append(_mm(w, _cols(vn, g * q, (g + 1) * q), False, False) + _col(b_st, g))
    return (), (u * jnp.concatenate(mixed, axis=1),)


def _conv4(halo, x, w, b):
    y = b + _row(w, 3) * x
    for k in range(3):
        y = y + _row(w, k) * _shift_rows(halo, x, 3 - k)
    return y


def _f_lru(carries, halos, xs, params):
    (h0,) = carries
    (halo,) = halos
    xb_pre, gate = xs
    conv_w, conv_b, w_r, b_r, w_i, b_i, lam = params
    xb = _conv4(halo, xb_pre, conv_w, conv_b)
    hd = D // HEADS_B
    r_parts, i_parts = [], []
    for h in range(HEADS_B):
        xh = _cols(xb, h * hd, (h + 1) * hd)
        r_parts.append(_mm(xh, _rows(w_r, h * hd, (h + 1) * hd), False, False))
        i_parts.append(_mm(xh, _rows(w_i, h * hd, (h + 1) * hd), False, False))
    r = _sigmoid(jnp.concatenate(r_parts, axis=1) + b_r)
    i = _sigmoid(jnp.concatenate(i_parts, axis=1) + b_i)
    log_a = -LRU_C * r * _softplus(-lam)
    a = jnp.exp(log_a)
    inp = jnp.sqrt(_neg_expm1(2.0 * log_a)) * (i * xb)
    h = _lin_scan(a, inp, h0)
    return (_row(h, h.shape[0] - 1),), (_gelu(gate) * h,)


def _f_ssd(carries, halos, xs, params):
    (st,) = carries
    (halo,) = halos
    z, xbc_pre, dt_raw = xs
    conv_w, conv_b, dt_bias, a_log, d_skip, norm_g = params
    t = z.shape[0]
    xc = _conv4(halo, xbc_pre, conv_w, conv_b)
    xbc = xc * _sigmoid(xc)
    x_all = _cols(xbc, 0, D)
    b_all = _cols(xbc, D, D + GROUPS_C * STATE_C)
    c_all = _cols(xbc, D + GROUPS_C * STATE_C, D + 2 * GROUPS_C * STATE_C)
    dt = _softplus(dt_raw + dt_bias)
    adt = dt * (-jnp.exp(a_log))
    acs = _cumsum_rows(adt)
    acs_t = acs.T
    a_last = _row(acs, t - 1)
    lo = lax.broadcasted_iota(jnp.int32, (t, 128), 1) < HEAD_DIM_C
    lo_rows = lax.broadcasted_iota(jnp.int32, (128, STATE_C), 0) < HEAD_DIM_C
    causal = lax.broadcasted_iota(jnp.int32, (t, t), 0) >= lax.broadcasted_iota(jnp.int32, (t, t), 1)
    y_parts, st_parts = [], []
    for g in range(GROUPS_C):
        bg = _cols(b_all, g * STATE_C, (g + 1) * STATE_C)
        cg = _cols(c_all, g * STATE_C, (g + 1) * STATE_C)
        cb = _mm(cg, bg, False, True)
        for pr in range(2):
            pair = 2 * g + pr
            h0, h1 = 2 * pair, 2 * pair + 1
            x2 = _cols(x_all, pair * 128, (pair + 1) * 128)
            ac0, ac1 = _col(acs, h0), _col(acs, h1)
            l0 = jnp.exp(jnp.where(causal, ac0 - _row(acs_t, h0), -1e30))
            l1 = jnp.exp(jnp.where(causal, ac1 - _row(acs_t, h1), -1e30))
            xdt = x2 * jnp.where(lo, _col(dt, h0), _col(dt, h1))
            y_diag = (_mm(cb * l0, jnp.where(lo, xdt, 0.0), False, False)
                      + _mm(cb * l1, jnp.where(lo, 0.0, xdt), False, False))
            al0, al1 = _col(a_last, h0), _col(a_last, h1)
            decay_s = jnp.where(lo, jnp.exp(al0 - ac0), jnp.exp(al1 - ac1))
            s_new = _mm(xdt * decay_s, bg, True, False)
            prev = _rows(st, pair * 128, (pair + 1) * 128)
            y_off = _mm(cg, prev, False, True) * jnp.where(lo, jnp.exp(ac0), jnp.exp(ac1))
            skip = jnp.where(lo, _col(d_skip, h0), _col(d_skip, h1))
            y_parts.append(y_diag + y_off + x2 * skip)
            st_parts.append(prev * jnp.where(lo_rows, jnp.exp(al0), jnp.exp(al1)) + s_new)
    y = jnp.concatenate(y_parts, axis=1) * (z * _sigmoid(z))
    gw = D // GROUPS_C
    yn = []
    for g in range(GROUPS_C):
        yg = _cols(y, g * gw, (g + 1) * gw)
        yn.append(yg * lax.rsqrt(jnp.mean(yg * yg, axis=-1, keepdims=True) + EPS))
    return (jnp.concatenate(st_parts, axis=0),), (jnp.concatenate(yn, axis=1) * norm_g,)


def _f_merge(carries, halos, xs, params):
    pa, pb, pc, g_raw = xs
    (b_gate,) = params
    m = (_sigmoid(_cols(g_raw, 0, D) + _row(b_gate, 0)) * pa
         + _sigmoid(_cols(g_raw, D, 2 * D) + _row(b_gate, 1)) * pb
         + _sigmoid(_cols(g_raw, 2 * D, 3 * D) + _row(b_gate, 2)) * pc)
    return (), (m,)


def _f_loss(carries, halos, xs, params):
    (acc,) = carries
    h, target = xs
    (g,) = params
    err = jnp.square(_rms(h, g) - target)
    part = 0.5 * jnp.sum(jnp.mean(err, axis=-1, keepdims=True), axis=0, keepdims=True)
    return (acc + part,), ()


def _x_specs(xs, t, index_of):
    specs = []
    for arr, off, width in xs:
        assert off % width == 0 and off + width <= arr.shape[1]
        specs.append(pl.BlockSpec((t, width), functools.partial(lambda j, cb: (index_of(j), cb), cb=off // width)))
    return specs


def _halo_specs(xs, halo_idx, t, index_of):
    specs = []
    for xi in halo_idx:
        _, off, width = xs[xi]
        specs.append(pl.BlockSpec(
            (HALO, width),
            functools.partial(lambda j, cb: (jnp.maximum(index_of(j) * (t // HALO) - 1, 0), cb), cb=off // width)))
    return specs


def _full_spec(a):
    return pl.BlockSpec(a.shape, functools.partial(lambda j, nd: (0,) * nd, nd=a.ndim))


def _chunk_fwd(f, name, t, xs, params, outs, halo_idx=(), carry_shapes=(), save_carries=False, final_carries=False):
    s = xs[0][0].shape[0]
    n = s // t
    nx, nh, npar, no, nc = len(xs), len(halo_idx), len(params), len(outs), len(carry_shapes)
    ns = nc if save_carries else 0
    nf = nc if final_carries else 0

    def body(*refs):
        x_refs, refs = refs[:nx], refs[nx:]
        h_refs, refs = refs[:nh], refs[nh:]
        p_refs, refs = refs[:npar], refs[npar:]
        y_refs, refs = refs[:no], refs[no:]
        s_refs, refs = refs[:ns], refs[ns:]
        f_refs, c_refs = refs[:nf], refs[nf:]
        i = pl.program_id(0)

        @pl.when(i == 0)
        def _():
            for c in c_refs:
                c[...] = jnp.zeros_like(c)

        carries = tuple(c[...] for c in c_refs)
        for s_ref, c in zip(s_refs, carries):
            s_ref[0] = c
        halos = tuple(jnp.where(i > 0, h[...].astype(F32), 0.0) for h in h_refs)
        new_c, ys = f(carries, halos, tuple(x[...].astype(F32) for x in x_refs), tuple(p[...] for p in p_refs))
        for y_ref, y in zip(y_refs, ys):
            y_ref[...] = y.astype(y_ref.dtype)
        for c, v in zip(c_refs, new_c):
            c[...] = v
        for f_ref, v in zip(f_refs, new_c):
            f_ref[...] = v

    ident = lambda j: j
    out_shape = [jax.ShapeDtypeStruct((s, w), dt) for w, dt in outs]
    out_specs = [pl.BlockSpec((t, w), lambda j: (j, 0)) for w, _ in outs]
    if save_carries:
        out_shape += [jax.ShapeDtypeStruct((n,) + tuple(cs), F32) for cs in carry_shapes]
        out_specs += [pl.BlockSpec((1,) + tuple(cs), lambda j: (j, 0, 0)) for cs in carry_shapes]
    if final_carries:
        out_shape += [jax.ShapeDtypeStruct(tuple(cs), F32) for cs in carry_shapes]
        out_specs += [pl.BlockSpec(tuple(cs), lambda j: (0, 0)) for cs in carry_shapes]
    res = pl.pallas_call(
        body, name=name, grid=(n,),
        in_specs=_x_specs(xs, t, ident) + _halo_specs(xs, halo_idx, t, ident) + [_full_spec(p) for p in params],
        out_specs=out_specs, out_shape=out_shape,
        scratch_shapes=[pltpu.VMEM(tuple(cs), F32) for cs in carry_shapes],
        compiler_params=_cparams("arbitrary"),
    )(*[x[0] for x in xs], *[xs[xi][0] for xi in halo_idx], *params)
    return res[:no], res[no:no + ns], res[no + ns:]


def _chunk_bwd(f, name, t, xs, params, dys, dx_dtypes, halo_idx=(), saved=(), carry_seed=None):
    s = xs[0][0].shape[0]
    n = s // t
    nx, nh, npar, nc, ndy = len(xs), len(halo_idx), len(params), len(saved), len(dys)

    def body(*refs):
        x_refs, refs = refs[:nx], refs[nx:]
        h_refs, refs = refs[:nh], refs[nh:]
        p_refs, refs = refs[:npar], refs[npar:]
        s_refs, refs = refs[:nc], refs[nc:]
        dy_refs, refs = refs[:ndy], refs[ndy:]
        dx_refs, refs = refs[:nx], refs[nx:]
        dp_refs, refs = refs[:npar], refs[npar:]
        dc_refs, dh_refs = refs[:nc], refs[nc:]
        j = pl.program_id(0)
        i = n - 1 - j

        @pl.when(j == 0)
        def _():
            for dc in dc_refs:
                dc[...] = jnp.zeros_like(dc) if carry_seed is None else carry_seed(dc.shape)
            for r in dh_refs + dp_refs:
                r[...] = jnp.zeros_like(r)

        carries = tuple(s_ref[0] for s_ref in s_refs)
        halos = tuple(jnp.where(i > 0, h[...].astype(F32), 0.0) for h in h_refs)
        x_vals = tuple(x[...].astype(F32) for x in x_refs)
        p_vals = tuple(p[...] for p in p_refs)
        _, vjp = jax.vjp(f, carries, halos, x_vals, p_vals)
        d_car, d_hal, d_xs, d_par = vjp((tuple(dc[...] for dc in dc_refs), tuple(d[...].astype(F32) for d in dy_refs)))
        d_xs = list(d_xs)
        for k, xi in enumerate(halo_idx):
            w = xs[xi][2]
            d_xs[xi] = d_xs[xi] + jnp.concatenate([jnp.zeros((t - HALO, w), F32), dh_refs[k][...]], axis=0)
            dh_refs[k][...] = jnp.where(i > 0, d_hal[k], 0.0)
        for dx_ref, dx in zip(dx_refs, d_xs):
            dx_ref[...] = dx.astype(dx_ref.dtype)
        for dp_ref, dp in zip(dp_refs, d_par):
            dp_ref[...] += dp
        for dc, v in zip(dc_refs, d_car):
            dc[...] = v

    rev = lambda j: n - 1 - j
    in_specs = (_x_specs(xs, t, rev) + _halo_specs(xs, halo_idx, t, rev) + [_full_spec(p) for p in params]
                + [pl.BlockSpec((1,) + a.shape[1:], lambda j: (n - 1 - j, 0, 0)) for a in saved]
                + [pl.BlockSpec((t, d.shape[1]), lambda j: (n - 1 - j, 0)) for d in dys])
    out_shape = ([jax.ShapeDtypeStruct((s, w), dt) for (_, _, w), dt in zip(xs, dx_dtypes)]
                 + [jax.ShapeDtypeStruct(p.shape, F32) for p in params])
    out_specs = ([pl.BlockSpec((t, w), lambda j: (n - 1 - j, 0)) for _, _, w in xs] + [_full_spec(p) for p in params])
    res = pl.pallas_call(
        body, name=name, grid=(n,), in_specs=in_specs, out_specs=out_specs, out_shape=out_shape,
        scratch_shapes=([pltpu.VMEM(a.shape[1:], F32) for a in saved]
                        + [pltpu.VMEM((HALO, xs[xi][2]), F32) for xi in halo_idx]),
        compiler_params=_cparams("arbitrary"),
    )(*[x[0] for x in xs], *[xs[xi][0] for xi in halo_idx], *params, *saved, *dys)
    return res[:nx], res[nx:]


def _tile(dim, pref):
    for cand in pref:
        if dim % cand == 0:
            return cand
    return dim


def _matmul(a, b, name, ta=False, tb=False, outs=(F32,), epilogue=None, extras=()):
    m, k = (a.shape[1], a.shape[0]) if ta else a.shape
    n = b.shape[0] if tb else b.shape[1]
    tm = _tile(m, (1024, 1152, 512, 256, 128))
    tn = _tile(n, (1152, 1024, 512, 256, 128))
    tk = _tile(k, (1024, 1152, 512, 256, 128))
    nk = k // tk
    ne, no = len(extras), len(outs)
    ca, cb = (0 if ta else 1), (1 if tb else 0)

    def body(*refs):
        a_ref, b_ref = refs[:2]
        e_refs = refs[2:2 + ne]
        o_refs = refs[2 + ne:2 + ne + no]
        acc = refs[-1]
        kk = pl.program_id(2)

        @pl.when(kk == 0)
        def _():
            acc[...] = jnp.zeros_like(acc)

        acc[...] += _dg(_bf(a_ref[...]), _bf(b_ref[...]), ca, cb)

        @pl.when(kk == nk - 1)
        def _():
            res = acc[...]
            vals = (res,) if epilogue is None else epilogue(res, *[e[...] for e in e_refs])
            for o_ref, v in zip(o_refs, vals):
                o_ref[...] = v.astype(o_ref.dtype)

    a_spec = pl.BlockSpec((tk, tm), lambda i, j, kk: (kk, i)) if ta else pl.BlockSpec((tm, tk), lambda i, j, kk: (i, kk))
    b_spec = pl.BlockSpec((tn, tk), lambda i, j, kk: (j, kk)) if tb else pl.BlockSpec((tk, tn), lambda i, j, kk: (kk, j))
    mn_spec = pl.BlockSpec((tm, tn), lambda i, j, kk: (i, j))
    res = pl.pallas_call(
        body, name=name, grid=(m // tm, n // tn, nk),
        in_specs=[a_spec, b_spec] + [mn_spec] * ne,
        out_specs=[mn_spec] * no,
        out_shape=[jax.ShapeDtypeStruct((m, n), dt) for dt in outs],
        scratch_shapes=[pltpu.VMEM((tm, tn), F32)],
        compiler_params=_cparams("parallel", "parallel", "arbitrary"),
    )(a, b, *extras)
    return res if no > 1 else res[0]


def _elementwise_block(r, c):
    if r % 8 == 0 and r >= 8:
        return _tile(r, (256, 128, 64, 32, 16, 8)), c
    return r, _tile(c, (256, 128))


PIECE_TILE = 1024


def _piece_steps(pieces):
    steps, s0 = [], 0
    for a in pieces:
        wt = min(a.shape[1], PIECE_TILE)
        assert a.shape[1] % wt == 0
        steps.append((s0, a.shape[1] // wt, wt))
        s0 += a.shape[1] // wt
    return steps, s0


def _matmul_pieces(pieces, b, name):
    steps, n_steps = _piece_steps(pieces)
    s_rows, n = pieces[0].shape[0], b.shape[1]
    tm = _tile(s_rows, (1024, 512, 256, 128))
    tail_rows = steps[-1][2]
    full_steps = n_steps - 1 if tail_rows < PIECE_TILE else n_steps
    b_tail = b[full_steps * PIECE_TILE:]
    np_ = len(pieces)

    def body(*refs):
        a_refs, b_ref, tail_ref, o_ref, acc = refs[:np_], refs[np_], refs[np_ + 1], refs[np_ + 2], refs[np_ + 3]
        s = pl.program_id(1)

        @pl.when(s == 0)
        def _():
            acc[...] = jnp.zeros_like(acc)

        for a_ref, (s0, ns, wt) in zip(a_refs, steps):
            @pl.when((s >= s0) & (s < s0 + ns))
            def _(a_ref=a_ref, wt=wt):
                rhs = b_ref[...] if wt == PIECE_TILE else tail_ref[...]
                acc[...] += _dg(_bf(a_ref[...]), _bf(rhs), 1, 0)

        @pl.when(s == n_steps - 1)
        def _():
            o_ref[...] = acc[...]

    a_specs = [pl.BlockSpec((tm, wt), functools.partial(lambda i, s, s0, ns: (i, jnp.clip(s - s0, 0, ns - 1)), s0=s0, ns=ns))
               for s0, ns, wt in steps]
    return pl.pallas_call(
        body, name=name, grid=(s_rows // tm, n_steps),
        in_specs=a_specs + [pl.BlockSpec((PIECE_TILE, n), lambda i, s: (jnp.minimum(s, full_steps - 1), 0)),
                            pl.BlockSpec(b_tail.shape, lambda i, s: (0, 0))],
        out_specs=pl.BlockSpec((tm, n), lambda i, s: (i, 0)), out_shape=jax.ShapeDtypeStruct((s_rows, n), F32),
        scratch_shapes=[pltpu.VMEM((tm, n), F32)],
        compiler_params=_cparams("parallel", "arbitrary"),
    )(*pieces, b, b_tail)


def _matmul_pieces_t(pieces, b, name):
    steps, n_steps = _piece_steps(pieces)
    s_rows, n = b.shape
    tk = _tile(s_rows, (1024, 512, 256, 128))
    nk = s_rows // tk
    total = sum(a.shape[1] for a in pieces)
    np_ = len(pieces)

    def body(*refs):
        a_refs, b_ref, o_ref, acc = refs[:np_], refs[np_], refs[np_ + 1], refs[np_ + 2]
        s, kk = pl.program_id(0), pl.program_id(1)

        @pl.when(kk == 0)
        def _():
            acc[...] = jnp.zeros_like(acc)

        for a_ref, (s0, ns, wt) in zip(a_refs, steps):
            @pl.when((s >= s0) & (s < s0 + ns))
            def _(a_ref=a_ref, wt=wt):
                acc[0:wt, :] += _dg(_bf(a_ref[...]), _bf(b_ref[...]), 0, 0)

        @pl.when(kk == nk - 1)
        def _():
            o_ref[...] = acc[...]

    def a_index(s, kk, s0, ns):
        active = (s >= s0) & (s < s0 + ns)
        return jnp.where(active, kk, jnp.where(s < s0, 0, nk - 1)), jnp.clip(s - s0, 0, ns - 1)

    a_specs = [pl.BlockSpec((tk, wt), functools.partial(a_index, s0=s0, ns=ns)) for s0, ns, wt in steps]
    return pl.pallas_call(
        body, name=name, grid=(n_steps, nk),
        in_specs=a_specs + [pl.BlockSpec((tk, n), lambda s, kk: (kk, 0))],
        out_specs=pl.BlockSpec((PIECE_TILE, n), lambda s, kk: (s, 0)), out_shape=jax.ShapeDtypeStruct((total, n), F32),
        scratch_shapes=[pltpu.VMEM((PIECE_TILE, n), F32)],
        compiler_params=_cparams("parallel", "arbitrary"),
    )(*pieces, b)


def _adamw_math(g, w, m, v):
    m_new = ADAM_B1 * m + (1.0 - ADAM_B1) * g
    v_new = ADAM_B2 * v + (1.0 - ADAM_B2) * jnp.square(g)
    m_hat = m_new / (1.0 - ADAM_B1 ** ADAM_STEP)
    v_hat = v_new / (1.0 - ADAM_B2 ** ADAM_STEP)
    return -ADAM_LR * (m_hat / (jnp.sqrt(v_hat) + ADAM_EPS) + ADAM_WD * w), m_new, v_new


def _adamw(parts, w, m, v, name):
    nl, r, c = w.shape
    k = parts[0].shape[0]
    tr = _tile(r, (128, 64, 32, 16, 8))
    nb = r // tr

    def body(*refs):
        p_refs, (w_ref, m_ref, v_ref), outs = refs[:nl], refs[nl:nl + 3], refs[nl + 3:]
        layer = pl.program_id(0)
        for q in range(nl):
            @pl.when(layer == q)
            def _(q=q):
                g = p_refs[q][0]
                for j in range(1, k):
                    g = g + p_refs[q][j]
                vals = (g,) + _adamw_math(g, w_ref[0], m_ref[0], v_ref[0])
                for o_ref, val in zip(outs, vals):
                    o_ref[0] = val

    spec = pl.BlockSpec((1, tr, c), lambda l, i: (l, i, 0))
    part_specs = [pl.BlockSpec((k, tr, c), functools.partial(
        lambda l, i, q: (0, jnp.where(l == q, i, jnp.where(l < q, 0, nb - 1)), 0), q=q)) for q in range(nl)]
    return pl.pallas_call(
        body, name=name, grid=(nl, nb), in_specs=part_specs + [spec] * 3,
        out_specs=[spec] * 4, out_shape=[jax.ShapeDtypeStruct((nl, r, c), F32)] * 4,
        compiler_params=_cparams("arbitrary", "arbitrary"),
    )(*parts, w, m, v)


def _adamw_transposed(grads, w, m, v, name):
    nl, r, c = w.shape
    tc = 64
    views = [jnp.transpose(a, (2, 0, 1)) for a in (w, m, v)]

    def body(*refs):
        g_refs, (w_ref, m_ref, v_ref), outs = refs[:nl], refs[nl:nl + 3], refs[nl + 3:]
        for l in range(nl):
            g = g_refs[l][...]
            vals = (g,) + _adamw_math(g, w_ref[:, l, :], m_ref[:, l, :], v_ref[:, l, :])
            for o_ref, val in zip(outs, vals):
                o_ref[:, l, :] = val

    spec = pl.BlockSpec((tc, nl, r), lambda i: (i, 0, 0))
    res = pl.pallas_call(
        body, name=name, grid=(pl.cdiv(c, tc),), in_specs=[pl.BlockSpec((tc, r), lambda i: (i, 0))] * nl + [spec] * 3,
        out_specs=[spec] * 4, out_shape=[jax.ShapeDtypeStruct((c, nl, r), F32)] * 4,
        compiler_params=_cparams("parallel"),
    )(*grads, *views)
    return [jnp.transpose(a, (1, 2, 0)) for a in res]


def _adamw_small(gathered, ws, ms, vs, loss_terms, name):
    n = len(ws)

    def device_sum(ref):
        s = ref[0]
        for j in range(1, N_DEV):
            s = s + ref[j]
        return s

    def body(*refs):
        g_refs, w_refs, m_refs, v_refs = refs[:n], refs[n:2 * n], refs[2 * n:3 * n], refs[3 * n:4 * n]
        loss_ref, outs, loss_out = refs[4 * n], refs[4 * n + 1:-1], refs[-1]
        for i in range(n):
            g = device_sum(g_refs[i])
            vals = (g,) + _adamw_math(g, w_refs[i][...], m_refs[i][...], v_refs[i][...])
            for kind, val in enumerate(vals):
                outs[kind * n + i][...] = val
        loss_out[...] = device_sum(loss_ref)

    res = pl.pallas_call(
        body, name=name,
        out_shape=[jax.ShapeDtypeStruct(a.shape, F32) for _ in range(4) for a in ws] + [
            jax.ShapeDtypeStruct(loss_terms.shape[1:], F32)],
        compiler_params=pltpu.CompilerParams(vmem_limit_bytes=VMEM_LIMIT),
    )(*gathered, *ws, *ms, *vs, loss_terms)
    return [res[kind * n:(kind + 1) * n] for kind in range(4)], res[-1]


ANY = pl.BlockSpec(memory_space=pl.ANY)


def _place():
    return lax.axis_index("x"), lax.axis_index("y"), lax.axis_index("c")


def _handshake(peers):
    barrier = pltpu.get_barrier_semaphore()
    for peer in peers:
        pl.semaphore_signal(barrier, inc=1, device_id=peer, device_id_type=MESH)
    pl.semaphore_wait(barrier, len(peers))


def _comm_call(body, name, inputs, out_shape, scratch, sequencer_id=None, after=()):
    if sequencer_id is None:
        return pl.pallas_call(body, name=name, out_shape=out_shape, in_specs=[ANY] * len(inputs),
                              out_specs=[ANY] * len(out_shape), scratch_shapes=scratch)(*inputs)
    n_in, n_after = len(inputs), len(after)

    def sequencer_body(*refs):
        body(*refs[:n_in], *refs[n_in + n_after:])

    return pl.kernel(
        sequencer_body, out_type=out_shape, mesh=plsc.ScalarSubcoreMesh(axis_name="sequencer", num_cores=1),
        scratch_types=scratch, compiler_params=pltpu.CompilerParams(collective_id=sequencer_id), name=name,
    )(*inputs, *after)


def _all_gather(blocks, name, sequencer_id=None, after=()):
    n = len(blocks)

    def body(*refs):
        x_refs, out_refs = refs[:n], refs[n:2 * n]
        send_sems, recv_sems, local_sems = refs[2 * n:]
        x, y, c = _place()
        me, sibling = (x, y, c), (x, y, 1 - c)
        chips = [(1 - x, y), (x, 1 - y), (1 - x, 1 - y)]
        if sequencer_id is not None:
            _handshake([sibling] + [(*chip, c) for chip in chips])

        def slot(a, px, py, pc):
            return out_refs[a].at[4 * px + 2 * py + pc]

        def copy(a, k, blk, to, src=None):
            return pltpu.make_async_remote_copy(
                src_ref=slot(a, *blk) if src is None else src, dst_ref=slot(a, *blk),
                send_sem=send_sems.at[7 * a + k], recv_sem=recv_sems.at[7 * a + k], device_id=to, device_id_type=MESH)

        mine = [pltpu.make_async_copy(x_refs[a], slot(a, *me), local_sems.at[a]) for a in range(n)]
        first = []
        for a in range(n):
            mine[a].start()
            first.append(copy(a, 0, me, sibling, src=x_refs[a]))
            first += [copy(a, 1 + j, me, (*chip, c), src=x_refs[a]) for j, chip in enumerate(chips)]
        for cp in first:
            cp.start()
        passed = []
        for j, chip in enumerate(chips):
            for a in range(n):
                copy(a, 1 + j, (*chip, c), me).wait_recv()
                passed.append(copy(a, 4 + j, (*chip, c), sibling))
                passed[-1].start()
        for a in range(n):
            copy(a, 0, sibling, me).wait_recv()
            for j, chip in enumerate(chips):
                copy(a, 4 + j, (*chip, 1 - c), me).wait_recv()
        for cp in first + passed:
            cp.wait_send()
        for cp in mine:
            cp.wait()

    return _comm_call(
        body, name, blocks, [jax.ShapeDtypeStruct((N_DEV,) + b.shape, b.dtype) for b in blocks],
        [pltpu.SemaphoreType.DMA((7 * n,)), pltpu.SemaphoreType.DMA((7 * n,)), pltpu.SemaphoreType.DMA((n,))],
        sequencer_id, after)


def _exchange_sibling(gs, name, sequencer_id=None, after=()):
    n = len(gs)

    def body(*refs):
        g_refs, out_refs = refs[:n], refs[n:2 * n]
        send_sems, recv_sems = refs[2 * n:]
        x, y, c = _place()
        if sequencer_id is not None:
            _handshake([(x, y, 1 - c)])
        copies = [pltpu.make_async_remote_copy(
            src_ref=g_refs[a].at[2 * k + 1 - c], dst_ref=out_refs[a].at[k], send_sem=send_sems.at[4 * a + k],
            recv_sem=recv_sems.at[4 * a + k], device_id=(x, y, 1 - c), device_id_type=MESH)
            for a in range(n) for k in range(4)]
        for cp in copies:
            cp.start()
        for cp in copies:
            cp.wait()

    return _comm_call(body, name, gs, [jax.ShapeDtypeStruct((4,) + g.shape[1:], g.dtype) for g in gs],
                      [pltpu.SemaphoreType.DMA((4 * n,)), pltpu.SemaphoreType.DMA((4 * n,))], sequencer_id, after)


def _other_chips():
    x, y = lax.axis_index("x"), lax.axis_index("y")
    return jnp.stack([2 * (1 - x) + y, 2 * x + 1 - y, 2 * (1 - x) + 1 - y]).astype(jnp.int32)


def _add_sibling(g, r1, name, after=()):
    _, r, w = g.shape
    tr, tc = _elementwise_block(r, w)
    chips = _other_chips()
    slabs = 2 * chips + lax.axis_index("c").astype(jnp.int32)

    def body(slab_ref, chip_ref, g_ref, r_ref, *rest):
        rest[-1][...] = (g_ref[...] + r_ref[...]).astype(BF16)

    return pl.pallas_call(
        body, name=name, out_shape=jax.ShapeDtypeStruct((3, r, w), BF16),
        grid_spec=pltpu.PrefetchScalarGridSpec(
            num_scalar_prefetch=2, grid=(3, r // tr, w // tc),
            in_specs=[pl.BlockSpec((1, tr, tc), lambda k, i, j, slab_ref, chip_ref: (slab_ref[k], i, j)),
                      pl.BlockSpec((1, tr, tc), lambda k, i, j, slab_ref, chip_ref: (chip_ref[k], i, j))]
            + [ANY] * len(after),
            out_specs=pl.BlockSpec((1, tr, tc), lambda k, i, j, slab_ref, chip_ref: (k, i, j))),
        compiler_params=_cparams("parallel", "parallel", "parallel"),
    )(slabs, chips, g, r1, *after)


def _exchange_chips(ps, name, sequencer_id=None, after=()):
    n = len(ps)

    def body(*refs):
        p_refs, out_refs = refs[:n], refs[n:2 * n]
        send_sems, recv_sems = refs[2 * n:]
        x, y, c = _place()
        chips = [(1 - x, y), (x, 1 - y), (1 - x, 1 - y)]
        if sequencer_id is not None:
            _handshake([(*chip, c) for chip in chips])
        copies = [pltpu.make_async_remote_copy(
            src_ref=p_refs[a].at[j], dst_ref=out_refs[a].at[j], send_sem=send_sems.at[3 * a + j],
            recv_sem=recv_sems.at[3 * a + j], device_id=(px, py, c), device_id_type=MESH)
            for a in range(n) for j, (px, py) in enumerate(chips)]
        for cp in copies:
            cp.start()
        for cp in copies:
            cp.wait()

    return _comm_call(body, name, ps, [jax.ShapeDtypeStruct(p.shape, p.dtype) for p in ps],
                      [pltpu.SemaphoreType.DMA((3 * n,)), pltpu.SemaphoreType.DMA((3 * n,))], sequencer_id, after)


def _sum_chips(g, r1, others, name):
    _, r, c = g.shape
    tr, tc = _elementwise_block(r, c)
    chip = 2 * lax.axis_index("x") + lax.axis_index("y")
    place = jnp.stack([2 * chip + lax.axis_index("c"), chip]).astype(jnp.int32)

    def body(place_ref, g_ref, r_ref, others_ref, o_ref):
        s = g_ref[0] + r_ref[0]
        for j in range(3):
            s = s + others_ref[j].astype(F32)
        o_ref[...] = s

    return pl.pallas_call(
        body, name=name, out_shape=jax.ShapeDtypeStruct((r, c), F32),
        grid_spec=pltpu.PrefetchScalarGridSpec(
            num_scalar_prefetch=1, grid=(r // tr, c // tc),
            in_specs=[pl.BlockSpec((1, tr, tc), lambda i, j, place_ref: (place_ref[0], i, j)),
                      pl.BlockSpec((1, tr, tc), lambda i, j, place_ref: (place_ref[1], i, j)),
                      pl.BlockSpec((3, tr, tc), lambda i, j, place_ref: (0, i, j))],
            out_specs=pl.BlockSpec((tr, tc), lambda i, j, place_ref: (i, j))),
        compiler_params=_cparams("parallel", "parallel"),
    )(place, g, r1, others)


def _reorder_in_proj(wt):
    za_zb, zc, xbc, dt, gates = (wt[:4096], wt[4096:5120], wt[5120:7168], wt[7168:7184], wt[7184:])
    return jnp.concatenate([za_zb, xbc, gates, zc, dt, jnp.zeros((DT_PAD - 16, wt.shape[1]), wt.dtype)], axis=0)


def _restore_in_proj(wt):
    return jnp.concatenate([wt[:4096], wt[OFF_ZC:OFF_ZC + W_ZC], wt[OFF_XBC:OFF_XBC + W_XBC],
                            wt[OFF_DT:OFF_DT + 16], wt[OFF_GATE:OFF_GATE + W_GATE]], axis=0)


def _lanes_from_devices(g):
    return jnp.moveaxis(g, 0, 1).reshape(g.shape[1], N_DEV * g.shape[2])


def _lanes_to_devices(a):
    return jnp.moveaxis(a.reshape(a.shape[0], N_DEV, a.shape[1] // N_DEV), 1, 0)


def _pad_lanes(a, width):
    return jnp.pad(a, ((0, 0), (0, width - a.shape[1])))


BIG = ("w_in", "w_branch_a", "w_branch_b", "w_branch_c", "w_out", "w_mlp_up", "w_mlp_down")
SMALL_SHARDED = ("b_gate", "lru_conv_w", "ssd_conv_w")
REPLICATED = ("norm_mix_g", "gmlp_ln_g", "gmlp_ln_b", "gmlp_w_s", "gmlp_b_s", "lru_conv_b", "lru_w_r", "lru_b_r",
              "lru_w_i", "lru_b_i", "lru_lambda", "ssd_conv_b", "ssd_dt_bias", "ssd_a_log", "ssd_d", "ssd_norm_g",
              "norm_mlp_g", "final_norm_g")
WEIGHTS = ("norm_mix_g", "w_in", "b_gate", "gmlp_ln_g", "gmlp_ln_b", "gmlp_w_s", "gmlp_b_s", "lru_conv_w", "lru_conv_b",
           "lru_w_r", "lru_b_r", "lru_w_i", "lru_b_i", "lru_lambda", "ssd_conv_w", "ssd_conv_b", "ssd_dt_bias",
           "ssd_a_log", "ssd_d", "ssd_norm_g", "w_branch_a", "w_branch_b", "w_branch_c", "w_out", "norm_mlp_g",
           "w_mlp_up", "w_mlp_down", "final_norm_g")
TRANSPOSED = ("w_in", "w_mlp_up")
SMALL_MATRICES = ("gmlp_w_s", "lru_w_r", "lru_w_i")
SMALL_VECTORS = tuple(n for n in REPLICATED if n not in SMALL_MATRICES)
GRADIENT_GROUPS = {"mlp": ("w_mlp_up", "w_mlp_down"), "mix": ("w_branch_a", "w_branch_b", "w_branch_c", "w_out"),
                   "in": ("w_in",)}


def _layer_params(full, l):
    row = lambda a: a.reshape(1, -1)
    return dict(
        norm_mix_g=row(full["norm_mix_g"][l]), norm_mlp_g=row(full["norm_mlp_g"][l]),
        gmlp=(row(full["gmlp_ln_g"][l]), row(full["gmlp_ln_b"][l]), full["gmlp_w_s"][l].reshape(GROUPS_A * CHUNK, CHUNK),
              full["gmlp_b_s"][l].T),
        lru=(full["lru_conv_w"][l], row(full["lru_conv_b"][l]), full["lru_w_r"][l].reshape(D, D // HEADS_B),
             row(full["lru_b_r"][l]), full["lru_w_i"][l].reshape(D, D // HEADS_B), row(full["lru_b_i"][l]),
             row(full["lru_lambda"][l])),
        ssd=(full["ssd_conv_w"][l], row(full["ssd_conv_b"][l]), _pad_lanes(row(full["ssd_dt_bias"][l]), DT_PAD),
             _pad_lanes(row(full["ssd_a_log"][l]), DT_PAD), _pad_lanes(row(full["ssd_d"][l]), DT_PAD),
             row(full["ssd_norm_g"][l])),
        b_gate=full["b_gate"][l],
    )


def _forward_layer(h, p, wb, l, after_mixers=None):
    tag = f"l{l}"
    t_row = 512
    (hn,), _, _ = _chunk_fwd(_f_rmsnorm, f"norm_mix_{tag}", t_row, [(h, 0, D)], [p["norm_mix_g"]], [(D, BF16)])
    proj = _matmul(hn, wb["w_in"], f"in_proj_{tag}", tb=True)
    (ya,), _, _ = _chunk_fwd(_f_gmlp, f"gmlp_{tag}", CHUNK, [(proj, OFF_ZA, W_ZA)], p["gmlp"], [(D, BF16)])
    lru_xs = [(proj, OFF_ZB, D), (proj, OFF_ZB + D, D)]
    (yb,), lru_saved, _ = _chunk_fwd(_f_lru, f"lru_{tag}", CHUNK, lru_xs, p["lru"], [(D, BF16)], halo_idx=(0,),
                                     carry_shapes=[(1, D)], save_carries=True)
    ssd_xs = [(proj, OFF_ZC, W_ZC), (proj, OFF_XBC, W_XBC), (proj, OFF_DT, W_DT)]
    (yc,), ssd_saved, _ = _chunk_fwd(_f_ssd, f"ssd_{tag}", CHUNK, ssd_xs, p["ssd"], [(D, BF16)], halo_idx=(1,),
                                     carry_shapes=[(HEADS_C * HEAD_DIM_C, STATE_C)], save_carries=True)
    if after_mixers is not None:
        after_mixers(yc)
    pa = _matmul(ya, wb["w_branch_a"], f"branch_a_{tag}")
    pb = _matmul(yb, wb["w_branch_b"], f"branch_b_{tag}")
    pc = _matmul(yc, wb["w_branch_c"], f"branch_c_{tag}")
    merge_xs = [(pa, 0, D), (pb, 0, D), (pc, 0, D), (proj, OFF_GATE, W_GATE)]
    (merged,), _, _ = _chunk_fwd(_f_merge, f"merge_{tag}", t_row, merge_xs, [p["b_gate"]], [(D, BF16)])
    h_mid = _matmul(merged, wb["w_out"], f"out_proj_{tag}", epilogue=lambda acc, res: (acc + res,), extras=(h,))
    (hn2,), _, _ = _chunk_fwd(_f_rmsnorm, f"norm_mlp_{tag}", t_row, [(h_mid, 0, D)], [p["norm_mlp_g"]], [(D, BF16)])

    def relu_sq(acc):
        r = jnp.maximum(acc, 0.0)
        return r, r * r

    relu_up, act = _matmul(hn2, wb["w_mlp_up"], f"mlp_up_{tag}", tb=True, outs=(F32, BF16), epilogue=relu_sq)
    h_out = _matmul(act, wb["w_mlp_down"], f"mlp_down_{tag}", epilogue=lambda acc, res: (acc + res,), extras=(h_mid,))
    saved = dict(h=h, hn=hn, proj=proj, ya=ya, yb=yb, yc=yc, lru_saved=lru_saved, ssd_saved=ssd_saved, pa=pa, pb=pb,
                 pc=pc, merged=merged, h_mid=h_mid, hn2=hn2, relu_up=relu_up, act=act, lru_xs=lru_xs, ssd_xs=ssd_xs,
                 merge_xs=merge_xs)
    return h_out, saved


def _backward_layer(dh, sv, p, wb, l):
    tag = f"l{l}"
    t_row = 512
    g = {}
    d_up = _matmul(dh, wb["w_mlp_down"], f"d_act_{tag}", tb=True, outs=(BF16,),
                   epilogue=lambda acc, r: (acc * (2.0 * r),), extras=(sv["relu_up"],))
    g["w_mlp_down"] = _matmul(sv["act"], dh, f"dw_mlp_down_{tag}", ta=True)
    g["w_mlp_up"] = _matmul(d_up, sv["hn2"], f"dw_mlp_up_{tag}", ta=True)
    d_hn2 = _matmul(d_up, wb["w_mlp_up"], f"d_hn2_{tag}")
    (d_mid,), (g["norm_mlp_g"],) = _chunk_bwd(_f_rmsnorm_res, f"norm_mlp_bwd_{tag}", t_row, [(sv["h_mid"], 0, D)],
                                              [p["norm_mlp_g"]], [d_hn2, dh], [F32])
    d_merged = _matmul(d_mid, wb["w_out"], f"d_merged_{tag}", tb=True)
    g["w_out"] = _matmul(sv["merged"], d_mid, f"dw_out_{tag}", ta=True)
    (d_pa, d_pb, d_pc, d_gate), (g["b_gate"],) = _chunk_bwd(
        _f_merge, f"merge_bwd_{tag}", t_row, sv["merge_xs"], [p["b_gate"]], [d_merged], [BF16] * 4)
    d_y = {}
    for br, d_p, y in (("a", d_pa, sv["ya"]), ("b", d_pb, sv["yb"]), ("c", d_pc, sv["yc"])):
        g[f"w_branch_{br}"] = _matmul(y, d_p, f"dw_branch_{br}_{tag}", ta=True)
        d_y[br] = _matmul(d_p, wb[f"w_branch_{br}"], f"d_y{br}_{tag}", tb=True)
    (d_za,), g_gmlp = _chunk_bwd(_f_gmlp, f"gmlp_bwd_{tag}", CHUNK, [(sv["proj"], OFF_ZA, W_ZA)], p["gmlp"],
                                 [d_y["a"]], [BF16])
    (d_xb, d_gt), g_lru = _chunk_bwd(_f_lru, f"lru_bwd_{tag}", CHUNK, sv["lru_xs"], p["lru"], [d_y["b"]], [BF16] * 2,
                                     halo_idx=(0,), saved=sv["lru_saved"])
    (d_zc, d_xbc, d_dt), g_ssd = _chunk_bwd(_f_ssd, f"ssd_bwd_{tag}", CHUNK, sv["ssd_xs"], p["ssd"], [d_y["c"]],
                                            [BF16] * 3, halo_idx=(1,), saved=sv["ssd_saved"])
    d_proj = [d_za, d_xb, d_gt, d_xbc, d_gate, d_zc, d_dt]
    g["w_in"] = _matmul_pieces_t(d_proj, sv["hn"], f"dw_in_{tag}")
    d_hn = _matmul_pieces(d_proj, wb["w_in"], f"d_hn_{tag}")
    (d_h,), (g["norm_mix_g"],) = _chunk_bwd(_f_rmsnorm_res, f"norm_mix_bwd_{tag}", t_row, [(sv["h"], 0, D)],
                                            [p["norm_mix_g"]], [d_hn, d_mid], [F32])
    g["w_in"] = _restore_in_proj(g["w_in"])
    for n in BIG:
        g[n] = g[n].reshape(N_DEV, g[n].shape[0] // N_DEV, g[n].shape[1])
    g["gmlp_ln_g"], g["gmlp_ln_b"], g["gmlp_w_s"] = g_gmlp[:3]
    g["gmlp_b_s"] = g_gmlp[3].T
    (g["lru_conv_w"], g["lru_conv_b"], g["lru_w_r"], g["lru_b_r"], g["lru_w_i"], g["lru_b_i"], g["lru_lambda"]) = g_lru
    g["ssd_conv_w"], g["ssd_conv_b"] = g_ssd[:2]
    g["ssd_dt_bias"], g["ssd_a_log"], g["ssd_d"] = (a[:, :HEADS_C] for a in g_ssd[2:5])
    g["ssd_norm_g"] = g_ssd[5]
    g["later"] = {"mlp": d_merged, "mix": d_za, "in": d_h}
    return d_h, g


LOSS_ROWS = 512


def _loss_and_grads(h, target, full, layer_weights, first_gathered):
    seq = h.shape[0]
    layer_p = [_layer_params(full, l) for l in range(DEPTH)]
    layer_w = [layer_weights(0, (first_gathered,))]
    saved = []
    for l in range(DEPTH):
        fetch_next = None
        if l + 1 < DEPTH:
            fetch_next = lambda y, l=l: layer_w.append(layer_weights(l + 1, (layer_w[l]["w_mlp_down"], y)))
        h, sv = _forward_layer(h, layer_p[l], layer_w[l], l, fetch_next)
        saved.append(sv)
    final_g = full["final_norm_g"].reshape(1, D)
    loss_xs = [(h, 0, D), (target, 0, D)]
    t_loss = min(LOSS_ROWS, seq)
    _, _, (loss_acc,) = _chunk_fwd(_f_loss, "loss", t_loss, loss_xs, [final_g], [], carry_shapes=[(1, 128)],
                                   final_carries=True)
    zero_acc = jnp.zeros((seq // t_loss, 1, 128), F32)
    seed = lambda shape: (lax.broadcasted_iota(jnp.int32, shape, 1) == 0).astype(F32)
    (dh, _), (g_final,) = _chunk_bwd(_f_loss, "loss_bwd", t_loss, loss_xs, [final_g], [], [F32, F32], saved=[zero_acc],
                                     carry_seed=seed)
    layer_g = [None] * DEPTH
    for l in reversed(range(DEPTH)):
        dh, layer_g[l] = _backward_layer(dh, saved[l], layer_p[l], layer_w[l], l)
    return loss_acc, dh, layer_g, g_final


def _small_views(d):
    views = {n: d[n] for n in REPLICATED}
    views["gmlp_b_s"] = d["gmlp_b_s"].reshape(DEPTH * GROUPS_A, CHUNK)
    views["final_norm_g"] = d["final_norm_g"].reshape(1, D)
    for n in SMALL_MATRICES:
        views[n] = d[n].reshape(DEPTH * D, D // HEADS_B)
    return views


def kernel(x, norm_mix_g, w_in, b_gate, gmlp_ln_g, gmlp_ln_b, gmlp_w_s, gmlp_b_s, lru_conv_w, lru_conv_b, lru_w_r, lru_b_r, lru_w_i, lru_b_i, lru_lambda, ssd_conv_w, ssd_conv_b, ssd_dt_bias, ssd_a_log, ssd_d, ssd_norm_g, w_branch_a, w_branch_b, w_branch_c, w_out, norm_mlp_g, w_mlp_up, w_mlp_down, final_norm_g, loss_target, m_norm_mix_g, m_w_in, m_b_gate, m_gmlp_ln_g, m_gmlp_ln_b, m_gmlp_w_s, m_gmlp_b_s, m_lru_conv_w, m_lru_conv_b, m_lru_w_r, m_lru_b_r, m_lru_w_i, m_lru_b_i, m_lru_lambda, m_ssd_conv_w, m_ssd_conv_b, m_ssd_dt_bias, m_ssd_a_log, m_ssd_d, m_ssd_norm_g, m_w_branch_a, m_w_branch_b, m_w_branch_c, m_w_out, m_norm_mlp_g, m_w_mlp_up, m_w_mlp_down, m_final_norm_g, v_norm_mix_g, v_w_in, v_b_gate, v_gmlp_ln_g, v_gmlp_ln_b, v_gmlp_w_s, v_gmlp_b_s, v_lru_conv_w, v_lru_conv_b, v_lru_w_r, v_lru_b_r, v_lru_w_i, v_lru_b_i, v_lru_lambda, v_ssd_conv_w, v_ssd_conv_b, v_ssd_dt_bias, v_ssd_a_log, v_ssd_d, v_ssd_norm_g, v_w_branch_a, v_w_branch_b, v_w_branch_c, v_w_out, v_norm_mlp_g, v_w_mlp_up, v_w_mlp_down, v_final_norm_g):
    args = locals()
    w = {n: args[n] for n in WEIGHTS}
    m = {n: args["m_" + n] for n in WEIGHTS}
    v = {n: args["v_" + n] for n in WEIGHTS}
    seq = x.shape[1]
    h = x.reshape(seq, D)
    target = loss_target.reshape(seq, D)

    def shard_on_wire(n, l):
        return (w[n][l].T if n in TRANSPOSED else w[n][l]).astype(BF16)

    first = _all_gather([shard_on_wire("w_in", 0)] + [w[n] for n in SMALL_SHARDED], "gather_weights_first")
    full = {n: w[n] for n in REPLICATED}
    for n, g in zip(SMALL_SHARDED, first[1:]):
        full[n] = jnp.stack([_lanes_from_devices(g[:, l]) for l in range(DEPTH)])

    def layer_weights(l, after):
        have = {"w_in": first[0]} if l == 0 else {}
        names = [n for n in BIG if n not in have]
        later = _all_gather([shard_on_wire(n, l) for n in names], f"gather_weights_l{l}", SEQ_GATHER + l, after=after)
        have.update(zip(names, later))
        wl = {n: have[n].reshape(-1, D) for n in BIG}
        wl["w_in"] = _reorder_in_proj(wl["w_in"])
        return wl

    loss_local, dh, layer_g, g_final = _loss_and_grads(h, target, full, layer_weights, first[0])
    grad_x = dh.reshape(x.shape)
    out = {}
    kinds = ("grad", "delta", "new_m", "new_v")

    sequencer_before = {}

    def reduce_scatter(slabs, tag, on_sequencer, later=()):
        keys = list(slabs)
        ids = (SEQ_TO_SIBLING, SEQ_TO_CHIPS) if on_sequencer else (None, None)
        from_sibling = _exchange_sibling([slabs[k] for k in keys], f"grads_to_sibling_{tag}", ids[0],
                                         sequencer_before.get("sibling", ()))
        chip_sums = [_add_sibling(slabs[k], r, f"add_sibling_{k[0]}_{k[1]}", later) for k, r in zip(keys, from_sibling)]
        from_chips = _exchange_chips(chip_sums, f"grads_to_chips_{tag}", ids[1], sequencer_before.get("chips", ()))
        if on_sequencer:
            sequencer_before["sibling"], sequencer_before["chips"] = (from_sibling[0],), (from_chips[0],)
        return {k: _sum_chips(slabs[k], r, others, f"sum_chips_{k[0]}_{k[1]}")
                for k, r, others in zip(keys, from_sibling, from_chips)}

    reduced = {}
    groups = [(l, grp) for l in range(DEPTH - 1, -1, -1) for grp in GRADIENT_GROUPS]
    for l, grp in groups[:-1]:
        slabs = {(n, l): layer_g[l][n] for n in GRADIENT_GROUPS[grp]}
        reduced.update(reduce_scatter(slabs, f"{grp}_l{l}", True, (layer_g[l]["later"][grp],)))
    l, grp = groups[-1]
    slabs = {(n, l): layer_g[l][n] for n in GRADIENT_GROUPS[grp]}
    for n in SMALL_SHARDED:
        slabs[n, None] = jnp.concatenate([_lanes_to_devices(layer_g[l][n]) for l in range(DEPTH)], axis=1)
    g_small = {n: jnp.concatenate([layer_g[l][n] for l in range(DEPTH)], axis=0) for n in REPLICATED[:-1]}
    g_small["final_norm_g"] = g_final
    for n in SMALL_MATRICES:
        slabs[n, None] = g_small[n].reshape(N_DEV, -1, g_small[n].shape[-1])
    reduced.update(reduce_scatter(slabs, "last", True, (layer_g[l]["later"][grp],)))
    for n in BIG:
        grads = [reduced[n, l] for l in range(DEPTH)]
        if n in TRANSPOSED and w[n].shape[-1] % LANES:
            res = _adamw_transposed(grads, w[n], m[n], v[n], f"adamw_{n}")
        else:
            res = _adamw([(g.T if n in TRANSPOSED else g)[None] for g in grads], w[n], m[n], v[n], f"adamw_{n}")
        for kind, a in zip(kinds, res):
            out[kind, n] = a
    for n in SMALL_SHARDED:
        one = lambda a: a.reshape((1, -1, a.shape[-1]))
        for kind, a in zip(kinds, _adamw([reduced[n, None][None]], one(w[n]), one(m[n]), one(v[n]), f"adamw_{n}")):
            out[kind, n] = a.reshape(w[n].shape)

    to_gather = [reduced[n, None] if n in SMALL_MATRICES else g_small[n] for n in REPLICATED]
    *g_gathered, loss_terms = _all_gather(to_gather + [loss_local], "gather_small_grads")
    g_all = dict(zip(REPLICATED, g_gathered))
    wv, mv, vv = _small_views(w), _small_views(m), _small_views(v)
    res, loss_sum = _adamw_small([g_all[n] for n in SMALL_VECTORS],
                                 *[[d[n] for n in SMALL_VECTORS] for d in (wv, mv, vv)], loss_terms, "adamw_vectors")
    loss = loss_sum[0, 0]
    for kind, arrays in zip(kinds, res):
        for n, a in zip(SMALL_VECTORS, arrays):
            out[kind, n] = a.reshape(w[n].shape)
    for n in SMALL_MATRICES:
        g_full = g_all[n].reshape((1, 1) + wv[n].shape)
        for kind, a in zip(kinds, _adamw([g_full[0]], wv[n][None], mv[n][None], vv[n][None], f"adamw_{n}")):
            out[kind, n] = a.reshape(w[n].shape)

    return (loss, grad_x, *[out[kind, n] for kind in kinds for n in WEIGHTS])
```

```python
import functools

import jax
import jax.numpy as jnp
from jax import lax
from jax.experimental import pallas as pl
from jax.experimental.pallas import tpu as pltpu
from jax.experimental.pallas import tpu_sc as plsc

F32 = jnp.float32
BF16 = jnp.bfloat16
MESH = pl.DeviceIdType.MESH

D = 1024
DEPTH = 2
EPS = 1e-6
CHUNK = 128
GROUPS_A = 8
HEADS_B = 8
LRU_C = 8.0
HEADS_C = 16
HEAD_DIM_C = 64
GROUPS_C = 4
STATE_C = 128
HIDDEN = 4 * D
DT_PAD = 128
OFF_ZA, W_ZA = 0, 2048
OFF_ZB, W_ZB = 2048, 2048
OFF_XBC, W_XBC = 4096, 2048
OFF_GATE, W_GATE = 6144, 3072
OFF_ZC, W_ZC = 9216, 1024
OFF_DT, W_DT = 10240, DT_PAD
D_IN_PAD = 10368
D_IN = 10256
N_DEV = 8
SEQ_GATHER = 1
SEQ_TO_SIBLING = SEQ_GATHER + DEPTH
SEQ_TO_CHIPS = SEQ_TO_SIBLING + 1
SEQ_GATHER_SMALL = SEQ_TO_CHIPS + 1

ADAM_LR = 0.001
ADAM_B1 = 0.9
ADAM_B2 = 0.999
ADAM_EPS = 1e-08
ADAM_WD = 0.01
ADAM_STEP = 10

VMEM_LIMIT = 56 * 1024 * 1024
HALO = 8


def _cparams(*sem):
    return pltpu.CompilerParams(dimension_semantics=sem, vmem_limit_bytes=VMEM_LIMIT)


def _bf(x):
    return x.astype(BF16)


def _dg(a, b, ca, cb):
    return lax.dot_general(a, b, (((ca,), (cb,)), ((), ())), preferred_element_type=F32)


@functools.partial(jax.custom_vjp, nondiff_argnums=(2, 3))
def _mm(a, b, ta, tb):
    return _dg(_bf(a), _bf(b), 0 if ta else 1, 1 if tb else 0)


def _mm_fwd(a, b, ta, tb):
    return _mm(a, b, ta, tb), (a, b)


def _mm_bwd(ta, tb, res, g):
    a, b = res
    ma = 1 if ta else 0
    nb = 0 if tb else 1
    gb, ab, bb = _bf(g), _bf(a), _bf(b)
    da = _dg(bb, gb, nb, 1) if ta else _dg(gb, bb, 1, nb)
    db = _dg(gb, ab, 0, ma) if tb else _dg(ab, gb, ma, 0)
    return da.astype(a.dtype), db.astype(b.dtype)


_mm.defvjp(_mm_fwd, _mm_bwd)


def _slices(x, sizes, axis):
    out, lo = [], 0
    for size in sizes:
        out.append(lax.slice_in_dim(x, lo, lo + size, axis=axis))
        lo += size
    return tuple(out)


@functools.partial(jax.custom_vjp, nondiff_argnums=(1,))
def _split_cols(x, widths):
    return _slices(x, widths, 1)


_split_cols.defvjp(lambda x, widths: (_slices(x, widths, 1), None),
                   lambda widths, _, gs: (jnp.concatenate(gs, axis=1),))


@functools.partial(jax.custom_vjp, nondiff_argnums=(1,))
def _split_rows(x, heights):
    return _slices(x, heights, 0)


_split_rows.defvjp(lambda x, heights: (_slices(x, heights, 0), None),
                   lambda heights, _, gs: (jnp.concatenate(gs, axis=0),))


def _col(x, j):
    lane = lax.broadcasted_iota(jnp.int32, x.shape, 1)
    return jnp.sum(jnp.where(lane == j, x, 0.0), axis=1, keepdims=True)


def _row(x, i):
    r = lax.broadcasted_iota(jnp.int32, x.shape, 0)
    return jnp.sum(jnp.where(r == i, x, 0.0), axis=0, keepdims=True)


def _roll_down(x, s):
    return pltpu.roll(x, s, 0)


def _roll_up(x, s):
    return pltpu.roll(x, x.shape[0] - s, 0)


def _row_iota(x):
    return lax.broadcasted_iota(jnp.int32, x.shape, 0)


@functools.partial(jax.custom_vjp, nondiff_argnums=(2,))
def _shift_rows(halo, x, s):
    if s == 0:
        return x
    return _roll_down(jnp.concatenate([halo, x], axis=0), s)[HALO:]


def _shift_rows_fwd(halo, x, s):
    return _shift_rows(halo, x, s), None


def _shift_rows_bwd(s, _, g):
    if s == 0:
        return jnp.zeros((HALO, g.shape[1]), g.dtype), g
    ge = jnp.concatenate([jnp.zeros((HALO, g.shape[1]), g.dtype), g], axis=0)
    de = _roll_up(ge, s)
    return de[:HALO], de[HALO:]


_shift_rows.defvjp(_shift_rows_fwd, _shift_rows_bwd)


SUBLANES = 8
LANES = 128


def _scan_tiles(a, b, carry, up):
    n, c = a.shape
    nt = n // SUBLANES
    a = a.reshape(nt, SUBLANES, c)
    b = b.reshape(nt, SUBLANES, c)
    sub = lax.broadcasted_iota(jnp.int32, a.shape, 1)
    s = 1
    while s < SUBLANES:
        keep = (sub < SUBLANES - s) if up else (sub >= s)
        shift = SUBLANES - s if up else s
        a_sh = jnp.where(keep, pltpu.roll(a, shift, 1), 1.0)
        b_sh = jnp.where(keep, pltpu.roll(b, shift, 1), 0.0)
        b = a * b_sh + b
        a = a * a_sh
        s *= 2
    tiles = [None] * nt
    edge = 0 if up else SUBLANES - 1
    for j in (range(nt - 1, -1, -1) if up else range(nt)):
        tiles[j] = b[j] if carry is None else b[j] + a[j] * carry
        carry = tiles[j][edge:edge + 1, :]
    return jnp.concatenate(tiles, axis=0)


@jax.custom_vjp
def _lin_scan(a, b, h0):
    return _scan_tiles(a, b, h0, up=False)


def _lin_scan_fwd(a, b, h0):
    h = _lin_scan(a, b, h0)
    return h, (a, h0, h)


def _lin_scan_bwd(res, g):
    a, h0, h = res
    n = a.shape[0]
    row = _row_iota(a)
    a_next = jnp.where(row < n - 1, _roll_up(a, 1), 0.0)
    gg = _scan_tiles(a_next, g, None, up=True)
    h_prev = jnp.where(row >= 1, _roll_down(h, 1), h0)
    return gg * h_prev, gg, _row(a * gg, 0)


_lin_scan.defvjp(_lin_scan_fwd, _lin_scan_bwd)


@jax.custom_vjp
def _cumsum_rows(x):
    n = x.shape[0]
    row = _row_iota(x)
    s = 1
    while s < n:
        x = x + jnp.where(row >= s, _roll_down(x, s), 0.0)
        s *= 2
    return x


def _cumsum_rows_fwd(x):
    return _cumsum_rows(x), None


def _cumsum_rows_bwd(_, g):
    n = g.shape[0]
    row = _row_iota(g)
    s = 1
    while s < n:
        g = g + jnp.where(row < n - s, _roll_up(g, s), 0.0)
        s *= 2
    return (g,)


_cumsum_rows.defvjp(_cumsum_rows_fwd, _cumsum_rows_bwd)


def _sigmoid(x):
    return jax.nn.sigmoid(x)


def _softplus(x):
    return jnp.maximum(x, 0.0) + jnp.log1p(jnp.exp(-jnp.abs(x)))


def _gelu(x):
    return jax.nn.gelu(x, approximate=True)


def _neg_expm1(x):
    series = -x * (1.0 + x * (0.5 + x * (1.0 / 6.0 + x * (1.0 / 24.0))))
    return jnp.where(x > -0.01, series, 1.0 - jnp.exp(x))


def _rms(x, g):
    return x * lax.rsqrt(jnp.mean(x * x, axis=-1, keepdims=True) + EPS) * g


def _f_rmsnorm(carries, halos, xs, params):
    (h,) = xs
    (g,) = params
    return (), (_rms(h, g),)


def _f_rmsnorm_res(carries, halos, xs, params):
    (h,) = xs
    (g,) = params
    return (), (_rms(h, g), h)


def _f_gmlp(carries, halos, xs, params):
    (za,) = xs
    ln_g, ln_b, w_s, b_st = params
    u, v = _split_cols(_gelu(za), (D, D))
    vc = v - jnp.mean(v, axis=-1, keepdims=True)
    vn = vc * lax.rsqrt(jnp.mean(vc * vc, axis=-1, keepdims=True) + EPS) * ln_g + ln_b
    q = CHUNK
    causal = lax.broadcasted_iota(jnp.int32, (q, q), 0) >= lax.broadcasted_iota(jnp.int32, (q, q), 1)
    mixed = []
    for g, (w, vg) in enumerate(zip(_split_rows(w_s, (q,) * GROUPS_A), _split_cols(vn, (q,) * GROUPS_A))):
        mixed.append(_mm(jnp.where(causal, w, 0.0), vg, False, False) + _col(b_st, g))
    return (), (u * jnp.concatenate(mixed, axis=1),)


def _conv4(halo, x, w, b):
    y = b + _row(w, 3) * x
    for k in range(3):
        y = y + _row(w, k) * _shift_rows(halo, x, 3 - k)
    return y


def _f_lru(carries, halos, xs, params):
    (h0,) = carries
    (halo,) = halos
    xb_pre, gate = xs
    conv_w, conv_b, w_r, b_r, w_i, b_i, lam = params
    xb = _conv4(halo, xb_pre, conv_w, conv_b)
    hd = D // HEADS_B
    r_parts, i_parts = [], []
    heads = (hd,) * HEADS_B
    for xh, wr, wi in zip(_split_cols(xb, heads), _split_rows(w_r, heads), _split_rows(w_i, heads)):
        r_parts.append(_mm(xh, wr, False, False))
        i_parts.append(_mm(xh, wi, False, False))
    r = _sigmoid(jnp.concatenate(r_parts, axis=1) + b_r)
    i = _sigmoid(jnp.concatenate(i_parts, axis=1) + b_i)
    log_a = -LRU_C * r * _softplus(-lam)
    a = jnp.exp(log_a)
    inp = jnp.sqrt(_neg_expm1(2.0 * log_a)) * (i * xb)
    h = _lin_scan(a, inp, h0)
    return (_row(h, h.shape[0] - 1),), (_gelu(gate) * h,)


def _f_ssd(carries, halos, xs, params):
    (st,) = carries
    (halo,) = halos
    z, xbc_pre, dt_raw = xs
    conv_w, conv_b, dt_bias, a_log, d_skip, norm_g = params
    t = z.shape[0]
    xc = _conv4(halo, xbc_pre, conv_w, conv_b)
    xbc = xc * _sigmoid(xc)
    x_all, b_all, c_all = _split_cols(xbc, (D, GROUPS_C * STATE_C, GROUPS_C * STATE_C))
    x_pairs = _split_cols(x_all, (128,) * (HEADS_C // 2))
    b_groups = _split_cols(b_all, (STATE_C,) * GROUPS_C)
    c_groups = _split_cols(c_all, (STATE_C,) * GROUPS_C)
    st_pairs = _split_rows(st, (128,) * (HEADS_C // 2))
    dt = _softplus(dt_raw + dt_bias)
    adt = dt * (-jnp.exp(a_log))
    acs = _cumsum_rows(adt)
    acs_t = acs.T
    a_last = _row(acs, t - 1)
    lo = lax.broadcasted_iota(jnp.int32, (t, 128), 1) < HEAD_DIM_C
    lo_rows = lax.broadcasted_iota(jnp.int32, (128, STATE_C), 0) < HEAD_DIM_C
    causal = lax.broadcasted_iota(jnp.int32, (t, t), 0) >= lax.broadcasted_iota(jnp.int32, (t, t), 1)
    y_parts, st_parts = [], []
    for g in range(GROUPS_C):
        bg, cg = b_groups[g], c_groups[g]
        cb = _mm(cg, bg, False, True)
        for pr in range(2):
            pair = 2 * g + pr
            h0, h1 = 2 * pair, 2 * pair + 1
            x2 = x_pairs[pair]
            ac0, ac1 = _col(acs, h0), _col(acs, h1)
            l0 = jnp.exp(jnp.where(causal, ac0 - _row(acs_t, h0), -1e30))
            l1 = jnp.exp(jnp.where(causal, ac1 - _row(acs_t, h1), -1e30))
            xdt = x2 * jnp.where(lo, _col(dt, h0), _col(dt, h1))
            y_diag = (_mm(cb * l0, jnp.where(lo, xdt, 0.0), False, False)
                      + _mm(cb * l1, jnp.where(lo, 0.0, xdt), False, False))
            al0, al1 = _col(a_last, h0), _col(a_last, h1)
            decay_s = jnp.where(lo, jnp.exp(al0 - ac0), jnp.exp(al1 - ac1))
            s_new = _mm(xdt * decay_s, bg, True, False)
            prev = st_pairs[pair]
            y_off = _mm(cg, prev, False, True) * jnp.where(lo, jnp.exp(ac0), jnp.exp(ac1))
            skip = jnp.where(lo, _col(d_skip, h0), _col(d_skip, h1))
            y_parts.append(y_diag + y_off + x2 * skip)
            st_parts.append(prev * jnp.where(lo_rows, jnp.exp(al0), jnp.exp(al1)) + s_new)
    y = jnp.concatenate(y_parts, axis=1) * (z * _sigmoid(z))
    gw = D // GROUPS_C
    yn = []
    for yg in _split_cols(y, (gw,) * GROUPS_C):
        yn.append(yg * lax.rsqrt(jnp.mean(yg * yg, axis=-1, keepdims=True) + EPS))
    return (jnp.concatenate(st_parts, axis=0),), (jnp.concatenate(yn, axis=1) * norm_g,)


def _f_merge(carries, halos, xs, params):
    pa, pb, pc, g_raw = xs
    (b_gate,) = params
    ga, gb, gc = _split_cols(g_raw, (D, D, D))
    m = (_sigmoid(ga + _row(b_gate, 0)) * pa + _sigmoid(gb + _row(b_gate, 1)) * pb
         + _sigmoid(gc + _row(b_gate, 2)) * pc)
    return (), (m,)


def _f_loss(carries, halos, xs, params):
    (acc,) = carries
    h, target = xs
    (g,) = params
    err = jnp.square(_rms(h, g) - target)
    part = 0.5 * jnp.sum(jnp.mean(err, axis=-1, keepdims=True), axis=0, keepdims=True)
    return (acc + part,), ()


def _x_specs(xs, t, index_of):
    specs = []
    for arr, off, width in xs:
        assert off % width == 0 and off + width <= arr.shape[1]
        specs.append(pl.BlockSpec((t, width), functools.partial(lambda j, cb: (index_of(j), cb), cb=off // width)))
    return specs


def _halo_specs(xs, halo_idx, t, index_of):
    specs = []
    for xi in halo_idx:
        _, off, width = xs[xi]
        specs.append(pl.BlockSpec(
            (HALO, width),
            functools.partial(lambda j, cb: (jnp.maximum(index_of(j) * (t // HALO) - 1, 0), cb), cb=off // width)))
    return specs


def _full_spec(a):
    return pl.BlockSpec(a.shape, functools.partial(lambda j, nd: (0,) * nd, nd=a.ndim))


def _chunk_fwd(f, name, t, xs, params, outs, halo_idx=(), carry_shapes=(), save_carries=False, final_carries=False):
    s = xs[0][0].shape[0]
    n = s // t
    nx, nh, npar, no, nc = len(xs), len(halo_idx), len(params), len(outs), len(carry_shapes)
    ns = nc if save_carries else 0
    nf = nc if final_carries else 0

    def body(*refs):
        x_refs, refs = refs[:nx], refs[nx:]
        h_refs, refs = refs[:nh], refs[nh:]
        p_refs, refs = refs[:npar], refs[npar:]
        y_refs, refs = refs[:no], refs[no:]
        s_refs, refs = refs[:ns], refs[ns:]
        f_refs, c_refs = refs[:nf], refs[nf:]
        i = pl.program_id(0)

        @pl.when(i == 0)
        def _():
            for c in c_refs:
                c[...] = jnp.zeros_like(c)

        carries = tuple(c[...] for c in c_refs)
        for s_ref, c in zip(s_refs, carries):
            s_ref[0] = c
        halos = tuple(jnp.where(i > 0, h[...].astype(F32), 0.0) for h in h_refs)
        new_c, ys = f(carries, halos, tuple(x[...].astype(F32) for x in x_refs), tuple(p[...] for p in p_refs))
        for y_ref, y in zip(y_refs, ys):
            y_ref[...] = y.astype(y_ref.dtype)
        for c, v in zip(c_refs, new_c):
            c[...] = v
        for f_ref, v in zip(f_refs, new_c):
            f_ref[...] = v

    ident = lambda j: j
    out_shape = [jax.ShapeDtypeStruct((s, w), dt) for w, dt in outs]
    out_specs = [pl.BlockSpec((t, w), lambda j: (j, 0)) for w, _ in outs]
    if save_carries:
        out_shape += [jax.ShapeDtypeStruct((n,) + tuple(cs), F32) for cs in carry_shapes]
        out_specs += [pl.BlockSpec((1,) + tuple(cs), lambda j: (j, 0, 0)) for cs in carry_shapes]
    if final_carries:
        out_shape += [jax.ShapeDtypeStruct(tuple(cs), F32) for cs in carry_shapes]
        out_specs += [pl.BlockSpec(tuple(cs), lambda j: (0, 0)) for cs in carry_shapes]
    res = pl.pallas_call(
        body, name=name, grid=(n,),
        in_specs=_x_specs(xs, t, ident) + _halo_specs(xs, halo_idx, t, ident) + [_full_spec(p) for p in params],
        out_specs=out_specs, out_shape=out_shape,
        scratch_shapes=[pltpu.VMEM(tuple(cs), F32) for cs in carry_shapes],
        compiler_params=_cparams("arbitrary"),
    )(*[x[0] for x in xs], *[xs[xi][0] for xi in halo_idx], *params)
    return res[:no], res[no:no + ns], res[no + ns:]


def _chunk_bwd(f, name, t, xs, params, dys, dx_dtypes, halo_idx=(), saved=(), carry_seed=None):
    s = xs[0][0].shape[0]
    n = s // t
    nx, nh, npar, nc, ndy = len(xs), len(halo_idx), len(params), len(saved), len(dys)

    def body(*refs):
        x_refs, refs = refs[:nx], refs[nx:]
        h_refs, refs = refs[:nh], refs[nh:]
        p_refs, refs = refs[:npar], refs[npar:]
        s_refs, refs = refs[:nc], refs[nc:]
        dy_refs, refs = refs[:ndy], refs[ndy:]
        dx_refs, refs = refs[:nx], refs[nx:]
        dp_refs, refs = refs[:npar], refs[npar:]
        dc_refs, dh_refs = refs[:nc], refs[nc:]
        j = pl.program_id(0)
        i = n - 1 - j

        @pl.when(j == 0)
        def _():
            for dc in dc_refs:
                dc[...] = jnp.zeros_like(dc) if carry_seed is None else carry_seed(dc.shape)
            for r in dh_refs + dp_refs:
                r[...] = jnp.zeros_like(r)

        carries = tuple(s_ref[0] for s_ref in s_refs)
        halos = tuple(jnp.where(i > 0, h[...].astype(F32), 0.0) for h in h_refs)
        x_vals = tuple(x[...].astype(F32) for x in x_refs)
        p_vals = tuple(p[...] for p in p_refs)
        _, vjp = jax.vjp(f, carries, halos, x_vals, p_vals)
        d_car, d_hal, d_xs, d_par = vjp((tuple(dc[...] for dc in dc_refs), tuple(d[...].astype(F32) for d in dy_refs)))
        d_xs = list(d_xs)
        for k, xi in enumerate(halo_idx):
            w = xs[xi][2]
            d_xs[xi] = d_xs[xi] + jnp.concatenate([jnp.zeros((t - HALO, w), F32), dh_refs[k][...]], axis=0)
            dh_refs[k][...] = jnp.where(i > 0, d_hal[k], 0.0)
        for dx_ref, dx in zip(dx_refs, d_xs):
            dx_ref[...] = dx.astype(dx_ref.dtype)
        for dp_ref, dp in zip(dp_refs, d_par):
            dp_ref[...] += dp
        for dc, v in zip(dc_refs, d_car):
            dc[...] = v

    rev = lambda j: n - 1 - j
    in_specs = (_x_specs(xs, t, rev) + _halo_specs(xs, halo_idx, t, rev) + [_full_spec(p) for p in params]
                + [pl.BlockSpec((1,) + a.shape[1:], lambda j: (n - 1 - j, 0, 0)) for a in saved]
                + [pl.BlockSpec((t, d.shape[1]), lambda j: (n - 1 - j, 0)) for d in dys])
    out_shape = ([jax.ShapeDtypeStruct((s, w), dt) for (_, _, w), dt in zip(xs, dx_dtypes)]
                 + [jax.ShapeDtypeStruct(p.shape, F32) for p in params])
    out_specs = ([pl.BlockSpec((t, w), lambda j: (n - 1 - j, 0)) for _, _, w in xs] + [_full_spec(p) for p in params])
    res = pl.pallas_call(
        body, name=name, grid=(n,), in_specs=in_specs, out_specs=out_specs, out_shape=out_shape,
        scratch_shapes=([pltpu.VMEM(a.shape[1:], F32) for a in saved]
                        + [pltpu.VMEM((HALO, xs[xi][2]), F32) for xi in halo_idx]),
        compiler_params=_cparams("arbitrary"),
    )(*[x[0] for x in xs], *[xs[xi][0] for xi in halo_idx], *params, *saved, *dys)
    return res[:nx], res[nx:]


def _tile(dim, pref):
    for cand in pref:
        if dim % cand == 0:
            return cand
    return dim


def _matmul(a, b, name, ta=False, tb=False, outs=(F32,), epilogue=None, extras=()):
    m, k = (a.shape[1], a.shape[0]) if ta else a.shape
    n = b.shape[0] if tb else b.shape[1]
    tm = _tile(m, (1024, 1152, 512, 256, 128))
    tn = _tile(n, (1152, 1024, 512, 256, 128))
    tk = _tile(k, (1024, 1152, 512, 256, 128))
    nk = k // tk
    ne, no = len(extras), len(outs)
    ca, cb = (0 if ta else 1), (1 if tb else 0)

    def body(*refs):
        a_ref, b_ref = refs[:2]
        e_refs = refs[2:2 + ne]
        o_refs = refs[2 + ne:2 + ne + no]

        def finish(res):
            vals = (res,) if epilogue is None else epilogue(res, *[e[...] for e in e_refs])
            for o_ref, v in zip(o_refs, vals):
                o_ref[...] = v.astype(o_ref.dtype)

        part = _dg(_bf(a_ref[...]), _bf(b_ref[...]), ca, cb)
        if nk == 1:
            finish(part)
            return
        acc = refs[-1]
        kk = pl.program_id(2)

        @pl.when(kk == 0)
        def _():
            acc[...] = part

        @pl.when(kk > 0)
        def _():
            acc[...] += part

        @pl.when(kk == nk - 1)
        def _():
            finish(acc[...])

    a_spec = pl.BlockSpec((tk, tm), lambda i, j, kk: (kk, i)) if ta else pl.BlockSpec((tm, tk), lambda i, j, kk: (i, kk))
    b_spec = pl.BlockSpec((tn, tk), lambda i, j, kk: (j, kk)) if tb else pl.BlockSpec((tk, tn), lambda i, j, kk: (kk, j))
    mn_spec = pl.BlockSpec((tm, tn), lambda i, j, kk: (i, j))
    res = pl.pallas_call(
        body, name=name, grid=(m // tm, n // tn, nk),
        in_specs=[a_spec, b_spec] + [mn_spec] * ne,
        out_specs=[mn_spec] * no,
        out_shape=[jax.ShapeDtypeStruct((m, n), dt) for dt in outs],
        scratch_shapes=[pltpu.VMEM((tm, tn), F32)] if nk > 1 else [],
        compiler_params=_cparams("parallel", "parallel", "arbitrary"),
    )(a, b, *extras)
    return res if no > 1 else res[0]


def _elementwise_block(r, c):
    if r % 8 == 0 and r >= 8:
        return _tile(r, (256, 128, 64, 32, 16, 8)), c
    return r, _tile(c, (256, 128))


PIECE_TILE = 1024


def _piece_steps(pieces):
    steps, s0 = [], 0
    for a in pieces:
        wt = min(a.shape[1], PIECE_TILE)
        assert a.shape[1] % wt == 0
        steps.append((s0, a.shape[1] // wt, wt))
        s0 += a.shape[1] // wt
    return steps, s0


def _matmul_pieces(pieces, b, name):
    steps, n_steps = _piece_steps(pieces)
    s_rows, n = pieces[0].shape[0], b.shape[1]
    tm = _tile(s_rows, (1024, 512, 256, 128))
    tail_rows = steps[-1][2]
    full_steps = n_steps - 1 if tail_rows < PIECE_TILE else n_steps
    b_tail = b[full_steps * PIECE_TILE:]
    np_ = len(pieces)

    def body(*refs):
        a_refs, b_ref, tail_ref, o_ref, acc = refs[:np_], refs[np_], refs[np_ + 1], refs[np_ + 2], refs[np_ + 3]
        s = pl.program_id(1)

        @pl.when(s == 0)
        def _():
            acc[...] = jnp.zeros_like(acc)

        for a_ref, (s0, ns, wt) in zip(a_refs, steps):
            @pl.when((s >= s0) & (s < s0 + ns))
            def _(a_ref=a_ref, wt=wt):
                rhs = b_ref[...] if wt == PIECE_TILE else tail_ref[...]
                acc[...] += _dg(_bf(a_ref[...]), _bf(rhs), 1, 0)

        @pl.when(s == n_steps - 1)
        def _():
            o_ref[...] = acc[...]

    a_specs = [pl.BlockSpec((tm, wt), functools.partial(lambda i, s, s0, ns: (i, jnp.clip(s - s0, 0, ns - 1)), s0=s0, ns=ns))
               for s0, ns, wt in steps]
    return pl.pallas_call(
        body, name=name, grid=(s_rows // tm, n_steps),
        in_specs=a_specs + [pl.BlockSpec((PIECE_TILE, n), lambda i, s: (jnp.minimum(s, full_steps - 1), 0)),
                            pl.BlockSpec(b_tail.shape, lambda i, s: (0, 0))],
        out_specs=pl.BlockSpec((tm, n), lambda i, s: (i, 0)), out_shape=jax.ShapeDtypeStruct((s_rows, n), F32),
        scratch_shapes=[pltpu.VMEM((tm, n), F32)],
        compiler_params=_cparams("parallel", "arbitrary"),
    )(*pieces, b, b_tail)


def _matmul_pieces_t(pieces, b, name):
    steps, n_steps = _piece_steps(pieces)
    s_rows, n = b.shape
    tk = _tile(s_rows, (1024, 512, 256, 128))
    nk = s_rows // tk
    total = sum(a.shape[1] for a in pieces)
    np_ = len(pieces)

    def body(*refs):
        a_refs, b_ref, o_ref, acc = refs[:np_], refs[np_], refs[np_ + 1], refs[np_ + 2]
        s, kk = pl.program_id(0), pl.program_id(1)

        @pl.when(kk == 0)
        def _():
            acc[...] = jnp.zeros_like(acc)

        for a_ref, (s0, ns, wt) in zip(a_refs, steps):
            @pl.when((s >= s0) & (s < s0 + ns))
            def _(a_ref=a_ref, wt=wt):
                acc[0:wt, :] += _dg(_bf(a_ref[...]), _bf(b_ref[...]), 0, 0)

        @pl.when(kk == nk - 1)
        def _():
            o_ref[...] = acc[...]

    def a_index(s, kk, s0, ns):
        active = (s >= s0) & (s < s0 + ns)
        return jnp.where(active, kk, jnp.where(s < s0, 0, nk - 1)), jnp.clip(s - s0, 0, ns - 1)

    a_specs = [pl.BlockSpec((tk, wt), functools.partial(a_index, s0=s0, ns=ns)) for s0, ns, wt in steps]
    return pl.pallas_call(
        body, name=name, grid=(n_steps, nk),
        in_specs=a_specs + [pl.BlockSpec((tk, n), lambda s, kk: (kk, 0))],
        out_specs=pl.BlockSpec((PIECE_TILE, n), lambda s, kk: (s, 0)), out_shape=jax.ShapeDtypeStruct((total, n), F32),
        scratch_shapes=[pltpu.VMEM((PIECE_TILE, n), F32)],
        compiler_params=_cparams("parallel", "arbitrary"),
    )(*pieces, b)


def _adamw_math(g, w, m, v):
    m_new = ADAM_B1 * m + (1.0 - ADAM_B1) * g
    v_new = ADAM_B2 * v + (1.0 - ADAM_B2) * jnp.square(g)
    m_hat = m_new / (1.0 - ADAM_B1 ** ADAM_STEP)
    v_hat = v_new / (1.0 - ADAM_B2 ** ADAM_STEP)
    return -ADAM_LR * (m_hat / (jnp.sqrt(v_hat) + ADAM_EPS) + ADAM_WD * w), m_new, v_new


def _adamw(parts, w, m, v, name):
    nl, r, c = w.shape
    k = parts[0].shape[0]
    tr = _tile(r, (128, 64, 32, 16, 8))
    nb = r // tr

    def body(*refs):
        p_refs, (w_ref, m_ref, v_ref), outs = refs[:nl], refs[nl:nl + 3], refs[nl + 3:]
        layer = pl.program_id(0)
        for q in range(nl):
            @pl.when(layer == q)
            def _(q=q):
                g = p_refs[q][0]
                for j in range(1, k):
                    g = g + p_refs[q][j]
                vals = (g,) + _adamw_math(g, w_ref[0], m_ref[0], v_ref[0])
                for o_ref, val in zip(outs, vals):
                    o_ref[0] = val

    spec = pl.BlockSpec((1, tr, c), lambda l, i: (l, i, 0))
    part_specs = [pl.BlockSpec((k, tr, c), functools.partial(
        lambda l, i, q: (0, jnp.where(l == q, i, jnp.where(l < q, 0, nb - 1)), 0), q=q)) for q in range(nl)]
    return pl.pallas_call(
        body, name=name, grid=(nl, nb), in_specs=part_specs + [spec] * 3,
        out_specs=[spec] * 4, out_shape=[jax.ShapeDtypeStruct((nl, r, c), F32)] * 4,
        compiler_params=_cparams("arbitrary", "arbitrary"),
    )(*parts, w, m, v)


def _adamw_transposed(grads, w, m, v, name):
    nl, r, c = w.shape
    tc = 64
    views = [jnp.transpose(a, (2, 0, 1)) for a in (w, m, v)]

    def body(*refs):
        g_refs, (w_ref, m_ref, v_ref), outs = refs[:nl], refs[nl:nl + 3], refs[nl + 3:]
        for l in range(nl):
            g = g_refs[l][...]
            vals = (g,) + _adamw_math(g, w_ref[:, l, :], m_ref[:, l, :], v_ref[:, l, :])
            for o_ref, val in zip(outs, vals):
                o_ref[:, l, :] = val

    spec = pl.BlockSpec((tc, nl, r), lambda i: (i, 0, 0))
    res = pl.pallas_call(
        body, name=name, grid=(pl.cdiv(c, tc),), in_specs=[pl.BlockSpec((tc, r), lambda i: (i, 0))] * nl + [spec] * 3,
        out_specs=[spec] * 4, out_shape=[jax.ShapeDtypeStruct((c, nl, r), F32)] * 4,
        compiler_params=_cparams("parallel"),
    )(*grads, *views)
    return [jnp.transpose(a, (1, 2, 0)) for a in res]


def _adamw_small(gathered, ws, ms, vs, loss_terms, name):
    n = len(ws)

    def device_sum(ref):
        s = ref[0]
        for j in range(1, N_DEV):
            s = s + ref[j]
        return s

    def body(*refs):
        g_refs, w_refs, m_refs, v_refs = refs[:n], refs[n:2 * n], refs[2 * n:3 * n], refs[3 * n:4 * n]
        loss_ref, outs, loss_out = refs[4 * n], refs[4 * n + 1:-1], refs[-1]
        for i in range(n):
            g = device_sum(g_refs[i])
            vals = (g,) + _adamw_math(g, w_refs[i][...], m_refs[i][...], v_refs[i][...])
            for kind, val in enumerate(vals):
                outs[kind * n + i][...] = val
        loss_out[...] = device_sum(loss_ref)

    res = pl.pallas_call(
        body, name=name,
        out_shape=[jax.ShapeDtypeStruct(a.shape, F32) for _ in range(4) for a in ws] + [
            jax.ShapeDtypeStruct(loss_terms.shape[1:], F32)],
        compiler_params=pltpu.CompilerParams(vmem_limit_bytes=VMEM_LIMIT),
    )(*gathered, *ws, *ms, *vs, loss_terms)
    return [res[kind * n:(kind + 1) * n] for kind in range(4)], res[-1]


ANY = pl.BlockSpec(memory_space=pl.ANY)


def _place():
    return lax.axis_index("x"), lax.axis_index("y"), lax.axis_index("c")


def _handshake(peers):
    barrier = pltpu.get_barrier_semaphore()
    for peer in peers:
        pl.semaphore_signal(barrier, inc=1, device_id=peer, device_id_type=MESH)
    pl.semaphore_wait(barrier, len(peers))


def _comm_call(body, name, inputs, out_shape, scratch, sequencer_id=None, after=()):
    if sequencer_id is None:
        return pl.pallas_call(body, name=name, out_shape=out_shape, in_specs=[ANY] * len(inputs),
                              out_specs=[ANY] * len(out_shape), scratch_shapes=scratch)(*inputs)
    n_in, n_after = len(inputs), len(after)

    def sequencer_body(*refs):
        body(*refs[:n_in], *refs[n_in + n_after:])

    return pl.kernel(
        sequencer_body, out_type=out_shape, mesh=plsc.ScalarSubcoreMesh(axis_name="sequencer", num_cores=1),
        scratch_types=scratch, compiler_params=pltpu.CompilerParams(collective_id=sequencer_id), name=name,
    )(*inputs, *after)


def _all_gather(blocks, name, sequencer_id=None, after=()):
    n = len(blocks)

    def body(*refs):
        x_refs, out_refs = refs[:n], refs[n:2 * n]
        send_sems, recv_sems, local_sems = refs[2 * n:]
        x, y, c = _place()
        me, sibling = (x, y, c), (x, y, 1 - c)
        chips = [(1 - x, y), (x, 1 - y), (1 - x, 1 - y)]
        if sequencer_id is not None:
            _handshake([sibling] + [(*chip, c) for chip in chips])

        def slot(a, px, py, pc):
            return out_refs[a].at[4 * px + 2 * py + pc]

        def copy(a, k, blk, to, src=None):
            return pltpu.make_async_remote_copy(
                src_ref=slot(a, *blk) if src is None else src, dst_ref=slot(a, *blk),
                send_sem=send_sems.at[7 * a + k], recv_sem=recv_sems.at[7 * a + k], device_id=to, device_id_type=MESH)

        mine = [pltpu.make_async_copy(x_refs[a], slot(a, *me), local_sems.at[a]) for a in range(n)]
        first = []
        for a in range(n):
            mine[a].start()
            first.append(copy(a, 0, me, sibling, src=x_refs[a]))
            first += [copy(a, 1 + j, me, (*chip, c), src=x_refs[a]) for j, chip in enumerate(chips)]
        for cp in first:
            cp.start()
        passed = []
        for j, chip in enumerate(chips):
            for a in range(n):
                copy(a, 1 + j, (*chip, c), me).wait_recv()
                passed.append(copy(a, 4 + j, (*chip, c), sibling))
                passed[-1].start()
        for a in range(n):
            copy(a, 0, sibling, me).wait_recv()
            for j, chip in enumerate(chips):
                copy(a, 4 + j, (*chip, 1 - c), me).wait_recv()
        for cp in first + passed:
            cp.wait_send()
        for cp in mine:
            cp.wait()

    return _comm_call(
        body, name, blocks, [jax.ShapeDtypeStruct((N_DEV,) + b.shape, b.dtype) for b in blocks],
        [pltpu.SemaphoreType.DMA((7 * n,)), pltpu.SemaphoreType.DMA((7 * n,)), pltpu.SemaphoreType.DMA((n,))],
        sequencer_id, after)


def _exchange_sibling(gs, name, sequencer_id=None, after=()):
    n = len(gs)

    def body(*refs):
        g_refs, out_refs = refs[:n], refs[n:2 * n]
        send_sems, recv_sems = refs[2 * n:]
        x, y, c = _place()
        if sequencer_id is not None:
            _handshake([(x, y, 1 - c)])
        copies = [pltpu.make_async_remote_copy(
            src_ref=g_refs[a].at[2 * k + 1 - c], dst_ref=out_refs[a].at[k], send_sem=send_sems.at[4 * a + k],
            recv_sem=recv_sems.at[4 * a + k], device_id=(x, y, 1 - c), device_id_type=MESH)
            for a in range(n) for k in range(4)]
        for cp in copies:
            cp.start()
        for cp in copies:
            cp.wait()

    return _comm_call(body, name, gs, [jax.ShapeDtypeStruct((4,) + g.shape[1:], g.dtype) for g in gs],
                      [pltpu.SemaphoreType.DMA((4 * n,)), pltpu.SemaphoreType.DMA((4 * n,))], sequencer_id, after)


def _other_chips():
    x, y = lax.axis_index("x"), lax.axis_index("y")
    return jnp.stack([2 * (1 - x) + y, 2 * x + 1 - y, 2 * (1 - x) + 1 - y]).astype(jnp.int32)


def _add_sibling(g, r1, name, after=()):
    _, r, w = g.shape
    tr, tc = _elementwise_block(r, w)
    chips = _other_chips()
    slabs = 2 * chips + lax.axis_index("c").astype(jnp.int32)

    def body(slab_ref, chip_ref, g_ref, r_ref, *rest):
        rest[-1][...] = (g_ref[...] + r_ref[...]).astype(BF16)

    return pl.pallas_call(
        body, name=name, out_shape=jax.ShapeDtypeStruct((3, r, w), BF16),
        grid_spec=pltpu.PrefetchScalarGridSpec(
            num_scalar_prefetch=2, grid=(3, r // tr, w // tc),
            in_specs=[pl.BlockSpec((1, tr, tc), lambda k, i, j, slab_ref, chip_ref: (slab_ref[k], i, j)),
                      pl.BlockSpec((1, tr, tc), lambda k, i, j, slab_ref, chip_ref: (chip_ref[k], i, j))]
            + [ANY] * len(after),
            out_specs=pl.BlockSpec((1, tr, tc), lambda k, i, j, slab_ref, chip_ref: (k, i, j))),
        compiler_params=_cparams("parallel", "parallel", "parallel"),
    )(slabs, chips, g, r1, *after)


def _exchange_chips(ps, name, sequencer_id=None, after=()):
    n = len(ps)

    def body(*refs):
        p_refs, out_refs = refs[:n], refs[n:2 * n]
        send_sems, recv_sems = refs[2 * n:]
        x, y, c = _place()
        chips = [(1 - x, y), (x, 1 - y), (1 - x, 1 - y)]
        if sequencer_id is not None:
            _handshake([(*chip, c) for chip in chips])
        copies = [pltpu.make_async_remote_copy(
            src_ref=p_refs[a].at[j], dst_ref=out_refs[a].at[j], send_sem=send_sems.at[3 * a + j],
            recv_sem=recv_sems.at[3 * a + j], device_id=(px, py, c), device_id_type=MESH)
            for a in range(n) for j, (px, py) in enumerate(chips)]
        for cp in copies:
            cp.start()
        for cp in copies:
            cp.wait()

    return _comm_call(body, name, ps, [jax.ShapeDtypeStruct(p.shape, p.dtype) for p in ps],
                      [pltpu.SemaphoreType.DMA((3 * n,)), pltpu.SemaphoreType.DMA((3 * n,))], sequencer_id, after)


def _sum_chips(g, r1, others, name):
    _, r, c = g.shape
    tr, tc = _elementwise_block(r, c)
    chip = 2 * lax.axis_index("x") + lax.axis_index("y")
    place = jnp.stack([2 * chip + lax.axis_index("c"), chip]).astype(jnp.int32)

    def body(place_ref, g_ref, r_ref, others_ref, o_ref):
        s = g_ref[0] + r_ref[0]
        for j in range(3):
            s = s + others_ref[j].astype(F32)
        o_ref[...] = s

    return pl.pallas_call(
        body, name=name, out_shape=jax.ShapeDtypeStruct((r, c), F32),
        grid_spec=pltpu.PrefetchScalarGridSpec(
            num_scalar_prefetch=1, grid=(r // tr, c // tc),
            in_specs=[pl.BlockSpec((1, tr, tc), lambda i, j, place_ref: (place_ref[0], i, j)),
                      pl.BlockSpec((1, tr, tc), lambda i, j, place_ref: (place_ref[1], i, j)),
                      pl.BlockSpec((3, tr, tc), lambda i, j, place_ref: (0, i, j))],
            out_specs=pl.BlockSpec((tr, tc), lambda i, j, place_ref: (i, j))),
        compiler_params=_cparams("parallel", "parallel"),
    )(place, g, r1, others)


def _reorder_in_proj(wt):
    za_zb, zc, xbc, dt, gates = (wt[:4096], wt[4096:5120], wt[5120:7168], wt[7168:7184], wt[7184:])
    return jnp.concatenate([za_zb, xbc, gates, zc, dt, jnp.zeros((DT_PAD - 16, wt.shape[1]), wt.dtype)], axis=0)


def _restore_in_proj(wt):
    return jnp.concatenate([wt[:4096], wt[OFF_ZC:OFF_ZC + W_ZC], wt[OFF_XBC:OFF_XBC + W_XBC],
                            wt[OFF_DT:OFF_DT + 16], wt[OFF_GATE:OFF_GATE + W_GATE]], axis=0)


def _lanes_from_devices(g):
    return jnp.moveaxis(g, 0, 1).reshape(g.shape[1], N_DEV * g.shape[2])


def _lanes_to_devices(a):
    return jnp.moveaxis(a.reshape(a.shape[0], N_DEV, a.shape[1] // N_DEV), 1, 0)


def _pad_lanes(a, width):
    return jnp.pad(a, ((0, 0), (0, width - a.shape[1])))


BIG = ("w_in", "w_branch_a", "w_branch_b", "w_branch_c", "w_out", "w_mlp_up", "w_mlp_down")
SMALL_SHARDED = ("b_gate", "lru_conv_w", "ssd_conv_w")
REPLICATED = ("norm_mix_g", "gmlp_ln_g", "gmlp_ln_b", "gmlp_w_s", "gmlp_b_s", "lru_conv_b", "lru_w_r", "lru_b_r",
              "lru_w_i", "lru_b_i", "lru_lambda", "ssd_conv_b", "ssd_dt_bias", "ssd_a_log", "ssd_d", "ssd_norm_g",
              "norm_mlp_g", "final_norm_g")
WEIGHTS = ("norm_mix_g", "w_in", "b_gate", "gmlp_ln_g", "gmlp_ln_b", "gmlp_w_s", "gmlp_b_s", "lru_conv_w", "lru_conv_b",
           "lru_w_r", "lru_b_r", "lru_w_i", "lru_b_i", "lru_lambda", "ssd_conv_w", "ssd_conv_b", "ssd_dt_bias",
           "ssd_a_log", "ssd_d", "ssd_norm_g", "w_branch_a", "w_branch_b", "w_branch_c", "w_out", "norm_mlp_g",
           "w_mlp_up", "w_mlp_down", "final_norm_g")
TRANSPOSED = ("w_in", "w_mlp_up")
SMALL_MATRICES = ("gmlp_w_s", "lru_w_r", "lru_w_i")
SMALL_VECTORS = tuple(n for n in REPLICATED if n not in SMALL_MATRICES)
GRADIENT_GROUPS = {"mlp": ("w_mlp_up", "w_mlp_down"), "mix": ("w_branch_a", "w_branch_b", "w_branch_c", "w_out"),
                   "in": ("w_in",)}


def _layer_params(full, l):
    row = lambda a: a.reshape(1, -1)
    return dict(
        norm_mix_g=row(full["norm_mix_g"][l]), norm_mlp_g=row(full["norm_mlp_g"][l]),
        gmlp=(row(full["gmlp_ln_g"][l]), row(full["gmlp_ln_b"][l]), full["gmlp_w_s"][l].reshape(GROUPS_A * CHUNK, CHUNK),
              full["gmlp_b_s"][l].T),
        lru=(full["lru_conv_w"][l], row(full["lru_conv_b"][l]), full["lru_w_r"][l].reshape(D, D // HEADS_B),
             row(full["lru_b_r"][l]), full["lru_w_i"][l].reshape(D, D // HEADS_B), row(full["lru_b_i"][l]),
             row(full["lru_lambda"][l])),
        ssd=(full["ssd_conv_w"][l], row(full["ssd_conv_b"][l]), _pad_lanes(row(full["ssd_dt_bias"][l]), DT_PAD),
             _pad_lanes(row(full["ssd_a_log"][l]), DT_PAD), _pad_lanes(row(full["ssd_d"][l]), DT_PAD),
             row(full["ssd_norm_g"][l])),
        b_gate=full["b_gate"][l],
    )


def _forward_layer(h, p, wb, l, after_mixers=None):
    tag = f"l{l}"
    t_row = 512
    (hn,), _, _ = _chunk_fwd(_f_rmsnorm, f"norm_mix_{tag}", t_row, [(h, 0, D)], [p["norm_mix_g"]], [(D, BF16)])
    proj = _matmul(hn, wb["w_in"], f"in_proj_{tag}", tb=True)
    (ya,), _, _ = _chunk_fwd(_f_gmlp, f"gmlp_{tag}", CHUNK, [(proj, OFF_ZA, W_ZA)], p["gmlp"], [(D, BF16)])
    lru_xs = [(proj, OFF_ZB, D), (proj, OFF_ZB + D, D)]
    (yb,), lru_saved, _ = _chunk_fwd(_f_lru, f"lru_{tag}", CHUNK, lru_xs, p["lru"], [(D, BF16)], halo_idx=(0,),
                                     carry_shapes=[(1, D)], save_carries=True)
    ssd_xs = [(proj, OFF_ZC, W_ZC), (proj, OFF_XBC, W_XBC), (proj, OFF_DT, W_DT)]
    (yc,), ssd_saved, _ = _chunk_fwd(_f_ssd, f"ssd_{tag}", CHUNK, ssd_xs, p["ssd"], [(D, BF16)], halo_idx=(1,),
                                     carry_shapes=[(HEADS_C * HEAD_DIM_C, STATE_C)], save_carries=True)
    if after_mixers is not None:
        after_mixers(yc)
    pa = _matmul(ya, wb["w_branch_a"], f"branch_a_{tag}")
    pb = _matmul(yb, wb["w_branch_b"], f"branch_b_{tag}")
    pc = _matmul(yc, wb["w_branch_c"], f"branch_c_{tag}")
    merge_xs = [(pa, 0, D), (pb, 0, D), (pc, 0, D), (proj, OFF_GATE, W_GATE)]
    (merged,), _, _ = _chunk_fwd(_f_merge, f"merge_{tag}", t_row, merge_xs, [p["b_gate"]], [(D, BF16)])
    h_mid = _matmul(merged, wb["w_out"], f"out_proj_{tag}", epilogue=lambda acc, res: (acc + res,), extras=(h,))
    (hn2,), _, _ = _chunk_fwd(_f_rmsnorm, f"norm_mlp_{tag}", t_row, [(h_mid, 0, D)], [p["norm_mlp_g"]], [(D, BF16)])

    def relu_sq(acc):
        r = jnp.maximum(acc, 0.0)
        return r, r * r

    relu_up, act = _matmul(hn2, wb["w_mlp_up"], f"mlp_up_{tag}", tb=True, outs=(F32, BF16), epilogue=relu_sq)
    h_out = _matmul(act, wb["w_mlp_down"], f"mlp_down_{tag}", epilogue=lambda acc, res: (acc + res,), extras=(h_mid,))
    saved = dict(h=h, hn=hn, proj=proj, ya=ya, yb=yb, yc=yc, lru_saved=lru_saved, ssd_saved=ssd_saved, pa=pa, pb=pb,
                 pc=pc, merged=merged, h_mid=h_mid, hn2=hn2, relu_up=relu_up, act=act, lru_xs=lru_xs, ssd_xs=ssd_xs,
                 merge_xs=merge_xs)
    return h_out, saved


def _backward_layer(dh, sv, p, wb, l):
    tag = f"l{l}"
    t_row = 512
    g = {}
    d_up = _matmul(dh, wb["w_mlp_down"], f"d_act_{tag}", tb=True, outs=(BF16,),
                   epilogue=lambda acc, r: (acc * (2.0 * r),), extras=(sv["relu_up"],))
    g["w_mlp_down"] = _matmul(sv["act"], dh, f"dw_mlp_down_{tag}", ta=True)
    g["w_mlp_up"] = _matmul(d_up, sv["hn2"], f"dw_mlp_up_{tag}", ta=True)
    d_hn2 = _matmul(d_up, wb["w_mlp_up"], f"d_hn2_{tag}")
    (d_mid,), (g["norm_mlp_g"],) = _chunk_bwd(_f_rmsnorm_res, f"norm_mlp_bwd_{tag}", t_row, [(sv["h_mid"], 0, D)],
                                              [p["norm_mlp_g"]], [d_hn2, dh], [F32])
    d_merged = _matmul(d_mid, wb["w_out"], f"d_merged_{tag}", tb=True)
    g["w_out"] = _matmul(sv["merged"], d_mid, f"dw_out_{tag}", ta=True)
    (d_pa, d_pb, d_pc, d_gate), (g["b_gate"],) = _chunk_bwd(
        _f_merge, f"merge_bwd_{tag}", t_row, sv["merge_xs"], [p["b_gate"]], [d_merged], [BF16] * 4)
    d_y = {}
    for br, d_p, y in (("a", d_pa, sv["ya"]), ("b", d_pb, sv["yb"]), ("c", d_pc, sv["yc"])):
        g[f"w_branch_{br}"] = _matmul(y, d_p, f"dw_branch_{br}_{tag}", ta=True)
        d_y[br] = _matmul(d_p, wb[f"w_branch_{br}"], f"d_y{br}_{tag}", tb=True)
    (d_za,), g_gmlp = _chunk_bwd(_f_gmlp, f"gmlp_bwd_{tag}", CHUNK, [(sv["proj"], OFF_ZA, W_ZA)], p["gmlp"],
                                 [d_y["a"]], [BF16])
    (d_xb, d_gt), g_lru = _chunk_bwd(_f_lru, f"lru_bwd_{tag}", CHUNK, sv["lru_xs"], p["lru"], [d_y["b"]], [BF16] * 2,
                                     halo_idx=(0,), saved=sv["lru_saved"])
    (d_zc, d_xbc, d_dt), g_ssd = _chunk_bwd(_f_ssd, f"ssd_bwd_{tag}", CHUNK, sv["ssd_xs"], p["ssd"], [d_y["c"]],
                                            [BF16] * 3, halo_idx=(1,), saved=sv["ssd_saved"])
    d_proj = [d_za, d_xb, d_gt, d_xbc, d_gate, d_zc, d_dt]
    g["w_in"] = _matmul_pieces_t(d_proj, sv["hn"], f"dw_in_{tag}")
    d_hn = _matmul_pieces(d_proj, wb["w_in"], f"d_hn_{tag}")
    (d_h,), (g["norm_mix_g"],) = _chunk_bwd(_f_rmsnorm_res, f"norm_mix_bwd_{tag}", t_row, [(sv["h"], 0, D)],
                                            [p["norm_mix_g"]], [d_hn, d_mid], [F32])
    g["w_in"] = _restore_in_proj(g["w_in"])
    for n in BIG:
        g[n] = g[n].reshape(N_DEV, g[n].shape[0] // N_DEV, g[n].shape[1])
    g["gmlp_ln_g"], g["gmlp_ln_b"], g["gmlp_w_s"] = g_gmlp[:3]
    g["gmlp_b_s"] = g_gmlp[3].T
    (g["lru_conv_w"], g["lru_conv_b"], g["lru_w_r"], g["lru_b_r"], g["lru_w_i"], g["lru_b_i"], g["lru_lambda"]) = g_lru
    g["ssd_conv_w"], g["ssd_conv_b"] = g_ssd[:2]
    g["ssd_dt_bias"], g["ssd_a_log"], g["ssd_d"] = (a[:, :HEADS_C] for a in g_ssd[2:5])
    g["ssd_norm_g"] = g_ssd[5]
    g["later"] = {"mlp": d_merged, "mix": d_za, "in": d_h}
    return d_h, g


LOSS_ROWS = 512


def _loss_and_grads(h, target, full, layer_weights, first_gathered):
    seq = h.shape[0]
    layer_p = [_layer_params(full, l) for l in range(DEPTH)]
    layer_w = [layer_weights(0, (first_gathered,))]
    saved = []
    for l in range(DEPTH):
        fetch_next = None
        if l + 1 < DEPTH:
            fetch_next = lambda y, l=l: layer_w.append(layer_weights(l + 1, (layer_w[l]["w_mlp_down"], y)))
        h, sv = _forward_layer(h, layer_p[l], layer_w[l], l, fetch_next)
        saved.append(sv)
    final_g = full["final_norm_g"].reshape(1, D)
    loss_xs = [(h, 0, D), (target, 0, D)]
    t_loss = min(LOSS_ROWS, seq)
    _, _, (loss_acc,) = _chunk_fwd(_f_loss, "loss", t_loss, loss_xs, [final_g], [], carry_shapes=[(1, 128)],
                                   final_carries=True)
    zero_acc = jnp.zeros((seq // t_loss, 1, 128), F32)
    seed = lambda shape: (lax.broadcasted_iota(jnp.int32, shape, 1) == 0).astype(F32)
    (dh, _), (g_final,) = _chunk_bwd(_f_loss, "loss_bwd", t_loss, loss_xs, [final_g], [], [F32, F32], saved=[zero_acc],
                                     carry_seed=seed)
    layer_g = [None] * DEPTH
    for l in reversed(range(DEPTH)):
        dh, layer_g[l] = _backward_layer(dh, saved[l], layer_p[l], layer_w[l], l)
    return loss_acc, dh, layer_g, g_final


def _small_views(d):
    views = {n: d[n] for n in REPLICATED}
    views["gmlp_b_s"] = d["gmlp_b_s"].reshape(DEPTH * GROUPS_A, CHUNK)
    views["final_norm_g"] = d["final_norm_g"].reshape(1, D)
    for n in SMALL_MATRICES:
        views[n] = d[n].reshape(DEPTH * D, D // HEADS_B)
    return views


def kernel(x, norm_mix_g, w_in, b_gate, gmlp_ln_g, gmlp_ln_b, gmlp_w_s, gmlp_b_s, lru_conv_w, lru_conv_b, lru_w_r, lru_b_r, lru_w_i, lru_b_i, lru_lambda, ssd_conv_w, ssd_conv_b, ssd_dt_bias, ssd_a_log, ssd_d, ssd_norm_g, w_branch_a, w_branch_b, w_branch_c, w_out, norm_mlp_g, w_mlp_up, w_mlp_down, final_norm_g, loss_target, m_norm_mix_g, m_w_in, m_b_gate, m_gmlp_ln_g, m_gmlp_ln_b, m_gmlp_w_s, m_gmlp_b_s, m_lru_conv_w, m_lru_conv_b, m_lru_w_r, m_lru_b_r, m_lru_w_i, m_lru_b_i, m_lru_lambda, m_ssd_conv_w, m_ssd_conv_b, m_ssd_dt_bias, m_ssd_a_log, m_ssd_d, m_ssd_norm_g, m_w_branch_a, m_w_branch_b, m_w_branch_c, m_w_out, m_norm_mlp_g, m_w_mlp_up, m_w_mlp_down, m_final_norm_g, v_norm_mix_g, v_w_in, v_b_gate, v_gmlp_ln_g, v_gmlp_ln_b, v_gmlp_w_s, v_gmlp_b_s, v_lru_conv_w, v_lru_conv_b, v_lru_w_r, v_lru_b_r, v_lru_w_i, v_lru_b_i, v_lru_lambda, v_ssd_conv_w, v_ssd_conv_b, v_ssd_dt_bias, v_ssd_a_log, v_ssd_d, v_ssd_norm_g, v_w_branch_a, v_w_branch_b, v_w_branch_c, v_w_out, v_norm_mlp_g, v_w_mlp_up, v_w_mlp_down, v_final_norm_g):
    args = locals()
    w = {n: args[n] for n in WEIGHTS}
    m = {n: args["m_" + n] for n in WEIGHTS}
    v = {n: args["v_" + n] for n in WEIGHTS}
    seq = x.shape[1]
    h = x.reshape(seq, D)
    target = loss_target.reshape(seq, D)

    def shard_on_wire(n, l):
        return (w[n][l].T if n in TRANSPOSED else w[n][l]).astype(BF16)

    first = _all_gather([shard_on_wire("w_in", 0)] + [w[n] for n in SMALL_SHARDED], "gather_weights_first")
    full = {n: w[n] for n in REPLICATED}
    for n, g in zip(SMALL_SHARDED, first[1:]):
        full[n] = jnp.stack([_lanes_from_devices(g[:, l]) for l in range(DEPTH)])

    def layer_weights(l, after):
        have = {"w_in": first[0]} if l == 0 else {}
        names = [n for n in BIG if n not in have]
        later = _all_gather([shard_on_wire(n, l) for n in names], f"gather_weights_l{l}", SEQ_GATHER + l, after=after)
        have.update(zip(names, later))
        wl = {n: have[n].reshape(-1, D) for n in BIG}
        wl["w_in"] = _reorder_in_proj(wl["w_in"])
        return wl

    loss_local, dh, layer_g, g_final = _loss_and_grads(h, target, full, layer_weights, first[0])
    grad_x = dh.reshape(x.shape)
    out = {}
    kinds = ("grad", "delta", "new_m", "new_v")

    sequencer_before = {}

    def reduce_scatter(slabs, tag, on_sequencer, later=()):
        keys = list(slabs)
        ids = (SEQ_TO_SIBLING, SEQ_TO_CHIPS) if on_sequencer else (None, None)
        from_sibling = _exchange_sibling([slabs[k] for k in keys], f"grads_to_sibling_{tag}", ids[0],
                                         sequencer_before.get("sibling", ()))
        chip_sums = [_add_sibling(slabs[k], r, f"add_sibling_{k[0]}_{k[1]}", later) for k, r in zip(keys, from_sibling)]
        from_chips = _exchange_chips(chip_sums, f"grads_to_chips_{tag}", ids[1], sequencer_before.get("chips", ()))
        if on_sequencer:
            sequencer_before["sibling"], sequencer_before["chips"] = (from_sibling[0],), (from_chips[0],)
        return {k: _sum_chips(slabs[k], r, others, f"sum_chips_{k[0]}_{k[1]}")
                for k, r, others in zip(keys, from_sibling, from_chips)}

    small = {}
    for n in SMALL_SHARDED:
        small[n, None] = jnp.concatenate([_lanes_to_devices(layer_g[l][n]) for l in range(DEPTH)], axis=1)
    g_small = {n: jnp.concatenate([layer_g[l][n] for l in range(DEPTH)], axis=0) for n in REPLICATED[:-1]}
    g_small["final_norm_g"] = g_final
    for n in SMALL_MATRICES:
        small[n, None] = g_small[n].reshape(N_DEV, -1, g_small[n].shape[-1])
    reduced = {}
    groups = [(l, grp) for l in range(DEPTH - 1, -1, -1) for grp in GRADIENT_GROUPS]
    for l, grp in groups:
        if (l, grp) == groups[-1]:
            reduced.update(reduce_scatter(small, "small", True, (layer_g[l][GRADIENT_GROUPS[grp][0]],)))
        slabs = {(n, l): layer_g[l][n] for n in GRADIENT_GROUPS[grp]}
        reduced.update(reduce_scatter(slabs, f"{grp}_l{l}", True, (layer_g[l]["later"][grp],)))
    for n in BIG:
        grads = [reduced[n, l] for l in range(DEPTH)]
        if n in TRANSPOSED and w[n].shape[-1] % LANES:
            res = _adamw_transposed(grads, w[n], m[n], v[n], f"adamw_{n}")
        else:
            res = _adamw([(g.T if n in TRANSPOSED else g)[None] for g in grads], w[n], m[n], v[n], f"adamw_{n}")
        for kind, a in zip(kinds, res):
            out[kind, n] = a
    for n in SMALL_SHARDED:
        one = lambda a: a.reshape((1, -1, a.shape[-1]))
        for kind, a in zip(kinds, _adamw([reduced[n, None][None]], one(w[n]), one(m[n]), one(v[n]), f"adamw_{n}")):
            out[kind, n] = a.reshape(w[n].shape)

    to_gather = [reduced[n, None] if n in SMALL_MATRICES else g_small[n] for n in REPLICATED]
    *g_gathered, loss_terms = _all_gather(to_gather + [loss_local], "gather_small_grads", SEQ_GATHER_SMALL)
    g_all = dict(zip(REPLICATED, g_gathered))
    wv, mv, vv = _small_views(w), _small_views(m), _small_views(v)
    res, loss_sum = _adamw_small([g_all[n] for n in SMALL_VECTORS],
                                 *[[d[n] for n in SMALL_VECTORS] for d in (wv, mv, vv)], loss_terms, "adamw_vectors")
    loss = loss_sum[0, 0]
    for kind, arrays in zip(kinds, res):
        for n, a in zip(SMALL_VECTORS, arrays):
            out[kind, n] = a.reshape(w[n].shape)
    for n in SMALL_MATRICES:
        g_full = g_all[n].reshape((1, 1) + wv[n].shape)
        for kind, a in zip(kinds, _adamw([g_full[0]], wv[n][None], mv[n][None], vv[n][None], f"adamw_{n}")):
            out[kind, n] = a.reshape(w[n].shape)

    return (loss, grad_x, *[out[kind, n] for kind in kinds for n in WEIGHTS])
```

```python
import functools

import jax
import jax.numpy as jnp
from jax import lax
from jax.experimental import pallas as pl
from jax.experimental.pallas import tpu as pltpu
from jax.experimental.pallas import tpu_sc as plsc

F32 = jnp.float32
BF16 = jnp.bfloat16
MESH = pl.DeviceIdType.MESH

D = 1024
DEPTH = 2
EPS = 1e-6
CHUNK = 128
GROUPS_A = 8
HEADS_B = 8
LRU_C = 8.0
HEADS_C = 16
HEAD_DIM_C = 64
GROUPS_C = 4
STATE_C = 128
HIDDEN = 4 * D
DT_PAD = 128
OFF_ZA, W_ZA = 0, 2048
OFF_ZB, W_ZB = 2048, 2048
OFF_XBC, W_XBC = 4096, 2048
OFF_GATE, W_GATE = 6144, 3072
OFF_ZC, W_ZC = 9216, 1024
OFF_DT, W_DT = 10240, DT_PAD
D_IN_PAD = 10368
D_IN = 10256
N_DEV = 8
SEQ_GATHER = 1
SEQ_TO_SIBLING = SEQ_GATHER + DEPTH
SEQ_TO_CHIPS = SEQ_TO_SIBLING + 1
SEQ_GATHER_SMALL = SEQ_TO_CHIPS + 1

ADAM_LR = 0.001
ADAM_B1 = 0.9
ADAM_B2 = 0.999
ADAM_EPS = 1e-08
ADAM_WD = 0.01
ADAM_STEP = 10

VMEM_LIMIT = 56 * 1024 * 1024
HALO = 8


def _cparams(*sem):
    return pltpu.CompilerParams(dimension_semantics=sem, vmem_limit_bytes=VMEM_LIMIT)


def _bf(x):
    return x.astype(BF16)


def _dg(a, b, ca, cb):
    return lax.dot_general(a, b, (((ca,), (cb,)), ((), ())), preferred_element_type=F32)


@functools.partial(jax.custom_vjp, nondiff_argnums=(2, 3))
def _mm(a, b, ta, tb):
    return _dg(_bf(a), _bf(b), 0 if ta else 1, 1 if tb else 0)


def _mm_fwd(a, b, ta, tb):
    return _mm(a, b, ta, tb), (a, b)


def _mm_bwd(ta, tb, res, g):
    a, b = res
    ma = 1 if ta else 0
    nb = 0 if tb else 1
    gb, ab, bb = _bf(g), _bf(a), _bf(b)
    da = _dg(bb, gb, nb, 1) if ta else _dg(gb, bb, 1, nb)
    db = _dg(gb, ab, 0, ma) if tb else _dg(ab, gb, ma, 0)
    return da.astype(a.dtype), db.astype(b.dtype)


_mm.defvjp(_mm_fwd, _mm_bwd)


def _slices(x, sizes, axis):
    out, lo = [], 0
    for size in sizes:
        out.append(lax.slice_in_dim(x, lo, lo + size, axis=axis))
        lo += size
    return tuple(out)


@functools.partial(jax.custom_vjp, nondiff_argnums=(1,))
def _split_cols(x, widths):
    return _slices(x, widths, 1)


_split_cols.defvjp(lambda x, widths: (_slices(x, widths, 1), None),
                   lambda widths, _, gs: (jnp.concatenate(gs, axis=1),))


@functools.partial(jax.custom_vjp, nondiff_argnums=(1,))
def _split_rows(x, heights):
    return _slices(x, heights, 0)


_split_rows.defvjp(lambda x, heights: (_slices(x, heights, 0), None),
                   lambda heights, _, gs: (jnp.concatenate(gs, axis=0),))


def _col(x, j):
    lane = lax.broadcasted_iota(jnp.int32, x.shape, 1)
    return jnp.sum(jnp.where(lane == j, x, 0.0), axis=1, keepdims=True)


def _row(x, i):
    r = lax.broadcasted_iota(jnp.int32, x.shape, 0)
    return jnp.sum(jnp.where(r == i, x, 0.0), axis=0, keepdims=True)


def _roll_down(x, s):
    return pltpu.roll(x, s, 0)


def _roll_up(x, s):
    return pltpu.roll(x, x.shape[0] - s, 0)


def _row_iota(x):
    return lax.broadcasted_iota(jnp.int32, x.shape, 0)


@functools.partial(jax.custom_vjp, nondiff_argnums=(2,))
def _shift_rows(halo, x, s):
    if s == 0:
        return x
    return _roll_down(jnp.concatenate([halo, x], axis=0), s)[HALO:]


def _shift_rows_fwd(halo, x, s):
    return _shift_rows(halo, x, s), None


def _shift_rows_bwd(s, _, g):
    if s == 0:
        return jnp.zeros((HALO, g.shape[1]), g.dtype), g
    ge = jnp.concatenate([jnp.zeros((HALO, g.shape[1]), g.dtype), g], axis=0)
    de = _roll_up(ge, s)
    return de[:HALO], de[HALO:]


_shift_rows.defvjp(_shift_rows_fwd, _shift_rows_bwd)


SUBLANES = 8
LANES = 128


def _scan_tiles(a, b, carry, up):
    n, c = a.shape
    nt = n // SUBLANES
    a = a.reshape(nt, SUBLANES, c)
    b = b.reshape(nt, SUBLANES, c)
    sub = lax.broadcasted_iota(jnp.int32, a.shape, 1)
    s = 1
    while s < SUBLANES:
        keep = (sub < SUBLANES - s) if up else (sub >= s)
        shift = SUBLANES - s if up else s
        a_sh = jnp.where(keep, pltpu.roll(a, shift, 1), 1.0)
        b_sh = jnp.where(keep, pltpu.roll(b, shift, 1), 0.0)
        b = a * b_sh + b
        a = a * a_sh
        s *= 2
    tiles = [None] * nt
    edge = 0 if up else SUBLANES - 1
    for j in (range(nt - 1, -1, -1) if up else range(nt)):
        tiles[j] = b[j] if carry is None else b[j] + a[j] * carry
        carry = tiles[j][edge:edge + 1, :]
    return jnp.concatenate(tiles, axis=0)


@jax.custom_vjp
def _lin_scan(a, b, h0):
    return _scan_tiles(a, b, h0, up=False)


def _lin_scan_fwd(a, b, h0):
    h = _lin_scan(a, b, h0)
    return h, (a, h0, h)


def _lin_scan_bwd(res, g):
    a, h0, h = res
    n = a.shape[0]
    row = _row_iota(a)
    a_next = jnp.where(row < n - 1, _roll_up(a, 1), 0.0)
    gg = _scan_tiles(a_next, g, None, up=True)
    h_prev = jnp.where(row >= 1, _roll_down(h, 1), h0)
    return gg * h_prev, gg, _row(a * gg, 0)


_lin_scan.defvjp(_lin_scan_fwd, _lin_scan_bwd)


@jax.custom_vjp
def _cumsum_rows(x):
    n = x.shape[0]
    row = _row_iota(x)
    s = 1
    while s < n:
        x = x + jnp.where(row >= s, _roll_down(x, s), 0.0)
        s *= 2
    return x


def _cumsum_rows_fwd(x):
    return _cumsum_rows(x), None


def _cumsum_rows_bwd(_, g):
    n = g.shape[0]
    row = _row_iota(g)
    s = 1
    while s < n:
        g = g + jnp.where(row < n - s, _roll_up(g, s), 0.0)
        s *= 2
    return (g,)


_cumsum_rows.defvjp(_cumsum_rows_fwd, _cumsum_rows_bwd)


def _sigmoid(x):
    return jax.nn.sigmoid(x)


def _softplus(x):
    return jnp.maximum(x, 0.0) + jnp.log1p(jnp.exp(-jnp.abs(x)))


def _gelu(x):
    return jax.nn.gelu(x, approximate=True)


def _neg_expm1(x):
    series = -x * (1.0 + x * (0.5 + x * (1.0 / 6.0 + x * (1.0 / 24.0))))
    return jnp.where(x > -0.01, series, 1.0 - jnp.exp(x))


def _rms(x, g):
    return x * lax.rsqrt(jnp.mean(x * x, axis=-1, keepdims=True) + EPS) * g


def _f_rmsnorm(carries, halos, xs, params):
    (h,) = xs
    (g,) = params
    return (), (_rms(h, g),)


def _f_rmsnorm_res(carries, halos, xs, params):
    (h,) = xs
    (g,) = params
    return (), (_rms(h, g), h)


def _f_gmlp(carries, halos, xs, params):
    (za,) = xs
    ln_g, ln_b, w_s, b_st = params
    u, v = _split_cols(_gelu(za), (D, D))
    vc = v - jnp.mean(v, axis=-1, keepdims=True)
    vn = vc * lax.rsqrt(jnp.mean(vc * vc, axis=-1, keepdims=True) + EPS) * ln_g + ln_b
    q = CHUNK
    causal = lax.broadcasted_iota(jnp.int32, (q, q), 0) >= lax.broadcasted_iota(jnp.int32, (q, q), 1)
    mixed = []
    for g, (w, vg) in enumerate(zip(_split_rows(w_s, (q,) * GROUPS_A), _split_cols(vn, (q,) * GROUPS_A))):
        mixed.append(_mm(jnp.where(causal, w, 0.0), vg, False, False) + _col(b_st, g))
    return (), (u * jnp.concatenate(mixed, axis=1),)


def _conv4(halo, x, w, b):
    y = b + _row(w, 3) * x
    for k in range(3):
        y = y + _row(w, k) * _shift_rows(halo, x, 3 - k)
    return y


def _f_lru(carries, halos, xs, params):
    (h0,) = carries
    (halo,) = halos
    xb_pre, gate = xs
    conv_w, conv_b, w_r, b_r, w_i, b_i, lam = params
    xb = _conv4(halo, xb_pre, conv_w, conv_b)
    hd = D // HEADS_B
    r_parts, i_parts = [], []
    heads = (hd,) * HEADS_B
    for xh, wr, wi in zip(_split_cols(xb, heads), _split_rows(w_r, heads), _split_rows(w_i, heads)):
        r_parts.append(_mm(xh, wr, False, False))
        i_parts.append(_mm(xh, wi, False, False))
    r = _sigmoid(jnp.concatenate(r_parts, axis=1) + b_r)
    i = _sigmoid(jnp.concatenate(i_parts, axis=1) + b_i)
    log_a = -LRU_C * r * _softplus(-lam)
    a = jnp.exp(log_a)
    inp = jnp.sqrt(_neg_expm1(2.0 * log_a)) * (i * xb)
    h = _lin_scan(a, inp, h0)
    return (_row(h, h.shape[0] - 1),), (_gelu(gate) * h,)


def _f_ssd(carries, halos, xs, params):
    (st,) = carries
    (halo,) = halos
    z, xbc_pre, dt_raw = xs
    conv_w, conv_b, dt_bias, a_log, d_skip, norm_g = params
    t = z.shape[0]
    xc = _conv4(halo, xbc_pre, conv_w, conv_b)
    xbc = xc * _sigmoid(xc)
    x_all, b_all, c_all = _split_cols(xbc, (D, GROUPS_C * STATE_C, GROUPS_C * STATE_C))
    x_pairs = _split_cols(x_all, (128,) * (HEADS_C // 2))
    b_groups = _split_cols(b_all, (STATE_C,) * GROUPS_C)
    c_groups = _split_cols(c_all, (STATE_C,) * GROUPS_C)
    st_pairs = _split_rows(st, (128,) * (HEADS_C // 2))
    dt = _softplus(dt_raw + dt_bias)
    adt = dt * (-jnp.exp(a_log))
    acs = _cumsum_rows(adt)
    acs_t = acs.T
    a_last = _row(acs, t - 1)
    lo = lax.broadcasted_iota(jnp.int32, (t, 128), 1) < HEAD_DIM_C
    lo_rows = lax.broadcasted_iota(jnp.int32, (128, STATE_C), 0) < HEAD_DIM_C
    causal = lax.broadcasted_iota(jnp.int32, (t, t), 0) >= lax.broadcasted_iota(jnp.int32, (t, t), 1)
    y_parts, st_parts = [], []
    for g in range(GROUPS_C):
        bg, cg = b_groups[g], c_groups[g]
        cb = _mm(cg, bg, False, True)
        for pr in range(2):
            pair = 2 * g + pr
            h0, h1 = 2 * pair, 2 * pair + 1
            x2 = x_pairs[pair]
            ac0, ac1 = _col(acs, h0), _col(acs, h1)
            l0 = jnp.exp(jnp.where(causal, ac0 - _row(acs_t, h0), -1e30))
            l1 = jnp.exp(jnp.where(causal, ac1 - _row(acs_t, h1), -1e30))
            xdt = x2 * jnp.where(lo, _col(dt, h0), _col(dt, h1))
            y_diag = (_mm(cb * l0, jnp.where(lo, xdt, 0.0), False, False)
                      + _mm(cb * l1, jnp.where(lo, 0.0, xdt), False, False))
            al0, al1 = _col(a_last, h0), _col(a_last, h1)
            decay_s = jnp.where(lo, jnp.exp(al0 - ac0), jnp.exp(al1 - ac1))
            s_new = _mm(xdt * decay_s, bg, True, False)
            prev = st_pairs[pair]
            y_off = _mm(cg, prev, False, True) * jnp.where(lo, jnp.exp(ac0), jnp.exp(ac1))
            skip = jnp.where(lo, _col(d_skip, h0), _col(d_skip, h1))
            y_parts.append(y_diag + y_off + x2 * skip)
            st_parts.append(prev * jnp.where(lo_rows, jnp.exp(al0), jnp.exp(al1)) + s_new)
    y = jnp.concatenate(y_parts, axis=1) * (z * _sigmoid(z))
    gw = D // GROUPS_C
    yn = []
    for yg in _split_cols(y, (gw,) * GROUPS_C):
        yn.append(yg * lax.rsqrt(jnp.mean(yg * yg, axis=-1, keepdims=True) + EPS))
    return (jnp.concatenate(st_parts, axis=0),), (jnp.concatenate(yn, axis=1) * norm_g,)


def _f_merge(carries, halos, xs, params):
    pa, pb, pc, g_raw = xs
    (b_gate,) = params
    ga, gb, gc = _split_cols(g_raw, (D, D, D))
    m = (_sigmoid(ga + _row(b_gate, 0)) * pa + _sigmoid(gb + _row(b_gate, 1)) * pb
         + _sigmoid(gc + _row(b_gate, 2)) * pc)
    return (), (m,)


def _f_loss(carries, halos, xs, params):
    (acc,) = carries
    h, target = xs
    (g,) = params
    err = jnp.square(_rms(h, g) - target)
    part = 0.5 * jnp.sum(jnp.mean(err, axis=-1, keepdims=True), axis=0, keepdims=True)
    return (acc + part,), ()


def _x_specs(xs, t, index_of):
    specs = []
    for arr, off, width in xs:
        assert off % width == 0 and off + width <= arr.shape[1]
        specs.append(pl.BlockSpec((t, width), functools.partial(lambda j, cb: (index_of(j), cb), cb=off // width)))
    return specs


def _halo_specs(xs, halo_idx, t, index_of):
    specs = []
    for xi in halo_idx:
        _, off, width = xs[xi]
        specs.append(pl.BlockSpec(
            (HALO, width),
            functools.partial(lambda j, cb: (jnp.maximum(index_of(j) * (t // HALO) - 1, 0), cb), cb=off // width)))
    return specs


def _full_spec(a):
    return pl.BlockSpec(a.shape, functools.partial(lambda j, nd: (0,) * nd, nd=a.ndim))


def _chunk_fwd(f, name, t, xs, params, outs, halo_idx=(), carry_shapes=(), save_carries=False, final_carries=False):
    s = xs[0][0].shape[0]
    n = s // t
    nx, nh, npar, no, nc = len(xs), len(halo_idx), len(params), len(outs), len(carry_shapes)
    ns = nc if save_carries else 0
    nf = nc if final_carries else 0

    def body(*refs):
        x_refs, refs = refs[:nx], refs[nx:]
        h_refs, refs = refs[:nh], refs[nh:]
        p_refs, refs = refs[:npar], refs[npar:]
        y_refs, refs = refs[:no], refs[no:]
        s_refs, refs = refs[:ns], refs[ns:]
        f_refs, c_refs = refs[:nf], refs[nf:]
        i = pl.program_id(0)

        @pl.when(i == 0)
        def _():
            for c in c_refs:
                c[...] = jnp.zeros_like(c)

        carries = tuple(c[...] for c in c_refs)
        for s_ref, c in zip(s_refs, carries):
            s_ref[0] = c
        halos = tuple(jnp.where(i > 0, h[...].astype(F32), 0.0) for h in h_refs)
        new_c, ys = f(carries, halos, tuple(x[...].astype(F32) for x in x_refs), tuple(p[...] for p in p_refs))
        for y_ref, y in zip(y_refs, ys):
            y_ref[...] = y.astype(y_ref.dtype)
        for c, v in zip(c_refs, new_c):
            c[...] = v
        for f_ref, v in zip(f_refs, new_c):
            f_ref[...] = v

    ident = lambda j: j
    out_shape = [jax.ShapeDtypeStruct((s, w), dt) for w, dt in outs]
    out_specs = [pl.BlockSpec((t, w), lambda j: (j, 0)) for w, _ in outs]
    if save_carries:
        out_shape += [jax.ShapeDtypeStruct((n,) + tuple(cs), F32) for cs in carry_shapes]
        out_specs += [pl.BlockSpec((1,) + tuple(cs), lambda j: (j, 0, 0)) for cs in carry_shapes]
    if final_carries:
        out_shape += [jax.ShapeDtypeStruct(tuple(cs), F32) for cs in carry_shapes]
        out_specs += [pl.BlockSpec(tuple(cs), lambda j: (0, 0)) for cs in carry_shapes]
    res = pl.pallas_call(
        body, name=name, grid=(n,),
        in_specs=_x_specs(xs, t, ident) + _halo_specs(xs, halo_idx, t, ident) + [_full_spec(p) for p in params],
        out_specs=out_specs, out_shape=out_shape,
        scratch_shapes=[pltpu.VMEM(tuple(cs), F32) for cs in carry_shapes],
        compiler_params=_cparams("arbitrary"),
    )(*[x[0] for x in xs], *[xs[xi][0] for xi in halo_idx], *params)
    return res[:no], res[no:no + ns], res[no + ns:]


def _chunk_bwd(f, name, t, xs, params, dys, dx_dtypes, halo_idx=(), saved=(), carry_seed=None):
    s = xs[0][0].shape[0]
    n = s // t
    nx, nh, npar, nc, ndy = len(xs), len(halo_idx), len(params), len(saved), len(dys)

    def body(*refs):
        x_refs, refs = refs[:nx], refs[nx:]
        h_refs, refs = refs[:nh], refs[nh:]
        p_refs, refs = refs[:npar], refs[npar:]
        s_refs, refs = refs[:nc], refs[nc:]
        dy_refs, refs = refs[:ndy], refs[ndy:]
        dx_refs, refs = refs[:nx], refs[nx:]
        dp_refs, refs = refs[:npar], refs[npar:]
        dc_refs, dh_refs = refs[:nc], refs[nc:]
        j = pl.program_id(0)
        i = n - 1 - j

        @pl.when(j == 0)
        def _():
            for dc in dc_refs:
                dc[...] = jnp.zeros_like(dc) if carry_seed is None else carry_seed(dc.shape)
            for r in dh_refs + dp_refs:
                r[...] = jnp.zeros_like(r)

        carries = tuple(s_ref[0] for s_ref in s_refs)
        halos = tuple(jnp.where(i > 0, h[...].astype(F32), 0.0) for h in h_refs)
        x_vals = tuple(x[...].astype(F32) for x in x_refs)
        p_vals = tuple(p[...] for p in p_refs)
        _, vjp = jax.vjp(f, carries, halos, x_vals, p_vals)
        d_car, d_hal, d_xs, d_par = vjp((tuple(dc[...] for dc in dc_refs), tuple(d[...].astype(F32) for d in dy_refs)))
        d_xs = list(d_xs)
        for k, xi in enumerate(halo_idx):
            w = xs[xi][2]
            d_xs[xi] = d_xs[xi] + jnp.concatenate([jnp.zeros((t - HALO, w), F32), dh_refs[k][...]], axis=0)
            dh_refs[k][...] = jnp.where(i > 0, d_hal[k], 0.0)
        for dx_ref, dx in zip(dx_refs, d_xs):
            dx_ref[...] = dx.astype(dx_ref.dtype)
        for dp_ref, dp in zip(dp_refs, d_par):
            dp_ref[...] += dp
        for dc, v in zip(dc_refs, d_car):
            dc[...] = v

    rev = lambda j: n - 1 - j
    in_specs = (_x_specs(xs, t, rev) + _halo_specs(xs, halo_idx, t, rev) + [_full_spec(p) for p in params]
                + [pl.BlockSpec((1,) + a.shape[1:], lambda j: (n - 1 - j, 0, 0)) for a in saved]
                + [pl.BlockSpec((t, d.shape[1]), lambda j: (n - 1 - j, 0)) for d in dys])
    out_shape = ([jax.ShapeDtypeStruct((s, w), dt) for (_, _, w), dt in zip(xs, dx_dtypes)]
                 + [jax.ShapeDtypeStruct(p.shape, F32) for p in params])
    out_specs = ([pl.BlockSpec((t, w), lambda j: (n - 1 - j, 0)) for _, _, w in xs] + [_full_spec(p) for p in params])
    res = pl.pallas_call(
        body, name=name, grid=(n,), in_specs=in_specs, out_specs=out_specs, out_shape=out_shape,
        scratch_shapes=([pltpu.VMEM(a.shape[1:], F32) for a in saved]
                        + [pltpu.VMEM((HALO, xs[xi][2]), F32) for xi in halo_idx]),
        compiler_params=_cparams("arbitrary"),
    )(*[x[0] for x in xs], *[xs[xi][0] for xi in halo_idx], *params, *saved, *dys)
    return res[:nx], res[nx:]


def _tile(dim, pref):
    for cand in pref:
        if dim % cand == 0:
            return cand
    return dim


def _matmul(a, b, name, ta=False, tb=False, outs=(F32,), epilogue=None, extras=()):
    m, k = (a.shape[1], a.shape[0]) if ta else a.shape
    n = b.shape[0] if tb else b.shape[1]
    tm = _tile(m, (1024, 1152, 512, 256, 128))
    tn = _tile(n, (1152, 1024, 512, 256, 128))
    tk = _tile(k, (1024, 1152, 512, 256, 128))
    nk = k // tk
    ne, no = len(extras), len(outs)
    ca, cb = (0 if ta else 1), (1 if tb else 0)

    def body(*refs):
        a_ref, b_ref = refs[:2]
        e_refs = refs[2:2 + ne]
        o_refs = refs[2 + ne:2 + ne + no]

        acc = refs[-1]
        kk = pl.program_id(2)

        @pl.when(kk == 0)
        def _():
            acc[...] = jnp.zeros_like(acc)

        acc[...] += _dg(_bf(a_ref[...]), _bf(b_ref[...]), ca, cb)

        @pl.when(kk == nk - 1)
        def _():
            res = acc[...]
            vals = (res,) if epilogue is None else epilogue(res, *[e[...] for e in e_refs])
            for o_ref, v in zip(o_refs, vals):
                o_ref[...] = v.astype(o_ref.dtype)

    a_spec = pl.BlockSpec((tk, tm), lambda i, j, kk: (kk, i)) if ta else pl.BlockSpec((tm, tk), lambda i, j, kk: (i, kk))
    b_spec = pl.BlockSpec((tn, tk), lambda i, j, kk: (j, kk)) if tb else pl.BlockSpec((tk, tn), lambda i, j, kk: (kk, j))
    mn_spec = pl.BlockSpec((tm, tn), lambda i, j, kk: (i, j))
    res = pl.pallas_call(
        body, name=name, grid=(m // tm, n // tn, nk),
        in_specs=[a_spec, b_spec] + [mn_spec] * ne,
        out_specs=[mn_spec] * no,
        out_shape=[jax.ShapeDtypeStruct((m, n), dt) for dt in outs],
        scratch_shapes=[pltpu.VMEM((tm, tn), F32)],
        compiler_params=_cparams("parallel", "parallel", "arbitrary"),
    )(a, b, *extras)
    return res if no > 1 else res[0]


def _elementwise_block(r, c):
    if r % 8 == 0 and r >= 8:
        return _tile(r, (256, 128, 64, 32, 16, 8)), c
    return r, _tile(c, (256, 128))


PIECE_TILE = 1024


def _piece_steps(pieces):
    steps, s0 = [], 0
    for a in pieces:
        wt = min(a.shape[1], PIECE_TILE)
        assert a.shape[1] % wt == 0
        steps.append((s0, a.shape[1] // wt, wt))
        s0 += a.shape[1] // wt
    return steps, s0


def _matmul_pieces(pieces, b, name):
    steps, n_steps = _piece_steps(pieces)
    s_rows, n = pieces[0].shape[0], b.shape[1]
    tm = _tile(s_rows, (1024, 512, 256, 128))
    tail_rows = steps[-1][2]
    full_steps = n_steps - 1 if tail_rows < PIECE_TILE else n_steps
    b_tail = b[full_steps * PIECE_TILE:]
    np_ = len(pieces)

    def body(*refs):
        a_refs, b_ref, tail_ref, o_ref, acc = refs[:np_], refs[np_], refs[np_ + 1], refs[np_ + 2], refs[np_ + 3]
        s = pl.program_id(1)

        @pl.when(s == 0)
        def _():
            acc[...] = jnp.zeros_like(acc)

        for a_ref, (s0, ns, wt) in zip(a_refs, steps):
            @pl.when((s >= s0) & (s < s0 + ns))
            def _(a_ref=a_ref, wt=wt):
                rhs = b_ref[...] if wt == PIECE_TILE else tail_ref[...]
                acc[...] += _dg(_bf(a_ref[...]), _bf(rhs), 1, 0)

        @pl.when(s == n_steps - 1)
        def _():
            o_ref[...] = acc[...]

    a_specs = [pl.BlockSpec((tm, wt), functools.partial(lambda i, s, s0, ns: (i, jnp.clip(s - s0, 0, ns - 1)), s0=s0, ns=ns))
               for s0, ns, wt in steps]
    return pl.pallas_call(
        body, name=name, grid=(s_rows // tm, n_steps),
        in_specs=a_specs + [pl.BlockSpec((PIECE_TILE, n), lambda i, s: (jnp.minimum(s, full_steps - 1), 0)),
                            pl.BlockSpec(b_tail.shape, lambda i, s: (0, 0))],
        out_specs=pl.BlockSpec((tm, n), lambda i, s: (i, 0)), out_shape=jax.ShapeDtypeStruct((s_rows, n), F32),
        scratch_shapes=[pltpu.VMEM((tm, n), F32)],
        compiler_params=_cparams("parallel", "arbitrary"),
    )(*pieces, b, b_tail)


def _matmul_pieces_t(pieces, b, name):
    steps, n_steps = _piece_steps(pieces)
    s_rows, n = b.shape
    tk = _tile(s_rows, (1024, 512, 256, 128))
    nk = s_rows // tk
    total = sum(a.shape[1] for a in pieces)
    np_ = len(pieces)

    def body(*refs):
        a_refs, b_ref, o_ref, acc = refs[:np_], refs[np_], refs[np_ + 1], refs[np_ + 2]
        s, kk = pl.program_id(0), pl.program_id(1)

        @pl.when(kk == 0)
        def _():
            acc[...] = jnp.zeros_like(acc)

        for a_ref, (s0, ns, wt) in zip(a_refs, steps):
            @pl.when((s >= s0) & (s < s0 + ns))
            def _(a_ref=a_ref, wt=wt):
                acc[0:wt, :] += _dg(_bf(a_ref[...]), _bf(b_ref[...]), 0, 0)

        @pl.when(kk == nk - 1)
        def _():
            o_ref[...] = acc[...]

    def a_index(s, kk, s0, ns):
        active = (s >= s0) & (s < s0 + ns)
        return jnp.where(active, kk, jnp.where(s < s0, 0, nk - 1)), jnp.clip(s - s0, 0, ns - 1)

    a_specs = [pl.BlockSpec((tk, wt), functools.partial(a_index, s0=s0, ns=ns)) for s0, ns, wt in steps]
    return pl.pallas_call(
        body, name=name, grid=(n_steps, nk),
        in_specs=a_specs + [pl.BlockSpec((tk, n), lambda s, kk: (kk, 0))],
        out_specs=pl.BlockSpec((PIECE_TILE, n), lambda s, kk: (s, 0)), out_shape=jax.ShapeDtypeStruct((total, n), F32),
        scratch_shapes=[pltpu.VMEM((PIECE_TILE, n), F32)],
        compiler_params=_cparams("parallel", "arbitrary"),
    )(*pieces, b)


def _adamw_math(g, w, m, v):
    m_new = ADAM_B1 * m + (1.0 - ADAM_B1) * g
    v_new = ADAM_B2 * v + (1.0 - ADAM_B2) * jnp.square(g)
    m_hat = m_new / (1.0 - ADAM_B1 ** ADAM_STEP)
    v_hat = v_new / (1.0 - ADAM_B2 ** ADAM_STEP)
    return -ADAM_LR * (m_hat / (jnp.sqrt(v_hat) + ADAM_EPS) + ADAM_WD * w), m_new, v_new


def _adamw(parts, w, m, v, name):
    nl, r, c = w.shape
    k = parts[0].shape[0]
    tr = _tile(r, (128, 64, 32, 16, 8))
    nb = r // tr

    def body(*refs):
        p_refs, (w_ref, m_ref, v_ref), outs = refs[:nl], refs[nl:nl + 3], refs[nl + 3:]
        layer = pl.program_id(0)
        for q in range(nl):
            @pl.when(layer == q)
            def _(q=q):
                g = p_refs[q][0]
                for j in range(1, k):
                    g = g + p_refs[q][j]
                vals = (g,) + _adamw_math(g, w_ref[0], m_ref[0], v_ref[0])
                for o_ref, val in zip(outs, vals):
                    o_ref[0] = val

    spec = pl.BlockSpec((1, tr, c), lambda l, i: (l, i, 0))
    part_specs = [pl.BlockSpec((k, tr, c), functools.partial(
        lambda l, i, q: (0, jnp.where(l == q, i, jnp.where(l < q, 0, nb - 1)), 0), q=q)) for q in range(nl)]
    return pl.pallas_call(
        body, name=name, grid=(nl, nb), in_specs=part_specs + [spec] * 3,
        out_specs=[spec] * 4, out_shape=[jax.ShapeDtypeStruct((nl, r, c), F32)] * 4,
        compiler_params=_cparams("arbitrary", "arbitrary"),
    )(*parts, w, m, v)


def _adamw_transposed(grads, w, m, v, name):
    nl, r, c = w.shape
    tc = 64
    views = [jnp.transpose(a, (2, 0, 1)) for a in (w, m, v)]

    def body(*refs):
        g_refs, (w_ref, m_ref, v_ref), outs = refs[:nl], refs[nl:nl + 3], refs[nl + 3:]
        for l in range(nl):
            g = g_refs[l][...]
            vals = (g,) + _adamw_math(g, w_ref[:, l, :], m_ref[:, l, :], v_ref[:, l, :])
            for o_ref, val in zip(outs, vals):
                o_ref[:, l, :] = val

    spec = pl.BlockSpec((tc, nl, r), lambda i: (i, 0, 0))
    res = pl.pallas_call(
        body, name=name, grid=(pl.cdiv(c, tc),), in_specs=[pl.BlockSpec((tc, r), lambda i: (i, 0))] * nl + [spec] * 3,
        out_specs=[spec] * 4, out_shape=[jax.ShapeDtypeStruct((c, nl, r), F32)] * 4,
        compiler_params=_cparams("parallel"),
    )(*grads, *views)
    return [jnp.transpose(a, (1, 2, 0)) for a in res]


def _adamw_small(gathered, ws, ms, vs, loss_terms, name):
    n = len(ws)

    def device_sum(ref):
        s = ref[0]
        for j in range(1, N_DEV):
            s = s + ref[j]
        return s

    def body(*refs):
        g_refs, w_refs, m_refs, v_refs = refs[:n], refs[n:2 * n], refs[2 * n:3 * n], refs[3 * n:4 * n]
        loss_ref, outs, loss_out = refs[4 * n], refs[4 * n + 1:-1], refs[-1]
        for i in range(n):
            g = device_sum(g_refs[i])
            vals = (g,) + _adamw_math(g, w_refs[i][...], m_refs[i][...], v_refs[i][...])
            for kind, val in enumerate(vals):
                outs[kind * n + i][...] = val
        loss_out[...] = device_sum(loss_ref)

    res = pl.pallas_call(
        body, name=name,
        out_shape=[jax.ShapeDtypeStruct(a.shape, F32) for _ in range(4) for a in ws] + [
            jax.ShapeDtypeStruct(loss_terms.shape[1:], F32)],
        compiler_params=pltpu.CompilerParams(vmem_limit_bytes=VMEM_LIMIT),
    )(*gathered, *ws, *ms, *vs, loss_terms)
    return [res[kind * n:(kind + 1) * n] for kind in range(4)], res[-1]


ANY = pl.BlockSpec(memory_space=pl.ANY)


def _place():
    return lax.axis_index("x"), lax.axis_index("y"), lax.axis_index("c")


def _handshake(peers):
    barrier = pltpu.get_barrier_semaphore()
    for peer in peers:
        pl.semaphore_signal(barrier, inc=1, device_id=peer, device_id_type=MESH)
    pl.semaphore_wait(barrier, len(peers))


def _comm_call(body, name, inputs, out_shape, scratch, sequencer_id=None, after=()):
    if sequencer_id is None:
        return pl.pallas_call(body, name=name, out_shape=out_shape, in_specs=[ANY] * len(inputs),
                              out_specs=[ANY] * len(out_shape), scratch_shapes=scratch)(*inputs)
    n_in, n_after = len(inputs), len(after)

    def sequencer_body(*refs):
        body(*refs[:n_in], *refs[n_in + n_after:])

    return pl.kernel(
        sequencer_body, out_type=out_shape, mesh=plsc.ScalarSubcoreMesh(axis_name="sequencer", num_cores=1),
        scratch_types=scratch, compiler_params=pltpu.CompilerParams(collective_id=sequencer_id), name=name,
    )(*inputs, *after)


def _all_gather(blocks, name, sequencer_id=None, after=()):
    n = len(blocks)

    def body(*refs):
        x_refs, out_refs = refs[:n], refs[n:2 * n]
        send_sems, recv_sems, local_sems = refs[2 * n:]
        x, y, c = _place()
        me, sibling = (x, y, c), (x, y, 1 - c)
        chips = [(1 - x, y), (x, 1 - y), (1 - x, 1 - y)]
        if sequencer_id is not None:
            _handshake([sibling] + [(*chip, c) for chip in chips])

        def slot(a, px, py, pc):
            return out_refs[a].at[4 * px + 2 * py + pc]

        def copy(a, k, blk, to, src=None):
            return pltpu.make_async_remote_copy(
                src_ref=slot(a, *blk) if src is None else src, dst_ref=slot(a, *blk),
                send_sem=send_sems.at[7 * a + k], recv_sem=recv_sems.at[7 * a + k], device_id=to, device_id_type=MESH)

        mine = [pltpu.make_async_copy(x_refs[a], slot(a, *me), local_sems.at[a]) for a in range(n)]
        first = []
        for a in range(n):
            mine[a].start()
            first.append(copy(a, 0, me, sibling, src=x_refs[a]))
            first += [copy(a, 1 + j, me, (*chip, c), src=x_refs[a]) for j, chip in enumerate(chips)]
        for cp in first:
            cp.start()
        passed = []
        for j, chip in enumerate(chips):
            for a in range(n):
                copy(a, 1 + j, (*chip, c), me).wait_recv()
                passed.append(copy(a, 4 + j, (*chip, c), sibling))
                passed[-1].start()
        for a in range(n):
            copy(a, 0, sibling, me).wait_recv()
            for j, chip in enumerate(chips):
                copy(a, 4 + j, (*chip, 1 - c), me).wait_recv()
        for cp in first + passed:
            cp.wait_send()
        for cp in mine:
            cp.wait()

    return _comm_call(
        body, name, blocks, [jax.ShapeDtypeStruct((N_DEV,) + b.shape, b.dtype) for b in blocks],
        [pltpu.SemaphoreType.DMA((7 * n,)), pltpu.SemaphoreType.DMA((7 * n,)), pltpu.SemaphoreType.DMA((n,))],
        sequencer_id, after)


def _exchange_sibling(gs, name, sequencer_id=None, after=()):
    n = len(gs)

    def body(*refs):
        g_refs, out_refs = refs[:n], refs[n:2 * n]
        send_sems, recv_sems = refs[2 * n:]
        x, y, c = _place()
        if sequencer_id is not None:
            _handshake([(x, y, 1 - c)])
        copies = [pltpu.make_async_remote_copy(
            src_ref=g_refs[a].at[2 * k + 1 - c], dst_ref=out_refs[a].at[k], send_sem=send_sems.at[4 * a + k],
            recv_sem=recv_sems.at[4 * a + k], device_id=(x, y, 1 - c), device_id_type=MESH)
            for a in range(n) for k in range(4)]
        for cp in copies:
            cp.start()
        for cp in copies:
            cp.wait()

    return _comm_call(body, name, gs, [jax.ShapeDtypeStruct((4,) + g.shape[1:], g.dtype) for g in gs],
                      [pltpu.SemaphoreType.DMA((4 * n,)), pltpu.SemaphoreType.DMA((4 * n,))], sequencer_id, after)


def _other_chips():
    x, y = lax.axis_index("x"), lax.axis_index("y")
    return jnp.stack([2 * (1 - x) + y, 2 * x + 1 - y, 2 * (1 - x) + 1 - y]).astype(jnp.int32)


def _add_sibling(g, r1, name, after=()):
    _, r, w = g.shape
    tr, tc = _elementwise_block(r, w)
    chips = _other_chips()
    slabs = 2 * chips + lax.axis_index("c").astype(jnp.int32)

    def body(slab_ref, chip_ref, g_ref, r_ref, *rest):
        rest[-1][...] = (g_ref[...] + r_ref[...]).astype(BF16)

    return pl.pallas_call(
        body, name=name, out_shape=jax.ShapeDtypeStruct((3, r, w), BF16),
        grid_spec=pltpu.PrefetchScalarGridSpec(
            num_scalar_prefetch=2, grid=(3, r // tr, w // tc),
            in_specs=[pl.BlockSpec((1, tr, tc), lambda k, i, j, slab_ref, chip_ref: (slab_ref[k], i, j)),
                      pl.BlockSpec((1, tr, tc), lambda k, i, j, slab_ref, chip_ref: (chip_ref[k], i, j))]
            + [ANY] * len(after),
            out_specs=pl.BlockSpec((1, tr, tc), lambda k, i, j, slab_ref, chip_ref: (k, i, j))),
        compiler_params=_cparams("parallel", "parallel", "parallel"),
    )(slabs, chips, g, r1, *after)


def _exchange_chips(ps, name, sequencer_id=None, after=()):
    n = len(ps)

    def body(*refs):
        p_refs, out_refs = refs[:n], refs[n:2 * n]
        send_sems, recv_sems = refs[2 * n:]
        x, y, c = _place()
        chips = [(1 - x, y), (x, 1 - y), (1 - x, 1 - y)]
        if sequencer_id is not None:
            _handshake([(*chip, c) for chip in chips])
        copies = [pltpu.make_async_remote_copy(
            src_ref=p_refs[a].at[j], dst_ref=out_refs[a].at[j], send_sem=send_sems.at[3 * a + j],
            recv_sem=recv_sems.at[3 * a + j], device_id=(px, py, c), device_id_type=MESH)
            for a in range(n) for j, (px, py) in enumerate(chips)]
        for cp in copies:
            cp.start()
        for cp in copies:
            cp.wait()

    return _comm_call(body, name, ps, [jax.ShapeDtypeStruct(p.shape, p.dtype) for p in ps],
                      [pltpu.SemaphoreType.DMA((3 * n,)), pltpu.SemaphoreType.DMA((3 * n,))], sequencer_id, after)


def _sum_chips(g, r1, others, name):
    _, r, c = g.shape
    tr, tc = _elementwise_block(r, c)
    chip = 2 * lax.axis_index("x") + lax.axis_index("y")
    place = jnp.stack([2 * chip + lax.axis_index("c"), chip]).astype(jnp.int32)

    def body(place_ref, g_ref, r_ref, others_ref, o_ref):
        s = g_ref[0] + r_ref[0]
        for j in range(3):
            s = s + others_ref[j].astype(F32)
        o_ref[...] = s

    return pl.pallas_call(
        body, name=name, out_shape=jax.ShapeDtypeStruct((r, c), F32),
        grid_spec=pltpu.PrefetchScalarGridSpec(
            num_scalar_prefetch=1, grid=(r // tr, c // tc),
            in_specs=[pl.BlockSpec((1, tr, tc), lambda i, j, place_ref: (place_ref[0], i, j)),
                      pl.BlockSpec((1, tr, tc), lambda i, j, place_ref: (place_ref[1], i, j)),
                      pl.BlockSpec((3, tr, tc), lambda i, j, place_ref: (0, i, j))],
            out_specs=pl.BlockSpec((tr, tc), lambda i, j, place_ref: (i, j))),
        compiler_params=_cparams("parallel", "parallel"),
    )(place, g, r1, others)


def _reorder_in_proj(wt):
    za_zb, zc, xbc, dt, gates = (wt[:4096], wt[4096:5120], wt[5120:7168], wt[7168:7184], wt[7184:])
    return jnp.concatenate([za_zb, xbc, gates, zc, dt, jnp.zeros((DT_PAD - 16, wt.shape[1]), wt.dtype)], axis=0)


def _restore_in_proj(wt):
    return jnp.concatenate([wt[:4096], wt[OFF_ZC:OFF_ZC + W_ZC], wt[OFF_XBC:OFF_XBC + W_XBC],
                            wt[OFF_DT:OFF_DT + 16], wt[OFF_GATE:OFF_GATE + W_GATE]], axis=0)


def _lanes_from_devices(g):
    return jnp.moveaxis(g, 0, 1).reshape(g.shape[1], N_DEV * g.shape[2])


def _lanes_to_devices(a):
    return jnp.moveaxis(a.reshape(a.shape[0], N_DEV, a.shape[1] // N_DEV), 1, 0)


def _pad_lanes(a, width):
    return jnp.pad(a, ((0, 0), (0, width - a.shape[1])))


BIG = ("w_in", "w_branch_a", "w_branch_b", "w_branch_c", "w_out", "w_mlp_up", "w_mlp_down")
SMALL_SHARDED = ("b_gate", "lru_conv_w", "ssd_conv_w")
REPLICATED = ("norm_mix_g", "gmlp_ln_g", "gmlp_ln_b", "gmlp_w_s", "gmlp_b_s", "lru_conv_b", "lru_w_r", "lru_b_r",
              "lru_w_i", "lru_b_i", "lru_lambda", "ssd_conv_b", "ssd_dt_bias", "ssd_a_log", "ssd_d", "ssd_norm_g",
              "norm_mlp_g", "final_norm_g")
WEIGHTS = ("norm_mix_g", "w_in", "b_gate", "gmlp_ln_g", "gmlp_ln_b", "gmlp_w_s", "gmlp_b_s", "lru_conv_w", "lru_conv_b",
           "lru_w_r", "lru_b_r", "lru_w_i", "lru_b_i", "lru_lambda", "ssd_conv_w", "ssd_conv_b", "ssd_dt_bias",
           "ssd_a_log", "ssd_d", "ssd_norm_g", "w_branch_a", "w_branch_b", "w_branch_c", "w_out", "norm_mlp_g",
           "w_mlp_up", "w_mlp_down", "final_norm_g")
TRANSPOSED = ("w_in", "w_mlp_up")
SMALL_MATRICES = ("gmlp_w_s", "lru_w_r", "lru_w_i")
SMALL_VECTORS = tuple(n for n in REPLICATED if n not in SMALL_MATRICES)
GRADIENT_GROUPS = {"mlp": ("w_mlp_up", "w_mlp_down"), "mix": ("w_branch_a", "w_branch_b", "w_branch_c", "w_out"),
                   "in": ("w_in",)}


def _layer_params(full, l):
    row = lambda a: a.reshape(1, -1)
    return dict(
        norm_mix_g=row(full["norm_mix_g"][l]), norm_mlp_g=row(full["norm_mlp_g"][l]),
        gmlp=(row(full["gmlp_ln_g"][l]), row(full["gmlp_ln_b"][l]), full["gmlp_w_s"][l].reshape(GROUPS_A * CHUNK, CHUNK),
              full["gmlp_b_s"][l].T),
        lru=(full["lru_conv_w"][l], row(full["lru_conv_b"][l]), full["lru_w_r"][l].reshape(D, D // HEADS_B),
             row(full["lru_b_r"][l]), full["lru_w_i"][l].reshape(D, D // HEADS_B), row(full["lru_b_i"][l]),
             row(full["lru_lambda"][l])),
        ssd=(full["ssd_conv_w"][l], row(full["ssd_conv_b"][l]), _pad_lanes(row(full["ssd_dt_bias"][l]), DT_PAD),
             _pad_lanes(row(full["ssd_a_log"][l]), DT_PAD), _pad_lanes(row(full["ssd_d"][l]), DT_PAD),
             row(full["ssd_norm_g"][l])),
        b_gate=full["b_gate"][l],
    )


def _forward_layer(h, p, wb, l, after_mixers=None):
    tag = f"l{l}"
    t_row = 512
    (hn,), _, _ = _chunk_fwd(_f_rmsnorm, f"norm_mix_{tag}", t_row, [(h, 0, D)], [p["norm_mix_g"]], [(D, BF16)])
    proj = _matmul(hn, wb["w_in"], f"in_proj_{tag}", tb=True)
    (ya,), _, _ = _chunk_fwd(_f_gmlp, f"gmlp_{tag}", CHUNK, [(proj, OFF_ZA, W_ZA)], p["gmlp"], [(D, BF16)])
    lru_xs = [(proj, OFF_ZB, D), (proj, OFF_ZB + D, D)]
    (yb,), lru_saved, _ = _chunk_fwd(_f_lru, f"lru_{tag}", CHUNK, lru_xs, p["lru"], [(D, BF16)], halo_idx=(0,),
                                     carry_shapes=[(1, D)], save_carries=True)
    ssd_xs = [(proj, OFF_ZC, W_ZC), (proj, OFF_XBC, W_XBC), (proj, OFF_DT, W_DT)]
    (yc,), ssd_saved, _ = _chunk_fwd(_f_ssd, f"ssd_{tag}", CHUNK, ssd_xs, p["ssd"], [(D, BF16)], halo_idx=(1,),
                                     carry_shapes=[(HEADS_C * HEAD_DIM_C, STATE_C)], save_carries=True)
    if after_mixers is not None:
        after_mixers(yc)
    pa = _matmul(ya, wb["w_branch_a"], f"branch_a_{tag}")
    pb = _matmul(yb, wb["w_branch_b"], f"branch_b_{tag}")
    pc = _matmul(yc, wb["w_branch_c"], f"branch_c_{tag}")
    merge_xs = [(pa, 0, D), (pb, 0, D), (pc, 0, D), (proj, OFF_GATE, W_GATE)]
    (merged,), _, _ = _chunk_fwd(_f_merge, f"merge_{tag}", t_row, merge_xs, [p["b_gate"]], [(D, BF16)])
    h_mid = _matmul(merged, wb["w_out"], f"out_proj_{tag}", epilogue=lambda acc, res: (acc + res,), extras=(h,))
    (hn2,), _, _ = _chunk_fwd(_f_rmsnorm, f"norm_mlp_{tag}", t_row, [(h_mid, 0, D)], [p["norm_mlp_g"]], [(D, BF16)])

    def relu_sq(acc):
        r = jnp.maximum(acc, 0.0)
        return r, r * r

    relu_up, act = _matmul(hn2, wb["w_mlp_up"], f"mlp_up_{tag}", tb=True, outs=(F32, BF16), epilogue=relu_sq)
    h_out = _matmul(act, wb["w_mlp_down"], f"mlp_down_{tag}", epilogue=lambda acc, res: (acc + res,), extras=(h_mid,))
    saved = dict(h=h, hn=hn, proj=proj, ya=ya, yb=yb, yc=yc, lru_saved=lru_saved, ssd_saved=ssd_saved, pa=pa, pb=pb,
                 pc=pc, merged=merged, h_mid=h_mid, hn2=hn2, relu_up=relu_up, act=act, lru_xs=lru_xs, ssd_xs=ssd_xs,
                 merge_xs=merge_xs)
    return h_out, saved


def _backward_layer(dh, sv, p, wb, l):
    tag = f"l{l}"
    t_row = 512
    g = {}
    d_up = _matmul(dh, wb["w_mlp_down"], f"d_act_{tag}", tb=True, outs=(BF16,),
                   epilogue=lambda acc, r: (acc * (2.0 * r),), extras=(sv["relu_up"],))
    g["w_mlp_down"] = _matmul(sv["act"], dh, f"dw_mlp_down_{tag}", ta=True)
    g["w_mlp_up"] = _matmul(d_up, sv["hn2"], f"dw_mlp_up_{tag}", ta=True)
    d_hn2 = _matmul(d_up, wb["w_mlp_up"], f"d_hn2_{tag}")
    (d_mid,), (g["norm_mlp_g"],) = _chunk_bwd(_f_rmsnorm_res, f"norm_mlp_bwd_{tag}", t_row, [(sv["h_mid"], 0, D)],
                                              [p["norm_mlp_g"]], [d_hn2, dh], [F32])
    d_merged = _matmul(d_mid, wb["w_out"], f"d_merged_{tag}", tb=True)
    g["w_out"] = _matmul(sv["merged"], d_mid, f"dw_out_{tag}", ta=True)
    (d_pa, d_pb, d_pc, d_gate), (g["b_gate"],) = _chunk_bwd(
        _f_merge, f"merge_bwd_{tag}", t_row, sv["merge_xs"], [p["b_gate"]], [d_merged], [BF16] * 4)
    d_y = {}
    for br, d_p, y in (("a", d_pa, sv["ya"]), ("b", d_pb, sv["yb"]), ("c", d_pc, sv["yc"])):
        g[f"w_branch_{br}"] = _matmul(y, d_p, f"dw_branch_{br}_{tag}", ta=True)
        d_y[br] = _matmul(d_p, wb[f"w_branch_{br}"], f"d_y{br}_{tag}", tb=True)
    (d_za,), g_gmlp = _chunk_bwd(_f_gmlp, f"gmlp_bwd_{tag}", CHUNK, [(sv["proj"], OFF_ZA, W_ZA)], p["gmlp"],
                                 [d_y["a"]], [BF16])
    (d_xb, d_gt), g_lru = _chunk_bwd(_f_lru, f"lru_bwd_{tag}", CHUNK, sv["lru_xs"], p["lru"], [d_y["b"]], [BF16] * 2,
                                     halo_idx=(0,), saved=sv["lru_saved"])
    (d_zc, d_xbc, d_dt), g_ssd = _chunk_bwd(_f_ssd, f"ssd_bwd_{tag}", CHUNK, sv["ssd_xs"], p["ssd"], [d_y["c"]],
                                            [BF16] * 3, halo_idx=(1,), saved=sv["ssd_saved"])
    d_proj = [d_za, d_xb, d_gt, d_xbc, d_gate, d_zc, d_dt]
    g["w_in"] = _matmul_pieces_t(d_proj, sv["hn"], f"dw_in_{tag}")
    d_hn = _matmul_pieces(d_proj, wb["w_in"], f"d_hn_{tag}")
    (d_h,), (g["norm_mix_g"],) = _chunk_bwd(_f_rmsnorm_res, f"norm_mix_bwd_{tag}", t_row, [(sv["h"], 0, D)],
                                            [p["norm_mix_g"]], [d_hn, d_mid], [F32])
    g["w_in"] = _restore_in_proj(g["w_in"])
    for n in BIG:
        g[n] = g[n].reshape(N_DEV, g[n].shape[0] // N_DEV, g[n].shape[1])
    g["gmlp_ln_g"], g["gmlp_ln_b"], g["gmlp_w_s"] = g_gmlp[:3]
    g["gmlp_b_s"] = g_gmlp[3].T
    (g["lru_conv_w"], g["lru_conv_b"], g["lru_w_r"], g["lru_b_r"], g["lru_w_i"], g["lru_b_i"], g["lru_lambda"]) = g_lru
    g["ssd_conv_w"], g["ssd_conv_b"] = g_ssd[:2]
    g["ssd_dt_bias"], g["ssd_a_log"], g["ssd_d"] = (a[:, :HEADS_C] for a in g_ssd[2:5])
    g["ssd_norm_g"] = g_ssd[5]
    g["later"] = {"mlp": d_merged, "mix": d_za, "in": d_h}
    return d_h, g


LOSS_ROWS = 512


def _loss_and_grads(h, target, full, layer_weights, first_gathered):
    seq = h.shape[0]
    layer_p = [_layer_params(full, l) for l in range(DEPTH)]
    layer_w = [layer_weights(0, (first_gathered,))]
    saved = []
    for l in range(DEPTH):
        fetch_next = None
        if l + 1 < DEPTH:
            fetch_next = lambda y, l=l: layer_w.append(layer_weights(l + 1, (layer_w[l]["w_mlp_down"], y)))
        h, sv = _forward_layer(h, layer_p[l], layer_w[l], l, fetch_next)
        saved.append(sv)
    final_g = full["final_norm_g"].reshape(1, D)
    loss_xs = [(h, 0, D), (target, 0, D)]
    t_loss = min(LOSS_ROWS, seq)
    _, _, (loss_acc,) = _chunk_fwd(_f_loss, "loss", t_loss, loss_xs, [final_g], [], carry_shapes=[(1, 128)],
                                   final_carries=True)
    zero_acc = jnp.zeros((seq // t_loss, 1, 128), F32)
    seed = lambda shape: (lax.broadcasted_iota(jnp.int32, shape, 1) == 0).astype(F32)
    (dh, _), (g_final,) = _chunk_bwd(_f_loss, "loss_bwd", t_loss, loss_xs, [final_g], [], [F32, F32], saved=[zero_acc],
                                     carry_seed=seed)
    layer_g = [None] * DEPTH
    for l in reversed(range(DEPTH)):
        dh, layer_g[l] = _backward_layer(dh, saved[l], layer_p[l], layer_w[l], l)
    return loss_acc, dh, layer_g, g_final


def _small_views(d):
    views = {n: d[n] for n in REPLICATED}
    views["gmlp_b_s"] = d["gmlp_b_s"].reshape(DEPTH * GROUPS_A, CHUNK)
    views["final_norm_g"] = d["final_norm_g"].reshape(1, D)
    for n in SMALL_MATRICES:
        views[n] = d[n].reshape(DEPTH * D, D // HEADS_B)
    return views


def kernel(x, norm_mix_g, w_in, b_gate, gmlp_ln_g, gmlp_ln_b, gmlp_w_s, gmlp_b_s, lru_conv_w, lru_conv_b, lru_w_r, lru_b_r, lru_w_i, lru_b_i, lru_lambda, ssd_conv_w, ssd_conv_b, ssd_dt_bias, ssd_a_log, ssd_d, ssd_norm_g, w_branch_a, w_branch_b, w_branch_c, w_out, norm_mlp_g, w_mlp_up, w_mlp_down, final_norm_g, loss_target, m_norm_mix_g, m_w_in, m_b_gate, m_gmlp_ln_g, m_gmlp_ln_b, m_gmlp_w_s, m_gmlp_b_s, m_lru_conv_w, m_lru_conv_b, m_lru_w_r, m_lru_b_r, m_lru_w_i, m_lru_b_i, m_lru_lambda, m_ssd_conv_w, m_ssd_conv_b, m_ssd_dt_bias, m_ssd_a_log, m_ssd_d, m_ssd_norm_g, m_w_branch_a, m_w_branch_b, m_w_branch_c, m_w_out, m_norm_mlp_g, m_w_mlp_up, m_w_mlp_down, m_final_norm_g, v_norm_mix_g, v_w_in, v_b_gate, v_gmlp_ln_g, v_gmlp_ln_b, v_gmlp_w_s, v_gmlp_b_s, v_lru_conv_w, v_lru_conv_b, v_lru_w_r, v_lru_b_r, v_lru_w_i, v_lru_b_i, v_lru_lambda, v_ssd_conv_w, v_ssd_conv_b, v_ssd_dt_bias, v_ssd_a_log, v_ssd_d, v_ssd_norm_g, v_w_branch_a, v_w_branch_b, v_w_branch_c, v_w_out, v_norm_mlp_g, v_w_mlp_up, v_w_mlp_down, v_final_norm_g):
    args = locals()
    w = {n: args[n] for n in WEIGHTS}
    m = {n: args["m_" + n] for n in WEIGHTS}
    v = {n: args["v_" + n] for n in WEIGHTS}
    seq = x.shape[1]
    h = x.reshape(seq, D)
    target = loss_target.reshape(seq, D)

    def shard_on_wire(n, l):
        return (w[n][l].T if n in TRANSPOSED else w[n][l]).astype(BF16)

    first = _all_gather([shard_on_wire("w_in", 0)] + [w[n] for n in SMALL_SHARDED], "gather_weights_first")
    full = {n: w[n] for n in REPLICATED}
    for n, g in zip(SMALL_SHARDED, first[1:]):
        full[n] = jnp.stack([_lanes_from_devices(g[:, l]) for l in range(DEPTH)])

    def layer_weights(l, after):
        have = {"w_in": first[0]} if l == 0 else {}
        names = [n for n in BIG if n not in have]
        later = _all_gather([shard_on_wire(n, l) for n in names], f"gather_weights_l{l}", SEQ_GATHER + l, after=after)
        have.update(zip(names, later))
        wl = {n: have[n].reshape(-1, D) for n in BIG}
        wl["w_in"] = _reorder_in_proj(wl["w_in"])
        return wl

    loss_local, dh, layer_g, g_final = _loss_and_grads(h, target, full, layer_weights, first[0])
    grad_x = dh.reshape(x.shape)
    out = {}
    kinds = ("grad", "delta", "new_m", "new_v")

    sequencer_before = {}

    def reduce_scatter(slabs, tag, on_sequencer, later=()):
        keys = list(slabs)
        ids = (SEQ_TO_SIBLING, SEQ_TO_CHIPS) if on_sequencer else (None, None)
        from_sibling = _exchange_sibling([slabs[k] for k in keys], f"grads_to_sibling_{tag}", ids[0],
                                         sequencer_before.get("sibling", ()))
        chip_sums = [_add_sibling(slabs[k], r, f"add_sibling_{k[0]}_{k[1]}", later) for k, r in zip(keys, from_sibling)]
        from_chips = _exchange_chips(chip_sums, f"grads_to_chips_{tag}", ids[1], sequencer_before.get("chips", ()))
        if on_sequencer:
            sequencer_before["sibling"], sequencer_before["chips"] = (from_sibling[0],), (from_chips[0],)
        return {k: _sum_chips(slabs[k], r, others, f"sum_chips_{k[0]}_{k[1]}")
                for k, r, others in zip(keys, from_sibling, from_chips)}

    small = {}
    for n in SMALL_SHARDED:
        small[n, None] = jnp.concatenate([_lanes_to_devices(layer_g[l][n]) for l in range(DEPTH)], axis=1)
    g_small = {n: jnp.concatenate([layer_g[l][n] for l in range(DEPTH)], axis=0) for n in REPLICATED[:-1]}
    g_small["final_norm_g"] = g_final
    for n in SMALL_MATRICES:
        small[n, None] = g_small[n].reshape(N_DEV, -1, g_small[n].shape[-1])
    reduced = {}
    groups = [(l, grp) for l in range(DEPTH - 1, -1, -1) for grp in GRADIENT_GROUPS]
    for l, grp in groups:
        if (l, grp) == groups[-1]:
            reduced.update(reduce_scatter(small, "small", True, (layer_g[l][GRADIENT_GROUPS[grp][0]],)))
        slabs = {(n, l): layer_g[l][n] for n in GRADIENT_GROUPS[grp]}
        reduced.update(reduce_scatter(slabs, f"{grp}_l{l}", True, (layer_g[l]["later"][grp],)))
    for n in BIG:
        grads = [reduced[n, l] for l in range(DEPTH)]
        if n in TRANSPOSED and w[n].shape[-1] % LANES:
            res = _adamw_transposed(grads, w[n], m[n], v[n], f"adamw_{n}")
        else:
            res = _adamw([(g.T if n in TRANSPOSED else g)[None] for g in grads], w[n], m[n], v[n], f"adamw_{n}")
        for kind, a in zip(kinds, res):
            out[kind, n] = a
    for n in SMALL_SHARDED:
        one = lambda a: a.reshape((1, -1, a.shape[-1]))
        for kind, a in zip(kinds, _adamw([reduced[n, None][None]], one(w[n]), one(m[n]), one(v[n]), f"adamw_{n}")):
            out[kind, n] = a.reshape(w[n].shape)

    to_gather = [reduced[n, None] if n in SMALL_MATRICES else g_small[n] for n in REPLICATED]
    *g_gathered, loss_terms = _all_gather(to_gather + [loss_local], "gather_small_grads", SEQ_GATHER_SMALL)
    g_all = dict(zip(REPLICATED, g_gathered))
    wv, mv, vv = _small_views(w), _small_views(m), _small_views(v)
    res, loss_sum = _adamw_small([g_all[n] for n in SMALL_VECTORS],
                                 *[[d[n] for n in SMALL_VECTORS] for d in (wv, mv, vv)], loss_terms, "adamw_vectors")
    loss = loss_sum[0, 0]
    for kind, arrays in zip(kinds, res):
        for n, a in zip(SMALL_VECTORS, arrays):
            out[kind, n] = a.reshape(w[n].shape)
    for n in SMALL_MATRICES:
        g_full = g_all[n].reshape((1, 1) + wv[n].shape)
        for kind, a in zip(kinds, _adamw([g_full[0]], wv[n][None], mv[n][None], vv[n][None], f"adamw_{n}")):
            out[kind, n] = a.reshape(w[n].shape)

    return (loss, grad_x, *[out[kind, n] for kind in kinds for n in WEIGHTS])
```

```python
import functools

import jax
import jax.numpy as jnp
from jax import lax
from jax.experimental import pallas as pl
from jax.experimental.pallas import tpu as pltpu
from jax.experimental.pallas import tpu_sc as plsc

F32 = jnp.float32
BF16 = jnp.bfloat16
MESH = pl.DeviceIdType.MESH

D = 1024
DEPTH = 2
EPS = 1e-6
CHUNK = 128
GROUPS_A = 8
HEADS_B = 8
LRU_C = 8.0
HEADS_C = 16
HEAD_DIM_C = 64
GROUPS_C = 4
STATE_C = 128
DT_PAD = 128
OFF_ZA, W_ZA = 0, 2048
OFF_ZB, W_ZB = 2048, 2048
OFF_XBC, W_XBC = 4096, 2048
OFF_GATE, W_GATE = 6144, 3072
OFF_ZC, W_ZC = 9216, 1024
OFF_DT, W_DT = 10240, DT_PAD
D_IN_PAD = 10368
N_DEV = 8
SEQ_GATHER = 1
SEQ_TO_SIBLING = SEQ_GATHER + DEPTH
SEQ_TO_CHIPS = SEQ_TO_SIBLING + 1
SEQ_GATHER_SMALL = SEQ_TO_CHIPS + 1

ADAM_LR = 0.001
ADAM_B1 = 0.9
ADAM_B2 = 0.999
ADAM_EPS = 1e-08
ADAM_WD = 0.01
ADAM_STEP = 10

VMEM_LIMIT = 56 * 1024 * 1024
HALO = 8


def _cparams(*sem):
    return pltpu.CompilerParams(dimension_semantics=sem, vmem_limit_bytes=VMEM_LIMIT)


def _bf(x):
    return x.astype(BF16)


def _dg(a, b, ca, cb):
    return lax.dot_general(a, b, (((ca,), (cb,)), ((), ())), preferred_element_type=F32)


@functools.partial(jax.custom_vjp, nondiff_argnums=(2, 3))
def _mm(a, b, ta, tb):
    return _dg(_bf(a), _bf(b), 0 if ta else 1, 1 if tb else 0)


def _mm_fwd(a, b, ta, tb):
    return _mm(a, b, ta, tb), (a, b)


def _mm_bwd(ta, tb, res, g):
    a, b = res
    ma = 1 if ta else 0
    nb = 0 if tb else 1
    gb, ab, bb = _bf(g), _bf(a), _bf(b)
    da = _dg(bb, gb, nb, 1) if ta else _dg(gb, bb, 1, nb)
    db = _dg(gb, ab, 0, ma) if tb else _dg(ab, gb, ma, 0)
    return da.astype(a.dtype), db.astype(b.dtype)


_mm.defvjp(_mm_fwd, _mm_bwd)


def _slices(x, sizes, axis):
    out, lo = [], 0
    for size in sizes:
        out.append(lax.slice_in_dim(x, lo, lo + size, axis=axis))
        lo += size
    return tuple(out)


@functools.partial(jax.custom_vjp, nondiff_argnums=(1,))
def _split_cols(x, widths):
    return _slices(x, widths, 1)


_split_cols.defvjp(lambda x, widths: (_slices(x, widths, 1), None),
                   lambda widths, _, gs: (jnp.concatenate(gs, axis=1),))


@functools.partial(jax.custom_vjp, nondiff_argnums=(1,))
def _split_rows(x, heights):
    return _slices(x, heights, 0)


_split_rows.defvjp(lambda x, heights: (_slices(x, heights, 0), None),
                   lambda heights, _, gs: (jnp.concatenate(gs, axis=0),))


def _col(x, j):
    lane = lax.broadcasted_iota(jnp.int32, x.shape, 1)
    return jnp.sum(jnp.where(lane == j, x, 0.0), axis=1, keepdims=True)


def _row(x, i):
    r = lax.broadcasted_iota(jnp.int32, x.shape, 0)
    return jnp.sum(jnp.where(r == i, x, 0.0), axis=0, keepdims=True)


def _roll_down(x, s):
    return pltpu.roll(x, s, 0)


def _roll_up(x, s):
    return pltpu.roll(x, x.shape[0] - s, 0)


def _row_iota(x):
    return lax.broadcasted_iota(jnp.int32, x.shape, 0)


@functools.partial(jax.custom_vjp, nondiff_argnums=(2,))
def _shift_rows(halo, x, s):
    if s == 0:
        return x
    return _roll_down(jnp.concatenate([halo, x], axis=0), s)[HALO:]


def _shift_rows_fwd(halo, x, s):
    return _shift_rows(halo, x, s), None


def _shift_rows_bwd(s, _, g):
    if s == 0:
        return jnp.zeros((HALO, g.shape[1]), g.dtype), g
    ge = jnp.concatenate([jnp.zeros((HALO, g.shape[1]), g.dtype), g], axis=0)
    de = _roll_up(ge, s)
    return de[:HALO], de[HALO:]


_shift_rows.defvjp(_shift_rows_fwd, _shift_rows_bwd)


SUBLANES = 8
LANES = 128


def _scan_tiles(a, b, carry, up):
    n, c = a.shape
    nt = n // SUBLANES
    a = a.reshape(nt, SUBLANES, c)
    b = b.reshape(nt, SUBLANES, c)
    sub = lax.broadcasted_iota(jnp.int32, a.shape, 1)
    s = 1
    while s < SUBLANES:
        keep = (sub < SUBLANES - s) if up else (sub >= s)
        shift = SUBLANES - s if up else s
        a_sh = jnp.where(keep, pltpu.roll(a, shift, 1), 1.0)
        b_sh = jnp.where(keep, pltpu.roll(b, shift, 1), 0.0)
        b = a * b_sh + b
        a = a * a_sh
        s *= 2
    tiles = [None] * nt
    edge = 0 if up else SUBLANES - 1
    for j in (range(nt - 1, -1, -1) if up else range(nt)):
        tiles[j] = b[j] if carry is None else b[j] + a[j] * carry
        carry = tiles[j][edge:edge + 1, :]
    return jnp.concatenate(tiles, axis=0)


@jax.custom_vjp
def _lin_scan(a, b, h0):
    return _scan_tiles(a, b, h0, up=False)


def _lin_scan_fwd(a, b, h0):
    h = _lin_scan(a, b, h0)
    return h, (a, h0, h)


def _lin_scan_bwd(res, g):
    a, h0, h = res
    n = a.shape[0]
    row = _row_iota(a)
    a_next = jnp.where(row < n - 1, _roll_up(a, 1), 0.0)
    gg = _scan_tiles(a_next, g, None, up=True)
    h_prev = jnp.where(row >= 1, _roll_down(h, 1), h0)
    return gg * h_prev, gg, _row(a * gg, 0)


_lin_scan.defvjp(_lin_scan_fwd, _lin_scan_bwd)


@jax.custom_vjp
def _cumsum_rows(x):
    n = x.shape[0]
    row = _row_iota(x)
    s = 1
    while s < n:
        x = x + jnp.where(row >= s, _roll_down(x, s), 0.0)
        s *= 2
    return x


def _cumsum_rows_fwd(x):
    return _cumsum_rows(x), None


def _cumsum_rows_bwd(_, g):
    n = g.shape[0]
    row = _row_iota(g)
    s = 1
    while s < n:
        g = g + jnp.where(row < n - s, _roll_up(g, s), 0.0)
        s *= 2
    return (g,)


_cumsum_rows.defvjp(_cumsum_rows_fwd, _cumsum_rows_bwd)


def _sigmoid(x):
    return jax.nn.sigmoid(x)


def _softplus(x):
    return jnp.maximum(x, 0.0) + jnp.log1p(jnp.exp(-jnp.abs(x)))


def _gelu(x):
    return jax.nn.gelu(x, approximate=True)


def _neg_expm1(x):
    series = -x * (1.0 + x * (0.5 + x * (1.0 / 6.0 + x * (1.0 / 24.0))))
    return jnp.where(x > -0.01, series, 1.0 - jnp.exp(x))


def _rms(x, g):
    return x * lax.rsqrt(jnp.mean(x * x, axis=-1, keepdims=True) + EPS) * g


def _f_rmsnorm(carries, halos, xs, params):
    (h,) = xs
    (g,) = params
    return (), (_rms(h, g),)


def _f_rmsnorm_res(carries, halos, xs, params):
    (h,) = xs
    (g,) = params
    return (), (_rms(h, g), h)


def _f_gmlp(carries, halos, xs, params):
    (za,) = xs
    ln_g, ln_b, w_s, b_st = params
    u, v = _split_cols(_gelu(za), (D, D))
    vc = v - jnp.mean(v, axis=-1, keepdims=True)
    vn = vc * lax.rsqrt(jnp.mean(vc * vc, axis=-1, keepdims=True) + EPS) * ln_g + ln_b
    q = CHUNK
    causal = lax.broadcasted_iota(jnp.int32, (q, q), 0) >= lax.broadcasted_iota(jnp.int32, (q, q), 1)
    mixed = []
    for g, (w, vg) in enumerate(zip(_split_rows(w_s, (q,) * GROUPS_A), _split_cols(vn, (q,) * GROUPS_A))):
        mixed.append(_mm(jnp.where(causal, w, 0.0), vg, False, False) + _col(b_st, g))
    return (), (u * jnp.concatenate(mixed, axis=1),)


def _conv4(halo, x, w, b):
    y = b + _row(w, 3) * x
    for k in range(3):
        y = y + _row(w, k) * _shift_rows(halo, x, 3 - k)
    return y


def _f_lru(carries, halos, xs, params):
    (h0,) = carries
    (halo,) = halos
    xb_pre, gate = xs
    conv_w, conv_b, w_r, b_r, w_i, b_i, lam = params
    xb = _conv4(halo, xb_pre, conv_w, conv_b)
    hd = D // HEADS_B
    r_parts, i_parts = [], []
    heads = (hd,) * HEADS_B
    for xh, wr, wi in zip(_split_cols(xb, heads), _split_rows(w_r, heads), _split_rows(w_i, heads)):
        r_parts.append(_mm(xh, wr, False, False))
        i_parts.append(_mm(xh, wi, False, False))
    r = _sigmoid(jnp.concatenate(r_parts, axis=1) + b_r)
    i = _sigmoid(jnp.concatenate(i_parts, axis=1) + b_i)
    log_a = -LRU_C * r * _softplus(-lam)
    a = jnp.exp(log_a)
    inp = jnp.sqrt(_neg_expm1(2.0 * log_a)) * (i * xb)
    h = _lin_scan(a, inp, h0)
    return (_row(h, h.shape[0] - 1),), (_gelu(gate) * h,)


def _f_ssd(carries, halos, xs, params):
    (st,) = carries
    (halo,) = halos
    z, xbc_pre, dt_raw = xs
    conv_w, conv_b, dt_bias, a_log, d_skip, norm_g = params
    t = z.shape[0]
    xc = _conv4(halo, xbc_pre, conv_w, conv_b)
    xbc = xc * _sigmoid(xc)
    x_all, b_all, c_all = _split_cols(xbc, (D, GROUPS_C * STATE_C, GROUPS_C * STATE_C))
    x_pairs = _split_cols(x_all, (128,) * (HEADS_C // 2))
    b_groups = _split_cols(b_all, (STATE_C,) * GROUPS_C)
    c_groups = _split_cols(c_all, (STATE_C,) * GROUPS_C)
    st_pairs = _split_rows(st, (128,) * (HEADS_C // 2))
    dt = _softplus(dt_raw + dt_bias)
    adt = dt * (-jnp.exp(a_log))
    acs = _cumsum_rows(adt)
    acs_t = acs.T
    a_last = _row(acs, t - 1)
    lo = lax.broadcasted_iota(jnp.int32, (t, 128), 1) < HEAD_DIM_C
    lo_rows = lax.broadcasted_iota(jnp.int32, (128, STATE_C), 0) < HEAD_DIM_C
    causal = lax.broadcasted_iota(jnp.int32, (t, t), 0) >= lax.broadcasted_iota(jnp.int32, (t, t), 1)
    y_parts, st_parts = [], []
    for g in range(GROUPS_C):
        bg, cg = b_groups[g], c_groups[g]
        cb = _mm(cg, bg, False, True)
        for pr in range(2):
            pair = 2 * g + pr
            h0, h1 = 2 * pair, 2 * pair + 1
            x2 = x_pairs[pair]
            ac0, ac1 = _col(acs, h0), _col(acs, h1)
            l0 = jnp.exp(jnp.where(causal, ac0 - _row(acs_t, h0), -1e30))
            l1 = jnp.exp(jnp.where(causal, ac1 - _row(acs_t, h1), -1e30))
            xdt = x2 * jnp.where(lo, _col(dt, h0), _col(dt, h1))
            y_diag = (_mm(cb * l0, jnp.where(lo, xdt, 0.0), False, False)
                      + _mm(cb * l1, jnp.where(lo, 0.0, xdt), False, False))
            al0, al1 = _col(a_last, h0), _col(a_last, h1)
            decay_s = jnp.where(lo, jnp.exp(al0 - ac0), jnp.exp(al1 - ac1))
            s_new = _mm(xdt * decay_s, bg, True, False)
            prev = st_pairs[pair]
            y_off = _mm(cg, prev, False, True) * jnp.where(lo, jnp.exp(ac0), jnp.exp(ac1))
            skip = jnp.where(lo, _col(d_skip, h0), _col(d_skip, h1))
            y_parts.append(y_diag + y_off + x2 * skip)
            st_parts.append(prev * jnp.where(lo_rows, jnp.exp(al0), jnp.exp(al1)) + s_new)
    y = jnp.concatenate(y_parts, axis=1) * (z * _sigmoid(z))
    gw = D // GROUPS_C
    yn = []
    for yg in _split_cols(y, (gw,) * GROUPS_C):
        yn.append(yg * lax.rsqrt(jnp.mean(yg * yg, axis=-1, keepdims=True) + EPS))
    return (jnp.concatenate(st_parts, axis=0),), (jnp.concatenate(yn, axis=1) * norm_g,)


def _f_merge(carries, halos, xs, params):
    pa, pb, pc, g_raw = xs
    (b_gate,) = params
    ga, gb, gc = _split_cols(g_raw, (D, D, D))
    m = (_sigmoid(ga + _row(b_gate, 0)) * pa + _sigmoid(gb + _row(b_gate, 1)) * pb
         + _sigmoid(gc + _row(b_gate, 2)) * pc)
    return (), (m,)


def _f_loss(carries, halos, xs, params):
    (acc,) = carries
    h, target = xs
    (g,) = params
    err = jnp.square(_rms(h, g) - target)
    part = 0.5 * jnp.sum(jnp.mean(err, axis=-1, keepdims=True), axis=0, keepdims=True)
    return (acc + part,), ()


def _x_specs(xs, t, index_of):
    specs = []
    for arr, off, width in xs:
        assert off % width == 0 and off + width <= arr.shape[1]
        specs.append(pl.BlockSpec((t, width), functools.partial(lambda j, cb: (index_of(j), cb), cb=off // width)))
    return specs


def _halo_specs(xs, halo_idx, t, index_of):
    specs = []
    for xi in halo_idx:
        _, off, width = xs[xi]
        specs.append(pl.BlockSpec(
            (HALO, width),
            functools.partial(lambda j, cb: (jnp.maximum(index_of(j) * (t // HALO) - 1, 0), cb), cb=off // width)))
    return specs


def _full_spec(a):
    return pl.BlockSpec(a.shape, functools.partial(lambda j, nd: (0,) * nd, nd=a.ndim))


def _chunk_fwd(f, name, t, xs, params, outs, halo_idx=(), carry_shapes=(), save_carries=False, final_carries=False):
    s = xs[0][0].shape[0]
    n = s // t
    nx, nh, npar, no, nc = len(xs), len(halo_idx), len(params), len(outs), len(carry_shapes)
    ns = nc if save_carries else 0
    nf = nc if final_carries else 0

    def body(*refs):
        x_refs, refs = refs[:nx], refs[nx:]
        h_refs, refs = refs[:nh], refs[nh:]
        p_refs, refs = refs[:npar], refs[npar:]
        y_refs, refs = refs[:no], refs[no:]
        s_refs, refs = refs[:ns], refs[ns:]
        f_refs, c_refs = refs[:nf], refs[nf:]
        i = pl.program_id(0)

        @pl.when(i == 0)
        def _():
            for c in c_refs:
                c[...] = jnp.zeros_like(c)

        carries = tuple(c[...] for c in c_refs)
        for s_ref, c in zip(s_refs, carries):
            s_ref[0] = c
        halos = tuple(jnp.where(i > 0, h[...].astype(F32), 0.0) for h in h_refs)
        new_c, ys = f(carries, halos, tuple(x[...].astype(F32) for x in x_refs), tuple(p[...] for p in p_refs))
        for y_ref, y in zip(y_refs, ys):
            y_ref[...] = y.astype(y_ref.dtype)
        for c, v in zip(c_refs, new_c):
            c[...] = v
        for f_ref, v in zip(f_refs, new_c):
            f_ref[...] = v

    ident = lambda j: j
    out_shape = [jax.ShapeDtypeStruct((s, w), dt) for w, dt in outs]
    out_specs = [pl.BlockSpec((t, w), lambda j: (j, 0)) for w, _ in outs]
    if save_carries:
        out_shape += [jax.ShapeDtypeStruct((n,) + tuple(cs), F32) for cs in carry_shapes]
        out_specs += [pl.BlockSpec((1,) + tuple(cs), lambda j: (j, 0, 0)) for cs in carry_shapes]
    if final_carries:
        out_shape += [jax.ShapeDtypeStruct(tuple(cs), F32) for cs in carry_shapes]
        out_specs += [pl.BlockSpec(tuple(cs), lambda j: (0, 0)) for cs in carry_shapes]
    res = pl.pallas_call(
        body, name=name, grid=(n,),
        in_specs=_x_specs(xs, t, ident) + _halo_specs(xs, halo_idx, t, ident) + [_full_spec(p) for p in params],
        out_specs=out_specs, out_shape=out_shape,
        scratch_shapes=[pltpu.VMEM(tuple(cs), F32) for cs in carry_shapes],
        compiler_params=_cparams("arbitrary"),
    )(*[x[0] for x in xs], *[xs[xi][0] for xi in halo_idx], *params)
    return res[:no], res[no:no + ns], res[no + ns:]


def _chunk_bwd(f, name, t, xs, params, dys, dx_dtypes, halo_idx=(), saved=(), carry_seed=None):
    s = xs[0][0].shape[0]
    n = s // t
    nx, nh, npar, nc, ndy = len(xs), len(halo_idx), len(params), len(saved), len(dys)

    def body(*refs):
        x_refs, refs = refs[:nx], refs[nx:]
        h_refs, refs = refs[:nh], refs[nh:]
        p_refs, refs = refs[:npar], refs[npar:]
        s_refs, refs = refs[:nc], refs[nc:]
        dy_refs, refs = refs[:ndy], refs[ndy:]
        dx_refs, refs = refs[:nx], refs[nx:]
        dp_refs, refs = refs[:npar], refs[npar:]
        dc_refs, dh_refs = refs[:nc], refs[nc:]
        j = pl.program_id(0)
        i = n - 1 - j

        @pl.when(j == 0)
        def _():
            for dc in dc_refs:
                dc[...] = jnp.zeros_like(dc) if carry_seed is None else carry_seed(dc.shape)
            for r in dh_refs + dp_refs:
                r[...] = jnp.zeros_like(r)

        carries = tuple(s_ref[0] for s_ref in s_refs)
        halos = tuple(jnp.where(i > 0, h[...].astype(F32), 0.0) for h in h_refs)
        x_vals = tuple(x[...].astype(F32) for x in x_refs)
        p_vals = tuple(p[...] for p in p_refs)
        _, vjp = jax.vjp(f, carries, halos, x_vals, p_vals)
        d_car, d_hal, d_xs, d_par = vjp((tuple(dc[...] for dc in dc_refs), tuple(d[...].astype(F32) for d in dy_refs)))
        d_xs = list(d_xs)
        for k, xi in enumerate(halo_idx):
            w = xs[xi][2]
            d_xs[xi] = d_xs[xi] + jnp.concatenate([jnp.zeros((t - HALO, w), F32), dh_refs[k][...]], axis=0)
            dh_refs[k][...] = jnp.where(i > 0, d_hal[k], 0.0)
        for dx_ref, dx in zip(dx_refs, d_xs):
            dx_ref[...] = dx.astype(dx_ref.dtype)
        for dp_ref, dp in zip(dp_refs, d_par):
            dp_ref[...] += dp
        for dc, v in zip(dc_refs, d_car):
            dc[...] = v

    rev = lambda j: n - 1 - j
    in_specs = (_x_specs(xs, t, rev) + _halo_specs(xs, halo_idx, t, rev) + [_full_spec(p) for p in params]
                + [pl.BlockSpec((1,) + a.shape[1:], lambda j: (n - 1 - j, 0, 0)) for a in saved]
                + [pl.BlockSpec((t, d.shape[1]), lambda j: (n - 1 - j, 0)) for d in dys])
    out_shape = ([jax.ShapeDtypeStruct((s, w), dt) for (_, _, w), dt in zip(xs, dx_dtypes)]
                 + [jax.ShapeDtypeStruct(p.shape, F32) for p in params])
    out_specs = ([pl.BlockSpec((t, w), lambda j: (n - 1 - j, 0)) for _, _, w in xs] + [_full_spec(p) for p in params])
    res = pl.pallas_call(
        body, name=name, grid=(n,), in_specs=in_specs, out_specs=out_specs, out_shape=out_shape,
        scratch_shapes=([pltpu.VMEM(a.shape[1:], F32) for a in saved]
                        + [pltpu.VMEM((HALO, xs[xi][2]), F32) for xi in halo_idx]),
        compiler_params=_cparams("arbitrary"),
    )(*[x[0] for x in xs], *[xs[xi][0] for xi in halo_idx], *params, *saved, *dys)
    return res[:nx], res[nx:]


def _tile(dim, pref):
    for cand in pref:
        if dim % cand == 0:
            return cand
    return dim


def _matmul(a, b, name, ta=False, tb=False, outs=(F32,), epilogue=None, extras=()):
    m, k = (a.shape[1], a.shape[0]) if ta else a.shape
    n = b.shape[0] if tb else b.shape[1]
    tm = _tile(m, (1024, 1152, 512, 256, 128))
    tn = _tile(n, (1152, 1024, 512, 256, 128))
    tk = _tile(k, (1024, 1152, 512, 256, 128))
    nk = k // tk
    ne, no = len(extras), len(outs)
    ca, cb = (0 if ta else 1), (1 if tb else 0)

    def body(*refs):
        a_ref, b_ref = refs[:2]
        e_refs = refs[2:2 + ne]
        o_refs = refs[2 + ne:2 + ne + no]

        acc = refs[-1]
        kk = pl.program_id(2)

        @pl.when(kk == 0)
        def _():
            acc[...] = jnp.zeros_like(acc)

        acc[...] += _dg(_bf(a_ref[...]), _bf(b_ref[...]), ca, cb)

        @pl.when(kk == nk - 1)
        def _():
            res = acc[...]
            vals = (res,) if epilogue is None else epilogue(res, *[e[...] for e in e_refs])
            for o_ref, v in zip(o_refs, vals):
                o_ref[...] = v.astype(o_ref.dtype)

    a_spec = pl.BlockSpec((tk, tm), lambda i, j, kk: (kk, i)) if ta else pl.BlockSpec((tm, tk), lambda i, j, kk: (i, kk))
    b_spec = pl.BlockSpec((tn, tk), lambda i, j, kk: (j, kk)) if tb else pl.BlockSpec((tk, tn), lambda i, j, kk: (kk, j))
    mn_spec = pl.BlockSpec((tm, tn), lambda i, j, kk: (i, j))
    res = pl.pallas_call(
        body, name=name, grid=(m // tm, n // tn, nk),
        in_specs=[a_spec, b_spec] + [mn_spec] * ne,
        out_specs=[mn_spec] * no,
        out_shape=[jax.ShapeDtypeStruct((m, n), dt) for dt in outs],
        scratch_shapes=[pltpu.VMEM((tm, tn), F32)],
        compiler_params=_cparams("parallel", "parallel", "arbitrary"),
    )(a, b, *extras)
    return res if no > 1 else res[0]


def _elementwise_block(r, c):
    if r % 8 == 0 and r >= 8:
        return _tile(r, (256, 128, 64, 32, 16, 8)), c
    return r, _tile(c, (256, 128))


PIECE_TILE = 1024


def _piece_steps(pieces):
    steps, s0 = [], 0
    for a in pieces:
        wt = min(a.shape[1], PIECE_TILE)
        assert a.shape[1] % wt == 0
        steps.append((s0, a.shape[1] // wt, wt))
        s0 += a.shape[1] // wt
    return steps, s0


def _matmul_pieces(pieces, b, name):
    steps, n_steps = _piece_steps(pieces)
    s_rows, n = pieces[0].shape[0], b.shape[1]
    tm = _tile(s_rows, (1024, 512, 256, 128))
    tail_rows = steps[-1][2]
    full_steps = n_steps - 1 if tail_rows < PIECE_TILE else n_steps
    b_tail = b[full_steps * PIECE_TILE:]
    np_ = len(pieces)

    def body(*refs):
        a_refs, b_ref, tail_ref, o_ref, acc = refs[:np_], refs[np_], refs[np_ + 1], refs[np_ + 2], refs[np_ + 3]
        s = pl.program_id(1)

        @pl.when(s == 0)
        def _():
            acc[...] = jnp.zeros_like(acc)

        for a_ref, (s0, ns, wt) in zip(a_refs, steps):
            @pl.when((s >= s0) & (s < s0 + ns))
            def _(a_ref=a_ref, wt=wt):
                rhs = b_ref[...] if wt == PIECE_TILE else tail_ref[...]
                acc[...] += _dg(_bf(a_ref[...]), _bf(rhs), 1, 0)

        @pl.when(s == n_steps - 1)
        def _():
            o_ref[...] = acc[...]

    a_specs = [pl.BlockSpec((tm, wt), functools.partial(lambda i, s, s0, ns: (i, jnp.clip(s - s0, 0, ns - 1)), s0=s0, ns=ns))
               for s0, ns, wt in steps]
    return pl.pallas_call(
        body, name=name, grid=(s_rows // tm, n_steps),
        in_specs=a_specs + [pl.BlockSpec((PIECE_TILE, n), lambda i, s: (jnp.minimum(s, full_steps - 1), 0)),
                            pl.BlockSpec(b_tail.shape, lambda i, s: (0, 0))],
        out_specs=pl.BlockSpec((tm, n), lambda i, s: (i, 0)), out_shape=jax.ShapeDtypeStruct((s_rows, n), F32),
        scratch_shapes=[pltpu.VMEM((tm, n), F32)],
        compiler_params=_cparams("parallel", "arbitrary"),
    )(*pieces, b, b_tail)


def _matmul_pieces_t(pieces, b, name):
    steps, n_steps = _piece_steps(pieces)
    s_rows, n = b.shape
    tk = _tile(s_rows, (1024, 512, 256, 128))
    nk = s_rows // tk
    total = sum(a.shape[1] for a in pieces)
    np_ = len(pieces)

    def body(*refs):
        a_refs, b_ref, o_ref, acc = refs[:np_], refs[np_], refs[np_ + 1], refs[np_ + 2]
        s, kk = pl.program_id(0), pl.program_id(1)

        @pl.when(kk == 0)
        def _():
            acc[...] = jnp.zeros_like(acc)

        for a_ref, (s0, ns, wt) in zip(a_refs, steps):
            @pl.when((s >= s0) & (s < s0 + ns))
            def _(a_ref=a_ref, wt=wt):
                acc[0:wt, :] += _dg(_bf(a_ref[...]), _bf(b_ref[...]), 0, 0)

        @pl.when(kk == nk - 1)
        def _():
            o_ref[...] = acc[...]

    def a_index(s, kk, s0, ns):
        active = (s >= s0) & (s < s0 + ns)
        return jnp.where(active, kk, jnp.where(s < s0, 0, nk - 1)), jnp.clip(s - s0, 0, ns - 1)

    a_specs = [pl.BlockSpec((tk, wt), functools.partial(a_index, s0=s0, ns=ns)) for s0, ns, wt in steps]
    return pl.pallas_call(
        body, name=name, grid=(n_steps, nk),
        in_specs=a_specs + [pl.BlockSpec((tk, n), lambda s, kk: (kk, 0))],
        out_specs=pl.BlockSpec((PIECE_TILE, n), lambda s, kk: (s, 0)), out_shape=jax.ShapeDtypeStruct((total, n), F32),
        scratch_shapes=[pltpu.VMEM((PIECE_TILE, n), F32)],
        compiler_params=_cparams("parallel", "arbitrary"),
    )(*pieces, b)


def _adamw_math(g, w, m, v):
    m_new = ADAM_B1 * m + (1.0 - ADAM_B1) * g
    v_new = ADAM_B2 * v + (1.0 - ADAM_B2) * jnp.square(g)
    m_hat = m_new / (1.0 - ADAM_B1 ** ADAM_STEP)
    v_hat = v_new / (1.0 - ADAM_B2 ** ADAM_STEP)
    return -ADAM_LR * (m_hat / (jnp.sqrt(v_hat) + ADAM_EPS) + ADAM_WD * w), m_new, v_new


def _adamw(parts, w, m, v, name):
    nl, r, c = w.shape
    k = parts[0].shape[0]
    tr = _tile(r, (128, 64, 32, 16, 8))
    nb = r // tr

    def body(*refs):
        p_refs, (w_ref, m_ref, v_ref), outs = refs[:nl], refs[nl:nl + 3], refs[nl + 3:]
        layer = pl.program_id(0)
        for q in range(nl):
            @pl.when(layer == q)
            def _(q=q):
                g = p_refs[q][0]
                for j in range(1, k):
                    g = g + p_refs[q][j]
                vals = (g,) + _adamw_math(g, w_ref[0], m_ref[0], v_ref[0])
                for o_ref, val in zip(outs, vals):
                    o_ref[0] = val

    spec = pl.BlockSpec((1, tr, c), lambda l, i: (l, i, 0))
    part_specs = [pl.BlockSpec((k, tr, c), functools.partial(
        lambda l, i, q: (0, jnp.where(l == q, i, jnp.where(l < q, 0, nb - 1)), 0), q=q)) for q in range(nl)]
    return pl.pallas_call(
        body, name=name, grid=(nl, nb), in_specs=part_specs + [spec] * 3,
        out_specs=[spec] * 4, out_shape=[jax.ShapeDtypeStruct((nl, r, c), F32)] * 4,
        compiler_params=_cparams("arbitrary", "arbitrary"),
    )(*parts, w, m, v)


def _adamw_transposed(grads, w, m, v, name):
    nl, r, c = w.shape
    tc = 64
    views = [jnp.transpose(a, (2, 0, 1)) for a in (w, m, v)]

    def body(*refs):
        g_refs, (w_ref, m_ref, v_ref), outs = refs[:nl], refs[nl:nl + 3], refs[nl + 3:]
        for l in range(nl):
            g = g_refs[l][...]
            vals = (g,) + _adamw_math(g, w_ref[:, l, :], m_ref[:, l, :], v_ref[:, l, :])
            for o_ref, val in zip(outs, vals):
                o_ref[:, l, :] = val

    spec = pl.BlockSpec((tc, nl, r), lambda i: (i, 0, 0))
    res = pl.pallas_call(
        body, name=name, grid=(pl.cdiv(c, tc),), in_specs=[pl.BlockSpec((tc, r), lambda i: (i, 0))] * nl + [spec] * 3,
        out_specs=[spec] * 4, out_shape=[jax.ShapeDtypeStruct((c, nl, r), F32)] * 4,
        compiler_params=_cparams("parallel"),
    )(*grads, *views)
    return [jnp.transpose(a, (1, 2, 0)) for a in res]


def _adamw_small(gathered, ws, ms, vs, loss_terms, name):
    n = len(ws)

    def device_sum(ref):
        s = ref[0]
        for j in range(1, N_DEV):
            s = s + ref[j]
        return s

    def body(*refs):
        g_refs, w_refs, m_refs, v_refs = refs[:n], refs[n:2 * n], refs[2 * n:3 * n], refs[3 * n:4 * n]
        loss_ref, outs, loss_out = refs[4 * n], refs[4 * n + 1:-1], refs[-1]
        for i in range(n):
            g = device_sum(g_refs[i])
            vals = (g,) + _adamw_math(g, w_refs[i][...], m_refs[i][...], v_refs[i][...])
            for kind, val in enumerate(vals):
                outs[kind * n + i][...] = val
        loss_out[...] = device_sum(loss_ref)

    res = pl.pallas_call(
        body, name=name,
        out_shape=[jax.ShapeDtypeStruct(a.shape, F32) for _ in range(4) for a in ws] + [
            jax.ShapeDtypeStruct(loss_terms.shape[1:], F32)],
        compiler_params=pltpu.CompilerParams(vmem_limit_bytes=VMEM_LIMIT),
    )(*gathered, *ws, *ms, *vs, loss_terms)
    return [res[kind * n:(kind + 1) * n] for kind in range(4)], res[-1]


ANY = pl.BlockSpec(memory_space=pl.ANY)


def _place():
    return lax.axis_index("x"), lax.axis_index("y"), lax.axis_index("c")


def _handshake(peers):
    barrier = pltpu.get_barrier_semaphore()
    for peer in peers:
        pl.semaphore_signal(barrier, inc=1, device_id=peer, device_id_type=MESH)
    pl.semaphore_wait(barrier, len(peers))


def _comm_call(body, name, inputs, out_shape, scratch, sequencer_id=None, after=()):
    if sequencer_id is None:
        return pl.pallas_call(body, name=name, out_shape=out_shape, in_specs=[ANY] * len(inputs),
                              out_specs=[ANY] * len(out_shape), scratch_shapes=scratch)(*inputs)
    n_in, n_after = len(inputs), len(after)

    def sequencer_body(*refs):
        body(*refs[:n_in], *refs[n_in + n_after:])

    return pl.kernel(
        sequencer_body, out_type=out_shape, mesh=plsc.ScalarSubcoreMesh(axis_name="sequencer", num_cores=1),
        scratch_types=scratch, compiler_params=pltpu.CompilerParams(collective_id=sequencer_id), name=name,
    )(*inputs, *after)


def _all_gather(blocks, name, sequencer_id=None, after=()):
    n = len(blocks)

    def body(*refs):
        x_refs, out_refs = refs[:n], refs[n:2 * n]
        send_sems, recv_sems, local_sems = refs[2 * n:]
        x, y, c = _place()
        me, sibling = (x, y, c), (x, y, 1 - c)
        chips = [(1 - x, y), (x, 1 - y), (1 - x, 1 - y)]
        if sequencer_id is not None:
            _handshake([sibling] + [(*chip, c) for chip in chips])

        def slot(a, px, py, pc):
            return out_refs[a].at[4 * px + 2 * py + pc]

        def copy(a, k, blk, to, src=None):
            return pltpu.make_async_remote_copy(
                src_ref=slot(a, *blk) if src is None else src, dst_ref=slot(a, *blk),
                send_sem=send_sems.at[7 * a + k], recv_sem=recv_sems.at[7 * a + k], device_id=to, device_id_type=MESH)

        mine = [pltpu.make_async_copy(x_refs[a], slot(a, *me), local_sems.at[a]) for a in range(n)]
        first = []
        for a in range(n):
            mine[a].start()
            first.append(copy(a, 0, me, sibling, src=x_refs[a]))
            first += [copy(a, 1 + j, me, (*chip, c), src=x_refs[a]) for j, chip in enumerate(chips)]
        for cp in first:
            cp.start()
        passed = []
        for j, chip in enumerate(chips):
            for a in range(n):
                copy(a, 1 + j, (*chip, c), me).wait_recv()
                passed.append(copy(a, 4 + j, (*chip, c), sibling))
                passed[-1].start()
        for a in range(n):
            copy(a, 0, sibling, me).wait_recv()
            for j, chip in enumerate(chips):
                copy(a, 4 + j, (*chip, 1 - c), me).wait_recv()
        for cp in first + passed:
            cp.wait_send()
        for cp in mine:
            cp.wait()

    return _comm_call(
        body, name, blocks, [jax.ShapeDtypeStruct((N_DEV,) + b.shape, b.dtype) for b in blocks],
        [pltpu.SemaphoreType.DMA((7 * n,)), pltpu.SemaphoreType.DMA((7 * n,)), pltpu.SemaphoreType.DMA((n,))],
        sequencer_id, after)


def _exchange_sibling(gs, name, sequencer_id=None, after=()):
    n = len(gs)

    def body(*refs):
        g_refs, out_refs = refs[:n], refs[n:2 * n]
        send_sems, recv_sems = refs[2 * n:]
        x, y, c = _place()
        if sequencer_id is not None:
            _handshake([(x, y, 1 - c)])
        copies = [pltpu.make_async_remote_copy(
            src_ref=g_refs[a].at[2 * k + 1 - c], dst_ref=out_refs[a].at[k], send_sem=send_sems.at[4 * a + k],
            recv_sem=recv_sems.at[4 * a + k], device_id=(x, y, 1 - c), device_id_type=MESH)
            for a in range(n) for k in range(4)]
        for cp in copies:
            cp.start()
        for cp in copies:
            cp.wait()

    return _comm_call(body, name, gs, [jax.ShapeDtypeStruct((4,) + g.shape[1:], g.dtype) for g in gs],
                      [pltpu.SemaphoreType.DMA((4 * n,)), pltpu.SemaphoreType.DMA((4 * n,))], sequencer_id, after)


def _other_chips():
    x, y = lax.axis_index("x"), lax.axis_index("y")
    return jnp.stack([2 * (1 - x) + y, 2 * x + 1 - y, 2 * (1 - x) + 1 - y]).astype(jnp.int32)


def _add_sibling(gs, r1s, name, after=()):
    n = len(gs)
    _, r, w = gs[0].shape
    tr, tc = _elementwise_block(r, w)
    chips = _other_chips()
    slabs = 2 * chips + lax.axis_index("c").astype(jnp.int32)

    def body(slab_ref, chip_ref, *refs):
        g_refs, r_refs, o_refs = refs[:n], refs[n:2 * n], refs[2 * n + len(after):]
        for g_ref, r_ref, o_ref in zip(g_refs, r_refs, o_refs):
            o_ref[...] = (g_ref[...] + r_ref[...]).astype(BF16)

    return pl.pallas_call(
        body, name=name, out_shape=[jax.ShapeDtypeStruct((3, r, w), BF16)] * n,
        grid_spec=pltpu.PrefetchScalarGridSpec(
            num_scalar_prefetch=2, grid=(3, r // tr, w // tc),
            in_specs=[pl.BlockSpec((1, tr, tc), lambda k, i, j, slab_ref, chip_ref: (slab_ref[k], i, j))] * n
            + [pl.BlockSpec((1, tr, tc), lambda k, i, j, slab_ref, chip_ref: (chip_ref[k], i, j))] * n
            + [ANY] * len(after),
            out_specs=[pl.BlockSpec((1, tr, tc), lambda k, i, j, slab_ref, chip_ref: (k, i, j))] * n),
        compiler_params=_cparams("parallel", "parallel", "parallel"),
    )(slabs, chips, *gs, *r1s, *after)


def _exchange_chips(ps, name, sequencer_id=None, after=()):
    n = len(ps)

    def body(*refs):
        p_refs, out_refs = refs[:n], refs[n:2 * n]
        send_sems, recv_sems = refs[2 * n:]
        x, y, c = _place()
        chips = [(1 - x, y), (x, 1 - y), (1 - x, 1 - y)]
        if sequencer_id is not None:
            _handshake([(*chip, c) for chip in chips])
        copies = [pltpu.make_async_remote_copy(
            src_ref=p_refs[a].at[j], dst_ref=out_refs[a].at[j], send_sem=send_sems.at[3 * a + j],
            recv_sem=recv_sems.at[3 * a + j], device_id=(px, py, c), device_id_type=MESH)
            for a in range(n) for j, (px, py) in enumerate(chips)]
        for cp in copies:
            cp.start()
        for cp in copies:
            cp.wait()

    return _comm_call(body, name, ps, [jax.ShapeDtypeStruct(p.shape, p.dtype) for p in ps],
                      [pltpu.SemaphoreType.DMA((3 * n,)), pltpu.SemaphoreType.DMA((3 * n,))], sequencer_id, after)


def _sum_chips(gs, r1s, others, name):
    n = len(gs)
    _, r, c = gs[0].shape
    tr, tc = _elementwise_block(r, c)
    chip = 2 * lax.axis_index("x") + lax.axis_index("y")
    place = jnp.stack([2 * chip + lax.axis_index("c"), chip]).astype(jnp.int32)

    def body(place_ref, *refs):
        for g_ref, r_ref, others_ref, o_ref in zip(refs[:n], refs[n:2 * n], refs[2 * n:3 * n], refs[3 * n:]):
            s = g_ref[0] + r_ref[0]
            for j in range(3):
                s = s + others_ref[j].astype(F32)
            o_ref[...] = s

    return pl.pallas_call(
        body, name=name, out_shape=[jax.ShapeDtypeStruct((r, c), F32)] * n,
        grid_spec=pltpu.PrefetchScalarGridSpec(
            num_scalar_prefetch=1, grid=(r // tr, c // tc),
            in_specs=[pl.BlockSpec((1, tr, tc), lambda i, j, place_ref: (place_ref[0], i, j))] * n
            + [pl.BlockSpec((1, tr, tc), lambda i, j, place_ref: (place_ref[1], i, j))] * n
            + [pl.BlockSpec((3, tr, tc), lambda i, j, place_ref: (0, i, j))] * n,
            out_specs=[pl.BlockSpec((tr, tc), lambda i, j, place_ref: (i, j))] * n),
        compiler_params=_cparams("parallel", "parallel"),
    )(place, *gs, *r1s, *others)


def _reorder_in_proj(wt):
    za_zb, zc, xbc, dt, gates = (wt[:4096], wt[4096:5120], wt[5120:7168], wt[7168:7184], wt[7184:])
    return jnp.concatenate([za_zb, xbc, gates, zc, dt, jnp.zeros((DT_PAD - 16, wt.shape[1]), wt.dtype)], axis=0)


def _restore_in_proj(wt):
    return jnp.concatenate([wt[:4096], wt[OFF_ZC:OFF_ZC + W_ZC], wt[OFF_XBC:OFF_XBC + W_XBC],
                            wt[OFF_DT:OFF_DT + 16], wt[OFF_GATE:OFF_GATE + W_GATE]], axis=0)


def _lanes_from_devices(g):
    return jnp.moveaxis(g, 0, 1).reshape(g.shape[1], N_DEV * g.shape[2])


def _lanes_to_devices(a):
    return jnp.moveaxis(a.reshape(a.shape[0], N_DEV, a.shape[1] // N_DEV), 1, 0)


def _pad_lanes(a, width):
    return jnp.pad(a, ((0, 0), (0, width - a.shape[1])))


BIG = ("w_in", "w_branch_a", "w_branch_b", "w_branch_c", "w_out", "w_mlp_up", "w_mlp_down")
SMALL_SHARDED = ("b_gate", "lru_conv_w", "ssd_conv_w")
REPLICATED = ("norm_mix_g", "gmlp_ln_g", "gmlp_ln_b", "gmlp_w_s", "gmlp_b_s", "lru_conv_b", "lru_w_r", "lru_b_r",
              "lru_w_i", "lru_b_i", "lru_lambda", "ssd_conv_b", "ssd_dt_bias", "ssd_a_log", "ssd_d", "ssd_norm_g",
              "norm_mlp_g", "final_norm_g")
WEIGHTS = ("norm_mix_g", "w_in", "b_gate", "gmlp_ln_g", "gmlp_ln_b", "gmlp_w_s", "gmlp_b_s", "lru_conv_w", "lru_conv_b",
           "lru_w_r", "lru_b_r", "lru_w_i", "lru_b_i", "lru_lambda", "ssd_conv_w", "ssd_conv_b", "ssd_dt_bias",
           "ssd_a_log", "ssd_d", "ssd_norm_g", "w_branch_a", "w_branch_b", "w_branch_c", "w_out", "norm_mlp_g",
           "w_mlp_up", "w_mlp_down", "final_norm_g")
TRANSPOSED = ("w_in", "w_mlp_up")
SMALL_MATRICES = ("gmlp_w_s", "lru_w_r", "lru_w_i")
SMALL_VECTORS = tuple(n for n in REPLICATED if n not in SMALL_MATRICES)
GRADIENT_GROUPS = {"mlp": ("w_mlp_up", "w_mlp_down"), "mix": ("w_branch_a", "w_branch_b", "w_branch_c", "w_out"),
                   "in": ("w_in",)}


def _layer_params(full, l):
    row = lambda a: a.reshape(1, -1)
    return dict(
        norm_mix_g=row(full["norm_mix_g"][l]), norm_mlp_g=row(full["norm_mlp_g"][l]),
        gmlp=(row(full["gmlp_ln_g"][l]), row(full["gmlp_ln_b"][l]), full["gmlp_w_s"][l].reshape(GROUPS_A * CHUNK, CHUNK),
              full["gmlp_b_s"][l].T),
        lru=(full["lru_conv_w"][l], row(full["lru_conv_b"][l]), full["lru_w_r"][l].reshape(D, D // HEADS_B),
             row(full["lru_b_r"][l]), full["lru_w_i"][l].reshape(D, D // HEADS_B), row(full["lru_b_i"][l]),
             row(full["lru_lambda"][l])),
        ssd=(full["ssd_conv_w"][l], row(full["ssd_conv_b"][l]), _pad_lanes(row(full["ssd_dt_bias"][l]), DT_PAD),
             _pad_lanes(row(full["ssd_a_log"][l]), DT_PAD), _pad_lanes(row(full["ssd_d"][l]), DT_PAD),
             row(full["ssd_norm_g"][l])),
        b_gate=full["b_gate"][l],
    )


def _forward_layer(h, p, wb, l, after_mixers=None):
    tag = f"l{l}"
    t_row = 512
    (hn,), _, _ = _chunk_fwd(_f_rmsnorm, f"norm_mix_{tag}", t_row, [(h, 0, D)], [p["norm_mix_g"]], [(D, BF16)])
    proj = _matmul(hn, wb["w_in"], f"in_proj_{tag}", tb=True)
    (ya,), _, _ = _chunk_fwd(_f_gmlp, f"gmlp_{tag}", CHUNK, [(proj, OFF_ZA, W_ZA)], p["gmlp"], [(D, BF16)])
    lru_xs = [(proj, OFF_ZB, D), (proj, OFF_ZB + D, D)]
    (yb,), lru_saved, _ = _chunk_fwd(_f_lru, f"lru_{tag}", CHUNK, lru_xs, p["lru"], [(D, BF16)], halo_idx=(0,),
                                     carry_shapes=[(1, D)], save_carries=True)
    ssd_xs = [(proj, OFF_ZC, W_ZC), (proj, OFF_XBC, W_XBC), (proj, OFF_DT, W_DT)]
    (yc,), ssd_saved, _ = _chunk_fwd(_f_ssd, f"ssd_{tag}", CHUNK, ssd_xs, p["ssd"], [(D, BF16)], halo_idx=(1,),
                                     carry_shapes=[(HEADS_C * HEAD_DIM_C, STATE_C)], save_carries=True)
    if after_mixers is not None:
        after_mixers(yc)
    pa = _matmul(ya, wb["w_branch_a"], f"branch_a_{tag}")
    pb = _matmul(yb, wb["w_branch_b"], f"branch_b_{tag}")
    pc = _matmul(yc, wb["w_branch_c"], f"branch_c_{tag}")
    merge_xs = [(pa, 0, D), (pb, 0, D), (pc, 0, D), (proj, OFF_GATE, W_GATE)]
    (merged,), _, _ = _chunk_fwd(_f_merge, f"merge_{tag}", t_row, merge_xs, [p["b_gate"]], [(D, BF16)])
    h_mid = _matmul(merged, wb["w_out"], f"out_proj_{tag}", epilogue=lambda acc, res: (acc + res,), extras=(h,))
    (hn2,), _, _ = _chunk_fwd(_f_rmsnorm, f"norm_mlp_{tag}", t_row, [(h_mid, 0, D)], [p["norm_mlp_g"]], [(D, BF16)])

    def relu_sq(acc):
        r = jnp.maximum(acc, 0.0)
        return r, r * r

    relu_up, act = _matmul(hn2, wb["w_mlp_up"], f"mlp_up_{tag}", tb=True, outs=(F32, BF16), epilogue=relu_sq)
    h_out = _matmul(act, wb["w_mlp_down"], f"mlp_down_{tag}", epilogue=lambda acc, res: (acc + res,), extras=(h_mid,))
    saved = dict(h=h, hn=hn, proj=proj, ya=ya, yb=yb, yc=yc, lru_saved=lru_saved, ssd_saved=ssd_saved, pa=pa, pb=pb,
                 pc=pc, merged=merged, h_mid=h_mid, hn2=hn2, relu_up=relu_up, act=act, lru_xs=lru_xs, ssd_xs=ssd_xs,
                 merge_xs=merge_xs)
    return h_out, saved


def _backward_layer(dh, sv, p, wb, l):
    tag = f"l{l}"
    t_row = 512
    g = {}
    d_up = _matmul(dh, wb["w_mlp_down"], f"d_act_{tag}", tb=True, outs=(BF16,),
                   epilogue=lambda acc, r: (acc * (2.0 * r),), extras=(sv["relu_up"],))
    g["w_mlp_down"] = _matmul(sv["act"], dh, f"dw_mlp_down_{tag}", ta=True)
    g["w_mlp_up"] = _matmul(d_up, sv["hn2"], f"dw_mlp_up_{tag}", ta=True)
    d_hn2 = _matmul(d_up, wb["w_mlp_up"], f"d_hn2_{tag}")
    (d_mid,), (g["norm_mlp_g"],) = _chunk_bwd(_f_rmsnorm_res, f"norm_mlp_bwd_{tag}", t_row, [(sv["h_mid"], 0, D)],
                                              [p["norm_mlp_g"]], [d_hn2, dh], [F32])
    d_merged = _matmul(d_mid, wb["w_out"], f"d_merged_{tag}", tb=True)
    g["w_out"] = _matmul(sv["merged"], d_mid, f"dw_out_{tag}", ta=True)
    (d_pa, d_pb, d_pc, d_gate), (g["b_gate"],) = _chunk_bwd(
        _f_merge, f"merge_bwd_{tag}", t_row, sv["merge_xs"], [p["b_gate"]], [d_merged], [BF16] * 4)
    d_y = {}
    for br, d_p, y in (("a", d_pa, sv["ya"]), ("b", d_pb, sv["yb"]), ("c", d_pc, sv["yc"])):
        g[f"w_branch_{br}"] = _matmul(y, d_p, f"dw_branch_{br}_{tag}", ta=True)
        d_y[br] = _matmul(d_p, wb[f"w_branch_{br}"], f"d_y{br}_{tag}", tb=True)
    (d_za,), g_gmlp = _chunk_bwd(_f_gmlp, f"gmlp_bwd_{tag}", CHUNK, [(sv["proj"], OFF_ZA, W_ZA)], p["gmlp"],
                                 [d_y["a"]], [BF16])
    (d_xb, d_gt), g_lru = _chunk_bwd(_f_lru, f"lru_bwd_{tag}", CHUNK, sv["lru_xs"], p["lru"], [d_y["b"]], [BF16] * 2,
                                     halo_idx=(0,), saved=sv["lru_saved"])
    (d_zc, d_xbc, d_dt), g_ssd = _chunk_bwd(_f_ssd, f"ssd_bwd_{tag}", CHUNK, sv["ssd_xs"], p["ssd"], [d_y["c"]],
                                            [BF16] * 3, halo_idx=(1,), saved=sv["ssd_saved"])
    d_proj = [d_za, d_xb, d_gt, d_xbc, d_gate, d_zc, d_dt]
    g["w_in"] = _matmul_pieces_t(d_proj, sv["hn"], f"dw_in_{tag}")
    d_hn = _matmul_pieces(d_proj, wb["w_in"], f"d_hn_{tag}")
    (d_h,), (g["norm_mix_g"],) = _chunk_bwd(_f_rmsnorm_res, f"norm_mix_bwd_{tag}", t_row, [(sv["h"], 0, D)],
                                            [p["norm_mix_g"]], [d_hn, d_mid], [F32])
    g["w_in"] = _restore_in_proj(g["w_in"])
    for n in BIG:
        g[n] = g[n].reshape(N_DEV, g[n].shape[0] // N_DEV, g[n].shape[1])
    g["gmlp_ln_g"], g["gmlp_ln_b"], g["gmlp_w_s"] = g_gmlp[:3]
    g["gmlp_b_s"] = g_gmlp[3].T
    (g["lru_conv_w"], g["lru_conv_b"], g["lru_w_r"], g["lru_b_r"], g["lru_w_i"], g["lru_b_i"], g["lru_lambda"]) = g_lru
    g["ssd_conv_w"], g["ssd_conv_b"] = g_ssd[:2]
    g["ssd_dt_bias"], g["ssd_a_log"], g["ssd_d"] = (a[:, :HEADS_C] for a in g_ssd[2:5])
    g["ssd_norm_g"] = g_ssd[5]
    g["later"] = {"mlp": d_merged, "mix": d_za, "in": d_h}
    return d_h, g


LOSS_ROWS = 512


def _loss_and_grads(h, target, full, layer_weights, first_gathered):
    seq = h.shape[0]
    layer_p = [_layer_params(full, l) for l in range(DEPTH)]
    layer_w = [layer_weights(0, (first_gathered,))]
    saved = []
    for l in range(DEPTH):
        fetch_next = None
        if l + 1 < DEPTH:
            fetch_next = lambda y, l=l: layer_w.append(layer_weights(l + 1, (layer_w[l]["w_mlp_down"], y)))
        h, sv = _forward_layer(h, layer_p[l], layer_w[l], l, fetch_next)
        saved.append(sv)
    final_g = full["final_norm_g"].reshape(1, D)
    loss_xs = [(h, 0, D), (target, 0, D)]
    t_loss = min(LOSS_ROWS, seq)
    _, _, (loss_acc,) = _chunk_fwd(_f_loss, "loss", t_loss, loss_xs, [final_g], [], carry_shapes=[(1, 128)],
                                   final_carries=True)
    zero_acc = jnp.zeros((seq // t_loss, 1, 128), F32)
    seed = lambda shape: (lax.broadcasted_iota(jnp.int32, shape, 1) == 0).astype(F32)
    (dh, _), (g_final,) = _chunk_bwd(_f_loss, "loss_bwd", t_loss, loss_xs, [final_g], [], [F32, F32], saved=[zero_acc],
                                     carry_seed=seed)
    layer_g = [None] * DEPTH
    for l in reversed(range(DEPTH)):
        dh, layer_g[l] = _backward_layer(dh, saved[l], layer_p[l], layer_w[l], l)
    return loss_acc, dh, layer_g, g_final


def _small_views(d):
    views = {n: d[n] for n in REPLICATED}
    views["gmlp_b_s"] = d["gmlp_b_s"].reshape(DEPTH * GROUPS_A, CHUNK)
    views["final_norm_g"] = d["final_norm_g"].reshape(1, D)
    for n in SMALL_MATRICES:
        views[n] = d[n].reshape(DEPTH * D, D // HEADS_B)
    return views


def kernel(x, norm_mix_g, w_in, b_gate, gmlp_ln_g, gmlp_ln_b, gmlp_w_s, gmlp_b_s, lru_conv_w, lru_conv_b, lru_w_r, lru_b_r, lru_w_i, lru_b_i, lru_lambda, ssd_conv_w, ssd_conv_b, ssd_dt_bias, ssd_a_log, ssd_d, ssd_norm_g, w_branch_a, w_branch_b, w_branch_c, w_out, norm_mlp_g, w_mlp_up, w_mlp_down, final_norm_g, loss_target, m_norm_mix_g, m_w_in, m_b_gate, m_gmlp_ln_g, m_gmlp_ln_b, m_gmlp_w_s, m_gmlp_b_s, m_lru_conv_w, m_lru_conv_b, m_lru_w_r, m_lru_b_r, m_lru_w_i, m_lru_b_i, m_lru_lambda, m_ssd_conv_w, m_ssd_conv_b, m_ssd_dt_bias, m_ssd_a_log, m_ssd_d, m_ssd_norm_g, m_w_branch_a, m_w_branch_b, m_w_branch_c, m_w_out, m_norm_mlp_g, m_w_mlp_up, m_w_mlp_down, m_final_norm_g, v_norm_mix_g, v_w_in, v_b_gate, v_gmlp_ln_g, v_gmlp_ln_b, v_gmlp_w_s, v_gmlp_b_s, v_lru_conv_w, v_lru_conv_b, v_lru_w_r, v_lru_b_r, v_lru_w_i, v_lru_b_i, v_lru_lambda, v_ssd_conv_w, v_ssd_conv_b, v_ssd_dt_bias, v_ssd_a_log, v_ssd_d, v_ssd_norm_g, v_w_branch_a, v_w_branch_b, v_w_branch_c, v_w_out, v_norm_mlp_g, v_w_mlp_up, v_w_mlp_down, v_final_norm_g):
    args = locals()
    w = {n: args[n] for n in WEIGHTS}
    m = {n: args["m_" + n] for n in WEIGHTS}
    v = {n: args["v_" + n] for n in WEIGHTS}
    seq = x.shape[1]
    h = x.reshape(seq, D)
    target = loss_target.reshape(seq, D)

    def shard_on_wire(n, l):
        return (w[n][l].T if n in TRANSPOSED else w[n][l]).astype(BF16)

    first = _all_gather([shard_on_wire("w_in", 0)] + [w[n] for n in SMALL_SHARDED], "gather_weights_first")
    full = {n: w[n] for n in REPLICATED}
    for n, g in zip(SMALL_SHARDED, first[1:]):
        full[n] = jnp.stack([_lanes_from_devices(g[:, l]) for l in range(DEPTH)])

    def layer_weights(l, after):
        have = {"w_in": first[0]} if l == 0 else {}
        names = [n for n in BIG if n not in have]
        later = _all_gather([shard_on_wire(n, l) for n in names], f"gather_weights_l{l}", SEQ_GATHER + l, after=after)
        have.update(zip(names, later))
        wl = {n: have[n].reshape(-1, D) for n in BIG}
        wl["w_in"] = _reorder_in_proj(wl["w_in"])
        return wl

    loss_local, dh, layer_g, g_final = _loss_and_grads(h, target, full, layer_weights, first[0])
    grad_x = dh.reshape(x.shape)
    out = {}
    kinds = ("grad", "delta", "new_m", "new_v")

    sequencer_before = {}

    def reduce_scatter(slabs, tag, on_sequencer, later=()):
        keys = list(slabs)
        ids = (SEQ_TO_SIBLING, SEQ_TO_CHIPS) if on_sequencer else (None, None)
        from_sibling = _exchange_sibling([slabs[k] for k in keys], f"grads_to_sibling_{tag}", ids[0],
                                         sequencer_before.get("sibling", ()))
        from_sibling = dict(zip(keys, from_sibling))
        same_shape = {}
        for k in keys:
            same_shape.setdefault(slabs[k].shape[1:], []).append(k)
        chip_sums = {}
        for (r, c), ks in same_shape.items():
            sums = _add_sibling([slabs[k] for k in ks], [from_sibling[k] for k in ks], f"add_sibling_{tag}_{r}x{c}", later)
            chip_sums.update(zip(ks, sums))
        from_chips = _exchange_chips([chip_sums[k] for k in keys], f"grads_to_chips_{tag}", ids[1],
                                     sequencer_before.get("chips", ()))
        if on_sequencer:
            sequencer_before["sibling"], sequencer_before["chips"] = (from_sibling[keys[0]],), (from_chips[0],)
        from_chips = dict(zip(keys, from_chips))
        summed = {}
        for (r, c), ks in same_shape.items():
            sums = _sum_chips([slabs[k] for k in ks], [from_sibling[k] for k in ks], [from_chips[k] for k in ks],
                              f"sum_chips_{tag}_{r}x{c}")
            summed.update(zip(ks, sums))
        return summed

    small = {}
    for n in SMALL_SHARDED:
        small[n, None] = jnp.concatenate([_lanes_to_devices(layer_g[l][n]) for l in range(DEPTH)], axis=1)
    g_small = {n: jnp.concatenate([layer_g[l][n] for l in range(DEPTH)], axis=0) for n in REPLICATED[:-1]}
    g_small["final_norm_g"] = g_final
    for n in SMALL_MATRICES:
        small[n, None] = g_small[n].reshape(N_DEV, -1, g_small[n].shape[-1])
    reduced = {}
    groups = [(l, grp) for l in range(DEPTH - 1, -1, -1) for grp in GRADIENT_GROUPS]
    for l, grp in groups:
        if (l, grp) == groups[-1]:
            reduced.update(reduce_scatter(small, "small", True, (layer_g[l][GRADIENT_GROUPS[grp][0]],)))
        slabs = {(n, l): layer_g[l][n] for n in GRADIENT_GROUPS[grp]}
        reduced.update(reduce_scatter(slabs, f"{grp}_l{l}", True, (layer_g[l]["later"][grp],)))
    for n in BIG:
        grads = [reduced[n, l] for l in range(DEPTH)]
        if n in TRANSPOSED and w[n].shape[-1] % LANES:
            res = _adamw_transposed(grads, w[n], m[n], v[n], f"adamw_{n}")
        else:
            res = _adamw([(g.T if n in TRANSPOSED else g)[None] for g in grads], w[n], m[n], v[n], f"adamw_{n}")
        for kind, a in zip(kinds, res):
            out[kind, n] = a
    for n in SMALL_SHARDED:
        one = lambda a: a.reshape((1, -1, a.shape[-1]))
        for kind, a in zip(kinds, _adamw([reduced[n, None][None]], one(w[n]), one(m[n]), one(v[n]), f"adamw_{n}")):
            out[kind, n] = a.reshape(w[n].shape)

    to_gather = [reduced[n, None] if n in SMALL_MATRICES else g_small[n] for n in REPLICATED]
    *g_gathered, loss_terms = _all_gather(to_gather + [loss_local], "gather_small_grads", SEQ_GATHER_SMALL)
    g_all = dict(zip(REPLICATED, g_gathered))
    wv, mv, vv = _small_views(w), _small_views(m), _small_views(v)
    res, loss_sum = _adamw_small([g_all[n] for n in SMALL_VECTORS],
                                 *[[d[n] for n in SMALL_VECTORS] for d in (wv, mv, vv)], loss_terms, "adamw_vectors")
    loss = loss_sum[0, 0]
    for kind, arrays in zip(kinds, res):
        for n, a in zip(SMALL_VECTORS, arrays):
            out[kind, n] = a.reshape(w[n].shape)
    for n in SMALL_MATRICES:
        g_full = g_all[n].reshape((1, 1) + wv[n].shape)
        for kind, a in zip(kinds, _adamw([g_full[0]], wv[n][None], mv[n][None], vv[n][None], f"adamw_{n}")):
            out[kind, n] = a.reshape(w[n].shape)

    return (loss, grad_x, *[out[kind, n] for kind in kinds for n in WEIGHTS])
```

```python
import functools

import jax
import jax.numpy as jnp
from jax import lax
from jax.experimental import pallas as pl
from jax.experimental.pallas import tpu as pltpu
from jax.experimental.pallas import tpu_sc as plsc

F32 = jnp.float32
BF16 = jnp.bfloat16
MESH = pl.DeviceIdType.MESH

D = 1024
DEPTH = 2
EPS = 1e-6
CHUNK = 128
GROUPS_A = 8
HEADS_B = 8
LRU_C = 8.0
HEADS_C = 16
HEAD_DIM_C = 64
GROUPS_C = 4
STATE_C = 128
DT_PAD = 128
OFF_ZA, W_ZA = 0, 2048
OFF_ZB, W_ZB = 2048, 2048
OFF_XBC, W_XBC = 4096, 2048
OFF_GATE, W_GATE = 6144, 3072
OFF_ZC, W_ZC = 9216, 1024
OFF_DT, W_DT = 10240, DT_PAD
D_IN_PAD = 10368
N_DEV = 8
SEQ_GATHER = 1
SEQ_TO_SIBLING = SEQ_GATHER + DEPTH
SEQ_TO_CHIPS = SEQ_TO_SIBLING + 1
SEQ_GATHER_SMALL = SEQ_TO_CHIPS + 1

ADAM_LR = 0.001
ADAM_B1 = 0.9
ADAM_B2 = 0.999
ADAM_EPS = 1e-08
ADAM_WD = 0.01
ADAM_STEP = 10

VMEM_LIMIT = 56 * 1024 * 1024
HALO = 8


def _cparams(*sem):
    return pltpu.CompilerParams(dimension_semantics=sem, vmem_limit_bytes=VMEM_LIMIT)


def _bf(x):
    return x.astype(BF16)


def _dg(a, b, ca, cb):
    return lax.dot_general(a, b, (((ca,), (cb,)), ((), ())), preferred_element_type=F32)


@functools.partial(jax.custom_vjp, nondiff_argnums=(2, 3))
def _mm(a, b, ta, tb):
    return _dg(_bf(a), _bf(b), 0 if ta else 1, 1 if tb else 0)


def _mm_fwd(a, b, ta, tb):
    return _mm(a, b, ta, tb), (a, b)


def _mm_bwd(ta, tb, res, g):
    a, b = res
    ma = 1 if ta else 0
    nb = 0 if tb else 1
    gb, ab, bb = _bf(g), _bf(a), _bf(b)
    da = _dg(bb, gb, nb, 1) if ta else _dg(gb, bb, 1, nb)
    db = _dg(gb, ab, 0, ma) if tb else _dg(ab, gb, ma, 0)
    return da.astype(a.dtype), db.astype(b.dtype)


_mm.defvjp(_mm_fwd, _mm_bwd)


def _slices(x, sizes, axis):
    out, lo = [], 0
    for size in sizes:
        out.append(lax.slice_in_dim(x, lo, lo + size, axis=axis))
        lo += size
    return tuple(out)


@functools.partial(jax.custom_vjp, nondiff_argnums=(1,))
def _split_cols(x, widths):
    return _slices(x, widths, 1)


_split_cols.defvjp(lambda x, widths: (_slices(x, widths, 1), None),
                   lambda widths, _, gs: (jnp.concatenate(gs, axis=1),))


@functools.partial(jax.custom_vjp, nondiff_argnums=(1,))
def _split_rows(x, heights):
    return _slices(x, heights, 0)


_split_rows.defvjp(lambda x, heights: (_slices(x, heights, 0), None),
                   lambda heights, _, gs: (jnp.concatenate(gs, axis=0),))


def _col(x, j):
    lane = lax.broadcasted_iota(jnp.int32, x.shape, 1)
    return jnp.sum(jnp.where(lane == j, x, 0.0), axis=1, keepdims=True)


def _row(x, i):
    r = lax.broadcasted_iota(jnp.int32, x.shape, 0)
    return jnp.sum(jnp.where(r == i, x, 0.0), axis=0, keepdims=True)


def _roll_down(x, s):
    return pltpu.roll(x, s, 0)


def _roll_up(x, s):
    return pltpu.roll(x, x.shape[0] - s, 0)


def _row_iota(x):
    return lax.broadcasted_iota(jnp.int32, x.shape, 0)


@functools.partial(jax.custom_vjp, nondiff_argnums=(2,))
def _shift_rows(halo, x, s):
    if s == 0:
        return x
    return _roll_down(jnp.concatenate([halo, x], axis=0), s)[HALO:]


def _shift_rows_fwd(halo, x, s):
    return _shift_rows(halo, x, s), None


def _shift_rows_bwd(s, _, g):
    if s == 0:
        return jnp.zeros((HALO, g.shape[1]), g.dtype), g
    ge = jnp.concatenate([jnp.zeros((HALO, g.shape[1]), g.dtype), g], axis=0)
    de = _roll_up(ge, s)
    return de[:HALO], de[HALO:]


_shift_rows.defvjp(_shift_rows_fwd, _shift_rows_bwd)


SUBLANES = 8
LANES = 128


def _scan_tiles(a, b, carry, up):
    n, c = a.shape
    nt = n // SUBLANES
    a = a.reshape(nt, SUBLANES, c)
    b = b.reshape(nt, SUBLANES, c)
    sub = lax.broadcasted_iota(jnp.int32, a.shape, 1)
    s = 1
    while s < SUBLANES:
        keep = (sub < SUBLANES - s) if up else (sub >= s)
        shift = SUBLANES - s if up else s
        a_sh = jnp.where(keep, pltpu.roll(a, shift, 1), 1.0)
        b_sh = jnp.where(keep, pltpu.roll(b, shift, 1), 0.0)
        b = a * b_sh + b
        a = a * a_sh
        s *= 2
    tiles = [None] * nt
    edge = 0 if up else SUBLANES - 1
    for j in (range(nt - 1, -1, -1) if up else range(nt)):
        tiles[j] = b[j] if carry is None else b[j] + a[j] * carry
        carry = tiles[j][edge:edge + 1, :]
    return jnp.concatenate(tiles, axis=0)


@jax.custom_vjp
def _lin_scan(a, b, h0):
    return _scan_tiles(a, b, h0, up=False)


def _lin_scan_fwd(a, b, h0):
    h = _lin_scan(a, b, h0)
    return h, (a, h0, h)


def _lin_scan_bwd(res, g):
    a, h0, h = res
    n = a.shape[0]
    row = _row_iota(a)
    a_next = jnp.where(row < n - 1, _roll_up(a, 1), 0.0)
    gg = _scan_tiles(a_next, g, None, up=True)
    h_prev = jnp.where(row >= 1, _roll_down(h, 1), h0)
    return gg * h_prev, gg, _row(a * gg, 0)


_lin_scan.defvjp(_lin_scan_fwd, _lin_scan_bwd)


@jax.custom_vjp
def _cumsum_rows(x):
    n = x.shape[0]
    row = _row_iota(x)
    s = 1
    while s < n:
        x = x + jnp.where(row >= s, _roll_down(x, s), 0.0)
        s *= 2
    return x


def _cumsum_rows_fwd(x):
    return _cumsum_rows(x), None


def _cumsum_rows_bwd(_, g):
    n = g.shape[0]
    row = _row_iota(g)
    s = 1
    while s < n:
        g = g + jnp.where(row < n - s, _roll_up(g, s), 0.0)
        s *= 2
    return (g,)


_cumsum_rows.defvjp(_cumsum_rows_fwd, _cumsum_rows_bwd)


def _sigmoid(x):
    return jax.nn.sigmoid(x)


def _softplus(x):
    return jnp.maximum(x, 0.0) + jnp.log1p(jnp.exp(-jnp.abs(x)))


def _gelu(x):
    return jax.nn.gelu(x, approximate=True)


def _neg_expm1(x):
    series = -x * (1.0 + x * (0.5 + x * (1.0 / 6.0 + x * (1.0 / 24.0))))
    return jnp.where(x > -0.01, series, 1.0 - jnp.exp(x))


def _rms(x, g):
    return x * lax.rsqrt(jnp.mean(x * x, axis=-1, keepdims=True) + EPS) * g


def _f_rmsnorm(carries, halos, xs, params):
    (h,) = xs
    (g,) = params
    return (), (_rms(h, g),)


def _f_rmsnorm_res(carries, halos, xs, params):
    (h,) = xs
    (g,) = params
    return (), (_rms(h, g), h)


def _f_gmlp(carries, halos, xs, params):
    (za,) = xs
    ln_g, ln_b, w_s, b_st = params
    u, v = _split_cols(_gelu(za), (D, D))
    vc = v - jnp.mean(v, axis=-1, keepdims=True)
    vn = vc * lax.rsqrt(jnp.mean(vc * vc, axis=-1, keepdims=True) + EPS) * ln_g + ln_b
    q = CHUNK
    causal = lax.broadcasted_iota(jnp.int32, (q, q), 0) >= lax.broadcasted_iota(jnp.int32, (q, q), 1)
    mixed = []
    for g, (w, vg) in enumerate(zip(_split_rows(w_s, (q,) * GROUPS_A), _split_cols(vn, (q,) * GROUPS_A))):
        mixed.append(_mm(jnp.where(causal, w, 0.0), vg, False, False) + _col(b_st, g))
    return (), (u * jnp.concatenate(mixed, axis=1),)


def _conv4(halo, x, w, b):
    y = b + _row(w, 3) * x
    for k in range(3):
        y = y + _row(w, k) * _shift_rows(halo, x, 3 - k)
    return y


def _f_lru(carries, halos, xs, params):
    (h0,) = carries
    (halo,) = halos
    xb_pre, gate = xs
    conv_w, conv_b, w_r, b_r, w_i, b_i, lam = params
    xb = _conv4(halo, xb_pre, conv_w, conv_b)
    hd = D // HEADS_B
    r_parts, i_parts = [], []
    heads = (hd,) * HEADS_B
    for xh, wr, wi in zip(_split_cols(xb, heads), _split_rows(w_r, heads), _split_rows(w_i, heads)):
        r_parts.append(_mm(xh, wr, False, False))
        i_parts.append(_mm(xh, wi, False, False))
    r = _sigmoid(jnp.concatenate(r_parts, axis=1) + b_r)
    i = _sigmoid(jnp.concatenate(i_parts, axis=1) + b_i)
    log_a = -LRU_C * r * _softplus(-lam)
    a = jnp.exp(log_a)
    inp = jnp.sqrt(_neg_expm1(2.0 * log_a)) * (i * xb)
    h = _lin_scan(a, inp, h0)
    return (_row(h, h.shape[0] - 1),), (_gelu(gate) * h,)


def _f_ssd(carries, halos, xs, params):
    (st,) = carries
    (halo,) = halos
    z, xbc_pre, dt_raw = xs
    conv_w, conv_b, dt_bias, a_log, d_skip, norm_g = params
    t = z.shape[0]
    xc = _conv4(halo, xbc_pre, conv_w, conv_b)
    xbc = xc * _sigmoid(xc)
    x_all, b_all, c_all = _split_cols(xbc, (D, GROUPS_C * STATE_C, GROUPS_C * STATE_C))
    x_pairs = _split_cols(x_all, (128,) * (HEADS_C // 2))
    b_groups = _split_cols(b_all, (STATE_C,) * GROUPS_C)
    c_groups = _split_cols(c_all, (STATE_C,) * GROUPS_C)
    st_pairs = _split_rows(st, (128,) * (HEADS_C // 2))
    dt = _softplus(dt_raw + dt_bias)
    adt = dt * (-jnp.exp(a_log))
    acs = _cumsum_rows(adt)
    acs_t = acs.T
    a_last = _row(acs, t - 1)
    lo = lax.broadcasted_iota(jnp.int32, (t, 128), 1) < HEAD_DIM_C
    lo_rows = lax.broadcasted_iota(jnp.int32, (128, STATE_C), 0) < HEAD_DIM_C
    causal = lax.broadcasted_iota(jnp.int32, (t, t), 0) >= lax.broadcasted_iota(jnp.int32, (t, t), 1)
    y_parts, st_parts = [], []
    for g in range(GROUPS_C):
        bg, cg = b_groups[g], c_groups[g]
        cb = _mm(cg, bg, False, True)
        for pr in range(2):
            pair = 2 * g + pr
            h0, h1 = 2 * pair, 2 * pair + 1
            x2 = x_pairs[pair]
            ac0, ac1 = _col(acs, h0), _col(acs, h1)
            l0 = jnp.exp(jnp.where(causal, ac0 - _row(acs_t, h0), -1e30))
            l1 = jnp.exp(jnp.where(causal, ac1 - _row(acs_t, h1), -1e30))
            xdt = x2 * jnp.where(lo, _col(dt, h0), _col(dt, h1))
            y_diag = (_mm(cb * l0, jnp.where(lo, xdt, 0.0), False, False)
                      + _mm(cb * l1, jnp.where(lo, 0.0, xdt), False, False))
            al0, al1 = _col(a_last, h0), _col(a_last, h1)
            decay_s = jnp.where(lo, jnp.exp(al0 - ac0), jnp.exp(al1 - ac1))
            s_new = _mm(xdt * decay_s, bg, True, False)
            prev = st_pairs[pair]
            y_off = _mm(cg, prev, False, True) * jnp.where(lo, jnp.exp(ac0), jnp.exp(ac1))
            skip = jnp.where(lo, _col(d_skip, h0), _col(d_skip, h1))
            y_parts.append(y_diag + y_off + x2 * skip)
            st_parts.append(prev * jnp.where(lo_rows, jnp.exp(al0), jnp.exp(al1)) + s_new)
    y = jnp.concatenate(y_parts, axis=1) * (z * _sigmoid(z))
    gw = D // GROUPS_C
    yn = []
    for yg in _split_cols(y, (gw,) * GROUPS_C):
        yn.append(yg * lax.rsqrt(jnp.mean(yg * yg, axis=-1, keepdims=True) + EPS))
    return (jnp.concatenate(st_parts, axis=0),), (jnp.concatenate(yn, axis=1) * norm_g,)


def _f_merge(carries, halos, xs, params):
    pa, pb, pc, g_raw = xs
    (b_gate,) = params
    ga, gb, gc = _split_cols(g_raw, (D, D, D))
    m = (_sigmoid(ga + _row(b_gate, 0)) * pa + _sigmoid(gb + _row(b_gate, 1)) * pb
         + _sigmoid(gc + _row(b_gate, 2)) * pc)
    return (), (m,)


def _f_loss(carries, halos, xs, params):
    (acc,) = carries
    h, target = xs
    (g,) = params
    err = jnp.square(_rms(h, g) - target)
    part = 0.5 * jnp.sum(jnp.mean(err, axis=-1, keepdims=True), axis=0, keepdims=True)
    return (acc + part,), ()


def _x_specs(xs, t, index_of):
    specs = []
    for arr, off, width in xs:
        assert off % width == 0 and off + width <= arr.shape[1]
        specs.append(pl.BlockSpec((t, width), functools.partial(lambda j, cb: (index_of(j), cb), cb=off // width)))
    return specs


def _halo_specs(xs, halo_idx, t, index_of):
    specs = []
    for xi in halo_idx:
        _, off, width = xs[xi]
        specs.append(pl.BlockSpec(
            (HALO, width),
            functools.partial(lambda j, cb: (jnp.maximum(index_of(j) * (t // HALO) - 1, 0), cb), cb=off // width)))
    return specs


def _full_spec(a):
    return pl.BlockSpec(a.shape, functools.partial(lambda j, nd: (0,) * nd, nd=a.ndim))


def _chunk_fwd(f, name, t, xs, params, outs, halo_idx=(), carry_shapes=(), save_carries=False, final_carries=False):
    s = xs[0][0].shape[0]
    n = s // t
    nx, nh, npar, no, nc = len(xs), len(halo_idx), len(params), len(outs), len(carry_shapes)
    ns = nc if save_carries else 0
    nf = nc if final_carries else 0

    def body(*refs):
        x_refs, refs = refs[:nx], refs[nx:]
        h_refs, refs = refs[:nh], refs[nh:]
        p_refs, refs = refs[:npar], refs[npar:]
        y_refs, refs = refs[:no], refs[no:]
        s_refs, refs = refs[:ns], refs[ns:]
        f_refs, c_refs = refs[:nf], refs[nf:]
        i = pl.program_id(0)

        @pl.when(i == 0)
        def _():
            for c in c_refs:
                c[...] = jnp.zeros_like(c)

        carries = tuple(c[...] for c in c_refs)
        for s_ref, c in zip(s_refs, carries):
            s_ref[0] = c
        halos = tuple(jnp.where(i > 0, h[...].astype(F32), 0.0) for h in h_refs)
        new_c, ys = f(carries, halos, tuple(x[...].astype(F32) for x in x_refs), tuple(p[...] for p in p_refs))
        for y_ref, y in zip(y_refs, ys):
            y_ref[...] = y.astype(y_ref.dtype)
        for c, v in zip(c_refs, new_c):
            c[...] = v
        for f_ref, v in zip(f_refs, new_c):
            f_ref[...] = v

    ident = lambda j: j
    out_shape = [jax.ShapeDtypeStruct((s, w), dt) for w, dt in outs]
    out_specs = [pl.BlockSpec((t, w), lambda j: (j, 0)) for w, _ in outs]
    if save_carries:
        out_shape += [jax.ShapeDtypeStruct((n,) + tuple(cs), F32) for cs in carry_shapes]
        out_specs += [pl.BlockSpec((1,) + tuple(cs), lambda j: (j, 0, 0)) for cs in carry_shapes]
    if final_carries:
        out_shape += [jax.ShapeDtypeStruct(tuple(cs), F32) for cs in carry_shapes]
        out_specs += [pl.BlockSpec(tuple(cs), lambda j: (0, 0)) for cs in carry_shapes]
    res = pl.pallas_call(
        body, name=name, grid=(n,),
        in_specs=_x_specs(xs, t, ident) + _halo_specs(xs, halo_idx, t, ident) + [_full_spec(p) for p in params],
        out_specs=out_specs, out_shape=out_shape,
        scratch_shapes=[pltpu.VMEM(tuple(cs), F32) for cs in carry_shapes],
        compiler_params=_cparams("arbitrary"),
    )(*[x[0] for x in xs], *[xs[xi][0] for xi in halo_idx], *params)
    return res[:no], res[no:no + ns], res[no + ns:]


def _chunk_bwd(f, name, t, xs, params, dys, dx_dtypes, halo_idx=(), saved=(), carry_seed=None):
    s = xs[0][0].shape[0]
    n = s // t
    nx, nh, npar, nc, ndy = len(xs), len(halo_idx), len(params), len(saved), len(dys)

    def body(*refs):
        x_refs, refs = refs[:nx], refs[nx:]
        h_refs, refs = refs[:nh], refs[nh:]
        p_refs, refs = refs[:npar], refs[npar:]
        s_refs, refs = refs[:nc], refs[nc:]
        dy_refs, refs = refs[:ndy], refs[ndy:]
        dx_refs, refs = refs[:nx], refs[nx:]
        dp_refs, refs = refs[:npar], refs[npar:]
        dc_refs, dh_refs = refs[:nc], refs[nc:]
        j = pl.program_id(0)
        i = n - 1 - j

        @pl.when(j == 0)
        def _():
            for dc in dc_refs:
                dc[...] = jnp.zeros_like(dc) if carry_seed is None else carry_seed(dc.shape)
            for r in dh_refs + dp_refs:
                r[...] = jnp.zeros_like(r)

        carries = tuple(s_ref[0] for s_ref in s_refs)
        halos = tuple(jnp.where(i > 0, h[...].astype(F32), 0.0) for h in h_refs)
        x_vals = tuple(x[...].astype(F32) for x in x_refs)
        p_vals = tuple(p[...] for p in p_refs)
        _, vjp = jax.vjp(f, carries, halos, x_vals, p_vals)
        d_car, d_hal, d_xs, d_par = vjp((tuple(dc[...] for dc in dc_refs), tuple(d[...].astype(F32) for d in dy_refs)))
        d_xs = list(d_xs)
        for k, xi in enumerate(halo_idx):
            w = xs[xi][2]
            d_xs[xi] = d_xs[xi] + jnp.concatenate([jnp.zeros((t - HALO, w), F32), dh_refs[k][...]], axis=0)
            dh_refs[k][...] = jnp.where(i > 0, d_hal[k], 0.0)
        for dx_ref, dx in zip(dx_refs, d_xs):
            dx_ref[...] = dx.astype(dx_ref.dtype)
        for dp_ref, dp in zip(dp_refs, d_par):
            dp_ref[...] += dp
        for dc, v in zip(dc_refs, d_car):
            dc[...] = v

    rev = lambda j: n - 1 - j
    in_specs = (_x_specs(xs, t, rev) + _halo_specs(xs, halo_idx, t, rev) + [_full_spec(p) for p in params]
                + [pl.BlockSpec((1,) + a.shape[1:], lambda j: (n - 1 - j, 0, 0)) for a in saved]
                + [pl.BlockSpec((t, d.shape[1]), lambda j: (n - 1 - j, 0)) for d in dys])
    out_shape = ([jax.ShapeDtypeStruct((s, w), dt) for (_, _, w), dt in zip(xs, dx_dtypes)]
                 + [jax.ShapeDtypeStruct(p.shape, F32) for p in params])
    out_specs = ([pl.BlockSpec((t, w), lambda j: (n - 1 - j, 0)) for _, _, w in xs] + [_full_spec(p) for p in params])
    res = pl.pallas_call(
        body, name=name, grid=(n,), in_specs=in_specs, out_specs=out_specs, out_shape=out_shape,
        scratch_shapes=([pltpu.VMEM(a.shape[1:], F32) for a in saved]
                        + [pltpu.VMEM((HALO, xs[xi][2]), F32) for xi in halo_idx]),
        compiler_params=_cparams("arbitrary"),
    )(*[x[0] for x in xs], *[xs[xi][0] for xi in halo_idx], *params, *saved, *dys)
    return res[:nx], res[nx:]


def _tile(dim, pref):
    for cand in pref:
        if dim % cand == 0:
            return cand
    return dim


def _matmul(a, b, name, ta=False, tb=False, outs=(F32,), epilogue=None, extras=(), row_params=()):
    m, k = (a.shape[1], a.shape[0]) if ta else a.shape
    n = b.shape[0] if tb else b.shape[1]
    tm = _tile(m, (1024, 1152, 512, 256, 128))
    tn = _tile(n, (1152, 1024, 512, 256, 128))
    tk = _tile(k, (1024, 1152, 512, 256, 128))
    nk = k // tk
    ne, no = len(extras) + len(row_params), len(outs)
    ca, cb = (0 if ta else 1), (1 if tb else 0)

    def body(*refs):
        a_ref, b_ref = refs[:2]
        e_refs = refs[2:2 + ne]
        o_refs = refs[2 + ne:2 + ne + no]

        acc = refs[-1]
        kk = pl.program_id(2)

        @pl.when(kk == 0)
        def _():
            acc[...] = jnp.zeros_like(acc)

        acc[...] += _dg(_bf(a_ref[...]), _bf(b_ref[...]), ca, cb)

        @pl.when(kk == nk - 1)
        def _():
            res = acc[...]
            vals = (res,) if epilogue is None else epilogue(res, *[e[...] for e in e_refs])
            for o_ref, v in zip(o_refs, vals):
                o_ref[...] = v.astype(o_ref.dtype)

    a_spec = pl.BlockSpec((tk, tm), lambda i, j, kk: (kk, i)) if ta else pl.BlockSpec((tm, tk), lambda i, j, kk: (i, kk))
    b_spec = pl.BlockSpec((tn, tk), lambda i, j, kk: (j, kk)) if tb else pl.BlockSpec((tk, tn), lambda i, j, kk: (kk, j))
    mn_spec = pl.BlockSpec((tm, tn), lambda i, j, kk: (i, j))
    res = pl.pallas_call(
        body, name=name, grid=(m // tm, n // tn, nk),
        in_specs=[a_spec, b_spec] + [mn_spec] * len(extras)
        + [pl.BlockSpec((1, tn), lambda i, j, kk: (0, j))] * len(row_params),
        out_specs=[mn_spec] * no,
        out_shape=[jax.ShapeDtypeStruct((m, n), dt) for dt in outs],
        scratch_shapes=[pltpu.VMEM((tm, tn), F32)],
        compiler_params=_cparams("parallel", "parallel", "arbitrary"),
    )(a, b, *extras, *row_params)
    return res if no > 1 else res[0]


def _elementwise_block(r, c):
    if r % 8 == 0 and r >= 8:
        return _tile(r, (256, 128, 64, 32, 16, 8)), c
    return r, _tile(c, (256, 128))


PIECE_TILE = 1024


def _piece_steps(pieces):
    steps, s0 = [], 0
    for a in pieces:
        wt = min(a.shape[1], PIECE_TILE)
        assert a.shape[1] % wt == 0
        steps.append((s0, a.shape[1] // wt, wt))
        s0 += a.shape[1] // wt
    return steps, s0


def _matmul_pieces(pieces, b, name):
    steps, n_steps = _piece_steps(pieces)
    s_rows, n = pieces[0].shape[0], b.shape[1]
    tm = _tile(s_rows, (1024, 512, 256, 128))
    tail_rows = steps[-1][2]
    full_steps = n_steps - 1 if tail_rows < PIECE_TILE else n_steps
    b_tail = b[full_steps * PIECE_TILE:]
    np_ = len(pieces)

    def body(*refs):
        a_refs, b_ref, tail_ref, o_ref, acc = refs[:np_], refs[np_], refs[np_ + 1], refs[np_ + 2], refs[np_ + 3]
        s = pl.program_id(1)

        @pl.when(s == 0)
        def _():
            acc[...] = jnp.zeros_like(acc)

        for a_ref, (s0, ns, wt) in zip(a_refs, steps):
            @pl.when((s >= s0) & (s < s0 + ns))
            def _(a_ref=a_ref, wt=wt):
                rhs = b_ref[...] if wt == PIECE_TILE else tail_ref[...]
                acc[...] += _dg(_bf(a_ref[...]), _bf(rhs), 1, 0)

        @pl.when(s == n_steps - 1)
        def _():
            o_ref[...] = acc[...]

    a_specs = [pl.BlockSpec((tm, wt), functools.partial(lambda i, s, s0, ns: (i, jnp.clip(s - s0, 0, ns - 1)), s0=s0, ns=ns))
               for s0, ns, wt in steps]
    return pl.pallas_call(
        body, name=name, grid=(s_rows // tm, n_steps),
        in_specs=a_specs + [pl.BlockSpec((PIECE_TILE, n), lambda i, s: (jnp.minimum(s, full_steps - 1), 0)),
                            pl.BlockSpec(b_tail.shape, lambda i, s: (0, 0))],
        out_specs=pl.BlockSpec((tm, n), lambda i, s: (i, 0)), out_shape=jax.ShapeDtypeStruct((s_rows, n), F32),
        scratch_shapes=[pltpu.VMEM((tm, n), F32)],
        compiler_params=_cparams("parallel", "arbitrary"),
    )(*pieces, b, b_tail)


def _matmul_pieces_t(pieces, b, name):
    steps, n_steps = _piece_steps(pieces)
    s_rows, n = b.shape
    tk = _tile(s_rows, (1024, 512, 256, 128))
    nk = s_rows // tk
    total = sum(a.shape[1] for a in pieces)
    np_ = len(pieces)

    def body(*refs):
        a_refs, b_ref, o_ref, acc = refs[:np_], refs[np_], refs[np_ + 1], refs[np_ + 2]
        s, kk = pl.program_id(0), pl.program_id(1)

        @pl.when(kk == 0)
        def _():
            acc[...] = jnp.zeros_like(acc)

        for a_ref, (s0, ns, wt) in zip(a_refs, steps):
            @pl.when((s >= s0) & (s < s0 + ns))
            def _(a_ref=a_ref, wt=wt):
                acc[0:wt, :] += _dg(_bf(a_ref[...]), _bf(b_ref[...]), 0, 0)

        @pl.when(kk == nk - 1)
        def _():
            o_ref[...] = acc[...]

    def a_index(s, kk, s0, ns):
        active = (s >= s0) & (s < s0 + ns)
        return jnp.where(active, kk, jnp.where(s < s0, 0, nk - 1)), jnp.clip(s - s0, 0, ns - 1)

    a_specs = [pl.BlockSpec((tk, wt), functools.partial(a_index, s0=s0, ns=ns)) for s0, ns, wt in steps]
    return pl.pallas_call(
        body, name=name, grid=(n_steps, nk),
        in_specs=a_specs + [pl.BlockSpec((tk, n), lambda s, kk: (kk, 0))],
        out_specs=pl.BlockSpec((PIECE_TILE, n), lambda s, kk: (s, 0)), out_shape=jax.ShapeDtypeStruct((total, n), F32),
        scratch_shapes=[pltpu.VMEM((PIECE_TILE, n), F32)],
        compiler_params=_cparams("parallel", "arbitrary"),
    )(*pieces, b)


def _adamw_math(g, w, m, v):
    m_new = ADAM_B1 * m + (1.0 - ADAM_B1) * g
    v_new = ADAM_B2 * v + (1.0 - ADAM_B2) * jnp.square(g)
    m_hat = m_new / (1.0 - ADAM_B1 ** ADAM_STEP)
    v_hat = v_new / (1.0 - ADAM_B2 ** ADAM_STEP)
    return -ADAM_LR * (m_hat / (jnp.sqrt(v_hat) + ADAM_EPS) + ADAM_WD * w), m_new, v_new


def _adamw(parts, w, m, v, name):
    nl, r, c = w.shape
    k = parts[0].shape[0]
    tr = _tile(r, (128, 64, 32, 16, 8))
    nb = r // tr

    def body(*refs):
        p_refs, (w_ref, m_ref, v_ref), outs = refs[:nl], refs[nl:nl + 3], refs[nl + 3:]
        layer = pl.program_id(0)
        for q in range(nl):
            @pl.when(layer == q)
            def _(q=q):
                g = p_refs[q][0]
                for j in range(1, k):
                    g = g + p_refs[q][j]
                vals = (g,) + _adamw_math(g, w_ref[0], m_ref[0], v_ref[0])
                for o_ref, val in zip(outs, vals):
                    o_ref[0] = val

    spec = pl.BlockSpec((1, tr, c), lambda l, i: (l, i, 0))
    part_specs = [pl.BlockSpec((k, tr, c), functools.partial(
        lambda l, i, q: (0, jnp.where(l == q, i, jnp.where(l < q, 0, nb - 1)), 0), q=q)) for q in range(nl)]
    return pl.pallas_call(
        body, name=name, grid=(nl, nb), in_specs=part_specs + [spec] * 3,
        out_specs=[spec] * 4, out_shape=[jax.ShapeDtypeStruct((nl, r, c), F32)] * 4,
        compiler_params=_cparams("arbitrary", "arbitrary"),
    )(*parts, w, m, v)


def _adamw_transposed(grads, w, m, v, name):
    nl, r, c = w.shape
    tc = 64
    views = [jnp.transpose(a, (2, 0, 1)) for a in (w, m, v)]

    def body(*refs):
        g_refs, (w_ref, m_ref, v_ref), outs = refs[:nl], refs[nl:nl + 3], refs[nl + 3:]
        for l in range(nl):
            g = g_refs[l][...]
            vals = (g,) + _adamw_math(g, w_ref[:, l, :], m_ref[:, l, :], v_ref[:, l, :])
            for o_ref, val in zip(outs, vals):
                o_ref[:, l, :] = val

    spec = pl.BlockSpec((tc, nl, r), lambda i: (i, 0, 0))
    res = pl.pallas_call(
        body, name=name, grid=(pl.cdiv(c, tc),), in_specs=[pl.BlockSpec((tc, r), lambda i: (i, 0))] * nl + [spec] * 3,
        out_specs=[spec] * 4, out_shape=[jax.ShapeDtypeStruct((c, nl, r), F32)] * 4,
        compiler_params=_cparams("parallel"),
    )(*grads, *views)
    return [jnp.transpose(a, (1, 2, 0)) for a in res]


def _adamw_small(gathered, ws, ms, vs, loss_terms, name):
    n = len(ws)

    def device_sum(ref):
        s = ref[0]
        for j in range(1, N_DEV):
            s = s + ref[j]
        return s

    def body(*refs):
        g_refs, w_refs, m_refs, v_refs = refs[:n], refs[n:2 * n], refs[2 * n:3 * n], refs[3 * n:4 * n]
        loss_ref, outs, loss_out = refs[4 * n], refs[4 * n + 1:-1], refs[-1]
        for i in range(n):
            g = device_sum(g_refs[i])
            vals = (g,) + _adamw_math(g, w_refs[i][...], m_refs[i][...], v_refs[i][...])
            for kind, val in enumerate(vals):
                outs[kind * n + i][...] = val
        loss_out[...] = device_sum(loss_ref)

    res = pl.pallas_call(
        body, name=name,
        out_shape=[jax.ShapeDtypeStruct(a.shape, F32) for _ in range(4) for a in ws] + [
            jax.ShapeDtypeStruct(loss_terms.shape[1:], F32)],
        compiler_params=pltpu.CompilerParams(vmem_limit_bytes=VMEM_LIMIT),
    )(*gathered, *ws, *ms, *vs, loss_terms)
    return [res[kind * n:(kind + 1) * n] for kind in range(4)], res[-1]


ANY = pl.BlockSpec(memory_space=pl.ANY)


def _place():
    return lax.axis_index("x"), lax.axis_index("y"), lax.axis_index("c")


def _handshake(peers):
    barrier = pltpu.get_barrier_semaphore()
    for peer in peers:
        pl.semaphore_signal(barrier, inc=1, device_id=peer, device_id_type=MESH)
    pl.semaphore_wait(barrier, len(peers))


def _comm_call(body, name, inputs, out_shape, scratch, sequencer_id=None, after=()):
    if sequencer_id is None:
        return pl.pallas_call(body, name=name, out_shape=out_shape, in_specs=[ANY] * len(inputs),
                              out_specs=[ANY] * len(out_shape), scratch_shapes=scratch)(*inputs)
    n_in, n_after = len(inputs), len(after)

    def sequencer_body(*refs):
        body(*refs[:n_in], *refs[n_in + n_after:])

    return pl.kernel(
        sequencer_body, out_type=out_shape, mesh=plsc.ScalarSubcoreMesh(axis_name="sequencer", num_cores=1),
        scratch_types=scratch, compiler_params=pltpu.CompilerParams(collective_id=sequencer_id), name=name,
    )(*inputs, *after)


def _all_gather(blocks, name, sequencer_id=None, after=()):
    n = len(blocks)

    def body(*refs):
        x_refs, out_refs = refs[:n], refs[n:2 * n]
        send_sems, recv_sems, local_sems = refs[2 * n:]
        x, y, c = _place()
        me, sibling = (x, y, c), (x, y, 1 - c)
        chips = [(1 - x, y), (x, 1 - y), (1 - x, 1 - y)]
        if sequencer_id is not None:
            _handshake([sibling] + [(*chip, c) for chip in chips])

        def slot(a, px, py, pc):
            return out_refs[a].at[4 * px + 2 * py + pc]

        def copy(a, k, blk, to, src=None):
            return pltpu.make_async_remote_copy(
                src_ref=slot(a, *blk) if src is None else src, dst_ref=slot(a, *blk),
                send_sem=send_sems.at[7 * a + k], recv_sem=recv_sems.at[7 * a + k], device_id=to, device_id_type=MESH)

        mine = [pltpu.make_async_copy(x_refs[a], slot(a, *me), local_sems.at[a]) for a in range(n)]
        first = []
        for a in range(n):
            mine[a].start()
            first.append(copy(a, 0, me, sibling, src=x_refs[a]))
            first += [copy(a, 1 + j, me, (*chip, c), src=x_refs[a]) for j, chip in enumerate(chips)]
        for cp in first:
            cp.start()
        passed = []
        for j, chip in enumerate(chips):
            for a in range(n):
                copy(a, 1 + j, (*chip, c), me).wait_recv()
                passed.append(copy(a, 4 + j, (*chip, c), sibling))
                passed[-1].start()
        for a in range(n):
            copy(a, 0, sibling, me).wait_recv()
            for j, chip in enumerate(chips):
                copy(a, 4 + j, (*chip, 1 - c), me).wait_recv()
        for cp in first + passed:
            cp.wait_send()
        for cp in mine:
            cp.wait()

    return _comm_call(
        body, name, blocks, [jax.ShapeDtypeStruct((N_DEV,) + b.shape, b.dtype) for b in blocks],
        [pltpu.SemaphoreType.DMA((7 * n,)), pltpu.SemaphoreType.DMA((7 * n,)), pltpu.SemaphoreType.DMA((n,))],
        sequencer_id, after)


def _exchange_sibling(gs, name, sequencer_id=None, after=()):
    n = len(gs)

    def body(*refs):
        g_refs, out_refs = refs[:n], refs[n:2 * n]
        send_sems, recv_sems = refs[2 * n:]
        x, y, c = _place()
        if sequencer_id is not None:
            _handshake([(x, y, 1 - c)])
        copies = [pltpu.make_async_remote_copy(
            src_ref=g_refs[a].at[2 * k + 1 - c], dst_ref=out_refs[a].at[k], send_sem=send_sems.at[4 * a + k],
            recv_sem=recv_sems.at[4 * a + k], device_id=(x, y, 1 - c), device_id_type=MESH)
            for a in range(n) for k in range(4)]
        for cp in copies:
            cp.start()
        for cp in copies:
            cp.wait()

    return _comm_call(body, name, gs, [jax.ShapeDtypeStruct((4,) + g.shape[1:], g.dtype) for g in gs],
                      [pltpu.SemaphoreType.DMA((4 * n,)), pltpu.SemaphoreType.DMA((4 * n,))], sequencer_id, after)


def _other_chips():
    x, y = lax.axis_index("x"), lax.axis_index("y")
    return jnp.stack([2 * (1 - x) + y, 2 * x + 1 - y, 2 * (1 - x) + 1 - y]).astype(jnp.int32)


def _add_sibling(gs, r1s, name, after=()):
    n = len(gs)
    _, r, w = gs[0].shape
    tr, tc = _elementwise_block(r, w)
    chips = _other_chips()
    slabs = 2 * chips + lax.axis_index("c").astype(jnp.int32)

    def body(slab_ref, chip_ref, *refs):
        g_refs, r_refs, o_refs = refs[:n], refs[n:2 * n], refs[2 * n + len(after):]
        for g_ref, r_ref, o_ref in zip(g_refs, r_refs, o_refs):
            o_ref[...] = (g_ref[...] + r_ref[...]).astype(BF16)

    return pl.pallas_call(
        body, name=name, out_shape=[jax.ShapeDtypeStruct((3, r, w), BF16)] * n,
        grid_spec=pltpu.PrefetchScalarGridSpec(
            num_scalar_prefetch=2, grid=(3, r // tr, w // tc),
            in_specs=[pl.BlockSpec((1, tr, tc), lambda k, i, j, slab_ref, chip_ref: (slab_ref[k], i, j))] * n
            + [pl.BlockSpec((1, tr, tc), lambda k, i, j, slab_ref, chip_ref: (chip_ref[k], i, j))] * n
            + [ANY] * len(after),
            out_specs=[pl.BlockSpec((1, tr, tc), lambda k, i, j, slab_ref, chip_ref: (k, i, j))] * n),
        compiler_params=_cparams("parallel", "parallel", "parallel"),
    )(slabs, chips, *gs, *r1s, *after)


def _exchange_chips(ps, name, sequencer_id=None, after=()):
    n = len(ps)

    def body(*refs):
        p_refs, out_refs = refs[:n], refs[n:2 * n]
        send_sems, recv_sems = refs[2 * n:]
        x, y, c = _place()
        chips = [(1 - x, y), (x, 1 - y), (1 - x, 1 - y)]
        if sequencer_id is not None:
            _handshake([(*chip, c) for chip in chips])
        copies = [pltpu.make_async_remote_copy(
            src_ref=p_refs[a].at[j], dst_ref=out_refs[a].at[j], send_sem=send_sems.at[3 * a + j],
            recv_sem=recv_sems.at[3 * a + j], device_id=(px, py, c), device_id_type=MESH)
            for a in range(n) for j, (px, py) in enumerate(chips)]
        for cp in copies:
            cp.start()
        for cp in copies:
            cp.wait()

    return _comm_call(body, name, ps, [jax.ShapeDtypeStruct(p.shape, p.dtype) for p in ps],
                      [pltpu.SemaphoreType.DMA((3 * n,)), pltpu.SemaphoreType.DMA((3 * n,))], sequencer_id, after)


def _sum_chips(gs, r1s, others, name):
    n = len(gs)
    _, r, c = gs[0].shape
    tr, tc = _elementwise_block(r, c)
    chip = 2 * lax.axis_index("x") + lax.axis_index("y")
    place = jnp.stack([2 * chip + lax.axis_index("c"), chip]).astype(jnp.int32)

    def body(place_ref, *refs):
        for g_ref, r_ref, others_ref, o_ref in zip(refs[:n], refs[n:2 * n], refs[2 * n:3 * n], refs[3 * n:]):
            s = g_ref[0] + r_ref[0]
            for j in range(3):
                s = s + others_ref[j].astype(F32)
            o_ref[...] = s

    return pl.pallas_call(
        body, name=name, out_shape=[jax.ShapeDtypeStruct((r, c), F32)] * n,
        grid_spec=pltpu.PrefetchScalarGridSpec(
            num_scalar_prefetch=1, grid=(r // tr, c // tc),
            in_specs=[pl.BlockSpec((1, tr, tc), lambda i, j, place_ref: (place_ref[0], i, j))] * n
            + [pl.BlockSpec((1, tr, tc), lambda i, j, place_ref: (place_ref[1], i, j))] * n
            + [pl.BlockSpec((3, tr, tc), lambda i, j, place_ref: (0, i, j))] * n,
            out_specs=[pl.BlockSpec((tr, tc), lambda i, j, place_ref: (i, j))] * n),
        compiler_params=_cparams("parallel", "parallel"),
    )(place, *gs, *r1s, *others)


def _reorder_in_proj(wt):
    za_zb, zc, xbc, dt, gates = (wt[:4096], wt[4096:5120], wt[5120:7168], wt[7168:7184], wt[7184:])
    return jnp.concatenate([za_zb, xbc, gates, zc, dt, jnp.zeros((DT_PAD - 16, wt.shape[1]), wt.dtype)], axis=0)


def _restore_in_proj(wt):
    return jnp.concatenate([wt[:4096], wt[OFF_ZC:OFF_ZC + W_ZC], wt[OFF_XBC:OFF_XBC + W_XBC],
                            wt[OFF_DT:OFF_DT + 16], wt[OFF_GATE:OFF_GATE + W_GATE]], axis=0)


def _lanes_from_devices(g):
    return jnp.moveaxis(g, 0, 1).reshape(g.shape[1], N_DEV * g.shape[2])


def _lanes_to_devices(a):
    return jnp.moveaxis(a.reshape(a.shape[0], N_DEV, a.shape[1] // N_DEV), 1, 0)


def _pad_lanes(a, width):
    return jnp.pad(a, ((0, 0), (0, width - a.shape[1])))


BIG = ("w_in", "w_branch_a", "w_branch_b", "w_branch_c", "w_out", "w_mlp_up", "w_mlp_down")
SMALL_SHARDED = ("b_gate", "lru_conv_w", "ssd_conv_w")
REPLICATED = ("norm_mix_g", "gmlp_ln_g", "gmlp_ln_b", "gmlp_w_s", "gmlp_b_s", "lru_conv_b", "lru_w_r", "lru_b_r",
              "lru_w_i", "lru_b_i", "lru_lambda", "ssd_conv_b", "ssd_dt_bias", "ssd_a_log", "ssd_d", "ssd_norm_g",
              "norm_mlp_g", "final_norm_g")
WEIGHTS = ("norm_mix_g", "w_in", "b_gate", "gmlp_ln_g", "gmlp_ln_b", "gmlp_w_s", "gmlp_b_s", "lru_conv_w", "lru_conv_b",
           "lru_w_r", "lru_b_r", "lru_w_i", "lru_b_i", "lru_lambda", "ssd_conv_w", "ssd_conv_b", "ssd_dt_bias",
           "ssd_a_log", "ssd_d", "ssd_norm_g", "w_branch_a", "w_branch_b", "w_branch_c", "w_out", "norm_mlp_g",
           "w_mlp_up", "w_mlp_down", "final_norm_g")
TRANSPOSED = ("w_in", "w_mlp_up")
SMALL_MATRICES = ("gmlp_w_s", "lru_w_r", "lru_w_i")
SMALL_VECTORS = tuple(n for n in REPLICATED if n not in SMALL_MATRICES)
GRADIENT_GROUPS = {"mlp": ("w_mlp_up", "w_mlp_down"), "mix": ("w_branch_a", "w_branch_b", "w_branch_c", "w_out"),
                   "in": ("w_in",)}


def _layer_params(full, l):
    row = lambda a: a.reshape(1, -1)
    return dict(
        norm_mix_g=row(full["norm_mix_g"][l]), norm_mlp_g=row(full["norm_mlp_g"][l]),
        gmlp=(row(full["gmlp_ln_g"][l]), row(full["gmlp_ln_b"][l]), full["gmlp_w_s"][l].reshape(GROUPS_A * CHUNK, CHUNK),
              full["gmlp_b_s"][l].T),
        lru=(full["lru_conv_w"][l], row(full["lru_conv_b"][l]), full["lru_w_r"][l].reshape(D, D // HEADS_B),
             row(full["lru_b_r"][l]), full["lru_w_i"][l].reshape(D, D // HEADS_B), row(full["lru_b_i"][l]),
             row(full["lru_lambda"][l])),
        ssd=(full["ssd_conv_w"][l], row(full["ssd_conv_b"][l]), _pad_lanes(row(full["ssd_dt_bias"][l]), DT_PAD),
             _pad_lanes(row(full["ssd_a_log"][l]), DT_PAD), _pad_lanes(row(full["ssd_d"][l]), DT_PAD),
             row(full["ssd_norm_g"][l])),
        b_gate=full["b_gate"][l],
    )


def _forward_layer(h, hn, p, wb, l, after_mixers=None, next_norm_g=None):
    tag = f"l{l}"
    t_row = 512
    if hn is None:
        (hn,), _, _ = _chunk_fwd(_f_rmsnorm, f"norm_mix_{tag}", t_row, [(h, 0, D)], [p["norm_mix_g"]], [(D, BF16)])
    proj = _matmul(hn, wb["w_in"], f"in_proj_{tag}", tb=True)
    (ya,), _, _ = _chunk_fwd(_f_gmlp, f"gmlp_{tag}", CHUNK, [(proj, OFF_ZA, W_ZA)], p["gmlp"], [(D, BF16)])
    lru_xs = [(proj, OFF_ZB, D), (proj, OFF_ZB + D, D)]
    (yb,), lru_saved, _ = _chunk_fwd(_f_lru, f"lru_{tag}", CHUNK, lru_xs, p["lru"], [(D, BF16)], halo_idx=(0,),
                                     carry_shapes=[(1, D)], save_carries=True)
    ssd_xs = [(proj, OFF_ZC, W_ZC), (proj, OFF_XBC, W_XBC), (proj, OFF_DT, W_DT)]
    (yc,), ssd_saved, _ = _chunk_fwd(_f_ssd, f"ssd_{tag}", CHUNK, ssd_xs, p["ssd"], [(D, BF16)], halo_idx=(1,),
                                     carry_shapes=[(HEADS_C * HEAD_DIM_C, STATE_C)], save_carries=True)
    if after_mixers is not None:
        after_mixers(yc)
    pa = _matmul(ya, wb["w_branch_a"], f"branch_a_{tag}")
    pb = _matmul(yb, wb["w_branch_b"], f"branch_b_{tag}")
    pc = _matmul(yc, wb["w_branch_c"], f"branch_c_{tag}")
    merge_xs = [(pa, 0, D), (pb, 0, D), (pc, 0, D), (proj, OFF_GATE, W_GATE)]
    (merged,), _, _ = _chunk_fwd(_f_merge, f"merge_{tag}", t_row, merge_xs, [p["b_gate"]], [(D, BF16)])

    def add_and_norm(acc, res, g):
        s = acc + res
        return s, _rms(s, g)

    h_mid, hn2 = _matmul(merged, wb["w_out"], f"out_proj_{tag}", outs=(F32, BF16), epilogue=add_and_norm, extras=(h,),
                         row_params=(p["norm_mlp_g"],))

    def relu_sq(acc):
        r = jnp.maximum(acc, 0.0)
        return r, r * r

    relu_up, act = _matmul(hn2, wb["w_mlp_up"], f"mlp_up_{tag}", tb=True, outs=(F32, BF16), epilogue=relu_sq)
    if next_norm_g is None:
        hn_out = None
        h_out = _matmul(act, wb["w_mlp_down"], f"mlp_down_{tag}", epilogue=lambda acc, res: (acc + res,), extras=(h_mid,))
    else:
        h_out, hn_out = _matmul(act, wb["w_mlp_down"], f"mlp_down_{tag}", outs=(F32, BF16), epilogue=add_and_norm,
                                extras=(h_mid,), row_params=(next_norm_g,))
    saved = dict(h=h, hn=hn, proj=proj, ya=ya, yb=yb, yc=yc, lru_saved=lru_saved, ssd_saved=ssd_saved, pa=pa, pb=pb,
                 pc=pc, merged=merged, h_mid=h_mid, hn2=hn2, relu_up=relu_up, act=act, lru_xs=lru_xs, ssd_xs=ssd_xs,
                 merge_xs=merge_xs)
    return h_out, hn_out, saved


def _backward_layer(dh, sv, p, wb, l):
    tag = f"l{l}"
    t_row = 512
    g = {}
    d_up = _matmul(dh, wb["w_mlp_down"], f"d_act_{tag}", tb=True, outs=(BF16,),
                   epilogue=lambda acc, r: (acc * (2.0 * r),), extras=(sv["relu_up"],))
    g["w_mlp_down"] = _matmul(sv["act"], dh, f"dw_mlp_down_{tag}", ta=True)
    g["w_mlp_up"] = _matmul(d_up, sv["hn2"], f"dw_mlp_up_{tag}", ta=True)
    d_hn2 = _matmul(d_up, wb["w_mlp_up"], f"d_hn2_{tag}")
    (d_mid,), (g["norm_mlp_g"],) = _chunk_bwd(_f_rmsnorm_res, f"norm_mlp_bwd_{tag}", t_row, [(sv["h_mid"], 0, D)],
                                              [p["norm_mlp_g"]], [d_hn2, dh], [F32])
    d_merged = _matmul(d_mid, wb["w_out"], f"d_merged_{tag}", tb=True)
    g["w_out"] = _matmul(sv["merged"], d_mid, f"dw_out_{tag}", ta=True)
    (d_pa, d_pb, d_pc, d_gate), (g["b_gate"],) = _chunk_bwd(
        _f_merge, f"merge_bwd_{tag}", t_row, sv["merge_xs"], [p["b_gate"]], [d_merged], [BF16] * 4)
    d_y = {}
    for br, d_p, y in (("a", d_pa, sv["ya"]), ("b", d_pb, sv["yb"]), ("c", d_pc, sv["yc"])):
        g[f"w_branch_{br}"] = _matmul(y, d_p, f"dw_branch_{br}_{tag}", ta=True)
        d_y[br] = _matmul(d_p, wb[f"w_branch_{br}"], f"d_y{br}_{tag}", tb=True)
    (d_za,), g_gmlp = _chunk_bwd(_f_gmlp, f"gmlp_bwd_{tag}", CHUNK, [(sv["proj"], OFF_ZA, W_ZA)], p["gmlp"],
                                 [d_y["a"]], [BF16])
    (d_xb, d_gt), g_lru = _chunk_bwd(_f_lru, f"lru_bwd_{tag}", CHUNK, sv["lru_xs"], p["lru"], [d_y["b"]], [BF16] * 2,
                                     halo_idx=(0,), saved=sv["lru_saved"])
    (d_zc, d_xbc, d_dt), g_ssd = _chunk_bwd(_f_ssd, f"ssd_bwd_{tag}", CHUNK, sv["ssd_xs"], p["ssd"], [d_y["c"]],
                                            [BF16] * 3, halo_idx=(1,), saved=sv["ssd_saved"])
    d_proj = [d_za, d_xb, d_gt, d_xbc, d_gate, d_zc, d_dt]
    g["w_in"] = _matmul_pieces_t(d_proj, sv["hn"], f"dw_in_{tag}")
    d_hn = _matmul_pieces(d_proj, wb["w_in"], f"d_hn_{tag}")
    (d_h,), (g["norm_mix_g"],) = _chunk_bwd(_f_rmsnorm_res, f"norm_mix_bwd_{tag}", t_row, [(sv["h"], 0, D)],
                                            [p["norm_mix_g"]], [d_hn, d_mid], [F32])
    g["w_in"] = _restore_in_proj(g["w_in"])
    for n in BIG:
        g[n] = g[n].reshape(N_DEV, g[n].shape[0] // N_DEV, g[n].shape[1])
    g["gmlp_ln_g"], g["gmlp_ln_b"], g["gmlp_w_s"] = g_gmlp[:3]
    g["gmlp_b_s"] = g_gmlp[3].T
    (g["lru_conv_w"], g["lru_conv_b"], g["lru_w_r"], g["lru_b_r"], g["lru_w_i"], g["lru_b_i"], g["lru_lambda"]) = g_lru
    g["ssd_conv_w"], g["ssd_conv_b"] = g_ssd[:2]
    g["ssd_dt_bias"], g["ssd_a_log"], g["ssd_d"] = (a[:, :HEADS_C] for a in g_ssd[2:5])
    g["ssd_norm_g"] = g_ssd[5]
    g["later"] = {"mlp": d_merged, "mix": d_za, "in": d_h}
    return d_h, g


LOSS_ROWS = 512


def _loss_and_grads(h, target, full, layer_weights, first_gathered):
    seq = h.shape[0]
    layer_p = [_layer_params(full, l) for l in range(DEPTH)]
    layer_w = [layer_weights(0, (first_gathered,))]
    saved = []
    hn = None
    for l in range(DEPTH):
        fetch_next = next_norm_g = None
        if l + 1 < DEPTH:
            fetch_next = lambda y, l=l: layer_w.append(layer_weights(l + 1, (layer_w[l]["w_mlp_down"], y)))
            next_norm_g = layer_p[l + 1]["norm_mix_g"]
        h, hn, sv = _forward_layer(h, hn, layer_p[l], layer_w[l], l, fetch_next, next_norm_g)
        saved.append(sv)
    final_g = full["final_norm_g"].reshape(1, D)
    loss_xs = [(h, 0, D), (target, 0, D)]
    t_loss = min(LOSS_ROWS, seq)
    _, _, (loss_acc,) = _chunk_fwd(_f_loss, "loss", t_loss, loss_xs, [final_g], [], carry_shapes=[(1, 128)],
                                   final_carries=True)
    zero_acc = jnp.zeros((seq // t_loss, 1, 128), F32)
    seed = lambda shape: (lax.broadcasted_iota(jnp.int32, shape, 1) == 0).astype(F32)
    (dh, _), (g_final,) = _chunk_bwd(_f_loss, "loss_bwd", t_loss, loss_xs, [final_g], [], [F32, F32], saved=[zero_acc],
                                     carry_seed=seed)
    layer_g = [None] * DEPTH
    for l in reversed(range(DEPTH)):
        dh, layer_g[l] = _backward_layer(dh, saved[l], layer_p[l], layer_w[l], l)
    return loss_acc, dh, layer_g, g_final


def _small_views(d):
    views = {n: d[n] for n in REPLICATED}
    views["gmlp_b_s"] = d["gmlp_b_s"].reshape(DEPTH * GROUPS_A, CHUNK)
    views["final_norm_g"] = d["final_norm_g"].reshape(1, D)
    for n in SMALL_MATRICES:
        views[n] = d[n].reshape(DEPTH * D, D // HEADS_B)
    return views


def kernel(x, norm_mix_g, w_in, b_gate, gmlp_ln_g, gmlp_ln_b, gmlp_w_s, gmlp_b_s, lru_conv_w, lru_conv_b, lru_w_r, lru_b_r, lru_w_i, lru_b_i, lru_lambda, ssd_conv_w, ssd_conv_b, ssd_dt_bias, ssd_a_log, ssd_d, ssd_norm_g, w_branch_a, w_branch_b, w_branch_c, w_out, norm_mlp_g, w_mlp_up, w_mlp_down, final_norm_g, loss_target, m_norm_mix_g, m_w_in, m_b_gate, m_gmlp_ln_g, m_gmlp_ln_b, m_gmlp_w_s, m_gmlp_b_s, m_lru_conv_w, m_lru_conv_b, m_lru_w_r, m_lru_b_r, m_lru_w_i, m_lru_b_i, m_lru_lambda, m_ssd_conv_w, m_ssd_conv_b, m_ssd_dt_bias, m_ssd_a_log, m_ssd_d, m_ssd_norm_g, m_w_branch_a, m_w_branch_b, m_w_branch_c, m_w_out, m_norm_mlp_g, m_w_mlp_up, m_w_mlp_down, m_final_norm_g, v_norm_mix_g, v_w_in, v_b_gate, v_gmlp_ln_g, v_gmlp_ln_b, v_gmlp_w_s, v_gmlp_b_s, v_lru_conv_w, v_lru_conv_b, v_lru_w_r, v_lru_b_r, v_lru_w_i, v_lru_b_i, v_lru_lambda, v_ssd_conv_w, v_ssd_conv_b, v_ssd_dt_bias, v_ssd_a_log, v_ssd_d, v_ssd_norm_g, v_w_branch_a, v_w_branch_b, v_w_branch_c, v_w_out, v_norm_mlp_g, v_w_mlp_up, v_w_mlp_down, v_final_norm_g):
    args = locals()
    w = {n: args[n] for n in WEIGHTS}
    m = {n: args["m_" + n] for n in WEIGHTS}
    v = {n: args["v_" + n] for n in WEIGHTS}
    seq = x.shape[1]
    h = x.reshape(seq, D)
    target = loss_target.reshape(seq, D)

    def shard_on_wire(n, l):
        return (w[n][l].T if n in TRANSPOSED else w[n][l]).astype(BF16)

    first = _all_gather([shard_on_wire("w_in", 0)] + [w[n] for n in SMALL_SHARDED], "gather_weights_first")
    full = {n: w[n] for n in REPLICATED}
    for n, g in zip(SMALL_SHARDED, first[1:]):
        full[n] = jnp.stack([_lanes_from_devices(g[:, l]) for l in range(DEPTH)])

    def layer_weights(l, after):
        have = {"w_in": first[0]} if l == 0 else {}
        names = [n for n in BIG if n not in have]
        later = _all_gather([shard_on_wire(n, l) for n in names], f"gather_weights_l{l}", SEQ_GATHER + l, after=after)
        have.update(zip(names, later))
        wl = {n: have[n].reshape(-1, D) for n in BIG}
        wl["w_in"] = _reorder_in_proj(wl["w_in"])
        return wl

    loss_local, dh, layer_g, g_final = _loss_and_grads(h, target, full, layer_weights, first[0])
    grad_x = dh.reshape(x.shape)
    out = {}
    kinds = ("grad", "delta", "new_m", "new_v")

    sequencer_before = {}

    def reduce_scatter(slabs, tag, on_sequencer, later=()):
        keys = list(slabs)
        ids = (SEQ_TO_SIBLING, SEQ_TO_CHIPS) if on_sequencer else (None, None)
        from_sibling = _exchange_sibling([slabs[k] for k in keys], f"grads_to_sibling_{tag}", ids[0],
                                         sequencer_before.get("sibling", ()))
        from_sibling = dict(zip(keys, from_sibling))
        same_shape = {}
        for k in keys:
            same_shape.setdefault(slabs[k].shape[1:], []).append(k)
        chip_sums = {}
        for (r, c), ks in same_shape.items():
            sums = _add_sibling([slabs[k] for k in ks], [from_sibling[k] for k in ks], f"add_sibling_{tag}_{r}x{c}", later)
            chip_sums.update(zip(ks, sums))
        from_chips = _exchange_chips([chip_sums[k] for k in keys], f"grads_to_chips_{tag}", ids[1],
                                     sequencer_before.get("chips", ()))
        if on_sequencer:
            sequencer_before["sibling"], sequencer_before["chips"] = (from_sibling[keys[0]],), (from_chips[0],)
        from_chips = dict(zip(keys, from_chips))
        summed = {}
        for (r, c), ks in same_shape.items():
            sums = _sum_chips([slabs[k] for k in ks], [from_sibling[k] for k in ks], [from_chips[k] for k in ks],
                              f"sum_chips_{tag}_{r}x{c}")
            summed.update(zip(ks, sums))
        return summed

    small = {}
    for n in SMALL_SHARDED:
        small[n, None] = jnp.concatenate([_lanes_to_devices(layer_g[l][n]) for l in range(DEPTH)], axis=1)
    g_small = {n: jnp.concatenate([layer_g[l][n] for l in range(DEPTH)], axis=0) for n in REPLICATED[:-1]}
    g_small["final_norm_g"] = g_final
    for n in SMALL_MATRICES:
        small[n, None] = g_small[n].reshape(N_DEV, -1, g_small[n].shape[-1])
    reduced = {}
    groups = [(l, grp) for l in range(DEPTH - 1, -1, -1) for grp in GRADIENT_GROUPS]
    for l, grp in groups:
        if (l, grp) == groups[-1]:
            reduced.update(reduce_scatter(small, "small", True, (layer_g[l][GRADIENT_GROUPS[grp][0]],)))
        slabs = {(n, l): layer_g[l][n] for n in GRADIENT_GROUPS[grp]}
        reduced.update(reduce_scatter(slabs, f"{grp}_l{l}", True, (layer_g[l]["later"][grp],)))
    for n in BIG:
        grads = [reduced[n, l] for l in range(DEPTH)]
        if n in TRANSPOSED and w[n].shape[-1] % LANES:
            res = _adamw_transposed(grads, w[n], m[n], v[n], f"adamw_{n}")
        else:
            res = _adamw([(g.T if n in TRANSPOSED else g)[None] for g in grads], w[n], m[n], v[n], f"adamw_{n}")
        for kind, a in zip(kinds, res):
            out[kind, n] = a
    for n in SMALL_SHARDED:
        one = lambda a: a.reshape((1, -1, a.shape[-1]))
        for kind, a in zip(kinds, _adamw([reduced[n, None][None]], one(w[n]), one(m[n]), one(v[n]), f"adamw_{n}")):
            out[kind, n] = a.reshape(w[n].shape)

    to_gather = [reduced[n, None] if n in SMALL_MATRICES else g_small[n] for n in REPLICATED]
    *g_gathered, loss_terms = _all_gather(to_gather + [loss_local], "gather_small_grads", SEQ_GATHER_SMALL)
    g_all = dict(zip(REPLICATED, g_gathered))
    wv, mv, vv = _small_views(w), _small_views(m), _small_views(v)
    res, loss_sum = _adamw_small([g_all[n] for n in SMALL_VECTORS],
                                 *[[d[n] for n in SMALL_VECTORS] for d in (wv, mv, vv)], loss_terms, "adamw_vectors")
    loss = loss_sum[0, 0]
    for kind, arrays in zip(kinds, res):
        for n, a in zip(SMALL_VECTORS, arrays):
            out[kind, n] = a.reshape(w[n].shape)
    for n in SMALL_MATRICES:
        g_full = g_all[n].reshape((1, 1) + wv[n].shape)
        for kind, a in zip(kinds, _adamw([g_full[0]], wv[n][None], mv[n][None], vv[n][None], f"adamw_{n}")):
            out[kind, n] = a.reshape(w[n].shape)

    return (loss, grad_x, *[out[kind, n] for kind in kinds for n in WEIGHTS])
```

```python
import functools

import jax
import jax.numpy as jnp
from jax import lax
from jax.experimental import pallas as pl
from jax.experimental.pallas import tpu as pltpu
from jax.experimental.pallas import tpu_sc as plsc

F32 = jnp.float32
BF16 = jnp.bfloat16
MESH = pl.DeviceIdType.MESH

D = 1024
DEPTH = 2
EPS = 1e-6
CHUNK = 128
GROUPS_A = 8
HEADS_B = 8
LRU_C = 8.0
HEADS_C = 16
HEAD_DIM_C = 64
GROUPS_C = 4
STATE_C = 128
DT_PAD = 128
OFF_ZA, W_ZA = 0, 2048
OFF_ZB, W_ZB = 2048, 2048
OFF_XBC, W_XBC = 4096, 2048
OFF_GATE, W_GATE = 6144, 3072
OFF_ZC, W_ZC = 9216, 1024
OFF_DT, W_DT = 10240, DT_PAD
D_IN_PAD = 10368
N_DEV = 8
SEQ_GATHER = 1
SEQ_TO_SIBLING = SEQ_GATHER + DEPTH
SEQ_TO_CHIPS = SEQ_TO_SIBLING + 1
SEQ_GATHER_SMALL = SEQ_TO_CHIPS + 1

ADAM_LR = 0.001
ADAM_B1 = 0.9
ADAM_B2 = 0.999
ADAM_EPS = 1e-08
ADAM_WD = 0.01
ADAM_STEP = 10

VMEM_LIMIT = 56 * 1024 * 1024
HALO = 8


def _cparams(*sem):
    return pltpu.CompilerParams(dimension_semantics=sem, vmem_limit_bytes=VMEM_LIMIT)


def _bf(x):
    return x.astype(BF16)


def _dg(a, b, ca, cb):
    return lax.dot_general(a, b, (((ca,), (cb,)), ((), ())), preferred_element_type=F32)


@functools.partial(jax.custom_vjp, nondiff_argnums=(2, 3))
def _mm(a, b, ta, tb):
    return _dg(_bf(a), _bf(b), 0 if ta else 1, 1 if tb else 0)


def _mm_fwd(a, b, ta, tb):
    return _mm(a, b, ta, tb), (a, b)


def _mm_bwd(ta, tb, res, g):
    a, b = res
    ma = 1 if ta else 0
    nb = 0 if tb else 1
    gb, ab, bb = _bf(g), _bf(a), _bf(b)
    da = _dg(bb, gb, nb, 1) if ta else _dg(gb, bb, 1, nb)
    db = _dg(gb, ab, 0, ma) if tb else _dg(ab, gb, ma, 0)
    return da.astype(a.dtype), db.astype(b.dtype)


_mm.defvjp(_mm_fwd, _mm_bwd)


def _slices(x, sizes, axis):
    out, lo = [], 0
    for size in sizes:
        out.append(lax.slice_in_dim(x, lo, lo + size, axis=axis))
        lo += size
    return tuple(out)


@functools.partial(jax.custom_vjp, nondiff_argnums=(1,))
def _split_cols(x, widths):
    return _slices(x, widths, 1)


_split_cols.defvjp(lambda x, widths: (_slices(x, widths, 1), None),
                   lambda widths, _, gs: (jnp.concatenate(gs, axis=1),))


@functools.partial(jax.custom_vjp, nondiff_argnums=(1,))
def _split_rows(x, heights):
    return _slices(x, heights, 0)


_split_rows.defvjp(lambda x, heights: (_slices(x, heights, 0), None),
                   lambda heights, _, gs: (jnp.concatenate(gs, axis=0),))


def _col(x, j):
    lane = lax.broadcasted_iota(jnp.int32, x.shape, 1)
    return jnp.sum(jnp.where(lane == j, x, 0.0), axis=1, keepdims=True)


def _row(x, i):
    r = lax.broadcasted_iota(jnp.int32, x.shape, 0)
    return jnp.sum(jnp.where(r == i, x, 0.0), axis=0, keepdims=True)


def _roll_down(x, s):
    return pltpu.roll(x, s, 0)


def _roll_up(x, s):
    return pltpu.roll(x, x.shape[0] - s, 0)


def _row_iota(x):
    return lax.broadcasted_iota(jnp.int32, x.shape, 0)


@functools.partial(jax.custom_vjp, nondiff_argnums=(2,))
def _shift_rows(halo, x, s):
    if s == 0:
        return x
    return _roll_down(jnp.concatenate([halo, x], axis=0), s)[HALO:]


def _shift_rows_fwd(halo, x, s):
    return _shift_rows(halo, x, s), None


def _shift_rows_bwd(s, _, g):
    if s == 0:
        return jnp.zeros((HALO, g.shape[1]), g.dtype), g
    ge = jnp.concatenate([jnp.zeros((HALO, g.shape[1]), g.dtype), g], axis=0)
    de = _roll_up(ge, s)
    return de[:HALO], de[HALO:]


_shift_rows.defvjp(_shift_rows_fwd, _shift_rows_bwd)


SUBLANES = 8
LANES = 128


def _scan_tiles(a, b, carry, up):
    n, c = a.shape
    nt = n // SUBLANES
    a = a.reshape(nt, SUBLANES, c)
    b = b.reshape(nt, SUBLANES, c)
    sub = lax.broadcasted_iota(jnp.int32, a.shape, 1)
    s = 1
    while s < SUBLANES:
        keep = (sub < SUBLANES - s) if up else (sub >= s)
        shift = SUBLANES - s if up else s
        a_sh = jnp.where(keep, pltpu.roll(a, shift, 1), 1.0)
        b_sh = jnp.where(keep, pltpu.roll(b, shift, 1), 0.0)
        b = a * b_sh + b
        a = a * a_sh
        s *= 2
    tiles = [None] * nt
    edge = 0 if up else SUBLANES - 1
    for j in (range(nt - 1, -1, -1) if up else range(nt)):
        tiles[j] = b[j] if carry is None else b[j] + a[j] * carry
        carry = tiles[j][edge:edge + 1, :]
    return jnp.concatenate(tiles, axis=0)


@jax.custom_vjp
def _lin_scan(a, b, h0):
    return _scan_tiles(a, b, h0, up=False)


def _lin_scan_fwd(a, b, h0):
    h = _lin_scan(a, b, h0)
    return h, (a, h0, h)


def _lin_scan_bwd(res, g):
    a, h0, h = res
    n = a.shape[0]
    row = _row_iota(a)
    a_next = jnp.where(row < n - 1, _roll_up(a, 1), 0.0)
    gg = _scan_tiles(a_next, g, None, up=True)
    h_prev = jnp.where(row >= 1, _roll_down(h, 1), h0)
    return gg * h_prev, gg, _row(a * gg, 0)


_lin_scan.defvjp(_lin_scan_fwd, _lin_scan_bwd)


@jax.custom_vjp
def _cumsum_rows(x):
    n = x.shape[0]
    row = _row_iota(x)
    s = 1
    while s < n:
        x = x + jnp.where(row >= s, _roll_down(x, s), 0.0)
        s *= 2
    return x


def _cumsum_rows_fwd(x):
    return _cumsum_rows(x), None


def _cumsum_rows_bwd(_, g):
    n = g.shape[0]
    row = _row_iota(g)
    s = 1
    while s < n:
        g = g + jnp.where(row < n - s, _roll_up(g, s), 0.0)
        s *= 2
    return (g,)


_cumsum_rows.defvjp(_cumsum_rows_fwd, _cumsum_rows_bwd)


def _sigmoid(x):
    return jax.nn.sigmoid(x)


def _softplus(x):
    return jnp.maximum(x, 0.0) + jnp.log1p(jnp.exp(-jnp.abs(x)))


def _gelu(x):
    return jax.nn.gelu(x, approximate=True)


def _neg_expm1(x):
    series = -x * (1.0 + x * (0.5 + x * (1.0 / 6.0 + x * (1.0 / 24.0))))
    return jnp.where(x > -0.01, series, 1.0 - jnp.exp(x))


def _rms(x, g):
    return x * lax.rsqrt(jnp.mean(x * x, axis=-1, keepdims=True) + EPS) * g


def _f_rmsnorm(carries, halos, xs, params):
    (h,) = xs
    (g,) = params
    return (), (_rms(h, g),)


def _f_rmsnorm_res(carries, halos, xs, params):
    (h,) = xs
    (g,) = params
    return (), (_rms(h, g), h)


def _f_gmlp(carries, halos, xs, params):
    (za,) = xs
    ln_g, ln_b, w_s, b_st = params
    u, v = _split_cols(_gelu(za), (D, D))
    vc = v - jnp.mean(v, axis=-1, keepdims=True)
    vn = vc * lax.rsqrt(jnp.mean(vc * vc, axis=-1, keepdims=True) + EPS) * ln_g + ln_b
    q = CHUNK
    causal = lax.broadcasted_iota(jnp.int32, (q, q), 0) >= lax.broadcasted_iota(jnp.int32, (q, q), 1)
    mixed = []
    for g, (w, vg) in enumerate(zip(_split_rows(w_s, (q,) * GROUPS_A), _split_cols(vn, (q,) * GROUPS_A))):
        mixed.append(_mm(jnp.where(causal, w, 0.0), vg, False, False) + _col(b_st, g))
    return (), (u * jnp.concatenate(mixed, axis=1),)


def _conv4(halo, x, w, b):
    y = b + _row(w, 3) * x
    for k in range(3):
        y = y + _row(w, k) * _shift_rows(halo, x, 3 - k)
    return y


def _f_lru(carries, halos, xs, params):
    (h0,) = carries
    (halo,) = halos
    xb_pre, gate = xs
    conv_w, conv_b, w_r, b_r, w_i, b_i, lam = params
    xb = _conv4(halo, xb_pre, conv_w, conv_b)
    hd = D // HEADS_B
    r_parts, i_parts = [], []
    heads = (hd,) * HEADS_B
    for xh, wr, wi in zip(_split_cols(xb, heads), _split_rows(w_r, heads), _split_rows(w_i, heads)):
        r_parts.append(_mm(xh, wr, False, False))
        i_parts.append(_mm(xh, wi, False, False))
    r = _sigmoid(jnp.concatenate(r_parts, axis=1) + b_r)
    i = _sigmoid(jnp.concatenate(i_parts, axis=1) + b_i)
    log_a = -LRU_C * r * _softplus(-lam)
    a = jnp.exp(log_a)
    inp = jnp.sqrt(_neg_expm1(2.0 * log_a)) * (i * xb)
    h = _lin_scan(a, inp, h0)
    return (_row(h, h.shape[0] - 1),), (_gelu(gate) * h,)


def _f_ssd(carries, halos, xs, params):
    (st,) = carries
    (halo,) = halos
    z, xbc_pre, dt_raw = xs
    conv_w, conv_b, dt_bias, a_log, d_skip, norm_g = params
    t = z.shape[0]
    xc = _conv4(halo, xbc_pre, conv_w, conv_b)
    xbc = xc * _sigmoid(xc)
    x_all, b_all, c_all = _split_cols(xbc, (D, GROUPS_C * STATE_C, GROUPS_C * STATE_C))
    x_pairs = _split_cols(x_all, (128,) * (HEADS_C // 2))
    b_groups = _split_cols(b_all, (STATE_C,) * GROUPS_C)
    c_groups = _split_cols(c_all, (STATE_C,) * GROUPS_C)
    st_pairs = _split_rows(st, (128,) * (HEADS_C // 2))
    dt = _softplus(dt_raw + dt_bias)
    adt = dt * (-jnp.exp(a_log))
    acs = _cumsum_rows(adt)
    acs_t = acs.T
    a_last = _row(acs, t - 1)
    lo = lax.broadcasted_iota(jnp.int32, (t, 128), 1) < HEAD_DIM_C
    lo_rows = lax.broadcasted_iota(jnp.int32, (128, STATE_C), 0) < HEAD_DIM_C
    causal = lax.broadcasted_iota(jnp.int32, (t, t), 0) >= lax.broadcasted_iota(jnp.int32, (t, t), 1)
    y_parts, st_parts = [], []
    for g in range(GROUPS_C):
        bg, cg = b_groups[g], c_groups[g]
        cb = _mm(cg, bg, False, True)
        for pr in range(2):
            pair = 2 * g + pr
            h0, h1 = 2 * pair, 2 * pair + 1
            x2 = x_pairs[pair]
            ac0, ac1 = _col(acs, h0), _col(acs, h1)
            l0 = jnp.exp(jnp.where(causal, ac0 - _row(acs_t, h0), -1e30))
            l1 = jnp.exp(jnp.where(causal, ac1 - _row(acs_t, h1), -1e30))
            xdt = x2 * jnp.where(lo, _col(dt, h0), _col(dt, h1))
            y_diag = (_mm(cb * l0, jnp.where(lo, xdt, 0.0), False, False)
                      + _mm(cb * l1, jnp.where(lo, 0.0, xdt), False, False))
            al0, al1 = _col(a_last, h0), _col(a_last, h1)
            decay_s = jnp.where(lo, jnp.exp(al0 - ac0), jnp.exp(al1 - ac1))
            s_new = _mm(xdt * decay_s, bg, True, False)
            prev = st_pairs[pair]
            y_off = _mm(cg, prev, False, True) * jnp.where(lo, jnp.exp(ac0), jnp.exp(ac1))
            skip = jnp.where(lo, _col(d_skip, h0), _col(d_skip, h1))
            y_parts.append(y_diag + y_off + x2 * skip)
            st_parts.append(prev * jnp.where(lo_rows, jnp.exp(al0), jnp.exp(al1)) + s_new)
    y = jnp.concatenate(y_parts, axis=1) * (z * _sigmoid(z))
    gw = D // GROUPS_C
    yn = []
    for yg in _split_cols(y, (gw,) * GROUPS_C):
        yn.append(yg * lax.rsqrt(jnp.mean(yg * yg, axis=-1, keepdims=True) + EPS))
    return (jnp.concatenate(st_parts, axis=0),), (jnp.concatenate(yn, axis=1) * norm_g,)


def _f_merge(carries, halos, xs, params):
    pa, pb, pc, g_raw = xs
    (b_gate,) = params
    ga, gb, gc = _split_cols(g_raw, (D, D, D))
    m = (_sigmoid(ga + _row(b_gate, 0)) * pa + _sigmoid(gb + _row(b_gate, 1)) * pb
         + _sigmoid(gc + _row(b_gate, 2)) * pc)
    return (), (m,)


def _f_loss(carries, halos, xs, params):
    (acc,) = carries
    h, target = xs
    (g,) = params
    err = jnp.square(_rms(h, g) - target)
    part = 0.5 * jnp.sum(jnp.mean(err, axis=-1, keepdims=True), axis=0, keepdims=True)
    return (acc + part,), ()


def _x_specs(xs, t, index_of):
    specs = []
    for arr, off, width in xs:
        assert off % width == 0 and off + width <= arr.shape[1]
        specs.append(pl.BlockSpec((t, width), functools.partial(lambda j, cb: (index_of(j), cb), cb=off // width)))
    return specs


def _halo_specs(xs, halo_idx, t, index_of):
    specs = []
    for xi in halo_idx:
        _, off, width = xs[xi]
        specs.append(pl.BlockSpec(
            (HALO, width),
            functools.partial(lambda j, cb: (jnp.maximum(index_of(j) * (t // HALO) - 1, 0), cb), cb=off // width)))
    return specs


def _full_spec(a):
    return pl.BlockSpec(a.shape, functools.partial(lambda j, nd: (0,) * nd, nd=a.ndim))


def _chunk_fwd(f, name, t, xs, params, outs, halo_idx=(), carry_shapes=(), save_carries=False, final_carries=False):
    s = xs[0][0].shape[0]
    n = s // t
    nx, nh, npar, no, nc = len(xs), len(halo_idx), len(params), len(outs), len(carry_shapes)
    ns = nc if save_carries else 0
    nf = nc if final_carries else 0

    def body(*refs):
        x_refs, refs = refs[:nx], refs[nx:]
        h_refs, refs = refs[:nh], refs[nh:]
        p_refs, refs = refs[:npar], refs[npar:]
        y_refs, refs = refs[:no], refs[no:]
        s_refs, refs = refs[:ns], refs[ns:]
        f_refs, c_refs = refs[:nf], refs[nf:]
        i = pl.program_id(0)

        @pl.when(i == 0)
        def _():
            for c in c_refs:
                c[...] = jnp.zeros_like(c)

        carries = tuple(c[...] for c in c_refs)
        for s_ref, c in zip(s_refs, carries):
            s_ref[0] = c
        halos = tuple(jnp.where(i > 0, h[...].astype(F32), 0.0) for h in h_refs)
        new_c, ys = f(carries, halos, tuple(x[...].astype(F32) for x in x_refs), tuple(p[...] for p in p_refs))
        for y_ref, y in zip(y_refs, ys):
            y_ref[...] = y.astype(y_ref.dtype)
        for c, v in zip(c_refs, new_c):
            c[...] = v
        for f_ref, v in zip(f_refs, new_c):
            f_ref[...] = v

    ident = lambda j: j
    out_shape = [jax.ShapeDtypeStruct((s, w), dt) for w, dt in outs]
    out_specs = [pl.BlockSpec((t, w), lambda j: (j, 0)) for w, _ in outs]
    if save_carries:
        out_shape += [jax.ShapeDtypeStruct((n,) + tuple(cs), F32) for cs in carry_shapes]
        out_specs += [pl.BlockSpec((1,) + tuple(cs), lambda j: (j, 0, 0)) for cs in carry_shapes]
    if final_carries:
        out_shape += [jax.ShapeDtypeStruct(tuple(cs), F32) for cs in carry_shapes]
        out_specs += [pl.BlockSpec(tuple(cs), lambda j: (0, 0)) for cs in carry_shapes]
    res = pl.pallas_call(
        body, name=name, grid=(n,),
        in_specs=_x_specs(xs, t, ident) + _halo_specs(xs, halo_idx, t, ident) + [_full_spec(p) for p in params],
        out_specs=out_specs, out_shape=out_shape,
        scratch_shapes=[pltpu.VMEM(tuple(cs), F32) for cs in carry_shapes],
        compiler_params=_cparams("arbitrary"),
    )(*[x[0] for x in xs], *[xs[xi][0] for xi in halo_idx], *params)
    return res[:no], res[no:no + ns], res[no + ns:]


def _chunk_bwd(f, name, t, xs, params, dys, dx_dtypes, halo_idx=(), saved=(), carry_seed=None):
    s = xs[0][0].shape[0]
    n = s // t
    nx, nh, npar, nc, ndy = len(xs), len(halo_idx), len(params), len(saved), len(dys)

    def body(*refs):
        x_refs, refs = refs[:nx], refs[nx:]
        h_refs, refs = refs[:nh], refs[nh:]
        p_refs, refs = refs[:npar], refs[npar:]
        s_refs, refs = refs[:nc], refs[nc:]
        dy_refs, refs = refs[:ndy], refs[ndy:]
        dx_refs, refs = refs[:nx], refs[nx:]
        dp_refs, refs = refs[:npar], refs[npar:]
        dc_refs, dh_refs = refs[:nc], refs[nc:]
        j = pl.program_id(0)
        i = n - 1 - j

        @pl.when(j == 0)
        def _():
            for dc in dc_refs:
                dc[...] = jnp.zeros_like(dc) if carry_seed is None else carry_seed(dc.shape)
            for r in dh_refs + dp_refs:
                r[...] = jnp.zeros_like(r)

        carries = tuple(s_ref[0] for s_ref in s_refs)
        halos = tuple(jnp.where(i > 0, h[...].astype(F32), 0.0) for h in h_refs)
        x_vals = tuple(x[...].astype(F32) for x in x_refs)
        p_vals = tuple(p[...] for p in p_refs)
        _, vjp = jax.vjp(f, carries, halos, x_vals, p_vals)
        d_car, d_hal, d_xs, d_par = vjp((tuple(dc[...] for dc in dc_refs), tuple(d[...].astype(F32) for d in dy_refs)))
        d_xs = list(d_xs)
        for k, xi in enumerate(halo_idx):
            w = xs[xi][2]
            d_xs[xi] = d_xs[xi] + jnp.concatenate([jnp.zeros((t - HALO, w), F32), dh_refs[k][...]], axis=0)
            dh_refs[k][...] = jnp.where(i > 0, d_hal[k], 0.0)
        for dx_ref, dx in zip(dx_refs, d_xs):
            dx_ref[...] = dx.astype(dx_ref.dtype)
        for dp_ref, dp in zip(dp_refs, d_par):
            dp_ref[...] += dp
        for dc, v in zip(dc_refs, d_car):
            dc[...] = v

    rev = lambda j: n - 1 - j
    in_specs = (_x_specs(xs, t, rev) + _halo_specs(xs, halo_idx, t, rev) + [_full_spec(p) for p in params]
                + [pl.BlockSpec((1,) + a.shape[1:], lambda j: (n - 1 - j, 0, 0)) for a in saved]
                + [pl.BlockSpec((t, d.shape[1]), lambda j: (n - 1 - j, 0)) for d in dys])
    out_shape = ([jax.ShapeDtypeStruct((s, w), dt) for (_, _, w), dt in zip(xs, dx_dtypes)]
                 + [jax.ShapeDtypeStruct(p.shape, F32) for p in params])
    out_specs = ([pl.BlockSpec((t, w), lambda j: (n - 1 - j, 0)) for _, _, w in xs] + [_full_spec(p) for p in params])
    res = pl.pallas_call(
        body, name=name, grid=(n,), in_specs=in_specs, out_specs=out_specs, out_shape=out_shape,
        scratch_shapes=([pltpu.VMEM(a.shape[1:], F32) for a in saved]
                        + [pltpu.VMEM((HALO, xs[xi][2]), F32) for xi in halo_idx]),
        compiler_params=_cparams("arbitrary"),
    )(*[x[0] for x in xs], *[xs[xi][0] for xi in halo_idx], *params, *saved, *dys)
    return res[:nx], res[nx:]


def _tile(dim, pref):
    for cand in pref:
        if dim % cand == 0:
            return cand
    return dim


def _write_outputs(vals, o_refs, sum_refs, first_rows):
    for o_ref, v in zip(o_refs, vals):
        o_ref[...] = v.astype(o_ref.dtype)
    for s_ref, v in zip(sum_refs, vals[len(o_refs):]):
        @pl.when(first_rows)
        def _(s_ref=s_ref, v=v):
            s_ref[...] = v

        @pl.when(jnp.logical_not(first_rows))
        def _(s_ref=s_ref, v=v):
            s_ref[...] += v


def _matmul(a, b, name, ta=False, tb=False, outs=(F32,), epilogue=None, extras=(), row_params=(), row_sums=0,
            max_tm=1024):
    m, k = (a.shape[1], a.shape[0]) if ta else a.shape
    n = b.shape[0] if tb else b.shape[1]
    tm = _tile(m, tuple(t for t in (1024, 1152, 512, 256, 128) if t <= max_tm))
    tn = _tile(n, (1152, 1024, 512, 256, 128))
    tk = _tile(k, (1024, 1152, 512, 256, 128))
    nk = k // tk
    assert not row_sums or tn == n
    ne, no = len(extras) + len(row_params), len(outs)
    ca, cb = (0 if ta else 1), (1 if tb else 0)

    def body(*refs):
        a_ref, b_ref = refs[:2]
        e_refs = refs[2:2 + ne]
        o_refs = refs[2 + ne:2 + ne + no]
        sum_refs = refs[2 + ne + no:2 + ne + no + row_sums]
        acc = refs[-1]
        kk = pl.program_id(2)
        first_rows = pl.program_id(0) == 0

        @pl.when(kk == 0)
        def _():
            acc[...] = jnp.zeros_like(acc)

        acc[...] += _dg(_bf(a_ref[...]), _bf(b_ref[...]), ca, cb)

        @pl.when(kk == nk - 1)
        def _():
            res = acc[...]
            vals = (res,) if epilogue is None else epilogue(res, *[e[...] for e in e_refs])
            _write_outputs(vals, o_refs, sum_refs, first_rows)

    a_spec = pl.BlockSpec((tk, tm), lambda i, j, kk: (kk, i)) if ta else pl.BlockSpec((tm, tk), lambda i, j, kk: (i, kk))
    b_spec = pl.BlockSpec((tn, tk), lambda i, j, kk: (j, kk)) if tb else pl.BlockSpec((tk, tn), lambda i, j, kk: (kk, j))
    mn_spec = pl.BlockSpec((tm, tn), lambda i, j, kk: (i, j))
    row_spec = pl.BlockSpec((1, tn), lambda i, j, kk: (0, j))
    res = pl.pallas_call(
        body, name=name, grid=(m // tm, n // tn, nk),
        in_specs=[a_spec, b_spec] + [mn_spec] * len(extras) + [row_spec] * len(row_params),
        out_specs=[mn_spec] * no + [row_spec] * row_sums,
        out_shape=[jax.ShapeDtypeStruct((m, n), dt) for dt in outs] + [jax.ShapeDtypeStruct((1, n), F32)] * row_sums,
        scratch_shapes=[pltpu.VMEM((tm, tn), F32)],
        compiler_params=_cparams("arbitrary" if row_sums else "parallel", "parallel", "arbitrary"),
    )(a, b, *extras, *row_params)
    return res if len(res) > 1 else res[0]


def _elementwise_block(r, c):
    if r % 8 == 0 and r >= 8:
        return _tile(r, (256, 128, 64, 32, 16, 8)), c
    return r, _tile(c, (256, 128))


PIECE_TILE = 1024
EPILOGUE_ROWS = 512


def _piece_steps(pieces):
    steps, s0 = [], 0
    for a in pieces:
        wt = min(a.shape[1], PIECE_TILE)
        assert a.shape[1] % wt == 0
        steps.append((s0, a.shape[1] // wt, wt))
        s0 += a.shape[1] // wt
    return steps, s0


def _matmul_pieces(pieces, b, name, epilogue=None, extras=(), row_params=(), row_sums=0, max_tm=1024):
    steps, n_steps = _piece_steps(pieces)
    s_rows, n = pieces[0].shape[0], b.shape[1]
    tm = _tile(s_rows, tuple(t for t in (1024, 512, 256, 128) if t <= max_tm))
    tail_rows = steps[-1][2]
    full_steps = n_steps - 1 if tail_rows < PIECE_TILE else n_steps
    b_tail = b[full_steps * PIECE_TILE:]
    np_ = len(pieces)
    ne = len(extras) + len(row_params)

    def body(*refs):
        a_refs, b_ref, tail_ref = refs[:np_], refs[np_], refs[np_ + 1]
        e_refs = refs[np_ + 2:np_ + 2 + ne]
        o_ref, sum_refs, acc = refs[np_ + 2 + ne], refs[np_ + 3 + ne:np_ + 3 + ne + row_sums], refs[-1]
        s = pl.program_id(1)
        first_rows = pl.program_id(0) == 0

        @pl.when(s == 0)
        def _():
            acc[...] = jnp.zeros_like(acc)

        for a_ref, (s0, ns, wt) in zip(a_refs, steps):
            @pl.when((s >= s0) & (s < s0 + ns))
            def _(a_ref=a_ref, wt=wt):
                rhs = b_ref[...] if wt == PIECE_TILE else tail_ref[...]
                acc[...] += _dg(_bf(a_ref[...]), _bf(rhs), 1, 0)

        @pl.when(s == n_steps - 1)
        def _():
            res = acc[...]
            vals = (res,) if epilogue is None else epilogue(res, *[e[...] for e in e_refs])
            _write_outputs(vals, (o_ref,), sum_refs, first_rows)

    a_specs = [pl.BlockSpec((tm, wt), functools.partial(lambda i, s, s0, ns: (i, jnp.clip(s - s0, 0, ns - 1)), s0=s0, ns=ns))
               for s0, ns, wt in steps]
    mn_spec = pl.BlockSpec((tm, n), lambda i, s: (i, 0))
    row_spec = pl.BlockSpec((1, n), lambda i, s: (0, 0))
    res = pl.pallas_call(
        body, name=name, grid=(s_rows // tm, n_steps),
        in_specs=a_specs + [pl.BlockSpec((PIECE_TILE, n), lambda i, s: (jnp.minimum(s, full_steps - 1), 0)),
                            pl.BlockSpec(b_tail.shape, lambda i, s: (0, 0))]
        + [mn_spec] * len(extras) + [row_spec] * len(row_params),
        out_specs=[mn_spec] + [row_spec] * row_sums,
        out_shape=[jax.ShapeDtypeStruct((s_rows, n), F32)] + [jax.ShapeDtypeStruct((1, n), F32)] * row_sums,
        scratch_shapes=[pltpu.VMEM((tm, n), F32)],
        compiler_params=_cparams("arbitrary" if row_sums else "parallel", "arbitrary"),
    )(*pieces, b, b_tail, *extras, *row_params)
    return res if len(res) > 1 else res[0]


def _matmul_pieces_t(pieces, b, name):
    steps, n_steps = _piece_steps(pieces)
    s_rows, n = b.shape
    tk = _tile(s_rows, (1024, 512, 256, 128))
    nk = s_rows // tk
    total = sum(a.shape[1] for a in pieces)
    np_ = len(pieces)

    def body(*refs):
        a_refs, b_ref, o_ref, acc = refs[:np_], refs[np_], refs[np_ + 1], refs[np_ + 2]
        s, kk = pl.program_id(0), pl.program_id(1)

        @pl.when(kk == 0)
        def _():
            acc[...] = jnp.zeros_like(acc)

        for a_ref, (s0, ns, wt) in zip(a_refs, steps):
            @pl.when((s >= s0) & (s < s0 + ns))
            def _(a_ref=a_ref, wt=wt):
                acc[0:wt, :] += _dg(_bf(a_ref[...]), _bf(b_ref[...]), 0, 0)

        @pl.when(kk == nk - 1)
        def _():
            o_ref[...] = acc[...]

    def a_index(s, kk, s0, ns):
        active = (s >= s0) & (s < s0 + ns)
        return jnp.where(active, kk, jnp.where(s < s0, 0, nk - 1)), jnp.clip(s - s0, 0, ns - 1)

    a_specs = [pl.BlockSpec((tk, wt), functools.partial(a_index, s0=s0, ns=ns)) for s0, ns, wt in steps]
    return pl.pallas_call(
        body, name=name, grid=(n_steps, nk),
        in_specs=a_specs + [pl.BlockSpec((tk, n), lambda s, kk: (kk, 0))],
        out_specs=pl.BlockSpec((PIECE_TILE, n), lambda s, kk: (s, 0)), out_shape=jax.ShapeDtypeStruct((total, n), F32),
        scratch_shapes=[pltpu.VMEM((PIECE_TILE, n), F32)],
        compiler_params=_cparams("parallel", "arbitrary"),
    )(*pieces, b)


def _adamw_math(g, w, m, v):
    m_new = ADAM_B1 * m + (1.0 - ADAM_B1) * g
    v_new = ADAM_B2 * v + (1.0 - ADAM_B2) * jnp.square(g)
    m_hat = m_new / (1.0 - ADAM_B1 ** ADAM_STEP)
    v_hat = v_new / (1.0 - ADAM_B2 ** ADAM_STEP)
    return -ADAM_LR * (m_hat / (jnp.sqrt(v_hat) + ADAM_EPS) + ADAM_WD * w), m_new, v_new


def _adamw(parts, w, m, v, name):
    nl, r, c = w.shape
    k = parts[0].shape[0]
    tr = _tile(r, (128, 64, 32, 16, 8))
    nb = r // tr

    def body(*refs):
        p_refs, (w_ref, m_ref, v_ref), outs = refs[:nl], refs[nl:nl + 3], refs[nl + 3:]
        layer = pl.program_id(0)
        for q in range(nl):
            @pl.when(layer == q)
            def _(q=q):
                g = p_refs[q][0]
                for j in range(1, k):
                    g = g + p_refs[q][j]
                vals = (g,) + _adamw_math(g, w_ref[0], m_ref[0], v_ref[0])
                for o_ref, val in zip(outs, vals):
                    o_ref[0] = val

    spec = pl.BlockSpec((1, tr, c), lambda l, i: (l, i, 0))
    part_specs = [pl.BlockSpec((k, tr, c), functools.partial(
        lambda l, i, q: (0, jnp.where(l == q, i, jnp.where(l < q, 0, nb - 1)), 0), q=q)) for q in range(nl)]
    return pl.pallas_call(
        body, name=name, grid=(nl, nb), in_specs=part_specs + [spec] * 3,
        out_specs=[spec] * 4, out_shape=[jax.ShapeDtypeStruct((nl, r, c), F32)] * 4,
        compiler_params=_cparams("arbitrary", "arbitrary"),
    )(*parts, w, m, v)


def _adamw_transposed(grads, w, m, v, name):
    nl, r, c = w.shape
    tc = 64
    views = [jnp.transpose(a, (2, 0, 1)) for a in (w, m, v)]

    def body(*refs):
        g_refs, (w_ref, m_ref, v_ref), outs = refs[:nl], refs[nl:nl + 3], refs[nl + 3:]
        for l in range(nl):
            g = g_refs[l][...]
            vals = (g,) + _adamw_math(g, w_ref[:, l, :], m_ref[:, l, :], v_ref[:, l, :])
            for o_ref, val in zip(outs, vals):
                o_ref[:, l, :] = val

    spec = pl.BlockSpec((tc, nl, r), lambda i: (i, 0, 0))
    res = pl.pallas_call(
        body, name=name, grid=(pl.cdiv(c, tc),), in_specs=[pl.BlockSpec((tc, r), lambda i: (i, 0))] * nl + [spec] * 3,
        out_specs=[spec] * 4, out_shape=[jax.ShapeDtypeStruct((c, nl, r), F32)] * 4,
        compiler_params=_cparams("parallel"),
    )(*grads, *views)
    return [jnp.transpose(a, (1, 2, 0)) for a in res]


def _adamw_small(gathered, ws, ms, vs, loss_terms, name):
    n = len(ws)

    def device_sum(ref):
        s = ref[0]
        for j in range(1, N_DEV):
            s = s + ref[j]
        return s

    def body(*refs):
        g_refs, w_refs, m_refs, v_refs = refs[:n], refs[n:2 * n], refs[2 * n:3 * n], refs[3 * n:4 * n]
        loss_ref, outs, loss_out = refs[4 * n], refs[4 * n + 1:-1], refs[-1]
        for i in range(n):
            g = device_sum(g_refs[i])
            vals = (g,) + _adamw_math(g, w_refs[i][...], m_refs[i][...], v_refs[i][...])
            for kind, val in enumerate(vals):
                outs[kind * n + i][...] = val
        loss_out[...] = device_sum(loss_ref)

    res = pl.pallas_call(
        body, name=name,
        out_shape=[jax.ShapeDtypeStruct(a.shape, F32) for _ in range(4) for a in ws] + [
            jax.ShapeDtypeStruct(loss_terms.shape[1:], F32)],
        compiler_params=pltpu.CompilerParams(vmem_limit_bytes=VMEM_LIMIT),
    )(*gathered, *ws, *ms, *vs, loss_terms)
    return [res[kind * n:(kind + 1) * n] for kind in range(4)], res[-1]


ANY = pl.BlockSpec(memory_space=pl.ANY)


def _place():
    return lax.axis_index("x"), lax.axis_index("y"), lax.axis_index("c")


def _handshake(peers):
    barrier = pltpu.get_barrier_semaphore()
    for peer in peers:
        pl.semaphore_signal(barrier, inc=1, device_id=peer, device_id_type=MESH)
    pl.semaphore_wait(barrier, len(peers))


def _comm_call(body, name, inputs, out_shape, scratch, sequencer_id=None, after=()):
    if sequencer_id is None:
        return pl.pallas_call(body, name=name, out_shape=out_shape, in_specs=[ANY] * len(inputs),
                              out_specs=[ANY] * len(out_shape), scratch_shapes=scratch)(*inputs)
    n_in, n_after = len(inputs), len(after)

    def sequencer_body(*refs):
        body(*refs[:n_in], *refs[n_in + n_after:])

    return pl.kernel(
        sequencer_body, out_type=out_shape, mesh=plsc.ScalarSubcoreMesh(axis_name="sequencer", num_cores=1),
        scratch_types=scratch, compiler_params=pltpu.CompilerParams(collective_id=sequencer_id), name=name,
    )(*inputs, *after)


def _all_gather(blocks, name, sequencer_id=None, after=()):
    n = len(blocks)

    def body(*refs):
        x_refs, out_refs = refs[:n], refs[n:2 * n]
        send_sems, recv_sems, local_sems = refs[2 * n:]
        x, y, c = _place()
        me, sibling = (x, y, c), (x, y, 1 - c)
        chips = [(1 - x, y), (x, 1 - y), (1 - x, 1 - y)]
        if sequencer_id is not None:
            _handshake([sibling] + [(*chip, c) for chip in chips])

        def slot(a, px, py, pc):
            return out_refs[a].at[4 * px + 2 * py + pc]

        def copy(a, k, blk, to, src=None):
            return pltpu.make_async_remote_copy(
                src_ref=slot(a, *blk) if src is None else src, dst_ref=slot(a, *blk),
                send_sem=send_sems.at[7 * a + k], recv_sem=recv_sems.at[7 * a + k], device_id=to, device_id_type=MESH)

        mine = [pltpu.make_async_copy(x_refs[a], slot(a, *me), local_sems.at[a]) for a in range(n)]
        first = []
        for a in range(n):
            mine[a].start()
            first.append(copy(a, 0, me, sibling, src=x_refs[a]))
            first += [copy(a, 1 + j, me, (*chip, c), src=x_refs[a]) for j, chip in enumerate(chips)]
        for cp in first:
            cp.start()
        passed = []
        for j, chip in enumerate(chips):
            for a in range(n):
                copy(a, 1 + j, (*chip, c), me).wait_recv()
                passed.append(copy(a, 4 + j, (*chip, c), sibling))
                passed[-1].start()
        for a in range(n):
            copy(a, 0, sibling, me).wait_recv()
            for j, chip in enumerate(chips):
                copy(a, 4 + j, (*chip, 1 - c), me).wait_recv()
        for cp in first + passed:
            cp.wait_send()
        for cp in mine:
            cp.wait()

    return _comm_call(
        body, name, blocks, [jax.ShapeDtypeStruct((N_DEV,) + b.shape, b.dtype) for b in blocks],
        [pltpu.SemaphoreType.DMA((7 * n,)), pltpu.SemaphoreType.DMA((7 * n,)), pltpu.SemaphoreType.DMA((n,))],
        sequencer_id, after)


def _exchange_sibling(gs, name, sequencer_id=None, after=()):
    n = len(gs)

    def body(*refs):
        g_refs, out_refs = refs[:n], refs[n:2 * n]
        send_sems, recv_sems = refs[2 * n:]
        x, y, c = _place()
        if sequencer_id is not None:
            _handshake([(x, y, 1 - c)])
        copies = [pltpu.make_async_remote_copy(
            src_ref=g_refs[a].at[2 * k + 1 - c], dst_ref=out_refs[a].at[k], send_sem=send_sems.at[4 * a + k],
            recv_sem=recv_sems.at[4 * a + k], device_id=(x, y, 1 - c), device_id_type=MESH)
            for a in range(n) for k in range(4)]
        for cp in copies:
            cp.start()
        for cp in copies:
            cp.wait()

    return _comm_call(body, name, gs, [jax.ShapeDtypeStruct((4,) + g.shape[1:], g.dtype) for g in gs],
                      [pltpu.SemaphoreType.DMA((4 * n,)), pltpu.SemaphoreType.DMA((4 * n,))], sequencer_id, after)


def _other_chips():
    x, y = lax.axis_index("x"), lax.axis_index("y")
    return jnp.stack([2 * (1 - x) + y, 2 * x + 1 - y, 2 * (1 - x) + 1 - y]).astype(jnp.int32)


def _add_sibling(gs, r1s, name, after=()):
    n = len(gs)
    _, r, w = gs[0].shape
    tr, tc = _elementwise_block(r, w)
    chips = _other_chips()
    slabs = 2 * chips + lax.axis_index("c").astype(jnp.int32)

    def body(slab_ref, chip_ref, *refs):
        g_refs, r_refs, o_refs = refs[:n], refs[n:2 * n], refs[2 * n + len(after):]
        for g_ref, r_ref, o_ref in zip(g_refs, r_refs, o_refs):
            o_ref[...] = (g_ref[...] + r_ref[...]).astype(BF16)

    return pl.pallas_call(
        body, name=name, out_shape=[jax.ShapeDtypeStruct((3, r, w), BF16)] * n,
        grid_spec=pltpu.PrefetchScalarGridSpec(
            num_scalar_prefetch=2, grid=(3, r // tr, w // tc),
            in_specs=[pl.BlockSpec((1, tr, tc), lambda k, i, j, slab_ref, chip_ref: (slab_ref[k], i, j))] * n
            + [pl.BlockSpec((1, tr, tc), lambda k, i, j, slab_ref, chip_ref: (chip_ref[k], i, j))] * n
            + [ANY] * len(after),
            out_specs=[pl.BlockSpec((1, tr, tc), lambda k, i, j, slab_ref, chip_ref: (k, i, j))] * n),
        compiler_params=_cparams("parallel", "parallel", "parallel"),
    )(slabs, chips, *gs, *r1s, *after)


def _exchange_chips(ps, name, sequencer_id=None, after=()):
    n = len(ps)

    def body(*refs):
        p_refs, out_refs = refs[:n], refs[n:2 * n]
        send_sems, recv_sems = refs[2 * n:]
        x, y, c = _place()
        chips = [(1 - x, y), (x, 1 - y), (1 - x, 1 - y)]
        if sequencer_id is not None:
            _handshake([(*chip, c) for chip in chips])
        copies = [pltpu.make_async_remote_copy(
            src_ref=p_refs[a].at[j], dst_ref=out_refs[a].at[j], send_sem=send_sems.at[3 * a + j],
            recv_sem=recv_sems.at[3 * a + j], device_id=(px, py, c), device_id_type=MESH)
            for a in range(n) for j, (px, py) in enumerate(chips)]
        for cp in copies:
            cp.start()
        for cp in copies:
            cp.wait()

    return _comm_call(body, name, ps, [jax.ShapeDtypeStruct(p.shape, p.dtype) for p in ps],
                      [pltpu.SemaphoreType.DMA((3 * n,)), pltpu.SemaphoreType.DMA((3 * n,))], sequencer_id, after)


def _sum_chips(gs, r1s, others, name):
    n = len(gs)
    _, r, c = gs[0].shape
    tr, tc = _elementwise_block(r, c)
    chip = 2 * lax.axis_index("x") + lax.axis_index("y")
    place = jnp.stack([2 * chip + lax.axis_index("c"), chip]).astype(jnp.int32)

    def body(place_ref, *refs):
        for g_ref, r_ref, others_ref, o_ref in zip(refs[:n], refs[n:2 * n], refs[2 * n:3 * n], refs[3 * n:]):
            s = g_ref[0] + r_ref[0]
            for j in range(3):
                s = s + others_ref[j].astype(F32)
            o_ref[...] = s

    return pl.pallas_call(
        body, name=name, out_shape=[jax.ShapeDtypeStruct((r, c), F32)] * n,
        grid_spec=pltpu.PrefetchScalarGridSpec(
            num_scalar_prefetch=1, grid=(r // tr, c // tc),
            in_specs=[pl.BlockSpec((1, tr, tc), lambda i, j, place_ref: (place_ref[0], i, j))] * n
            + [pl.BlockSpec((1, tr, tc), lambda i, j, place_ref: (place_ref[1], i, j))] * n
            + [pl.BlockSpec((3, tr, tc), lambda i, j, place_ref: (0, i, j))] * n,
            out_specs=[pl.BlockSpec((tr, tc), lambda i, j, place_ref: (i, j))] * n),
        compiler_params=_cparams("parallel", "parallel"),
    )(place, *gs, *r1s, *others)


def _reorder_in_proj(wt):
    za_zb, zc, xbc, dt, gates = (wt[:4096], wt[4096:5120], wt[5120:7168], wt[7168:7184], wt[7184:])
    return jnp.concatenate([za_zb, xbc, gates, zc, dt, jnp.zeros((DT_PAD - 16, wt.shape[1]), wt.dtype)], axis=0)


def _restore_in_proj(wt):
    return jnp.concatenate([wt[:4096], wt[OFF_ZC:OFF_ZC + W_ZC], wt[OFF_XBC:OFF_XBC + W_XBC],
                            wt[OFF_DT:OFF_DT + 16], wt[OFF_GATE:OFF_GATE + W_GATE]], axis=0)


def _lanes_from_devices(g):
    return jnp.moveaxis(g, 0, 1).reshape(g.shape[1], N_DEV * g.shape[2])


def _lanes_to_devices(a):
    return jnp.moveaxis(a.reshape(a.shape[0], N_DEV, a.shape[1] // N_DEV), 1, 0)


def _pad_lanes(a, width):
    return jnp.pad(a, ((0, 0), (0, width - a.shape[1])))


BIG = ("w_in", "w_branch_a", "w_branch_b", "w_branch_c", "w_out", "w_mlp_up", "w_mlp_down")
SMALL_SHARDED = ("b_gate", "lru_conv_w", "ssd_conv_w")
REPLICATED = ("norm_mix_g", "gmlp_ln_g", "gmlp_ln_b", "gmlp_w_s", "gmlp_b_s", "lru_conv_b", "lru_w_r", "lru_b_r",
              "lru_w_i", "lru_b_i", "lru_lambda", "ssd_conv_b", "ssd_dt_bias", "ssd_a_log", "ssd_d", "ssd_norm_g",
              "norm_mlp_g", "final_norm_g")
WEIGHTS = ("norm_mix_g", "w_in", "b_gate", "gmlp_ln_g", "gmlp_ln_b", "gmlp_w_s", "gmlp_b_s", "lru_conv_w", "lru_conv_b",
           "lru_w_r", "lru_b_r", "lru_w_i", "lru_b_i", "lru_lambda", "ssd_conv_w", "ssd_conv_b", "ssd_dt_bias",
           "ssd_a_log", "ssd_d", "ssd_norm_g", "w_branch_a", "w_branch_b", "w_branch_c", "w_out", "norm_mlp_g",
           "w_mlp_up", "w_mlp_down", "final_norm_g")
TRANSPOSED = ("w_in", "w_mlp_up")
SMALL_MATRICES = ("gmlp_w_s", "lru_w_r", "lru_w_i")
SMALL_VECTORS = tuple(n for n in REPLICATED if n not in SMALL_MATRICES)
GRADIENT_GROUPS = {"mlp": ("w_mlp_up", "w_mlp_down"), "mix": ("w_branch_a", "w_branch_b", "w_branch_c", "w_out"),
                   "in": ("w_in",)}


def _layer_params(full, l):
    row = lambda a: a.reshape(1, -1)
    return dict(
        norm_mix_g=row(full["norm_mix_g"][l]), norm_mlp_g=row(full["norm_mlp_g"][l]),
        gmlp=(row(full["gmlp_ln_g"][l]), row(full["gmlp_ln_b"][l]), full["gmlp_w_s"][l].reshape(GROUPS_A * CHUNK, CHUNK),
              full["gmlp_b_s"][l].T),
        lru=(full["lru_conv_w"][l], row(full["lru_conv_b"][l]), full["lru_w_r"][l].reshape(D, D // HEADS_B),
             row(full["lru_b_r"][l]), full["lru_w_i"][l].reshape(D, D // HEADS_B), row(full["lru_b_i"][l]),
             row(full["lru_lambda"][l])),
        ssd=(full["ssd_conv_w"][l], row(full["ssd_conv_b"][l]), _pad_lanes(row(full["ssd_dt_bias"][l]), DT_PAD),
             _pad_lanes(row(full["ssd_a_log"][l]), DT_PAD), _pad_lanes(row(full["ssd_d"][l]), DT_PAD),
             row(full["ssd_norm_g"][l])),
        b_gate=full["b_gate"][l],
    )


def _forward_layer(h, hn, p, wb, l, after_mixers=None, next_norm_g=None):
    tag = f"l{l}"
    t_row = 512
    if hn is None:
        (hn,), _, _ = _chunk_fwd(_f_rmsnorm, f"norm_mix_{tag}", t_row, [(h, 0, D)], [p["norm_mix_g"]], [(D, BF16)])
    proj = _matmul(hn, wb["w_in"], f"in_proj_{tag}", tb=True)
    (ya,), _, _ = _chunk_fwd(_f_gmlp, f"gmlp_{tag}", CHUNK, [(proj, OFF_ZA, W_ZA)], p["gmlp"], [(D, BF16)])
    lru_xs = [(proj, OFF_ZB, D), (proj, OFF_ZB + D, D)]
    (yb,), lru_saved, _ = _chunk_fwd(_f_lru, f"lru_{tag}", CHUNK, lru_xs, p["lru"], [(D, BF16)], halo_idx=(0,),
                                     carry_shapes=[(1, D)], save_carries=True)
    ssd_xs = [(proj, OFF_ZC, W_ZC), (proj, OFF_XBC, W_XBC), (proj, OFF_DT, W_DT)]
    (yc,), ssd_saved, _ = _chunk_fwd(_f_ssd, f"ssd_{tag}", CHUNK, ssd_xs, p["ssd"], [(D, BF16)], halo_idx=(1,),
                                     carry_shapes=[(HEADS_C * HEAD_DIM_C, STATE_C)], save_carries=True)
    if after_mixers is not None:
        after_mixers(yc)
    pa = _matmul(ya, wb["w_branch_a"], f"branch_a_{tag}")
    pb = _matmul(yb, wb["w_branch_b"], f"branch_b_{tag}")
    pc = _matmul(yc, wb["w_branch_c"], f"branch_c_{tag}")
    merge_xs = [(pa, 0, D), (pb, 0, D), (pc, 0, D), (proj, OFF_GATE, W_GATE)]
    (merged,), _, _ = _chunk_fwd(_f_merge, f"merge_{tag}", t_row, merge_xs, [p["b_gate"]], [(D, BF16)])

    def add_and_norm(acc, res, g):
        s = acc + res
        return s, _rms(s, g)

    h_mid, hn2 = _matmul(merged, wb["w_out"], f"out_proj_{tag}", outs=(F32, BF16), epilogue=add_and_norm, extras=(h,),
                         row_params=(p["norm_mlp_g"],))

    def relu_sq(acc):
        r = jnp.maximum(acc, 0.0)
        return r, r * r

    relu_up, act = _matmul(hn2, wb["w_mlp_up"], f"mlp_up_{tag}", tb=True, outs=(F32, BF16), epilogue=relu_sq)
    if next_norm_g is None:
        hn_out = None
        h_out = _matmul(act, wb["w_mlp_down"], f"mlp_down_{tag}", epilogue=lambda acc, res: (acc + res,), extras=(h_mid,))
    else:
        h_out, hn_out = _matmul(act, wb["w_mlp_down"], f"mlp_down_{tag}", outs=(F32, BF16), epilogue=add_and_norm,
                                extras=(h_mid,), row_params=(next_norm_g,))
    saved = dict(h=h, hn=hn, proj=proj, ya=ya, yb=yb, yc=yc, lru_saved=lru_saved, ssd_saved=ssd_saved, pa=pa, pb=pb,
                 pc=pc, merged=merged, h_mid=h_mid, hn2=hn2, relu_up=relu_up, act=act, lru_xs=lru_xs, ssd_xs=ssd_xs,
                 merge_xs=merge_xs)
    return h_out, hn_out, saved


def _backward_layer(dh, sv, p, wb, l):
    tag = f"l{l}"
    t_row = 512
    g = {}
    d_up = _matmul(dh, wb["w_mlp_down"], f"d_act_{tag}", tb=True, outs=(BF16,),
                   epilogue=lambda acc, r: (acc * (2.0 * r),), extras=(sv["relu_up"],))
    g["w_mlp_down"] = _matmul(sv["act"], dh, f"dw_mlp_down_{tag}", ta=True)
    g["w_mlp_up"] = _matmul(d_up, sv["hn2"], f"dw_mlp_up_{tag}", ta=True)

    def norm_bwd(d_normed, h_rows, d_rows, gain):
        _, vjp = jax.vjp(lambda hh, gg: _f_rmsnorm_res((), (), (hh,), (gg,)), h_rows, gain)
        return vjp(((), (d_normed, d_rows)))

    d_mid, g["norm_mlp_g"] = _matmul(d_up, wb["w_mlp_up"], f"d_hn2_{tag}", epilogue=norm_bwd, extras=(sv["h_mid"], dh),
                                     row_params=(p["norm_mlp_g"],), row_sums=1, max_tm=EPILOGUE_ROWS)
    d_merged = _matmul(d_mid, wb["w_out"], f"d_merged_{tag}", tb=True)
    g["w_out"] = _matmul(sv["merged"], d_mid, f"dw_out_{tag}", ta=True)
    (d_pa, d_pb, d_pc, d_gate), (g["b_gate"],) = _chunk_bwd(
        _f_merge, f"merge_bwd_{tag}", t_row, sv["merge_xs"], [p["b_gate"]], [d_merged], [BF16] * 4)
    d_y = {}
    for br, d_p, y in (("a", d_pa, sv["ya"]), ("b", d_pb, sv["yb"]), ("c", d_pc, sv["yc"])):
        g[f"w_branch_{br}"] = _matmul(y, d_p, f"dw_branch_{br}_{tag}", ta=True)
        d_y[br] = _matmul(d_p, wb[f"w_branch_{br}"], f"d_y{br}_{tag}", tb=True)
    (d_za,), g_gmlp = _chunk_bwd(_f_gmlp, f"gmlp_bwd_{tag}", CHUNK, [(sv["proj"], OFF_ZA, W_ZA)], p["gmlp"],
                                 [d_y["a"]], [BF16])
    (d_xb, d_gt), g_lru = _chunk_bwd(_f_lru, f"lru_bwd_{tag}", CHUNK, sv["lru_xs"], p["lru"], [d_y["b"]], [BF16] * 2,
                                     halo_idx=(0,), saved=sv["lru_saved"])
    (d_zc, d_xbc, d_dt), g_ssd = _chunk_bwd(_f_ssd, f"ssd_bwd_{tag}", CHUNK, sv["ssd_xs"], p["ssd"], [d_y["c"]],
                                            [BF16] * 3, halo_idx=(1,), saved=sv["ssd_saved"])
    d_proj = [d_za, d_xb, d_gt, d_xbc, d_gate, d_zc, d_dt]
    g["w_in"] = _matmul_pieces_t(d_proj, sv["hn"], f"dw_in_{tag}")
    d_h, g["norm_mix_g"] = _matmul_pieces(d_proj, wb["w_in"], f"d_hn_{tag}", epilogue=norm_bwd, extras=(sv["h"], d_mid),
                                          row_params=(p["norm_mix_g"],), row_sums=1, max_tm=EPILOGUE_ROWS)
    g["w_in"] = _restore_in_proj(g["w_in"])
    for n in BIG:
        g[n] = g[n].reshape(N_DEV, g[n].shape[0] // N_DEV, g[n].shape[1])
    g["gmlp_ln_g"], g["gmlp_ln_b"], g["gmlp_w_s"] = g_gmlp[:3]
    g["gmlp_b_s"] = g_gmlp[3].T
    (g["lru_conv_w"], g["lru_conv_b"], g["lru_w_r"], g["lru_b_r"], g["lru_w_i"], g["lru_b_i"], g["lru_lambda"]) = g_lru
    g["ssd_conv_w"], g["ssd_conv_b"] = g_ssd[:2]
    g["ssd_dt_bias"], g["ssd_a_log"], g["ssd_d"] = (a[:, :HEADS_C] for a in g_ssd[2:5])
    g["ssd_norm_g"] = g_ssd[5]
    g["later"] = {"mlp": d_merged, "mix": d_za, "in": d_h}
    return d_h, g


LOSS_ROWS = 512


def _loss_and_grads(h, target, full, layer_weights, first_gathered):
    seq = h.shape[0]
    layer_p = [_layer_params(full, l) for l in range(DEPTH)]
    layer_w = [layer_weights(0, (first_gathered,))]
    saved = []
    hn = None
    for l in range(DEPTH):
        fetch_next = next_norm_g = None
        if l + 1 < DEPTH:
            fetch_next = lambda y, l=l: layer_w.append(layer_weights(l + 1, (layer_w[l]["w_mlp_down"], y)))
            next_norm_g = layer_p[l + 1]["norm_mix_g"]
        h, hn, sv = _forward_layer(h, hn, layer_p[l], layer_w[l], l, fetch_next, next_norm_g)
        saved.append(sv)
    final_g = full["final_norm_g"].reshape(1, D)
    loss_xs = [(h, 0, D), (target, 0, D)]
    t_loss = min(LOSS_ROWS, seq)
    _, _, (loss_acc,) = _chunk_fwd(_f_loss, "loss", t_loss, loss_xs, [final_g], [], carry_shapes=[(1, 128)],
                                   final_carries=True)
    zero_acc = jnp.zeros((seq // t_loss, 1, 128), F32)
    seed = lambda shape: (lax.broadcasted_iota(jnp.int32, shape, 1) == 0).astype(F32)
    (dh, _), (g_final,) = _chunk_bwd(_f_loss, "loss_bwd", t_loss, loss_xs, [final_g], [], [F32, F32], saved=[zero_acc],
                                     carry_seed=seed)
    layer_g = [None] * DEPTH
    for l in reversed(range(DEPTH)):
        dh, layer_g[l] = _backward_layer(dh, saved[l], layer_p[l], layer_w[l], l)
    return loss_acc, dh, layer_g, g_final


def _small_views(d):
    views = {n: d[n] for n in REPLICATED}
    views["gmlp_b_s"] = d["gmlp_b_s"].reshape(DEPTH * GROUPS_A, CHUNK)
    views["final_norm_g"] = d["final_norm_g"].reshape(1, D)
    for n in SMALL_MATRICES:
        views[n] = d[n].reshape(DEPTH * D, D // HEADS_B)
    return views


def kernel(x, norm_mix_g, w_in, b_gate, gmlp_ln_g, gmlp_ln_b, gmlp_w_s, gmlp_b_s, lru_conv_w, lru_conv_b, lru_w_r, lru_b_r, lru_w_i, lru_b_i, lru_lambda, ssd_conv_w, ssd_conv_b, ssd_dt_bias, ssd_a_log, ssd_d, ssd_norm_g, w_branch_a, w_branch_b, w_branch_c, w_out, norm_mlp_g, w_mlp_up, w_mlp_down, final_norm_g, loss_target, m_norm_mix_g, m_w_in, m_b_gate, m_gmlp_ln_g, m_gmlp_ln_b, m_gmlp_w_s, m_gmlp_b_s, m_lru_conv_w, m_lru_conv_b, m_lru_w_r, m_lru_b_r, m_lru_w_i, m_lru_b_i, m_lru_lambda, m_ssd_conv_w, m_ssd_conv_b, m_ssd_dt_bias, m_ssd_a_log, m_ssd_d, m_ssd_norm_g, m_w_branch_a, m_w_branch_b, m_w_branch_c, m_w_out, m_norm_mlp_g, m_w_mlp_up, m_w_mlp_down, m_final_norm_g, v_norm_mix_g, v_w_in, v_b_gate, v_gmlp_ln_g, v_gmlp_ln_b, v_gmlp_w_s, v_gmlp_b_s, v_lru_conv_w, v_lru_conv_b, v_lru_w_r, v_lru_b_r, v_lru_w_i, v_lru_b_i, v_lru_lambda, v_ssd_conv_w, v_ssd_conv_b, v_ssd_dt_bias, v_ssd_a_log, v_ssd_d, v_ssd_norm_g, v_w_branch_a, v_w_branch_b, v_w_branch_c, v_w_out, v_norm_mlp_g, v_w_mlp_up, v_w_mlp_down, v_final_norm_g):
    args = locals()
    w = {n: args[n] for n in WEIGHTS}
    m = {n: args["m_" + n] for n in WEIGHTS}
    v = {n: args["v_" + n] for n in WEIGHTS}
    seq = x.shape[1]
    h = x.reshape(seq, D)
    target = loss_target.reshape(seq, D)

    def shard_on_wire(n, l):
        return (w[n][l].T if n in TRANSPOSED else w[n][l]).astype(BF16)

    first = _all_gather([shard_on_wire("w_in", 0)] + [w[n] for n in SMALL_SHARDED], "gather_weights_first")
    full = {n: w[n] for n in REPLICATED}
    for n, g in zip(SMALL_SHARDED, first[1:]):
        full[n] = jnp.stack([_lanes_from_devices(g[:, l]) for l in range(DEPTH)])

    def layer_weights(l, after):
        have = {"w_in": first[0]} if l == 0 else {}
        names = [n for n in BIG if n not in have]
        later = _all_gather([shard_on_wire(n, l) for n in names], f"gather_weights_l{l}", SEQ_GATHER + l, after=after)
        have.update(zip(names, later))
        wl = {n: have[n].reshape(-1, D) for n in BIG}
        wl["w_in"] = _reorder_in_proj(wl["w_in"])
        return wl

    loss_local, dh, layer_g, g_final = _loss_and_grads(h, target, full, layer_weights, first[0])
    grad_x = dh.reshape(x.shape)
    out = {}
    kinds = ("grad", "delta", "new_m", "new_v")

    sequencer_before = {}

    def reduce_scatter(slabs, tag, on_sequencer, later=()):
        keys = list(slabs)
        ids = (SEQ_TO_SIBLING, SEQ_TO_CHIPS) if on_sequencer else (None, None)
        from_sibling = _exchange_sibling([slabs[k] for k in keys], f"grads_to_sibling_{tag}", ids[0],
                                         sequencer_before.get("sibling", ()))
        from_sibling = dict(zip(keys, from_sibling))
        same_shape = {}
        for k in keys:
            same_shape.setdefault(slabs[k].shape[1:], []).append(k)
        chip_sums = {}
        for (r, c), ks in same_shape.items():
            sums = _add_sibling([slabs[k] for k in ks], [from_sibling[k] for k in ks], f"add_sibling_{tag}_{r}x{c}", later)
            chip_sums.update(zip(ks, sums))
        from_chips = _exchange_chips([chip_sums[k] for k in keys], f"grads_to_chips_{tag}", ids[1],
                                     sequencer_before.get("chips", ()))
        if on_sequencer:
            sequencer_before["sibling"], sequencer_before["chips"] = (from_sibling[keys[0]],), (from_chips[0],)
        from_chips = dict(zip(keys, from_chips))
        summed = {}
        for (r, c), ks in same_shape.items():
            sums = _sum_chips([slabs[k] for k in ks], [from_sibling[k] for k in ks], [from_chips[k] for k in ks],
                              f"sum_chips_{tag}_{r}x{c}")
            summed.update(zip(ks, sums))
        return summed

    small = {}
    for n in SMALL_SHARDED:
        small[n, None] = jnp.concatenate([_lanes_to_devices(layer_g[l][n]) for l in range(DEPTH)], axis=1)
    g_small = {n: jnp.concatenate([layer_g[l][n] for l in range(DEPTH)], axis=0) for n in REPLICATED[:-1]}
    g_small["final_norm_g"] = g_final
    for n in SMALL_MATRICES:
        small[n, None] = g_small[n].reshape(N_DEV, -1, g_small[n].shape[-1])
    reduced = {}
    groups = [(l, grp) for l in range(DEPTH - 1, -1, -1) for grp in GRADIENT_GROUPS]
    for l, grp in groups:
        if (l, grp) == groups[-1]:
            reduced.update(reduce_scatter(small, "small", True, (layer_g[l][GRADIENT_GROUPS[grp][0]],)))
        slabs = {(n, l): layer_g[l][n] for n in GRADIENT_GROUPS[grp]}
        reduced.update(reduce_scatter(slabs, f"{grp}_l{l}", True, (layer_g[l]["later"][grp],)))
    for n in BIG:
        grads = [reduced[n, l] for l in range(DEPTH)]
        if n in TRANSPOSED and w[n].shape[-1] % LANES:
            res = _adamw_transposed(grads, w[n], m[n], v[n], f"adamw_{n}")
        else:
            res = _adamw([(g.T if n in TRANSPOSED else g)[None] for g in grads], w[n], m[n], v[n], f"adamw_{n}")
        for kind, a in zip(kinds, res):
            out[kind, n] = a
    for n in SMALL_SHARDED:
        one = lambda a: a.reshape((1, -1, a.shape[-1]))
        for kind, a in zip(kinds, _adamw([reduced[n, None][None]], one(w[n]), one(m[n]), one(v[n]), f"adamw_{n}")):
            out[kind, n] = a.reshape(w[n].shape)

    to_gather = [reduced[n, None] if n in SMALL_MATRICES else g_small[n] for n in REPLICATED]
    *g_gathered, loss_terms = _all_gather(to_gather + [loss_local], "gather_small_grads", SEQ_GATHER_SMALL)
    g_all = dict(zip(REPLICATED, g_gathered))
    wv, mv, vv = _small_views(w), _small_views(m), _small_views(v)
    res, loss_sum = _adamw_small([g_all[n] for n in SMALL_VECTORS],
                                 *[[d[n] for n in SMALL_VECTORS] for d in (wv, mv, vv)], loss_terms, "adamw_vectors")
    loss = loss_sum[0, 0]
    for kind, arrays in zip(kinds, res):
        for n, a in zip(SMALL_VECTORS, arrays):
            out[kind, n] = a.reshape(w[n].shape)
    for n in SMALL_MATRICES:
        g_full = g_all[n].reshape((1, 1) + wv[n].shape)
        for kind, a in zip(kinds, _adamw([g_full[0]], wv[n][None], mv[n][None], vv[n][None], f"adamw_{n}")):
            out[kind, n] = a.reshape(w[n].shape)

    return (loss, grad_x, *[out[kind, n] for kind in kinds for n in WEIGHTS])
```

```python
import functools

import jax
import jax.numpy as jnp
from jax import lax
from jax.experimental import pallas as pl
from jax.experimental.pallas import tpu as pltpu
from jax.experimental.pallas import tpu_sc as plsc

F32 = jnp.float32
BF16 = jnp.bfloat16
MESH = pl.DeviceIdType.MESH

D = 1024
DEPTH = 2
EPS = 1e-6
CHUNK = 128
GROUPS_A = 8
HEADS_B = 8
LRU_C = 8.0
HEADS_C = 16
HEAD_DIM_C = 64
GROUPS_C = 4
STATE_C = 128
DT_PAD = 128
OFF_ZA, W_ZA = 0, 2048
OFF_ZB, W_ZB = 2048, 2048
OFF_XBC, W_XBC = 4096, 2048
OFF_GATE, W_GATE = 6144, 3072
OFF_ZC, W_ZC = 9216, 1024
OFF_DT, W_DT = 10240, DT_PAD
D_IN_PAD = 10368
N_DEV = 8
SEQ_GATHER = 1
SEQ_TO_SIBLING = SEQ_GATHER + DEPTH
SEQ_TO_CHIPS = SEQ_TO_SIBLING + 1
SEQ_GATHER_SMALL = SEQ_TO_CHIPS + 1

ADAM_LR = 0.001
ADAM_B1 = 0.9
ADAM_B2 = 0.999
ADAM_EPS = 1e-08
ADAM_WD = 0.01
ADAM_STEP = 10

VMEM_LIMIT = 56 * 1024 * 1024
HALO = 8


def _cparams(*sem):
    return pltpu.CompilerParams(dimension_semantics=sem, vmem_limit_bytes=VMEM_LIMIT)


def _bf(x):
    return x.astype(BF16)


def _dg(a, b, ca, cb):
    return lax.dot_general(a, b, (((ca,), (cb,)), ((), ())), preferred_element_type=F32)


@functools.partial(jax.custom_vjp, nondiff_argnums=(2, 3))
def _mm(a, b, ta, tb):
    return _dg(_bf(a), _bf(b), 0 if ta else 1, 1 if tb else 0)


def _mm_fwd(a, b, ta, tb):
    return _mm(a, b, ta, tb), (a, b)


def _mm_bwd(ta, tb, res, g):
    a, b = res
    ma = 1 if ta else 0
    nb = 0 if tb else 1
    gb, ab, bb = _bf(g), _bf(a), _bf(b)
    da = _dg(bb, gb, nb, 1) if ta else _dg(gb, bb, 1, nb)
    db = _dg(gb, ab, 0, ma) if tb else _dg(ab, gb, ma, 0)
    return da.astype(a.dtype), db.astype(b.dtype)


_mm.defvjp(_mm_fwd, _mm_bwd)


def _slices(x, sizes, axis):
    out, lo = [], 0
    for size in sizes:
        out.append(lax.slice_in_dim(x, lo, lo + size, axis=axis))
        lo += size
    return tuple(out)


@functools.partial(jax.custom_vjp, nondiff_argnums=(1,))
def _split_cols(x, widths):
    return _slices(x, widths, 1)


_split_cols.defvjp(lambda x, widths: (_slices(x, widths, 1), None),
                   lambda widths, _, gs: (jnp.concatenate(gs, axis=1),))


@functools.partial(jax.custom_vjp, nondiff_argnums=(1,))
def _split_rows(x, heights):
    return _slices(x, heights, 0)


_split_rows.defvjp(lambda x, heights: (_slices(x, heights, 0), None),
                   lambda heights, _, gs: (jnp.concatenate(gs, axis=0),))


def _col(x, j):
    lane = lax.broadcasted_iota(jnp.int32, x.shape, 1)
    return jnp.sum(jnp.where(lane == j, x, 0.0), axis=1, keepdims=True)


def _row(x, i):
    r = lax.broadcasted_iota(jnp.int32, x.shape, 0)
    return jnp.sum(jnp.where(r == i, x, 0.0), axis=0, keepdims=True)


def _roll_down(x, s):
    return pltpu.roll(x, s, 0)


def _roll_up(x, s):
    return pltpu.roll(x, x.shape[0] - s, 0)


def _row_iota(x):
    return lax.broadcasted_iota(jnp.int32, x.shape, 0)


@functools.partial(jax.custom_vjp, nondiff_argnums=(2,))
def _shift_rows(halo, x, s):
    if s == 0:
        return x
    return _roll_down(jnp.concatenate([halo, x], axis=0), s)[HALO:]


def _shift_rows_fwd(halo, x, s):
    return _shift_rows(halo, x, s), None


def _shift_rows_bwd(s, _, g):
    if s == 0:
        return jnp.zeros((HALO, g.shape[1]), g.dtype), g
    ge = jnp.concatenate([jnp.zeros((HALO, g.shape[1]), g.dtype), g], axis=0)
    de = _roll_up(ge, s)
    return de[:HALO], de[HALO:]


_shift_rows.defvjp(_shift_rows_fwd, _shift_rows_bwd)


SUBLANES = 8
LANES = 128


def _scan_tiles(a, b, carry, up):
    n, c = a.shape
    nt = n // SUBLANES
    a = a.reshape(nt, SUBLANES, c)
    b = b.reshape(nt, SUBLANES, c)
    sub = lax.broadcasted_iota(jnp.int32, a.shape, 1)
    s = 1
    while s < SUBLANES:
        keep = (sub < SUBLANES - s) if up else (sub >= s)
        shift = SUBLANES - s if up else s
        a_sh = jnp.where(keep, pltpu.roll(a, shift, 1), 1.0)
        b_sh = jnp.where(keep, pltpu.roll(b, shift, 1), 0.0)
        b = a * b_sh + b
        a = a * a_sh
        s *= 2
    tiles = [None] * nt
    edge = 0 if up else SUBLANES - 1
    for j in (range(nt - 1, -1, -1) if up else range(nt)):
        tiles[j] = b[j] if carry is None else b[j] + a[j] * carry
        carry = tiles[j][edge:edge + 1, :]
    return jnp.concatenate(tiles, axis=0)


@jax.custom_vjp
def _lin_scan(a, b, h0):
    return _scan_tiles(a, b, h0, up=False)


def _lin_scan_fwd(a, b, h0):
    h = _lin_scan(a, b, h0)
    return h, (a, h0, h)


def _lin_scan_bwd(res, g):
    a, h0, h = res
    n = a.shape[0]
    row = _row_iota(a)
    a_next = jnp.where(row < n - 1, _roll_up(a, 1), 0.0)
    gg = _scan_tiles(a_next, g, None, up=True)
    h_prev = jnp.where(row >= 1, _roll_down(h, 1), h0)
    return gg * h_prev, gg, _row(a * gg, 0)


_lin_scan.defvjp(_lin_scan_fwd, _lin_scan_bwd)


@jax.custom_vjp
def _cumsum_rows(x):
    n = x.shape[0]
    row = _row_iota(x)
    s = 1
    while s < n:
        x = x + jnp.where(row >= s, _roll_down(x, s), 0.0)
        s *= 2
    return x


def _cumsum_rows_fwd(x):
    return _cumsum_rows(x), None


def _cumsum_rows_bwd(_, g):
    n = g.shape[0]
    row = _row_iota(g)
    s = 1
    while s < n:
        g = g + jnp.where(row < n - s, _roll_up(g, s), 0.0)
        s *= 2
    return (g,)


_cumsum_rows.defvjp(_cumsum_rows_fwd, _cumsum_rows_bwd)


def _sigmoid(x):
    return jax.nn.sigmoid(x)


def _softplus(x):
    return jnp.maximum(x, 0.0) + jnp.log1p(jnp.exp(-jnp.abs(x)))


def _gelu(x):
    return jax.nn.gelu(x, approximate=True)


def _neg_expm1(x):
    series = -x * (1.0 + x * (0.5 + x * (1.0 / 6.0 + x * (1.0 / 24.0))))
    return jnp.where(x > -0.01, series, 1.0 - jnp.exp(x))


def _rms(x, g):
    return x * lax.rsqrt(jnp.mean(x * x, axis=-1, keepdims=True) + EPS) * g


def _f_rmsnorm(carries, halos, xs, params):
    (h,) = xs
    (g,) = params
    return (), (_rms(h, g),)


def _f_rmsnorm_res(carries, halos, xs, params):
    (h,) = xs
    (g,) = params
    return (), (_rms(h, g), h)


def _f_gmlp(carries, halos, xs, params):
    (za,) = xs
    ln_g, ln_b, w_s, b_st = params
    u, v = _split_cols(_gelu(za), (D, D))
    vc = v - jnp.mean(v, axis=-1, keepdims=True)
    vn = vc * lax.rsqrt(jnp.mean(vc * vc, axis=-1, keepdims=True) + EPS) * ln_g + ln_b
    q = CHUNK
    causal = lax.broadcasted_iota(jnp.int32, (q, q), 0) >= lax.broadcasted_iota(jnp.int32, (q, q), 1)
    mixed = []
    for g, (w, vg) in enumerate(zip(_split_rows(w_s, (q,) * GROUPS_A), _split_cols(vn, (q,) * GROUPS_A))):
        mixed.append(_mm(jnp.where(causal, w, 0.0), vg, False, False) + _col(b_st, g))
    return (), (u * jnp.concatenate(mixed, axis=1),)


def _conv4(halo, x, w, b):
    y = b + _row(w, 3) * x
    for k in range(3):
        y = y + _row(w, k) * _shift_rows(halo, x, 3 - k)
    return y


def _f_lru(carries, halos, xs, params):
    (h0,) = carries
    (halo,) = halos
    xb_pre, gate = xs
    conv_w, conv_b, w_r, b_r, w_i, b_i, lam = params
    xb = _conv4(halo, xb_pre, conv_w, conv_b)
    hd = D // HEADS_B
    r_parts, i_parts = [], []
    heads = (hd,) * HEADS_B
    for xh, wr, wi in zip(_split_cols(xb, heads), _split_rows(w_r, heads), _split_rows(w_i, heads)):
        r_parts.append(_mm(xh, wr, False, False))
        i_parts.append(_mm(xh, wi, False, False))
    r = _sigmoid(jnp.concatenate(r_parts, axis=1) + b_r)
    i = _sigmoid(jnp.concatenate(i_parts, axis=1) + b_i)
    log_a = -LRU_C * r * _softplus(-lam)
    a = jnp.exp(log_a)
    inp = jnp.sqrt(_neg_expm1(2.0 * log_a)) * (i * xb)
    h = _lin_scan(a, inp, h0)
    return (_row(h, h.shape[0] - 1),), (_gelu(gate) * h,)


def _f_ssd(carries, halos, xs, params):
    (st,) = carries
    (halo,) = halos
    z, xbc_pre, dt_raw = xs
    conv_w, conv_b, dt_bias, a_log, d_skip, norm_g = params
    t = z.shape[0]
    xc = _conv4(halo, xbc_pre, conv_w, conv_b)
    xbc = xc * _sigmoid(xc)
    x_all, b_all, c_all = _split_cols(xbc, (D, GROUPS_C * STATE_C, GROUPS_C * STATE_C))
    x_pairs = _split_cols(x_all, (128,) * (HEADS_C // 2))
    b_groups = _split_cols(b_all, (STATE_C,) * GROUPS_C)
    c_groups = _split_cols(c_all, (STATE_C,) * GROUPS_C)
    st_pairs = _split_rows(st, (128,) * (HEADS_C // 2))
    dt = _softplus(dt_raw + dt_bias)
    adt = dt * (-jnp.exp(a_log))
    acs = _cumsum_rows(adt)
    acs_t = acs.T
    a_last = _row(acs, t - 1)
    lo = lax.broadcasted_iota(jnp.int32, (t, 128), 1) < HEAD_DIM_C
    lo_rows = lax.broadcasted_iota(jnp.int32, (128, STATE_C), 0) < HEAD_DIM_C
    causal = lax.broadcasted_iota(jnp.int32, (t, t), 0) >= lax.broadcasted_iota(jnp.int32, (t, t), 1)
    y_parts, st_parts = [], []
    for g in range(GROUPS_C):
        bg, cg = b_groups[g], c_groups[g]
        cb = _mm(cg, bg, False, True)
        for pr in range(2):
            pair = 2 * g + pr
            h0, h1 = 2 * pair, 2 * pair + 1
            x2 = x_pairs[pair]
            ac0, ac1 = _col(acs, h0), _col(acs, h1)
            l0 = jnp.exp(jnp.where(causal, ac0 - _row(acs_t, h0), -1e30))
            l1 = jnp.exp(jnp.where(causal, ac1 - _row(acs_t, h1), -1e30))
            xdt = x2 * jnp.where(lo, _col(dt, h0), _col(dt, h1))
            y_diag = (_mm(cb * l0, jnp.where(lo, xdt, 0.0), False, False)
                      + _mm(cb * l1, jnp.where(lo, 0.0, xdt), False, False))
            al0, al1 = _col(a_last, h0), _col(a_last, h1)
            decay_s = jnp.where(lo, jnp.exp(al0 - ac0), jnp.exp(al1 - ac1))
            s_new = _mm(xdt * decay_s, bg, True, False)
            prev = st_pairs[pair]
            y_off = _mm(cg, prev, False, True) * jnp.where(lo, jnp.exp(ac0), jnp.exp(ac1))
            skip = jnp.where(lo, _col(d_skip, h0), _col(d_skip, h1))
            y_parts.append(y_diag + y_off + x2 * skip)
            st_parts.append(prev * jnp.where(lo_rows, jnp.exp(al0), jnp.exp(al1)) + s_new)
    y = jnp.concatenate(y_parts, axis=1) * (z * _sigmoid(z))
    gw = D // GROUPS_C
    yn = []
    for yg in _split_cols(y, (gw,) * GROUPS_C):
        yn.append(yg * lax.rsqrt(jnp.mean(yg * yg, axis=-1, keepdims=True) + EPS))
    return (jnp.concatenate(st_parts, axis=0),), (jnp.concatenate(yn, axis=1) * norm_g,)


def _f_merge(carries, halos, xs, params):
    pa, pb, pc, g_raw = xs
    (b_gate,) = params
    ga, gb, gc = _split_cols(g_raw, (D, D, D))
    m = (_sigmoid(ga + _row(b_gate, 0)) * pa + _sigmoid(gb + _row(b_gate, 1)) * pb
         + _sigmoid(gc + _row(b_gate, 2)) * pc)
    return (), (m,)


def _f_loss(carries, halos, xs, params):
    (acc,) = carries
    h, target = xs
    (g,) = params
    err = jnp.square(_rms(h, g) - target)
    part = 0.5 * jnp.sum(jnp.mean(err, axis=-1, keepdims=True), axis=0, keepdims=True)
    return (acc + part,), ()


def _x_specs(xs, t, index_of):
    specs = []
    for arr, off, width in xs:
        assert off % width == 0 and off + width <= arr.shape[1]
        specs.append(pl.BlockSpec((t, width), functools.partial(lambda j, cb: (index_of(j), cb), cb=off // width)))
    return specs


def _halo_specs(xs, halo_idx, t, index_of):
    specs = []
    for xi in halo_idx:
        _, off, width = xs[xi]
        specs.append(pl.BlockSpec(
            (HALO, width),
            functools.partial(lambda j, cb: (jnp.maximum(index_of(j) * (t // HALO) - 1, 0), cb), cb=off // width)))
    return specs


def _full_spec(a):
    return pl.BlockSpec(a.shape, functools.partial(lambda j, nd: (0,) * nd, nd=a.ndim))


def _chunk_fwd(f, name, t, xs, params, outs, halo_idx=(), carry_shapes=(), save_carries=False, final_carries=False):
    s = xs[0][0].shape[0]
    n = s // t
    nx, nh, npar, no, nc = len(xs), len(halo_idx), len(params), len(outs), len(carry_shapes)
    ns = nc if save_carries else 0
    nf = nc if final_carries else 0

    def body(*refs):
        x_refs, refs = refs[:nx], refs[nx:]
        h_refs, refs = refs[:nh], refs[nh:]
        p_refs, refs = refs[:npar], refs[npar:]
        y_refs, refs = refs[:no], refs[no:]
        s_refs, refs = refs[:ns], refs[ns:]
        f_refs, c_refs = refs[:nf], refs[nf:]
        i = pl.program_id(0)

        @pl.when(i == 0)
        def _():
            for c in c_refs:
                c[...] = jnp.zeros_like(c)

        carries = tuple(c[...] for c in c_refs)
        for s_ref, c in zip(s_refs, carries):
            s_ref[0] = c
        halos = tuple(jnp.where(i > 0, h[...].astype(F32), 0.0) for h in h_refs)
        new_c, ys = f(carries, halos, tuple(x[...].astype(F32) for x in x_refs), tuple(p[...] for p in p_refs))
        for y_ref, y in zip(y_refs, ys):
            y_ref[...] = y.astype(y_ref.dtype)
        for c, v in zip(c_refs, new_c):
            c[...] = v
        for f_ref, v in zip(f_refs, new_c):
            f_ref[...] = v

    ident = lambda j: j
    out_shape = [jax.ShapeDtypeStruct((s, w), dt) for w, dt in outs]
    out_specs = [pl.BlockSpec((t, w), lambda j: (j, 0)) for w, _ in outs]
    if save_carries:
        out_shape += [jax.ShapeDtypeStruct((n,) + tuple(cs), F32) for cs in carry_shapes]
        out_specs += [pl.BlockSpec((1,) + tuple(cs), lambda j: (j, 0, 0)) for cs in carry_shapes]
    if final_carries:
        out_shape += [jax.ShapeDtypeStruct(tuple(cs), F32) for cs in carry_shapes]
        out_specs += [pl.BlockSpec(tuple(cs), lambda j: (0, 0)) for cs in carry_shapes]
    res = pl.pallas_call(
        body, name=name, grid=(n,),
        in_specs=_x_specs(xs, t, ident) + _halo_specs(xs, halo_idx, t, ident) + [_full_spec(p) for p in params],
        out_specs=out_specs, out_shape=out_shape,
        scratch_shapes=[pltpu.VMEM(tuple(cs), F32) for cs in carry_shapes],
        compiler_params=_cparams("arbitrary"),
    )(*[x[0] for x in xs], *[xs[xi][0] for xi in halo_idx], *params)
    return res[:no], res[no:no + ns], res[no + ns:]


def _chunk_bwd(f, name, t, xs, params, dys, dx_dtypes, halo_idx=(), saved=(), carry_seed=None):
    s = xs[0][0].shape[0]
    n = s // t
    nx, nh, npar, nc, ndy = len(xs), len(halo_idx), len(params), len(saved), len(dys)

    def body(*refs):
        x_refs, refs = refs[:nx], refs[nx:]
        h_refs, refs = refs[:nh], refs[nh:]
        p_refs, refs = refs[:npar], refs[npar:]
        s_refs, refs = refs[:nc], refs[nc:]
        dy_refs, refs = refs[:ndy], refs[ndy:]
        dx_refs, refs = refs[:nx], refs[nx:]
        dp_refs, refs = refs[:npar], refs[npar:]
        dc_refs, dh_refs = refs[:nc], refs[nc:]
        j = pl.program_id(0)
        i = n - 1 - j

        @pl.when(j == 0)
        def _():
            for dc in dc_refs:
                dc[...] = jnp.zeros_like(dc) if carry_seed is None else carry_seed(dc.shape)
            for r in dh_refs + dp_refs:
                r[...] = jnp.zeros_like(r)

        carries = tuple(s_ref[0] for s_ref in s_refs)
        halos = tuple(jnp.where(i > 0, h[...].astype(F32), 0.0) for h in h_refs)
        x_vals = tuple(x[...].astype(F32) for x in x_refs)
        p_vals = tuple(p[...] for p in p_refs)
        _, vjp = jax.vjp(f, carries, halos, x_vals, p_vals)
        d_car, d_hal, d_xs, d_par = vjp((tuple(dc[...] for dc in dc_refs), tuple(d[...].astype(F32) for d in dy_refs)))
        d_xs = list(d_xs)
        for k, xi in enumerate(halo_idx):
            w = xs[xi][2]
            d_xs[xi] = d_xs[xi] + jnp.concatenate([jnp.zeros((t - HALO, w), F32), dh_refs[k][...]], axis=0)
            dh_refs[k][...] = jnp.where(i > 0, d_hal[k], 0.0)
        for dx_ref, dx in zip(dx_refs, d_xs):
            dx_ref[...] = dx.astype(dx_ref.dtype)
        for dp_ref, dp in zip(dp_refs, d_par):
            dp_ref[...] += dp
        for dc, v in zip(dc_refs, d_car):
            dc[...] = v

    rev = lambda j: n - 1 - j
    in_specs = (_x_specs(xs, t, rev) + _halo_specs(xs, halo_idx, t, rev) + [_full_spec(p) for p in params]
                + [pl.BlockSpec((1,) + a.shape[1:], lambda j: (n - 1 - j, 0, 0)) for a in saved]
                + [pl.BlockSpec((t, d.shape[1]), lambda j: (n - 1 - j, 0)) for d in dys])
    out_shape = ([jax.ShapeDtypeStruct((s, w), dt) for (_, _, w), dt in zip(xs, dx_dtypes)]
                 + [jax.ShapeDtypeStruct(p.shape, F32) for p in params])
    out_specs = ([pl.BlockSpec((t, w), lambda j: (n - 1 - j, 0)) for _, _, w in xs] + [_full_spec(p) for p in params])
    res = pl.pallas_call(
        body, name=name, grid=(n,), in_specs=in_specs, out_specs=out_specs, out_shape=out_shape,
        scratch_shapes=([pltpu.VMEM(a.shape[1:], F32) for a in saved]
                        + [pltpu.VMEM((HALO, xs[xi][2]), F32) for xi in halo_idx]),
        compiler_params=_cparams("arbitrary"),
    )(*[x[0] for x in xs], *[xs[xi][0] for xi in halo_idx], *params, *saved, *dys)
    return res[:nx], res[nx:]


def _tile(dim, pref):
    for cand in pref:
        if dim % cand == 0:
            return cand
    return dim


def _matmul(a, b, name, ta=False, tb=False, outs=(F32,), epilogue=None, extras=(), row_params=()):
    m, k = (a.shape[1], a.shape[0]) if ta else a.shape
    n = b.shape[0] if tb else b.shape[1]
    tm = _tile(m, (1024, 1152, 512, 256, 128))
    tn = _tile(n, (1152, 1024, 512, 256, 128))
    tk = _tile(k, (1024, 1152, 512, 256, 128))
    nk = k // tk
    ne, no = len(extras) + len(row_params), len(outs)
    ca, cb = (0 if ta else 1), (1 if tb else 0)

    def body(*refs):
        a_ref, b_ref = refs[:2]
        e_refs = refs[2:2 + ne]
        o_refs = refs[2 + ne:2 + ne + no]

        acc = refs[-1]
        kk = pl.program_id(2)

        @pl.when(kk == 0)
        def _():
            acc[...] = jnp.zeros_like(acc)

        acc[...] += _dg(_bf(a_ref[...]), _bf(b_ref[...]), ca, cb)

        @pl.when(kk == nk - 1)
        def _():
            res = acc[...]
            vals = (res,) if epilogue is None else epilogue(res, *[e[...] for e in e_refs])
            for o_ref, v in zip(o_refs, vals):
                o_ref[...] = v.astype(o_ref.dtype)

    a_spec = pl.BlockSpec((tk, tm), lambda i, j, kk: (kk, i)) if ta else pl.BlockSpec((tm, tk), lambda i, j, kk: (i, kk))
    b_spec = pl.BlockSpec((tn, tk), lambda i, j, kk: (j, kk)) if tb else pl.BlockSpec((tk, tn), lambda i, j, kk: (kk, j))
    mn_spec = pl.BlockSpec((tm, tn), lambda i, j, kk: (i, j))
    res = pl.pallas_call(
        body, name=name, grid=(m // tm, n // tn, nk),
        in_specs=[a_spec, b_spec] + [mn_spec] * len(extras)
        + [pl.BlockSpec((1, tn), lambda i, j, kk: (0, j))] * len(row_params),
        out_specs=[mn_spec] * no,
        out_shape=[jax.ShapeDtypeStruct((m, n), dt) for dt in outs],
        scratch_shapes=[pltpu.VMEM((tm, tn), F32)],
        compiler_params=_cparams("parallel", "parallel", "arbitrary"),
    )(a, b, *extras, *row_params)
    return res if no > 1 else res[0]


def _elementwise_block(r, c):
    if r % 8 == 0 and r >= 8:
        return _tile(r, (256, 128, 64, 32, 16, 8)), c
    return r, _tile(c, (256, 128))


PIECE_TILE = 1024


def _piece_steps(pieces):
    steps, s0 = [], 0
    for a in pieces:
        wt = min(a.shape[1], PIECE_TILE)
        assert a.shape[1] % wt == 0
        steps.append((s0, a.shape[1] // wt, wt))
        s0 += a.shape[1] // wt
    return steps, s0


def _matmul_pieces(pieces, b, name):
    steps, n_steps = _piece_steps(pieces)
    s_rows, n = pieces[0].shape[0], b.shape[1]
    tm = _tile(s_rows, (1024, 512, 256, 128))
    tail_rows = steps[-1][2]
    full_steps = n_steps - 1 if tail_rows < PIECE_TILE else n_steps
    b_tail = b[full_steps * PIECE_TILE:]
    np_ = len(pieces)

    def body(*refs):
        a_refs, b_ref, tail_ref, o_ref, acc = refs[:np_], refs[np_], refs[np_ + 1], refs[np_ + 2], refs[np_ + 3]
        s = pl.program_id(1)

        @pl.when(s == 0)
        def _():
            acc[...] = jnp.zeros_like(acc)

        for a_ref, (s0, ns, wt) in zip(a_refs, steps):
            @pl.when((s >= s0) & (s < s0 + ns))
            def _(a_ref=a_ref, wt=wt):
                rhs = b_ref[...] if wt == PIECE_TILE else tail_ref[...]
                acc[...] += _dg(_bf(a_ref[...]), _bf(rhs), 1, 0)

        @pl.when(s == n_steps - 1)
        def _():
            o_ref[...] = acc[...]

    n_rows = s_rows // tm

    def a_index(i, s, s0, ns):
        done = s >= s0 + ns
        return jnp.minimum(i + done.astype(jnp.int32), n_rows - 1), jnp.where(done, 0, jnp.clip(s - s0, 0, ns - 1))

    a_specs = [pl.BlockSpec((tm, wt), functools.partial(a_index, s0=s0, ns=ns)) for s0, ns, wt in steps]
    return pl.pallas_call(
        body, name=name, grid=(n_rows, n_steps),
        in_specs=a_specs + [pl.BlockSpec((PIECE_TILE, n), lambda i, s: (jnp.minimum(s, full_steps - 1), 0)),
                            pl.BlockSpec(b_tail.shape, lambda i, s: (0, 0))],
        out_specs=pl.BlockSpec((tm, n), lambda i, s: (i, 0)), out_shape=jax.ShapeDtypeStruct((s_rows, n), F32),
        scratch_shapes=[pltpu.VMEM((tm, n), F32)],
        compiler_params=_cparams("parallel", "arbitrary"),
    )(*pieces, b, b_tail)


def _matmul_pieces_t(pieces, b, name):
    steps, n_steps = _piece_steps(pieces)
    s_rows, n = b.shape
    tk = _tile(s_rows, (1024, 512, 256, 128))
    nk = s_rows // tk
    total = sum(a.shape[1] for a in pieces)
    np_ = len(pieces)

    def body(*refs):
        a_refs, b_ref, o_ref, acc = refs[:np_], refs[np_], refs[np_ + 1], refs[np_ + 2]
        s, kk = pl.program_id(0), pl.program_id(1)

        @pl.when(kk == 0)
        def _():
            acc[...] = jnp.zeros_like(acc)

        for a_ref, (s0, ns, wt) in zip(a_refs, steps):
            @pl.when((s >= s0) & (s < s0 + ns))
            def _(a_ref=a_ref, wt=wt):
                acc[0:wt, :] += _dg(_bf(a_ref[...]), _bf(b_ref[...]), 0, 0)

        @pl.when(kk == nk - 1)
        def _():
            o_ref[...] = acc[...]

    def a_index(s, kk, s0, ns):
        active = (s >= s0) & (s < s0 + ns)
        return jnp.where(active, kk, jnp.where(s < s0, 0, nk - 1)), jnp.clip(s - s0, 0, ns - 1)

    a_specs = [pl.BlockSpec((tk, wt), functools.partial(a_index, s0=s0, ns=ns)) for s0, ns, wt in steps]
    return pl.pallas_call(
        body, name=name, grid=(n_steps, nk),
        in_specs=a_specs + [pl.BlockSpec((tk, n), lambda s, kk: (kk, 0))],
        out_specs=pl.BlockSpec((PIECE_TILE, n), lambda s, kk: (s, 0)), out_shape=jax.ShapeDtypeStruct((total, n), F32),
        scratch_shapes=[pltpu.VMEM((PIECE_TILE, n), F32)],
        compiler_params=_cparams("parallel", "arbitrary"),
    )(*pieces, b)


def _adamw_math(g, w, m, v):
    m_new = ADAM_B1 * m + (1.0 - ADAM_B1) * g
    v_new = ADAM_B2 * v + (1.0 - ADAM_B2) * jnp.square(g)
    m_hat = m_new / (1.0 - ADAM_B1 ** ADAM_STEP)
    v_hat = v_new / (1.0 - ADAM_B2 ** ADAM_STEP)
    return -ADAM_LR * (m_hat / (jnp.sqrt(v_hat) + ADAM_EPS) + ADAM_WD * w), m_new, v_new


def _adamw(parts, w, m, v, name):
    nl, r, c = w.shape
    k = parts[0].shape[0]
    tr = _tile(r, (128, 64, 32, 16, 8))
    nb = r // tr

    def body(*refs):
        p_refs, (w_ref, m_ref, v_ref), outs = refs[:nl], refs[nl:nl + 3], refs[nl + 3:]
        layer = pl.program_id(0)
        for q in range(nl):
            @pl.when(layer == q)
            def _(q=q):
                g = p_refs[q][0]
                for j in range(1, k):
                    g = g + p_refs[q][j]
                vals = (g,) + _adamw_math(g, w_ref[0], m_ref[0], v_ref[0])
                for o_ref, val in zip(outs, vals):
                    o_ref[0] = val

    spec = pl.BlockSpec((1, tr, c), lambda l, i: (l, i, 0))
    part_specs = [pl.BlockSpec((k, tr, c), functools.partial(
        lambda l, i, q: (0, jnp.where(l == q, i, jnp.where(l < q, 0, nb - 1)), 0), q=q)) for q in range(nl)]
    return pl.pallas_call(
        body, name=name, grid=(nl, nb), in_specs=part_specs + [spec] * 3,
        out_specs=[spec] * 4, out_shape=[jax.ShapeDtypeStruct((nl, r, c), F32)] * 4,
        compiler_params=_cparams("arbitrary", "arbitrary"),
    )(*parts, w, m, v)


def _adamw_transposed(grads, w, m, v, name):
    nl, r, c = w.shape
    tc = 64
    views = [jnp.transpose(a, (2, 0, 1)) for a in (w, m, v)]

    def body(*refs):
        g_refs, (w_ref, m_ref, v_ref), outs = refs[:nl], refs[nl:nl + 3], refs[nl + 3:]
        for l in range(nl):
            g = g_refs[l][...]
            vals = (g,) + _adamw_math(g, w_ref[:, l, :], m_ref[:, l, :], v_ref[:, l, :])
            for o_ref, val in zip(outs, vals):
                o_ref[:, l, :] = val

    spec = pl.BlockSpec((tc, nl, r), lambda i: (i, 0, 0))
    res = pl.pallas_call(
        body, name=name, grid=(pl.cdiv(c, tc),), in_specs=[pl.BlockSpec((tc, r), lambda i: (i, 0))] * nl + [spec] * 3,
        out_specs=[spec] * 4, out_shape=[jax.ShapeDtypeStruct((c, nl, r), F32)] * 4,
        compiler_params=_cparams("parallel"),
    )(*grads, *views)
    return [jnp.transpose(a, (1, 2, 0)) for a in res]


def _adamw_small(gathered, ws, ms, vs, loss_terms, name):
    n = len(ws)

    def device_sum(ref):
        s = ref[0]
        for j in range(1, N_DEV):
            s = s + ref[j]
        return s

    def body(*refs):
        g_refs, w_refs, m_refs, v_refs = refs[:n], refs[n:2 * n], refs[2 * n:3 * n], refs[3 * n:4 * n]
        loss_ref, outs, loss_out = refs[4 * n], refs[4 * n + 1:-1], refs[-1]
        for i in range(n):
            g = device_sum(g_refs[i])
            vals = (g,) + _adamw_math(g, w_refs[i][...], m_refs[i][...], v_refs[i][...])
            for kind, val in enumerate(vals):
                outs[kind * n + i][...] = val
        loss_out[...] = device_sum(loss_ref)

    res = pl.pallas_call(
        body, name=name,
        out_shape=[jax.ShapeDtypeStruct(a.shape, F32) for _ in range(4) for a in ws] + [
            jax.ShapeDtypeStruct(loss_terms.shape[1:], F32)],
        compiler_params=pltpu.CompilerParams(vmem_limit_bytes=VMEM_LIMIT),
    )(*gathered, *ws, *ms, *vs, loss_terms)
    return [res[kind * n:(kind + 1) * n] for kind in range(4)], res[-1]


ANY = pl.BlockSpec(memory_space=pl.ANY)


def _place():
    return lax.axis_index("x"), lax.axis_index("y"), lax.axis_index("c")


def _handshake(peers):
    barrier = pltpu.get_barrier_semaphore()
    for peer in peers:
        pl.semaphore_signal(barrier, inc=1, device_id=peer, device_id_type=MESH)
    pl.semaphore_wait(barrier, len(peers))


def _comm_call(body, name, inputs, out_shape, scratch, sequencer_id=None, after=()):
    if sequencer_id is None:
        return pl.pallas_call(body, name=name, out_shape=out_shape, in_specs=[ANY] * len(inputs),
                              out_specs=[ANY] * len(out_shape), scratch_shapes=scratch)(*inputs)
    n_in, n_after = len(inputs), len(after)

    def sequencer_body(*refs):
        body(*refs[:n_in], *refs[n_in + n_after:])

    return pl.kernel(
        sequencer_body, out_type=out_shape, mesh=plsc.ScalarSubcoreMesh(axis_name="sequencer", num_cores=1),
        scratch_types=scratch, compiler_params=pltpu.CompilerParams(collective_id=sequencer_id), name=name,
    )(*inputs, *after)


def _all_gather(blocks, name, sequencer_id=None, after=()):
    n = len(blocks)

    def body(*refs):
        x_refs, out_refs = refs[:n], refs[n:2 * n]
        send_sems, recv_sems, local_sems = refs[2 * n:]
        x, y, c = _place()
        me, sibling = (x, y, c), (x, y, 1 - c)
        chips = [(1 - x, y), (x, 1 - y), (1 - x, 1 - y)]
        if sequencer_id is not None:
            _handshake([sibling] + [(*chip, c) for chip in chips])

        def slot(a, px, py, pc):
            return out_refs[a].at[4 * px + 2 * py + pc]

        def copy(a, k, blk, to, src=None):
            return pltpu.make_async_remote_copy(
                src_ref=slot(a, *blk) if src is None else src, dst_ref=slot(a, *blk),
                send_sem=send_sems.at[7 * a + k], recv_sem=recv_sems.at[7 * a + k], device_id=to, device_id_type=MESH)

        mine = [pltpu.make_async_copy(x_refs[a], slot(a, *me), local_sems.at[a]) for a in range(n)]
        first = []
        for a in range(n):
            mine[a].start()
            first.append(copy(a, 0, me, sibling, src=x_refs[a]))
            first += [copy(a, 1 + j, me, (*chip, c), src=x_refs[a]) for j, chip in enumerate(chips)]
        for cp in first:
            cp.start()
        passed = []
        for j, chip in enumerate(chips):
            for a in range(n):
                copy(a, 1 + j, (*chip, c), me).wait_recv()
                passed.append(copy(a, 4 + j, (*chip, c), sibling))
                passed[-1].start()
        for a in range(n):
            copy(a, 0, sibling, me).wait_recv()
            for j, chip in enumerate(chips):
                copy(a, 4 + j, (*chip, 1 - c), me).wait_recv()
        for cp in first + passed:
            cp.wait_send()
        for cp in mine:
            cp.wait()

    return _comm_call(
        body, name, blocks, [jax.ShapeDtypeStruct((N_DEV,) + b.shape, b.dtype) for b in blocks],
        [pltpu.SemaphoreType.DMA((7 * n,)), pltpu.SemaphoreType.DMA((7 * n,)), pltpu.SemaphoreType.DMA((n,))],
        sequencer_id, after)


def _exchange_sibling(gs, name, sequencer_id=None, after=()):
    n = len(gs)

    def body(*refs):
        g_refs, out_refs = refs[:n], refs[n:2 * n]
        send_sems, recv_sems = refs[2 * n:]
        x, y, c = _place()
        if sequencer_id is not None:
            _handshake([(x, y, 1 - c)])
        copies = [pltpu.make_async_remote_copy(
            src_ref=g_refs[a].at[2 * k + 1 - c], dst_ref=out_refs[a].at[k], send_sem=send_sems.at[4 * a + k],
            recv_sem=recv_sems.at[4 * a + k], device_id=(x, y, 1 - c), device_id_type=MESH)
            for a in range(n) for k in range(4)]
        for cp in copies:
            cp.start()
        for cp in copies:
            cp.wait()

    return _comm_call(body, name, gs, [jax.ShapeDtypeStruct((4,) + g.shape[1:], g.dtype) for g in gs],
                      [pltpu.SemaphoreType.DMA((4 * n,)), pltpu.SemaphoreType.DMA((4 * n,))], sequencer_id, after)


def _other_chips():
    x, y = lax.axis_index("x"), lax.axis_index("y")
    return jnp.stack([2 * (1 - x) + y, 2 * x + 1 - y, 2 * (1 - x) + 1 - y]).astype(jnp.int32)


def _add_sibling(gs, r1s, name, after=()):
    n = len(gs)
    _, r, w = gs[0].shape
    tr, tc = _elementwise_block(r, w)
    chips = _other_chips()
    slabs = 2 * chips + lax.axis_index("c").astype(jnp.int32)

    def body(slab_ref, chip_ref, *refs):
        g_refs, r_refs, o_refs = refs[:n], refs[n:2 * n], refs[2 * n + len(after):]
        for g_ref, r_ref, o_ref in zip(g_refs, r_refs, o_refs):
            o_ref[...] = (g_ref[...] + r_ref[...]).astype(BF16)

    return pl.pallas_call(
        body, name=name, out_shape=[jax.ShapeDtypeStruct((3, r, w), BF16)] * n,
        grid_spec=pltpu.PrefetchScalarGridSpec(
            num_scalar_prefetch=2, grid=(3, r // tr, w // tc),
            in_specs=[pl.BlockSpec((1, tr, tc), lambda k, i, j, slab_ref, chip_ref: (slab_ref[k], i, j))] * n
            + [pl.BlockSpec((1, tr, tc), lambda k, i, j, slab_ref, chip_ref: (chip_ref[k], i, j))] * n
            + [ANY] * len(after),
            out_specs=[pl.BlockSpec((1, tr, tc), lambda k, i, j, slab_ref, chip_ref: (k, i, j))] * n),
        compiler_params=_cparams("parallel", "parallel", "parallel"),
    )(slabs, chips, *gs, *r1s, *after)


def _exchange_chips(ps, name, sequencer_id=None, after=()):
    n = len(ps)

    def body(*refs):
        p_refs, out_refs = refs[:n], refs[n:2 * n]
        send_sems, recv_sems = refs[2 * n:]
        x, y, c = _place()
        chips = [(1 - x, y), (x, 1 - y), (1 - x, 1 - y)]
        if sequencer_id is not None:
            _handshake([(*chip, c) for chip in chips])
        copies = [pltpu.make_async_remote_copy(
            src_ref=p_refs[a].at[j], dst_ref=out_refs[a].at[j], send_sem=send_sems.at[3 * a + j],
            recv_sem=recv_sems.at[3 * a + j], device_id=(px, py, c), device_id_type=MESH)
            for a in range(n) for j, (px, py) in enumerate(chips)]
        for cp in copies:
            cp.start()
        for cp in copies:
            cp.wait()

    return _comm_call(body, name, ps, [jax.ShapeDtypeStruct(p.shape, p.dtype) for p in ps],
                      [pltpu.SemaphoreType.DMA((3 * n,)), pltpu.SemaphoreType.DMA((3 * n,))], sequencer_id, after)


def _sum_chips(gs, r1s, others, name):
    n = len(gs)
    _, r, c = gs[0].shape
    tr, tc = _elementwise_block(r, c)
    chip = 2 * lax.axis_index("x") + lax.axis_index("y")
    place = jnp.stack([2 * chip + lax.axis_index("c"), chip]).astype(jnp.int32)

    def body(place_ref, *refs):
        for g_ref, r_ref, others_ref, o_ref in zip(refs[:n], refs[n:2 * n], refs[2 * n:3 * n], refs[3 * n:]):
            s = g_ref[0] + r_ref[0]
            for j in range(3):
                s = s + others_ref[j].astype(F32)
            o_ref[...] = s

    return pl.pallas_call(
        body, name=name, out_shape=[jax.ShapeDtypeStruct((r, c), F32)] * n,
        grid_spec=pltpu.PrefetchScalarGridSpec(
            num_scalar_prefetch=1, grid=(r // tr, c // tc),
            in_specs=[pl.BlockSpec((1, tr, tc), lambda i, j, place_ref: (place_ref[0], i, j))] * n
            + [pl.BlockSpec((1, tr, tc), lambda i, j, place_ref: (place_ref[1], i, j))] * n
            + [pl.BlockSpec((3, tr, tc), lambda i, j, place_ref: (0, i, j))] * n,
            out_specs=[pl.BlockSpec((tr, tc), lambda i, j, place_ref: (i, j))] * n),
        compiler_params=_cparams("parallel", "parallel"),
    )(place, *gs, *r1s, *others)


def _reorder_in_proj(wt):
    za_zb, zc, xbc, dt, gates = (wt[:4096], wt[4096:5120], wt[5120:7168], wt[7168:7184], wt[7184:])
    return jnp.concatenate([za_zb, xbc, gates, zc, dt, jnp.zeros((DT_PAD - 16, wt.shape[1]), wt.dtype)], axis=0)


def _restore_in_proj(wt):
    return jnp.concatenate([wt[:4096], wt[OFF_ZC:OFF_ZC + W_ZC], wt[OFF_XBC:OFF_XBC + W_XBC],
                            wt[OFF_DT:OFF_DT + 16], wt[OFF_GATE:OFF_GATE + W_GATE]], axis=0)


def _lanes_from_devices(g):
    return jnp.moveaxis(g, 0, 1).reshape(g.shape[1], N_DEV * g.shape[2])


def _lanes_to_devices(a):
    return jnp.moveaxis(a.reshape(a.shape[0], N_DEV, a.shape[1] // N_DEV), 1, 0)


def _pad_lanes(a, width):
    return jnp.pad(a, ((0, 0), (0, width - a.shape[1])))


BIG = ("w_in", "w_branch_a", "w_branch_b", "w_branch_c", "w_out", "w_mlp_up", "w_mlp_down")
SMALL_SHARDED = ("b_gate", "lru_conv_w", "ssd_conv_w")
REPLICATED = ("norm_mix_g", "gmlp_ln_g", "gmlp_ln_b", "gmlp_w_s", "gmlp_b_s", "lru_conv_b", "lru_w_r", "lru_b_r",
              "lru_w_i", "lru_b_i", "lru_lambda", "ssd_conv_b", "ssd_dt_bias", "ssd_a_log", "ssd_d", "ssd_norm_g",
              "norm_mlp_g", "final_norm_g")
WEIGHTS = ("norm_mix_g", "w_in", "b_gate", "gmlp_ln_g", "gmlp_ln_b", "gmlp_w_s", "gmlp_b_s", "lru_conv_w", "lru_conv_b",
           "lru_w_r", "lru_b_r", "lru_w_i", "lru_b_i", "lru_lambda", "ssd_conv_w", "ssd_conv_b", "ssd_dt_bias",
           "ssd_a_log", "ssd_d", "ssd_norm_g", "w_branch_a", "w_branch_b", "w_branch_c", "w_out", "norm_mlp_g",
           "w_mlp_up", "w_mlp_down", "final_norm_g")
TRANSPOSED = ("w_in", "w_mlp_up")
SMALL_MATRICES = ("gmlp_w_s", "lru_w_r", "lru_w_i")
SMALL_VECTORS = tuple(n for n in REPLICATED if n not in SMALL_MATRICES)
GRADIENT_GROUPS = {"mlp": ("w_mlp_up", "w_mlp_down"), "mix": ("w_branch_a", "w_branch_b", "w_branch_c", "w_out"),
                   "in": ("w_in",)}


def _layer_params(full, l):
    row = lambda a: a.reshape(1, -1)
    return dict(
        norm_mix_g=row(full["norm_mix_g"][l]), norm_mlp_g=row(full["norm_mlp_g"][l]),
        gmlp=(row(full["gmlp_ln_g"][l]), row(full["gmlp_ln_b"][l]), full["gmlp_w_s"][l].reshape(GROUPS_A * CHUNK, CHUNK),
              full["gmlp_b_s"][l].T),
        lru=(full["lru_conv_w"][l], row(full["lru_conv_b"][l]), full["lru_w_r"][l].reshape(D, D // HEADS_B),
             row(full["lru_b_r"][l]), full["lru_w_i"][l].reshape(D, D // HEADS_B), row(full["lru_b_i"][l]),
             row(full["lru_lambda"][l])),
        ssd=(full["ssd_conv_w"][l], row(full["ssd_conv_b"][l]), _pad_lanes(row(full["ssd_dt_bias"][l]), DT_PAD),
             _pad_lanes(row(full["ssd_a_log"][l]), DT_PAD), _pad_lanes(row(full["ssd_d"][l]), DT_PAD),
             row(full["ssd_norm_g"][l])),
        b_gate=full["b_gate"][l],
    )


def _forward_layer(h, hn, p, wb, l, after_mixers=None, next_norm_g=None):
    tag = f"l{l}"
    t_row = 512
    if hn is None:
        (hn,), _, _ = _chunk_fwd(_f_rmsnorm, f"norm_mix_{tag}", t_row, [(h, 0, D)], [p["norm_mix_g"]], [(D, BF16)])
    proj = _matmul(hn, wb["w_in"], f"in_proj_{tag}", tb=True)
    (ya,), _, _ = _chunk_fwd(_f_gmlp, f"gmlp_{tag}", CHUNK, [(proj, OFF_ZA, W_ZA)], p["gmlp"], [(D, BF16)])
    lru_xs = [(proj, OFF_ZB, D), (proj, OFF_ZB + D, D)]
    (yb,), lru_saved, _ = _chunk_fwd(_f_lru, f"lru_{tag}", CHUNK, lru_xs, p["lru"], [(D, BF16)], halo_idx=(0,),
                                     carry_shapes=[(1, D)], save_carries=True)
    ssd_xs = [(proj, OFF_ZC, W_ZC), (proj, OFF_XBC, W_XBC), (proj, OFF_DT, W_DT)]
    (yc,), ssd_saved, _ = _chunk_fwd(_f_ssd, f"ssd_{tag}", CHUNK, ssd_xs, p["ssd"], [(D, BF16)], halo_idx=(1,),
                                     carry_shapes=[(HEADS_C * HEAD_DIM_C, STATE_C)], save_carries=True)
    if after_mixers is not None:
        after_mixers(yc)
    pa = _matmul(ya, wb["w_branch_a"], f"branch_a_{tag}")
    pb = _matmul(yb, wb["w_branch_b"], f"branch_b_{tag}")
    pc = _matmul(yc, wb["w_branch_c"], f"branch_c_{tag}")
    merge_xs = [(pa, 0, D), (pb, 0, D), (pc, 0, D), (proj, OFF_GATE, W_GATE)]
    (merged,), _, _ = _chunk_fwd(_f_merge, f"merge_{tag}", t_row, merge_xs, [p["b_gate"]], [(D, BF16)])

    def add_and_norm(acc, res, g):
        s = acc + res
        return s, _rms(s, g)

    h_mid, hn2 = _matmul(merged, wb["w_out"], f"out_proj_{tag}", outs=(F32, BF16), epilogue=add_and_norm, extras=(h,),
                         row_params=(p["norm_mlp_g"],))

    def relu_sq(acc):
        r = jnp.maximum(acc, 0.0)
        return r, r * r

    relu_up, act = _matmul(hn2, wb["w_mlp_up"], f"mlp_up_{tag}", tb=True, outs=(F32, BF16), epilogue=relu_sq)
    if next_norm_g is None:
        hn_out = None
        h_out = _matmul(act, wb["w_mlp_down"], f"mlp_down_{tag}", epilogue=lambda acc, res: (acc + res,), extras=(h_mid,))
    else:
        h_out, hn_out = _matmul(act, wb["w_mlp_down"], f"mlp_down_{tag}", outs=(F32, BF16), epilogue=add_and_norm,
                                extras=(h_mid,), row_params=(next_norm_g,))
    saved = dict(h=h, hn=hn, proj=proj, ya=ya, yb=yb, yc=yc, lru_saved=lru_saved, ssd_saved=ssd_saved, pa=pa, pb=pb,
                 pc=pc, merged=merged, h_mid=h_mid, hn2=hn2, relu_up=relu_up, act=act, lru_xs=lru_xs, ssd_xs=ssd_xs,
                 merge_xs=merge_xs)
    return h_out, hn_out, saved


def _backward_layer(dh, sv, p, wb, l):
    tag = f"l{l}"
    t_row = 512
    g = {}
    d_up = _matmul(dh, wb["w_mlp_down"], f"d_act_{tag}", tb=True, outs=(BF16,),
                   epilogue=lambda acc, r: (acc * (2.0 * r),), extras=(sv["relu_up"],))
    g["w_mlp_down"] = _matmul(sv["act"], dh, f"dw_mlp_down_{tag}", ta=True)
    g["w_mlp_up"] = _matmul(d_up, sv["hn2"], f"dw_mlp_up_{tag}", ta=True)
    d_hn2 = _matmul(d_up, wb["w_mlp_up"], f"d_hn2_{tag}")
    (d_mid,), (g["norm_mlp_g"],) = _chunk_bwd(_f_rmsnorm_res, f"norm_mlp_bwd_{tag}", t_row, [(sv["h_mid"], 0, D)],
                                              [p["norm_mlp_g"]], [d_hn2, dh], [F32])
    d_merged = _matmul(d_mid, wb["w_out"], f"d_merged_{tag}", tb=True)
    g["w_out"] = _matmul(sv["merged"], d_mid, f"dw_out_{tag}", ta=True)
    (d_pa, d_pb, d_pc, d_gate), (g["b_gate"],) = _chunk_bwd(
        _f_merge, f"merge_bwd_{tag}", t_row, sv["merge_xs"], [p["b_gate"]], [d_merged], [BF16] * 4)
    d_y = {}
    for br, d_p, y in (("a", d_pa, sv["ya"]), ("b", d_pb, sv["yb"]), ("c", d_pc, sv["yc"])):
        g[f"w_branch_{br}"] = _matmul(y, d_p, f"dw_branch_{br}_{tag}", ta=True)
        d_y[br] = _matmul(d_p, wb[f"w_branch_{br}"], f"d_y{br}_{tag}", tb=True)
    (d_za,), g_gmlp = _chunk_bwd(_f_gmlp, f"gmlp_bwd_{tag}", CHUNK, [(sv["proj"], OFF_ZA, W_ZA)], p["gmlp"],
                                 [d_y["a"]], [BF16])
    (d_xb, d_gt), g_lru = _chunk_bwd(_f_lru, f"lru_bwd_{tag}", CHUNK, sv["lru_xs"], p["lru"], [d_y["b"]], [BF16] * 2,
                                     halo_idx=(0,), saved=sv["lru_saved"])
    (d_zc, d_xbc, d_dt), g_ssd = _chunk_bwd(_f_ssd, f"ssd_bwd_{tag}", CHUNK, sv["ssd_xs"], p["ssd"], [d_y["c"]],
                                            [BF16] * 3, halo_idx=(1,), saved=sv["ssd_saved"])
    d_proj = [d_za, d_xb, d_gt, d_xbc, d_gate, d_zc, d_dt]
    g["w_in"] = _matmul_pieces_t(d_proj, sv["hn"], f"dw_in_{tag}")
    d_hn = _matmul_pieces(d_proj, wb["w_in"], f"d_hn_{tag}")
    (d_h,), (g["norm_mix_g"],) = _chunk_bwd(_f_rmsnorm_res, f"norm_mix_bwd_{tag}", t_row, [(sv["h"], 0, D)],
                                            [p["norm_mix_g"]], [d_hn, d_mid], [F32])
    g["w_in"] = _restore_in_proj(g["w_in"])
    for n in BIG:
        g[n] = g[n].reshape(N_DEV, g[n].shape[0] // N_DEV, g[n].shape[1])
    g["gmlp_ln_g"], g["gmlp_ln_b"], g["gmlp_w_s"] = g_gmlp[:3]
    g["gmlp_b_s"] = g_gmlp[3].T
    (g["lru_conv_w"], g["lru_conv_b"], g["lru_w_r"], g["lru_b_r"], g["lru_w_i"], g["lru_b_i"], g["lru_lambda"]) = g_lru
    g["ssd_conv_w"], g["ssd_conv_b"] = g_ssd[:2]
    g["ssd_dt_bias"], g["ssd_a_log"], g["ssd_d"] = (a[:, :HEADS_C] for a in g_ssd[2:5])
    g["ssd_norm_g"] = g_ssd[5]
    g["later"] = {"mlp": d_merged, "mix": d_za, "in": d_h}
    return d_h, g


LOSS_ROWS = 512


def _loss_and_grads(h, target, full, layer_weights, first_gathered):
    seq = h.shape[0]
    layer_p = [_layer_params(full, l) for l in range(DEPTH)]
    layer_w = [layer_weights(0, (first_gathered,))]
    saved = []
    hn = None
    for l in range(DEPTH):
        fetch_next = next_norm_g = None
        if l + 1 < DEPTH:
            fetch_next = lambda y, l=l: layer_w.append(layer_weights(l + 1, (layer_w[l]["w_mlp_down"], y)))
            next_norm_g = layer_p[l + 1]["norm_mix_g"]
        h, hn, sv = _forward_layer(h, hn, layer_p[l], layer_w[l], l, fetch_next, next_norm_g)
        saved.append(sv)
    final_g = full["final_norm_g"].reshape(1, D)
    loss_xs = [(h, 0, D), (target, 0, D)]
    t_loss = min(LOSS_ROWS, seq)
    _, _, (loss_acc,) = _chunk_fwd(_f_loss, "loss", t_loss, loss_xs, [final_g], [], carry_shapes=[(1, 128)],
                                   final_carries=True)
    zero_acc = jnp.zeros((seq // t_loss, 1, 128), F32)
    seed = lambda shape: (lax.broadcasted_iota(jnp.int32, shape, 1) == 0).astype(F32)
    (dh, _), (g_final,) = _chunk_bwd(_f_loss, "loss_bwd", t_loss, loss_xs, [final_g], [], [F32, F32], saved=[zero_acc],
                                     carry_seed=seed)
    layer_g = [None] * DEPTH
    for l in reversed(range(DEPTH)):
        dh, layer_g[l] = _backward_layer(dh, saved[l], layer_p[l], layer_w[l], l)
    return loss_acc, dh, layer_g, g_final


def _small_views(d):
    views = {n: d[n] for n in REPLICATED}
    views["gmlp_b_s"] = d["gmlp_b_s"].reshape(DEPTH * GROUPS_A, CHUNK)
    views["final_norm_g"] = d["final_norm_g"].reshape(1, D)
    for n in SMALL_MATRICES:
        views[n] = d[n].reshape(DEPTH * D, D // HEADS_B)
    return views


def kernel(x, norm_mix_g, w_in, b_gate, gmlp_ln_g, gmlp_ln_b, gmlp_w_s, gmlp_b_s, lru_conv_w, lru_conv_b, lru_w_r, lru_b_r, lru_w_i, lru_b_i, lru_lambda, ssd_conv_w, ssd_conv_b, ssd_dt_bias, ssd_a_log, ssd_d, ssd_norm_g, w_branch_a, w_branch_b, w_branch_c, w_out, norm_mlp_g, w_mlp_up, w_mlp_down, final_norm_g, loss_target, m_norm_mix_g, m_w_in, m_b_gate, m_gmlp_ln_g, m_gmlp_ln_b, m_gmlp_w_s, m_gmlp_b_s, m_lru_conv_w, m_lru_conv_b, m_lru_w_r, m_lru_b_r, m_lru_w_i, m_lru_b_i, m_lru_lambda, m_ssd_conv_w, m_ssd_conv_b, m_ssd_dt_bias, m_ssd_a_log, m_ssd_d, m_ssd_norm_g, m_w_branch_a, m_w_branch_b, m_w_branch_c, m_w_out, m_norm_mlp_g, m_w_mlp_up, m_w_mlp_down, m_final_norm_g, v_norm_mix_g, v_w_in, v_b_gate, v_gmlp_ln_g, v_gmlp_ln_b, v_gmlp_w_s, v_gmlp_b_s, v_lru_conv_w, v_lru_conv_b, v_lru_w_r, v_lru_b_r, v_lru_w_i, v_lru_b_i, v_lru_lambda, v_ssd_conv_w, v_ssd_conv_b, v_ssd_dt_bias, v_ssd_a_log, v_ssd_d, v_ssd_norm_g, v_w_branch_a, v_w_branch_b, v_w_branch_c, v_w_out, v_norm_mlp_g, v_w_mlp_up, v_w_mlp_down, v_final_norm_g):
    args = locals()
    w = {n: args[n] for n in WEIGHTS}
    m = {n: args["m_" + n] for n in WEIGHTS}
    v = {n: args["v_" + n] for n in WEIGHTS}
    seq = x.shape[1]
    h = x.reshape(seq, D)
    target = loss_target.reshape(seq, D)

    def shard_on_wire(n, l):
        return (w[n][l].T if n in TRANSPOSED else w[n][l]).astype(BF16)

    first = _all_gather([shard_on_wire("w_in", 0)] + [w[n] for n in SMALL_SHARDED], "gather_weights_first")
    full = {n: w[n] for n in REPLICATED}
    for n, g in zip(SMALL_SHARDED, first[1:]):
        full[n] = jnp.stack([_lanes_from_devices(g[:, l]) for l in range(DEPTH)])

    def layer_weights(l, after):
        have = {"w_in": first[0]} if l == 0 else {}
        names = [n for n in BIG if n not in have]
        later = _all_gather([shard_on_wire(n, l) for n in names], f"gather_weights_l{l}", SEQ_GATHER + l, after=after)
        have.update(zip(names, later))
        wl = {n: have[n].reshape(-1, D) for n in BIG}
        wl["w_in"] = _reorder_in_proj(wl["w_in"])
        return wl

    loss_local, dh, layer_g, g_final = _loss_and_grads(h, target, full, layer_weights, first[0])
    grad_x = dh.reshape(x.shape)
    out = {}
    kinds = ("grad", "delta", "new_m", "new_v")

    sequencer_before = {}

    def reduce_scatter(slabs, tag, on_sequencer, later=()):
        keys = list(slabs)
        ids = (SEQ_TO_SIBLING, SEQ_TO_CHIPS) if on_sequencer else (None, None)
        from_sibling = _exchange_sibling([slabs[k] for k in keys], f"grads_to_sibling_{tag}", ids[0],
                                         sequencer_before.get("sibling", ()))
        from_sibling = dict(zip(keys, from_sibling))
        same_shape = {}
        for k in keys:
            same_shape.setdefault(slabs[k].shape[1:], []).append(k)
        chip_sums = {}
        for (r, c), ks in same_shape.items():
            sums = _add_sibling([slabs[k] for k in ks], [from_sibling[k] for k in ks], f"add_sibling_{tag}_{r}x{c}", later)
            chip_sums.update(zip(ks, sums))
        from_chips = _exchange_chips([chip_sums[k] for k in keys], f"grads_to_chips_{tag}", ids[1],
                                     sequencer_before.get("chips", ()))
        if on_sequencer:
            sequencer_before["sibling"], sequencer_before["chips"] = (from_sibling[keys[0]],), (from_chips[0],)
        from_chips = dict(zip(keys, from_chips))
        summed = {}
        for (r, c), ks in same_shape.items():
            sums = _sum_chips([slabs[k] for k in ks], [from_sibling[k] for k in ks], [from_chips[k] for k in ks],
                              f"sum_chips_{tag}_{r}x{c}")
            summed.update(zip(ks, sums))
        return summed

    small = {}
    for n in SMALL_SHARDED:
        small[n, None] = jnp.concatenate([_lanes_to_devices(layer_g[l][n]) for l in range(DEPTH)], axis=1)
    g_small = {n: jnp.concatenate([layer_g[l][n] for l in range(DEPTH)], axis=0) for n in REPLICATED[:-1]}
    g_small["final_norm_g"] = g_final
    for n in SMALL_MATRICES:
        small[n, None] = g_small[n].reshape(N_DEV, -1, g_small[n].shape[-1])
    reduced = {}
    groups = [(l, grp) for l in range(DEPTH - 1, -1, -1) for grp in GRADIENT_GROUPS]
    for l, grp in groups:
        if (l, grp) == groups[-1]:
            reduced.update(reduce_scatter(small, "small", True, (layer_g[l][GRADIENT_GROUPS[grp][0]],)))
        slabs = {(n, l): layer_g[l][n] for n in GRADIENT_GROUPS[grp]}
        reduced.update(reduce_scatter(slabs, f"{grp}_l{l}", True, (layer_g[l]["later"][grp],)))
    for n in BIG:
        grads = [reduced[n, l] for l in range(DEPTH)]
        if n in TRANSPOSED and w[n].shape[-1] % LANES:
            res = _adamw_transposed(grads, w[n], m[n], v[n], f"adamw_{n}")
        else:
            res = _adamw([(g.T if n in TRANSPOSED else g)[None] for g in grads], w[n], m[n], v[n], f"adamw_{n}")
        for kind, a in zip(kinds, res):
            out[kind, n] = a
    for n in SMALL_SHARDED:
        one = lambda a: a.reshape((1, -1, a.shape[-1]))
        for kind, a in zip(kinds, _adamw([reduced[n, None][None]], one(w[n]), one(m[n]), one(v[n]), f"adamw_{n}")):
            out[kind, n] = a.reshape(w[n].shape)

    to_gather = [reduced[n, None] if n in SMALL_MATRICES else g_small[n] for n in REPLICATED]
    *g_gathered, loss_terms = _all_gather(to_gather + [loss_local], "gather_small_grads", SEQ_GATHER_SMALL)
    g_all = dict(zip(REPLICATED, g_gathered))
    wv, mv, vv = _small_views(w), _small_views(m), _small_views(v)
    res, loss_sum = _adamw_small([g_all[n] for n in SMALL_VECTORS],
                                 *[[d[n] for n in SMALL_VECTORS] for d in (wv, mv, vv)], loss_terms, "adamw_vectors")
    loss = loss_sum[0, 0]
    for kind, arrays in zip(kinds, res):
        for n, a in zip(SMALL_VECTORS, arrays):
            out[kind, n] = a.reshape(w[n].shape)
    for n in SMALL_MATRICES:
        g_full = g_all[n].reshape((1, 1) + wv[n].shape)
        for kind, a in zip(kinds, _adamw([g_full[0]], wv[n][None], mv[n][None], vv[n][None], f"adamw_{n}")):
            out[kind, n] = a.reshape(w[n].shape)

    return (loss, grad_x, *[out[kind, n] for kind in kinds for n in WEIGHTS])
```

```python
import functools

import jax
import jax.numpy as jnp
from jax import lax
from jax.experimental import pallas as pl
from jax.experimental.pallas import tpu as pltpu
from jax.experimental.pallas import tpu_sc as plsc

F32 = jnp.float32
BF16 = jnp.bfloat16
MESH = pl.DeviceIdType.MESH

D = 1024
DEPTH = 2
EPS = 1e-6
CHUNK = 128
GROUPS_A = 8
HEADS_B = 8
LRU_C = 8.0
HEADS_C = 16
HEAD_DIM_C = 64
GROUPS_C = 4
STATE_C = 128
DT_PAD = 128
OFF_ZA, W_ZA = 0, 2048
OFF_ZB, W_ZB = 2048, 2048
OFF_XBC, W_XBC = 4096, 2048
OFF_GATE, W_GATE = 6144, 3072
OFF_ZC, W_ZC = 9216, 1024
OFF_DT, W_DT = 10240, DT_PAD
D_IN_PAD = 10368
N_DEV = 8
SEQ_GATHER = 1
SEQ_TO_SIBLING = SEQ_GATHER + DEPTH
SEQ_TO_CHIPS = SEQ_TO_SIBLING + 1
SEQ_GATHER_SMALL = SEQ_TO_CHIPS + 1

ADAM_LR = 0.001
ADAM_B1 = 0.9
ADAM_B2 = 0.999
ADAM_EPS = 1e-08
ADAM_WD = 0.01
ADAM_STEP = 10

VMEM_LIMIT = 56 * 1024 * 1024
HALO = 8


def _cparams(*sem):
    return pltpu.CompilerParams(dimension_semantics=sem, vmem_limit_bytes=VMEM_LIMIT)


def _bf(x):
    return x.astype(BF16)


def _dg(a, b, ca, cb):
    return lax.dot_general(a, b, (((ca,), (cb,)), ((), ())), preferred_element_type=F32)


@functools.partial(jax.custom_vjp, nondiff_argnums=(2, 3))
def _mm(a, b, ta, tb):
    return _dg(_bf(a), _bf(b), 0 if ta else 1, 1 if tb else 0)


def _mm_fwd(a, b, ta, tb):
    return _mm(a, b, ta, tb), (a, b)


def _mm_bwd(ta, tb, res, g):
    a, b = res
    ma = 1 if ta else 0
    nb = 0 if tb else 1
    gb, ab, bb = _bf(g), _bf(a), _bf(b)
    da = _dg(bb, gb, nb, 1) if ta else _dg(gb, bb, 1, nb)
    db = _dg(gb, ab, 0, ma) if tb else _dg(ab, gb, ma, 0)
    return da.astype(a.dtype), db.astype(b.dtype)


_mm.defvjp(_mm_fwd, _mm_bwd)


def _slices(x, sizes, axis):
    out, lo = [], 0
    for size in sizes:
        out.append(lax.slice_in_dim(x, lo, lo + size, axis=axis))
        lo += size
    return tuple(out)


@functools.partial(jax.custom_vjp, nondiff_argnums=(1,))
def _split_cols(x, widths):
    return _slices(x, widths, 1)


_split_cols.defvjp(lambda x, widths: (_slices(x, widths, 1), None),
                   lambda widths, _, gs: (jnp.concatenate(gs, axis=1),))


@functools.partial(jax.custom_vjp, nondiff_argnums=(1,))
def _split_rows(x, heights):
    return _slices(x, heights, 0)


_split_rows.defvjp(lambda x, heights: (_slices(x, heights, 0), None),
                   lambda heights, _, gs: (jnp.concatenate(gs, axis=0),))


def _col(x, j):
    lane = lax.broadcasted_iota(jnp.int32, x.shape, 1)
    return jnp.sum(jnp.where(lane == j, x, 0.0), axis=1, keepdims=True)


def _row(x, i):
    r = lax.broadcasted_iota(jnp.int32, x.shape, 0)
    return jnp.sum(jnp.where(r == i, x, 0.0), axis=0, keepdims=True)


def _roll_down(x, s):
    return pltpu.roll(x, s, 0)


def _roll_up(x, s):
    return pltpu.roll(x, x.shape[0] - s, 0)


def _row_iota(x):
    return lax.broadcasted_iota(jnp.int32, x.shape, 0)


@functools.partial(jax.custom_vjp, nondiff_argnums=(2,))
def _shift_rows(halo, x, s):
    if s == 0:
        return x
    return _roll_down(jnp.concatenate([halo, x], axis=0), s)[HALO:]


def _shift_rows_fwd(halo, x, s):
    return _shift_rows(halo, x, s), None


def _shift_rows_bwd(s, _, g):
    if s == 0:
        return jnp.zeros((HALO, g.shape[1]), g.dtype), g
    ge = jnp.concatenate([jnp.zeros((HALO, g.shape[1]), g.dtype), g], axis=0)
    de = _roll_up(ge, s)
    return de[:HALO], de[HALO:]


_shift_rows.defvjp(_shift_rows_fwd, _shift_rows_bwd)


SUBLANES = 8
LANES = 128


def _scan_tiles(a, b, carry, up):
    n, c = a.shape
    nt = n // SUBLANES
    a = a.reshape(nt, SUBLANES, c)
    b = b.reshape(nt, SUBLANES, c)
    sub = lax.broadcasted_iota(jnp.int32, a.shape, 1)
    s = 1
    while s < SUBLANES:
        keep = (sub < SUBLANES - s) if up else (sub >= s)
        shift = SUBLANES - s if up else s
        a_sh = jnp.where(keep, pltpu.roll(a, shift, 1), 1.0)
        b_sh = jnp.where(keep, pltpu.roll(b, shift, 1), 0.0)
        b = a * b_sh + b
        a = a * a_sh
        s *= 2
    tiles = [None] * nt
    edge = 0 if up else SUBLANES - 1
    for j in (range(nt - 1, -1, -1) if up else range(nt)):
        tiles[j] = b[j] if carry is None else b[j] + a[j] * carry
        carry = tiles[j][edge:edge + 1, :]
    return jnp.concatenate(tiles, axis=0)


@jax.custom_vjp
def _lin_scan(a, b, h0):
    return _scan_tiles(a, b, h0, up=False)


def _lin_scan_fwd(a, b, h0):
    h = _lin_scan(a, b, h0)
    return h, (a, h0, h)


def _lin_scan_bwd(res, g):
    a, h0, h = res
    n = a.shape[0]
    row = _row_iota(a)
    a_next = jnp.where(row < n - 1, _roll_up(a, 1), 0.0)
    gg = _scan_tiles(a_next, g, None, up=True)
    h_prev = jnp.where(row >= 1, _roll_down(h, 1), h0)
    return gg * h_prev, gg, _row(a * gg, 0)


_lin_scan.defvjp(_lin_scan_fwd, _lin_scan_bwd)


@jax.custom_vjp
def _cumsum_rows(x):
    n = x.shape[0]
    row = _row_iota(x)
    s = 1
    while s < n:
        x = x + jnp.where(row >= s, _roll_down(x, s), 0.0)
        s *= 2
    return x


def _cumsum_rows_fwd(x):
    return _cumsum_rows(x), None


def _cumsum_rows_bwd(_, g):
    n = g.shape[0]
    row = _row_iota(g)
    s = 1
    while s < n:
        g = g + jnp.where(row < n - s, _roll_up(g, s), 0.0)
        s *= 2
    return (g,)


_cumsum_rows.defvjp(_cumsum_rows_fwd, _cumsum_rows_bwd)


def _sigmoid(x):
    return jax.nn.sigmoid(x)


def _softplus(x):
    return jnp.maximum(x, 0.0) + jnp.log1p(jnp.exp(-jnp.abs(x)))


def _gelu(x):
    return jax.nn.gelu(x, approximate=True)


def _neg_expm1(x):
    series = -x * (1.0 + x * (0.5 + x * (1.0 / 6.0 + x * (1.0 / 24.0))))
    return jnp.where(x > -0.01, series, 1.0 - jnp.exp(x))


def _rms(x, g):
    return x * lax.rsqrt(jnp.mean(x * x, axis=-1, keepdims=True) + EPS) * g


def _f_rmsnorm(carries, halos, xs, params):
    (h,) = xs
    (g,) = params
    return (), (_rms(h, g),)


def _f_rmsnorm_res(carries, halos, xs, params):
    (h,) = xs
    (g,) = params
    return (), (_rms(h, g), h)


def _f_gmlp(carries, halos, xs, params):
    (za,) = xs
    ln_g, ln_b, w_s, b_st = params
    u, v = _split_cols(_gelu(za), (D, D))
    vc = v - jnp.mean(v, axis=-1, keepdims=True)
    vn = vc * lax.rsqrt(jnp.mean(vc * vc, axis=-1, keepdims=True) + EPS) * ln_g + ln_b
    q = CHUNK
    causal = lax.broadcasted_iota(jnp.int32, (q, q), 0) >= lax.broadcasted_iota(jnp.int32, (q, q), 1)
    mixed = []
    for g, (w, vg) in enumerate(zip(_split_rows(w_s, (q,) * GROUPS_A), _split_cols(vn, (q,) * GROUPS_A))):
        mixed.append(_mm(jnp.where(causal, w, 0.0), vg, False, False) + _col(b_st, g))
    return (), (u * jnp.concatenate(mixed, axis=1),)


def _conv4(halo, x, w, b):
    y = b + _row(w, 3) * x
    for k in range(3):
        y = y + _row(w, k) * _shift_rows(halo, x, 3 - k)
    return y


def _f_lru(carries, halos, xs, params):
    (h0,) = carries
    (halo,) = halos
    xb_pre, gate = xs
    conv_w, conv_b, w_r, b_r, w_i, b_i, lam = params
    xb = _conv4(halo, xb_pre, conv_w, conv_b)
    hd = D // HEADS_B
    r_parts, i_parts = [], []
    heads = (hd,) * HEADS_B
    for xh, wr, wi in zip(_split_cols(xb, heads), _split_rows(w_r, heads), _split_rows(w_i, heads)):
        r_parts.append(_mm(xh, wr, False, False))
        i_parts.append(_mm(xh, wi, False, False))
    r = _sigmoid(jnp.concatenate(r_parts, axis=1) + b_r)
    i = _sigmoid(jnp.concatenate(i_parts, axis=1) + b_i)
    log_a = -LRU_C * r * _softplus(-lam)
    a = jnp.exp(log_a)
    inp = jnp.sqrt(_neg_expm1(2.0 * log_a)) * (i * xb)
    h = _lin_scan(a, inp, h0)
    return (_row(h, h.shape[0] - 1),), (_gelu(gate) * h,)


def _f_ssd(carries, halos, xs, params):
    (st,) = carries
    (halo,) = halos
    z, xbc_pre, dt_raw = xs
    conv_w, conv_b, dt_bias, a_log, d_skip, norm_g = params
    t = z.shape[0]
    xc = _conv4(halo, xbc_pre, conv_w, conv_b)
    xbc = xc * _sigmoid(xc)
    x_all, b_all, c_all = _split_cols(xbc, (D, GROUPS_C * STATE_C, GROUPS_C * STATE_C))
    x_pairs = _split_cols(x_all, (128,) * (HEADS_C // 2))
    b_groups = _split_cols(b_all, (STATE_C,) * GROUPS_C)
    c_groups = _split_cols(c_all, (STATE_C,) * GROUPS_C)
    st_pairs = _split_rows(st, (128,) * (HEADS_C // 2))
    dt = _softplus(dt_raw + dt_bias)
    adt = dt * (-jnp.exp(a_log))
    acs = _cumsum_rows(adt)
    acs_t = acs.T
    a_last = _row(acs, t - 1)
    lo = lax.broadcasted_iota(jnp.int32, (t, 128), 1) < HEAD_DIM_C
    lo_rows = lax.broadcasted_iota(jnp.int32, (128, STATE_C), 0) < HEAD_DIM_C
    causal = lax.broadcasted_iota(jnp.int32, (t, t), 0) >= lax.broadcasted_iota(jnp.int32, (t, t), 1)
    y_parts, st_parts = [], []
    for g in range(GROUPS_C):
        bg, cg = b_groups[g], c_groups[g]
        cb = _mm(cg, bg, False, True)
        for pr in range(2):
            pair = 2 * g + pr
            h0, h1 = 2 * pair, 2 * pair + 1
            x2 = x_pairs[pair]
            ac0, ac1 = _col(acs, h0), _col(acs, h1)
            l0 = jnp.exp(jnp.where(causal, ac0 - _row(acs_t, h0), -1e30))
            l1 = jnp.exp(jnp.where(causal, ac1 - _row(acs_t, h1), -1e30))
            xdt = x2 * jnp.where(lo, _col(dt, h0), _col(dt, h1))
            y_diag = (_mm(cb * l0, jnp.where(lo, xdt, 0.0), False, False)
                      + _mm(cb * l1, jnp.where(lo, 0.0, xdt), False, False))
            al0, al1 = _col(a_last, h0), _col(a_last, h1)
            decay_s = jnp.where(lo, jnp.exp(al0 - ac0), jnp.exp(al1 - ac1))
            s_new = _mm(xdt * decay_s, bg, True, False)
            prev = st_pairs[pair]
            y_off = _mm(cg, prev, False, True) * jnp.where(lo, jnp.exp(ac0), jnp.exp(ac1))
            skip = jnp.where(lo, _col(d_skip, h0), _col(d_skip, h1))
            y_parts.append(y_diag + y_off + x2 * skip)
            st_parts.append(prev * jnp.where(lo_rows, jnp.exp(al0), jnp.exp(al1)) + s_new)
    y = jnp.concatenate(y_parts, axis=1) * (z * _sigmoid(z))
    gw = D // GROUPS_C
    yn = []
    for yg in _split_cols(y, (gw,) * GROUPS_C):
        yn.append(yg * lax.rsqrt(jnp.mean(yg * yg, axis=-1, keepdims=True) + EPS))
    return (jnp.concatenate(st_parts, axis=0),), (jnp.concatenate(yn, axis=1) * norm_g,)


def _f_merge(carries, halos, xs, params):
    pa, pb, pc, g_raw = xs
    (b_gate,) = params
    ga, gb, gc = _split_cols(g_raw, (D, D, D))
    m = (_sigmoid(ga + _row(b_gate, 0)) * pa + _sigmoid(gb + _row(b_gate, 1)) * pb
         + _sigmoid(gc + _row(b_gate, 2)) * pc)
    return (), (m,)


def _f_loss(carries, halos, xs, params):
    (acc,) = carries
    h, target = xs
    (g,) = params
    err = jnp.square(_rms(h, g) - target)
    part = 0.5 * jnp.sum(jnp.mean(err, axis=-1, keepdims=True), axis=0, keepdims=True)
    return (acc + part,), ()


def _x_specs(xs, t, index_of):
    specs = []
    for arr, off, width in xs:
        assert off % width == 0 and off + width <= arr.shape[1]
        specs.append(pl.BlockSpec((t, width), functools.partial(lambda j, cb: (index_of(j), cb), cb=off // width)))
    return specs


def _halo_specs(xs, halo_idx, t, index_of):
    specs = []
    for xi in halo_idx:
        _, off, width = xs[xi]
        specs.append(pl.BlockSpec(
            (HALO, width),
            functools.partial(lambda j, cb: (jnp.maximum(index_of(j) * (t // HALO) - 1, 0), cb), cb=off // width)))
    return specs


def _full_spec(a):
    return pl.BlockSpec(a.shape, functools.partial(lambda j, nd: (0,) * nd, nd=a.ndim))


def _chunk_fwd(f, name, t, xs, params, outs, halo_idx=(), carry_shapes=(), save_carries=False, final_carries=False):
    s = xs[0][0].shape[0]
    n = s // t
    nx, nh, npar, no, nc = len(xs), len(halo_idx), len(params), len(outs), len(carry_shapes)
    ns = nc if save_carries else 0
    nf = nc if final_carries else 0

    def body(*refs):
        x_refs, refs = refs[:nx], refs[nx:]
        h_refs, refs = refs[:nh], refs[nh:]
        p_refs, refs = refs[:npar], refs[npar:]
        y_refs, refs = refs[:no], refs[no:]
        s_refs, refs = refs[:ns], refs[ns:]
        f_refs, c_refs = refs[:nf], refs[nf:]
        i = pl.program_id(0)

        @pl.when(i == 0)
        def _():
            for c in c_refs:
                c[...] = jnp.zeros_like(c)

        carries = tuple(c[...] for c in c_refs)
        for s_ref, c in zip(s_refs, carries):
            s_ref[0] = c
        halos = tuple(jnp.where(i > 0, h[...].astype(F32), 0.0) for h in h_refs)
        new_c, ys = f(carries, halos, tuple(x[...].astype(F32) for x in x_refs), tuple(p[...] for p in p_refs))
        for y_ref, y in zip(y_refs, ys):
            y_ref[...] = y.astype(y_ref.dtype)
        for c, v in zip(c_refs, new_c):
            c[...] = v
        for f_ref, v in zip(f_refs, new_c):
            f_ref[...] = v

    ident = lambda j: j
    out_shape = [jax.ShapeDtypeStruct((s, w), dt) for w, dt in outs]
    out_specs = [pl.BlockSpec((t, w), lambda j: (j, 0)) for w, _ in outs]
    if save_carries:
        out_shape += [jax.ShapeDtypeStruct((n,) + tuple(cs), F32) for cs in carry_shapes]
        out_specs += [pl.BlockSpec((1,) + tuple(cs), lambda j: (j, 0, 0)) for cs in carry_shapes]
    if final_carries:
        out_shape += [jax.ShapeDtypeStruct(tuple(cs), F32) for cs in carry_shapes]
        out_specs += [pl.BlockSpec(tuple(cs), lambda j: (0, 0)) for cs in carry_shapes]
    res = pl.pallas_call(
        body, name=name, grid=(n,),
        in_specs=_x_specs(xs, t, ident) + _halo_specs(xs, halo_idx, t, ident) + [_full_spec(p) for p in params],
        out_specs=out_specs, out_shape=out_shape,
        scratch_shapes=[pltpu.VMEM(tuple(cs), F32) for cs in carry_shapes],
        compiler_params=_cparams("arbitrary"),
    )(*[x[0] for x in xs], *[xs[xi][0] for xi in halo_idx], *params)
    return res[:no], res[no:no + ns], res[no + ns:]


def _chunk_bwd(f, name, t, xs, params, dys, dx_dtypes, halo_idx=(), saved=(), carry_seed=None, transposed=()):
    s = xs[0][0].shape[0]
    n = s // t
    nx, nh, npar, nc, ndy = len(xs), len(halo_idx), len(params), len(saved), len(dys)

    def body(*refs):
        x_refs, refs = refs[:nx], refs[nx:]
        h_refs, refs = refs[:nh], refs[nh:]
        p_refs, refs = refs[:npar], refs[npar:]
        s_refs, refs = refs[:nc], refs[nc:]
        dy_refs, refs = refs[:ndy], refs[ndy:]
        dx_refs, refs = refs[:nx], refs[nx:]
        dp_refs, refs = refs[:npar], refs[npar:]
        dxt_refs, refs = refs[:len(transposed)], refs[len(transposed):]
        dc_refs, dh_refs = refs[:nc], refs[nc:]
        j = pl.program_id(0)
        i = n - 1 - j

        @pl.when(j == 0)
        def _():
            for dc in dc_refs:
                dc[...] = jnp.zeros_like(dc) if carry_seed is None else carry_seed(dc.shape)
            for r in dh_refs + dp_refs:
                r[...] = jnp.zeros_like(r)

        carries = tuple(s_ref[0] for s_ref in s_refs)
        halos = tuple(jnp.where(i > 0, h[...].astype(F32), 0.0) for h in h_refs)
        x_vals = tuple(x[...].astype(F32) for x in x_refs)
        p_vals = tuple(p[...] for p in p_refs)
        _, vjp = jax.vjp(f, carries, halos, x_vals, p_vals)
        d_car, d_hal, d_xs, d_par = vjp((tuple(dc[...] for dc in dc_refs), tuple(d[...].astype(F32) for d in dy_refs)))
        d_xs = list(d_xs)
        for k, xi in enumerate(halo_idx):
            w = xs[xi][2]
            d_xs[xi] = d_xs[xi] + jnp.concatenate([jnp.zeros((t - HALO, w), F32), dh_refs[k][...]], axis=0)
            dh_refs[k][...] = jnp.where(i > 0, d_hal[k], 0.0)
        for dx_ref, dx in zip(dx_refs, d_xs):
            dx_ref[...] = dx.astype(dx_ref.dtype)
        for dxt_ref, xi in zip(dxt_refs, transposed):
            dxt_ref[...] = d_xs[xi].T.astype(BF16)
        for dp_ref, dp in zip(dp_refs, d_par):
            dp_ref[...] += dp
        for dc, v in zip(dc_refs, d_car):
            dc[...] = v

    rev = lambda j: n - 1 - j
    in_specs = (_x_specs(xs, t, rev) + _halo_specs(xs, halo_idx, t, rev) + [_full_spec(p) for p in params]
                + [pl.BlockSpec((1,) + a.shape[1:], lambda j: (n - 1 - j, 0, 0)) for a in saved]
                + [pl.BlockSpec((t, d.shape[1]), lambda j: (n - 1 - j, 0)) for d in dys])
    out_shape = ([jax.ShapeDtypeStruct((s, w), dt) for (_, _, w), dt in zip(xs, dx_dtypes)]
                 + [jax.ShapeDtypeStruct(p.shape, F32) for p in params])
    out_specs = ([pl.BlockSpec((t, w), lambda j: (n - 1 - j, 0)) for _, _, w in xs] + [_full_spec(p) for p in params])
    out_shape += [jax.ShapeDtypeStruct((xs[xi][2], s), BF16) for xi in transposed]
    out_specs += [pl.BlockSpec((xs[xi][2], t), lambda j: (0, n - 1 - j)) for xi in transposed]
    res = pl.pallas_call(
        body, name=name, grid=(n,), in_specs=in_specs, out_specs=out_specs, out_shape=out_shape,
        scratch_shapes=([pltpu.VMEM(a.shape[1:], F32) for a in saved]
                        + [pltpu.VMEM((HALO, xs[xi][2]), F32) for xi in halo_idx]),
        compiler_params=_cparams("arbitrary"),
    )(*[x[0] for x in xs], *[xs[xi][0] for xi in halo_idx], *params, *saved, *dys)
    return res[:nx], res[nx:nx + npar], res[nx + npar:]


def _tile(dim, pref):
    for cand in pref:
        if dim % cand == 0:
            return cand
    return dim


def _matmul(a, b, name, ta=False, tb=False, outs=(F32,), epilogue=None, extras=(), row_params=()):
    m, k = (a.shape[1], a.shape[0]) if ta else a.shape
    n = b.shape[0] if tb else b.shape[1]
    tm = _tile(m, (1024, 1152, 512, 256, 128))
    tn = _tile(n, (1152, 1024, 512, 256, 128))
    tk = _tile(k, (1024, 1152, 512, 256, 128))
    nk = k // tk
    ne, no = len(extras) + len(row_params), len(outs)
    ca, cb = (0 if ta else 1), (1 if tb else 0)

    def body(*refs):
        a_ref, b_ref = refs[:2]
        e_refs = refs[2:2 + ne]
        o_refs = refs[2 + ne:2 + ne + no]

        acc = refs[-1]
        kk = pl.program_id(2)

        @pl.when(kk == 0)
        def _():
            acc[...] = jnp.zeros_like(acc)

        acc[...] += _dg(_bf(a_ref[...]), _bf(b_ref[...]), ca, cb)

        @pl.when(kk == nk - 1)
        def _():
            res = acc[...]
            vals = (res,) if epilogue is None else epilogue(res, *[e[...] for e in e_refs])
            for o_ref, v in zip(o_refs, vals):
                o_ref[...] = v.astype(o_ref.dtype)

    a_spec = pl.BlockSpec((tk, tm), lambda i, j, kk: (kk, i)) if ta else pl.BlockSpec((tm, tk), lambda i, j, kk: (i, kk))
    b_spec = pl.BlockSpec((tn, tk), lambda i, j, kk: (j, kk)) if tb else pl.BlockSpec((tk, tn), lambda i, j, kk: (kk, j))
    mn_spec = pl.BlockSpec((tm, tn), lambda i, j, kk: (i, j))
    res = pl.pallas_call(
        body, name=name, grid=(m // tm, n // tn, nk),
        in_specs=[a_spec, b_spec] + [mn_spec] * len(extras)
        + [pl.BlockSpec((1, tn), lambda i, j, kk: (0, j))] * len(row_params),
        out_specs=[mn_spec] * no,
        out_shape=[jax.ShapeDtypeStruct((m, n), dt) for dt in outs],
        scratch_shapes=[pltpu.VMEM((tm, tn), F32)],
        compiler_params=_cparams("parallel", "parallel", "arbitrary"),
    )(a, b, *extras, *row_params)
    return res if no > 1 else res[0]


def _elementwise_block(r, c):
    if r % 8 == 0 and r >= 8:
        return _tile(r, (256, 128, 64, 32, 16, 8)), c
    return r, _tile(c, (256, 128))


PIECE_TILE = 1024


def _piece_steps(pieces):
    steps, s0 = [], 0
    for a in pieces:
        wt = min(a.shape[1], PIECE_TILE)
        assert a.shape[1] % wt == 0
        steps.append((s0, a.shape[1] // wt, wt))
        s0 += a.shape[1] // wt
    return steps, s0


def _matmul_pieces(pieces, b, name):
    steps, n_steps = _piece_steps(pieces)
    s_rows, n = pieces[0].shape[0], b.shape[1]
    tm = _tile(s_rows, (1024, 512, 256, 128))
    tail_rows = steps[-1][2]
    full_steps = n_steps - 1 if tail_rows < PIECE_TILE else n_steps
    b_tail = b[full_steps * PIECE_TILE:]
    np_ = len(pieces)

    def body(*refs):
        a_refs, b_ref, tail_ref, o_ref, acc = refs[:np_], refs[np_], refs[np_ + 1], refs[np_ + 2], refs[np_ + 3]
        s = pl.program_id(1)

        @pl.when(s == 0)
        def _():
            acc[...] = jnp.zeros_like(acc)

        for a_ref, (s0, ns, wt) in zip(a_refs, steps):
            @pl.when((s >= s0) & (s < s0 + ns))
            def _(a_ref=a_ref, wt=wt):
                rhs = b_ref[...] if wt == PIECE_TILE else tail_ref[...]
                acc[...] += _dg(_bf(a_ref[...]), _bf(rhs), 1, 0)

        @pl.when(s == n_steps - 1)
        def _():
            o_ref[...] = acc[...]

    n_rows = s_rows // tm

    def a_index(i, s, s0, ns):
        done = s >= s0 + ns
        return jnp.minimum(i + done.astype(jnp.int32), n_rows - 1), jnp.where(done, 0, jnp.clip(s - s0, 0, ns - 1))

    a_specs = [pl.BlockSpec((tm, wt), functools.partial(a_index, s0=s0, ns=ns)) for s0, ns, wt in steps]
    return pl.pallas_call(
        body, name=name, grid=(n_rows, n_steps),
        in_specs=a_specs + [pl.BlockSpec((PIECE_TILE, n), lambda i, s: (jnp.minimum(s, full_steps - 1), 0)),
                            pl.BlockSpec(b_tail.shape, lambda i, s: (0, 0))],
        out_specs=pl.BlockSpec((tm, n), lambda i, s: (i, 0)), out_shape=jax.ShapeDtypeStruct((s_rows, n), F32),
        scratch_shapes=[pltpu.VMEM((tm, n), F32)],
        compiler_params=_cparams("parallel", "arbitrary"),
    )(*pieces, b, b_tail)


def _matmul_pieces_t(pieces_t, b, name):
    steps, n_steps = _piece_steps([jax.ShapeDtypeStruct(a.shape[::-1], a.dtype) for a in pieces_t])
    s_rows, n = b.shape
    tk = _tile(s_rows, (1024, 512, 256, 128))
    nk = s_rows // tk
    total = sum(a.shape[0] for a in pieces_t)
    np_ = len(pieces_t)

    def body(*refs):
        a_refs, b_ref, o_ref, acc = refs[:np_], refs[np_], refs[np_ + 1], refs[np_ + 2]
        s, kk = pl.program_id(0), pl.program_id(1)

        @pl.when(kk == 0)
        def _():
            acc[...] = jnp.zeros_like(acc)

        for a_ref, (s0, ns, wt) in zip(a_refs, steps):
            @pl.when((s >= s0) & (s < s0 + ns))
            def _(a_ref=a_ref, wt=wt):
                acc[0:wt, :] += _dg(a_ref[...], _bf(b_ref[...]), 1, 0)

        @pl.when(kk == nk - 1)
        def _():
            o_ref[...] = acc[...]

    def a_index(s, kk, s0, ns):
        active = (s >= s0) & (s < s0 + ns)
        return jnp.clip(s - s0, 0, ns - 1), jnp.where(active, kk, jnp.where(s < s0, 0, nk - 1))

    a_specs = [pl.BlockSpec((wt, tk), functools.partial(a_index, s0=s0, ns=ns)) for s0, ns, wt in steps]
    return pl.pallas_call(
        body, name=name, grid=(n_steps, nk),
        in_specs=a_specs + [pl.BlockSpec((tk, n), lambda s, kk: (kk, 0))],
        out_specs=pl.BlockSpec((PIECE_TILE, n), lambda s, kk: (s, 0)), out_shape=jax.ShapeDtypeStruct((total, n), F32),
        scratch_shapes=[pltpu.VMEM((PIECE_TILE, n), F32)],
        compiler_params=_cparams("parallel", "arbitrary"),
    )(*pieces_t, b)


def _adamw_math(g, w, m, v):
    m_new = ADAM_B1 * m + (1.0 - ADAM_B1) * g
    v_new = ADAM_B2 * v + (1.0 - ADAM_B2) * jnp.square(g)
    m_hat = m_new / (1.0 - ADAM_B1 ** ADAM_STEP)
    v_hat = v_new / (1.0 - ADAM_B2 ** ADAM_STEP)
    return -ADAM_LR * (m_hat / (jnp.sqrt(v_hat) + ADAM_EPS) + ADAM_WD * w), m_new, v_new


def _adamw(parts, w, m, v, name):
    nl, r, c = w.shape
    k = parts[0].shape[0]
    tr = _tile(r, (128, 64, 32, 16, 8))
    nb = r // tr

    def body(*refs):
        p_refs, (w_ref, m_ref, v_ref), outs = refs[:nl], refs[nl:nl + 3], refs[nl + 3:]
        layer = pl.program_id(0)
        for q in range(nl):
            @pl.when(layer == q)
            def _(q=q):
                g = p_refs[q][0]
                for j in range(1, k):
                    g = g + p_refs[q][j]
                vals = (g,) + _adamw_math(g, w_ref[0], m_ref[0], v_ref[0])
                for o_ref, val in zip(outs, vals):
                    o_ref[0] = val

    spec = pl.BlockSpec((1, tr, c), lambda l, i: (l, i, 0))
    part_specs = [pl.BlockSpec((k, tr, c), functools.partial(
        lambda l, i, q: (0, jnp.where(l == q, i, jnp.where(l < q, 0, nb - 1)), 0), q=q)) for q in range(nl)]
    return pl.pallas_call(
        body, name=name, grid=(nl, nb), in_specs=part_specs + [spec] * 3,
        out_specs=[spec] * 4, out_shape=[jax.ShapeDtypeStruct((nl, r, c), F32)] * 4,
        compiler_params=_cparams("arbitrary", "arbitrary"),
    )(*parts, w, m, v)


def _adamw_transposed(grads, w, m, v, name):
    nl, r, c = w.shape
    tc = 64
    views = [jnp.transpose(a, (2, 0, 1)) for a in (w, m, v)]

    def body(*refs):
        g_refs, (w_ref, m_ref, v_ref), outs = refs[:nl], refs[nl:nl + 3], refs[nl + 3:]
        for l in range(nl):
            g = g_refs[l][...]
            vals = (g,) + _adamw_math(g, w_ref[:, l, :], m_ref[:, l, :], v_ref[:, l, :])
            for o_ref, val in zip(outs, vals):
                o_ref[:, l, :] = val

    spec = pl.BlockSpec((tc, nl, r), lambda i: (i, 0, 0))
    res = pl.pallas_call(
        body, name=name, grid=(pl.cdiv(c, tc),), in_specs=[pl.BlockSpec((tc, r), lambda i: (i, 0))] * nl + [spec] * 3,
        out_specs=[spec] * 4, out_shape=[jax.ShapeDtypeStruct((c, nl, r), F32)] * 4,
        compiler_params=_cparams("parallel"),
    )(*grads, *views)
    return [jnp.transpose(a, (1, 2, 0)) for a in res]


def _adamw_small(gathered, ws, ms, vs, loss_terms, name):
    n = len(ws)

    def device_sum(ref):
        s = ref[0]
        for j in range(1, N_DEV):
            s = s + ref[j]
        return s

    def body(*refs):
        g_refs, w_refs, m_refs, v_refs = refs[:n], refs[n:2 * n], refs[2 * n:3 * n], refs[3 * n:4 * n]
        loss_ref, outs, loss_out = refs[4 * n], refs[4 * n + 1:-1], refs[-1]
        for i in range(n):
            g = device_sum(g_refs[i])
            vals = (g,) + _adamw_math(g, w_refs[i][...], m_refs[i][...], v_refs[i][...])
            for kind, val in enumerate(vals):
                outs[kind * n + i][...] = val
        loss_out[...] = device_sum(loss_ref)

    res = pl.pallas_call(
        body, name=name,
        out_shape=[jax.ShapeDtypeStruct(a.shape, F32) for _ in range(4) for a in ws] + [
            jax.ShapeDtypeStruct(loss_terms.shape[1:], F32)],
        compiler_params=pltpu.CompilerParams(vmem_limit_bytes=VMEM_LIMIT),
    )(*gathered, *ws, *ms, *vs, loss_terms)
    return [res[kind * n:(kind + 1) * n] for kind in range(4)], res[-1]


ANY = pl.BlockSpec(memory_space=pl.ANY)


def _place():
    return lax.axis_index("x"), lax.axis_index("y"), lax.axis_index("c")


def _handshake(peers):
    barrier = pltpu.get_barrier_semaphore()
    for peer in peers:
        pl.semaphore_signal(barrier, inc=1, device_id=peer, device_id_type=MESH)
    pl.semaphore_wait(barrier, len(peers))


def _comm_call(body, name, inputs, out_shape, scratch, sequencer_id=None, after=()):
    if sequencer_id is None:
        return pl.pallas_call(body, name=name, out_shape=out_shape, in_specs=[ANY] * len(inputs),
                              out_specs=[ANY] * len(out_shape), scratch_shapes=scratch)(*inputs)
    n_in, n_after = len(inputs), len(after)

    def sequencer_body(*refs):
        body(*refs[:n_in], *refs[n_in + n_after:])

    return pl.kernel(
        sequencer_body, out_type=out_shape, mesh=plsc.ScalarSubcoreMesh(axis_name="sequencer", num_cores=1),
        scratch_types=scratch, compiler_params=pltpu.CompilerParams(collective_id=sequencer_id), name=name,
    )(*inputs, *after)


def _all_gather(blocks, name, sequencer_id=None, after=()):
    n = len(blocks)

    def body(*refs):
        x_refs, out_refs = refs[:n], refs[n:2 * n]
        send_sems, recv_sems, local_sems = refs[2 * n:]
        x, y, c = _place()
        me, sibling = (x, y, c), (x, y, 1 - c)
        chips = [(1 - x, y), (x, 1 - y), (1 - x, 1 - y)]
        if sequencer_id is not None:
            _handshake([sibling] + [(*chip, c) for chip in chips])

        def slot(a, px, py, pc):
            return out_refs[a].at[4 * px + 2 * py + pc]

        def copy(a, k, blk, to, src=None):
            return pltpu.make_async_remote_copy(
                src_ref=slot(a, *blk) if src is None else src, dst_ref=slot(a, *blk),
                send_sem=send_sems.at[7 * a + k], recv_sem=recv_sems.at[7 * a + k], device_id=to, device_id_type=MESH)

        mine = [pltpu.make_async_copy(x_refs[a], slot(a, *me), local_sems.at[a]) for a in range(n)]
        first = []
        for a in range(n):
            mine[a].start()
            first.append(copy(a, 0, me, sibling, src=x_refs[a]))
            first += [copy(a, 1 + j, me, (*chip, c), src=x_refs[a]) for j, chip in enumerate(chips)]
        for cp in first:
            cp.start()
        passed = []
        for j, chip in enumerate(chips):
            for a in range(n):
                copy(a, 1 + j, (*chip, c), me).wait_recv()
                passed.append(copy(a, 4 + j, (*chip, c), sibling))
                passed[-1].start()
        for a in range(n):
            copy(a, 0, sibling, me).wait_recv()
            for j, chip in enumerate(chips):
                copy(a, 4 + j, (*chip, 1 - c), me).wait_recv()
        for cp in first + passed:
            cp.wait_send()
        for cp in mine:
            cp.wait()

    return _comm_call(
        body, name, blocks, [jax.ShapeDtypeStruct((N_DEV,) + b.shape, b.dtype) for b in blocks],
        [pltpu.SemaphoreType.DMA((7 * n,)), pltpu.SemaphoreType.DMA((7 * n,)), pltpu.SemaphoreType.DMA((n,))],
        sequencer_id, after)


def _exchange_sibling(gs, name, sequencer_id=None, after=()):
    n = len(gs)

    def body(*refs):
        g_refs, out_refs = refs[:n], refs[n:2 * n]
        send_sems, recv_sems = refs[2 * n:]
        x, y, c = _place()
        if sequencer_id is not None:
            _handshake([(x, y, 1 - c)])
        copies = [pltpu.make_async_remote_copy(
            src_ref=g_refs[a].at[2 * k + 1 - c], dst_ref=out_refs[a].at[k], send_sem=send_sems.at[4 * a + k],
            recv_sem=recv_sems.at[4 * a + k], device_id=(x, y, 1 - c), device_id_type=MESH)
            for a in range(n) for k in range(4)]
        for cp in copies:
            cp.start()
        for cp in copies:
            cp.wait()

    return _comm_call(body, name, gs, [jax.ShapeDtypeStruct((4,) + g.shape[1:], g.dtype) for g in gs],
                      [pltpu.SemaphoreType.DMA((4 * n,)), pltpu.SemaphoreType.DMA((4 * n,))], sequencer_id, after)


def _other_chips():
    x, y = lax.axis_index("x"), lax.axis_index("y")
    return jnp.stack([2 * (1 - x) + y, 2 * x + 1 - y, 2 * (1 - x) + 1 - y]).astype(jnp.int32)


def _add_sibling(gs, r1s, name, after=()):
    n = len(gs)
    _, r, w = gs[0].shape
    tr, tc = _elementwise_block(r, w)
    chips = _other_chips()
    slabs = 2 * chips + lax.axis_index("c").astype(jnp.int32)

    def body(slab_ref, chip_ref, *refs):
        g_refs, r_refs, o_refs = refs[:n], refs[n:2 * n], refs[2 * n + len(after):]
        for g_ref, r_ref, o_ref in zip(g_refs, r_refs, o_refs):
            o_ref[...] = (g_ref[...] + r_ref[...]).astype(BF16)

    return pl.pallas_call(
        body, name=name, out_shape=[jax.ShapeDtypeStruct((3, r, w), BF16)] * n,
        grid_spec=pltpu.PrefetchScalarGridSpec(
            num_scalar_prefetch=2, grid=(3, r // tr, w // tc),
            in_specs=[pl.BlockSpec((1, tr, tc), lambda k, i, j, slab_ref, chip_ref: (slab_ref[k], i, j))] * n
            + [pl.BlockSpec((1, tr, tc), lambda k, i, j, slab_ref, chip_ref: (chip_ref[k], i, j))] * n
            + [ANY] * len(after),
            out_specs=[pl.BlockSpec((1, tr, tc), lambda k, i, j, slab_ref, chip_ref: (k, i, j))] * n),
        compiler_params=_cparams("parallel", "parallel", "parallel"),
    )(slabs, chips, *gs, *r1s, *after)


def _exchange_chips(ps, name, sequencer_id=None, after=()):
    n = len(ps)

    def body(*refs):
        p_refs, out_refs = refs[:n], refs[n:2 * n]
        send_sems, recv_sems = refs[2 * n:]
        x, y, c = _place()
        chips = [(1 - x, y), (x, 1 - y), (1 - x, 1 - y)]
        if sequencer_id is not None:
            _handshake([(*chip, c) for chip in chips])
        copies = [pltpu.make_async_remote_copy(
            src_ref=p_refs[a].at[j], dst_ref=out_refs[a].at[j], send_sem=send_sems.at[3 * a + j],
            recv_sem=recv_sems.at[3 * a + j], device_id=(px, py, c), device_id_type=MESH)
            for a in range(n) for j, (px, py) in enumerate(chips)]
        for cp in copies:
            cp.start()
        for cp in copies:
            cp.wait()

    return _comm_call(body, name, ps, [jax.ShapeDtypeStruct(p.shape, p.dtype) for p in ps],
                      [pltpu.SemaphoreType.DMA((3 * n,)), pltpu.SemaphoreType.DMA((3 * n,))], sequencer_id, after)


def _sum_chips(gs, r1s, others, name):
    n = len(gs)
    _, r, c = gs[0].shape
    tr, tc = _elementwise_block(r, c)
    chip = 2 * lax.axis_index("x") + lax.axis_index("y")
    place = jnp.stack([2 * chip + lax.axis_index("c"), chip]).astype(jnp.int32)

    def body(place_ref, *refs):
        for g_ref, r_ref, others_ref, o_ref in zip(refs[:n], refs[n:2 * n], refs[2 * n:3 * n], refs[3 * n:]):
            s = g_ref[0] + r_ref[0]
            for j in range(3):
                s = s + others_ref[j].astype(F32)
            o_ref[...] = s

    return pl.pallas_call(
        body, name=name, out_shape=[jax.ShapeDtypeStruct((r, c), F32)] * n,
        grid_spec=pltpu.PrefetchScalarGridSpec(
            num_scalar_prefetch=1, grid=(r // tr, c // tc),
            in_specs=[pl.BlockSpec((1, tr, tc), lambda i, j, place_ref: (place_ref[0], i, j))] * n
            + [pl.BlockSpec((1, tr, tc), lambda i, j, place_ref: (place_ref[1], i, j))] * n
            + [pl.BlockSpec((3, tr, tc), lambda i, j, place_ref: (0, i, j))] * n,
            out_specs=[pl.BlockSpec((tr, tc), lambda i, j, place_ref: (i, j))] * n),
        compiler_params=_cparams("parallel", "parallel"),
    )(place, *gs, *r1s, *others)


def _reorder_in_proj(wt):
    za_zb, zc, xbc, dt, gates = (wt[:4096], wt[4096:5120], wt[5120:7168], wt[7168:7184], wt[7184:])
    return jnp.concatenate([za_zb, xbc, gates, zc, dt, jnp.zeros((DT_PAD - 16, wt.shape[1]), wt.dtype)], axis=0)


def _restore_in_proj(wt):
    return jnp.concatenate([wt[:4096], wt[OFF_ZC:OFF_ZC + W_ZC], wt[OFF_XBC:OFF_XBC + W_XBC],
                            wt[OFF_DT:OFF_DT + 16], wt[OFF_GATE:OFF_GATE + W_GATE]], axis=0)


def _lanes_from_devices(g):
    return jnp.moveaxis(g, 0, 1).reshape(g.shape[1], N_DEV * g.shape[2])


def _lanes_to_devices(a):
    return jnp.moveaxis(a.reshape(a.shape[0], N_DEV, a.shape[1] // N_DEV), 1, 0)


def _pad_lanes(a, width):
    return jnp.pad(a, ((0, 0), (0, width - a.shape[1])))


BIG = ("w_in", "w_branch_a", "w_branch_b", "w_branch_c", "w_out", "w_mlp_up", "w_mlp_down")
SMALL_SHARDED = ("b_gate", "lru_conv_w", "ssd_conv_w")
REPLICATED = ("norm_mix_g", "gmlp_ln_g", "gmlp_ln_b", "gmlp_w_s", "gmlp_b_s", "lru_conv_b", "lru_w_r", "lru_b_r",
              "lru_w_i", "lru_b_i", "lru_lambda", "ssd_conv_b", "ssd_dt_bias", "ssd_a_log", "ssd_d", "ssd_norm_g",
              "norm_mlp_g", "final_norm_g")
WEIGHTS = ("norm_mix_g", "w_in", "b_gate", "gmlp_ln_g", "gmlp_ln_b", "gmlp_w_s", "gmlp_b_s", "lru_conv_w", "lru_conv_b",
           "lru_w_r", "lru_b_r", "lru_w_i", "lru_b_i", "lru_lambda", "ssd_conv_w", "ssd_conv_b", "ssd_dt_bias",
           "ssd_a_log", "ssd_d", "ssd_norm_g", "w_branch_a", "w_branch_b", "w_branch_c", "w_out", "norm_mlp_g",
           "w_mlp_up", "w_mlp_down", "final_norm_g")
TRANSPOSED = ("w_in", "w_mlp_up")
SMALL_MATRICES = ("gmlp_w_s", "lru_w_r", "lru_w_i")
SMALL_VECTORS = tuple(n for n in REPLICATED if n not in SMALL_MATRICES)
GRADIENT_GROUPS = {"mlp": ("w_mlp_up", "w_mlp_down"), "mix": ("w_branch_a", "w_branch_b", "w_branch_c", "w_out"),
                   "in": ("w_in",)}


def _layer_params(full, l):
    row = lambda a: a.reshape(1, -1)
    return dict(
        norm_mix_g=row(full["norm_mix_g"][l]), norm_mlp_g=row(full["norm_mlp_g"][l]),
        gmlp=(row(full["gmlp_ln_g"][l]), row(full["gmlp_ln_b"][l]), full["gmlp_w_s"][l].reshape(GROUPS_A * CHUNK, CHUNK),
              full["gmlp_b_s"][l].T),
        lru=(full["lru_conv_w"][l], row(full["lru_conv_b"][l]), full["lru_w_r"][l].reshape(D, D // HEADS_B),
             row(full["lru_b_r"][l]), full["lru_w_i"][l].reshape(D, D // HEADS_B), row(full["lru_b_i"][l]),
             row(full["lru_lambda"][l])),
        ssd=(full["ssd_conv_w"][l], row(full["ssd_conv_b"][l]), _pad_lanes(row(full["ssd_dt_bias"][l]), DT_PAD),
             _pad_lanes(row(full["ssd_a_log"][l]), DT_PAD), _pad_lanes(row(full["ssd_d"][l]), DT_PAD),
             row(full["ssd_norm_g"][l])),
        b_gate=full["b_gate"][l],
    )


def _forward_layer(h, hn, p, wb, l, after_mixers=None, next_norm_g=None):
    tag = f"l{l}"
    t_row = 512
    if hn is None:
        (hn,), _, _ = _chunk_fwd(_f_rmsnorm, f"norm_mix_{tag}", t_row, [(h, 0, D)], [p["norm_mix_g"]], [(D, BF16)])
    proj = _matmul(hn, wb["w_in"], f"in_proj_{tag}", tb=True)
    (ya,), _, _ = _chunk_fwd(_f_gmlp, f"gmlp_{tag}", CHUNK, [(proj, OFF_ZA, W_ZA)], p["gmlp"], [(D, BF16)])
    lru_xs = [(proj, OFF_ZB, D), (proj, OFF_ZB + D, D)]
    (yb,), lru_saved, _ = _chunk_fwd(_f_lru, f"lru_{tag}", CHUNK, lru_xs, p["lru"], [(D, BF16)], halo_idx=(0,),
                                     carry_shapes=[(1, D)], save_carries=True)
    ssd_xs = [(proj, OFF_ZC, W_ZC), (proj, OFF_XBC, W_XBC), (proj, OFF_DT, W_DT)]
    (yc,), ssd_saved, _ = _chunk_fwd(_f_ssd, f"ssd_{tag}", CHUNK, ssd_xs, p["ssd"], [(D, BF16)], halo_idx=(1,),
                                     carry_shapes=[(HEADS_C * HEAD_DIM_C, STATE_C)], save_carries=True)
    if after_mixers is not None:
        after_mixers(yc)
    pa = _matmul(ya, wb["w_branch_a"], f"branch_a_{tag}")
    pb = _matmul(yb, wb["w_branch_b"], f"branch_b_{tag}")
    pc = _matmul(yc, wb["w_branch_c"], f"branch_c_{tag}")
    merge_xs = [(pa, 0, D), (pb, 0, D), (pc, 0, D), (proj, OFF_GATE, W_GATE)]
    (merged,), _, _ = _chunk_fwd(_f_merge, f"merge_{tag}", t_row, merge_xs, [p["b_gate"]], [(D, BF16)])

    def add_and_norm(acc, res, g):
        s = acc + res
        return s, _rms(s, g)

    h_mid, hn2 = _matmul(merged, wb["w_out"], f"out_proj_{tag}", outs=(F32, BF16), epilogue=add_and_norm, extras=(h,),
                         row_params=(p["norm_mlp_g"],))

    def relu_sq(acc):
        r = jnp.maximum(acc, 0.0)
        return r, r * r

    relu_up, act = _matmul(hn2, wb["w_mlp_up"], f"mlp_up_{tag}", tb=True, outs=(F32, BF16), epilogue=relu_sq)
    if next_norm_g is None:
        hn_out = None
        h_out = _matmul(act, wb["w_mlp_down"], f"mlp_down_{tag}", epilogue=lambda acc, res: (acc + res,), extras=(h_mid,))
    else:
        h_out, hn_out = _matmul(act, wb["w_mlp_down"], f"mlp_down_{tag}", outs=(F32, BF16), epilogue=add_and_norm,
                                extras=(h_mid,), row_params=(next_norm_g,))
    saved = dict(h=h, hn=hn, proj=proj, ya=ya, yb=yb, yc=yc, lru_saved=lru_saved, ssd_saved=ssd_saved, pa=pa, pb=pb,
                 pc=pc, merged=merged, h_mid=h_mid, hn2=hn2, relu_up=relu_up, act=act, lru_xs=lru_xs, ssd_xs=ssd_xs,
                 merge_xs=merge_xs)
    return h_out, hn_out, saved


def _backward_layer(dh, sv, p, wb, l):
    tag = f"l{l}"
    t_row = 512
    g = {}
    d_up = _matmul(dh, wb["w_mlp_down"], f"d_act_{tag}", tb=True, outs=(BF16,),
                   epilogue=lambda acc, r: (acc * (2.0 * r),), extras=(sv["relu_up"],))
    g["w_mlp_down"] = _matmul(sv["act"], dh, f"dw_mlp_down_{tag}", ta=True)
    g["w_mlp_up"] = _matmul(d_up, sv["hn2"], f"dw_mlp_up_{tag}", ta=True)
    d_hn2 = _matmul(d_up, wb["w_mlp_up"], f"d_hn2_{tag}")
    (d_mid,), (g["norm_mlp_g"],), _ = _chunk_bwd(_f_rmsnorm_res, f"norm_mlp_bwd_{tag}", t_row, [(sv["h_mid"], 0, D)],
                                                 [p["norm_mlp_g"]], [d_hn2, dh], [F32])
    d_merged = _matmul(d_mid, wb["w_out"], f"d_merged_{tag}", tb=True)
    g["w_out"] = _matmul(sv["merged"], d_mid, f"dw_out_{tag}", ta=True)
    (d_pa, d_pb, d_pc, d_gate), (g["b_gate"],), (t_gate,) = _chunk_bwd(
        _f_merge, f"merge_bwd_{tag}", t_row, sv["merge_xs"], [p["b_gate"]], [d_merged], [BF16] * 4, transposed=(3,))
    d_y = {}
    for br, d_p, y in (("a", d_pa, sv["ya"]), ("b", d_pb, sv["yb"]), ("c", d_pc, sv["yc"])):
        g[f"w_branch_{br}"] = _matmul(y, d_p, f"dw_branch_{br}_{tag}", ta=True)
        d_y[br] = _matmul(d_p, wb[f"w_branch_{br}"], f"d_y{br}_{tag}", tb=True)
    (d_za,), g_gmlp, (t_za,) = _chunk_bwd(_f_gmlp, f"gmlp_bwd_{tag}", CHUNK, [(sv["proj"], OFF_ZA, W_ZA)], p["gmlp"],
                                          [d_y["a"]], [BF16], transposed=(0,))
    (d_xb, d_gt), g_lru, (t_xb, t_gt) = _chunk_bwd(
        _f_lru, f"lru_bwd_{tag}", CHUNK, sv["lru_xs"], p["lru"], [d_y["b"]], [BF16] * 2, halo_idx=(0,),
        saved=sv["lru_saved"], transposed=(0, 1))
    (d_zc, d_xbc, d_dt), g_ssd, (t_zc, t_xbc, t_dt) = _chunk_bwd(
        _f_ssd, f"ssd_bwd_{tag}", CHUNK, sv["ssd_xs"], p["ssd"], [d_y["c"]], [BF16] * 3, halo_idx=(1,),
        saved=sv["ssd_saved"], transposed=(0, 1, 2))
    d_proj = [d_za, d_xb, d_gt, d_xbc, d_gate, d_zc, d_dt]
    g["w_in"] = _matmul_pieces_t([t_za, t_xb, t_gt, t_xbc, t_gate, t_zc, t_dt], sv["hn"], f"dw_in_{tag}")
    d_hn = _matmul_pieces(d_proj, wb["w_in"], f"d_hn_{tag}")
    (d_h,), (g["norm_mix_g"],), _ = _chunk_bwd(_f_rmsnorm_res, f"norm_mix_bwd_{tag}", t_row, [(sv["h"], 0, D)],
                                               [p["norm_mix_g"]], [d_hn, d_mid], [F32])
    g["w_in"] = _restore_in_proj(g["w_in"])
    for n in BIG:
        g[n] = g[n].reshape(N_DEV, g[n].shape[0] // N_DEV, g[n].shape[1])
    g["gmlp_ln_g"], g["gmlp_ln_b"], g["gmlp_w_s"] = g_gmlp[:3]
    g["gmlp_b_s"] = g_gmlp[3].T
    (g["lru_conv_w"], g["lru_conv_b"], g["lru_w_r"], g["lru_b_r"], g["lru_w_i"], g["lru_b_i"], g["lru_lambda"]) = g_lru
    g["ssd_conv_w"], g["ssd_conv_b"] = g_ssd[:2]
    g["ssd_dt_bias"], g["ssd_a_log"], g["ssd_d"] = (a[:, :HEADS_C] for a in g_ssd[2:5])
    g["ssd_norm_g"] = g_ssd[5]
    g["later"] = {"mlp": d_merged, "mix": d_za, "in": d_h}
    return d_h, g


LOSS_ROWS = 512


def _loss_and_grads(h, target, full, layer_weights, first_gathered):
    seq = h.shape[0]
    layer_p = [_layer_params(full, l) for l in range(DEPTH)]
    layer_w = [layer_weights(0, (first_gathered,))]
    saved = []
    hn = None
    for l in range(DEPTH):
        fetch_next = next_norm_g = None
        if l + 1 < DEPTH:
            fetch_next = lambda y, l=l: layer_w.append(layer_weights(l + 1, (layer_w[l]["w_mlp_down"], y)))
            next_norm_g = layer_p[l + 1]["norm_mix_g"]
        h, hn, sv = _forward_layer(h, hn, layer_p[l], layer_w[l], l, fetch_next, next_norm_g)
        saved.append(sv)
    final_g = full["final_norm_g"].reshape(1, D)
    loss_xs = [(h, 0, D), (target, 0, D)]
    t_loss = min(LOSS_ROWS, seq)
    _, _, (loss_acc,) = _chunk_fwd(_f_loss, "loss", t_loss, loss_xs, [final_g], [], carry_shapes=[(1, 128)],
                                   final_carries=True)
    zero_acc = jnp.zeros((seq // t_loss, 1, 128), F32)
    seed = lambda shape: (lax.broadcasted_iota(jnp.int32, shape, 1) == 0).astype(F32)
    (dh, _), (g_final,), _ = _chunk_bwd(_f_loss, "loss_bwd", t_loss, loss_xs, [final_g], [], [F32, F32],
                                        saved=[zero_acc], carry_seed=seed)
    layer_g = [None] * DEPTH
    for l in reversed(range(DEPTH)):
        dh, layer_g[l] = _backward_layer(dh, saved[l], layer_p[l], layer_w[l], l)
    return loss_acc, dh, layer_g, g_final


def _small_views(d):
    views = {n: d[n] for n in REPLICATED}
    views["gmlp_b_s"] = d["gmlp_b_s"].reshape(DEPTH * GROUPS_A, CHUNK)
    views["final_norm_g"] = d["final_norm_g"].reshape(1, D)
    for n in SMALL_MATRICES:
        views[n] = d[n].reshape(DEPTH * D, D // HEADS_B)
    return views


def kernel(x, norm_mix_g, w_in, b_gate, gmlp_ln_g, gmlp_ln_b, gmlp_w_s, gmlp_b_s, lru_conv_w, lru_conv_b, lru_w_r, lru_b_r, lru_w_i, lru_b_i, lru_lambda, ssd_conv_w, ssd_conv_b, ssd_dt_bias, ssd_a_log, ssd_d, ssd_norm_g, w_branch_a, w_branch_b, w_branch_c, w_out, norm_mlp_g, w_mlp_up, w_mlp_down, final_norm_g, loss_target, m_norm_mix_g, m_w_in, m_b_gate, m_gmlp_ln_g, m_gmlp_ln_b, m_gmlp_w_s, m_gmlp_b_s, m_lru_conv_w, m_lru_conv_b, m_lru_w_r, m_lru_b_r, m_lru_w_i, m_lru_b_i, m_lru_lambda, m_ssd_conv_w, m_ssd_conv_b, m_ssd_dt_bias, m_ssd_a_log, m_ssd_d, m_ssd_norm_g, m_w_branch_a, m_w_branch_b, m_w_branch_c, m_w_out, m_norm_mlp_g, m_w_mlp_up, m_w_mlp_down, m_final_norm_g, v_norm_mix_g, v_w_in, v_b_gate, v_gmlp_ln_g, v_gmlp_ln_b, v_gmlp_w_s, v_gmlp_b_s, v_lru_conv_w, v_lru_conv_b, v_lru_w_r, v_lru_b_r, v_lru_w_i, v_lru_b_i, v_lru_lambda, v_ssd_conv_w, v_ssd_conv_b, v_ssd_dt_bias, v_ssd_a_log, v_ssd_d, v_ssd_norm_g, v_w_branch_a, v_w_branch_b, v_w_branch_c, v_w_out, v_norm_mlp_g, v_w_mlp_up, v_w_mlp_down, v_final_norm_g):
    args = locals()
    w = {n: args[n] for n in WEIGHTS}
    m = {n: args["m_" + n] for n in WEIGHTS}
    v = {n: args["v_" + n] for n in WEIGHTS}
    seq = x.shape[1]
    h = x.reshape(seq, D)
    target = loss_target.reshape(seq, D)

    def shard_on_wire(n, l):
        return (w[n][l].T if n in TRANSPOSED else w[n][l]).astype(BF16)

    first = _all_gather([shard_on_wire("w_in", 0)] + [w[n] for n in SMALL_SHARDED], "gather_weights_first")
    full = {n: w[n] for n in REPLICATED}
    for n, g in zip(SMALL_SHARDED, first[1:]):
        full[n] = jnp.stack([_lanes_from_devices(g[:, l]) for l in range(DEPTH)])

    def layer_weights(l, after):
        have = {"w_in": first[0]} if l == 0 else {}
        names = [n for n in BIG if n not in have]
        later = _all_gather([shard_on_wire(n, l) for n in names], f"gather_weights_l{l}", SEQ_GATHER + l, after=after)
        have.update(zip(names, later))
        wl = {n: have[n].reshape(-1, D) for n in BIG}
        wl["w_in"] = _reorder_in_proj(wl["w_in"])
        return wl

    loss_local, dh, layer_g, g_final = _loss_and_grads(h, target, full, layer_weights, first[0])
    grad_x = dh.reshape(x.shape)
    out = {}
    kinds = ("grad", "delta", "new_m", "new_v")

    sequencer_before = {}

    def reduce_scatter(slabs, tag, on_sequencer, later=()):
        keys = list(slabs)
        ids = (SEQ_TO_SIBLING, SEQ_TO_CHIPS) if on_sequencer else (None, None)
        from_sibling = _exchange_sibling([slabs[k] for k in keys], f"grads_to_sibling_{tag}", ids[0],
                                         sequencer_before.get("sibling", ()))
        from_sibling = dict(zip(keys, from_sibling))
        same_shape = {}
        for k in keys:
            same_shape.setdefault(slabs[k].shape[1:], []).append(k)
        chip_sums = {}
        for (r, c), ks in same_shape.items():
            sums = _add_sibling([slabs[k] for k in ks], [from_sibling[k] for k in ks], f"add_sibling_{tag}_{r}x{c}", later)
            chip_sums.update(zip(ks, sums))
        from_chips = _exchange_chips([chip_sums[k] for k in keys], f"grads_to_chips_{tag}", ids[1],
                                     sequencer_before.get("chips", ()))
        if on_sequencer:
            sequencer_before["sibling"], sequencer_before["chips"] = (from_sibling[keys[0]],), (from_chips[0],)
        from_chips = dict(zip(keys, from_chips))
        summed = {}
        for (r, c), ks in same_shape.items():
            sums = _sum_chips([slabs[k] for k in ks], [from_sibling[k] for k in ks], [from_chips[k] for k in ks],
                              f"sum_chips_{tag}_{r}x{c}")
            summed.update(zip(ks, sums))
        return summed

    small = {}
    for n in SMALL_SHARDED:
        small[n, None] = jnp.concatenate([_lanes_to_devices(layer_g[l][n]) for l in range(DEPTH)], axis=1)
    g_small = {n: jnp.concatenate([layer_g[l][n] for l in range(DEPTH)], axis=0) for n in REPLICATED[:-1]}
    g_small["final_norm_g"] = g_final
    for n in SMALL_MATRICES:
        small[n, None] = g_small[n].reshape(N_DEV, -1, g_small[n].shape[-1])
    reduced = {}
    groups = [(l, grp) for l in range(DEPTH - 1, -1, -1) for grp in GRADIENT_GROUPS]
    for l, grp in groups:
        if (l, grp) == groups[-1]:
            reduced.update(reduce_scatter(small, "small", True, (layer_g[l][GRADIENT_GROUPS[grp][0]],)))
        slabs = {(n, l): layer_g[l][n] for n in GRADIENT_GROUPS[grp]}
        reduced.update(reduce_scatter(slabs, f"{grp}_l{l}", True, (layer_g[l]["later"][grp],)))
    for n in BIG:
        grads = [reduced[n, l] for l in range(DEPTH)]
        if n in TRANSPOSED and w[n].shape[-1] % LANES:
            res = _adamw_transposed(grads, w[n], m[n], v[n], f"adamw_{n}")
        else:
            res = _adamw([(g.T if n in TRANSPOSED else g)[None] for g in grads], w[n], m[n], v[n], f"adamw_{n}")
        for kind, a in zip(kinds, res):
            out[kind, n] = a
    for n in SMALL_SHARDED:
        one = lambda a: a.reshape((1, -1, a.shape[-1]))
        for kind, a in zip(kinds, _adamw([reduced[n, None][None]], one(w[n]), one(m[n]), one(v[n]), f"adamw_{n}")):
            out[kind, n] = a.reshape(w[n].shape)

    to_gather = [reduced[n, None] if n in SMALL_MATRICES else g_small[n] for n in REPLICATED]
    *g_gathered, loss_terms = _all_gather(to_gather + [loss_local], "gather_small_grads", SEQ_GATHER_SMALL)
    g_all = dict(zip(REPLICATED, g_gathered))
    wv, mv, vv = _small_views(w), _small_views(m), _small_views(v)
    res, loss_sum = _adamw_small([g_all[n] for n in SMALL_VECTORS],
                                 *[[d[n] for n in SMALL_VECTORS] for d in (wv, mv, vv)], loss_terms, "adamw_vectors")
    loss = loss_sum[0, 0]
    for kind, arrays in zip(kinds, res):
        for n, a in zip(SMALL_VECTORS, arrays):
            out[kind, n] = a.reshape(w[n].shape)
    for n in SMALL_MATRICES:
        g_full = g_all[n].reshape((1, 1) + wv[n].shape)
        for kind, a in zip(kinds, _adamw([g_full[0]], wv[n][None], mv[n][None], vv[n][None], f"adamw_{n}")):
            out[kind, n] = a.reshape(w[n].shape)

    return (loss, grad_x, *[out[kind, n] for kind in kinds for n in WEIGHTS])
```

```python
import functools

import jax
import jax.numpy as jnp
from jax import lax
from jax.experimental import pallas as pl
from jax.experimental.pallas import tpu as pltpu
from jax.experimental.pallas import tpu_sc as plsc

F32 = jnp.float32
BF16 = jnp.bfloat16
MESH = pl.DeviceIdType.MESH

D = 1024
DEPTH = 2
EPS = 1e-6
CHUNK = 128
GROUPS_A = 8
HEADS_B = 8
LRU_C = 8.0
HEADS_C = 16
HEAD_DIM_C = 64
GROUPS_C = 4
STATE_C = 128
DT_PAD = 128
OFF_ZA, W_ZA = 0, 2048
OFF_ZB, W_ZB = 2048, 2048
OFF_XBC, W_XBC = 4096, 2048
OFF_GATE, W_GATE = 6144, 3072
OFF_ZC, W_ZC = 9216, 1024
OFF_DT, W_DT = 10240, DT_PAD
D_IN_PAD = 10368
N_DEV = 8
SEQ_GATHER = 1
SEQ_TO_SIBLING = SEQ_GATHER + DEPTH
SEQ_TO_CHIPS = SEQ_TO_SIBLING + 1
SEQ_GATHER_SMALL = SEQ_TO_CHIPS + 1

ADAM_LR = 0.001
ADAM_B1 = 0.9
ADAM_B2 = 0.999
ADAM_EPS = 1e-08
ADAM_WD = 0.01
ADAM_STEP = 10

VMEM_LIMIT = 56 * 1024 * 1024
HALO = 8


def _cparams(*sem):
    return pltpu.CompilerParams(dimension_semantics=sem, vmem_limit_bytes=VMEM_LIMIT)


def _bf(x):
    return x.astype(BF16)


def _dg(a, b, ca, cb):
    return lax.dot_general(a, b, (((ca,), (cb,)), ((), ())), preferred_element_type=F32)


@functools.partial(jax.custom_vjp, nondiff_argnums=(2, 3))
def _mm(a, b, ta, tb):
    return _dg(_bf(a), _bf(b), 0 if ta else 1, 1 if tb else 0)


def _mm_fwd(a, b, ta, tb):
    return _mm(a, b, ta, tb), (a, b)


def _mm_bwd(ta, tb, res, g):
    a, b = res
    ma = 1 if ta else 0
    nb = 0 if tb else 1
    gb, ab, bb = _bf(g), _bf(a), _bf(b)
    da = _dg(bb, gb, nb, 1) if ta else _dg(gb, bb, 1, nb)
    db = _dg(gb, ab, 0, ma) if tb else _dg(ab, gb, ma, 0)
    return da.astype(a.dtype), db.astype(b.dtype)


_mm.defvjp(_mm_fwd, _mm_bwd)


def _slices(x, sizes, axis):
    out, lo = [], 0
    for size in sizes:
        out.append(lax.slice_in_dim(x, lo, lo + size, axis=axis))
        lo += size
    return tuple(out)


@functools.partial(jax.custom_vjp, nondiff_argnums=(1,))
def _split_cols(x, widths):
    return _slices(x, widths, 1)


_split_cols.defvjp(lambda x, widths: (_slices(x, widths, 1), None),
                   lambda widths, _, gs: (jnp.concatenate(gs, axis=1),))


@functools.partial(jax.custom_vjp, nondiff_argnums=(1,))
def _split_rows(x, heights):
    return _slices(x, heights, 0)


_split_rows.defvjp(lambda x, heights: (_slices(x, heights, 0), None),
                   lambda heights, _, gs: (jnp.concatenate(gs, axis=0),))


def _col(x, j):
    lane = lax.broadcasted_iota(jnp.int32, x.shape, 1)
    return jnp.sum(jnp.where(lane == j, x, 0.0), axis=1, keepdims=True)


def _row(x, i):
    r = lax.broadcasted_iota(jnp.int32, x.shape, 0)
    return jnp.sum(jnp.where(r == i, x, 0.0), axis=0, keepdims=True)


def _roll_down(x, s):
    return pltpu.roll(x, s, 0)


def _roll_up(x, s):
    return pltpu.roll(x, x.shape[0] - s, 0)


def _row_iota(x):
    return lax.broadcasted_iota(jnp.int32, x.shape, 0)


@functools.partial(jax.custom_vjp, nondiff_argnums=(2,))
def _shift_rows(halo, x, s):
    if s == 0:
        return x
    return _roll_down(jnp.concatenate([halo, x], axis=0), s)[HALO:]


def _shift_rows_fwd(halo, x, s):
    return _shift_rows(halo, x, s), None


def _shift_rows_bwd(s, _, g):
    if s == 0:
        return jnp.zeros((HALO, g.shape[1]), g.dtype), g
    ge = jnp.concatenate([jnp.zeros((HALO, g.shape[1]), g.dtype), g], axis=0)
    de = _roll_up(ge, s)
    return de[:HALO], de[HALO:]


_shift_rows.defvjp(_shift_rows_fwd, _shift_rows_bwd)


SUBLANES = 8
LANES = 128


def _scan_tiles(a, b, carry, up):
    n, c = a.shape
    nt = n // SUBLANES
    a = a.reshape(nt, SUBLANES, c)
    b = b.reshape(nt, SUBLANES, c)
    sub = lax.broadcasted_iota(jnp.int32, a.shape, 1)
    s = 1
    while s < SUBLANES:
        keep = (sub < SUBLANES - s) if up else (sub >= s)
        shift = SUBLANES - s if up else s
        a_sh = jnp.where(keep, pltpu.roll(a, shift, 1), 1.0)
        b_sh = jnp.where(keep, pltpu.roll(b, shift, 1), 0.0)
        b = a * b_sh + b
        a = a * a_sh
        s *= 2
    tiles = [None] * nt
    edge = 0 if up else SUBLANES - 1
    for j in (range(nt - 1, -1, -1) if up else range(nt)):
        tiles[j] = b[j] if carry is None else b[j] + a[j] * carry
        carry = tiles[j][edge:edge + 1, :]
    return jnp.concatenate(tiles, axis=0)


@jax.custom_vjp
def _lin_scan(a, b, h0):
    return _scan_tiles(a, b, h0, up=False)


def _lin_scan_fwd(a, b, h0):
    h = _lin_scan(a, b, h0)
    return h, (a, h0, h)


def _lin_scan_bwd(res, g):
    a, h0, h = res
    n = a.shape[0]
    row = _row_iota(a)
    a_next = jnp.where(row < n - 1, _roll_up(a, 1), 0.0)
    gg = _scan_tiles(a_next, g, None, up=True)
    h_prev = jnp.where(row >= 1, _roll_down(h, 1), h0)
    return gg * h_prev, gg, _row(a * gg, 0)


_lin_scan.defvjp(_lin_scan_fwd, _lin_scan_bwd)


@jax.custom_vjp
def _cumsum_rows(x):
    n = x.shape[0]
    row = _row_iota(x)
    s = 1
    while s < n:
        x = x + jnp.where(row >= s, _roll_down(x, s), 0.0)
        s *= 2
    return x


def _cumsum_rows_fwd(x):
    return _cumsum_rows(x), None


def _cumsum_rows_bwd(_, g):
    n = g.shape[0]
    row = _row_iota(g)
    s = 1
    while s < n:
        g = g + jnp.where(row < n - s, _roll_up(g, s), 0.0)
        s *= 2
    return (g,)


_cumsum_rows.defvjp(_cumsum_rows_fwd, _cumsum_rows_bwd)


def _sigmoid(x):
    return jax.nn.sigmoid(x)


def _softplus(x):
    return jnp.maximum(x, 0.0) + jnp.log1p(jnp.exp(-jnp.abs(x)))


def _gelu(x):
    return jax.nn.gelu(x, approximate=True)


def _neg_expm1(x):
    series = -x * (1.0 + x * (0.5 + x * (1.0 / 6.0 + x * (1.0 / 24.0))))
    return jnp.where(x > -0.01, series, 1.0 - jnp.exp(x))


def _rms(x, g):
    return x * lax.rsqrt(jnp.mean(x * x, axis=-1, keepdims=True) + EPS) * g


def _f_rmsnorm(carries, halos, xs, params):
    (h,) = xs
    (g,) = params
    return (), (_rms(h, g),)


def _f_rmsnorm_res(carries, halos, xs, params):
    (h,) = xs
    (g,) = params
    return (), (_rms(h, g), h)


def _f_gmlp(carries, halos, xs, params):
    (za,) = xs
    ln_g, ln_b, w_s, b_st = params
    u, v = _split_cols(_gelu(za), (D, D))
    vc = v - jnp.mean(v, axis=-1, keepdims=True)
    vn = vc * lax.rsqrt(jnp.mean(vc * vc, axis=-1, keepdims=True) + EPS) * ln_g + ln_b
    q = CHUNK
    causal = lax.broadcasted_iota(jnp.int32, (q, q), 0) >= lax.broadcasted_iota(jnp.int32, (q, q), 1)
    mixed = []
    for g, (w, vg) in enumerate(zip(_split_rows(w_s, (q,) * GROUPS_A), _split_cols(vn, (q,) * GROUPS_A))):
        mixed.append(_mm(jnp.where(causal, w, 0.0), vg, False, False) + _col(b_st, g))
    return (), (u * jnp.concatenate(mixed, axis=1),)


def _conv4(halo, x, w, b):
    y = b + _row(w, 3) * x
    for k in range(3):
        y = y + _row(w, k) * _shift_rows(halo, x, 3 - k)
    return y


def _f_lru(carries, halos, xs, params):
    (h0,) = carries
    (halo,) = halos
    xb_pre, gate = xs
    conv_w, conv_b, w_r, b_r, w_i, b_i, lam = params
    xb = _conv4(halo, xb_pre, conv_w, conv_b)
    hd = D // HEADS_B
    r_parts, i_parts = [], []
    heads = (hd,) * HEADS_B
    for xh, wr, wi in zip(_split_cols(xb, heads), _split_rows(w_r, heads), _split_rows(w_i, heads)):
        r_parts.append(_mm(xh, wr, False, False))
        i_parts.append(_mm(xh, wi, False, False))
    r = _sigmoid(jnp.concatenate(r_parts, axis=1) + b_r)
    i = _sigmoid(jnp.concatenate(i_parts, axis=1) + b_i)
    log_a = -LRU_C * r * _softplus(-lam)
    a = jnp.exp(log_a)
    inp = jnp.sqrt(_neg_expm1(2.0 * log_a)) * (i * xb)
    h = _lin_scan(a, inp, h0)
    return (_row(h, h.shape[0] - 1),), (_gelu(gate) * h,)


def _f_ssd(carries, halos, xs, params):
    (st,) = carries
    (halo,) = halos
    z, xbc_pre, dt_raw = xs
    conv_w, conv_b, dt_bias, a_log, d_skip, norm_g = params
    t = z.shape[0]
    xc = _conv4(halo, xbc_pre, conv_w, conv_b)
    xbc = xc * _sigmoid(xc)
    x_all, b_all, c_all = _split_cols(xbc, (D, GROUPS_C * STATE_C, GROUPS_C * STATE_C))
    x_pairs = _split_cols(x_all, (128,) * (HEADS_C // 2))
    b_groups = _split_cols(b_all, (STATE_C,) * GROUPS_C)
    c_groups = _split_cols(c_all, (STATE_C,) * GROUPS_C)
    st_pairs = _split_rows(st, (128,) * (HEADS_C // 2))
    dt = _softplus(dt_raw + dt_bias)
    adt = dt * (-jnp.exp(a_log))
    acs = _cumsum_rows(adt)
    acs_t = acs.T
    a_last = _row(acs, t - 1)
    lo = lax.broadcasted_iota(jnp.int32, (t, 128), 1) < HEAD_DIM_C
    lo_rows = lax.broadcasted_iota(jnp.int32, (128, STATE_C), 0) < HEAD_DIM_C
    causal = lax.broadcasted_iota(jnp.int32, (t, t), 0) >= lax.broadcasted_iota(jnp.int32, (t, t), 1)
    y_parts, st_parts = [], []
    for g in range(GROUPS_C):
        bg, cg = b_groups[g], c_groups[g]
        cb = _mm(cg, bg, False, True)
        for pr in range(2):
            pair = 2 * g + pr
            h0, h1 = 2 * pair, 2 * pair + 1
            x2 = x_pairs[pair]
            ac0, ac1 = _col(acs, h0), _col(acs, h1)
            l0 = jnp.exp(jnp.where(causal, ac0 - _row(acs_t, h0), -1e30))
            l1 = jnp.exp(jnp.where(causal, ac1 - _row(acs_t, h1), -1e30))
            xdt = x2 * jnp.where(lo, _col(dt, h0), _col(dt, h1))
            y_diag = (_mm(cb * l0, jnp.where(lo, xdt, 0.0), False, False)
                      + _mm(cb * l1, jnp.where(lo, 0.0, xdt), False, False))
            al0, al1 = _col(a_last, h0), _col(a_last, h1)
            decay_s = jnp.where(lo, jnp.exp(al0 - ac0), jnp.exp(al1 - ac1))
            s_new = _mm(xdt * decay_s, bg, True, False)
            prev = st_pairs[pair]
            y_off = _mm(cg, prev, False, True) * jnp.where(lo, jnp.exp(ac0), jnp.exp(ac1))
            skip = jnp.where(lo, _col(d_skip, h0), _col(d_skip, h1))
            y_parts.append(y_diag + y_off + x2 * skip)
            st_parts.append(prev * jnp.where(lo_rows, jnp.exp(al0), jnp.exp(al1)) + s_new)
    y = jnp.concatenate(y_parts, axis=1) * (z * _sigmoid(z))
    gw = D // GROUPS_C
    yn = []
    for yg in _split_cols(y, (gw,) * GROUPS_C):
        yn.append(yg * lax.rsqrt(jnp.mean(yg * yg, axis=-1, keepdims=True) + EPS))
    return (jnp.concatenate(st_parts, axis=0),), (jnp.concatenate(yn, axis=1) * norm_g,)


def _f_merge(carries, halos, xs, params):
    pa, pb, pc, g_raw = xs
    (b_gate,) = params
    ga, gb, gc = _split_cols(g_raw, (D, D, D))
    m = (_sigmoid(ga + _row(b_gate, 0)) * pa + _sigmoid(gb + _row(b_gate, 1)) * pb
         + _sigmoid(gc + _row(b_gate, 2)) * pc)
    return (), (m,)


def _f_loss(carries, halos, xs, params):
    (acc,) = carries
    h, target = xs
    (g,) = params
    err = jnp.square(_rms(h, g) - target)
    part = 0.5 * jnp.sum(jnp.mean(err, axis=-1, keepdims=True), axis=0, keepdims=True)
    return (acc + part,), ()


def _x_specs(xs, t, index_of):
    specs = []
    for arr, off, width in xs:
        assert off % width == 0 and off + width <= arr.shape[1]
        specs.append(pl.BlockSpec((t, width), functools.partial(lambda j, cb: (index_of(j), cb), cb=off // width)))
    return specs


def _halo_specs(xs, halo_idx, t, index_of):
    specs = []
    for xi in halo_idx:
        _, off, width = xs[xi]
        specs.append(pl.BlockSpec(
            (HALO, width),
            functools.partial(lambda j, cb: (jnp.maximum(index_of(j) * (t // HALO) - 1, 0), cb), cb=off // width)))
    return specs


def _full_spec(a):
    return pl.BlockSpec(a.shape, functools.partial(lambda j, nd: (0,) * nd, nd=a.ndim))


def _chunk_fwd(f, name, t, xs, params, outs, halo_idx=(), carry_shapes=(), save_carries=False, final_carries=False):
    s = xs[0][0].shape[0]
    n = s // t
    nx, nh, npar, no, nc = len(xs), len(halo_idx), len(params), len(outs), len(carry_shapes)
    ns = nc if save_carries else 0
    nf = nc if final_carries else 0

    def body(*refs):
        x_refs, refs = refs[:nx], refs[nx:]
        h_refs, refs = refs[:nh], refs[nh:]
        p_refs, refs = refs[:npar], refs[npar:]
        y_refs, refs = refs[:no], refs[no:]
        s_refs, refs = refs[:ns], refs[ns:]
        f_refs, c_refs = refs[:nf], refs[nf:]
        i = pl.program_id(0)

        @pl.when(i == 0)
        def _():
            for c in c_refs:
                c[...] = jnp.zeros_like(c)

        carries = tuple(c[...] for c in c_refs)
        for s_ref, c in zip(s_refs, carries):
            s_ref[0] = c
        halos = tuple(jnp.where(i > 0, h[...].astype(F32), 0.0) for h in h_refs)
        new_c, ys = f(carries, halos, tuple(x[...].astype(F32) for x in x_refs), tuple(p[...] for p in p_refs))
        for y_ref, y in zip(y_refs, ys):
            y_ref[...] = y.astype(y_ref.dtype)
        for c, v in zip(c_refs, new_c):
            c[...] = v
        for f_ref, v in zip(f_refs, new_c):
            f_ref[...] = v

    ident = lambda j: j
    out_shape = [jax.ShapeDtypeStruct((s, w), dt) for w, dt in outs]
    out_specs = [pl.BlockSpec((t, w), lambda j: (j, 0)) for w, _ in outs]
    if save_carries:
        out_shape += [jax.ShapeDtypeStruct((n,) + tuple(cs), F32) for cs in carry_shapes]
        out_specs += [pl.BlockSpec((1,) + tuple(cs), lambda j: (j, 0, 0)) for cs in carry_shapes]
    if final_carries:
        out_shape += [jax.ShapeDtypeStruct(tuple(cs), F32) for cs in carry_shapes]
        out_specs += [pl.BlockSpec(tuple(cs), lambda j: (0, 0)) for cs in carry_shapes]
    res = pl.pallas_call(
        body, name=name, grid=(n,),
        in_specs=_x_specs(xs, t, ident) + _halo_specs(xs, halo_idx, t, ident) + [_full_spec(p) for p in params],
        out_specs=out_specs, out_shape=out_shape,
        scratch_shapes=[pltpu.VMEM(tuple(cs), F32) for cs in carry_shapes],
        compiler_params=_cparams("arbitrary"),
    )(*[x[0] for x in xs], *[xs[xi][0] for xi in halo_idx], *params)
    return res[:no], res[no:no + ns], res[no + ns:]


def _chunk_bwd(f, name, t, xs, params, dys, dx_dtypes, halo_idx=(), saved=(), carry_seed=None):
    s = xs[0][0].shape[0]
    n = s // t
    nx, nh, npar, nc, ndy = len(xs), len(halo_idx), len(params), len(saved), len(dys)

    def body(*refs):
        x_refs, refs = refs[:nx], refs[nx:]
        h_refs, refs = refs[:nh], refs[nh:]
        p_refs, refs = refs[:npar], refs[npar:]
        s_refs, refs = refs[:nc], refs[nc:]
        dy_refs, refs = refs[:ndy], refs[ndy:]
        dx_refs, refs = refs[:nx], refs[nx:]
        dp_refs, refs = refs[:npar], refs[npar:]
        dc_refs, dh_refs = refs[:nc], refs[nc:]
        j = pl.program_id(0)
        i = n - 1 - j

        @pl.when(j == 0)
        def _():
            for dc in dc_refs:
                dc[...] = jnp.zeros_like(dc) if carry_seed is None else carry_seed(dc.shape)
            for r in dh_refs + dp_refs:
                r[...] = jnp.zeros_like(r)

        carries = tuple(s_ref[0] for s_ref in s_refs)
        halos = tuple(jnp.where(i > 0, h[...].astype(F32), 0.0) for h in h_refs)
        x_vals = tuple(x[...].astype(F32) for x in x_refs)
        p_vals = tuple(p[...] for p in p_refs)
        _, vjp = jax.vjp(f, carries, halos, x_vals, p_vals)
        d_car, d_hal, d_xs, d_par = vjp((tuple(dc[...] for dc in dc_refs), tuple(d[...].astype(F32) for d in dy_refs)))
        d_xs = list(d_xs)
        for k, xi in enumerate(halo_idx):
            w = xs[xi][2]
            d_xs[xi] = d_xs[xi] + jnp.concatenate([jnp.zeros((t - HALO, w), F32), dh_refs[k][...]], axis=0)
            dh_refs[k][...] = jnp.where(i > 0, d_hal[k], 0.0)
        for dx_ref, dx in zip(dx_refs, d_xs):
            dx_ref[...] = dx.astype(dx_ref.dtype)
        for dp_ref, dp in zip(dp_refs, d_par):
            dp_ref[...] += dp
        for dc, v in zip(dc_refs, d_car):
            dc[...] = v

    rev = lambda j: n - 1 - j
    in_specs = (_x_specs(xs, t, rev) + _halo_specs(xs, halo_idx, t, rev) + [_full_spec(p) for p in params]
                + [pl.BlockSpec((1,) + a.shape[1:], lambda j: (n - 1 - j, 0, 0)) for a in saved]
                + [pl.BlockSpec((t, d.shape[1]), lambda j: (n - 1 - j, 0)) for d in dys])
    out_shape = ([jax.ShapeDtypeStruct((s, w), dt) for (_, _, w), dt in zip(xs, dx_dtypes)]
                 + [jax.ShapeDtypeStruct(p.shape, F32) for p in params])
    out_specs = ([pl.BlockSpec((t, w), lambda j: (n - 1 - j, 0)) for _, _, w in xs] + [_full_spec(p) for p in params])
    res = pl.pallas_call(
        body, name=name, grid=(n,), in_specs=in_specs, out_specs=out_specs, out_shape=out_shape,
        scratch_shapes=([pltpu.VMEM(a.shape[1:], F32) for a in saved]
                        + [pltpu.VMEM((HALO, xs[xi][2]), F32) for xi in halo_idx]),
        compiler_params=_cparams("arbitrary"),
    )(*[x[0] for x in xs], *[xs[xi][0] for xi in halo_idx], *params, *saved, *dys)
    return res[:nx], res[nx:]


def _tile(dim, pref):
    for cand in pref:
        if dim % cand == 0:
            return cand
    return dim


def _matmul(a, b, name, ta=False, tb=False, outs=(F32,), epilogue=None, extras=(), row_params=()):
    m, k = (a.shape[1], a.shape[0]) if ta else a.shape
    n = b.shape[0] if tb else b.shape[1]
    tn = _tile(n, (1152, 1024, 512, 256, 128))
    tk = _tile(k, (1024, 1152, 512, 256, 128))
    nk = k // tk
    tall = nk == 1 and a.dtype == BF16 and not extras and not ta
    tm = _tile(m, ((2048,) if tall else ()) + (1024, 1152, 512, 256, 128))
    ne, no = len(extras) + len(row_params), len(outs)
    ca, cb = (0 if ta else 1), (1 if tb else 0)

    def body(*refs):
        a_ref, b_ref = refs[:2]
        e_refs = refs[2:2 + ne]
        o_refs = refs[2 + ne:2 + ne + no]

        acc = refs[-1]
        kk = pl.program_id(2)

        @pl.when(kk == 0)
        def _():
            acc[...] = jnp.zeros_like(acc)

        acc[...] += _dg(_bf(a_ref[...]), _bf(b_ref[...]), ca, cb)

        @pl.when(kk == nk - 1)
        def _():
            res = acc[...]
            vals = (res,) if epilogue is None else epilogue(res, *[e[...] for e in e_refs])
            for o_ref, v in zip(o_refs, vals):
                o_ref[...] = v.astype(o_ref.dtype)

    a_spec = pl.BlockSpec((tk, tm), lambda i, j, kk: (kk, i)) if ta else pl.BlockSpec((tm, tk), lambda i, j, kk: (i, kk))
    b_spec = pl.BlockSpec((tn, tk), lambda i, j, kk: (j, kk)) if tb else pl.BlockSpec((tk, tn), lambda i, j, kk: (kk, j))
    mn_spec = pl.BlockSpec((tm, tn), lambda i, j, kk: (i, j))
    res = pl.pallas_call(
        body, name=name, grid=(m // tm, n // tn, nk),
        in_specs=[a_spec, b_spec] + [mn_spec] * len(extras)
        + [pl.BlockSpec((1, tn), lambda i, j, kk: (0, j))] * len(row_params),
        out_specs=[mn_spec] * no,
        out_shape=[jax.ShapeDtypeStruct((m, n), dt) for dt in outs],
        scratch_shapes=[pltpu.VMEM((tm, tn), F32)],
        compiler_params=_cparams("parallel", "parallel", "arbitrary"),
    )(a, b, *extras, *row_params)
    return res if no > 1 else res[0]


def _elementwise_block(r, c):
    if r % 8 == 0 and r >= 8:
        return _tile(r, (256, 128, 64, 32, 16, 8)), c
    return r, _tile(c, (256, 128))


PIECE_TILE = 1024


def _piece_steps(pieces):
    steps, s0 = [], 0
    for a in pieces:
        wt = min(a.shape[1], PIECE_TILE)
        assert a.shape[1] % wt == 0
        steps.append((s0, a.shape[1] // wt, wt))
        s0 += a.shape[1] // wt
    return steps, s0


def _matmul_pieces(pieces, b, name):
    steps, n_steps = _piece_steps(pieces)
    s_rows, n = pieces[0].shape[0], b.shape[1]
    tm = _tile(s_rows, (1024, 512, 256, 128))
    tail_rows = steps[-1][2]
    full_steps = n_steps - 1 if tail_rows < PIECE_TILE else n_steps
    b_tail = b[full_steps * PIECE_TILE:]
    np_ = len(pieces)

    def body(*refs):
        a_refs, b_ref, tail_ref, o_ref, acc = refs[:np_], refs[np_], refs[np_ + 1], refs[np_ + 2], refs[np_ + 3]
        s = pl.program_id(1)

        @pl.when(s == 0)
        def _():
            acc[...] = jnp.zeros_like(acc)

        for a_ref, (s0, ns, wt) in zip(a_refs, steps):
            @pl.when((s >= s0) & (s < s0 + ns))
            def _(a_ref=a_ref, wt=wt):
                rhs = b_ref[...] if wt == PIECE_TILE else tail_ref[...]
                acc[...] += _dg(_bf(a_ref[...]), _bf(rhs), 1, 0)

        @pl.when(s == n_steps - 1)
        def _():
            o_ref[...] = acc[...]

    n_rows = s_rows // tm

    def a_index(i, s, s0, ns):
        done = s >= s0 + ns
        return jnp.minimum(i + done.astype(jnp.int32), n_rows - 1), jnp.where(done, 0, jnp.clip(s - s0, 0, ns - 1))

    a_specs = [pl.BlockSpec((tm, wt), functools.partial(a_index, s0=s0, ns=ns)) for s0, ns, wt in steps]
    return pl.pallas_call(
        body, name=name, grid=(n_rows, n_steps),
        in_specs=a_specs + [pl.BlockSpec((PIECE_TILE, n), lambda i, s: (jnp.minimum(s, full_steps - 1), 0)),
                            pl.BlockSpec(b_tail.shape, lambda i, s: (0, 0))],
        out_specs=pl.BlockSpec((tm, n), lambda i, s: (i, 0)), out_shape=jax.ShapeDtypeStruct((s_rows, n), F32),
        scratch_shapes=[pltpu.VMEM((tm, n), F32)],
        compiler_params=_cparams("parallel", "arbitrary"),
    )(*pieces, b, b_tail)


def _matmul_pieces_t(pieces, b, name):
    steps, n_steps = _piece_steps(pieces)
    s_rows, n = b.shape
    tk = _tile(s_rows, (1024, 512, 256, 128))
    nk = s_rows // tk
    total = sum(a.shape[1] for a in pieces)
    np_ = len(pieces)

    def body(*refs):
        a_refs, b_ref, o_ref, acc = refs[:np_], refs[np_], refs[np_ + 1], refs[np_ + 2]
        s, kk = pl.program_id(0), pl.program_id(1)

        @pl.when(kk == 0)
        def _():
            acc[...] = jnp.zeros_like(acc)

        for a_ref, (s0, ns, wt) in zip(a_refs, steps):
            @pl.when((s >= s0) & (s < s0 + ns))
            def _(a_ref=a_ref, wt=wt):
                acc[0:wt, :] += _dg(_bf(a_ref[...]), _bf(b_ref[...]), 0, 0)

        @pl.when(kk == nk - 1)
        def _():
            o_ref[...] = acc[...]

    def a_index(s, kk, s0, ns):
        active = (s >= s0) & (s < s0 + ns)
        return jnp.where(active, kk, jnp.where(s < s0, 0, nk - 1)), jnp.clip(s - s0, 0, ns - 1)

    a_specs = [pl.BlockSpec((tk, wt), functools.partial(a_index, s0=s0, ns=ns)) for s0, ns, wt in steps]
    return pl.pallas_call(
        body, name=name, grid=(n_steps, nk),
        in_specs=a_specs + [pl.BlockSpec((tk, n), lambda s, kk: (kk, 0))],
        out_specs=pl.BlockSpec((PIECE_TILE, n), lambda s, kk: (s, 0)), out_shape=jax.ShapeDtypeStruct((total, n), F32),
        scratch_shapes=[pltpu.VMEM((PIECE_TILE, n), F32)],
        compiler_params=_cparams("parallel", "arbitrary"),
    )(*pieces, b)


def _adamw_math(g, w, m, v):
    m_new = ADAM_B1 * m + (1.0 - ADAM_B1) * g
    v_new = ADAM_B2 * v + (1.0 - ADAM_B2) * jnp.square(g)
    m_hat = m_new / (1.0 - ADAM_B1 ** ADAM_STEP)
    v_hat = v_new / (1.0 - ADAM_B2 ** ADAM_STEP)
    return -ADAM_LR * (m_hat / (jnp.sqrt(v_hat) + ADAM_EPS) + ADAM_WD * w), m_new, v_new


def _adamw(parts, w, m, v, name):
    nl, r, c = w.shape
    k = parts[0].shape[0]
    tr = _tile(r, (128, 64, 32, 16, 8))
    nb = r // tr

    def body(*refs):
        p_refs, (w_ref, m_ref, v_ref), outs = refs[:nl], refs[nl:nl + 3], refs[nl + 3:]
        layer = pl.program_id(0)
        for q in range(nl):
            @pl.when(layer == q)
            def _(q=q):
                g = p_refs[q][0]
                for j in range(1, k):
                    g = g + p_refs[q][j]
                vals = (g,) + _adamw_math(g, w_ref[0], m_ref[0], v_ref[0])
                for o_ref, val in zip(outs, vals):
                    o_ref[0] = val

    spec = pl.BlockSpec((1, tr, c), lambda l, i: (l, i, 0))
    part_specs = [pl.BlockSpec((k, tr, c), functools.partial(
        lambda l, i, q: (0, jnp.where(l == q, i, jnp.where(l < q, 0, nb - 1)), 0), q=q)) for q in range(nl)]
    return pl.pallas_call(
        body, name=name, grid=(nl, nb), in_specs=part_specs + [spec] * 3,
        out_specs=[spec] * 4, out_shape=[jax.ShapeDtypeStruct((nl, r, c), F32)] * 4,
        compiler_params=_cparams("arbitrary", "arbitrary"),
    )(*parts, w, m, v)


def _adamw_transposed(grads, w, m, v, name):
    nl, r, c = w.shape
    tc = 64
    views = [jnp.transpose(a, (2, 0, 1)) for a in (w, m, v)]

    def body(*refs):
        g_refs, (w_ref, m_ref, v_ref), outs = refs[:nl], refs[nl:nl + 3], refs[nl + 3:]
        for l in range(nl):
            g = g_refs[l][...]
            vals = (g,) + _adamw_math(g, w_ref[:, l, :], m_ref[:, l, :], v_ref[:, l, :])
            for o_ref, val in zip(outs, vals):
                o_ref[:, l, :] = val

    spec = pl.BlockSpec((tc, nl, r), lambda i: (i, 0, 0))
    res = pl.pallas_call(
        body, name=name, grid=(pl.cdiv(c, tc),), in_specs=[pl.BlockSpec((tc, r), lambda i: (i, 0))] * nl + [spec] * 3,
        out_specs=[spec] * 4, out_shape=[jax.ShapeDtypeStruct((c, nl, r), F32)] * 4,
        compiler_params=_cparams("parallel"),
    )(*grads, *views)
    return [jnp.transpose(a, (1, 2, 0)) for a in res]


def _adamw_small(gathered, ws, ms, vs, loss_terms, name):
    n = len(ws)

    def device_sum(ref):
        s = ref[0]
        for j in range(1, N_DEV):
            s = s + ref[j]
        return s

    def body(*refs):
        g_refs, w_refs, m_refs, v_refs = refs[:n], refs[n:2 * n], refs[2 * n:3 * n], refs[3 * n:4 * n]
        loss_ref, outs, loss_out = refs[4 * n], refs[4 * n + 1:-1], refs[-1]
        for i in range(n):
            g = device_sum(g_refs[i])
            vals = (g,) + _adamw_math(g, w_refs[i][...], m_refs[i][...], v_refs[i][...])
            for kind, val in enumerate(vals):
                outs[kind * n + i][...] = val
        loss_out[...] = device_sum(loss_ref)

    res = pl.pallas_call(
        body, name=name,
        out_shape=[jax.ShapeDtypeStruct(a.shape, F32) for _ in range(4) for a in ws] + [
            jax.ShapeDtypeStruct(loss_terms.shape[1:], F32)],
        compiler_params=pltpu.CompilerParams(vmem_limit_bytes=VMEM_LIMIT),
    )(*gathered, *ws, *ms, *vs, loss_terms)
    return [res[kind * n:(kind + 1) * n] for kind in range(4)], res[-1]


ANY = pl.BlockSpec(memory_space=pl.ANY)


def _place():
    return lax.axis_index("x"), lax.axis_index("y"), lax.axis_index("c")


def _handshake(peers):
    barrier = pltpu.get_barrier_semaphore()
    for peer in peers:
        pl.semaphore_signal(barrier, inc=1, device_id=peer, device_id_type=MESH)
    pl.semaphore_wait(barrier, len(peers))


def _comm_call(body, name, inputs, out_shape, scratch, sequencer_id=None, after=()):
    if sequencer_id is None:
        return pl.pallas_call(body, name=name, out_shape=out_shape, in_specs=[ANY] * len(inputs),
                              out_specs=[ANY] * len(out_shape), scratch_shapes=scratch)(*inputs)
    n_in, n_after = len(inputs), len(after)

    def sequencer_body(*refs):
        body(*refs[:n_in], *refs[n_in + n_after:])

    return pl.kernel(
        sequencer_body, out_type=out_shape, mesh=plsc.ScalarSubcoreMesh(axis_name="sequencer", num_cores=1),
        scratch_types=scratch, compiler_params=pltpu.CompilerParams(collective_id=sequencer_id), name=name,
    )(*inputs, *after)


def _all_gather(blocks, name, sequencer_id=None, after=()):
    n = len(blocks)

    def body(*refs):
        x_refs, out_refs = refs[:n], refs[n:2 * n]
        send_sems, recv_sems, local_sems = refs[2 * n:]
        x, y, c = _place()
        me, sibling = (x, y, c), (x, y, 1 - c)
        chips = [(1 - x, y), (x, 1 - y), (1 - x, 1 - y)]
        if sequencer_id is not None:
            _handshake([sibling] + [(*chip, c) for chip in chips])

        def slot(a, px, py, pc):
            return out_refs[a].at[4 * px + 2 * py + pc]

        def copy(a, k, blk, to, src=None):
            return pltpu.make_async_remote_copy(
                src_ref=slot(a, *blk) if src is None else src, dst_ref=slot(a, *blk),
                send_sem=send_sems.at[7 * a + k], recv_sem=recv_sems.at[7 * a + k], device_id=to, device_id_type=MESH)

        mine = [pltpu.make_async_copy(x_refs[a], slot(a, *me), local_sems.at[a]) for a in range(n)]
        first = []
        for a in range(n):
            mine[a].start()
            first.append(copy(a, 0, me, sibling, src=x_refs[a]))
            first += [copy(a, 1 + j, me, (*chip, c), src=x_refs[a]) for j, chip in enumerate(chips)]
        for cp in first:
            cp.start()
        passed = []
        for j, chip in enumerate(chips):
            for a in range(n):
                copy(a, 1 + j, (*chip, c), me).wait_recv()
                passed.append(copy(a, 4 + j, (*chip, c), sibling))
                passed[-1].start()
        for a in range(n):
            copy(a, 0, sibling, me).wait_recv()
            for j, chip in enumerate(chips):
                copy(a, 4 + j, (*chip, 1 - c), me).wait_recv()
        for cp in first + passed:
            cp.wait_send()
        for cp in mine:
            cp.wait()

    return _comm_call(
        body, name, blocks, [jax.ShapeDtypeStruct((N_DEV,) + b.shape, b.dtype) for b in blocks],
        [pltpu.SemaphoreType.DMA((7 * n,)), pltpu.SemaphoreType.DMA((7 * n,)), pltpu.SemaphoreType.DMA((n,))],
        sequencer_id, after)


def _exchange_sibling(gs, name, sequencer_id=None, after=()):
    n = len(gs)

    def body(*refs):
        g_refs, out_refs = refs[:n], refs[n:2 * n]
        send_sems, recv_sems = refs[2 * n:]
        x, y, c = _place()
        if sequencer_id is not None:
            _handshake([(x, y, 1 - c)])
        copies = [pltpu.make_async_remote_copy(
            src_ref=g_refs[a].at[2 * k + 1 - c], dst_ref=out_refs[a].at[k], send_sem=send_sems.at[4 * a + k],
            recv_sem=recv_sems.at[4 * a + k], device_id=(x, y, 1 - c), device_id_type=MESH)
            for a in range(n) for k in range(4)]
        for cp in copies:
            cp.start()
        for cp in copies:
            cp.wait()

    return _comm_call(body, name, gs, [jax.ShapeDtypeStruct((4,) + g.shape[1:], g.dtype) for g in gs],
                      [pltpu.SemaphoreType.DMA((4 * n,)), pltpu.SemaphoreType.DMA((4 * n,))], sequencer_id, after)


def _other_chips():
    x, y = lax.axis_index("x"), lax.axis_index("y")
    return jnp.stack([2 * (1 - x) + y, 2 * x + 1 - y, 2 * (1 - x) + 1 - y]).astype(jnp.int32)


def _add_sibling(gs, r1s, name, after=()):
    n = len(gs)
    _, r, w = gs[0].shape
    tr, tc = _elementwise_block(r, w)
    chips = _other_chips()
    slabs = 2 * chips + lax.axis_index("c").astype(jnp.int32)

    def body(slab_ref, chip_ref, *refs):
        g_refs, r_refs, o_refs = refs[:n], refs[n:2 * n], refs[2 * n + len(after):]
        for g_ref, r_ref, o_ref in zip(g_refs, r_refs, o_refs):
            o_ref[...] = (g_ref[...] + r_ref[...]).astype(BF16)

    return pl.pallas_call(
        body, name=name, out_shape=[jax.ShapeDtypeStruct((3, r, w), BF16)] * n,
        grid_spec=pltpu.PrefetchScalarGridSpec(
            num_scalar_prefetch=2, grid=(3, r // tr, w // tc),
            in_specs=[pl.BlockSpec((1, tr, tc), lambda k, i, j, slab_ref, chip_ref: (slab_ref[k], i, j))] * n
            + [pl.BlockSpec((1, tr, tc), lambda k, i, j, slab_ref, chip_ref: (chip_ref[k], i, j))] * n
            + [ANY] * len(after),
            out_specs=[pl.BlockSpec((1, tr, tc), lambda k, i, j, slab_ref, chip_ref: (k, i, j))] * n),
        compiler_params=_cparams("parallel", "parallel", "parallel"),
    )(slabs, chips, *gs, *r1s, *after)


def _exchange_chips(ps, name, sequencer_id=None, after=()):
    n = len(ps)

    def body(*refs):
        p_refs, out_refs = refs[:n], refs[n:2 * n]
        send_sems, recv_sems = refs[2 * n:]
        x, y, c = _place()
        chips = [(1 - x, y), (x, 1 - y), (1 - x, 1 - y)]
        if sequencer_id is not None:
            _handshake([(*chip, c) for chip in chips])
        copies = [pltpu.make_async_remote_copy(
            src_ref=p_refs[a].at[j], dst_ref=out_refs[a].at[j], send_sem=send_sems.at[3 * a + j],
            recv_sem=recv_sems.at[3 * a + j], device_id=(px, py, c), device_id_type=MESH)
            for a in range(n) for j, (px, py) in enumerate(chips)]
        for cp in copies:
            cp.start()
        for cp in copies:
            cp.wait()

    return _comm_call(body, name, ps, [jax.ShapeDtypeStruct(p.shape, p.dtype) for p in ps],
                      [pltpu.SemaphoreType.DMA((3 * n,)), pltpu.SemaphoreType.DMA((3 * n,))], sequencer_id, after)


def _sum_chips(gs, r1s, others, name):
    n = len(gs)
    _, r, c = gs[0].shape
    tr, tc = _elementwise_block(r, c)
    chip = 2 * lax.axis_index("x") + lax.axis_index("y")
    place = jnp.stack([2 * chip + lax.axis_index("c"), chip]).astype(jnp.int32)

    def body(place_ref, *refs):
        for g_ref, r_ref, others_ref, o_ref in zip(refs[:n], refs[n:2 * n], refs[2 * n:3 * n], refs[3 * n:]):
            s = g_ref[0] + r_ref[0]
            for j in range(3):
                s = s + others_ref[j].astype(F32)
            o_ref[...] = s

    return pl.pallas_call(
        body, name=name, out_shape=[jax.ShapeDtypeStruct((r, c), F32)] * n,
        grid_spec=pltpu.PrefetchScalarGridSpec(
            num_scalar_prefetch=1, grid=(r // tr, c // tc),
            in_specs=[pl.BlockSpec((1, tr, tc), lambda i, j, place_ref: (place_ref[0], i, j))] * n
            + [pl.BlockSpec((1, tr, tc), lambda i, j, place_ref: (place_ref[1], i, j))] * n
            + [pl.BlockSpec((3, tr, tc), lambda i, j, place_ref: (0, i, j))] * n,
            out_specs=[pl.BlockSpec((tr, tc), lambda i, j, place_ref: (i, j))] * n),
        compiler_params=_cparams("parallel", "parallel"),
    )(place, *gs, *r1s, *others)


def _reorder_in_proj(wt):
    za_zb, zc, xbc, dt, gates = (wt[:4096], wt[4096:5120], wt[5120:7168], wt[7168:7184], wt[7184:])
    return jnp.concatenate([za_zb, xbc, gates, zc, dt, jnp.zeros((DT_PAD - 16, wt.shape[1]), wt.dtype)], axis=0)


def _restore_in_proj(wt):
    return jnp.concatenate([wt[:4096], wt[OFF_ZC:OFF_ZC + W_ZC], wt[OFF_XBC:OFF_XBC + W_XBC],
                            wt[OFF_DT:OFF_DT + 16], wt[OFF_GATE:OFF_GATE + W_GATE]], axis=0)


def _lanes_from_devices(g):
    return jnp.moveaxis(g, 0, 1).reshape(g.shape[1], N_DEV * g.shape[2])


def _lanes_to_devices(a):
    return jnp.moveaxis(a.reshape(a.shape[0], N_DEV, a.shape[1] // N_DEV), 1, 0)


def _pad_lanes(a, width):
    return jnp.pad(a, ((0, 0), (0, width - a.shape[1])))


BIG = ("w_in", "w_branch_a", "w_branch_b", "w_branch_c", "w_out", "w_mlp_up", "w_mlp_down")
SMALL_SHARDED = ("b_gate", "lru_conv_w", "ssd_conv_w")
REPLICATED = ("norm_mix_g", "gmlp_ln_g", "gmlp_ln_b", "gmlp_w_s", "gmlp_b_s", "lru_conv_b", "lru_w_r", "lru_b_r",
              "lru_w_i", "lru_b_i", "lru_lambda", "ssd_conv_b", "ssd_dt_bias", "ssd_a_log", "ssd_d", "ssd_norm_g",
              "norm_mlp_g", "final_norm_g")
WEIGHTS = ("norm_mix_g", "w_in", "b_gate", "gmlp_ln_g", "gmlp_ln_b", "gmlp_w_s", "gmlp_b_s", "lru_conv_w", "lru_conv_b",
           "lru_w_r", "lru_b_r", "lru_w_i", "lru_b_i", "lru_lambda", "ssd_conv_w", "ssd_conv_b", "ssd_dt_bias",
           "ssd_a_log", "ssd_d", "ssd_norm_g", "w_branch_a", "w_branch_b", "w_branch_c", "w_out", "norm_mlp_g",
           "w_mlp_up", "w_mlp_down", "final_norm_g")
TRANSPOSED = ("w_in", "w_mlp_up")
SMALL_MATRICES = ("gmlp_w_s", "lru_w_r", "lru_w_i")
SMALL_VECTORS = tuple(n for n in REPLICATED if n not in SMALL_MATRICES)
GRADIENT_GROUPS = {"mlp": ("w_mlp_up", "w_mlp_down"), "mix": ("w_branch_a", "w_branch_b", "w_branch_c", "w_out"),
                   "in": ("w_in",)}


def _layer_params(full, l):
    row = lambda a: a.reshape(1, -1)
    return dict(
        norm_mix_g=row(full["norm_mix_g"][l]), norm_mlp_g=row(full["norm_mlp_g"][l]),
        gmlp=(row(full["gmlp_ln_g"][l]), row(full["gmlp_ln_b"][l]), full["gmlp_w_s"][l].reshape(GROUPS_A * CHUNK, CHUNK),
              full["gmlp_b_s"][l].T),
        lru=(full["lru_conv_w"][l], row(full["lru_conv_b"][l]), full["lru_w_r"][l].reshape(D, D // HEADS_B),
             row(full["lru_b_r"][l]), full["lru_w_i"][l].reshape(D, D // HEADS_B), row(full["lru_b_i"][l]),
             row(full["lru_lambda"][l])),
        ssd=(full["ssd_conv_w"][l], row(full["ssd_conv_b"][l]), _pad_lanes(row(full["ssd_dt_bias"][l]), DT_PAD),
             _pad_lanes(row(full["ssd_a_log"][l]), DT_PAD), _pad_lanes(row(full["ssd_d"][l]), DT_PAD),
             row(full["ssd_norm_g"][l])),
        b_gate=full["b_gate"][l],
    )


def _forward_layer(h, hn, p, wb, l, after_mixers=None, next_norm_g=None):
    tag = f"l{l}"
    t_row = 512
    if hn is None:
        (hn,), _, _ = _chunk_fwd(_f_rmsnorm, f"norm_mix_{tag}", t_row, [(h, 0, D)], [p["norm_mix_g"]], [(D, BF16)])
    proj = _matmul(hn, wb["w_in"], f"in_proj_{tag}", tb=True)
    (ya,), _, _ = _chunk_fwd(_f_gmlp, f"gmlp_{tag}", CHUNK, [(proj, OFF_ZA, W_ZA)], p["gmlp"], [(D, BF16)])
    lru_xs = [(proj, OFF_ZB, D), (proj, OFF_ZB + D, D)]
    (yb,), lru_saved, _ = _chunk_fwd(_f_lru, f"lru_{tag}", CHUNK, lru_xs, p["lru"], [(D, BF16)], halo_idx=(0,),
                                     carry_shapes=[(1, D)], save_carries=True)
    ssd_xs = [(proj, OFF_ZC, W_ZC), (proj, OFF_XBC, W_XBC), (proj, OFF_DT, W_DT)]
    (yc,), ssd_saved, _ = _chunk_fwd(_f_ssd, f"ssd_{tag}", CHUNK, ssd_xs, p["ssd"], [(D, BF16)], halo_idx=(1,),
                                     carry_shapes=[(HEADS_C * HEAD_DIM_C, STATE_C)], save_carries=True)
    if after_mixers is not None:
        after_mixers(yc)
    pa = _matmul(ya, wb["w_branch_a"], f"branch_a_{tag}")
    pb = _matmul(yb, wb["w_branch_b"], f"branch_b_{tag}")
    pc = _matmul(yc, wb["w_branch_c"], f"branch_c_{tag}")
    merge_xs = [(pa, 0, D), (pb, 0, D), (pc, 0, D), (proj, OFF_GATE, W_GATE)]
    (merged,), _, _ = _chunk_fwd(_f_merge, f"merge_{tag}", t_row, merge_xs, [p["b_gate"]], [(D, BF16)])

    def add_and_norm(acc, res, g):
        s = acc + res
        return s, _rms(s, g)

    h_mid, hn2 = _matmul(merged, wb["w_out"], f"out_proj_{tag}", outs=(F32, BF16), epilogue=add_and_norm, extras=(h,),
                         row_params=(p["norm_mlp_g"],))

    def relu_sq(acc):
        r = jnp.maximum(acc, 0.0)
        return r, r * r

    relu_up, act = _matmul(hn2, wb["w_mlp_up"], f"mlp_up_{tag}", tb=True, outs=(F32, BF16), epilogue=relu_sq)
    if next_norm_g is None:
        hn_out = None
        h_out = _matmul(act, wb["w_mlp_down"], f"mlp_down_{tag}", epilogue=lambda acc, res: (acc + res,), extras=(h_mid,))
    else:
        h_out, hn_out = _matmul(act, wb["w_mlp_down"], f"mlp_down_{tag}", outs=(F32, BF16), epilogue=add_and_norm,
                                extras=(h_mid,), row_params=(next_norm_g,))
    saved = dict(h=h, hn=hn, proj=proj, ya=ya, yb=yb, yc=yc, lru_saved=lru_saved, ssd_saved=ssd_saved, pa=pa, pb=pb,
                 pc=pc, merged=merged, h_mid=h_mid, hn2=hn2, relu_up=relu_up, act=act, lru_xs=lru_xs, ssd_xs=ssd_xs,
                 merge_xs=merge_xs)
    return h_out, hn_out, saved


def _backward_layer(dh, sv, p, wb, l):
    tag = f"l{l}"
    t_row = 512
    g = {}
    d_up = _matmul(dh, wb["w_mlp_down"], f"d_act_{tag}", tb=True, outs=(BF16,),
                   epilogue=lambda acc, r: (acc * (2.0 * r),), extras=(sv["relu_up"],))
    g["w_mlp_down"] = _matmul(sv["act"], dh, f"dw_mlp_down_{tag}", ta=True)
    g["w_mlp_up"] = _matmul(d_up, sv["hn2"], f"dw_mlp_up_{tag}", ta=True)
    d_hn2 = _matmul(d_up, wb["w_mlp_up"], f"d_hn2_{tag}")
    (d_mid,), (g["norm_mlp_g"],) = _chunk_bwd(_f_rmsnorm_res, f"norm_mlp_bwd_{tag}", t_row, [(sv["h_mid"], 0, D)],
                                              [p["norm_mlp_g"]], [d_hn2, dh], [F32])
    d_merged = _matmul(d_mid, wb["w_out"], f"d_merged_{tag}", tb=True)
    g["w_out"] = _matmul(sv["merged"], d_mid, f"dw_out_{tag}", ta=True)
    (d_pa, d_pb, d_pc, d_gate), (g["b_gate"],) = _chunk_bwd(
        _f_merge, f"merge_bwd_{tag}", t_row, sv["merge_xs"], [p["b_gate"]], [d_merged], [BF16] * 4)
    d_y = {}
    for br, d_p, y in (("a", d_pa, sv["ya"]), ("b", d_pb, sv["yb"]), ("c", d_pc, sv["yc"])):
        g[f"w_branch_{br}"] = _matmul(y, d_p, f"dw_branch_{br}_{tag}", ta=True)
        d_y[br] = _matmul(d_p, wb[f"w_branch_{br}"], f"d_y{br}_{tag}", tb=True)
    (d_za,), g_gmlp = _chunk_bwd(_f_gmlp, f"gmlp_bwd_{tag}", CHUNK, [(sv["proj"], OFF_ZA, W_ZA)], p["gmlp"],
                                 [d_y["a"]], [BF16])
    (d_xb, d_gt), g_lru = _chunk_bwd(_f_lru, f"lru_bwd_{tag}", CHUNK, sv["lru_xs"], p["lru"], [d_y["b"]], [BF16] * 2,
                                     halo_idx=(0,), saved=sv["lru_saved"])
    (d_zc, d_xbc, d_dt), g_ssd = _chunk_bwd(_f_ssd, f"ssd_bwd_{tag}", CHUNK, sv["ssd_xs"], p["ssd"], [d_y["c"]],
                                            [BF16] * 3, halo_idx=(1,), saved=sv["ssd_saved"])
    d_proj = [d_za, d_xb, d_gt, d_xbc, d_gate, d_zc, d_dt]
    g["w_in"] = _matmul_pieces_t(d_proj, sv["hn"], f"dw_in_{tag}")
    d_hn = _matmul_pieces(d_proj, wb["w_in"], f"d_hn_{tag}")
    (d_h,), (g["norm_mix_g"],) = _chunk_bwd(_f_rmsnorm_res, f"norm_mix_bwd_{tag}", t_row, [(sv["h"], 0, D)],
                                            [p["norm_mix_g"]], [d_hn, d_mid], [F32])
    g["w_in"] = _restore_in_proj(g["w_in"])
    for n in BIG:
        g[n] = g[n].reshape(N_DEV, g[n].shape[0] // N_DEV, g[n].shape[1])
    g["gmlp_ln_g"], g["gmlp_ln_b"], g["gmlp_w_s"] = g_gmlp[:3]
    g["gmlp_b_s"] = g_gmlp[3].T
    (g["lru_conv_w"], g["lru_conv_b"], g["lru_w_r"], g["lru_b_r"], g["lru_w_i"], g["lru_b_i"], g["lru_lambda"]) = g_lru
    g["ssd_conv_w"], g["ssd_conv_b"] = g_ssd[:2]
    g["ssd_dt_bias"], g["ssd_a_log"], g["ssd_d"] = (a[:, :HEADS_C] for a in g_ssd[2:5])
    g["ssd_norm_g"] = g_ssd[5]
    g["later"] = {"mlp": d_merged, "mix": d_za, "in": d_h}
    return d_h, g


LOSS_ROWS = 512


def _loss_and_grads(h, target, full, layer_weights, first_gathered):
    seq = h.shape[0]
    layer_p = [_layer_params(full, l) for l in range(DEPTH)]
    layer_w = [layer_weights(0, (first_gathered,))]
    saved = []
    hn = None
    for l in range(DEPTH):
        fetch_next = next_norm_g = None
        if l + 1 < DEPTH:
            fetch_next = lambda y, l=l: layer_w.append(layer_weights(l + 1, (layer_w[l]["w_mlp_down"], y)))
            next_norm_g = layer_p[l + 1]["norm_mix_g"]
        h, hn, sv = _forward_layer(h, hn, layer_p[l], layer_w[l], l, fetch_next, next_norm_g)
        saved.append(sv)
    final_g = full["final_norm_g"].reshape(1, D)
    loss_xs = [(h, 0, D), (target, 0, D)]
    t_loss = min(LOSS_ROWS, seq)
    _, _, (loss_acc,) = _chunk_fwd(_f_loss, "loss", t_loss, loss_xs, [final_g], [], carry_shapes=[(1, 128)],
                                   final_carries=True)
    zero_acc = jnp.zeros((seq // t_loss, 1, 128), F32)
    seed = lambda shape: (lax.broadcasted_iota(jnp.int32, shape, 1) == 0).astype(F32)
    (dh, _), (g_final,) = _chunk_bwd(_f_loss, "loss_bwd", t_loss, loss_xs, [final_g], [], [F32, F32], saved=[zero_acc],
                                     carry_seed=seed)
    layer_g = [None] * DEPTH
    for l in reversed(range(DEPTH)):
        dh, layer_g[l] = _backward_layer(dh, saved[l], layer_p[l], layer_w[l], l)
    return loss_acc, dh, layer_g, g_final


def _small_views(d):
    views = {n: d[n] for n in REPLICATED}
    views["gmlp_b_s"] = d["gmlp_b_s"].reshape(DEPTH * GROUPS_A, CHUNK)
    views["final_norm_g"] = d["final_norm_g"].reshape(1, D)
    for n in SMALL_MATRICES:
        views[n] = d[n].reshape(DEPTH * D, D // HEADS_B)
    return views


def kernel(x, norm_mix_g, w_in, b_gate, gmlp_ln_g, gmlp_ln_b, gmlp_w_s, gmlp_b_s, lru_conv_w, lru_conv_b, lru_w_r, lru_b_r, lru_w_i, lru_b_i, lru_lambda, ssd_conv_w, ssd_conv_b, ssd_dt_bias, ssd_a_log, ssd_d, ssd_norm_g, w_branch_a, w_branch_b, w_branch_c, w_out, norm_mlp_g, w_mlp_up, w_mlp_down, final_norm_g, loss_target, m_norm_mix_g, m_w_in, m_b_gate, m_gmlp_ln_g, m_gmlp_ln_b, m_gmlp_w_s, m_gmlp_b_s, m_lru_conv_w, m_lru_conv_b, m_lru_w_r, m_lru_b_r, m_lru_w_i, m_lru_b_i, m_lru_lambda, m_ssd_conv_w, m_ssd_conv_b, m_ssd_dt_bias, m_ssd_a_log, m_ssd_d, m_ssd_norm_g, m_w_branch_a, m_w_branch_b, m_w_branch_c, m_w_out, m_norm_mlp_g, m_w_mlp_up, m_w_mlp_down, m_final_norm_g, v_norm_mix_g, v_w_in, v_b_gate, v_gmlp_ln_g, v_gmlp_ln_b, v_gmlp_w_s, v_gmlp_b_s, v_lru_conv_w, v_lru_conv_b, v_lru_w_r, v_lru_b_r, v_lru_w_i, v_lru_b_i, v_lru_lambda, v_ssd_conv_w, v_ssd_conv_b, v_ssd_dt_bias, v_ssd_a_log, v_ssd_d, v_ssd_norm_g, v_w_branch_a, v_w_branch_b, v_w_branch_c, v_w_out, v_norm_mlp_g, v_w_mlp_up, v_w_mlp_down, v_final_norm_g):
    args = locals()
    w = {n: args[n] for n in WEIGHTS}
    m = {n: args["m_" + n] for n in WEIGHTS}
    v = {n: args["v_" + n] for n in WEIGHTS}
    seq = x.shape[1]
    h = x.reshape(seq, D)
    target = loss_target.reshape(seq, D)

    def shard_on_wire(n, l):
        return (w[n][l].T if n in TRANSPOSED else w[n][l]).astype(BF16)

    first = _all_gather([shard_on_wire("w_in", 0)] + [w[n] for n in SMALL_SHARDED], "gather_weights_first")
    full = {n: w[n] for n in REPLICATED}
    for n, g in zip(SMALL_SHARDED, first[1:]):
        full[n] = jnp.stack([_lanes_from_devices(g[:, l]) for l in range(DEPTH)])

    def layer_weights(l, after):
        have = {"w_in": first[0]} if l == 0 else {}
        names = [n for n in BIG if n not in have]
        later = _all_gather([shard_on_wire(n, l) for n in names], f"gather_weights_l{l}", SEQ_GATHER + l, after=after)
        have.update(zip(names, later))
        wl = {n: have[n].reshape(-1, D) for n in BIG}
        wl["w_in"] = _reorder_in_proj(wl["w_in"])
        return wl

    loss_local, dh, layer_g, g_final = _loss_and_grads(h, target, full, layer_weights, first[0])
    grad_x = dh.reshape(x.shape)
    out = {}
    kinds = ("grad", "delta", "new_m", "new_v")

    sequencer_before = {}

    def reduce_scatter(slabs, tag, on_sequencer, later=()):
        keys = list(slabs)
        ids = (SEQ_TO_SIBLING, SEQ_TO_CHIPS) if on_sequencer else (None, None)
        from_sibling = _exchange_sibling([slabs[k] for k in keys], f"grads_to_sibling_{tag}", ids[0],
                                         sequencer_before.get("sibling", ()))
        from_sibling = dict(zip(keys, from_sibling))
        same_shape = {}
        for k in keys:
            same_shape.setdefault(slabs[k].shape[1:], []).append(k)
        chip_sums = {}
        for (r, c), ks in same_shape.items():
            sums = _add_sibling([slabs[k] for k in ks], [from_sibling[k] for k in ks], f"add_sibling_{tag}_{r}x{c}", later)
            chip_sums.update(zip(ks, sums))
        from_chips = _exchange_chips([chip_sums[k] for k in keys], f"grads_to_chips_{tag}", ids[1],
                                     sequencer_before.get("chips", ()))
        if on_sequencer:
            sequencer_before["sibling"], sequencer_before["chips"] = (from_sibling[keys[0]],), (from_chips[0],)
        from_chips = dict(zip(keys, from_chips))
        summed = {}
        for (r, c), ks in same_shape.items():
            sums = _sum_chips([slabs[k] for k in ks], [from_sibling[k] for k in ks], [from_chips[k] for k in ks],
                              f"sum_chips_{tag}_{r}x{c}")
            summed.update(zip(ks, sums))
        return summed

    small = {}
    for n in SMALL_SHARDED:
        small[n, None] = jnp.concatenate([_lanes_to_devices(layer_g[l][n]) for l in range(DEPTH)], axis=1)
    g_small = {n: jnp.concatenate([layer_g[l][n] for l in range(DEPTH)], axis=0) for n in REPLICATED[:-1]}
    g_small["final_norm_g"] = g_final
    for n in SMALL_MATRICES:
        small[n, None] = g_small[n].reshape(N_DEV, -1, g_small[n].shape[-1])
    reduced = {}
    groups = [(l, grp) for l in range(DEPTH - 1, -1, -1) for grp in GRADIENT_GROUPS]
    for l, grp in groups:
        if (l, grp) == groups[-1]:
            reduced.update(reduce_scatter(small, "small", True, (layer_g[l][GRADIENT_GROUPS[grp][0]],)))
        slabs = {(n, l): layer_g[l][n] for n in GRADIENT_GROUPS[grp]}
        reduced.update(reduce_scatter(slabs, f"{grp}_l{l}", True, (layer_g[l]["later"][grp],)))
    for n in BIG:
        grads = [reduced[n, l] for l in range(DEPTH)]
        if n in TRANSPOSED and w[n].shape[-1] % LANES:
            res = _adamw_transposed(grads, w[n], m[n], v[n], f"adamw_{n}")
        else:
            res = _adamw([(g.T if n in TRANSPOSED else g)[None] for g in grads], w[n], m[n], v[n], f"adamw_{n}")
        for kind, a in zip(kinds, res):
            out[kind, n] = a
    for n in SMALL_SHARDED:
        one = lambda a: a.reshape((1, -1, a.shape[-1]))
        for kind, a in zip(kinds, _adamw([reduced[n, None][None]], one(w[n]), one(m[n]), one(v[n]), f"adamw_{n}")):
            out[kind, n] = a.reshape(w[n].shape)

    to_gather = [reduced[n, None] if n in SMALL_MATRICES else g_small[n] for n in REPLICATED]
    *g_gathered, loss_terms = _all_gather(to_gather + [loss_local], "gather_small_grads", SEQ_GATHER_SMALL)
    g_all = dict(zip(REPLICATED, g_gathered))
    wv, mv, vv = _small_views(w), _small_views(m), _small_views(v)
    res, loss_sum = _adamw_small([g_all[n] for n in SMALL_VECTORS],
                                 *[[d[n] for n in SMALL_VECTORS] for d in (wv, mv, vv)], loss_terms, "adamw_vectors")
    loss = loss_sum[0, 0]
    for kind, arrays in zip(kinds, res):
        for n, a in zip(SMALL_VECTORS, arrays):
            out[kind, n] = a.reshape(w[n].shape)
    for n in SMALL_MATRICES:
        g_full = g_all[n].reshape((1, 1) + wv[n].shape)
        for kind, a in zip(kinds, _adamw([g_full[0]], wv[n][None], mv[n][None], vv[n][None], f"adamw_{n}")):
            out[kind, n] = a.reshape(w[n].shape)

    return (loss, grad_x, *[out[kind, n] for kind in kinds for n in WEIGHTS])
```

```python
import functools

import jax
import jax.numpy as jnp
from jax import lax
from jax.experimental import pallas as pl
from jax.experimental.pallas import tpu as pltpu
from jax.experimental.pallas import tpu_sc as plsc

F32 = jnp.float32
BF16 = jnp.bfloat16
MESH = pl.DeviceIdType.MESH

D = 1024
DEPTH = 2
EPS = 1e-6
CHUNK = 128
GROUPS_A = 8
HEADS_B = 8
LRU_C = 8.0
HEADS_C = 16
HEAD_DIM_C = 64
GROUPS_C = 4
STATE_C = 128
DT_PAD = 128
OFF_ZA, W_ZA = 0, 2048
OFF_ZB, W_ZB = 2048, 2048
OFF_XBC, W_XBC = 4096, 2048
OFF_GATE, W_GATE = 6144, 3072
OFF_ZC, W_ZC = 9216, 1024
OFF_DT, W_DT = 10240, DT_PAD
D_IN_PAD = 10368
N_DEV = 8
SEQ_GATHER = 1
SEQ_TO_SIBLING = SEQ_GATHER + DEPTH
SEQ_TO_CHIPS = SEQ_TO_SIBLING + 1
SEQ_GATHER_SMALL = SEQ_TO_CHIPS + 1

ADAM_LR = 0.001
ADAM_B1 = 0.9
ADAM_B2 = 0.999
ADAM_EPS = 1e-08
ADAM_WD = 0.01
ADAM_STEP = 10

VMEM_LIMIT = 56 * 1024 * 1024
HALO = 8


def _cparams(*sem):
    return pltpu.CompilerParams(dimension_semantics=sem, vmem_limit_bytes=VMEM_LIMIT)


def _bf(x):
    return x.astype(BF16)


def _dg(a, b, ca, cb):
    return lax.dot_general(a, b, (((ca,), (cb,)), ((), ())), preferred_element_type=F32)


@functools.partial(jax.custom_vjp, nondiff_argnums=(2, 3))
def _mm(a, b, ta, tb):
    return _dg(_bf(a), _bf(b), 0 if ta else 1, 1 if tb else 0)


def _mm_fwd(a, b, ta, tb):
    return _mm(a, b, ta, tb), (a, b)


def _mm_bwd(ta, tb, res, g):
    a, b = res
    ma = 1 if ta else 0
    nb = 0 if tb else 1
    gb, ab, bb = _bf(g), _bf(a), _bf(b)
    da = _dg(bb, gb, nb, 1) if ta else _dg(gb, bb, 1, nb)
    db = _dg(gb, ab, 0, ma) if tb else _dg(ab, gb, ma, 0)
    return da.astype(a.dtype), db.astype(b.dtype)


_mm.defvjp(_mm_fwd, _mm_bwd)


def _slices(x, sizes, axis):
    out, lo = [], 0
    for size in sizes:
        out.append(lax.slice_in_dim(x, lo, lo + size, axis=axis))
        lo += size
    return tuple(out)


@functools.partial(jax.custom_vjp, nondiff_argnums=(1,))
def _split_cols(x, widths):
    return _slices(x, widths, 1)


_split_cols.defvjp(lambda x, widths: (_slices(x, widths, 1), None),
                   lambda widths, _, gs: (jnp.concatenate(gs, axis=1),))


@functools.partial(jax.custom_vjp, nondiff_argnums=(1,))
def _split_rows(x, heights):
    return _slices(x, heights, 0)


_split_rows.defvjp(lambda x, heights: (_slices(x, heights, 0), None),
                   lambda heights, _, gs: (jnp.concatenate(gs, axis=0),))


def _col(x, j):
    lane = lax.broadcasted_iota(jnp.int32, x.shape, 1)
    return jnp.sum(jnp.where(lane == j, x, 0.0), axis=1, keepdims=True)


def _row(x, i):
    r = lax.broadcasted_iota(jnp.int32, x.shape, 0)
    return jnp.sum(jnp.where(r == i, x, 0.0), axis=0, keepdims=True)


def _roll_down(x, s):
    return pltpu.roll(x, s, 0)


def _roll_up(x, s):
    return pltpu.roll(x, x.shape[0] - s, 0)


def _row_iota(x):
    return lax.broadcasted_iota(jnp.int32, x.shape, 0)


@functools.partial(jax.custom_vjp, nondiff_argnums=(2,))
def _shift_rows(halo, x, s):
    if s == 0:
        return x
    return _roll_down(jnp.concatenate([halo, x], axis=0), s)[HALO:]


def _shift_rows_fwd(halo, x, s):
    return _shift_rows(halo, x, s), None


def _shift_rows_bwd(s, _, g):
    if s == 0:
        return jnp.zeros((HALO, g.shape[1]), g.dtype), g
    ge = jnp.concatenate([jnp.zeros((HALO, g.shape[1]), g.dtype), g], axis=0)
    de = _roll_up(ge, s)
    return de[:HALO], de[HALO:]


_shift_rows.defvjp(_shift_rows_fwd, _shift_rows_bwd)


SUBLANES = 8
LANES = 128


def _scan_tiles(a, b, carry, up):
    n, c = a.shape
    nt = n // SUBLANES
    a = a.reshape(nt, SUBLANES, c)
    b = b.reshape(nt, SUBLANES, c)
    sub = lax.broadcasted_iota(jnp.int32, a.shape, 1)
    s = 1
    while s < SUBLANES:
        keep = (sub < SUBLANES - s) if up else (sub >= s)
        shift = SUBLANES - s if up else s
        a_sh = jnp.where(keep, pltpu.roll(a, shift, 1), 1.0)
        b_sh = jnp.where(keep, pltpu.roll(b, shift, 1), 0.0)
        b = a * b_sh + b
        a = a * a_sh
        s *= 2
    tiles = [None] * nt
    edge = 0 if up else SUBLANES - 1
    for j in (range(nt - 1, -1, -1) if up else range(nt)):
        tiles[j] = b[j] if carry is None else b[j] + a[j] * carry
        carry = tiles[j][edge:edge + 1, :]
    return jnp.concatenate(tiles, axis=0)


@jax.custom_vjp
def _lin_scan(a, b, h0):
    return _scan_tiles(a, b, h0, up=False)


def _lin_scan_fwd(a, b, h0):
    h = _lin_scan(a, b, h0)
    return h, (a, h0, h)


def _lin_scan_bwd(res, g):
    a, h0, h = res
    n = a.shape[0]
    row = _row_iota(a)
    a_next = jnp.where(row < n - 1, _roll_up(a, 1), 0.0)
    gg = _scan_tiles(a_next, g, None, up=True)
    h_prev = jnp.where(row >= 1, _roll_down(h, 1), h0)
    return gg * h_prev, gg, _row(a * gg, 0)


_lin_scan.defvjp(_lin_scan_fwd, _lin_scan_bwd)


@jax.custom_vjp
def _cumsum_rows(x):
    n = x.shape[0]
    row = _row_iota(x)
    s = 1
    while s < n:
        x = x + jnp.where(row >= s, _roll_down(x, s), 0.0)
        s *= 2
    return x


def _cumsum_rows_fwd(x):
    return _cumsum_rows(x), None


def _cumsum_rows_bwd(_, g):
    n = g.shape[0]
    row = _row_iota(g)
    s = 1
    while s < n:
        g = g + jnp.where(row < n - s, _roll_up(g, s), 0.0)
        s *= 2
    return (g,)


_cumsum_rows.defvjp(_cumsum_rows_fwd, _cumsum_rows_bwd)


def _sigmoid(x):
    return jax.nn.sigmoid(x)


def _softplus(x):
    return jnp.maximum(x, 0.0) + jnp.log1p(jnp.exp(-jnp.abs(x)))


def _gelu(x):
    return jax.nn.gelu(x, approximate=True)


def _neg_expm1(x):
    series = -x * (1.0 + x * (0.5 + x * (1.0 / 6.0 + x * (1.0 / 24.0))))
    return jnp.where(x > -0.01, series, 1.0 - jnp.exp(x))


def _rms(x, g):
    return x * lax.rsqrt(jnp.mean(x * x, axis=-1, keepdims=True) + EPS) * g


def _f_rmsnorm(carries, halos, xs, params):
    (h,) = xs
    (g,) = params
    return (), (_rms(h, g),)


def _f_rmsnorm_res(carries, halos, xs, params):
    (h,) = xs
    (g,) = params
    return (), (_rms(h, g), h)


def _f_gmlp(carries, halos, xs, params):
    (za,) = xs
    ln_g, ln_b, w_s, b_st = params
    u, v = _split_cols(_gelu(za), (D, D))
    vc = v - jnp.mean(v, axis=-1, keepdims=True)
    vn = vc * lax.rsqrt(jnp.mean(vc * vc, axis=-1, keepdims=True) + EPS) * ln_g + ln_b
    q = CHUNK
    causal = lax.broadcasted_iota(jnp.int32, (q, q), 0) >= lax.broadcasted_iota(jnp.int32, (q, q), 1)
    mixed = []
    for g, (w, vg) in enumerate(zip(_split_rows(w_s, (q,) * GROUPS_A), _split_cols(vn, (q,) * GROUPS_A))):
        mixed.append(_mm(jnp.where(causal, w, 0.0), vg, False, False) + _col(b_st, g))
    return (), (u * jnp.concatenate(mixed, axis=1),)


def _conv4(halo, x, w, b):
    y = b + _row(w, 3) * x
    for k in range(3):
        y = y + _row(w, k) * _shift_rows(halo, x, 3 - k)
    return y


def _f_lru(carries, halos, xs, params):
    (h0,) = carries
    (halo,) = halos
    xb_pre, gate = xs
    conv_w, conv_b, w_r, b_r, w_i, b_i, lam = params
    xb = _conv4(halo, xb_pre, conv_w, conv_b)
    hd = D // HEADS_B
    r_parts, i_parts = [], []
    heads = (hd,) * HEADS_B
    for xh, wr, wi in zip(_split_cols(xb, heads), _split_rows(w_r, heads), _split_rows(w_i, heads)):
        r_parts.append(_mm(xh, wr, False, False))
        i_parts.append(_mm(xh, wi, False, False))
    r = _sigmoid(jnp.concatenate(r_parts, axis=1) + b_r)
    i = _sigmoid(jnp.concatenate(i_parts, axis=1) + b_i)
    log_a = -LRU_C * r * _softplus(-lam)
    a = jnp.exp(log_a)
    inp = jnp.sqrt(_neg_expm1(2.0 * log_a)) * (i * xb)
    h = _lin_scan(a, inp, h0)
    return (_row(h, h.shape[0] - 1),), (_gelu(gate) * h,)


def _f_ssd(carries, halos, xs, params):
    (st,) = carries
    (halo,) = halos
    z, xbc_pre, dt_raw = xs
    conv_w, conv_b, dt_bias, a_log, d_skip, norm_g = params
    t = z.shape[0]
    xc = _conv4(halo, xbc_pre, conv_w, conv_b)
    xbc = xc * _sigmoid(xc)
    x_all, b_all, c_all = _split_cols(xbc, (D, GROUPS_C * STATE_C, GROUPS_C * STATE_C))
    x_pairs = _split_cols(x_all, (128,) * (HEADS_C // 2))
    b_groups = _split_cols(b_all, (STATE_C,) * GROUPS_C)
    c_groups = _split_cols(c_all, (STATE_C,) * GROUPS_C)
    st_pairs = _split_rows(st, (128,) * (HEADS_C // 2))
    dt = _softplus(dt_raw + dt_bias)
    adt = dt * (-jnp.exp(a_log))
    acs = _cumsum_rows(adt)
    acs_t = acs.T
    a_last = _row(acs, t - 1)
    lo = lax.broadcasted_iota(jnp.int32, (t, 128), 1) < HEAD_DIM_C
    lo_rows = lax.broadcasted_iota(jnp.int32, (128, STATE_C), 0) < HEAD_DIM_C
    causal = lax.broadcasted_iota(jnp.int32, (t, t), 0) >= lax.broadcasted_iota(jnp.int32, (t, t), 1)
    y_parts, st_parts = [], []
    for g in range(GROUPS_C):
        bg, cg = b_groups[g], c_groups[g]
        cb = _mm(cg, bg, False, True)
        for pr in range(2):
            pair = 2 * g + pr
            h0, h1 = 2 * pair, 2 * pair + 1
            x2 = x_pairs[pair]
            ac0, ac1 = _col(acs, h0), _col(acs, h1)
            l0 = jnp.exp(jnp.where(causal, ac0 - _row(acs_t, h0), -1e30))
            l1 = jnp.exp(jnp.where(causal, ac1 - _row(acs_t, h1), -1e30))
            xdt = x2 * jnp.where(lo, _col(dt, h0), _col(dt, h1))
            y_diag = (_mm(cb * l0, jnp.where(lo, xdt, 0.0), False, False)
                      + _mm(cb * l1, jnp.where(lo, 0.0, xdt), False, False))
            al0, al1 = _col(a_last, h0), _col(a_last, h1)
            decay_s = jnp.where(lo, jnp.exp(al0 - ac0), jnp.exp(al1 - ac1))
            s_new = _mm(xdt * decay_s, bg, True, False)
            prev = st_pairs[pair]
            y_off = _mm(cg, prev, False, True) * jnp.where(lo, jnp.exp(ac0), jnp.exp(ac1))
            skip = jnp.where(lo, _col(d_skip, h0), _col(d_skip, h1))
            y_parts.append(y_diag + y_off + x2 * skip)
            st_parts.append(prev * jnp.where(lo_rows, jnp.exp(al0), jnp.exp(al1)) + s_new)
    y = jnp.concatenate(y_parts, axis=1) * (z * _sigmoid(z))
    gw = D // GROUPS_C
    yn = []
    for yg in _split_cols(y, (gw,) * GROUPS_C):
        yn.append(yg * lax.rsqrt(jnp.mean(yg * yg, axis=-1, keepdims=True) + EPS))
    return (jnp.concatenate(st_parts, axis=0),), (jnp.concatenate(yn, axis=1) * norm_g,)


def _f_merge(carries, halos, xs, params):
    pa, pb, pc, g_raw = xs
    (b_gate,) = params
    ga, gb, gc = _split_cols(g_raw, (D, D, D))
    m = (_sigmoid(ga + _row(b_gate, 0)) * pa + _sigmoid(gb + _row(b_gate, 1)) * pb
         + _sigmoid(gc + _row(b_gate, 2)) * pc)
    return (), (m,)


def _f_loss(carries, halos, xs, params):
    (acc,) = carries
    h, target = xs
    (g,) = params
    err = jnp.square(_rms(h, g) - target)
    part = 0.5 * jnp.sum(jnp.mean(err, axis=-1, keepdims=True), axis=0, keepdims=True)
    return (acc + part,), ()


def _x_specs(xs, t, index_of):
    specs = []
    for arr, off, width in xs:
        assert off % width == 0 and off + width <= arr.shape[1]
        specs.append(pl.BlockSpec((t, width), functools.partial(lambda j, cb: (index_of(j), cb), cb=off // width)))
    return specs


def _halo_specs(xs, halo_idx, t, index_of):
    specs = []
    for xi in halo_idx:
        _, off, width = xs[xi]
        specs.append(pl.BlockSpec(
            (HALO, width),
            functools.partial(lambda j, cb: (jnp.maximum(index_of(j) * (t // HALO) - 1, 0), cb), cb=off // width)))
    return specs


def _full_spec(a):
    return pl.BlockSpec(a.shape, functools.partial(lambda j, nd: (0,) * nd, nd=a.ndim))


def _chunk_fwd(f, name, t, xs, params, outs, halo_idx=(), carry_shapes=(), save_carries=False, final_carries=False):
    s = xs[0][0].shape[0]
    n = s // t
    nx, nh, npar, no, nc = len(xs), len(halo_idx), len(params), len(outs), len(carry_shapes)
    ns = nc if save_carries else 0
    nf = nc if final_carries else 0

    def body(*refs):
        x_refs, refs = refs[:nx], refs[nx:]
        h_refs, refs = refs[:nh], refs[nh:]
        p_refs, refs = refs[:npar], refs[npar:]
        y_refs, refs = refs[:no], refs[no:]
        s_refs, refs = refs[:ns], refs[ns:]
        f_refs, c_refs = refs[:nf], refs[nf:]
        i = pl.program_id(0)

        @pl.when(i == 0)
        def _():
            for c in c_refs:
                c[...] = jnp.zeros_like(c)

        carries = tuple(c[...] for c in c_refs)
        for s_ref, c in zip(s_refs, carries):
            s_ref[0] = c
        halos = tuple(jnp.where(i > 0, h[...].astype(F32), 0.0) for h in h_refs)
        new_c, ys = f(carries, halos, tuple(x[...].astype(F32) for x in x_refs), tuple(p[...] for p in p_refs))
        for y_ref, y in zip(y_refs, ys):
            y_ref[...] = y.astype(y_ref.dtype)
        for c, v in zip(c_refs, new_c):
            c[...] = v
        for f_ref, v in zip(f_refs, new_c):
            f_ref[...] = v

    ident = lambda j: j
    out_shape = [jax.ShapeDtypeStruct((s, w), dt) for w, dt in outs]
    out_specs = [pl.BlockSpec((t, w), lambda j: (j, 0)) for w, _ in outs]
    if save_carries:
        out_shape += [jax.ShapeDtypeStruct((n,) + tuple(cs), F32) for cs in carry_shapes]
        out_specs += [pl.BlockSpec((1,) + tuple(cs), lambda j: (j, 0, 0)) for cs in carry_shapes]
    if final_carries:
        out_shape += [jax.ShapeDtypeStruct(tuple(cs), F32) for cs in carry_shapes]
        out_specs += [pl.BlockSpec(tuple(cs), lambda j: (0, 0)) for cs in carry_shapes]
    res = pl.pallas_call(
        body, name=name, grid=(n,),
        in_specs=_x_specs(xs, t, ident) + _halo_specs(xs, halo_idx, t, ident) + [_full_spec(p) for p in params],
        out_specs=out_specs, out_shape=out_shape,
        scratch_shapes=[pltpu.VMEM(tuple(cs), F32) for cs in carry_shapes],
        compiler_params=_cparams("arbitrary"),
    )(*[x[0] for x in xs], *[xs[xi][0] for xi in halo_idx], *params)
    return res[:no], res[no:no + ns], res[no + ns:]


def _chunk_bwd(f, name, t, xs, params, dys, dx_dtypes, halo_idx=(), saved=(), carry_seed=None):
    s = xs[0][0].shape[0]
    n = s // t
    nx, nh, npar, nc, ndy = len(xs), len(halo_idx), len(params), len(saved), len(dys)

    def body(*refs):
        x_refs, refs = refs[:nx], refs[nx:]
        h_refs, refs = refs[:nh], refs[nh:]
        p_refs, refs = refs[:npar], refs[npar:]
        s_refs, refs = refs[:nc], refs[nc:]
        dy_refs, refs = refs[:ndy], refs[ndy:]
        dx_refs, refs = refs[:nx], refs[nx:]
        dp_refs, refs = refs[:npar], refs[npar:]
        dc_refs, dh_refs = refs[:nc], refs[nc:]
        j = pl.program_id(0)
        i = n - 1 - j

        @pl.when(j == 0)
        def _():
            for dc in dc_refs:
                dc[...] = jnp.zeros_like(dc) if carry_seed is None else carry_seed(dc.shape)
            for r in dh_refs + dp_refs:
                r[...] = jnp.zeros_like(r)

        carries = tuple(s_ref[0] for s_ref in s_refs)
        halos = tuple(jnp.where(i > 0, h[...].astype(F32), 0.0) for h in h_refs)
        x_vals = tuple(x[...].astype(F32) for x in x_refs)
        p_vals = tuple(p[...] for p in p_refs)
        _, vjp = jax.vjp(f, carries, halos, x_vals, p_vals)
        d_car, d_hal, d_xs, d_par = vjp((tuple(dc[...] for dc in dc_refs), tuple(d[...].astype(F32) for d in dy_refs)))
        d_xs = list(d_xs)
        for k, xi in enumerate(halo_idx):
            w = xs[xi][2]
            d_xs[xi] = d_xs[xi] + jnp.concatenate([jnp.zeros((t - HALO, w), F32), dh_refs[k][...]], axis=0)
            dh_refs[k][...] = jnp.where(i > 0, d_hal[k], 0.0)
        for dx_ref, dx in zip(dx_refs, d_xs):
            dx_ref[...] = dx.astype(dx_ref.dtype)
        for dp_ref, dp in zip(dp_refs, d_par):
            dp_ref[...] += dp
        for dc, v in zip(dc_refs, d_car):
            dc[...] = v

    rev = lambda j: n - 1 - j
    in_specs = (_x_specs(xs, t, rev) + _halo_specs(xs, halo_idx, t, rev) + [_full_spec(p) for p in params]
                + [pl.BlockSpec((1,) + a.shape[1:], lambda j: (n - 1 - j, 0, 0)) for a in saved]
                + [pl.BlockSpec((t, d.shape[1]), lambda j: (n - 1 - j, 0)) for d in dys])
    out_shape = ([jax.ShapeDtypeStruct((s, w), dt) for (_, _, w), dt in zip(xs, dx_dtypes)]
                 + [jax.ShapeDtypeStruct(p.shape, F32) for p in params])
    out_specs = ([pl.BlockSpec((t, w), lambda j: (n - 1 - j, 0)) for _, _, w in xs] + [_full_spec(p) for p in params])
    res = pl.pallas_call(
        body, name=name, grid=(n,), in_specs=in_specs, out_specs=out_specs, out_shape=out_shape,
        scratch_shapes=([pltpu.VMEM(a.shape[1:], F32) for a in saved]
                        + [pltpu.VMEM((HALO, xs[xi][2]), F32) for xi in halo_idx]),
        compiler_params=_cparams("arbitrary"),
    )(*[x[0] for x in xs], *[xs[xi][0] for xi in halo_idx], *params, *saved, *dys)
    return res[:nx], res[nx:]


def _tile(dim, pref):
    for cand in pref:
        if dim % cand == 0:
            return cand
    return dim


def _matmul(a, b, name, ta=False, tb=False, outs=(F32,), epilogue=None, extras=(), row_params=()):
    m, k = (a.shape[1], a.shape[0]) if ta else a.shape
    n = b.shape[0] if tb else b.shape[1]
    tn = _tile(n, (1152, 1024, 512, 256, 128))
    tk = _tile(k, (1024, 1152, 512, 256, 128))
    nk = k // tk
    tall = nk == 1 and a.dtype == BF16 and not extras and not ta and n // tn >= 4
    tm = _tile(m, ((2048,) if tall else ()) + (1024, 1152, 512, 256, 128))
    ne, no = len(extras) + len(row_params), len(outs)
    ca, cb = (0 if ta else 1), (1 if tb else 0)

    def body(*refs):
        a_ref, b_ref = refs[:2]
        e_refs = refs[2:2 + ne]
        o_refs = refs[2 + ne:2 + ne + no]

        acc = refs[-1]
        kk = pl.program_id(2)

        @pl.when(kk == 0)
        def _():
            acc[...] = jnp.zeros_like(acc)

        acc[...] += _dg(_bf(a_ref[...]), _bf(b_ref[...]), ca, cb)

        @pl.when(kk == nk - 1)
        def _():
            res = acc[...]
            vals = (res,) if epilogue is None else epilogue(res, *[e[...] for e in e_refs])
            for o_ref, v in zip(o_refs, vals):
                o_ref[...] = v.astype(o_ref.dtype)

    a_spec = pl.BlockSpec((tk, tm), lambda i, j, kk: (kk, i)) if ta else pl.BlockSpec((tm, tk), lambda i, j, kk: (i, kk))
    b_spec = pl.BlockSpec((tn, tk), lambda i, j, kk: (j, kk)) if tb else pl.BlockSpec((tk, tn), lambda i, j, kk: (kk, j))
    mn_spec = pl.BlockSpec((tm, tn), lambda i, j, kk: (i, j))
    res = pl.pallas_call(
        body, name=name, grid=(m // tm, n // tn, nk),
        in_specs=[a_spec, b_spec] + [mn_spec] * len(extras)
        + [pl.BlockSpec((1, tn), lambda i, j, kk: (0, j))] * len(row_params),
        out_specs=[mn_spec] * no,
        out_shape=[jax.ShapeDtypeStruct((m, n), dt) for dt in outs],
        scratch_shapes=[pltpu.VMEM((tm, tn), F32)],
        compiler_params=_cparams("parallel", "parallel", "arbitrary"),
    )(a, b, *extras, *row_params)
    return res if no > 1 else res[0]


def _elementwise_block(r, c):
    if r % 8 == 0 and r >= 8:
        return _tile(r, (256, 128, 64, 32, 16, 8)), c
    return r, _tile(c, (256, 128))


PIECE_TILE = 1024


def _piece_steps(pieces):
    steps, s0 = [], 0
    for a in pieces:
        wt = min(a.shape[1], PIECE_TILE)
        assert a.shape[1] % wt == 0
        steps.append((s0, a.shape[1] // wt, wt))
        s0 += a.shape[1] // wt
    return steps, s0


def _matmul_pieces(pieces, b, name):
    steps, n_steps = _piece_steps(pieces)
    s_rows, n = pieces[0].shape[0], b.shape[1]
    tm = _tile(s_rows, (1024, 512, 256, 128))
    tail_rows = steps[-1][2]
    full_steps = n_steps - 1 if tail_rows < PIECE_TILE else n_steps
    b_tail = b[full_steps * PIECE_TILE:]
    np_ = len(pieces)

    def body(*refs):
        a_refs, b_ref, tail_ref, o_ref, acc = refs[:np_], refs[np_], refs[np_ + 1], refs[np_ + 2], refs[np_ + 3]
        s = pl.program_id(1)

        @pl.when(s == 0)
        def _():
            acc[...] = jnp.zeros_like(acc)

        for a_ref, (s0, ns, wt) in zip(a_refs, steps):
            @pl.when((s >= s0) & (s < s0 + ns))
            def _(a_ref=a_ref, wt=wt):
                rhs = b_ref[...] if wt == PIECE_TILE else tail_ref[...]
                acc[...] += _dg(_bf(a_ref[...]), _bf(rhs), 1, 0)

        @pl.when(s == n_steps - 1)
        def _():
            o_ref[...] = acc[...]

    n_rows = s_rows // tm

    def a_index(i, s, s0, ns):
        done = s >= s0 + ns
        return jnp.minimum(i + done.astype(jnp.int32), n_rows - 1), jnp.where(done, 0, jnp.clip(s - s0, 0, ns - 1))

    a_specs = [pl.BlockSpec((tm, wt), functools.partial(a_index, s0=s0, ns=ns)) for s0, ns, wt in steps]
    return pl.pallas_call(
        body, name=name, grid=(n_rows, n_steps),
        in_specs=a_specs + [pl.BlockSpec((PIECE_TILE, n), lambda i, s: (jnp.minimum(s, full_steps - 1), 0)),
                            pl.BlockSpec(b_tail.shape, lambda i, s: (0, 0))],
        out_specs=pl.BlockSpec((tm, n), lambda i, s: (i, 0)), out_shape=jax.ShapeDtypeStruct((s_rows, n), F32),
        scratch_shapes=[pltpu.VMEM((tm, n), F32)],
        compiler_params=_cparams("parallel", "arbitrary"),
    )(*pieces, b, b_tail)


def _matmul_pieces_t(pieces, b, name):
    steps, n_steps = _piece_steps(pieces)
    s_rows, n = b.shape
    tk = _tile(s_rows, (1024, 512, 256, 128))
    nk = s_rows // tk
    total = sum(a.shape[1] for a in pieces)
    np_ = len(pieces)

    def body(*refs):
        a_refs, b_ref, o_ref, acc = refs[:np_], refs[np_], refs[np_ + 1], refs[np_ + 2]
        s, kk = pl.program_id(0), pl.program_id(1)

        @pl.when(kk == 0)
        def _():
            acc[...] = jnp.zeros_like(acc)

        for a_ref, (s0, ns, wt) in zip(a_refs, steps):
            @pl.when((s >= s0) & (s < s0 + ns))
            def _(a_ref=a_ref, wt=wt):
                acc[0:wt, :] += _dg(_bf(a_ref[...]), _bf(b_ref[...]), 0, 0)

        @pl.when(kk == nk - 1)
        def _():
            o_ref[...] = acc[...]

    def a_index(s, kk, s0, ns):
        active = (s >= s0) & (s < s0 + ns)
        return jnp.where(active, kk, jnp.where(s < s0, 0, nk - 1)), jnp.clip(s - s0, 0, ns - 1)

    a_specs = [pl.BlockSpec((tk, wt), functools.partial(a_index, s0=s0, ns=ns)) for s0, ns, wt in steps]
    return pl.pallas_call(
        body, name=name, grid=(n_steps, nk),
        in_specs=a_specs + [pl.BlockSpec((tk, n), lambda s, kk: (kk, 0))],
        out_specs=pl.BlockSpec((PIECE_TILE, n), lambda s, kk: (s, 0)), out_shape=jax.ShapeDtypeStruct((total, n), F32),
        scratch_shapes=[pltpu.VMEM((PIECE_TILE, n), F32)],
        compiler_params=_cparams("parallel", "arbitrary"),
    )(*pieces, b)


def _adamw_math(g, w, m, v):
    m_new = ADAM_B1 * m + (1.0 - ADAM_B1) * g
    v_new = ADAM_B2 * v + (1.0 - ADAM_B2) * jnp.square(g)
    m_hat = m_new / (1.0 - ADAM_B1 ** ADAM_STEP)
    v_hat = v_new / (1.0 - ADAM_B2 ** ADAM_STEP)
    return -ADAM_LR * (m_hat / (jnp.sqrt(v_hat) + ADAM_EPS) + ADAM_WD * w), m_new, v_new


def _adamw(parts, w, m, v, name):
    nl, r, c = w.shape
    k = parts[0].shape[0]
    tr = _tile(r, (128, 64, 32, 16, 8))
    nb = r // tr

    def body(*refs):
        p_refs, (w_ref, m_ref, v_ref), outs = refs[:nl], refs[nl:nl + 3], refs[nl + 3:]
        layer = pl.program_id(0)
        for q in range(nl):
            @pl.when(layer == q)
            def _(q=q):
                g = p_refs[q][0]
                for j in range(1, k):
                    g = g + p_refs[q][j]
                vals = (g,) + _adamw_math(g, w_ref[0], m_ref[0], v_ref[0])
                for o_ref, val in zip(outs, vals):
                    o_ref[0] = val

    spec = pl.BlockSpec((1, tr, c), lambda l, i: (l, i, 0))
    part_specs = [pl.BlockSpec((k, tr, c), functools.partial(
        lambda l, i, q: (0, jnp.where(l == q, i, jnp.where(l < q, 0, nb - 1)), 0), q=q)) for q in range(nl)]
    return pl.pallas_call(
        body, name=name, grid=(nl, nb), in_specs=part_specs + [spec] * 3,
        out_specs=[spec] * 4, out_shape=[jax.ShapeDtypeStruct((nl, r, c), F32)] * 4,
        compiler_params=_cparams("arbitrary", "arbitrary"),
    )(*parts, w, m, v)


def _adamw_transposed(grads, w, m, v, name):
    nl, r, c = w.shape
    tc = 64
    views = [jnp.transpose(a, (2, 0, 1)) for a in (w, m, v)]

    def body(*refs):
        g_refs, (w_ref, m_ref, v_ref), outs = refs[:nl], refs[nl:nl + 3], refs[nl + 3:]
        for l in range(nl):
            g = g_refs[l][...]
            vals = (g,) + _adamw_math(g, w_ref[:, l, :], m_ref[:, l, :], v_ref[:, l, :])
            for o_ref, val in zip(outs, vals):
                o_ref[:, l, :] = val

    spec = pl.BlockSpec((tc, nl, r), lambda i: (i, 0, 0))
    res = pl.pallas_call(
        body, name=name, grid=(pl.cdiv(c, tc),), in_specs=[pl.BlockSpec((tc, r), lambda i: (i, 0))] * nl + [spec] * 3,
        out_specs=[spec] * 4, out_shape=[jax.ShapeDtypeStruct((c, nl, r), F32)] * 4,
        compiler_params=_cparams("parallel"),
    )(*grads, *views)
    return [jnp.transpose(a, (1, 2, 0)) for a in res]


def _adamw_small(gathered, ws, ms, vs, loss_terms, name):
    n = len(ws)

    def device_sum(ref):
        s = ref[0]
        for j in range(1, N_DEV):
            s = s + ref[j]
        return s

    def body(*refs):
        g_refs, w_refs, m_refs, v_refs = refs[:n], refs[n:2 * n], refs[2 * n:3 * n], refs[3 * n:4 * n]
        loss_ref, outs, loss_out = refs[4 * n], refs[4 * n + 1:-1], refs[-1]
        for i in range(n):
            g = device_sum(g_refs[i])
            vals = (g,) + _adamw_math(g, w_refs[i][...], m_refs[i][...], v_refs[i][...])
            for kind, val in enumerate(vals):
                outs[kind * n + i][...] = val
        loss_out[...] = device_sum(loss_ref)

    res = pl.pallas_call(
        body, name=name,
        out_shape=[jax.ShapeDtypeStruct(a.shape, F32) for _ in range(4) for a in ws] + [
            jax.ShapeDtypeStruct(loss_terms.shape[1:], F32)],
        compiler_params=pltpu.CompilerParams(vmem_limit_bytes=VMEM_LIMIT),
    )(*gathered, *ws, *ms, *vs, loss_terms)
    return [res[kind * n:(kind + 1) * n] for kind in range(4)], res[-1]


ANY = pl.BlockSpec(memory_space=pl.ANY)


def _place():
    return lax.axis_index("x"), lax.axis_index("y"), lax.axis_index("c")


def _handshake(peers):
    barrier = pltpu.get_barrier_semaphore()
    for peer in peers:
        pl.semaphore_signal(barrier, inc=1, device_id=peer, device_id_type=MESH)
    pl.semaphore_wait(barrier, len(peers))


def _comm_call(body, name, inputs, out_shape, scratch, sequencer_id=None, after=()):
    if sequencer_id is None:
        return pl.pallas_call(body, name=name, out_shape=out_shape, in_specs=[ANY] * len(inputs),
                              out_specs=[ANY] * len(out_shape), scratch_shapes=scratch)(*inputs)
    n_in, n_after = len(inputs), len(after)

    def sequencer_body(*refs):
        body(*refs[:n_in], *refs[n_in + n_after:])

    return pl.kernel(
        sequencer_body, out_type=out_shape, mesh=plsc.ScalarSubcoreMesh(axis_name="sequencer", num_cores=1),
        scratch_types=scratch, compiler_params=pltpu.CompilerParams(collective_id=sequencer_id), name=name,
    )(*inputs, *after)


def _all_gather(blocks, name, sequencer_id=None, after=()):
    n = len(blocks)

    def body(*refs):
        x_refs, out_refs = refs[:n], refs[n:2 * n]
        send_sems, recv_sems, local_sems = refs[2 * n:]
        x, y, c = _place()
        me, sibling = (x, y, c), (x, y, 1 - c)
        chips = [(1 - x, y), (x, 1 - y), (1 - x, 1 - y)]
        if sequencer_id is not None:
            _handshake([sibling] + [(*chip, c) for chip in chips])

        def slot(a, px, py, pc):
            return out_refs[a].at[4 * px + 2 * py + pc]

        def copy(a, k, blk, to, src=None):
            return pltpu.make_async_remote_copy(
                src_ref=slot(a, *blk) if src is None else src, dst_ref=slot(a, *blk),
                send_sem=send_sems.at[7 * a + k], recv_sem=recv_sems.at[7 * a + k], device_id=to, device_id_type=MESH)

        mine = [pltpu.make_async_copy(x_refs[a], slot(a, *me), local_sems.at[a]) for a in range(n)]
        first = []
        for a in range(n):
            mine[a].start()
            first.append(copy(a, 0, me, sibling, src=x_refs[a]))
            first += [copy(a, 1 + j, me, (*chip, c), src=x_refs[a]) for j, chip in enumerate(chips)]
        for cp in first:
            cp.start()
        passed = []
        for j, chip in enumerate(chips):
            for a in range(n):
                copy(a, 1 + j, (*chip, c), me).wait_recv()
                passed.append(copy(a, 4 + j, (*chip, c), sibling))
                passed[-1].start()
        for a in range(n):
            copy(a, 0, sibling, me).wait_recv()
            for j, chip in enumerate(chips):
                copy(a, 4 + j, (*chip, 1 - c), me).wait_recv()
        for cp in first + passed:
            cp.wait_send()
        for cp in mine:
            cp.wait()

    return _comm_call(
        body, name, blocks, [jax.ShapeDtypeStruct((N_DEV,) + b.shape, b.dtype) for b in blocks],
        [pltpu.SemaphoreType.DMA((7 * n,)), pltpu.SemaphoreType.DMA((7 * n,)), pltpu.SemaphoreType.DMA((n,))],
        sequencer_id, after)


def _exchange_sibling(gs, name, sequencer_id=None, after=()):
    n = len(gs)

    def body(*refs):
        g_refs, out_refs = refs[:n], refs[n:2 * n]
        send_sems, recv_sems = refs[2 * n:]
        x, y, c = _place()
        if sequencer_id is not None:
            _handshake([(x, y, 1 - c)])
        copies = [pltpu.make_async_remote_copy(
            src_ref=g_refs[a].at[2 * k + 1 - c], dst_ref=out_refs[a].at[k], send_sem=send_sems.at[4 * a + k],
            recv_sem=recv_sems.at[4 * a + k], device_id=(x, y, 1 - c), device_id_type=MESH)
            for a in range(n) for k in range(4)]
        for cp in copies:
            cp.start()
        for cp in copies:
            cp.wait()

    return _comm_call(body, name, gs, [jax.ShapeDtypeStruct((4,) + g.shape[1:], g.dtype) for g in gs],
                      [pltpu.SemaphoreType.DMA((4 * n,)), pltpu.SemaphoreType.DMA((4 * n,))], sequencer_id, after)


def _other_chips():
    x, y = lax.axis_index("x"), lax.axis_index("y")
    return jnp.stack([2 * (1 - x) + y, 2 * x + 1 - y, 2 * (1 - x) + 1 - y]).astype(jnp.int32)


def _add_sibling(gs, r1s, name, after=()):
    n = len(gs)
    _, r, w = gs[0].shape
    tr, tc = _elementwise_block(r, w)
    chips = _other_chips()
    slabs = 2 * chips + lax.axis_index("c").astype(jnp.int32)

    def body(slab_ref, chip_ref, *refs):
        g_refs, r_refs, o_refs = refs[:n], refs[n:2 * n], refs[2 * n + len(after):]
        for g_ref, r_ref, o_ref in zip(g_refs, r_refs, o_refs):
            o_ref[...] = (g_ref[...] + r_ref[...]).astype(BF16)

    return pl.pallas_call(
        body, name=name, out_shape=[jax.ShapeDtypeStruct((3, r, w), BF16)] * n,
        grid_spec=pltpu.PrefetchScalarGridSpec(
            num_scalar_prefetch=2, grid=(3, r // tr, w // tc),
            in_specs=[pl.BlockSpec((1, tr, tc), lambda k, i, j, slab_ref, chip_ref: (slab_ref[k], i, j))] * n
            + [pl.BlockSpec((1, tr, tc), lambda k, i, j, slab_ref, chip_ref: (chip_ref[k], i, j))] * n
            + [ANY] * len(after),
            out_specs=[pl.BlockSpec((1, tr, tc), lambda k, i, j, slab_ref, chip_ref: (k, i, j))] * n),
        compiler_params=_cparams("parallel", "parallel", "parallel"),
    )(slabs, chips, *gs, *r1s, *after)


def _exchange_chips(ps, name, sequencer_id=None, after=()):
    n = len(ps)

    def body(*refs):
        p_refs, out_refs = refs[:n], refs[n:2 * n]
        send_sems, recv_sems = refs[2 * n:]
        x, y, c = _place()
        chips = [(1 - x, y), (x, 1 - y), (1 - x, 1 - y)]
        if sequencer_id is not None:
            _handshake([(*chip, c) for chip in chips])
        copies = [pltpu.make_async_remote_copy(
            src_ref=p_refs[a].at[j], dst_ref=out_refs[a].at[j], send_sem=send_sems.at[3 * a + j],
            recv_sem=recv_sems.at[3 * a + j], device_id=(px, py, c), device_id_type=MESH)
            for a in range(n) for j, (px, py) in enumerate(chips)]
        for cp in copies:
            cp.start()
        for cp in copies:
            cp.wait()

    return _comm_call(body, name, ps, [jax.ShapeDtypeStruct(p.shape, p.dtype) for p in ps],
                      [pltpu.SemaphoreType.DMA((3 * n,)), pltpu.SemaphoreType.DMA((3 * n,))], sequencer_id, after)


def _sum_chips(gs, r1s, others, name):
    n = len(gs)
    _, r, c = gs[0].shape
    tr, tc = _elementwise_block(r, c)
    chip = 2 * lax.axis_index("x") + lax.axis_index("y")
    place = jnp.stack([2 * chip + lax.axis_index("c"), chip]).astype(jnp.int32)

    def body(place_ref, *refs):
        for g_ref, r_ref, others_ref, o_ref in zip(refs[:n], refs[n:2 * n], refs[2 * n:3 * n], refs[3 * n:]):
            s = g_ref[0] + r_ref[0]
            for j in range(3):
                s = s + others_ref[j].astype(F32)
            o_ref[...] = s

    return pl.pallas_call(
        body, name=name, out_shape=[jax.ShapeDtypeStruct((r, c), F32)] * n,
        grid_spec=pltpu.PrefetchScalarGridSpec(
            num_scalar_prefetch=1, grid=(r // tr, c // tc),
            in_specs=[pl.BlockSpec((1, tr, tc), lambda i, j, place_ref: (place_ref[0], i, j))] * n
            + [pl.BlockSpec((1, tr, tc), lambda i, j, place_ref: (place_ref[1], i, j))] * n
            + [pl.BlockSpec((3, tr, tc), lambda i, j, place_ref: (0, i, j))] * n,
            out_specs=[pl.BlockSpec((tr, tc), lambda i, j, place_ref: (i, j))] * n),
        compiler_params=_cparams("parallel", "parallel"),
    )(place, *gs, *r1s, *others)


def _reorder_in_proj(wt):
    za_zb, zc, xbc, dt, gates = (wt[:4096], wt[4096:5120], wt[5120:7168], wt[7168:7184], wt[7184:])
    return jnp.concatenate([za_zb, xbc, gates, zc, dt, jnp.zeros((DT_PAD - 16, wt.shape[1]), wt.dtype)], axis=0)


def _restore_in_proj(wt):
    return jnp.concatenate([wt[:4096], wt[OFF_ZC:OFF_ZC + W_ZC], wt[OFF_XBC:OFF_XBC + W_XBC],
                            wt[OFF_DT:OFF_DT + 16], wt[OFF_GATE:OFF_GATE + W_GATE]], axis=0)


def _lanes_from_devices(g):
    return jnp.moveaxis(g, 0, 1).reshape(g.shape[1], N_DEV * g.shape[2])


def _lanes_to_devices(a):
    return jnp.moveaxis(a.reshape(a.shape[0], N_DEV, a.shape[1] // N_DEV), 1, 0)


def _pad_lanes(a, width):
    return jnp.pad(a, ((0, 0), (0, width - a.shape[1])))


BIG = ("w_in", "w_branch_a", "w_branch_b", "w_branch_c", "w_out", "w_mlp_up", "w_mlp_down")
SMALL_SHARDED = ("b_gate", "lru_conv_w", "ssd_conv_w")
REPLICATED = ("norm_mix_g", "gmlp_ln_g", "gmlp_ln_b", "gmlp_w_s", "gmlp_b_s", "lru_conv_b", "lru_w_r", "lru_b_r",
              "lru_w_i", "lru_b_i", "lru_lambda", "ssd_conv_b", "ssd_dt_bias", "ssd_a_log", "ssd_d", "ssd_norm_g",
              "norm_mlp_g", "final_norm_g")
WEIGHTS = ("norm_mix_g", "w_in", "b_gate", "gmlp_ln_g", "gmlp_ln_b", "gmlp_w_s", "gmlp_b_s", "lru_conv_w", "lru_conv_b",
           "lru_w_r", "lru_b_r", "lru_w_i", "lru_b_i", "lru_lambda", "ssd_conv_w", "ssd_conv_b", "ssd_dt_bias",
           "ssd_a_log", "ssd_d", "ssd_norm_g", "w_branch_a", "w_branch_b", "w_branch_c", "w_out", "norm_mlp_g",
           "w_mlp_up", "w_mlp_down", "final_norm_g")
TRANSPOSED = ("w_in", "w_mlp_up")
SMALL_MATRICES = ("gmlp_w_s", "lru_w_r", "lru_w_i")
SMALL_VECTORS = tuple(n for n in REPLICATED if n not in SMALL_MATRICES)
GRADIENT_GROUPS = {"mlp": ("w_mlp_up", "w_mlp_down"), "mix": ("w_branch_a", "w_branch_b", "w_branch_c", "w_out"),
                   "in": ("w_in",)}


def _layer_params(full, l):
    row = lambda a: a.reshape(1, -1)
    return dict(
        norm_mix_g=row(full["norm_mix_g"][l]), norm_mlp_g=row(full["norm_mlp_g"][l]),
        gmlp=(row(full["gmlp_ln_g"][l]), row(full["gmlp_ln_b"][l]), full["gmlp_w_s"][l].reshape(GROUPS_A * CHUNK, CHUNK),
              full["gmlp_b_s"][l].T),
        lru=(full["lru_conv_w"][l], row(full["lru_conv_b"][l]), full["lru_w_r"][l].reshape(D, D // HEADS_B),
             row(full["lru_b_r"][l]), full["lru_w_i"][l].reshape(D, D // HEADS_B), row(full["lru_b_i"][l]),
             row(full["lru_lambda"][l])),
        ssd=(full["ssd_conv_w"][l], row(full["ssd_conv_b"][l]), _pad_lanes(row(full["ssd_dt_bias"][l]), DT_PAD),
             _pad_lanes(row(full["ssd_a_log"][l]), DT_PAD), _pad_lanes(row(full["ssd_d"][l]), DT_PAD),
             row(full["ssd_norm_g"][l])),
        b_gate=full["b_gate"][l],
    )


def _forward_layer(h, hn, p, wb, l, after_mixers=None, next_norm_g=None):
    tag = f"l{l}"
    t_row = 512
    if hn is None:
        (hn,), _, _ = _chunk_fwd(_f_rmsnorm, f"norm_mix_{tag}", t_row, [(h, 0, D)], [p["norm_mix_g"]], [(D, BF16)])
    proj = _matmul(hn, wb["w_in"], f"in_proj_{tag}", tb=True)
    (ya,), _, _ = _chunk_fwd(_f_gmlp, f"gmlp_{tag}", CHUNK, [(proj, OFF_ZA, W_ZA)], p["gmlp"], [(D, BF16)])
    lru_xs = [(proj, OFF_ZB, D), (proj, OFF_ZB + D, D)]
    (yb,), lru_saved, _ = _chunk_fwd(_f_lru, f"lru_{tag}", CHUNK, lru_xs, p["lru"], [(D, BF16)], halo_idx=(0,),
                                     carry_shapes=[(1, D)], save_carries=True)
    ssd_xs = [(proj, OFF_ZC, W_ZC), (proj, OFF_XBC, W_XBC), (proj, OFF_DT, W_DT)]
    (yc,), ssd_saved, _ = _chunk_fwd(_f_ssd, f"ssd_{tag}", CHUNK, ssd_xs, p["ssd"], [(D, BF16)], halo_idx=(1,),
                                     carry_shapes=[(HEADS_C * HEAD_DIM_C, STATE_C)], save_carries=True)
    if after_mixers is not None:
        after_mixers(yc)
    pa = _matmul(ya, wb["w_branch_a"], f"branch_a_{tag}")
    pb = _matmul(yb, wb["w_branch_b"], f"branch_b_{tag}")
    pc = _matmul(yc, wb["w_branch_c"], f"branch_c_{tag}")
    merge_xs = [(pa, 0, D), (pb, 0, D), (pc, 0, D), (proj, OFF_GATE, W_GATE)]
    (merged,), _, _ = _chunk_fwd(_f_merge, f"merge_{tag}", t_row, merge_xs, [p["b_gate"]], [(D, BF16)])

    def add_and_norm(acc, res, g):
        s = acc + res
        return s, _rms(s, g)

    h_mid, hn2 = _matmul(merged, wb["w_out"], f"out_proj_{tag}", outs=(F32, BF16), epilogue=add_and_norm, extras=(h,),
                         row_params=(p["norm_mlp_g"],))

    def relu_sq(acc):
        r = jnp.maximum(acc, 0.0)
        return r, r * r

    relu_up, act = _matmul(hn2, wb["w_mlp_up"], f"mlp_up_{tag}", tb=True, outs=(F32, BF16), epilogue=relu_sq)
    if next_norm_g is None:
        hn_out = None
        h_out = _matmul(act, wb["w_mlp_down"], f"mlp_down_{tag}", epilogue=lambda acc, res: (acc + res,), extras=(h_mid,))
    else:
        h_out, hn_out = _matmul(act, wb["w_mlp_down"], f"mlp_down_{tag}", outs=(F32, BF16), epilogue=add_and_norm,
                                extras=(h_mid,), row_params=(next_norm_g,))
    saved = dict(h=h, hn=hn, proj=proj, ya=ya, yb=yb, yc=yc, lru_saved=lru_saved, ssd_saved=ssd_saved, pa=pa, pb=pb,
                 pc=pc, merged=merged, h_mid=h_mid, hn2=hn2, relu_up=relu_up, act=act, lru_xs=lru_xs, ssd_xs=ssd_xs,
                 merge_xs=merge_xs)
    return h_out, hn_out, saved


def _backward_layer(dh, sv, p, wb, l):
    tag = f"l{l}"
    t_row = 512
    g = {}
    d_up = _matmul(dh, wb["w_mlp_down"], f"d_act_{tag}", tb=True, outs=(BF16,),
                   epilogue=lambda acc, r: (acc * (2.0 * r),), extras=(sv["relu_up"],))
    g["w_mlp_down"] = _matmul(sv["act"], dh, f"dw_mlp_down_{tag}", ta=True)
    g["w_mlp_up"] = _matmul(d_up, sv["hn2"], f"dw_mlp_up_{tag}", ta=True)
    d_hn2 = _matmul(d_up, wb["w_mlp_up"], f"d_hn2_{tag}")
    (d_mid,), (g["norm_mlp_g"],) = _chunk_bwd(_f_rmsnorm_res, f"norm_mlp_bwd_{tag}", t_row, [(sv["h_mid"], 0, D)],
                                              [p["norm_mlp_g"]], [d_hn2, dh], [F32])
    d_merged = _matmul(d_mid, wb["w_out"], f"d_merged_{tag}", tb=True)
    g["w_out"] = _matmul(sv["merged"], d_mid, f"dw_out_{tag}", ta=True)
    (d_pa, d_pb, d_pc, d_gate), (g["b_gate"],) = _chunk_bwd(
        _f_merge, f"merge_bwd_{tag}", t_row, sv["merge_xs"], [p["b_gate"]], [d_merged], [BF16] * 4)
    d_y = {}
    for br, d_p, y in (("a", d_pa, sv["ya"]), ("b", d_pb, sv["yb"]), ("c", d_pc, sv["yc"])):
        g[f"w_branch_{br}"] = _matmul(y, d_p, f"dw_branch_{br}_{tag}", ta=True)
        d_y[br] = _matmul(d_p, wb[f"w_branch_{br}"], f"d_y{br}_{tag}", tb=True)
    (d_za,), g_gmlp = _chunk_bwd(_f_gmlp, f"gmlp_bwd_{tag}", CHUNK, [(sv["proj"], OFF_ZA, W_ZA)], p["gmlp"],
                                 [d_y["a"]], [BF16])
    (d_xb, d_gt), g_lru = _chunk_bwd(_f_lru, f"lru_bwd_{tag}", CHUNK, sv["lru_xs"], p["lru"], [d_y["b"]], [BF16] * 2,
                                     halo_idx=(0,), saved=sv["lru_saved"])
    (d_zc, d_xbc, d_dt), g_ssd = _chunk_bwd(_f_ssd, f"ssd_bwd_{tag}", CHUNK, sv["ssd_xs"], p["ssd"], [d_y["c"]],
                                            [BF16] * 3, halo_idx=(1,), saved=sv["ssd_saved"])
    d_proj = [d_za, d_xb, d_gt, d_xbc, d_gate, d_zc, d_dt]
    g["w_in"] = _matmul_pieces_t(d_proj, sv["hn"], f"dw_in_{tag}")
    d_hn = _matmul_pieces(d_proj, wb["w_in"], f"d_hn_{tag}")
    (d_h,), (g["norm_mix_g"],) = _chunk_bwd(_f_rmsnorm_res, f"norm_mix_bwd_{tag}", t_row, [(sv["h"], 0, D)],
                                            [p["norm_mix_g"]], [d_hn, d_mid], [F32])
    g["w_in"] = _restore_in_proj(g["w_in"])
    for n in BIG:
        g[n] = g[n].reshape(N_DEV, g[n].shape[0] // N_DEV, g[n].shape[1])
    g["gmlp_ln_g"], g["gmlp_ln_b"], g["gmlp_w_s"] = g_gmlp[:3]
    g["gmlp_b_s"] = g_gmlp[3].T
    (g["lru_conv_w"], g["lru_conv_b"], g["lru_w_r"], g["lru_b_r"], g["lru_w_i"], g["lru_b_i"], g["lru_lambda"]) = g_lru
    g["ssd_conv_w"], g["ssd_conv_b"] = g_ssd[:2]
    g["ssd_dt_bias"], g["ssd_a_log"], g["ssd_d"] = (a[:, :HEADS_C] for a in g_ssd[2:5])
    g["ssd_norm_g"] = g_ssd[5]
    g["later"] = {"mlp": d_merged, "mix": d_za, "in": d_h}
    return d_h, g


LOSS_ROWS = 512


def _loss_and_grads(h, target, full, layer_weights, first_gathered):
    seq = h.shape[0]
    layer_p = [_layer_params(full, l) for l in range(DEPTH)]
    layer_w = [layer_weights(0, (first_gathered,))]
    saved = []
    hn = None
    for l in range(DEPTH):
        fetch_next = next_norm_g = None
        if l + 1 < DEPTH:
            fetch_next = lambda y, l=l: layer_w.append(layer_weights(l + 1, (layer_w[l]["w_mlp_down"], y)))
            next_norm_g = layer_p[l + 1]["norm_mix_g"]
        h, hn, sv = _forward_layer(h, hn, layer_p[l], layer_w[l], l, fetch_next, next_norm_g)
        saved.append(sv)
    final_g = full["final_norm_g"].reshape(1, D)
    loss_xs = [(h, 0, D), (target, 0, D)]
    t_loss = min(LOSS_ROWS, seq)
    _, _, (loss_acc,) = _chunk_fwd(_f_loss, "loss", t_loss, loss_xs, [final_g], [], carry_shapes=[(1, 128)],
                                   final_carries=True)
    zero_acc = jnp.zeros((seq // t_loss, 1, 128), F32)
    seed = lambda shape: (lax.broadcasted_iota(jnp.int32, shape, 1) == 0).astype(F32)
    (dh, _), (g_final,) = _chunk_bwd(_f_loss, "loss_bwd", t_loss, loss_xs, [final_g], [], [F32, F32], saved=[zero_acc],
                                     carry_seed=seed)
    layer_g = [None] * DEPTH
    for l in reversed(range(DEPTH)):
        dh, layer_g[l] = _backward_layer(dh, saved[l], layer_p[l], layer_w[l], l)
    return loss_acc, dh, layer_g, g_final


def _small_views(d):
    views = {n: d[n] for n in REPLICATED}
    views["gmlp_b_s"] = d["gmlp_b_s"].reshape(DEPTH * GROUPS_A, CHUNK)
    views["final_norm_g"] = d["final_norm_g"].reshape(1, D)
    for n in SMALL_MATRICES:
        views[n] = d[n].reshape(DEPTH * D, D // HEADS_B)
    return views


def kernel(x, norm_mix_g, w_in, b_gate, gmlp_ln_g, gmlp_ln_b, gmlp_w_s, gmlp_b_s, lru_conv_w, lru_conv_b, lru_w_r, lru_b_r, lru_w_i, lru_b_i, lru_lambda, ssd_conv_w, ssd_conv_b, ssd_dt_bias, ssd_a_log, ssd_d, ssd_norm_g, w_branch_a, w_branch_b, w_branch_c, w_out, norm_mlp_g, w_mlp_up, w_mlp_down, final_norm_g, loss_target, m_norm_mix_g, m_w_in, m_b_gate, m_gmlp_ln_g, m_gmlp_ln_b, m_gmlp_w_s, m_gmlp_b_s, m_lru_conv_w, m_lru_conv_b, m_lru_w_r, m_lru_b_r, m_lru_w_i, m_lru_b_i, m_lru_lambda, m_ssd_conv_w, m_ssd_conv_b, m_ssd_dt_bias, m_ssd_a_log, m_ssd_d, m_ssd_norm_g, m_w_branch_a, m_w_branch_b, m_w_branch_c, m_w_out, m_norm_mlp_g, m_w_mlp_up, m_w_mlp_down, m_final_norm_g, v_norm_mix_g, v_w_in, v_b_gate, v_gmlp_ln_g, v_gmlp_ln_b, v_gmlp_w_s, v_gmlp_b_s, v_lru_conv_w, v_lru_conv_b, v_lru_w_r, v_lru_b_r, v_lru_w_i, v_lru_b_i, v_lru_lambda, v_ssd_conv_w, v_ssd_conv_b, v_ssd_dt_bias, v_ssd_a_log, v_ssd_d, v_ssd_norm_g, v_w_branch_a, v_w_branch_b, v_w_branch_c, v_w_out, v_norm_mlp_g, v_w_mlp_up, v_w_mlp_down, v_final_norm_g):
    args = locals()
    w = {n: args[n] for n in WEIGHTS}
    m = {n: args["m_" + n] for n in WEIGHTS}
    v = {n: args["v_" + n] for n in WEIGHTS}
    seq = x.shape[1]
    h = x.reshape(seq, D)
    target = loss_target.reshape(seq, D)

    def shard_on_wire(n, l):
        return (w[n][l].T if n in TRANSPOSED else w[n][l]).astype(BF16)

    first = _all_gather([shard_on_wire("w_in", 0)] + [w[n] for n in SMALL_SHARDED], "gather_weights_first")
    full = {n: w[n] for n in REPLICATED}
    for n, g in zip(SMALL_SHARDED, first[1:]):
        full[n] = jnp.stack([_lanes_from_devices(g[:, l]) for l in range(DEPTH)])

    def layer_weights(l, after):
        have = {"w_in": first[0]} if l == 0 else {}
        names = [n for n in BIG if n not in have]
        later = _all_gather([shard_on_wire(n, l) for n in names], f"gather_weights_l{l}", SEQ_GATHER + l, after=after)
        have.update(zip(names, later))
        wl = {n: have[n].reshape(-1, D) for n in BIG}
        wl["w_in"] = _reorder_in_proj(wl["w_in"])
        return wl

    loss_local, dh, layer_g, g_final = _loss_and_grads(h, target, full, layer_weights, first[0])
    grad_x = dh.reshape(x.shape)
    out = {}
    kinds = ("grad", "delta", "new_m", "new_v")

    sequencer_before = {}

    def reduce_scatter(slabs, tag, on_sequencer, later=()):
        keys = list(slabs)
        ids = (SEQ_TO_SIBLING, SEQ_TO_CHIPS) if on_sequencer else (None, None)
        from_sibling = _exchange_sibling([slabs[k] for k in keys], f"grads_to_sibling_{tag}", ids[0],
                                         sequencer_before.get("sibling", ()))
        from_sibling = dict(zip(keys, from_sibling))
        same_shape = {}
        for k in keys:
            same_shape.setdefault(slabs[k].shape[1:], []).append(k)
        chip_sums = {}
        for (r, c), ks in same_shape.items():
            sums = _add_sibling([slabs[k] for k in ks], [from_sibling[k] for k in ks], f"add_sibling_{tag}_{r}x{c}", later)
            chip_sums.update(zip(ks, sums))
        from_chips = _exchange_chips([chip_sums[k] for k in keys], f"grads_to_chips_{tag}", ids[1],
                                     sequencer_before.get("chips", ()))
        if on_sequencer:
            sequencer_before["sibling"], sequencer_before["chips"] = (from_sibling[keys[0]],), (from_chips[0],)
        from_chips = dict(zip(keys, from_chips))
        summed = {}
        for (r, c), ks in same_shape.items():
            sums = _sum_chips([slabs[k] for k in ks], [from_sibling[k] for k in ks], [from_chips[k] for k in ks],
                              f"sum_chips_{tag}_{r}x{c}")
            summed.update(zip(ks, sums))
        return summed

    small = {}
    for n in SMALL_SHARDED:
        small[n, None] = jnp.concatenate([_lanes_to_devices(layer_g[l][n]) for l in range(DEPTH)], axis=1)
    g_small = {n: jnp.concatenate([layer_g[l][n] for l in range(DEPTH)], axis=0) for n in REPLICATED[:-1]}
    g_small["final_norm_g"] = g_final
    for n in SMALL_MATRICES:
        small[n, None] = g_small[n].reshape(N_DEV, -1, g_small[n].shape[-1])
    reduced = {}
    groups = [(l, grp) for l in range(DEPTH - 1, -1, -1) for grp in GRADIENT_GROUPS]
    for l, grp in groups:
        if (l, grp) == groups[-1]:
            reduced.update(reduce_scatter(small, "small", True, (layer_g[l][GRADIENT_GROUPS[grp][0]],)))
        slabs = {(n, l): layer_g[l][n] for n in GRADIENT_GROUPS[grp]}
        reduced.update(reduce_scatter(slabs, f"{grp}_l{l}", True, (layer_g[l]["later"][grp],)))
    for n in BIG:
        grads = [reduced[n, l] for l in range(DEPTH)]
        if n in TRANSPOSED and w[n].shape[-1] % LANES:
            res = _adamw_transposed(grads, w[n], m[n], v[n], f"adamw_{n}")
        else:
            res = _adamw([(g.T if n in TRANSPOSED else g)[None] for g in grads], w[n], m[n], v[n], f"adamw_{n}")
        for kind, a in zip(kinds, res):
            out[kind, n] = a
    for n in SMALL_SHARDED:
        one = lambda a: a.reshape((1, -1, a.shape[-1]))
        for kind, a in zip(kinds, _adamw([reduced[n, None][None]], one(w[n]), one(m[n]), one(v[n]), f"adamw_{n}")):
            out[kind, n] = a.reshape(w[n].shape)

    to_gather = [reduced[n, None] if n in SMALL_MATRICES else g_small[n] for n in REPLICATED]
    *g_gathered, loss_terms = _all_gather(to_gather + [loss_local], "gather_small_grads", SEQ_GATHER_SMALL)
    g_all = dict(zip(REPLICATED, g_gathered))
    wv, mv, vv = _small_views(w), _small_views(m), _small_views(v)
    res, loss_sum = _adamw_small([g_all[n] for n in SMALL_VECTORS],
                                 *[[d[n] for n in SMALL_VECTORS] for d in (wv, mv, vv)], loss_terms, "adamw_vectors")
    loss = loss_sum[0, 0]
    for kind, arrays in zip(kinds, res):
        for n, a in zip(SMALL_VECTORS, arrays):
            out[kind, n] = a.reshape(w[n].shape)
    for n in SMALL_MATRICES:
        g_full = g_all[n].reshape((1, 1) + wv[n].shape)
        for kind, a in zip(kinds, _adamw([g_full[0]], wv[n][None], mv[n][None], vv[n][None], f"adamw_{n}")):
            out[kind, n] = a.reshape(w[n].shape)

    return (loss, grad_x, *[out[kind, n] for kind in kinds for n in WEIGHTS])
```

```python
import functools

import jax
import jax.numpy as jnp
from jax import lax
from jax.experimental import pallas as pl
from jax.experimental.pallas import tpu as pltpu
from jax.experimental.pallas import tpu_sc as plsc

F32 = jnp.float32
BF16 = jnp.bfloat16
MESH = pl.DeviceIdType.MESH

D = 1024
DEPTH = 2
EPS = 1e-6
CHUNK = 128
GROUPS_A = 8
HEADS_B = 8
LRU_C = 8.0
HEADS_C = 16
HEAD_DIM_C = 64
GROUPS_C = 4
STATE_C = 128
DT_PAD = 128
OFF_ZA, W_ZA = 0, 2048
OFF_ZB, W_ZB = 2048, 2048
OFF_XBC, W_XBC = 4096, 2048
OFF_GATE, W_GATE = 6144, 3072
OFF_ZC, W_ZC = 9216, 1024
OFF_DT, W_DT = 10240, DT_PAD
D_IN_PAD = 10368
N_DEV = 8
SEQ_GATHER = 1
SEQ_TO_SIBLING = SEQ_GATHER + DEPTH
SEQ_TO_CHIPS = SEQ_TO_SIBLING + 1
SEQ_GATHER_SMALL = SEQ_TO_CHIPS + 1

ADAM_LR = 0.001
ADAM_B1 = 0.9
ADAM_B2 = 0.999
ADAM_EPS = 1e-08
ADAM_WD = 0.01
ADAM_STEP = 10

VMEM_LIMIT = 56 * 1024 * 1024
HALO = 8


def _cparams(*sem):
    return pltpu.CompilerParams(dimension_semantics=sem, vmem_limit_bytes=VMEM_LIMIT)


def _bf(x):
    return x.astype(BF16)


def _dg(a, b, ca, cb):
    return lax.dot_general(a, b, (((ca,), (cb,)), ((), ())), preferred_element_type=F32)


@functools.partial(jax.custom_vjp, nondiff_argnums=(2, 3))
def _mm(a, b, ta, tb):
    return _dg(_bf(a), _bf(b), 0 if ta else 1, 1 if tb else 0)


def _mm_fwd(a, b, ta, tb):
    return _mm(a, b, ta, tb), (a, b)


def _mm_bwd(ta, tb, res, g):
    a, b = res
    ma = 1 if ta else 0
    nb = 0 if tb else 1
    gb, ab, bb = _bf(g), _bf(a), _bf(b)
    da = _dg(bb, gb, nb, 1) if ta else _dg(gb, bb, 1, nb)
    db = _dg(gb, ab, 0, ma) if tb else _dg(ab, gb, ma, 0)
    return da.astype(a.dtype), db.astype(b.dtype)


_mm.defvjp(_mm_fwd, _mm_bwd)


def _slices(x, sizes, axis):
    out, lo = [], 0
    for size in sizes:
        out.append(lax.slice_in_dim(x, lo, lo + size, axis=axis))
        lo += size
    return tuple(out)


@functools.partial(jax.custom_vjp, nondiff_argnums=(1,))
def _split_cols(x, widths):
    return _slices(x, widths, 1)


_split_cols.defvjp(lambda x, widths: (_slices(x, widths, 1), None),
                   lambda widths, _, gs: (jnp.concatenate(gs, axis=1),))


@functools.partial(jax.custom_vjp, nondiff_argnums=(1,))
def _split_rows(x, heights):
    return _slices(x, heights, 0)


_split_rows.defvjp(lambda x, heights: (_slices(x, heights, 0), None),
                   lambda heights, _, gs: (jnp.concatenate(gs, axis=0),))


def _col(x, j):
    lane = lax.broadcasted_iota(jnp.int32, x.shape, 1)
    return jnp.sum(jnp.where(lane == j, x, 0.0), axis=1, keepdims=True)


def _row(x, i):
    r = lax.broadcasted_iota(jnp.int32, x.shape, 0)
    return jnp.sum(jnp.where(r == i, x, 0.0), axis=0, keepdims=True)


def _roll_down(x, s):
    return pltpu.roll(x, s, 0)


def _roll_up(x, s):
    return pltpu.roll(x, x.shape[0] - s, 0)


def _row_iota(x):
    return lax.broadcasted_iota(jnp.int32, x.shape, 0)


@functools.partial(jax.custom_vjp, nondiff_argnums=(2,))
def _shift_rows(halo, x, s):
    if s == 0:
        return x
    return _roll_down(jnp.concatenate([halo, x], axis=0), s)[HALO:]


def _shift_rows_fwd(halo, x, s):
    return _shift_rows(halo, x, s), None


def _shift_rows_bwd(s, _, g):
    if s == 0:
        return jnp.zeros((HALO, g.shape[1]), g.dtype), g
    ge = jnp.concatenate([jnp.zeros((HALO, g.shape[1]), g.dtype), g], axis=0)
    de = _roll_up(ge, s)
    return de[:HALO], de[HALO:]


_shift_rows.defvjp(_shift_rows_fwd, _shift_rows_bwd)


SUBLANES = 8
LANES = 128


def _scan_tiles(a, b, carry, up):
    n, c = a.shape
    nt = n // SUBLANES
    a = a.reshape(nt, SUBLANES, c)
    b = b.reshape(nt, SUBLANES, c)
    sub = lax.broadcasted_iota(jnp.int32, a.shape, 1)
    s = 1
    while s < SUBLANES:
        keep = (sub < SUBLANES - s) if up else (sub >= s)
        shift = SUBLANES - s if up else s
        a_sh = jnp.where(keep, pltpu.roll(a, shift, 1), 1.0)
        b_sh = jnp.where(keep, pltpu.roll(b, shift, 1), 0.0)
        b = a * b_sh + b
        a = a * a_sh
        s *= 2
    tiles = [None] * nt
    edge = 0 if up else SUBLANES - 1
    for j in (range(nt - 1, -1, -1) if up else range(nt)):
        tiles[j] = b[j] if carry is None else b[j] + a[j] * carry
        carry = tiles[j][edge:edge + 1, :]
    return jnp.concatenate(tiles, axis=0)


@jax.custom_vjp
def _lin_scan(a, b, h0):
    return _scan_tiles(a, b, h0, up=False)


def _lin_scan_fwd(a, b, h0):
    h = _lin_scan(a, b, h0)
    return h, (a, h0, h)


def _lin_scan_bwd(res, g):
    a, h0, h = res
    n = a.shape[0]
    row = _row_iota(a)
    a_next = jnp.where(row < n - 1, _roll_up(a, 1), 0.0)
    gg = _scan_tiles(a_next, g, None, up=True)
    h_prev = jnp.where(row >= 1, _roll_down(h, 1), h0)
    return gg * h_prev, gg, _row(a * gg, 0)


_lin_scan.defvjp(_lin_scan_fwd, _lin_scan_bwd)


@jax.custom_vjp
def _cumsum_rows(x):
    n = x.shape[0]
    row = _row_iota(x)
    s = 1
    while s < n:
        x = x + jnp.where(row >= s, _roll_down(x, s), 0.0)
        s *= 2
    return x


def _cumsum_rows_fwd(x):
    return _cumsum_rows(x), None


def _cumsum_rows_bwd(_, g):
    n = g.shape[0]
    row = _row_iota(g)
    s = 1
    while s < n:
        g = g + jnp.where(row < n - s, _roll_up(g, s), 0.0)
        s *= 2
    return (g,)


_cumsum_rows.defvjp(_cumsum_rows_fwd, _cumsum_rows_bwd)


def _sigmoid(x):
    return jax.nn.sigmoid(x)


def _softplus(x):
    return jnp.maximum(x, 0.0) + jnp.log1p(jnp.exp(-jnp.abs(x)))


def _gelu(x):
    return jax.nn.gelu(x, approximate=True)


def _neg_expm1(x):
    series = -x * (1.0 + x * (0.5 + x * (1.0 / 6.0 + x * (1.0 / 24.0))))
    return jnp.where(x > -0.01, series, 1.0 - jnp.exp(x))


def _rms(x, g):
    return x * lax.rsqrt(jnp.mean(x * x, axis=-1, keepdims=True) + EPS) * g


def _f_rmsnorm(carries, halos, xs, params):
    (h,) = xs
    (g,) = params
    return (), (_rms(h, g),)


def _f_rmsnorm_res(carries, halos, xs, params):
    (h,) = xs
    (g,) = params
    return (), (_rms(h, g), h)


def _f_gmlp(carries, halos, xs, params):
    (za,) = xs
    ln_g, ln_b, w_s, b_st = params
    u, v = _split_cols(_gelu(za), (D, D))
    vc = v - jnp.mean(v, axis=-1, keepdims=True)
    vn = vc * lax.rsqrt(jnp.mean(vc * vc, axis=-1, keepdims=True) + EPS) * ln_g + ln_b
    q = CHUNK
    causal = lax.broadcasted_iota(jnp.int32, (q, q), 0) >= lax.broadcasted_iota(jnp.int32, (q, q), 1)
    mixed = []
    for g, (w, vg) in enumerate(zip(_split_rows(w_s, (q,) * GROUPS_A), _split_cols(vn, (q,) * GROUPS_A))):
        mixed.append(_mm(jnp.where(causal, w, 0.0), vg, False, False) + _col(b_st, g))
    return (), (u * jnp.concatenate(mixed, axis=1),)


def _conv4(halo, x, w, b):
    y = b + _row(w, 3) * x
    for k in range(3):
        y = y + _row(w, k) * _shift_rows(halo, x, 3 - k)
    return y


def _f_lru(carries, halos, xs, params):
    (h0,) = carries
    (halo,) = halos
    xb_pre, gate = xs
    conv_w, conv_b, w_r, b_r, w_i, b_i, lam = params
    xb = _conv4(halo, xb_pre, conv_w, conv_b)
    hd = D // HEADS_B
    r_parts, i_parts = [], []
    heads = (hd,) * HEADS_B
    for xh, wr, wi in zip(_split_cols(xb, heads), _split_rows(w_r, heads), _split_rows(w_i, heads)):
        r_parts.append(_mm(xh, wr, False, False))
        i_parts.append(_mm(xh, wi, False, False))
    r = _sigmoid(jnp.concatenate(r_parts, axis=1) + b_r)
    i = _sigmoid(jnp.concatenate(i_parts, axis=1) + b_i)
    log_a = -LRU_C * r * _softplus(-lam)
    a = jnp.exp(log_a)
    inp = jnp.sqrt(_neg_expm1(2.0 * log_a)) * (i * xb)
    h = _lin_scan(a, inp, h0)
    return (_row(h, h.shape[0] - 1),), (_gelu(gate) * h,)


def _f_ssd(carries, halos, xs, params):
    (st,) = carries
    (halo,) = halos
    z, xbc_pre, dt_raw = xs
    conv_w, conv_b, dt_bias, a_log, d_skip, norm_g = params
    t = z.shape[0]
    xc = _conv4(halo, xbc_pre, conv_w, conv_b)
    xbc = xc * _sigmoid(xc)
    x_all, b_all, c_all = _split_cols(xbc, (D, GROUPS_C * STATE_C, GROUPS_C * STATE_C))
    x_pairs = _split_cols(x_all, (128,) * (HEADS_C // 2))
    b_groups = _split_cols(b_all, (STATE_C,) * GROUPS_C)
    c_groups = _split_cols(c_all, (STATE_C,) * GROUPS_C)
    st_pairs = _split_rows(st, (128,) * (HEADS_C // 2))
    dt = _softplus(dt_raw + dt_bias)
    adt = dt * (-jnp.exp(a_log))
    acs = _cumsum_rows(adt)
    acs_t = acs.T
    a_last = _row(acs, t - 1)
    lo = lax.broadcasted_iota(jnp.int32, (t, 128), 1) < HEAD_DIM_C
    lo_rows = lax.broadcasted_iota(jnp.int32, (128, STATE_C), 0) < HEAD_DIM_C
    causal = lax.broadcasted_iota(jnp.int32, (t, t), 0) >= lax.broadcasted_iota(jnp.int32, (t, t), 1)
    y_parts, st_parts = [], []
    for g in range(GROUPS_C):
        bg, cg = b_groups[g], c_groups[g]
        cb = _mm(cg, bg, False, True)
        for pr in range(2):
            pair = 2 * g + pr
            h0, h1 = 2 * pair, 2 * pair + 1
            x2 = x_pairs[pair]
            ac0, ac1 = _col(acs, h0), _col(acs, h1)
            l0 = jnp.exp(jnp.where(causal, ac0 - _row(acs_t, h0), -1e30))
            l1 = jnp.exp(jnp.where(causal, ac1 - _row(acs_t, h1), -1e30))
            xdt = x2 * jnp.where(lo, _col(dt, h0), _col(dt, h1))
            y_diag = (_mm(cb * l0, jnp.where(lo, xdt, 0.0), False, False)
                      + _mm(cb * l1, jnp.where(lo, 0.0, xdt), False, False))
            al0, al1 = _col(a_last, h0), _col(a_last, h1)
            decay_s = jnp.where(lo, jnp.exp(al0 - ac0), jnp.exp(al1 - ac1))
            s_new = _mm(xdt * decay_s, bg, True, False)
            prev = st_pairs[pair]
            y_off = _mm(cg, prev, False, True) * jnp.where(lo, jnp.exp(ac0), jnp.exp(ac1))
            skip = jnp.where(lo, _col(d_skip, h0), _col(d_skip, h1))
            y_parts.append(y_diag + y_off + x2 * skip)
            st_parts.append(prev * jnp.where(lo_rows, jnp.exp(al0), jnp.exp(al1)) + s_new)
    y = jnp.concatenate(y_parts, axis=1) * (z * _sigmoid(z))
    gw = D // GROUPS_C
    yn = []
    for yg in _split_cols(y, (gw,) * GROUPS_C):
        yn.append(yg * lax.rsqrt(jnp.mean(yg * yg, axis=-1, keepdims=True) + EPS))
    return (jnp.concatenate(st_parts, axis=0),), (jnp.concatenate(yn, axis=1) * norm_g,)


def _f_merge(carries, halos, xs, params):
    pa, pb, pc, g_raw = xs
    (b_gate,) = params
    ga, gb, gc = _split_cols(g_raw, (D, D, D))
    m = (_sigmoid(ga + _row(b_gate, 0)) * pa + _sigmoid(gb + _row(b_gate, 1)) * pb
         + _sigmoid(gc + _row(b_gate, 2)) * pc)
    return (), (m,)


def _f_loss(carries, halos, xs, params):
    (acc,) = carries
    h, target = xs
    (g,) = params
    err = jnp.square(_rms(h, g) - target)
    part = 0.5 * jnp.sum(jnp.mean(err, axis=-1, keepdims=True), axis=0, keepdims=True)
    return (acc + part,), ()


def _x_specs(xs, t, index_of):
    specs = []
    for arr, off, width in xs:
        assert off % width == 0 and off + width <= arr.shape[1]
        specs.append(pl.BlockSpec((t, width), functools.partial(lambda j, cb: (index_of(j), cb), cb=off // width)))
    return specs


def _halo_specs(xs, halo_idx, t, index_of):
    specs = []
    for xi in halo_idx:
        _, off, width = xs[xi]
        specs.append(pl.BlockSpec(
            (HALO, width),
            functools.partial(lambda j, cb: (jnp.maximum(index_of(j) * (t // HALO) - 1, 0), cb), cb=off // width)))
    return specs


def _full_spec(a):
    return pl.BlockSpec(a.shape, functools.partial(lambda j, nd: (0,) * nd, nd=a.ndim))


def _chunk_fwd(f, name, t, xs, params, outs, halo_idx=(), carry_shapes=(), save_carries=False, final_carries=False):
    s = xs[0][0].shape[0]
    n = s // t
    nx, nh, npar, no, nc = len(xs), len(halo_idx), len(params), len(outs), len(carry_shapes)
    ns = nc if save_carries else 0
    nf = nc if final_carries else 0

    def body(*refs):
        x_refs, refs = refs[:nx], refs[nx:]
        h_refs, refs = refs[:nh], refs[nh:]
        p_refs, refs = refs[:npar], refs[npar:]
        y_refs, refs = refs[:no], refs[no:]
        s_refs, refs = refs[:ns], refs[ns:]
        f_refs, c_refs = refs[:nf], refs[nf:]
        i = pl.program_id(0)

        @pl.when(i == 0)
        def _():
            for c in c_refs:
                c[...] = jnp.zeros_like(c)

        carries = tuple(c[...] for c in c_refs)
        for s_ref, c in zip(s_refs, carries):
            s_ref[0] = c
        halos = tuple(jnp.where(i > 0, h[...].astype(F32), 0.0) for h in h_refs)
        new_c, ys = f(carries, halos, tuple(x[...].astype(F32) for x in x_refs), tuple(p[...] for p in p_refs))
        for y_ref, y in zip(y_refs, ys):
            y_ref[...] = y.astype(y_ref.dtype)
        for c, v in zip(c_refs, new_c):
            c[...] = v
        for f_ref, v in zip(f_refs, new_c):
            f_ref[...] = v

    ident = lambda j: j
    out_shape = [jax.ShapeDtypeStruct((s, w), dt) for w, dt in outs]
    out_specs = [pl.BlockSpec((t, w), lambda j: (j, 0)) for w, _ in outs]
    if save_carries:
        out_shape += [jax.ShapeDtypeStruct((n,) + tuple(cs), F32) for cs in carry_shapes]
        out_specs += [pl.BlockSpec((1,) + tuple(cs), lambda j: (j, 0, 0)) for cs in carry_shapes]
    if final_carries:
        out_shape += [jax.ShapeDtypeStruct(tuple(cs), F32) for cs in carry_shapes]
        out_specs += [pl.BlockSpec(tuple(cs), lambda j: (0, 0)) for cs in carry_shapes]
    res = pl.pallas_call(
        body, name=name, grid=(n,),
        in_specs=_x_specs(xs, t, ident) + _halo_specs(xs, halo_idx, t, ident) + [_full_spec(p) for p in params],
        out_specs=out_specs, out_shape=out_shape,
        scratch_shapes=[pltpu.VMEM(tuple(cs), F32) for cs in carry_shapes],
        compiler_params=_cparams("arbitrary"),
    )(*[x[0] for x in xs], *[xs[xi][0] for xi in halo_idx], *params)
    return res[:no], res[no:no + ns], res[no + ns:]


def _chunk_bwd(f, name, t, xs, params, dys, dx_dtypes, halo_idx=(), saved=(), carry_seed=None):
    s = xs[0][0].shape[0]
    n = s // t
    nx, nh, npar, nc, ndy = len(xs), len(halo_idx), len(params), len(saved), len(dys)

    def body(*refs):
        x_refs, refs = refs[:nx], refs[nx:]
        h_refs, refs = refs[:nh], refs[nh:]
        p_refs, refs = refs[:npar], refs[npar:]
        s_refs, refs = refs[:nc], refs[nc:]
        dy_refs, refs = refs[:ndy], refs[ndy:]
        dx_refs, refs = refs[:nx], refs[nx:]
        dp_refs, refs = refs[:npar], refs[npar:]
        dc_refs, dh_refs = refs[:nc], refs[nc:]
        j = pl.program_id(0)
        i = n - 1 - j

        @pl.when(j == 0)
        def _():
            for dc in dc_refs:
                dc[...] = jnp.zeros_like(dc) if carry_seed is None else carry_seed(dc.shape)
            for r in dh_refs + dp_refs:
                r[...] = jnp.zeros_like(r)

        carries = tuple(s_ref[0] for s_ref in s_refs)
        halos = tuple(jnp.where(i > 0, h[...].astype(F32), 0.0) for h in h_refs)
        x_vals = tuple(x[...].astype(F32) for x in x_refs)
        p_vals = tuple(p[...] for p in p_refs)
        _, vjp = jax.vjp(f, carries, halos, x_vals, p_vals)
        d_car, d_hal, d_xs, d_par = vjp((tuple(dc[...] for dc in dc_refs), tuple(d[...].astype(F32) for d in dy_refs)))
        d_xs = list(d_xs)
        for k, xi in enumerate(halo_idx):
            w = xs[xi][2]
            d_xs[xi] = d_xs[xi] + jnp.concatenate([jnp.zeros((t - HALO, w), F32), dh_refs[k][...]], axis=0)
            dh_refs[k][...] = jnp.where(i > 0, d_hal[k], 0.0)
        for dx_ref, dx in zip(dx_refs, d_xs):
            dx_ref[...] = dx.astype(dx_ref.dtype)
        for dp_ref, dp in zip(dp_refs, d_par):
            dp_ref[...] += dp
        for dc, v in zip(dc_refs, d_car):
            dc[...] = v

    rev = lambda j: n - 1 - j
    in_specs = (_x_specs(xs, t, rev) + _halo_specs(xs, halo_idx, t, rev) + [_full_spec(p) for p in params]
                + [pl.BlockSpec((1,) + a.shape[1:], lambda j: (n - 1 - j, 0, 0)) for a in saved]
                + [pl.BlockSpec((t, d.shape[1]), lambda j: (n - 1 - j, 0)) for d in dys])
    out_shape = ([jax.ShapeDtypeStruct((s, w), dt) for (_, _, w), dt in zip(xs, dx_dtypes)]
                 + [jax.ShapeDtypeStruct(p.shape, F32) for p in params])
    out_specs = ([pl.BlockSpec((t, w), lambda j: (n - 1 - j, 0)) for _, _, w in xs] + [_full_spec(p) for p in params])
    res = pl.pallas_call(
        body, name=name, grid=(n,), in_specs=in_specs, out_specs=out_specs, out_shape=out_shape,
        scratch_shapes=([pltpu.VMEM(a.shape[1:], F32) for a in saved]
                        + [pltpu.VMEM((HALO, xs[xi][2]), F32) for xi in halo_idx]),
        compiler_params=_cparams("arbitrary"),
    )(*[x[0] for x in xs], *[xs[xi][0] for xi in halo_idx], *params, *saved, *dys)
    return res[:nx], res[nx:]


def _tile(dim, pref):
    for cand in pref:
        if dim % cand == 0:
            return cand
    return dim


def _matmul(a, b, name, ta=False, tb=False, outs=(F32,), epilogue=None, extras=(), row_params=()):
    m, k = (a.shape[1], a.shape[0]) if ta else a.shape
    n = b.shape[0] if tb else b.shape[1]
    tn = _tile(n, (1152, 1024, 512, 256, 128))
    tk = _tile(k, (1024, 1152, 512, 256, 128))
    nk = k // tk
    tall = nk == 1 and a.dtype == BF16 and not extras and not ta and n // tn >= 4
    tm = _tile(m, ((2048,) if tall else ()) + (1024, 1152, 512, 256, 128))
    ne, no = len(extras) + len(row_params), len(outs)
    ca, cb = (0 if ta else 1), (1 if tb else 0)

    def body(*refs):
        a_ref, b_ref = refs[:2]
        e_refs = refs[2:2 + ne]
        o_refs = refs[2 + ne:2 + ne + no]

        acc = refs[-1]
        kk = pl.program_id(2)

        @pl.when(kk == 0)
        def _():
            acc[...] = jnp.zeros_like(acc)

        acc[...] += _dg(_bf(a_ref[...]), _bf(b_ref[...]), ca, cb)

        @pl.when(kk == nk - 1)
        def _():
            res = acc[...]
            vals = (res,) if epilogue is None else epilogue(res, *[e[...] for e in e_refs])
            for o_ref, v in zip(o_refs, vals):
                o_ref[...] = v.astype(o_ref.dtype)

    a_spec = pl.BlockSpec((tk, tm), lambda i, j, kk: (kk, i)) if ta else pl.BlockSpec((tm, tk), lambda i, j, kk: (i, kk))
    b_spec = pl.BlockSpec((tn, tk), lambda i, j, kk: (j, kk)) if tb else pl.BlockSpec((tk, tn), lambda i, j, kk: (kk, j))
    mn_spec = pl.BlockSpec((tm, tn), lambda i, j, kk: (i, j))
    res = pl.pallas_call(
        body, name=name, grid=(m // tm, n // tn, nk),
        in_specs=[a_spec, b_spec] + [mn_spec] * len(extras)
        + [pl.BlockSpec((1, tn), lambda i, j, kk: (0, j))] * len(row_params),
        out_specs=[mn_spec] * no,
        out_shape=[jax.ShapeDtypeStruct((m, n), dt) for dt in outs],
        scratch_shapes=[pltpu.VMEM((tm, tn), F32)],
        compiler_params=_cparams("parallel", "parallel", "arbitrary"),
    )(a, b, *extras, *row_params)
    return res if no > 1 else res[0]


def _elementwise_block(r, c):
    if r % 8 == 0 and r >= 8:
        return _tile(r, (256, 128, 64, 32, 16, 8)), c
    return r, _tile(c, (256, 128))


PIECE_TILE = 1024


def _piece_steps(pieces):
    steps, s0 = [], 0
    for a in pieces:
        wt = min(a.shape[1], PIECE_TILE)
        assert a.shape[1] % wt == 0
        steps.append((s0, a.shape[1] // wt, wt))
        s0 += a.shape[1] // wt
    return steps, s0


def _matmul_pieces(pieces, b, name):
    steps, n_steps = _piece_steps(pieces)
    s_rows, n = pieces[0].shape[0], b.shape[1]
    tm = _tile(s_rows, (1024, 512, 256, 128))
    tail_rows = steps[-1][2]
    full_steps = n_steps - 1 if tail_rows < PIECE_TILE else n_steps
    b_tail = b[full_steps * PIECE_TILE:]
    np_ = len(pieces)

    def body(*refs):
        a_refs, b_ref, tail_ref, o_ref, acc = refs[:np_], refs[np_], refs[np_ + 1], refs[np_ + 2], refs[np_ + 3]
        s = pl.program_id(1)

        @pl.when(s == 0)
        def _():
            acc[...] = jnp.zeros_like(acc)

        for a_ref, (s0, ns, wt) in zip(a_refs, steps):
            @pl.when((s >= s0) & (s < s0 + ns))
            def _(a_ref=a_ref, wt=wt):
                rhs = b_ref[...] if wt == PIECE_TILE else tail_ref[...]
                acc[...] += _dg(_bf(a_ref[...]), _bf(rhs), 1, 0)

        @pl.when(s == n_steps - 1)
        def _():
            o_ref[...] = acc[...]

    n_rows = s_rows // tm

    def a_index(i, s, s0, ns):
        done = s >= s0 + ns
        return jnp.minimum(i + done.astype(jnp.int32), n_rows - 1), jnp.where(done, 0, jnp.clip(s - s0, 0, ns - 1))

    a_specs = [pl.BlockSpec((tm, wt), functools.partial(a_index, s0=s0, ns=ns)) for s0, ns, wt in steps]
    return pl.pallas_call(
        body, name=name, grid=(n_rows, n_steps),
        in_specs=a_specs + [pl.BlockSpec((PIECE_TILE, n), lambda i, s: (jnp.minimum(s, full_steps - 1), 0)),
                            pl.BlockSpec(b_tail.shape, lambda i, s: (0, 0))],
        out_specs=pl.BlockSpec((tm, n), lambda i, s: (i, 0)), out_shape=jax.ShapeDtypeStruct((s_rows, n), F32),
        scratch_shapes=[pltpu.VMEM((tm, n), F32)],
        compiler_params=_cparams("parallel", "arbitrary"),
    )(*pieces, b, b_tail)


def _matmul_pieces_t(pieces, b, name):
    steps, n_steps = _piece_steps(pieces)
    s_rows, n = b.shape
    tk = _tile(s_rows, (1024, 512, 256, 128))
    nk = s_rows // tk
    total = sum(a.shape[1] for a in pieces)
    np_ = len(pieces)

    def body(*refs):
        a_refs, b_ref, o_ref, acc = refs[:np_], refs[np_], refs[np_ + 1], refs[np_ + 2]
        s, kk = pl.program_id(0), pl.program_id(1)

        @pl.when(kk == 0)
        def _():
            acc[...] = jnp.zeros_like(acc)

        for a_ref, (s0, ns, wt) in zip(a_refs, steps):
            @pl.when((s >= s0) & (s < s0 + ns))
            def _(a_ref=a_ref, wt=wt):
                acc[0:wt, :] += _dg(_bf(a_ref[...]), _bf(b_ref[...]), 0, 0)

        @pl.when(kk == nk - 1)
        def _():
            o_ref[...] = acc[...]

    def a_index(s, kk, s0, ns):
        active = (s >= s0) & (s < s0 + ns)
        return jnp.where(active, kk, jnp.where(s < s0, 0, nk - 1)), jnp.clip(s - s0, 0, ns - 1)

    a_specs = [pl.BlockSpec((tk, wt), functools.partial(a_index, s0=s0, ns=ns)) for s0, ns, wt in steps]
    return pl.pallas_call(
        body, name=name, grid=(n_steps, nk),
        in_specs=a_specs + [pl.BlockSpec((tk, n), lambda s, kk: (kk, 0))],
        out_specs=pl.BlockSpec((PIECE_TILE, n), lambda s, kk: (s, 0)), out_shape=jax.ShapeDtypeStruct((total, n), F32),
        scratch_shapes=[pltpu.VMEM((PIECE_TILE, n), F32)],
        compiler_params=_cparams("parallel", "arbitrary"),
    )(*pieces, b)


def _adamw_math(g, w, m, v):
    m_new = ADAM_B1 * m + (1.0 - ADAM_B1) * g
    v_new = ADAM_B2 * v + (1.0 - ADAM_B2) * jnp.square(g)
    m_hat = m_new / (1.0 - ADAM_B1 ** ADAM_STEP)
    v_hat = v_new / (1.0 - ADAM_B2 ** ADAM_STEP)
    return -ADAM_LR * (m_hat / (jnp.sqrt(v_hat) + ADAM_EPS) + ADAM_WD * w), m_new, v_new


def _adamw(parts, ws, ms, vs, name):
    nw = len(ws)
    nl, r, c = ws[0].shape
    k = parts[0][0].shape[0]
    tr = _tile(r, (128, 64, 32, 16, 8))
    nb = r // tr

    def body(*refs):
        p_refs, refs = refs[:nw * nl], refs[nw * nl:]
        w_refs, m_refs, v_refs, outs = refs[:nw], refs[nw:2 * nw], refs[2 * nw:3 * nw], refs[3 * nw:]
        layer = pl.program_id(0)
        for q in range(nl):
            @pl.when(layer == q)
            def _(q=q):
                for i in range(nw):
                    p_ref = p_refs[i * nl + q]
                    g = p_ref[0]
                    for j in range(1, k):
                        g = g + p_ref[j]
                    vals = (g,) + _adamw_math(g, w_refs[i][0], m_refs[i][0], v_refs[i][0])
                    for o_ref, val in zip(outs[4 * i:4 * i + 4], vals):
                        o_ref[0] = val

    spec = pl.BlockSpec((1, tr, c), lambda l, i: (l, i, 0))
    part_specs = [pl.BlockSpec((k, tr, c), functools.partial(
        lambda l, i, q: (0, jnp.where(l == q, i, jnp.where(l < q, 0, nb - 1)), 0), q=q)) for q in range(nl)]
    res = pl.pallas_call(
        body, name=name, grid=(nl, nb), in_specs=part_specs * nw + [spec] * (3 * nw),
        out_specs=[spec] * (4 * nw), out_shape=[jax.ShapeDtypeStruct((nl, r, c), F32)] * (4 * nw),
        compiler_params=_cparams("arbitrary", "arbitrary"),
    )(*[p for per_weight in parts for p in per_weight], *ws, *ms, *vs)
    return [res[4 * i:4 * i + 4] for i in range(nw)]


def _adamw_transposed(grads, w, m, v, name):
    nl, r, c = w.shape
    tc = 64
    views = [jnp.transpose(a, (2, 0, 1)) for a in (w, m, v)]

    def body(*refs):
        g_refs, (w_ref, m_ref, v_ref), outs = refs[:nl], refs[nl:nl + 3], refs[nl + 3:]
        for l in range(nl):
            g = g_refs[l][...]
            vals = (g,) + _adamw_math(g, w_ref[:, l, :], m_ref[:, l, :], v_ref[:, l, :])
            for o_ref, val in zip(outs, vals):
                o_ref[:, l, :] = val

    spec = pl.BlockSpec((tc, nl, r), lambda i: (i, 0, 0))
    res = pl.pallas_call(
        body, name=name, grid=(pl.cdiv(c, tc),), in_specs=[pl.BlockSpec((tc, r), lambda i: (i, 0))] * nl + [spec] * 3,
        out_specs=[spec] * 4, out_shape=[jax.ShapeDtypeStruct((c, nl, r), F32)] * 4,
        compiler_params=_cparams("parallel"),
    )(*grads, *views)
    return [jnp.transpose(a, (1, 2, 0)) for a in res]


def _adamw_small(gathered, ws, ms, vs, loss_terms, name):
    n = len(ws)

    def device_sum(ref):
        s = ref[0]
        for j in range(1, N_DEV):
            s = s + ref[j]
        return s

    def body(*refs):
        g_refs, w_refs, m_refs, v_refs = refs[:n], refs[n:2 * n], refs[2 * n:3 * n], refs[3 * n:4 * n]
        loss_ref, outs, loss_out = refs[4 * n], refs[4 * n + 1:-1], refs[-1]
        for i in range(n):
            g = device_sum(g_refs[i])
            vals = (g,) + _adamw_math(g, w_refs[i][...], m_refs[i][...], v_refs[i][...])
            for kind, val in enumerate(vals):
                outs[kind * n + i][...] = val
        loss_out[...] = device_sum(loss_ref)

    res = pl.pallas_call(
        body, name=name,
        out_shape=[jax.ShapeDtypeStruct(a.shape, F32) for _ in range(4) for a in ws] + [
            jax.ShapeDtypeStruct(loss_terms.shape[1:], F32)],
        compiler_params=pltpu.CompilerParams(vmem_limit_bytes=VMEM_LIMIT),
    )(*gathered, *ws, *ms, *vs, loss_terms)
    return [res[kind * n:(kind + 1) * n] for kind in range(4)], res[-1]


ANY = pl.BlockSpec(memory_space=pl.ANY)


def _place():
    return lax.axis_index("x"), lax.axis_index("y"), lax.axis_index("c")


def _handshake(peers):
    barrier = pltpu.get_barrier_semaphore()
    for peer in peers:
        pl.semaphore_signal(barrier, inc=1, device_id=peer, device_id_type=MESH)
    pl.semaphore_wait(barrier, len(peers))


def _comm_call(body, name, inputs, out_shape, scratch, sequencer_id=None, after=()):
    if sequencer_id is None:
        return pl.pallas_call(body, name=name, out_shape=out_shape, in_specs=[ANY] * len(inputs),
                              out_specs=[ANY] * len(out_shape), scratch_shapes=scratch)(*inputs)
    n_in, n_after = len(inputs), len(after)

    def sequencer_body(*refs):
        body(*refs[:n_in], *refs[n_in + n_after:])

    return pl.kernel(
        sequencer_body, out_type=out_shape, mesh=plsc.ScalarSubcoreMesh(axis_name="sequencer", num_cores=1),
        scratch_types=scratch, compiler_params=pltpu.CompilerParams(collective_id=sequencer_id), name=name,
    )(*inputs, *after)


def _all_gather(blocks, name, sequencer_id=None, after=()):
    n = len(blocks)

    def body(*refs):
        x_refs, out_refs = refs[:n], refs[n:2 * n]
        send_sems, recv_sems, local_sems = refs[2 * n:]
        x, y, c = _place()
        me, sibling = (x, y, c), (x, y, 1 - c)
        chips = [(1 - x, y), (x, 1 - y), (1 - x, 1 - y)]
        if sequencer_id is not None:
            _handshake([sibling] + [(*chip, c) for chip in chips])

        def slot(a, px, py, pc):
            return out_refs[a].at[4 * px + 2 * py + pc]

        def copy(a, k, blk, to, src=None):
            return pltpu.make_async_remote_copy(
                src_ref=slot(a, *blk) if src is None else src, dst_ref=slot(a, *blk),
                send_sem=send_sems.at[7 * a + k], recv_sem=recv_sems.at[7 * a + k], device_id=to, device_id_type=MESH)

        mine = [pltpu.make_async_copy(x_refs[a], slot(a, *me), local_sems.at[a]) for a in range(n)]
        first = []
        for a in range(n):
            mine[a].start()
            first.append(copy(a, 0, me, sibling, src=x_refs[a]))
            first += [copy(a, 1 + j, me, (*chip, c), src=x_refs[a]) for j, chip in enumerate(chips)]
        for cp in first:
            cp.start()
        passed = []
        for j, chip in enumerate(chips):
            for a in range(n):
                copy(a, 1 + j, (*chip, c), me).wait_recv()
                passed.append(copy(a, 4 + j, (*chip, c), sibling))
                passed[-1].start()
        for a in range(n):
            copy(a, 0, sibling, me).wait_recv()
            for j, chip in enumerate(chips):
                copy(a, 4 + j, (*chip, 1 - c), me).wait_recv()
        for cp in first + passed:
            cp.wait_send()
        for cp in mine:
            cp.wait()

    return _comm_call(
        body, name, blocks, [jax.ShapeDtypeStruct((N_DEV,) + b.shape, b.dtype) for b in blocks],
        [pltpu.SemaphoreType.DMA((7 * n,)), pltpu.SemaphoreType.DMA((7 * n,)), pltpu.SemaphoreType.DMA((n,))],
        sequencer_id, after)


def _exchange_sibling(gs, name, sequencer_id=None, after=()):
    n = len(gs)

    def body(*refs):
        g_refs, out_refs = refs[:n], refs[n:2 * n]
        send_sems, recv_sems = refs[2 * n:]
        x, y, c = _place()
        if sequencer_id is not None:
            _handshake([(x, y, 1 - c)])
        copies = [pltpu.make_async_remote_copy(
            src_ref=g_refs[a].at[2 * k + 1 - c], dst_ref=out_refs[a].at[k], send_sem=send_sems.at[4 * a + k],
            recv_sem=recv_sems.at[4 * a + k], device_id=(x, y, 1 - c), device_id_type=MESH)
            for a in range(n) for k in range(4)]
        for cp in copies:
            cp.start()
        for cp in copies:
            cp.wait()

    return _comm_call(body, name, gs, [jax.ShapeDtypeStruct((4,) + g.shape[1:], g.dtype) for g in gs],
                      [pltpu.SemaphoreType.DMA((4 * n,)), pltpu.SemaphoreType.DMA((4 * n,))], sequencer_id, after)


def _other_chips():
    x, y = lax.axis_index("x"), lax.axis_index("y")
    return jnp.stack([2 * (1 - x) + y, 2 * x + 1 - y, 2 * (1 - x) + 1 - y]).astype(jnp.int32)


def _add_sibling(gs, r1s, name, after=()):
    n = len(gs)
    _, r, w = gs[0].shape
    tr, tc = _elementwise_block(r, w)
    chips = _other_chips()
    slabs = 2 * chips + lax.axis_index("c").astype(jnp.int32)

    def body(slab_ref, chip_ref, *refs):
        g_refs, r_refs, o_refs = refs[:n], refs[n:2 * n], refs[2 * n + len(after):]
        for g_ref, r_ref, o_ref in zip(g_refs, r_refs, o_refs):
            o_ref[...] = (g_ref[...] + r_ref[...]).astype(BF16)

    return pl.pallas_call(
        body, name=name, out_shape=[jax.ShapeDtypeStruct((3, r, w), BF16)] * n,
        grid_spec=pltpu.PrefetchScalarGridSpec(
            num_scalar_prefetch=2, grid=(3, r // tr, w // tc),
            in_specs=[pl.BlockSpec((1, tr, tc), lambda k, i, j, slab_ref, chip_ref: (slab_ref[k], i, j))] * n
            + [pl.BlockSpec((1, tr, tc), lambda k, i, j, slab_ref, chip_ref: (chip_ref[k], i, j))] * n
            + [ANY] * len(after),
            out_specs=[pl.BlockSpec((1, tr, tc), lambda k, i, j, slab_ref, chip_ref: (k, i, j))] * n),
        compiler_params=_cparams("parallel", "parallel", "parallel"),
    )(slabs, chips, *gs, *r1s, *after)


def _exchange_chips(ps, name, sequencer_id=None, after=()):
    n = len(ps)

    def body(*refs):
        p_refs, out_refs = refs[:n], refs[n:2 * n]
        send_sems, recv_sems = refs[2 * n:]
        x, y, c = _place()
        chips = [(1 - x, y), (x, 1 - y), (1 - x, 1 - y)]
        if sequencer_id is not None:
            _handshake([(*chip, c) for chip in chips])
        copies = [pltpu.make_async_remote_copy(
            src_ref=p_refs[a].at[j], dst_ref=out_refs[a].at[j], send_sem=send_sems.at[3 * a + j],
            recv_sem=recv_sems.at[3 * a + j], device_id=(px, py, c), device_id_type=MESH)
            for a in range(n) for j, (px, py) in enumerate(chips)]
        for cp in copies:
            cp.start()
        for cp in copies:
            cp.wait()

    return _comm_call(body, name, ps, [jax.ShapeDtypeStruct(p.shape, p.dtype) for p in ps],
                      [pltpu.SemaphoreType.DMA((3 * n,)), pltpu.SemaphoreType.DMA((3 * n,))], sequencer_id, after)


def _sum_chips(gs, r1s, others, name):
    n = len(gs)
    _, r, c = gs[0].shape
    tr, tc = _elementwise_block(r, c)
    chip = 2 * lax.axis_index("x") + lax.axis_index("y")
    place = jnp.stack([2 * chip + lax.axis_index("c"), chip]).astype(jnp.int32)

    def body(place_ref, *refs):
        for g_ref, r_ref, others_ref, o_ref in zip(refs[:n], refs[n:2 * n], refs[2 * n:3 * n], refs[3 * n:]):
            s = g_ref[0] + r_ref[0]
            for j in range(3):
                s = s + others_ref[j].astype(F32)
            o_ref[...] = s

    return pl.pallas_call(
        body, name=name, out_shape=[jax.ShapeDtypeStruct((r, c), F32)] * n,
        grid_spec=pltpu.PrefetchScalarGridSpec(
            num_scalar_prefetch=1, grid=(r // tr, c // tc),
            in_specs=[pl.BlockSpec((1, tr, tc), lambda i, j, place_ref: (place_ref[0], i, j))] * n
            + [pl.BlockSpec((1, tr, tc), lambda i, j, place_ref: (place_ref[1], i, j))] * n
            + [pl.BlockSpec((3, tr, tc), lambda i, j, place_ref: (0, i, j))] * n,
            out_specs=[pl.BlockSpec((tr, tc), lambda i, j, place_ref: (i, j))] * n),
        compiler_params=_cparams("parallel", "parallel"),
    )(place, *gs, *r1s, *others)


def _reorder_in_proj(wt):
    za_zb, zc, xbc, dt, gates = (wt[:4096], wt[4096:5120], wt[5120:7168], wt[7168:7184], wt[7184:])
    return jnp.concatenate([za_zb, xbc, gates, zc, dt, jnp.zeros((DT_PAD - 16, wt.shape[1]), wt.dtype)], axis=0)


def _restore_in_proj(wt):
    return jnp.concatenate([wt[:4096], wt[OFF_ZC:OFF_ZC + W_ZC], wt[OFF_XBC:OFF_XBC + W_XBC],
                            wt[OFF_DT:OFF_DT + 16], wt[OFF_GATE:OFF_GATE + W_GATE]], axis=0)


def _lanes_from_devices(g):
    return jnp.moveaxis(g, 0, 1).reshape(g.shape[1], N_DEV * g.shape[2])


def _lanes_to_devices(a):
    return jnp.moveaxis(a.reshape(a.shape[0], N_DEV, a.shape[1] // N_DEV), 1, 0)


def _pad_lanes(a, width):
    return jnp.pad(a, ((0, 0), (0, width - a.shape[1])))


BIG = ("w_in", "w_branch_a", "w_branch_b", "w_branch_c", "w_out", "w_mlp_up", "w_mlp_down")
SMALL_SHARDED = ("b_gate", "lru_conv_w", "ssd_conv_w")
REPLICATED = ("norm_mix_g", "gmlp_ln_g", "gmlp_ln_b", "gmlp_w_s", "gmlp_b_s", "lru_conv_b", "lru_w_r", "lru_b_r",
              "lru_w_i", "lru_b_i", "lru_lambda", "ssd_conv_b", "ssd_dt_bias", "ssd_a_log", "ssd_d", "ssd_norm_g",
              "norm_mlp_g", "final_norm_g")
WEIGHTS = ("norm_mix_g", "w_in", "b_gate", "gmlp_ln_g", "gmlp_ln_b", "gmlp_w_s", "gmlp_b_s", "lru_conv_w", "lru_conv_b",
           "lru_w_r", "lru_b_r", "lru_w_i", "lru_b_i", "lru_lambda", "ssd_conv_w", "ssd_conv_b", "ssd_dt_bias",
           "ssd_a_log", "ssd_d", "ssd_norm_g", "w_branch_a", "w_branch_b", "w_branch_c", "w_out", "norm_mlp_g",
           "w_mlp_up", "w_mlp_down", "final_norm_g")
TRANSPOSED = ("w_in", "w_mlp_up")
SMALL_MATRICES = ("gmlp_w_s", "lru_w_r", "lru_w_i")
SMALL_VECTORS = tuple(n for n in REPLICATED if n not in SMALL_MATRICES)
GRADIENT_GROUPS = {"mlp": ("w_mlp_up", "w_mlp_down"), "mix": ("w_branch_a", "w_branch_b", "w_branch_c", "w_out"),
                   "in": ("w_in",)}


def _layer_params(full, l):
    row = lambda a: a.reshape(1, -1)
    return dict(
        norm_mix_g=row(full["norm_mix_g"][l]), norm_mlp_g=row(full["norm_mlp_g"][l]),
        gmlp=(row(full["gmlp_ln_g"][l]), row(full["gmlp_ln_b"][l]), full["gmlp_w_s"][l].reshape(GROUPS_A * CHUNK, CHUNK),
              full["gmlp_b_s"][l].T),
        lru=(full["lru_conv_w"][l], row(full["lru_conv_b"][l]), full["lru_w_r"][l].reshape(D, D // HEADS_B),
             row(full["lru_b_r"][l]), full["lru_w_i"][l].reshape(D, D // HEADS_B), row(full["lru_b_i"][l]),
             row(full["lru_lambda"][l])),
        ssd=(full["ssd_conv_w"][l], row(full["ssd_conv_b"][l]), _pad_lanes(row(full["ssd_dt_bias"][l]), DT_PAD),
             _pad_lanes(row(full["ssd_a_log"][l]), DT_PAD), _pad_lanes(row(full["ssd_d"][l]), DT_PAD),
             row(full["ssd_norm_g"][l])),
        b_gate=full["b_gate"][l],
    )


def _forward_layer(h, hn, p, wb, l, after_mixers=None, next_norm_g=None):
    tag = f"l{l}"
    t_row = 512
    if hn is None:
        (hn,), _, _ = _chunk_fwd(_f_rmsnorm, f"norm_mix_{tag}", t_row, [(h, 0, D)], [p["norm_mix_g"]], [(D, BF16)])
    proj = _matmul(hn, wb["w_in"], f"in_proj_{tag}", tb=True)
    (ya,), _, _ = _chunk_fwd(_f_gmlp, f"gmlp_{tag}", CHUNK, [(proj, OFF_ZA, W_ZA)], p["gmlp"], [(D, BF16)])
    lru_xs = [(proj, OFF_ZB, D), (proj, OFF_ZB + D, D)]
    (yb,), lru_saved, _ = _chunk_fwd(_f_lru, f"lru_{tag}", CHUNK, lru_xs, p["lru"], [(D, BF16)], halo_idx=(0,),
                                     carry_shapes=[(1, D)], save_carries=True)
    ssd_xs = [(proj, OFF_ZC, W_ZC), (proj, OFF_XBC, W_XBC), (proj, OFF_DT, W_DT)]
    (yc,), ssd_saved, _ = _chunk_fwd(_f_ssd, f"ssd_{tag}", CHUNK, ssd_xs, p["ssd"], [(D, BF16)], halo_idx=(1,),
                                     carry_shapes=[(HEADS_C * HEAD_DIM_C, STATE_C)], save_carries=True)
    if after_mixers is not None:
        after_mixers(yc)
    pa = _matmul(ya, wb["w_branch_a"], f"branch_a_{tag}")
    pb = _matmul(yb, wb["w_branch_b"], f"branch_b_{tag}")
    pc = _matmul(yc, wb["w_branch_c"], f"branch_c_{tag}")
    merge_xs = [(pa, 0, D), (pb, 0, D), (pc, 0, D), (proj, OFF_GATE, W_GATE)]
    (merged,), _, _ = _chunk_fwd(_f_merge, f"merge_{tag}", t_row, merge_xs, [p["b_gate"]], [(D, BF16)])

    def add_and_norm(acc, res, g):
        s = acc + res
        return s, _rms(s, g)

    h_mid, hn2 = _matmul(merged, wb["w_out"], f"out_proj_{tag}", outs=(F32, BF16), epilogue=add_and_norm, extras=(h,),
                         row_params=(p["norm_mlp_g"],))

    def relu_sq(acc):
        r = jnp.maximum(acc, 0.0)
        return r, r * r

    relu_up, act = _matmul(hn2, wb["w_mlp_up"], f"mlp_up_{tag}", tb=True, outs=(F32, BF16), epilogue=relu_sq)
    if next_norm_g is None:
        hn_out = None
        h_out = _matmul(act, wb["w_mlp_down"], f"mlp_down_{tag}", epilogue=lambda acc, res: (acc + res,), extras=(h_mid,))
    else:
        h_out, hn_out = _matmul(act, wb["w_mlp_down"], f"mlp_down_{tag}", outs=(F32, BF16), epilogue=add_and_norm,
                                extras=(h_mid,), row_params=(next_norm_g,))
    saved = dict(h=h, hn=hn, proj=proj, ya=ya, yb=yb, yc=yc, lru_saved=lru_saved, ssd_saved=ssd_saved, pa=pa, pb=pb,
                 pc=pc, merged=merged, h_mid=h_mid, hn2=hn2, relu_up=relu_up, act=act, lru_xs=lru_xs, ssd_xs=ssd_xs,
                 merge_xs=merge_xs)
    return h_out, hn_out, saved


def _backward_layer(dh, sv, p, wb, l):
    tag = f"l{l}"
    t_row = 512
    g = {}
    d_up = _matmul(dh, wb["w_mlp_down"], f"d_act_{tag}", tb=True, outs=(BF16,),
                   epilogue=lambda acc, r: (acc * (2.0 * r),), extras=(sv["relu_up"],))
    g["w_mlp_down"] = _matmul(sv["act"], dh, f"dw_mlp_down_{tag}", ta=True)
    g["w_mlp_up"] = _matmul(d_up, sv["hn2"], f"dw_mlp_up_{tag}", ta=True)
    d_hn2 = _matmul(d_up, wb["w_mlp_up"], f"d_hn2_{tag}")
    (d_mid,), (g["norm_mlp_g"],) = _chunk_bwd(_f_rmsnorm_res, f"norm_mlp_bwd_{tag}", t_row, [(sv["h_mid"], 0, D)],
                                              [p["norm_mlp_g"]], [d_hn2, dh], [F32])
    d_merged = _matmul(d_mid, wb["w_out"], f"d_merged_{tag}", tb=True)
    g["w_out"] = _matmul(sv["merged"], d_mid, f"dw_out_{tag}", ta=True)
    (d_pa, d_pb, d_pc, d_gate), (g["b_gate"],) = _chunk_bwd(
        _f_merge, f"merge_bwd_{tag}", t_row, sv["merge_xs"], [p["b_gate"]], [d_merged], [BF16] * 4)
    d_y = {}
    for br, d_p, y in (("a", d_pa, sv["ya"]), ("b", d_pb, sv["yb"]), ("c", d_pc, sv["yc"])):
        g[f"w_branch_{br}"] = _matmul(y, d_p, f"dw_branch_{br}_{tag}", ta=True)
        d_y[br] = _matmul(d_p, wb[f"w_branch_{br}"], f"d_y{br}_{tag}", tb=True)
    (d_za,), g_gmlp = _chunk_bwd(_f_gmlp, f"gmlp_bwd_{tag}", CHUNK, [(sv["proj"], OFF_ZA, W_ZA)], p["gmlp"],
                                 [d_y["a"]], [BF16])
    (d_xb, d_gt), g_lru = _chunk_bwd(_f_lru, f"lru_bwd_{tag}", CHUNK, sv["lru_xs"], p["lru"], [d_y["b"]], [BF16] * 2,
                                     halo_idx=(0,), saved=sv["lru_saved"])
    (d_zc, d_xbc, d_dt), g_ssd = _chunk_bwd(_f_ssd, f"ssd_bwd_{tag}", CHUNK, sv["ssd_xs"], p["ssd"], [d_y["c"]],
                                            [BF16] * 3, halo_idx=(1,), saved=sv["ssd_saved"])
    d_proj = [d_za, d_xb, d_gt, d_xbc, d_gate, d_zc, d_dt]
    g["w_in"] = _matmul_pieces_t(d_proj, sv["hn"], f"dw_in_{tag}")
    d_hn = _matmul_pieces(d_proj, wb["w_in"], f"d_hn_{tag}")
    (d_h,), (g["norm_mix_g"],) = _chunk_bwd(_f_rmsnorm_res, f"norm_mix_bwd_{tag}", t_row, [(sv["h"], 0, D)],
                                            [p["norm_mix_g"]], [d_hn, d_mid], [F32])
    g["w_in"] = _restore_in_proj(g["w_in"])
    for n in BIG:
        g[n] = g[n].reshape(N_DEV, g[n].shape[0] // N_DEV, g[n].shape[1])
    g["gmlp_ln_g"], g["gmlp_ln_b"], g["gmlp_w_s"] = g_gmlp[:3]
    g["gmlp_b_s"] = g_gmlp[3].T
    (g["lru_conv_w"], g["lru_conv_b"], g["lru_w_r"], g["lru_b_r"], g["lru_w_i"], g["lru_b_i"], g["lru_lambda"]) = g_lru
    g["ssd_conv_w"], g["ssd_conv_b"] = g_ssd[:2]
    g["ssd_dt_bias"], g["ssd_a_log"], g["ssd_d"] = (a[:, :HEADS_C] for a in g_ssd[2:5])
    g["ssd_norm_g"] = g_ssd[5]
    g["later"] = {"mlp": d_merged, "mix": d_za, "in": d_h}
    return d_h, g


LOSS_ROWS = 512


def _loss_and_grads(h, target, full, layer_weights, first_gathered):
    seq = h.shape[0]
    layer_p = [_layer_params(full, l) for l in range(DEPTH)]
    layer_w = [layer_weights(0, (first_gathered,))]
    saved = []
    hn = None
    for l in range(DEPTH):
        fetch_next = next_norm_g = None
        if l + 1 < DEPTH:
            fetch_next = lambda y, l=l: layer_w.append(layer_weights(l + 1, (layer_w[l]["w_mlp_down"], y)))
            next_norm_g = layer_p[l + 1]["norm_mix_g"]
        h, hn, sv = _forward_layer(h, hn, layer_p[l], layer_w[l], l, fetch_next, next_norm_g)
        saved.append(sv)
    final_g = full["final_norm_g"].reshape(1, D)
    loss_xs = [(h, 0, D), (target, 0, D)]
    t_loss = min(LOSS_ROWS, seq)
    _, _, (loss_acc,) = _chunk_fwd(_f_loss, "loss", t_loss, loss_xs, [final_g], [], carry_shapes=[(1, 128)],
                                   final_carries=True)
    zero_acc = jnp.zeros((seq // t_loss, 1, 128), F32)
    seed = lambda shape: (lax.broadcasted_iota(jnp.int32, shape, 1) == 0).astype(F32)
    (dh, _), (g_final,) = _chunk_bwd(_f_loss, "loss_bwd", t_loss, loss_xs, [final_g], [], [F32, F32], saved=[zero_acc],
                                     carry_seed=seed)
    layer_g = [None] * DEPTH
    for l in reversed(range(DEPTH)):
        dh, layer_g[l] = _backward_layer(dh, saved[l], layer_p[l], layer_w[l], l)
    return loss_acc, dh, layer_g, g_final


def _small_views(d):
    views = {n: d[n] for n in REPLICATED}
    views["gmlp_b_s"] = d["gmlp_b_s"].reshape(DEPTH * GROUPS_A, CHUNK)
    views["final_norm_g"] = d["final_norm_g"].reshape(1, D)
    for n in SMALL_MATRICES:
        views[n] = d[n].reshape(DEPTH * D, D // HEADS_B)
    return views


def kernel(x, norm_mix_g, w_in, b_gate, gmlp_ln_g, gmlp_ln_b, gmlp_w_s, gmlp_b_s, lru_conv_w, lru_conv_b, lru_w_r, lru_b_r, lru_w_i, lru_b_i, lru_lambda, ssd_conv_w, ssd_conv_b, ssd_dt_bias, ssd_a_log, ssd_d, ssd_norm_g, w_branch_a, w_branch_b, w_branch_c, w_out, norm_mlp_g, w_mlp_up, w_mlp_down, final_norm_g, loss_target, m_norm_mix_g, m_w_in, m_b_gate, m_gmlp_ln_g, m_gmlp_ln_b, m_gmlp_w_s, m_gmlp_b_s, m_lru_conv_w, m_lru_conv_b, m_lru_w_r, m_lru_b_r, m_lru_w_i, m_lru_b_i, m_lru_lambda, m_ssd_conv_w, m_ssd_conv_b, m_ssd_dt_bias, m_ssd_a_log, m_ssd_d, m_ssd_norm_g, m_w_branch_a, m_w_branch_b, m_w_branch_c, m_w_out, m_norm_mlp_g, m_w_mlp_up, m_w_mlp_down, m_final_norm_g, v_norm_mix_g, v_w_in, v_b_gate, v_gmlp_ln_g, v_gmlp_ln_b, v_gmlp_w_s, v_gmlp_b_s, v_lru_conv_w, v_lru_conv_b, v_lru_w_r, v_lru_b_r, v_lru_w_i, v_lru_b_i, v_lru_lambda, v_ssd_conv_w, v_ssd_conv_b, v_ssd_dt_bias, v_ssd_a_log, v_ssd_d, v_ssd_norm_g, v_w_branch_a, v_w_branch_b, v_w_branch_c, v_w_out, v_norm_mlp_g, v_w_mlp_up, v_w_mlp_down, v_final_norm_g):
    args = locals()
    w = {n: args[n] for n in WEIGHTS}
    m = {n: args["m_" + n] for n in WEIGHTS}
    v = {n: args["v_" + n] for n in WEIGHTS}
    seq = x.shape[1]
    h = x.reshape(seq, D)
    target = loss_target.reshape(seq, D)

    def shard_on_wire(n, l):
        return (w[n][l].T if n in TRANSPOSED else w[n][l]).astype(BF16)

    first = _all_gather([shard_on_wire("w_in", 0)] + [w[n] for n in SMALL_SHARDED], "gather_weights_first")
    full = {n: w[n] for n in REPLICATED}
    for n, g in zip(SMALL_SHARDED, first[1:]):
        full[n] = jnp.stack([_lanes_from_devices(g[:, l]) for l in range(DEPTH)])

    def layer_weights(l, after):
        have = {"w_in": first[0]} if l == 0 else {}
        names = [n for n in BIG if n not in have]
        later = _all_gather([shard_on_wire(n, l) for n in names], f"gather_weights_l{l}", SEQ_GATHER + l, after=after)
        have.update(zip(names, later))
        wl = {n: have[n].reshape(-1, D) for n in BIG}
        wl["w_in"] = _reorder_in_proj(wl["w_in"])
        return wl

    loss_local, dh, layer_g, g_final = _loss_and_grads(h, target, full, layer_weights, first[0])
    grad_x = dh.reshape(x.shape)
    out = {}
    kinds = ("grad", "delta", "new_m", "new_v")

    sequencer_before = {}

    def reduce_scatter(slabs, tag, on_sequencer, later=()):
        keys = list(slabs)
        ids = (SEQ_TO_SIBLING, SEQ_TO_CHIPS) if on_sequencer else (None, None)
        from_sibling = _exchange_sibling([slabs[k] for k in keys], f"grads_to_sibling_{tag}", ids[0],
                                         sequencer_before.get("sibling", ()))
        from_sibling = dict(zip(keys, from_sibling))
        same_shape = {}
        for k in keys:
            same_shape.setdefault(slabs[k].shape[1:], []).append(k)
        chip_sums = {}
        for (r, c), ks in same_shape.items():
            sums = _add_sibling([slabs[k] for k in ks], [from_sibling[k] for k in ks], f"add_sibling_{tag}_{r}x{c}", later)
            chip_sums.update(zip(ks, sums))
        from_chips = _exchange_chips([chip_sums[k] for k in keys], f"grads_to_chips_{tag}", ids[1],
                                     sequencer_before.get("chips", ()))
        if on_sequencer:
            sequencer_before["sibling"], sequencer_before["chips"] = (from_sibling[keys[0]],), (from_chips[0],)
        from_chips = dict(zip(keys, from_chips))
        summed = {}
        for (r, c), ks in same_shape.items():
            sums = _sum_chips([slabs[k] for k in ks], [from_sibling[k] for k in ks], [from_chips[k] for k in ks],
                              f"sum_chips_{tag}_{r}x{c}")
            summed.update(zip(ks, sums))
        return summed

    small = {}
    for n in SMALL_SHARDED:
        small[n, None] = jnp.concatenate([_lanes_to_devices(layer_g[l][n]) for l in range(DEPTH)], axis=1)
    g_small = {n: jnp.concatenate([layer_g[l][n] for l in range(DEPTH)], axis=0) for n in REPLICATED[:-1]}
    g_small["final_norm_g"] = g_final
    for n in SMALL_MATRICES:
        small[n, None] = g_small[n].reshape(N_DEV, -1, g_small[n].shape[-1])
    reduced = {}
    groups = [(l, grp) for l in range(DEPTH - 1, -1, -1) for grp in GRADIENT_GROUPS]
    for l, grp in groups:
        if (l, grp) == groups[-1]:
            reduced.update(reduce_scatter(small, "small", True, (layer_g[l][GRADIENT_GROUPS[grp][0]],)))
        slabs = {(n, l): layer_g[l][n] for n in GRADIENT_GROUPS[grp]}
        reduced.update(reduce_scatter(slabs, f"{grp}_l{l}", True, (layer_g[l]["later"][grp],)))
    same_shape = {}
    for n in BIG:
        grads = [reduced[n, l] for l in range(DEPTH)]
        if n in TRANSPOSED and w[n].shape[-1] % LANES:
            for kind, a in zip(kinds, _adamw_transposed(grads, w[n], m[n], v[n], f"adamw_{n}")):
                out[kind, n] = a
        else:
            same_shape.setdefault(w[n].shape, []).append((n, [(g.T if n in TRANSPOSED else g)[None] for g in grads]))
    for shape, group in same_shape.items():
        names = [n for n, _ in group]
        res = _adamw([p for _, p in group], [w[n] for n in names], [m[n] for n in names], [v[n] for n in names],
                     f"adamw_{names[0]}")
        for n, per_weight in zip(names, res):
            for kind, a in zip(kinds, per_weight):
                out[kind, n] = a
    for n in SMALL_SHARDED:
        one = lambda a: a.reshape((1, -1, a.shape[-1]))
        (res,) = _adamw([[reduced[n, None][None]]], [one(w[n])], [one(m[n])], [one(v[n])], f"adamw_{n}")
        for kind, a in zip(kinds, res):
            out[kind, n] = a.reshape(w[n].shape)

    to_gather = [reduced[n, None] if n in SMALL_MATRICES else g_small[n] for n in REPLICATED]
    *g_gathered, loss_terms = _all_gather(to_gather + [loss_local], "gather_small_grads", SEQ_GATHER_SMALL)
    g_all = dict(zip(REPLICATED, g_gathered))
    wv, mv, vv = _small_views(w), _small_views(m), _small_views(v)
    res, loss_sum = _adamw_small([g_all[n] for n in SMALL_VECTORS],
                                 *[[d[n] for n in SMALL_VECTORS] for d in (wv, mv, vv)], loss_terms, "adamw_vectors")
    loss = loss_sum[0, 0]
    for kind, arrays in zip(kinds, res):
        for n, a in zip(SMALL_VECTORS, arrays):
            out[kind, n] = a.reshape(w[n].shape)
    g_full = [[g_all[n].reshape((1,) + wv[n].shape)] for n in SMALL_MATRICES]
    res = _adamw(g_full, *[[d[n][None] for n in SMALL_MATRICES] for d in (wv, mv, vv)], "adamw_small_matrices")
    for n, per_weight in zip(SMALL_MATRICES, res):
        for kind, a in zip(kinds, per_weight):
            out[kind, n] = a.reshape(w[n].shape)

    return (loss, grad_x, *[out[kind, n] for kind in kinds for n in WEIGHTS])
```

```python
import functools

import jax
import jax.numpy as jnp
from jax import lax
from jax.experimental import pallas as pl
from jax.experimental.pallas import tpu as pltpu
from jax.experimental.pallas import tpu_sc as plsc

F32 = jnp.float32
BF16 = jnp.bfloat16
MESH = pl.DeviceIdType.MESH

D = 1024
DEPTH = 2
EPS = 1e-6
CHUNK = 128
GROUPS_A = 8
HEADS_B = 8
LRU_C = 8.0
HEADS_C = 16
HEAD_DIM_C = 64
GROUPS_C = 4
STATE_C = 128
DT_PAD = 128
OFF_ZA, W_ZA = 0, 2048
OFF_ZB, W_ZB = 2048, 2048
OFF_XBC, W_XBC = 4096, 2048
OFF_GATE, W_GATE = 6144, 3072
OFF_ZC, W_ZC = 9216, 1024
OFF_DT, W_DT = 10240, DT_PAD
D_IN_PAD = 10368
N_DEV = 8
SEQ_GATHER = 1
SEQ_TO_SIBLING = SEQ_GATHER + DEPTH
SEQ_TO_CHIPS = SEQ_TO_SIBLING + 1
SEQ_GATHER_SMALL = SEQ_TO_CHIPS + 1

ADAM_LR = 0.001
ADAM_B1 = 0.9
ADAM_B2 = 0.999
ADAM_EPS = 1e-08
ADAM_WD = 0.01
ADAM_STEP = 10

VMEM_LIMIT = 56 * 1024 * 1024
HALO = 8


def _cparams(*sem):
    return pltpu.CompilerParams(dimension_semantics=sem, vmem_limit_bytes=VMEM_LIMIT)


def _bf(x):
    return x.astype(BF16)


def _dg(a, b, ca, cb):
    return lax.dot_general(a, b, (((ca,), (cb,)), ((), ())), preferred_element_type=F32)


@functools.partial(jax.custom_vjp, nondiff_argnums=(2, 3))
def _mm(a, b, ta, tb):
    return _dg(_bf(a), _bf(b), 0 if ta else 1, 1 if tb else 0)


def _mm_fwd(a, b, ta, tb):
    return _mm(a, b, ta, tb), (a, b)


def _mm_bwd(ta, tb, res, g):
    a, b = res
    ma = 1 if ta else 0
    nb = 0 if tb else 1
    gb, ab, bb = _bf(g), _bf(a), _bf(b)
    da = _dg(bb, gb, nb, 1) if ta else _dg(gb, bb, 1, nb)
    db = _dg(gb, ab, 0, ma) if tb else _dg(ab, gb, ma, 0)
    return da.astype(a.dtype), db.astype(b.dtype)


_mm.defvjp(_mm_fwd, _mm_bwd)


def _slices(x, sizes, axis):
    out, lo = [], 0
    for size in sizes:
        out.append(lax.slice_in_dim(x, lo, lo + size, axis=axis))
        lo += size
    return tuple(out)


@functools.partial(jax.custom_vjp, nondiff_argnums=(1,))
def _split_cols(x, widths):
    return _slices(x, widths, 1)


_split_cols.defvjp(lambda x, widths: (_slices(x, widths, 1), None),
                   lambda widths, _, gs: (jnp.concatenate(gs, axis=1),))


@functools.partial(jax.custom_vjp, nondiff_argnums=(1,))
def _split_rows(x, heights):
    return _slices(x, heights, 0)


_split_rows.defvjp(lambda x, heights: (_slices(x, heights, 0), None),
                   lambda heights, _, gs: (jnp.concatenate(gs, axis=0),))


def _col(x, j):
    lane = lax.broadcasted_iota(jnp.int32, x.shape, 1)
    return jnp.sum(jnp.where(lane == j, x, 0.0), axis=1, keepdims=True)


def _row(x, i):
    r = lax.broadcasted_iota(jnp.int32, x.shape, 0)
    return jnp.sum(jnp.where(r == i, x, 0.0), axis=0, keepdims=True)


def _roll_down(x, s):
    return pltpu.roll(x, s, 0)


def _roll_up(x, s):
    return pltpu.roll(x, x.shape[0] - s, 0)


def _row_iota(x):
    return lax.broadcasted_iota(jnp.int32, x.shape, 0)


@functools.partial(jax.custom_vjp, nondiff_argnums=(2,))
def _shift_rows(halo, x, s):
    if s == 0:
        return x
    return _roll_down(jnp.concatenate([halo, x], axis=0), s)[HALO:]


def _shift_rows_fwd(halo, x, s):
    return _shift_rows(halo, x, s), None


def _shift_rows_bwd(s, _, g):
    if s == 0:
        return jnp.zeros((HALO, g.shape[1]), g.dtype), g
    ge = jnp.concatenate([jnp.zeros((HALO, g.shape[1]), g.dtype), g], axis=0)
    de = _roll_up(ge, s)
    return de[:HALO], de[HALO:]


_shift_rows.defvjp(_shift_rows_fwd, _shift_rows_bwd)


SUBLANES = 8
LANES = 128


def _scan_tiles(a, b, carry, up):
    n, c = a.shape
    nt = n // SUBLANES
    a = a.reshape(nt, SUBLANES, c)
    b = b.reshape(nt, SUBLANES, c)
    sub = lax.broadcasted_iota(jnp.int32, a.shape, 1)
    s = 1
    while s < SUBLANES:
        keep = (sub < SUBLANES - s) if up else (sub >= s)
        shift = SUBLANES - s if up else s
        a_sh = jnp.where(keep, pltpu.roll(a, shift, 1), 1.0)
        b_sh = jnp.where(keep, pltpu.roll(b, shift, 1), 0.0)
        b = a * b_sh + b
        a = a * a_sh
        s *= 2
    tiles = [None] * nt
    edge = 0 if up else SUBLANES - 1
    for j in (range(nt - 1, -1, -1) if up else range(nt)):
        tiles[j] = b[j] if carry is None else b[j] + a[j] * carry
        carry = tiles[j][edge:edge + 1, :]
    return jnp.concatenate(tiles, axis=0)


@jax.custom_vjp
def _lin_scan(a, b, h0):
    return _scan_tiles(a, b, h0, up=False)


def _lin_scan_fwd(a, b, h0):
    h = _lin_scan(a, b, h0)
    return h, (a, h0, h)


def _lin_scan_bwd(res, g):
    a, h0, h = res
    n = a.shape[0]
    row = _row_iota(a)
    a_next = jnp.where(row < n - 1, _roll_up(a, 1), 0.0)
    gg = _scan_tiles(a_next, g, None, up=True)
    h_prev = jnp.where(row >= 1, _roll_down(h, 1), h0)
    return gg * h_prev, gg, _row(a * gg, 0)


_lin_scan.defvjp(_lin_scan_fwd, _lin_scan_bwd)


@jax.custom_vjp
def _cumsum_rows(x):
    n = x.shape[0]
    row = _row_iota(x)
    s = 1
    while s < n:
        x = x + jnp.where(row >= s, _roll_down(x, s), 0.0)
        s *= 2
    return x


def _cumsum_rows_fwd(x):
    return _cumsum_rows(x), None


def _cumsum_rows_bwd(_, g):
    n = g.shape[0]
    row = _row_iota(g)
    s = 1
    while s < n:
        g = g + jnp.where(row < n - s, _roll_up(g, s), 0.0)
        s *= 2
    return (g,)


_cumsum_rows.defvjp(_cumsum_rows_fwd, _cumsum_rows_bwd)


def _sigmoid(x):
    return jax.nn.sigmoid(x)


def _softplus(x):
    return jnp.maximum(x, 0.0) + jnp.log1p(jnp.exp(-jnp.abs(x)))


def _gelu(x):
    return jax.nn.gelu(x, approximate=True)


def _neg_expm1(x):
    series = -x * (1.0 + x * (0.5 + x * (1.0 / 6.0 + x * (1.0 / 24.0))))
    return jnp.where(x > -0.01, series, 1.0 - jnp.exp(x))


def _rms(x, g):
    return x * lax.rsqrt(jnp.mean(x * x, axis=-1, keepdims=True) + EPS) * g


def _f_rmsnorm(carries, halos, xs, params):
    (h,) = xs
    (g,) = params
    return (), (_rms(h, g),)


def _f_rmsnorm_res(carries, halos, xs, params):
    (h,) = xs
    (g,) = params
    return (), (_rms(h, g), h)


def _f_gmlp(carries, halos, xs, params):
    (za,) = xs
    ln_g, ln_b, w_s, b_st = params
    u, v = _split_cols(_gelu(za), (D, D))
    vc = v - jnp.mean(v, axis=-1, keepdims=True)
    vn = vc * lax.rsqrt(jnp.mean(vc * vc, axis=-1, keepdims=True) + EPS) * ln_g + ln_b
    q = CHUNK
    causal = lax.broadcasted_iota(jnp.int32, (q, q), 0) >= lax.broadcasted_iota(jnp.int32, (q, q), 1)
    mixed = []
    for g, (w, vg) in enumerate(zip(_split_rows(w_s, (q,) * GROUPS_A), _split_cols(vn, (q,) * GROUPS_A))):
        mixed.append(_mm(jnp.where(causal, w, 0.0), vg, False, False) + _col(b_st, g))
    return (), (u * jnp.concatenate(mixed, axis=1),)


def _conv4(halo, x, w, b):
    y = b + _row(w, 3) * x
    for k in range(3):
        y = y + _row(w, k) * _shift_rows(halo, x, 3 - k)
    return y


def _f_lru(carries, halos, xs, params):
    (h0,) = carries
    (halo,) = halos
    xb_pre, gate = xs
    conv_w, conv_b, w_r, b_r, w_i, b_i, lam = params
    xb = _conv4(halo, xb_pre, conv_w, conv_b)
    hd = D // HEADS_B
    r_parts, i_parts = [], []
    heads = (hd,) * HEADS_B
    for xh, wr, wi in zip(_split_cols(xb, heads), _split_rows(w_r, heads), _split_rows(w_i, heads)):
        r_parts.append(_mm(xh, wr, False, False))
        i_parts.append(_mm(xh, wi, False, False))
    r = _sigmoid(jnp.concatenate(r_parts, axis=1) + b_r)
    i = _sigmoid(jnp.concatenate(i_parts, axis=1) + b_i)
    log_a = -LRU_C * r * _softplus(-lam)
    a = jnp.exp(log_a)
    inp = jnp.sqrt(_neg_expm1(2.0 * log_a)) * (i * xb)
    h = _lin_scan(a, inp, h0)
    return (_row(h, h.shape[0] - 1),), (_gelu(gate) * h,)


def _f_ssd(carries, halos, xs, params):
    (st,) = carries
    (halo,) = halos
    z, xbc_pre, dt_raw = xs
    conv_w, conv_b, dt_bias, a_log, d_skip, norm_g = params
    t = z.shape[0]
    xc = _conv4(halo, xbc_pre, conv_w, conv_b)
    xbc = xc * _sigmoid(xc)
    x_all, b_all, c_all = _split_cols(xbc, (D, GROUPS_C * STATE_C, GROUPS_C * STATE_C))
    x_pairs = _split_cols(x_all, (128,) * (HEADS_C // 2))
    b_groups = _split_cols(b_all, (STATE_C,) * GROUPS_C)
    c_groups = _split_cols(c_all, (STATE_C,) * GROUPS_C)
    st_pairs = _split_rows(st, (128,) * (HEADS_C // 2))
    dt = _softplus(dt_raw + dt_bias)
    adt = dt * (-jnp.exp(a_log))
    acs = _cumsum_rows(adt)
    acs_t = acs.T
    a_last = _row(acs, t - 1)
    lo = lax.broadcasted_iota(jnp.int32, (t, 128), 1) < HEAD_DIM_C
    lo_rows = lax.broadcasted_iota(jnp.int32, (128, STATE_C), 0) < HEAD_DIM_C
    causal = lax.broadcasted_iota(jnp.int32, (t, t), 0) >= lax.broadcasted_iota(jnp.int32, (t, t), 1)
    y_parts, st_parts = [], []
    for g in range(GROUPS_C):
        bg, cg = b_groups[g], c_groups[g]
        cb = _mm(cg, bg, False, True)
        for pr in range(2):
            pair = 2 * g + pr
            h0, h1 = 2 * pair, 2 * pair + 1
            x2 = x_pairs[pair]
            ac0, ac1 = _col(acs, h0), _col(acs, h1)
            l0 = jnp.exp(jnp.where(causal, ac0 - _row(acs_t, h0), -1e30))
            l1 = jnp.exp(jnp.where(causal, ac1 - _row(acs_t, h1), -1e30))
            xdt = x2 * jnp.where(lo, _col(dt, h0), _col(dt, h1))
            y_diag = (_mm(cb * l0, jnp.where(lo, xdt, 0.0), False, False)
                      + _mm(cb * l1, jnp.where(lo, 0.0, xdt), False, False))
            al0, al1 = _col(a_last, h0), _col(a_last, h1)
            decay_s = jnp.where(lo, jnp.exp(al0 - ac0), jnp.exp(al1 - ac1))
            s_new = _mm(xdt * decay_s, bg, True, False)
            prev = st_pairs[pair]
            y_off = _mm(cg, prev, False, True) * jnp.where(lo, jnp.exp(ac0), jnp.exp(ac1))
            skip = jnp.where(lo, _col(d_skip, h0), _col(d_skip, h1))
            y_parts.append(y_diag + y_off + x2 * skip)
            st_parts.append(prev * jnp.where(lo_rows, jnp.exp(al0), jnp.exp(al1)) + s_new)
    y = jnp.concatenate(y_parts, axis=1) * (z * _sigmoid(z))
    gw = D // GROUPS_C
    yn = []
    for yg in _split_cols(y, (gw,) * GROUPS_C):
        yn.append(yg * lax.rsqrt(jnp.mean(yg * yg, axis=-1, keepdims=True) + EPS))
    return (jnp.concatenate(st_parts, axis=0),), (jnp.concatenate(yn, axis=1) * norm_g,)


def _f_merge(carries, halos, xs, params):
    pa, pb, pc, g_raw = xs
    (b_gate,) = params
    ga, gb, gc = _split_cols(g_raw, (D, D, D))
    m = (_sigmoid(ga + _row(b_gate, 0)) * pa + _sigmoid(gb + _row(b_gate, 1)) * pb
         + _sigmoid(gc + _row(b_gate, 2)) * pc)
    return (), (m,)


def _f_loss(carries, halos, xs, params):
    (acc,) = carries
    h, target = xs
    (g,) = params
    err = jnp.square(_rms(h, g) - target)
    part = 0.5 * jnp.sum(jnp.mean(err, axis=-1, keepdims=True), axis=0, keepdims=True)
    return (acc + part,), ()


def _x_specs(xs, t, index_of):
    specs = []
    for arr, off, width in xs:
        assert off % width == 0 and off + width <= arr.shape[1]
        specs.append(pl.BlockSpec((t, width), functools.partial(lambda j, cb: (index_of(j), cb), cb=off // width)))
    return specs


def _halo_specs(xs, halo_idx, t, index_of):
    specs = []
    for xi in halo_idx:
        _, off, width = xs[xi]
        specs.append(pl.BlockSpec(
            (HALO, width),
            functools.partial(lambda j, cb: (jnp.maximum(index_of(j) * (t // HALO) - 1, 0), cb), cb=off // width)))
    return specs


def _full_spec(a):
    return pl.BlockSpec(a.shape, functools.partial(lambda j, nd: (0,) * nd, nd=a.ndim))


def _chunk_fwd(f, name, t, xs, params, outs, halo_idx=(), carry_shapes=(), save_carries=False):
    s = xs[0][0].shape[0]
    n = s // t
    nx, nh, npar, no, nc = len(xs), len(halo_idx), len(params), len(outs), len(carry_shapes)
    ns = nc if save_carries else 0

    def body(*refs):
        x_refs, refs = refs[:nx], refs[nx:]
        h_refs, refs = refs[:nh], refs[nh:]
        p_refs, refs = refs[:npar], refs[npar:]
        y_refs, refs = refs[:no], refs[no:]
        s_refs, c_refs = refs[:ns], refs[ns:]
        i = pl.program_id(0)

        @pl.when(i == 0)
        def _():
            for c in c_refs:
                c[...] = jnp.zeros_like(c)

        carries = tuple(c[...] for c in c_refs)
        for s_ref, c in zip(s_refs, carries):
            s_ref[0] = c
        halos = tuple(jnp.where(i > 0, h[...].astype(F32), 0.0) for h in h_refs)
        new_c, ys = f(carries, halos, tuple(x[...].astype(F32) for x in x_refs), tuple(p[...] for p in p_refs))
        for y_ref, y in zip(y_refs, ys):
            y_ref[...] = y.astype(y_ref.dtype)
        for c, v in zip(c_refs, new_c):
            c[...] = v

    ident = lambda j: j
    out_shape = [jax.ShapeDtypeStruct((s, w), dt) for w, dt in outs]
    out_specs = [pl.BlockSpec((t, w), lambda j: (j, 0)) for w, _ in outs]
    if save_carries:
        out_shape += [jax.ShapeDtypeStruct((n,) + tuple(cs), F32) for cs in carry_shapes]
        out_specs += [pl.BlockSpec((1,) + tuple(cs), lambda j: (j, 0, 0)) for cs in carry_shapes]
    res = pl.pallas_call(
        body, name=name, grid=(n,),
        in_specs=_x_specs(xs, t, ident) + _halo_specs(xs, halo_idx, t, ident) + [_full_spec(p) for p in params],
        out_specs=out_specs, out_shape=out_shape,
        scratch_shapes=[pltpu.VMEM(tuple(cs), F32) for cs in carry_shapes],
        compiler_params=_cparams("arbitrary"),
    )(*[x[0] for x in xs], *[xs[xi][0] for xi in halo_idx], *params)
    return res[:no], res[no:no + ns], res[no + ns:]


def _chunk_bwd(f, name, t, xs, params, dys, dx_dtypes, halo_idx=(), saved=(), carry_seed=None, carry_growth=False):
    s = xs[0][0].shape[0]
    n = s // t
    nx, nh, npar, nc, ndy = len(xs), len(halo_idx), len(params), len(saved), len(dys)
    ng = nc if carry_growth else 0

    def body(*refs):
        x_refs, refs = refs[:nx], refs[nx:]
        h_refs, refs = refs[:nh], refs[nh:]
        p_refs, refs = refs[:npar], refs[npar:]
        s_refs, refs = refs[:nc], refs[nc:]
        dy_refs, refs = refs[:ndy], refs[ndy:]
        dx_refs, refs = refs[:nx], refs[nx:]
        dp_refs, refs = refs[:npar], refs[npar:]
        grow_refs, refs = refs[:ng], refs[ng:]
        dc_refs, dh_refs = refs[:nc], refs[nc:]
        j = pl.program_id(0)
        i = n - 1 - j

        @pl.when(j == 0)
        def _():
            for dc in dc_refs:
                dc[...] = jnp.zeros_like(dc) if carry_seed is None else carry_seed(dc.shape)
            for r in dh_refs + dp_refs + grow_refs:
                r[...] = jnp.zeros_like(r)

        carries = tuple(s_ref[0] for s_ref in s_refs)
        halos = tuple(jnp.where(i > 0, h[...].astype(F32), 0.0) for h in h_refs)
        x_vals = tuple(x[...].astype(F32) for x in x_refs)
        p_vals = tuple(p[...] for p in p_refs)
        (new_c, _), vjp = jax.vjp(f, carries, halos, x_vals, p_vals)
        for grow_ref, after, before in zip(grow_refs, new_c, carries):
            grow_ref[...] += after - before
        d_car, d_hal, d_xs, d_par = vjp((tuple(dc[...] for dc in dc_refs), tuple(d[...].astype(F32) for d in dy_refs)))
        d_xs = list(d_xs)
        for k, xi in enumerate(halo_idx):
            w = xs[xi][2]
            d_xs[xi] = d_xs[xi] + jnp.concatenate([jnp.zeros((t - HALO, w), F32), dh_refs[k][...]], axis=0)
            dh_refs[k][...] = jnp.where(i > 0, d_hal[k], 0.0)
        for dx_ref, dx in zip(dx_refs, d_xs):
            dx_ref[...] = dx.astype(dx_ref.dtype)
        for dp_ref, dp in zip(dp_refs, d_par):
            dp_ref[...] += dp
        for dc, v in zip(dc_refs, d_car):
            dc[...] = v

    rev = lambda j: n - 1 - j
    in_specs = (_x_specs(xs, t, rev) + _halo_specs(xs, halo_idx, t, rev) + [_full_spec(p) for p in params]
                + [pl.BlockSpec((1,) + a.shape[1:], lambda j: (n - 1 - j, 0, 0)) for a in saved]
                + [pl.BlockSpec((t, d.shape[1]), lambda j: (n - 1 - j, 0)) for d in dys])
    out_shape = ([jax.ShapeDtypeStruct((s, w), dt) for (_, _, w), dt in zip(xs, dx_dtypes)]
                 + [jax.ShapeDtypeStruct(p.shape, F32) for p in params])
    out_specs = ([pl.BlockSpec((t, w), lambda j: (n - 1 - j, 0)) for _, _, w in xs] + [_full_spec(p) for p in params])
    out_shape += [jax.ShapeDtypeStruct(a.shape[1:], F32) for a in saved[:ng]]
    out_specs += [pl.BlockSpec(a.shape[1:], lambda j: (0, 0)) for a in saved[:ng]]
    res = pl.pallas_call(
        body, name=name, grid=(n,), in_specs=in_specs, out_specs=out_specs, out_shape=out_shape,
        scratch_shapes=([pltpu.VMEM(a.shape[1:], F32) for a in saved]
                        + [pltpu.VMEM((HALO, xs[xi][2]), F32) for xi in halo_idx]),
        compiler_params=_cparams("arbitrary"),
    )(*[x[0] for x in xs], *[xs[xi][0] for xi in halo_idx], *params, *saved, *dys)
    return (res[:nx], res[nx:nx + npar]) + ((res[nx + npar:],) if carry_growth else ())


def _tile(dim, pref):
    for cand in pref:
        if dim % cand == 0:
            return cand
    return dim


def _matmul(a, b, name, ta=False, tb=False, outs=(F32,), epilogue=None, extras=(), row_params=()):
    m, k = (a.shape[1], a.shape[0]) if ta else a.shape
    n = b.shape[0] if tb else b.shape[1]
    tn = _tile(n, (1152, 1024, 512, 256, 128))
    tk = _tile(k, (1024, 1152, 512, 256, 128))
    nk = k // tk
    tall = nk == 1 and a.dtype == BF16 and not extras and not ta and n // tn >= 4
    tm = _tile(m, ((2048,) if tall else ()) + (1024, 1152, 512, 256, 128))
    ne, no = len(extras) + len(row_params), len(outs)
    ca, cb = (0 if ta else 1), (1 if tb else 0)

    def body(*refs):
        a_ref, b_ref = refs[:2]
        e_refs = refs[2:2 + ne]
        o_refs = refs[2 + ne:2 + ne + no]

        acc = refs[-1]
        kk = pl.program_id(2)

        @pl.when(kk == 0)
        def _():
            acc[...] = jnp.zeros_like(acc)

        acc[...] += _dg(_bf(a_ref[...]), _bf(b_ref[...]), ca, cb)

        @pl.when(kk == nk - 1)
        def _():
            res = acc[...]
            vals = (res,) if epilogue is None else epilogue(res, *[e[...] for e in e_refs])
            for o_ref, v in zip(o_refs, vals):
                o_ref[...] = v.astype(o_ref.dtype)

    a_spec = pl.BlockSpec((tk, tm), lambda i, j, kk: (kk, i)) if ta else pl.BlockSpec((tm, tk), lambda i, j, kk: (i, kk))
    b_spec = pl.BlockSpec((tn, tk), lambda i, j, kk: (j, kk)) if tb else pl.BlockSpec((tk, tn), lambda i, j, kk: (kk, j))
    mn_spec = pl.BlockSpec((tm, tn), lambda i, j, kk: (i, j))
    res = pl.pallas_call(
        body, name=name, grid=(m // tm, n // tn, nk),
        in_specs=[a_spec, b_spec] + [mn_spec] * len(extras)
        + [pl.BlockSpec((1, tn), lambda i, j, kk: (0, j))] * len(row_params),
        out_specs=[mn_spec] * no,
        out_shape=[jax.ShapeDtypeStruct((m, n), dt) for dt in outs],
        scratch_shapes=[pltpu.VMEM((tm, tn), F32)],
        compiler_params=_cparams("parallel", "parallel", "arbitrary"),
    )(a, b, *extras, *row_params)
    return res if no > 1 else res[0]


def _elementwise_block(r, c):
    if r % 8 == 0 and r >= 8:
        return _tile(r, (256, 128, 64, 32, 16, 8)), c
    return r, _tile(c, (256, 128))


PIECE_TILE = 1024


def _piece_steps(pieces):
    steps, s0 = [], 0
    for a in pieces:
        wt = min(a.shape[1], PIECE_TILE)
        assert a.shape[1] % wt == 0
        steps.append((s0, a.shape[1] // wt, wt))
        s0 += a.shape[1] // wt
    return steps, s0


def _matmul_pieces(pieces, b, name):
    steps, n_steps = _piece_steps(pieces)
    s_rows, n = pieces[0].shape[0], b.shape[1]
    tm = _tile(s_rows, (1024, 512, 256, 128))
    tail_rows = steps[-1][2]
    full_steps = n_steps - 1 if tail_rows < PIECE_TILE else n_steps
    b_tail = b[full_steps * PIECE_TILE:]
    np_ = len(pieces)

    def body(*refs):
        a_refs, b_ref, tail_ref, o_ref, acc = refs[:np_], refs[np_], refs[np_ + 1], refs[np_ + 2], refs[np_ + 3]
        s = pl.program_id(1)

        @pl.when(s == 0)
        def _():
            acc[...] = jnp.zeros_like(acc)

        for a_ref, (s0, ns, wt) in zip(a_refs, steps):
            @pl.when((s >= s0) & (s < s0 + ns))
            def _(a_ref=a_ref, wt=wt):
                rhs = b_ref[...] if wt == PIECE_TILE else tail_ref[...]
                acc[...] += _dg(_bf(a_ref[...]), _bf(rhs), 1, 0)

        @pl.when(s == n_steps - 1)
        def _():
            o_ref[...] = acc[...]

    n_rows = s_rows // tm

    def a_index(i, s, s0, ns):
        done = s >= s0 + ns
        return jnp.minimum(i + done.astype(jnp.int32), n_rows - 1), jnp.where(done, 0, jnp.clip(s - s0, 0, ns - 1))

    a_specs = [pl.BlockSpec((tm, wt), functools.partial(a_index, s0=s0, ns=ns)) for s0, ns, wt in steps]
    return pl.pallas_call(
        body, name=name, grid=(n_rows, n_steps),
        in_specs=a_specs + [pl.BlockSpec((PIECE_TILE, n), lambda i, s: (jnp.minimum(s, full_steps - 1), 0)),
                            pl.BlockSpec(b_tail.shape, lambda i, s: (0, 0))],
        out_specs=pl.BlockSpec((tm, n), lambda i, s: (i, 0)), out_shape=jax.ShapeDtypeStruct((s_rows, n), F32),
        scratch_shapes=[pltpu.VMEM((tm, n), F32)],
        compiler_params=_cparams("parallel", "arbitrary"),
    )(*pieces, b, b_tail)


def _matmul_pieces_t(pieces, b, name):
    steps, n_steps = _piece_steps(pieces)
    s_rows, n = b.shape
    tk = _tile(s_rows, (1024, 512, 256, 128))
    nk = s_rows // tk
    total = sum(a.shape[1] for a in pieces)
    np_ = len(pieces)

    def body(*refs):
        a_refs, b_ref, o_ref, acc = refs[:np_], refs[np_], refs[np_ + 1], refs[np_ + 2]
        s, kk = pl.program_id(0), pl.program_id(1)

        @pl.when(kk == 0)
        def _():
            acc[...] = jnp.zeros_like(acc)

        for a_ref, (s0, ns, wt) in zip(a_refs, steps):
            @pl.when((s >= s0) & (s < s0 + ns))
            def _(a_ref=a_ref, wt=wt):
                acc[0:wt, :] += _dg(_bf(a_ref[...]), _bf(b_ref[...]), 0, 0)

        @pl.when(kk == nk - 1)
        def _():
            o_ref[...] = acc[...]

    def a_index(s, kk, s0, ns):
        active = (s >= s0) & (s < s0 + ns)
        return jnp.where(active, kk, jnp.where(s < s0, 0, nk - 1)), jnp.clip(s - s0, 0, ns - 1)

    a_specs = [pl.BlockSpec((tk, wt), functools.partial(a_index, s0=s0, ns=ns)) for s0, ns, wt in steps]
    return pl.pallas_call(
        body, name=name, grid=(n_steps, nk),
        in_specs=a_specs + [pl.BlockSpec((tk, n), lambda s, kk: (kk, 0))],
        out_specs=pl.BlockSpec((PIECE_TILE, n), lambda s, kk: (s, 0)), out_shape=jax.ShapeDtypeStruct((total, n), F32),
        scratch_shapes=[pltpu.VMEM((PIECE_TILE, n), F32)],
        compiler_params=_cparams("parallel", "arbitrary"),
    )(*pieces, b)


def _adamw_math(g, w, m, v):
    m_new = ADAM_B1 * m + (1.0 - ADAM_B1) * g
    v_new = ADAM_B2 * v + (1.0 - ADAM_B2) * jnp.square(g)
    m_hat = m_new / (1.0 - ADAM_B1 ** ADAM_STEP)
    v_hat = v_new / (1.0 - ADAM_B2 ** ADAM_STEP)
    return -ADAM_LR * (m_hat / (jnp.sqrt(v_hat) + ADAM_EPS) + ADAM_WD * w), m_new, v_new


def _adamw(parts, ws, ms, vs, name):
    nw = len(ws)
    nl, r, c = ws[0].shape
    k = parts[0][0].shape[0]
    tr = _tile(r, (128, 64, 32, 16, 8))
    nb = r // tr

    def body(*refs):
        p_refs, refs = refs[:nw * nl], refs[nw * nl:]
        w_refs, m_refs, v_refs, outs = refs[:nw], refs[nw:2 * nw], refs[2 * nw:3 * nw], refs[3 * nw:]
        layer = pl.program_id(0)
        for q in range(nl):
            @pl.when(layer == q)
            def _(q=q):
                for i in range(nw):
                    p_ref = p_refs[i * nl + q]
                    g = p_ref[0]
                    for j in range(1, k):
                        g = g + p_ref[j]
                    vals = (g,) + _adamw_math(g, w_refs[i][0], m_refs[i][0], v_refs[i][0])
                    for o_ref, val in zip(outs[4 * i:4 * i + 4], vals):
                        o_ref[0] = val

    spec = pl.BlockSpec((1, tr, c), lambda l, i: (l, i, 0))
    part_specs = [pl.BlockSpec((k, tr, c), functools.partial(
        lambda l, i, q: (0, jnp.where(l == q, i, jnp.where(l < q, 0, nb - 1)), 0), q=q)) for q in range(nl)]
    res = pl.pallas_call(
        body, name=name, grid=(nl, nb), in_specs=part_specs * nw + [spec] * (3 * nw),
        out_specs=[spec] * (4 * nw), out_shape=[jax.ShapeDtypeStruct((nl, r, c), F32)] * (4 * nw),
        compiler_params=_cparams("arbitrary", "arbitrary"),
    )(*[p for per_weight in parts for p in per_weight], *ws, *ms, *vs)
    return [res[4 * i:4 * i + 4] for i in range(nw)]


def _adamw_transposed(grads, w, m, v, name):
    nl, r, c = w.shape
    tc = 64
    views = [jnp.transpose(a, (2, 0, 1)) for a in (w, m, v)]

    def body(*refs):
        g_refs, (w_ref, m_ref, v_ref), outs = refs[:nl], refs[nl:nl + 3], refs[nl + 3:]
        for l in range(nl):
            g = g_refs[l][...]
            vals = (g,) + _adamw_math(g, w_ref[:, l, :], m_ref[:, l, :], v_ref[:, l, :])
            for o_ref, val in zip(outs, vals):
                o_ref[:, l, :] = val

    spec = pl.BlockSpec((tc, nl, r), lambda i: (i, 0, 0))
    res = pl.pallas_call(
        body, name=name, grid=(pl.cdiv(c, tc),), in_specs=[pl.BlockSpec((tc, r), lambda i: (i, 0))] * nl + [spec] * 3,
        out_specs=[spec] * 4, out_shape=[jax.ShapeDtypeStruct((c, nl, r), F32)] * 4,
        compiler_params=_cparams("parallel"),
    )(*grads, *views)
    return [jnp.transpose(a, (1, 2, 0)) for a in res]


def _adamw_small(gathered, ws, ms, vs, loss_terms, name):
    n = len(ws)

    def device_sum(ref):
        s = ref[0]
        for j in range(1, N_DEV):
            s = s + ref[j]
        return s

    def body(*refs):
        g_refs, w_refs, m_refs, v_refs = refs[:n], refs[n:2 * n], refs[2 * n:3 * n], refs[3 * n:4 * n]
        loss_ref, outs, loss_out = refs[4 * n], refs[4 * n + 1:-1], refs[-1]
        for i in range(n):
            g = device_sum(g_refs[i])
            vals = (g,) + _adamw_math(g, w_refs[i][...], m_refs[i][...], v_refs[i][...])
            for kind, val in enumerate(vals):
                outs[kind * n + i][...] = val
        loss_out[...] = device_sum(loss_ref)

    res = pl.pallas_call(
        body, name=name,
        out_shape=[jax.ShapeDtypeStruct(a.shape, F32) for _ in range(4) for a in ws] + [
            jax.ShapeDtypeStruct(loss_terms.shape[1:], F32)],
        compiler_params=pltpu.CompilerParams(vmem_limit_bytes=VMEM_LIMIT),
    )(*gathered, *ws, *ms, *vs, loss_terms)
    return [res[kind * n:(kind + 1) * n] for kind in range(4)], res[-1]


ANY = pl.BlockSpec(memory_space=pl.ANY)


def _place():
    return lax.axis_index("x"), lax.axis_index("y"), lax.axis_index("c")


def _handshake(peers):
    barrier = pltpu.get_barrier_semaphore()
    for peer in peers:
        pl.semaphore_signal(barrier, inc=1, device_id=peer, device_id_type=MESH)
    pl.semaphore_wait(barrier, len(peers))


def _comm_call(body, name, inputs, out_shape, scratch, sequencer_id=None, after=()):
    if sequencer_id is None:
        return pl.pallas_call(body, name=name, out_shape=out_shape, in_specs=[ANY] * len(inputs),
                              out_specs=[ANY] * len(out_shape), scratch_shapes=scratch)(*inputs)
    n_in, n_after = len(inputs), len(after)

    def sequencer_body(*refs):
        body(*refs[:n_in], *refs[n_in + n_after:])

    return pl.kernel(
        sequencer_body, out_type=out_shape, mesh=plsc.ScalarSubcoreMesh(axis_name="sequencer", num_cores=1),
        scratch_types=scratch, compiler_params=pltpu.CompilerParams(collective_id=sequencer_id), name=name,
    )(*inputs, *after)


def _all_gather(blocks, name, sequencer_id=None, after=()):
    n = len(blocks)

    def body(*refs):
        x_refs, out_refs = refs[:n], refs[n:2 * n]
        send_sems, recv_sems, local_sems = refs[2 * n:]
        x, y, c = _place()
        me, sibling = (x, y, c), (x, y, 1 - c)
        chips = [(1 - x, y), (x, 1 - y), (1 - x, 1 - y)]
        if sequencer_id is not None:
            _handshake([sibling] + [(*chip, c) for chip in chips])

        def slot(a, px, py, pc):
            return out_refs[a].at[4 * px + 2 * py + pc]

        def copy(a, k, blk, to, src=None):
            return pltpu.make_async_remote_copy(
                src_ref=slot(a, *blk) if src is None else src, dst_ref=slot(a, *blk),
                send_sem=send_sems.at[7 * a + k], recv_sem=recv_sems.at[7 * a + k], device_id=to, device_id_type=MESH)

        mine = [pltpu.make_async_copy(x_refs[a], slot(a, *me), local_sems.at[a]) for a in range(n)]
        first = []
        for a in range(n):
            mine[a].start()
            first.append(copy(a, 0, me, sibling, src=x_refs[a]))
            first += [copy(a, 1 + j, me, (*chip, c), src=x_refs[a]) for j, chip in enumerate(chips)]
        for cp in first:
            cp.start()
        passed = []
        for j, chip in enumerate(chips):
            for a in range(n):
                copy(a, 1 + j, (*chip, c), me).wait_recv()
                passed.append(copy(a, 4 + j, (*chip, c), sibling))
                passed[-1].start()
        for a in range(n):
            copy(a, 0, sibling, me).wait_recv()
            for j, chip in enumerate(chips):
                copy(a, 4 + j, (*chip, 1 - c), me).wait_recv()
        for cp in first + passed:
            cp.wait_send()
        for cp in mine:
            cp.wait()

    return _comm_call(
        body, name, blocks, [jax.ShapeDtypeStruct((N_DEV,) + b.shape, b.dtype) for b in blocks],
        [pltpu.SemaphoreType.DMA((7 * n,)), pltpu.SemaphoreType.DMA((7 * n,)), pltpu.SemaphoreType.DMA((n,))],
        sequencer_id, after)


def _exchange_sibling(gs, name, sequencer_id=None, after=()):
    n = len(gs)

    def body(*refs):
        g_refs, out_refs = refs[:n], refs[n:2 * n]
        send_sems, recv_sems = refs[2 * n:]
        x, y, c = _place()
        if sequencer_id is not None:
            _handshake([(x, y, 1 - c)])
        copies = [pltpu.make_async_remote_copy(
            src_ref=g_refs[a].at[2 * k + 1 - c], dst_ref=out_refs[a].at[k], send_sem=send_sems.at[4 * a + k],
            recv_sem=recv_sems.at[4 * a + k], device_id=(x, y, 1 - c), device_id_type=MESH)
            for a in range(n) for k in range(4)]
        for cp in copies:
            cp.start()
        for cp in copies:
            cp.wait()

    return _comm_call(body, name, gs, [jax.ShapeDtypeStruct((4,) + g.shape[1:], g.dtype) for g in gs],
                      [pltpu.SemaphoreType.DMA((4 * n,)), pltpu.SemaphoreType.DMA((4 * n,))], sequencer_id, after)


def _other_chips():
    x, y = lax.axis_index("x"), lax.axis_index("y")
    return jnp.stack([2 * (1 - x) + y, 2 * x + 1 - y, 2 * (1 - x) + 1 - y]).astype(jnp.int32)


def _add_sibling(gs, r1s, name, after=()):
    n = len(gs)
    _, r, w = gs[0].shape
    tr, tc = _elementwise_block(r, w)
    chips = _other_chips()
    slabs = 2 * chips + lax.axis_index("c").astype(jnp.int32)

    def body(slab_ref, chip_ref, *refs):
        g_refs, r_refs, o_refs = refs[:n], refs[n:2 * n], refs[2 * n + len(after):]
        for g_ref, r_ref, o_ref in zip(g_refs, r_refs, o_refs):
            o_ref[...] = (g_ref[...] + r_ref[...]).astype(BF16)

    return pl.pallas_call(
        body, name=name, out_shape=[jax.ShapeDtypeStruct((3, r, w), BF16)] * n,
        grid_spec=pltpu.PrefetchScalarGridSpec(
            num_scalar_prefetch=2, grid=(3, r // tr, w // tc),
            in_specs=[pl.BlockSpec((1, tr, tc), lambda k, i, j, slab_ref, chip_ref: (slab_ref[k], i, j))] * n
            + [pl.BlockSpec((1, tr, tc), lambda k, i, j, slab_ref, chip_ref: (chip_ref[k], i, j))] * n
            + [ANY] * len(after),
            out_specs=[pl.BlockSpec((1, tr, tc), lambda k, i, j, slab_ref, chip_ref: (k, i, j))] * n),
        compiler_params=_cparams("parallel", "parallel", "parallel"),
    )(slabs, chips, *gs, *r1s, *after)


def _exchange_chips(ps, name, sequencer_id=None, after=()):
    n = len(ps)

    def body(*refs):
        p_refs, out_refs = refs[:n], refs[n:2 * n]
        send_sems, recv_sems = refs[2 * n:]
        x, y, c = _place()
        chips = [(1 - x, y), (x, 1 - y), (1 - x, 1 - y)]
        if sequencer_id is not None:
            _handshake([(*chip, c) for chip in chips])
        copies = [pltpu.make_async_remote_copy(
            src_ref=p_refs[a].at[j], dst_ref=out_refs[a].at[j], send_sem=send_sems.at[3 * a + j],
            recv_sem=recv_sems.at[3 * a + j], device_id=(px, py, c), device_id_type=MESH)
            for a in range(n) for j, (px, py) in enumerate(chips)]
        for cp in copies:
            cp.start()
        for cp in copies:
            cp.wait()

    return _comm_call(body, name, ps, [jax.ShapeDtypeStruct(p.shape, p.dtype) for p in ps],
                      [pltpu.SemaphoreType.DMA((3 * n,)), pltpu.SemaphoreType.DMA((3 * n,))], sequencer_id, after)


def _sum_chips(gs, r1s, others, name):
    n = len(gs)
    _, r, c = gs[0].shape
    tr, tc = _elementwise_block(r, c)
    chip = 2 * lax.axis_index("x") + lax.axis_index("y")
    place = jnp.stack([2 * chip + lax.axis_index("c"), chip]).astype(jnp.int32)

    def body(place_ref, *refs):
        for g_ref, r_ref, others_ref, o_ref in zip(refs[:n], refs[n:2 * n], refs[2 * n:3 * n], refs[3 * n:]):
            s = g_ref[0] + r_ref[0]
            for j in range(3):
                s = s + others_ref[j].astype(F32)
            o_ref[...] = s

    return pl.pallas_call(
        body, name=name, out_shape=[jax.ShapeDtypeStruct((r, c), F32)] * n,
        grid_spec=pltpu.PrefetchScalarGridSpec(
            num_scalar_prefetch=1, grid=(r // tr, c // tc),
            in_specs=[pl.BlockSpec((1, tr, tc), lambda i, j, place_ref: (place_ref[0], i, j))] * n
            + [pl.BlockSpec((1, tr, tc), lambda i, j, place_ref: (place_ref[1], i, j))] * n
            + [pl.BlockSpec((3, tr, tc), lambda i, j, place_ref: (0, i, j))] * n,
            out_specs=[pl.BlockSpec((tr, tc), lambda i, j, place_ref: (i, j))] * n),
        compiler_params=_cparams("parallel", "parallel"),
    )(place, *gs, *r1s, *others)


def _reorder_in_proj(wt):
    za_zb, zc, xbc, dt, gates = (wt[:4096], wt[4096:5120], wt[5120:7168], wt[7168:7184], wt[7184:])
    return jnp.concatenate([za_zb, xbc, gates, zc, dt, jnp.zeros((DT_PAD - 16, wt.shape[1]), wt.dtype)], axis=0)


def _restore_in_proj(wt):
    return jnp.concatenate([wt[:4096], wt[OFF_ZC:OFF_ZC + W_ZC], wt[OFF_XBC:OFF_XBC + W_XBC],
                            wt[OFF_DT:OFF_DT + 16], wt[OFF_GATE:OFF_GATE + W_GATE]], axis=0)


def _lanes_from_devices(g):
    return jnp.moveaxis(g, 0, 1).reshape(g.shape[1], N_DEV * g.shape[2])


def _lanes_to_devices(a):
    return jnp.moveaxis(a.reshape(a.shape[0], N_DEV, a.shape[1] // N_DEV), 1, 0)


def _pad_lanes(a, width):
    return jnp.pad(a, ((0, 0), (0, width - a.shape[1])))


BIG = ("w_in", "w_branch_a", "w_branch_b", "w_branch_c", "w_out", "w_mlp_up", "w_mlp_down")
SMALL_SHARDED = ("b_gate", "lru_conv_w", "ssd_conv_w")
REPLICATED = ("norm_mix_g", "gmlp_ln_g", "gmlp_ln_b", "gmlp_w_s", "gmlp_b_s", "lru_conv_b", "lru_w_r", "lru_b_r",
              "lru_w_i", "lru_b_i", "lru_lambda", "ssd_conv_b", "ssd_dt_bias", "ssd_a_log", "ssd_d", "ssd_norm_g",
              "norm_mlp_g", "final_norm_g")
WEIGHTS = ("norm_mix_g", "w_in", "b_gate", "gmlp_ln_g", "gmlp_ln_b", "gmlp_w_s", "gmlp_b_s", "lru_conv_w", "lru_conv_b",
           "lru_w_r", "lru_b_r", "lru_w_i", "lru_b_i", "lru_lambda", "ssd_conv_w", "ssd_conv_b", "ssd_dt_bias",
           "ssd_a_log", "ssd_d", "ssd_norm_g", "w_branch_a", "w_branch_b", "w_branch_c", "w_out", "norm_mlp_g",
           "w_mlp_up", "w_mlp_down", "final_norm_g")
TRANSPOSED = ("w_in", "w_mlp_up")
SMALL_MATRICES = ("gmlp_w_s", "lru_w_r", "lru_w_i")
SMALL_VECTORS = tuple(n for n in REPLICATED if n not in SMALL_MATRICES)
GRADIENT_GROUPS = {"mlp": ("w_mlp_up", "w_mlp_down"), "mix": ("w_branch_a", "w_branch_b", "w_branch_c", "w_out"),
                   "in": ("w_in",)}


def _layer_params(full, l):
    row = lambda a: a.reshape(1, -1)
    return dict(
        norm_mix_g=row(full["norm_mix_g"][l]), norm_mlp_g=row(full["norm_mlp_g"][l]),
        gmlp=(row(full["gmlp_ln_g"][l]), row(full["gmlp_ln_b"][l]), full["gmlp_w_s"][l].reshape(GROUPS_A * CHUNK, CHUNK),
              full["gmlp_b_s"][l].T),
        lru=(full["lru_conv_w"][l], row(full["lru_conv_b"][l]), full["lru_w_r"][l].reshape(D, D // HEADS_B),
             row(full["lru_b_r"][l]), full["lru_w_i"][l].reshape(D, D // HEADS_B), row(full["lru_b_i"][l]),
             row(full["lru_lambda"][l])),
        ssd=(full["ssd_conv_w"][l], row(full["ssd_conv_b"][l]), _pad_lanes(row(full["ssd_dt_bias"][l]), DT_PAD),
             _pad_lanes(row(full["ssd_a_log"][l]), DT_PAD), _pad_lanes(row(full["ssd_d"][l]), DT_PAD),
             row(full["ssd_norm_g"][l])),
        b_gate=full["b_gate"][l],
    )


def _forward_layer(h, hn, p, wb, l, after_mixers=None, next_norm_g=None):
    tag = f"l{l}"
    t_row = 512
    if hn is None:
        (hn,), _, _ = _chunk_fwd(_f_rmsnorm, f"norm_mix_{tag}", t_row, [(h, 0, D)], [p["norm_mix_g"]], [(D, BF16)])
    proj = _matmul(hn, wb["w_in"], f"in_proj_{tag}", tb=True)
    (ya,), _, _ = _chunk_fwd(_f_gmlp, f"gmlp_{tag}", CHUNK, [(proj, OFF_ZA, W_ZA)], p["gmlp"], [(D, BF16)])
    lru_xs = [(proj, OFF_ZB, D), (proj, OFF_ZB + D, D)]
    (yb,), lru_saved, _ = _chunk_fwd(_f_lru, f"lru_{tag}", CHUNK, lru_xs, p["lru"], [(D, BF16)], halo_idx=(0,),
                                     carry_shapes=[(1, D)], save_carries=True)
    ssd_xs = [(proj, OFF_ZC, W_ZC), (proj, OFF_XBC, W_XBC), (proj, OFF_DT, W_DT)]
    (yc,), ssd_saved, _ = _chunk_fwd(_f_ssd, f"ssd_{tag}", CHUNK, ssd_xs, p["ssd"], [(D, BF16)], halo_idx=(1,),
                                     carry_shapes=[(HEADS_C * HEAD_DIM_C, STATE_C)], save_carries=True)
    if after_mixers is not None:
        after_mixers(yc)
    pa = _matmul(ya, wb["w_branch_a"], f"branch_a_{tag}")
    pb = _matmul(yb, wb["w_branch_b"], f"branch_b_{tag}")
    pc = _matmul(yc, wb["w_branch_c"], f"branch_c_{tag}")
    merge_xs = [(pa, 0, D), (pb, 0, D), (pc, 0, D), (proj, OFF_GATE, W_GATE)]
    (merged,), _, _ = _chunk_fwd(_f_merge, f"merge_{tag}", t_row, merge_xs, [p["b_gate"]], [(D, BF16)])

    def add_and_norm(acc, res, g):
        s = acc + res
        return s, _rms(s, g)

    h_mid, hn2 = _matmul(merged, wb["w_out"], f"out_proj_{tag}", outs=(F32, BF16), epilogue=add_and_norm, extras=(h,),
                         row_params=(p["norm_mlp_g"],))

    def relu_sq(acc):
        r = jnp.maximum(acc, 0.0)
        return r, r * r

    relu_up, act = _matmul(hn2, wb["w_mlp_up"], f"mlp_up_{tag}", tb=True, outs=(F32, BF16), epilogue=relu_sq)
    if next_norm_g is None:
        hn_out = None
        h_out = _matmul(act, wb["w_mlp_down"], f"mlp_down_{tag}", epilogue=lambda acc, res: (acc + res,), extras=(h_mid,))
    else:
        h_out, hn_out = _matmul(act, wb["w_mlp_down"], f"mlp_down_{tag}", outs=(F32, BF16), epilogue=add_and_norm,
                                extras=(h_mid,), row_params=(next_norm_g,))
    saved = dict(h=h, hn=hn, proj=proj, ya=ya, yb=yb, yc=yc, lru_saved=lru_saved, ssd_saved=ssd_saved, pa=pa, pb=pb,
                 pc=pc, merged=merged, h_mid=h_mid, hn2=hn2, relu_up=relu_up, act=act, lru_xs=lru_xs, ssd_xs=ssd_xs,
                 merge_xs=merge_xs)
    return h_out, hn_out, saved


def _backward_layer(dh, sv, p, wb, l):
    tag = f"l{l}"
    t_row = 512
    g = {}
    d_up = _matmul(dh, wb["w_mlp_down"], f"d_act_{tag}", tb=True, outs=(BF16,),
                   epilogue=lambda acc, r: (acc * (2.0 * r),), extras=(sv["relu_up"],))
    g["w_mlp_down"] = _matmul(sv["act"], dh, f"dw_mlp_down_{tag}", ta=True)
    g["w_mlp_up"] = _matmul(d_up, sv["hn2"], f"dw_mlp_up_{tag}", ta=True)
    d_hn2 = _matmul(d_up, wb["w_mlp_up"], f"d_hn2_{tag}")
    (d_mid,), (g["norm_mlp_g"],) = _chunk_bwd(_f_rmsnorm_res, f"norm_mlp_bwd_{tag}", t_row, [(sv["h_mid"], 0, D)],
                                              [p["norm_mlp_g"]], [d_hn2, dh], [F32])
    d_merged = _matmul(d_mid, wb["w_out"], f"d_merged_{tag}", tb=True)
    g["w_out"] = _matmul(sv["merged"], d_mid, f"dw_out_{tag}", ta=True)
    (d_pa, d_pb, d_pc, d_gate), (g["b_gate"],) = _chunk_bwd(
        _f_merge, f"merge_bwd_{tag}", t_row, sv["merge_xs"], [p["b_gate"]], [d_merged], [BF16] * 4)
    d_y = {}
    for br, d_p, y in (("a", d_pa, sv["ya"]), ("b", d_pb, sv["yb"]), ("c", d_pc, sv["yc"])):
        g[f"w_branch_{br}"] = _matmul(y, d_p, f"dw_branch_{br}_{tag}", ta=True)
        d_y[br] = _matmul(d_p, wb[f"w_branch_{br}"], f"d_y{br}_{tag}", tb=True)
    (d_za,), g_gmlp = _chunk_bwd(_f_gmlp, f"gmlp_bwd_{tag}", CHUNK, [(sv["proj"], OFF_ZA, W_ZA)], p["gmlp"],
                                 [d_y["a"]], [BF16])
    (d_xb, d_gt), g_lru = _chunk_bwd(_f_lru, f"lru_bwd_{tag}", CHUNK, sv["lru_xs"], p["lru"], [d_y["b"]], [BF16] * 2,
                                     halo_idx=(0,), saved=sv["lru_saved"])
    (d_zc, d_xbc, d_dt), g_ssd = _chunk_bwd(_f_ssd, f"ssd_bwd_{tag}", CHUNK, sv["ssd_xs"], p["ssd"], [d_y["c"]],
                                            [BF16] * 3, halo_idx=(1,), saved=sv["ssd_saved"])
    d_proj = [d_za, d_xb, d_gt, d_xbc, d_gate, d_zc, d_dt]
    g["w_in"] = _matmul_pieces_t(d_proj, sv["hn"], f"dw_in_{tag}")
    d_hn = _matmul_pieces(d_proj, wb["w_in"], f"d_hn_{tag}")
    (d_h,), (g["norm_mix_g"],) = _chunk_bwd(_f_rmsnorm_res, f"norm_mix_bwd_{tag}", t_row, [(sv["h"], 0, D)],
                                            [p["norm_mix_g"]], [d_hn, d_mid], [F32])
    g["w_in"] = _restore_in_proj(g["w_in"])
    for n in BIG:
        g[n] = g[n].reshape(N_DEV, g[n].shape[0] // N_DEV, g[n].shape[1])
    g["gmlp_ln_g"], g["gmlp_ln_b"], g["gmlp_w_s"] = g_gmlp[:3]
    g["gmlp_b_s"] = g_gmlp[3].T
    (g["lru_conv_w"], g["lru_conv_b"], g["lru_w_r"], g["lru_b_r"], g["lru_w_i"], g["lru_b_i"], g["lru_lambda"]) = g_lru
    g["ssd_conv_w"], g["ssd_conv_b"] = g_ssd[:2]
    g["ssd_dt_bias"], g["ssd_a_log"], g["ssd_d"] = (a[:, :HEADS_C] for a in g_ssd[2:5])
    g["ssd_norm_g"] = g_ssd[5]
    g["later"] = {"mlp": d_merged, "mix": d_za, "in": d_h}
    return d_h, g


LOSS_ROWS = 512


def _loss_and_grads(h, target, full, layer_weights, first_gathered):
    seq = h.shape[0]
    layer_p = [_layer_params(full, l) for l in range(DEPTH)]
    layer_w = [layer_weights(0, (first_gathered,))]
    saved = []
    hn = None
    for l in range(DEPTH):
        fetch_next = next_norm_g = None
        if l + 1 < DEPTH:
            fetch_next = lambda y, l=l: layer_w.append(layer_weights(l + 1, (layer_w[l]["w_mlp_down"], y)))
            next_norm_g = layer_p[l + 1]["norm_mix_g"]
        h, hn, sv = _forward_layer(h, hn, layer_p[l], layer_w[l], l, fetch_next, next_norm_g)
        saved.append(sv)
    final_g = full["final_norm_g"].reshape(1, D)
    loss_xs = [(h, 0, D), (target, 0, D)]
    t_loss = min(LOSS_ROWS, seq)
    zero_acc = jnp.zeros((seq // t_loss, 1, 128), F32)
    seed = lambda shape: (lax.broadcasted_iota(jnp.int32, shape, 1) == 0).astype(F32)
    (dh, _), (g_final,), (loss_acc,) = _chunk_bwd(_f_loss, "loss_and_bwd", t_loss, loss_xs, [final_g], [], [F32, F32],
                                                  saved=[zero_acc], carry_seed=seed, carry_growth=True)
    layer_g = [None] * DEPTH
    for l in reversed(range(DEPTH)):
        dh, layer_g[l] = _backward_layer(dh, saved[l], layer_p[l], layer_w[l], l)
    return loss_acc, dh, layer_g, g_final


def _small_views(d):
    views = {n: d[n] for n in REPLICATED}
    views["gmlp_b_s"] = d["gmlp_b_s"].reshape(DEPTH * GROUPS_A, CHUNK)
    views["final_norm_g"] = d["final_norm_g"].reshape(1, D)
    for n in SMALL_MATRICES:
        views[n] = d[n].reshape(DEPTH * D, D // HEADS_B)
    return views


def kernel(x, norm_mix_g, w_in, b_gate, gmlp_ln_g, gmlp_ln_b, gmlp_w_s, gmlp_b_s, lru_conv_w, lru_conv_b, lru_w_r, lru_b_r, lru_w_i, lru_b_i, lru_lambda, ssd_conv_w, ssd_conv_b, ssd_dt_bias, ssd_a_log, ssd_d, ssd_norm_g, w_branch_a, w_branch_b, w_branch_c, w_out, norm_mlp_g, w_mlp_up, w_mlp_down, final_norm_g, loss_target, m_norm_mix_g, m_w_in, m_b_gate, m_gmlp_ln_g, m_gmlp_ln_b, m_gmlp_w_s, m_gmlp_b_s, m_lru_conv_w, m_lru_conv_b, m_lru_w_r, m_lru_b_r, m_lru_w_i, m_lru_b_i, m_lru_lambda, m_ssd_conv_w, m_ssd_conv_b, m_ssd_dt_bias, m_ssd_a_log, m_ssd_d, m_ssd_norm_g, m_w_branch_a, m_w_branch_b, m_w_branch_c, m_w_out, m_norm_mlp_g, m_w_mlp_up, m_w_mlp_down, m_final_norm_g, v_norm_mix_g, v_w_in, v_b_gate, v_gmlp_ln_g, v_gmlp_ln_b, v_gmlp_w_s, v_gmlp_b_s, v_lru_conv_w, v_lru_conv_b, v_lru_w_r, v_lru_b_r, v_lru_w_i, v_lru_b_i, v_lru_lambda, v_ssd_conv_w, v_ssd_conv_b, v_ssd_dt_bias, v_ssd_a_log, v_ssd_d, v_ssd_norm_g, v_w_branch_a, v_w_branch_b, v_w_branch_c, v_w_out, v_norm_mlp_g, v_w_mlp_up, v_w_mlp_down, v_final_norm_g):
    args = locals()
    w = {n: args[n] for n in WEIGHTS}
    m = {n: args["m_" + n] for n in WEIGHTS}
    v = {n: args["v_" + n] for n in WEIGHTS}
    seq = x.shape[1]
    h = x.reshape(seq, D)
    target = loss_target.reshape(seq, D)

    def shard_on_wire(n, l):
        return (w[n][l].T if n in TRANSPOSED else w[n][l]).astype(BF16)

    first = _all_gather([shard_on_wire("w_in", 0)] + [w[n] for n in SMALL_SHARDED], "gather_weights_first")
    full = {n: w[n] for n in REPLICATED}
    for n, g in zip(SMALL_SHARDED, first[1:]):
        full[n] = jnp.stack([_lanes_from_devices(g[:, l]) for l in range(DEPTH)])

    def layer_weights(l, after):
        have = {"w_in": first[0]} if l == 0 else {}
        names = [n for n in BIG if n not in have]
        later = _all_gather([shard_on_wire(n, l) for n in names], f"gather_weights_l{l}", SEQ_GATHER + l, after=after)
        have.update(zip(names, later))
        wl = {n: have[n].reshape(-1, D) for n in BIG}
        wl["w_in"] = _reorder_in_proj(wl["w_in"])
        return wl

    loss_local, dh, layer_g, g_final = _loss_and_grads(h, target, full, layer_weights, first[0])
    grad_x = dh.reshape(x.shape)
    out = {}
    kinds = ("grad", "delta", "new_m", "new_v")

    sequencer_before = {}

    def reduce_scatter(slabs, tag, on_sequencer, later=()):
        keys = list(slabs)
        ids = (SEQ_TO_SIBLING, SEQ_TO_CHIPS) if on_sequencer else (None, None)
        from_sibling = _exchange_sibling([slabs[k] for k in keys], f"grads_to_sibling_{tag}", ids[0],
                                         sequencer_before.get("sibling", ()))
        from_sibling = dict(zip(keys, from_sibling))
        same_shape = {}
        for k in keys:
            same_shape.setdefault(slabs[k].shape[1:], []).append(k)
        chip_sums = {}
        for (r, c), ks in same_shape.items():
            sums = _add_sibling([slabs[k] for k in ks], [from_sibling[k] for k in ks], f"add_sibling_{tag}_{r}x{c}", later)
            chip_sums.update(zip(ks, sums))
        from_chips = _exchange_chips([chip_sums[k] for k in keys], f"grads_to_chips_{tag}", ids[1],
                                     sequencer_before.get("chips", ()))
        if on_sequencer:
            sequencer_before["sibling"], sequencer_before["chips"] = (from_sibling[keys[0]],), (from_chips[0],)
        from_chips = dict(zip(keys, from_chips))
        summed = {}
        for (r, c), ks in same_shape.items():
            sums = _sum_chips([slabs[k] for k in ks], [from_sibling[k] for k in ks], [from_chips[k] for k in ks],
                              f"sum_chips_{tag}_{r}x{c}")
            summed.update(zip(ks, sums))
        return summed

    small = {}
    for n in SMALL_SHARDED:
        small[n, None] = jnp.concatenate([_lanes_to_devices(layer_g[l][n]) for l in range(DEPTH)], axis=1)
    g_small = {n: jnp.concatenate([layer_g[l][n] for l in range(DEPTH)], axis=0) for n in REPLICATED[:-1]}
    g_small["final_norm_g"] = g_final
    for n in SMALL_MATRICES:
        small[n, None] = g_small[n].reshape(N_DEV, -1, g_small[n].shape[-1])
    reduced = {}
    groups = [(l, grp) for l in range(DEPTH - 1, -1, -1) for grp in GRADIENT_GROUPS]
    for l, grp in groups:
        if (l, grp) == groups[-1]:
            reduced.update(reduce_scatter(small, "small", True, (layer_g[l][GRADIENT_GROUPS[grp][0]],)))
        slabs = {(n, l): layer_g[l][n] for n in GRADIENT_GROUPS[grp]}
        reduced.update(reduce_scatter(slabs, f"{grp}_l{l}", True, (layer_g[l]["later"][grp],)))
    same_shape = {}
    for n in BIG:
        grads = [reduced[n, l] for l in range(DEPTH)]
        if n in TRANSPOSED and w[n].shape[-1] % LANES:
            for kind, a in zip(kinds, _adamw_transposed(grads, w[n], m[n], v[n], f"adamw_{n}")):
                out[kind, n] = a
        else:
            same_shape.setdefault(w[n].shape, []).append((n, [(g.T if n in TRANSPOSED else g)[None] for g in grads]))
    for shape, group in same_shape.items():
        names = [n for n, _ in group]
        res = _adamw([p for _, p in group], [w[n] for n in names], [m[n] for n in names], [v[n] for n in names],
                     f"adamw_{names[0]}")
        for n, per_weight in zip(names, res):
            for kind, a in zip(kinds, per_weight):
                out[kind, n] = a
    for n in SMALL_SHARDED:
        one = lambda a: a.reshape((1, -1, a.shape[-1]))
        (res,) = _adamw([[reduced[n, None][None]]], [one(w[n])], [one(m[n])], [one(v[n])], f"adamw_{n}")
        for kind, a in zip(kinds, res):
            out[kind, n] = a.reshape(w[n].shape)

    to_gather = [reduced[n, None] if n in SMALL_MATRICES else g_small[n] for n in REPLICATED]
    *g_gathered, loss_terms = _all_gather(to_gather + [loss_local], "gather_small_grads", SEQ_GATHER_SMALL)
    g_all = dict(zip(REPLICATED, g_gathered))
    wv, mv, vv = _small_views(w), _small_views(m), _small_views(v)
    res, loss_sum = _adamw_small([g_all[n] for n in SMALL_VECTORS],
                                 *[[d[n] for n in SMALL_VECTORS] for d in (wv, mv, vv)], loss_terms, "adamw_vectors")
    loss = loss_sum[0, 0]
    for kind, arrays in zip(kinds, res):
        for n, a in zip(SMALL_VECTORS, arrays):
            out[kind, n] = a.reshape(w[n].shape)
    g_full = [[g_all[n].reshape((1,) + wv[n].shape)] for n in SMALL_MATRICES]
    res = _adamw(g_full, *[[d[n][None] for n in SMALL_MATRICES] for d in (wv, mv, vv)], "adamw_small_matrices")
    for n, per_weight in zip(SMALL_MATRICES, res):
        for kind, a in zip(kinds, per_weight):
            out[kind, n] = a.reshape(w[n].shape)

    return (loss, grad_x, *[out[kind, n] for kind in kinds for n in WEIGHTS])
```
